```python
import math
import jax
import jax.numpy as jnp
from jax import lax
import numpy as np

D_MODEL = 1024
BATCH = 1
SEQ = 16384
DEPTH = 2

N_MIXERS = 2
N_RET_LAYERS = (DEPTH + 1) // 2
N_CONV_LAYERS = DEPTH // 2
RET_HEADS = 4
RET_DK = 256
RET_DV = 512
RET_QK = RET_HEADS * RET_DK
RET_V = RET_HEADS * RET_DV
RET_IN = 2 * RET_QK + 2 * RET_V
RET_CHUNK = 128
ROPE_BASE = 10000.0
CONV_WIDTH = 31
MOE_GROUPS = 4
MOE_EXPERTS_PER_GROUP = 8
MOE_EXPERTS = MOE_GROUPS * MOE_EXPERTS_PER_GROUP
MOE_TOPK = 2
MOE_FF = 512
MOE_BLOCK = 128
NORM_EPS = 1e-6

kernel_name = "hybrid_retention_conformer_hmoe"


def rmsnorm(x, g):
    xf = x.astype(jnp.float32)
    y = xf * lax.rsqrt(jnp.mean(xf * xf, axis=-1, keepdims=True) + NORM_EPS)
    return (y * g.astype(jnp.float32)).astype(x.dtype)


def rotary(t, positions):
    half = t.shape[-1] // 2
    inv = ROPE_BASE ** (-jnp.arange(half, dtype=jnp.float32) / half)
    ang = positions.astype(jnp.float32)[:, :, None] * inv
    cos = jnp.cos(ang)[:, :, None, :]
    sin = jnp.sin(ang)[:, :, None, :]
    t1 = t[..., :half].astype(jnp.float32)
    t2 = t[..., half:].astype(jnp.float32)
    return jnp.concatenate([t1 * cos - t2 * sin, t1 * sin + t2 * cos], axis=-1).astype(t.dtype)


def retention_mixer(h, positions, w_in, head_g, w_out):
    B, S, _ = h.shape
    dt = h.dtype
    proj = h @ w_in
    q, k, v, g = jnp.split(proj, [RET_QK, 2 * RET_QK, 2 * RET_QK + RET_V], axis=-1)
    q = rotary(q.reshape(B, S, RET_HEADS, RET_DK), positions)
    k = rotary(k.reshape(B, S, RET_HEADS, RET_DK), positions) * (RET_DK ** -0.5)
    v = v.reshape(B, S, RET_HEADS, RET_DV)

    n_chunks = S // RET_CHUNK

    def to_chunks(t):
        return t.reshape(B, n_chunks, RET_CHUNK, RET_HEADS, t.shape[-1]).transpose(1, 0, 3, 2, 4)

    log_gamma = jnp.log1p(-(2.0 ** (-5.0 - jnp.arange(RET_HEADS, dtype=jnp.float32))))
    idx = jnp.arange(RET_CHUNK, dtype=jnp.float32)
    diff = idx[:, None] - idx[None, :]
    intra = jnp.where(diff >= 0, jnp.exp(log_gamma[:, None, None] * jnp.maximum(diff, 0.0)), 0.0).astype(dt)
    cross_dec = jnp.exp(log_gamma[:, None] * (idx + 1.0)).astype(dt)
    k_dec = jnp.exp(log_gamma[:, None] * (RET_CHUNK - 1.0 - idx)).astype(dt)
    chunk_dec = jnp.exp(log_gamma * RET_CHUNK).astype(dt)

    def step(state, qkv):
        qc, kc, vc = qkv
        scores = jnp.einsum('bhid,bhjd->bhij', qc, kc) * intra[None]
        o_inner = jnp.einsum('bhij,bhjv->bhiv', scores, vc)
        o_cross = jnp.einsum('bhid,bhdv->bhiv', qc, state) * cross_dec[None, :, :, None]
        new_state = state * chunk_dec[None, :, None, None] + jnp.einsum(
            'bhjd,bhjv->bhdv', kc * k_dec[None, :, :, None], vc)
        return new_state, o_inner + o_cross

    state0 = jnp.zeros((B, RET_HEADS, RET_DK, RET_DV), dt)
    _, o = lax.scan(step, state0, (to_chunks(q), to_chunks(k), to_chunks(v)))
    o = o.transpose(1, 0, 3, 2, 4).reshape(B, S, RET_HEADS, RET_DV)
    of = o.astype(jnp.float32)
    of = of * lax.rsqrt(jnp.mean(of * of, axis=-1, keepdims=True) + NORM_EPS)
    of = of * head_g.astype(jnp.float32).reshape(RET_HEADS, RET_DV)
    o = of.reshape(B, S, RET_V).astype(dt)
    return (jax.nn.silu(g) * o) @ w_out


def conformer_conv_mixer(h, w_pw1, b_pw1, w_dw, b_dw, ln_g, ln_b, w_pw2, b_pw2):
    D = h.shape[-1]
    u = h @ w_pw1 + b_pw1
    a, gate = jnp.split(u, 2, axis=-1)
    u = a * jax.nn.sigmoid(gate)
    z = lax.conv_general_dilated(
        u, w_dw[:, None, :].astype(u.dtype), window_strides=(1,),
        padding=[(CONV_WIDTH - 1, 0)], dimension_numbers=('NWC', 'WIO', 'NWC'),
        feature_group_count=D) + b_dw
    zf = z.astype(jnp.float32)
    mu = jnp.mean(zf, axis=-1, keepdims=True)
    var = jnp.mean(jnp.square(zf - mu), axis=-1, keepdims=True)
    zf = (zf - mu) * lax.rsqrt(var + NORM_EPS) * ln_g.astype(jnp.float32) + ln_b.astype(jnp.float32)
    z = jax.nn.silu(zf).astype(h.dtype)
    return z @ w_pw2 + b_pw2


def hierarchical_moe(h, w_rg, b_rg, w_re, b_re, w_gate, w_up, w_down):
    B, S, D = h.shape
    N = B * S
    xt = h.reshape(N, D)
    lg = (xt @ w_rg + b_rg).astype(jnp.float32)
    pg = jax.nn.softmax(lg, axis=-1)
    g_idx = jnp.argmax(lg, axis=-1).astype(jnp.int32)
    gate_g = jnp.take_along_axis(pg, g_idx[:, None], axis=-1)
    le = (xt @ w_re + b_re).astype(jnp.float32).reshape(N, MOE_GROUPS, MOE_EXPERTS_PER_GROUP)
    le_sel = jnp.take_along_axis(le, g_idx[:, None, None], axis=1)[:, 0]
    top_v, top_i = lax.top_k(le_sel, MOE_TOPK)
    pe = jax.nn.softmax(top_v, axis=-1)
    wts = (gate_g * pe).reshape(-1)
    eid = (g_idx[:, None] * MOE_EXPERTS_PER_GROUP + top_i.astype(jnp.int32)).reshape(-1)
    A = N * MOE_TOPK
    tok = jnp.arange(A, dtype=jnp.int32) // MOE_TOPK
    order = jnp.argsort(eid, stable=True)
    e_s, tok_s, w_s = eid[order], tok[order], wts[order]
    counts = jnp.bincount(eid, length=MOE_EXPERTS).astype(jnp.int32)
    pcounts = (counts + MOE_BLOCK - 1) // MOE_BLOCK * MOE_BLOCK
    start = jnp.cumsum(counts) - counts
    pend = jnp.cumsum(pcounts)
    pstart = pend - pcounts
    dest = pstart[e_s] + (jnp.arange(A, dtype=jnp.int32) - start[e_s])
    P = -(-A // MOE_BLOCK) * MOE_BLOCK + MOE_EXPERTS * MOE_BLOCK
    n_blocks = P // MOE_BLOCK
    buf_tok = jnp.full((P,), N, jnp.int32).at[dest].set(tok_s)
    buf_w = jnp.zeros((P,), jnp.float32).at[dest].set(w_s)
    blk_e = jnp.clip(jnp.searchsorted(pend, jnp.arange(n_blocks, dtype=jnp.int32) * MOE_BLOCK,
                                      side='right'), 0, MOE_EXPERTS - 1).astype(jnp.int32)
    x_pad = jnp.concatenate([xt, jnp.zeros((1, D), xt.dtype)], axis=0)
    xb = x_pad[buf_tok].reshape(n_blocks, MOE_BLOCK, D)

    def run_block(args):
        xblk, e = args
        return (jax.nn.silu(xblk @ w_gate[e]) * (xblk @ w_up[e])) @ w_down[e]

    yb = lax.map(run_block, (xb, blk_e)).reshape(P, D)
    out = jax.ops.segment_sum(yb * buf_w[:, None].astype(yb.dtype), buf_tok, num_segments=N + 1)[:N]
    return out.reshape(B, S, D).astype(h.dtype)


def setup_inputs(seed: int = 0) -> dict:
    key = jax.random.key(seed)
    ks = iter(jax.random.split(key, 32))
    D = D_MODEL
    f32 = jnp.float32

    def nrm(shape, scale):
        return jax.random.normal(next(ks), shape, f32) * scale

    x = jax.random.normal(next(ks), (BATCH, SEQ, D), f32)
    offs = jax.random.randint(next(ks), (BATCH, 1), 0, 1024, dtype=jnp.int32)
    positions = (jnp.arange(SEQ, dtype=jnp.int32)[None, :] + offs).astype(jnp.int32)
    return {
        "x": x,
        "positions": positions,
        "norm_mix_g": 1.0 + nrm((DEPTH, D), 0.01),
        "norm_ffn_g": 1.0 + nrm((DEPTH, D), 0.01),
        "ret_w_in": nrm((N_RET_LAYERS, D, RET_IN), D ** -0.5),
        "ret_head_g": 1.0 + nrm((N_RET_LAYERS, RET_V), 0.01),
        "ret_w_out": nrm((N_RET_LAYERS, RET_V, D), RET_V ** -0.5),
        "conv_w_pw1": nrm((N_CONV_LAYERS, D, 2 * D), D ** -0.5),
        "conv_b_pw1": nrm((N_CONV_LAYERS, 2 * D), 0.01),
        "conv_w_dw": nrm((N_CONV_LAYERS, CONV_WIDTH, D), CONV_WIDTH ** -0.5),
        "conv_b_dw": nrm((N_CONV_LAYERS, D), 0.01),
        "conv_ln_g": 1.0 + nrm((N_CONV_LAYERS, D), 0.01),
        "conv_ln_b": nrm((N_CONV_LAYERS, D), 0.01),
        "conv_w_pw2": nrm((N_CONV_LAYERS, D, D), D ** -0.5),
        "conv_b_pw2": nrm((N_CONV_LAYERS, D), 0.01),
        "moe_w_rg": nrm((DEPTH, D, MOE_GROUPS), D ** -0.5),
        "moe_b_rg": nrm((DEPTH, MOE_GROUPS), 0.01),
        "moe_w_re": nrm((DEPTH, D, MOE_EXPERTS), D ** -0.5),
        "moe_b_re": nrm((DEPTH, MOE_EXPERTS), 0.01),
        "moe_w_gate": nrm((DEPTH, MOE_EXPERTS, D, MOE_FF), D ** -0.5),
        "moe_w_up": nrm((DEPTH, MOE_EXPERTS, D, MOE_FF), D ** -0.5),
        "moe_w_down": nrm((DEPTH, MOE_EXPERTS, MOE_FF, D), MOE_FF ** -0.5),
        "final_norm_g": 1.0 + nrm((D,), 0.01),
    }


def reference(x, positions, norm_mix_g, norm_ffn_g, ret_w_in, ret_head_g, ret_w_out,
              conv_w_pw1, conv_b_pw1, conv_w_dw, conv_b_dw, conv_ln_g, conv_ln_b,
              conv_w_pw2, conv_b_pw2, moe_w_rg, moe_b_rg, moe_w_re, moe_b_re,
              moe_w_gate, moe_w_up, moe_w_down, final_norm_g):
    for i in range(DEPTH):
        h = rmsnorm(x, norm_mix_g[i])
        if i % N_MIXERS == 0:
            j = i // N_MIXERS
            x = x + retention_mixer(h, positions, ret_w_in[j], ret_head_g[j], ret_w_out[j])
        else:
            j = i // N_MIXERS
            x = x + conformer_conv_mixer(h, conv_w_pw1[j], conv_b_pw1[j], conv_w_dw[j], conv_b_dw[j],
                                         conv_ln_g[j], conv_ln_b[j], conv_w_pw2[j], conv_b_pw2[j])
        h = rmsnorm(x, norm_ffn_g[i])
        x = x + hierarchical_moe(h, moe_w_rg[i], moe_b_rg[i], moe_w_re[i], moe_b_re[i],
                                 moe_w_gate[i], moe_w_up[i], moe_w_down[i])
    return rmsnorm(x, final_norm_g)
```

```python
import functools
import math

import jax
import jax.numpy as jnp
import numpy as np
from jax import lax
from jax.experimental import pallas as pl
from jax.experimental.pallas import tpu as pltpu

F32 = jnp.float32
BF16 = jnp.bfloat16
U32 = jnp.uint32
I32 = jnp.int32

D_MODEL = 1024
RET_HEADS = 4
RET_DK = 256
RET_DV = 512
RET_QK = RET_HEADS * RET_DK
RET_V = RET_HEADS * RET_DV
ROPE_BASE = 10000.0
CONV_WIDTH = 31
MOE_GROUPS = 4
MOE_EPG = 8
MOE_EXPERTS = MOE_GROUPS * MOE_EPG
MOE_FF = 512
NORM_EPS = 1e-6

TM_PROJ = 256
RET_C = 256
TM_CONV = 256
CONV_HALO = 32
T_ROUTE = 512
TM_EXP = 256
T_DISP = 256
HALF = D_MODEL // 2

VMEM_LIMIT = 56 * 1024 * 1024


def _cparams(sem):
    return pltpu.CompilerParams(dimension_semantics=sem, vmem_limit_bytes=VMEM_LIMIT)


def _rms(x, g):
    ms = jnp.mean(x * x, axis=-1, keepdims=True)
    return x * lax.rsqrt(ms + NORM_EPS) * g


def _silu(x):
    return x * (1.0 / (1.0 + jnp.exp(-x)))


def _pack_bf16_pairs(y):
    lo = pltpu.bitcast(y[:, :HALF].astype(BF16).astype(F32), U32)
    hi = pltpu.bitcast(y[:, HALF:].astype(BF16).astype(F32), U32)
    return (hi & jnp.uint32(0xFFFF0000)) | (lo >> 16)


def _unpack_bf16_pairs(p):
    lo = pltpu.bitcast(p << 16, F32)
    hi = pltpu.bitcast(p & jnp.uint32(0xFFFF0000), F32)
    return lo, hi


def _ret_inproj_kernel(x_ref, pos_ref, g_ref, inv_ref, w_ref, q_ref, k_ref, v_ref, gate_ref):
    h = _rms(x_ref[...], g_ref[...]).astype(BF16)
    ang = pos_ref[...].astype(F32) * inv_ref[...]
    cos = jnp.cos(ang)
    sin = jnp.sin(ang)
    half = RET_DK // 2
    for hd in range(RET_HEADS):
        for base, out_ref, scale in ((0, q_ref, 1.0), (RET_QK, k_ref, RET_DK ** -0.5)):
            c0 = base + hd * RET_DK
            t = jnp.dot(h, w_ref[:, c0:c0 + RET_DK], preferred_element_type=F32)
            t1 = t[:, :half]
            t2 = t[:, half:]
            o1 = (t1 * cos - t2 * sin) * scale
            o2 = (t1 * sin + t2 * cos) * scale
            out_ref[:, hd * RET_DK:hd * RET_DK + half] = o1.astype(BF16)
            out_ref[:, hd * RET_DK + half:(hd + 1) * RET_DK] = o2.astype(BF16)
    for j in range(RET_V // 512):
        c0 = 2 * RET_QK + j * 512
        v_ref[:, j * 512:(j + 1) * 512] = jnp.dot(
            h, w_ref[:, c0:c0 + 512], preferred_element_type=F32).astype(BF16)
        c1 = 2 * RET_QK + RET_V + j * 512
        gate_ref[:, j * 512:(j + 1) * 512] = jnp.dot(
            h, w_ref[:, c1:c1 + 512], preferred_element_type=F32).astype(BF16)


def _ret_inproj(x, pos, g, w_in_bf16):
    n = x.shape[0]
    half = RET_DK // 2
    inv = (ROPE_BASE ** (-jnp.arange(half, dtype=F32) / half)).reshape(1, half)
    tm = TM_PROJ
    return pl.pallas_call(
        _ret_inproj_kernel,
        grid=(n // tm,),
        in_specs=[
            pl.BlockSpec((tm, D_MODEL), lambda i: (i, 0)),
            pl.BlockSpec((tm, 1), lambda i: (i, 0)),
            pl.BlockSpec((1, D_MODEL), lambda i: (0, 0)),
            pl.BlockSpec((1, half), lambda i: (0, 0)),
            pl.BlockSpec(w_in_bf16.shape, lambda i: (0, 0)),
        ],
        out_specs=[
            pl.BlockSpec((tm, RET_QK), lambda i: (i, 0)),
            pl.BlockSpec((tm, RET_QK), lambda i: (i, 0)),
            pl.BlockSpec((tm, RET_V), lambda i: (i, 0)),
            pl.BlockSpec((tm, RET_V), lambda i: (i, 0)),
        ],
        out_shape=[
            jax.ShapeDtypeStruct((n, RET_QK), BF16),
            jax.ShapeDtypeStruct((n, RET_QK), BF16),
            jax.ShapeDtypeStruct((n, RET_V), BF16),
            jax.ShapeDtypeStruct((n, RET_V), BF16),
        ],
        compiler_params=_cparams(("arbitrary",)),
        name="ret_inproj",
    )(x, pos, g, inv, w_in_bf16)


def _ret_core_kernel(cdec_ref, q_ref, k_ref, v_ref, gate_ref, x_ref, hg_ref, intra_ref, cross_ref,
                     kdec_ref, wo_ref, out_ref, state_ref):
    @pl.when(pl.program_id(0) == 0)
    def _():
        state_ref[...] = jnp.zeros_like(state_ref)

    acc = x_ref[...]
    for hd in range(RET_HEADS):
        q = q_ref[:, hd * RET_DK:(hd + 1) * RET_DK]
        k = k_ref[:, hd * RET_DK:(hd + 1) * RET_DK]
        v = v_ref[:, hd * RET_DV:(hd + 1) * RET_DV]
        state = state_ref[hd]
        scores = lax.dot_general(q, k, (((1,), (1,)), ((), ())), preferred_element_type=F32)
        scores = (scores * intra_ref[hd]).astype(BF16)
        o = jnp.dot(scores, v, preferred_element_type=F32)
        cross = cross_ref[hd]
        o_cross = jnp.dot(q, state.astype(BF16), preferred_element_type=F32)
        o = o + o_cross * jnp.concatenate([cross] * (RET_DV // 128), axis=1)
        kdec = kdec_ref[hd]
        kd = (k.astype(F32) * jnp.concatenate([kdec] * (RET_DK // 128), axis=1)).astype(BF16)
        upd = lax.dot_general(kd, v, (((0,), (0,)), ((), ())), preferred_element_type=F32)
        state_ref[hd] = state * cdec_ref[hd] + upd
        ms = jnp.mean(o * o, axis=-1, keepdims=True)
        on = o * lax.rsqrt(ms + NORM_EPS) * hg_ref[:, hd * RET_DV:(hd + 1) * RET_DV]
        gt = gate_ref[:, hd * RET_DV:(hd + 1) * RET_DV].astype(F32)
        y = (_silu(gt) * on).astype(BF16)
        acc = acc + jnp.dot(y, wo_ref[hd * RET_DV:(hd + 1) * RET_DV, :], preferred_element_type=F32)
    out_ref[...] = acc


def _ret_core(q, k, v, gate, x, head_g, w_out_bf16):
    n = x.shape[0]
    c = RET_C
    log_gamma = jnp.log1p(-(2.0 ** (-5.0 - jnp.arange(RET_HEADS, dtype=F32))))
    idx = jnp.arange(c, dtype=F32)
    diff = idx[:, None] - idx[None, :]
    intra = jnp.where(diff >= 0, jnp.exp(log_gamma[:, None, None] * jnp.maximum(diff, 0.0)), 0.0)
    cross = jnp.broadcast_to(jnp.exp(log_gamma[:, None] * (idx + 1.0))[:, :, None], (RET_HEADS, c, 128))
    kdec = jnp.broadcast_to(jnp.exp(log_gamma[:, None] * (c - 1.0 - idx))[:, :, None], (RET_HEADS, c, 128))
    cdec = jnp.exp(log_gamma * c)
    return pl.pallas_call(
        _ret_core_kernel,
        grid=(n // c,),
        in_specs=[
            pl.BlockSpec(memory_space=pltpu.SMEM),
            pl.BlockSpec((c, RET_QK), lambda i: (i, 0)),
            pl.BlockSpec((c, RET_QK), lambda i: (i, 0)),
            pl.BlockSpec((c, RET_V), lambda i: (i, 0)),
            pl.BlockSpec((c, RET_V), lambda i: (i, 0)),
            pl.BlockSpec((c, D_MODEL), lambda i: (i, 0)),
            pl.BlockSpec((1, RET_V), lambda i: (0, 0)),
            pl.BlockSpec((RET_HEADS, c, c), lambda i: (0, 0, 0)),
            pl.BlockSpec((RET_HEADS, c, 128), lambda i: (0, 0, 0)),
            pl.BlockSpec((RET_HEADS, c, 128), lambda i: (0, 0, 0)),
            pl.BlockSpec((RET_V, D_MODEL), lambda i: (0, 0)),
        ],
        out_specs=pl.BlockSpec((c, D_MODEL), lambda i: (i, 0)),
        out_shape=jax.ShapeDtypeStruct((n, D_MODEL), F32),
        scratch_shapes=[pltpu.VMEM((RET_HEADS, RET_DK, RET_DV), F32)],
        compiler_params=_cparams(("arbitrary",)),
        name="ret_core",
    )(cdec, q, k, v, gate, x, head_g, intra, cross, kdec, w_out_bf16)


def _conv_pw1_kernel(x_ref, g_ref, w_ref, b_ref, u_ref):
    h = _rms(x_ref[...], g_ref[...]).astype(BF16)
    for j in range(D_MODEL // 512):
        a = jnp.dot(h, w_ref[:, j * 512:(j + 1) * 512], preferred_element_type=F32)
        a = a + b_ref[:, j * 512:(j + 1) * 512]
        gt = jnp.dot(h, w_ref[:, D_MODEL + j * 512:D_MODEL + (j + 1) * 512], preferred_element_type=F32)
        gt = gt + b_ref[:, D_MODEL + j * 512:D_MODEL + (j + 1) * 512]
        u_ref[:, j * 512:(j + 1) * 512] = a * (1.0 / (1.0 + jnp.exp(-gt)))


def _conv_pw1(x, g, w_bf16, b):
    n = x.shape[0]
    tm = TM_CONV
    return pl.pallas_call(
        _conv_pw1_kernel,
        grid=(n // tm,),
        in_specs=[
            pl.BlockSpec((tm, D_MODEL), lambda i: (i, 0)),
            pl.BlockSpec((1, D_MODEL), lambda i: (0, 0)),
            pl.BlockSpec((D_MODEL, 2 * D_MODEL), lambda i: (0, 0)),
            pl.BlockSpec((1, 2 * D_MODEL), lambda i: (0, 0)),
        ],
        out_specs=pl.BlockSpec((tm, D_MODEL), lambda i: (i, 0)),
        out_shape=jax.ShapeDtypeStruct((n, D_MODEL), F32),
        compiler_params=_cparams(("arbitrary",)),
        name="conv_pw1",
    )(x, g, w_bf16, b)


def _conv_core_kernel(u_ref, halo_ref, x_ref, wdw_ref, bdw_ref, lng_ref, lnb_ref, w2_ref, b2_ref,
                      out_ref, win_ref, z_ref):
    tm = TM_CONV
    first = pl.program_id(0) == 0
    halo = halo_ref[...]
    win_ref[0:CONV_HALO, :] = jnp.where(first, jnp.zeros_like(halo), halo)
    win_ref[CONV_HALO:CONV_HALO + tm, :] = u_ref[...]
    off = CONV_HALO - (CONV_WIDTH - 1)
    for cc in range(D_MODEL // 128):
        cs = slice(cc * 128, (cc + 1) * 128)
        acc = jnp.zeros((tm, 128), F32)
        for w in range(CONV_WIDTH):
            acc = acc + win_ref[off + w:off + w + tm, cs] * wdw_ref[w:w + 1, cs]
        z_ref[:, cs] = acc + bdw_ref[:, cs]
    z = z_ref[...]
    mu = jnp.mean(z, axis=-1, keepdims=True)
    zc = z - mu
    var = jnp.mean(zc * zc, axis=-1, keepdims=True)
    zn = zc * lax.rsqrt(var + NORM_EPS) * lng_ref[...] + lnb_ref[...]
    y = _silu(zn).astype(BF16)
    out_ref[...] = x_ref[...] + jnp.dot(y, w2_ref[...], preferred_element_type=F32) + b2_ref[...]


def _conv_core(u, x, w_dw, b_dw, ln_g, ln_b, w2_bf16, b2):
    n = x.shape[0]
    tm = TM_CONV
    r = tm // CONV_HALO
    wdw_pad = jnp.zeros((32, D_MODEL), F32).at[:CONV_WIDTH].set(w_dw)
    return pl.pallas_call(
        _conv_core_kernel,
        grid=(n // tm,),
        in_specs=[
            pl.BlockSpec((tm, D_MODEL), lambda i: (i, 0)),
            pl.BlockSpec((CONV_HALO, D_MODEL), lambda i: (jnp.maximum(i * r - 1, 0), 0)),
            pl.BlockSpec((tm, D_MODEL), lambda i: (i, 0)),
            pl.BlockSpec((32, D_MODEL), lambda i: (0, 0)),
            pl.BlockSpec((1, D_MODEL), lambda i: (0, 0)),
            pl.BlockSpec((1, D_MODEL), lambda i: (0, 0)),
            pl.BlockSpec((1, D_MODEL), lambda i: (0, 0)),
            pl.BlockSpec((D_MODEL, D_MODEL), lambda i: (0, 0)),
            pl.BlockSpec((1, D_MODEL), lambda i: (0, 0)),
        ],
        out_specs=pl.BlockSpec((tm, D_MODEL), lambda i: (i, 0)),
        out_shape=jax.ShapeDtypeStruct((n, D_MODEL), F32),
        scratch_shapes=[pltpu.VMEM((CONV_HALO + tm, D_MODEL), F32), pltpu.VMEM((tm, D_MODEL), F32)],
        compiler_params=_cparams(("arbitrary",)),
        name="conv_core",
    )(u, u, x, wdw_pad, b_dw, ln_g, ln_b, w2_bf16, b2)


def _router_kernel(x_ref, g_ref, wr_ref, br_ref, hp_ref, ri_ref, rw_ref, cnt_ref, carry_ref):
    t = T_ROUTE

    @pl.when(pl.program_id(0) == 0)
    def _():
        carry_ref[...] = jnp.zeros_like(carry_ref)

    h = _rms(x_ref[...], g_ref[...])
    hp_ref[...] = _pack_bf16_pairs(h)
    logits = lax.dot_general(wr_ref[...], h, (((1,), (1,)), ((), ())),
                             precision=lax.Precision.HIGHEST, preferred_element_type=F32)
    logits = logits + br_ref[:, 0:1]

    best = logits[0:1]
    gi = jnp.zeros((1, t), I32)
    for j in range(1, MOE_GROUPS):
        r = logits[j:j + 1]
        up = r > best
        gi = jnp.where(up, j, gi)
        best = jnp.where(up, r, best)
    den = jnp.zeros((1, t), F32)
    for j in range(MOE_GROUPS):
        den = den + jnp.exp(logits[j:j + 1] - best)
    gate_g = 1.0 / den

    sel = logits[8:8 + MOE_EPG]
    for j in range(1, MOE_GROUPS):
        sel = jnp.where(gi == j, logits[8 + j * MOE_EPG:8 + (j + 1) * MOE_EPG], sel)

    m1 = sel[0:1]
    i1 = jnp.zeros((1, t), I32)
    for j in range(1, MOE_EPG):
        r = sel[j:j + 1]
        up = r > m1
        i1 = jnp.where(up, j, i1)
        m1 = jnp.where(up, r, m1)
    m2 = jnp.full((1, t), -jnp.inf, F32)
    i2 = jnp.zeros((1, t), I32)
    started = jnp.zeros((1, t), jnp.bool_)
    for j in range(MOE_EPG):
        r = sel[j:j + 1]
        ok = i1 != j
        up = ok & ((r > m2) | jnp.logical_not(started))
        i2 = jnp.where(up, j, i2)
        m2 = jnp.where(up, r, m2)
        started = started | ok
    e21 = jnp.exp(m2 - m1)
    p1 = 1.0 / (1.0 + e21)
    w1 = gate_g * p1
    w2 = gate_g * (e21 * p1)
    eid1 = gi * MOE_EPG + i1
    eid2 = gi * MOE_EPG + i2

    eio = lax.broadcasted_iota(I32, (MOE_EXPERTS, t), 0)
    oh1 = eio == eid1
    oh2 = eio == eid2
    oh = (oh1 | oh2).astype(F32)
    rio = lax.broadcasted_iota(I32, (t, t), 0)
    cio = lax.broadcasted_iota(I32, (t, t), 1)
    upper = (rio < cio).astype(BF16)
    cum = jnp.dot(oh.astype(BF16), upper, preferred_element_type=F32) + carry_ref[:, 0:1]
    rank1 = jnp.sum(jnp.where(oh1, cum, 0.0), axis=0, keepdims=True)
    rank2 = jnp.sum(jnp.where(oh2, cum, 0.0), axis=0, keepdims=True)
    carry_ref[...] = carry_ref[...] + jnp.sum(oh, axis=1, keepdims=True)
    cnt_ref[...] = carry_ref[...]

    zi = jnp.zeros((4, t), I32)
    ri_ref[...] = jnp.concatenate([eid1, eid2, rank1.astype(I32), rank2.astype(I32), zi], axis=0)
    zf = jnp.zeros((6, t), F32)
    rw_ref[...] = jnp.concatenate([w1, w2, zf], axis=0)


def _router(x, g, w_rg, b_rg, w_re, b_re):
    n = x.shape[0]
    t = T_ROUTE
    wr = jnp.zeros((40, D_MODEL), F32).at[0:MOE_GROUPS].set(w_rg.T).at[8:40].set(w_re.T)
    br = jnp.zeros((40,), F32).at[0:MOE_GROUPS].set(b_rg).at[8:40].set(b_re)
    br = jnp.broadcast_to(br[:, None], (40, 128))
    return pl.pallas_call(
        _router_kernel,
        grid=(n // t,),
        in_specs=[
            pl.BlockSpec((t, D_MODEL), lambda i: (i, 0)),
            pl.BlockSpec((1, D_MODEL), lambda i: (0, 0)),
            pl.BlockSpec((40, D_MODEL), lambda i: (0, 0)),
            pl.BlockSpec((40, 128), lambda i: (0, 0)),
        ],
        out_specs=[
            pl.BlockSpec((t, HALF), lambda i: (i, 0)),
            pl.BlockSpec((8, t), lambda i: (0, i)),
            pl.BlockSpec((8, t), lambda i: (0, i)),
            pl.BlockSpec((MOE_EXPERTS, 128), lambda i: (0, 0)),
        ],
        out_shape=[
            jax.ShapeDtypeStruct((n, HALF), U32),
            jax.ShapeDtypeStruct((8, n), I32),
            jax.ShapeDtypeStruct((8, n), F32),
            jax.ShapeDtypeStruct((MOE_EXPERTS, 128), F32),
        ],
        scratch_shapes=[pltpu.VMEM((MOE_EXPERTS, 128), F32)],
        compiler_params=_cparams(("arbitrary",)),
        name="moe_router",
    )(x, g, wr, br)


def _dispatch_kernel(dest_ref, hp_ref, xs_in_ref, xs_ref, sem):
    del xs_in_ref
    td = T_DISP

    def copy(j):
        row = lax.rem(j, td)
        return pltpu.make_async_copy(hp_ref.at[pl.ds(row, 1)], xs_ref.at[pl.ds(dest_ref[0, 0, j], 1)], sem)

    def start(j, c):
        copy(j).start()
        return c

    def wait(j, c):
        copy(j).wait()
        return c

    lax.fori_loop(0, 2 * td, start, 0)
    lax.fori_loop(0, 2 * td, wait, 0)


def _dispatch(hp, dest_blocks, p_rows):
    n = hp.shape[0]
    td = T_DISP
    xs0 = jnp.zeros((p_rows, HALF), U32)
    return pl.pallas_call(
        _dispatch_kernel,
        grid=(n // td,),
        in_specs=[
            pl.BlockSpec((1, 1, 2 * td), lambda i: (i, 0, 0), memory_space=pltpu.SMEM),
            pl.BlockSpec((td, HALF), lambda i: (i, 0)),
            pl.BlockSpec(memory_space=pl.ANY),
        ],
        out_specs=pl.BlockSpec(memory_space=pl.ANY),
        out_shape=jax.ShapeDtypeStruct((p_rows, HALF), U32),
        scratch_shapes=[pltpu.SemaphoreType.DMA(())],
        input_output_aliases={2: 0},
        compiler_params=_cparams(("arbitrary",)),
        name="moe_dispatch",
    )(dest_blocks, hp, xs0)


def _expert_kernel(blk_e_ref, nused_ref, xs_ref, wg_ref, wu_ref, wd_ref, ys_ref, wgb_ref, wub_ref, wdb_ref):
    i = pl.program_id(0)

    @pl.when(i < nused_ref[0])
    def _():
        prev = blk_e_ref[jnp.maximum(i - 1, 0)]
        changed = jnp.logical_or(i == 0, blk_e_ref[i] != prev)

        @pl.when(changed)
        def _():
            wgb_ref[...] = wg_ref[0].astype(BF16)
            wub_ref[...] = wu_ref[0].astype(BF16)
            wdb_ref[...] = wd_ref[0].astype(BF16)

        lo, hi = _unpack_bf16_pairs(xs_ref[...])
        xb = jnp.concatenate([lo.astype(BF16), hi.astype(BF16)], axis=1)
        a = jnp.dot(xb, wgb_ref[...], preferred_element_type=F32)
        b = jnp.dot(xb, wub_ref[...], preferred_element_type=F32)
        hm = (_silu(a) * b).astype(BF16)
        y = jnp.dot(hm, wdb_ref[...], preferred_element_type=F32)
        ys_ref[...] = _pack_bf16_pairs(y)

    @pl.when(i >= nused_ref[0])
    def _():
        ys_ref[...] = jnp.zeros_like(ys_ref)


def _experts(xs, blk_e, nused, w_gate, w_up, w_down):
    p_rows = xs.shape[0]
    tm = TM_EXP
    nblk = p_rows // tm

    def blk(i, be, nu):
        return jnp.minimum(i, nu[0] - 1)

    grid_spec = pltpu.PrefetchScalarGridSpec(
        num_scalar_prefetch=2,
        grid=(nblk,),
        in_specs=[
            pl.BlockSpec((tm, HALF), lambda i, be, nu: (blk(i, be, nu), 0)),
            pl.BlockSpec((1, D_MODEL, MOE_FF), lambda i, be, nu: (be[blk(i, be, nu)], 0, 0)),
            pl.BlockSpec((1, D_MODEL, MOE_FF), lambda i, be, nu: (be[blk(i, be, nu)], 0, 0)),
            pl.BlockSpec((1, MOE_FF, D_MODEL), lambda i, be, nu: (be[blk(i, be, nu)], 0, 0)),
        ],
        out_specs=pl.BlockSpec((tm, HALF), lambda i, be, nu: (i, 0)),
        scratch_shapes=[
            pltpu.VMEM((D_MODEL, MOE_FF), BF16),
            pltpu.VMEM((D_MODEL, MOE_FF), BF16),
            pltpu.VMEM((MOE_FF, D_MODEL), BF16),
        ],
    )
    return pl.pallas_call(
        _expert_kernel,
        grid_spec=grid_spec,
        out_shape=jax.ShapeDtypeStruct((p_rows, HALF), U32),
        compiler_params=_cparams(("arbitrary",)),
        name="moe_experts",
    )(blk_e, nused, xs, w_gate, w_up, w_down)


def _combine_kernel(dest_ref, x_ref, w1_ref, w2_ref, fg_ref, ys_ref, out_ref, ybuf_ref, sem, *, final_norm):
    td = T_DISP

    def copy(j):
        return pltpu.make_async_copy(ys_ref.at[pl.ds(dest_ref[0, 0, j], 1)], ybuf_ref.at[pl.ds(j, 1)], sem)

    def start(j, c):
        copy(j).start()
        return c

    def wait(j, c):
        copy(j).wait()
        return c

    lax.fori_loop(0, 2 * td, start, 0)
    lax.fori_loop(0, 2 * td, wait, 0)

    lo1, hi1 = _unpack_bf16_pairs(ybuf_ref[0:td, :])
    lo2, hi2 = _unpack_bf16_pairs(ybuf_ref[td:2 * td, :])
    w1 = w1_ref[...]
    w2 = w2_ref[...]
    x = x_ref[...]
    o_lo = x[:, :HALF] + w1 * lo1 + w2 * lo2
    o_hi = x[:, HALF:] + w1 * hi1 + w2 * hi2
    if final_norm:
        ms = (jnp.sum(o_lo * o_lo, axis=-1, keepdims=True)
              + jnp.sum(o_hi * o_hi, axis=-1, keepdims=True)) * (1.0 / D_MODEL)
        sc = lax.rsqrt(ms + NORM_EPS)
        o_lo = o_lo * sc * fg_ref[:, :HALF]
        o_hi = o_hi * sc * fg_ref[:, HALF:]
    out_ref[:, :HALF] = o_lo
    out_ref[:, HALF:] = o_hi


def _combine(ys, dest_blocks, x, w1, w2, final_g, final_norm):
    n = x.shape[0]
    td = T_DISP
    return pl.pallas_call(
        functools.partial(_combine_kernel, final_norm=final_norm),
        grid=(n // td,),
        in_specs=[
            pl.BlockSpec((1, 1, 2 * td), lambda i: (i, 0, 0), memory_space=pltpu.SMEM),
            pl.BlockSpec((td, D_MODEL), lambda i: (i, 0)),
            pl.BlockSpec((td, 1), lambda i: (i, 0)),
            pl.BlockSpec((td, 1), lambda i: (i, 0)),
            pl.BlockSpec((1, D_MODEL), lambda i: (0, 0)),
            pl.BlockSpec(memory_space=pl.ANY),
        ],
        out_specs=pl.BlockSpec((td, D_MODEL), lambda i: (i, 0)),
        out_shape=jax.ShapeDtypeStruct((n, D_MODEL), F32),
        scratch_shapes=[pltpu.VMEM((2 * td, HALF), U32), pltpu.SemaphoreType.DMA(())],
        compiler_params=_cparams(("arbitrary",)),
        name="moe_combine",
    )(dest_blocks, x, w1, w2, final_g, ys)


def _moe(x, g, w_rg, b_rg, w_re, b_re, w_gate, w_up, w_down, final_g, final_norm):
    n = x.shape[0]
    tm = TM_EXP
    p_rows = 2 * n + MOE_EXPERTS * tm
    nblk = p_rows // tm
    hp, ri, rw, cnt = _router(x, g, w_rg, b_rg, w_re, b_re)
    counts = cnt[:, 0].astype(I32)
    pcounts = (counts + tm - 1) // tm * tm
    pend = jnp.cumsum(pcounts)
    pstart = pend - pcounts
    dest1 = pstart[ri[0]] + ri[2]
    dest2 = pstart[ri[1]] + ri[3]
    td = T_DISP
    dest_blocks = jnp.concatenate([dest1.reshape(n // td, td), dest2.reshape(n // td, td)], axis=1)
    dest_blocks = dest_blocks.reshape(n // td, 1, 2 * td)
    blk_e = jnp.clip(jnp.searchsorted(pend, jnp.arange(nblk, dtype=I32) * tm, side='right'),
                     0, MOE_EXPERTS - 1).astype(I32)
    nused = (pend[-1] // tm).astype(I32).reshape(1)
    xs = _dispatch(hp, dest_blocks, p_rows)
    ys = _experts(xs, blk_e, nused, w_gate, w_up, w_down)
    return _combine(ys, dest_blocks, x, rw[0].reshape(n, 1), rw[1].reshape(n, 1), final_g, final_norm)


def kernel(x, positions, norm_mix_g, norm_ffn_g, ret_w_in, ret_head_g, ret_w_out, conv_w_pw1, conv_b_pw1, conv_w_dw, conv_b_dw, conv_ln_g, conv_ln_b, conv_w_pw2, conv_b_pw2, moe_w_rg, moe_b_rg, moe_w_re, moe_b_re, moe_w_gate, moe_w_up, moe_w_down, final_norm_g):
    b, s, d = x.shape
    n = b * s
    xt = x.reshape(n, d)
    pos = positions.reshape(n, 1)
    fg = final_norm_g.reshape(1, d)

    q, k, v, gate = _ret_inproj(xt, pos, norm_mix_g[0].reshape(1, d), ret_w_in[0].astype(BF16))
    xt = _ret_core(q, k, v, gate, xt, ret_head_g[0].reshape(1, RET_V), ret_w_out[0].astype(BF16))
    xt = _moe(xt, norm_ffn_g[0].reshape(1, d), moe_w_rg[0], moe_b_rg[0], moe_w_re[0], moe_b_re[0],
              moe_w_gate[0], moe_w_up[0], moe_w_down[0], fg, False)

    u = _conv_pw1(xt, norm_mix_g[1].reshape(1, d), conv_w_pw1[0].astype(BF16), conv_b_pw1[0].reshape(1, 2 * d))
    xt = _conv_core(u, xt, conv_w_dw[0], conv_b_dw[0].reshape(1, d), conv_ln_g[0].reshape(1, d),
                    conv_ln_b[0].reshape(1, d), conv_w_pw2[0].astype(BF16), conv_b_pw2[0].reshape(1, d))
    xt = _moe(xt, norm_ffn_g[1].reshape(1, d), moe_w_rg[1], moe_b_rg[1], moe_w_re[1], moe_b_re[1],
              moe_w_gate[1], moe_w_up[1], moe_w_down[1], fg, True)
    return xt.reshape(b, s, d)
```

```python
import functools
import math

import jax
import jax.numpy as jnp
import numpy as np
from jax import lax
from jax.experimental import pallas as pl
from jax.experimental.pallas import tpu as pltpu

F32 = jnp.float32
BF16 = jnp.bfloat16
U32 = jnp.uint32
I32 = jnp.int32

D_MODEL = 1024
RET_HEADS = 4
RET_DK = 256
RET_DV = 512
RET_QK = RET_HEADS * RET_DK
RET_V = RET_HEADS * RET_DV
ROPE_BASE = 10000.0
CONV_WIDTH = 31
MOE_GROUPS = 4
MOE_EPG = 8
MOE_EXPERTS = MOE_GROUPS * MOE_EPG
MOE_FF = 512
NORM_EPS = 1e-6

TM_PROJ = 256
RET_C = 256
TM_CONV = 256
CONV_HALO = 32
T_ROUTE = 512
TM_EXP = 256
T_DISP = 256
HALF = D_MODEL // 2

VMEM_LIMIT = 56 * 1024 * 1024


def _cparams(sem):
    return pltpu.CompilerParams(dimension_semantics=sem, vmem_limit_bytes=VMEM_LIMIT)


def _rms(x, g):
    ms = jnp.mean(x * x, axis=-1, keepdims=True)
    return x * lax.rsqrt(ms + NORM_EPS) * g


def _silu(x):
    return x * (1.0 / (1.0 + jnp.exp(-x)))


def _pack_bf16_pairs(y):
    lo = pltpu.bitcast(y[:, :HALF].astype(BF16).astype(F32), U32)
    hi = pltpu.bitcast(y[:, HALF:].astype(BF16).astype(F32), U32)
    return (hi & jnp.uint32(0xFFFF0000)) | (lo >> 16)


def _unpack_bf16_pairs(p):
    lo = pltpu.bitcast(p << 16, F32)
    hi = pltpu.bitcast(p & jnp.uint32(0xFFFF0000), F32)
    return lo, hi


def _ret_inproj_kernel(x_ref, pos_ref, g_ref, inv_ref, w_ref, q_ref, k_ref, v_ref, gate_ref):
    h = _rms(x_ref[...], g_ref[...]).astype(BF16)
    ang = pos_ref[...].astype(F32) * inv_ref[...]
    cos = jnp.cos(ang)
    sin = jnp.sin(ang)
    half = RET_DK // 2
    for hd in range(RET_HEADS):
        for base, out_ref, scale in ((0, q_ref, 1.0), (RET_QK, k_ref, RET_DK ** -0.5)):
            c0 = base + hd * RET_DK
            t = jnp.dot(h, w_ref[:, c0:c0 + RET_DK], preferred_element_type=F32)
            t1 = t[:, :half]
            t2 = t[:, half:]
            o1 = (t1 * cos - t2 * sin) * scale
            o2 = (t1 * sin + t2 * cos) * scale
            out_ref[:, hd * RET_DK:hd * RET_DK + half] = o1.astype(BF16)
            out_ref[:, hd * RET_DK + half:(hd + 1) * RET_DK] = o2.astype(BF16)
    for j in range(RET_V // 512):
        c0 = 2 * RET_QK + j * 512
        v_ref[:, j * 512:(j + 1) * 512] = jnp.dot(
            h, w_ref[:, c0:c0 + 512], preferred_element_type=F32).astype(BF16)
        c1 = 2 * RET_QK + RET_V + j * 512
        gate_ref[:, j * 512:(j + 1) * 512] = jnp.dot(
            h, w_ref[:, c1:c1 + 512], preferred_element_type=F32).astype(BF16)


def _ret_inproj(x, pos, g, w_in_bf16):
    n = x.shape[0]
    half = RET_DK // 2
    inv = (ROPE_BASE ** (-jnp.arange(half, dtype=F32) / half)).reshape(1, half)
    tm = TM_PROJ
    return pl.pallas_call(
        _ret_inproj_kernel,
        grid=(n // tm,),
        in_specs=[
            pl.BlockSpec((tm, D_MODEL), lambda i: (i, 0)),
            pl.BlockSpec((tm, 1), lambda i: (i, 0)),
            pl.BlockSpec((1, D_MODEL), lambda i: (0, 0)),
            pl.BlockSpec((1, half), lambda i: (0, 0)),
            pl.BlockSpec(w_in_bf16.shape, lambda i: (0, 0)),
        ],
        out_specs=[
            pl.BlockSpec((tm, RET_QK), lambda i: (i, 0)),
            pl.BlockSpec((tm, RET_QK), lambda i: (i, 0)),
            pl.BlockSpec((tm, RET_V), lambda i: (i, 0)),
            pl.BlockSpec((tm, RET_V), lambda i: (i, 0)),
        ],
        out_shape=[
            jax.ShapeDtypeStruct((n, RET_QK), BF16),
            jax.ShapeDtypeStruct((n, RET_QK), BF16),
            jax.ShapeDtypeStruct((n, RET_V), BF16),
            jax.ShapeDtypeStruct((n, RET_V), BF16),
        ],
        compiler_params=_cparams(("arbitrary",)),
        name="ret_inproj",
    )(x, pos, g, inv, w_in_bf16)


def _ret_core_kernel(cdec_ref, q_ref, k_ref, v_ref, gate_ref, x_ref, hg_ref, intra_ref, cross_ref,
                     kdec_ref, wo_ref, out_ref, state_ref):
    @pl.when(pl.program_id(0) == 0)
    def _():
        state_ref[...] = jnp.zeros_like(state_ref)

    acc = x_ref[...]
    for hd in range(RET_HEADS):
        q = q_ref[:, hd * RET_DK:(hd + 1) * RET_DK]
        k = k_ref[:, hd * RET_DK:(hd + 1) * RET_DK]
        v = v_ref[:, hd * RET_DV:(hd + 1) * RET_DV]
        state = state_ref[hd]
        scores = lax.dot_general(q, k, (((1,), (1,)), ((), ())), preferred_element_type=F32)
        scores = (scores * intra_ref[hd]).astype(BF16)
        o = jnp.dot(scores, v, preferred_element_type=F32)
        cross = cross_ref[hd]
        o_cross = jnp.dot(q, state.astype(BF16), preferred_element_type=F32)
        o = o + o_cross * jnp.concatenate([cross] * (RET_DV // 128), axis=1)
        kdec = kdec_ref[hd]
        kd = (k.astype(F32) * jnp.concatenate([kdec] * (RET_DK // 128), axis=1)).astype(BF16)
        upd = lax.dot_general(kd, v, (((0,), (0,)), ((), ())), preferred_element_type=F32)
        state_ref[hd] = state * cdec_ref[hd] + upd
        ms = jnp.mean(o * o, axis=-1, keepdims=True)
        on = o * lax.rsqrt(ms + NORM_EPS) * hg_ref[:, hd * RET_DV:(hd + 1) * RET_DV]
        gt = gate_ref[:, hd * RET_DV:(hd + 1) * RET_DV].astype(F32)
        y = (_silu(gt) * on).astype(BF16)
        acc = acc + jnp.dot(y, wo_ref[hd * RET_DV:(hd + 1) * RET_DV, :], preferred_element_type=F32)
    out_ref[...] = acc


def _ret_core(q, k, v, gate, x, head_g, w_out_bf16):
    n = x.shape[0]
    c = RET_C
    log_gamma = jnp.log1p(-(2.0 ** (-5.0 - jnp.arange(RET_HEADS, dtype=F32))))
    idx = jnp.arange(c, dtype=F32)
    diff = idx[:, None] - idx[None, :]
    intra = jnp.where(diff >= 0, jnp.exp(log_gamma[:, None, None] * jnp.maximum(diff, 0.0)), 0.0)
    cross = jnp.broadcast_to(jnp.exp(log_gamma[:, None] * (idx + 1.0))[:, :, None], (RET_HEADS, c, 128))
    kdec = jnp.broadcast_to(jnp.exp(log_gamma[:, None] * (c - 1.0 - idx))[:, :, None], (RET_HEADS, c, 128))
    cdec = jnp.exp(log_gamma * c)
    return pl.pallas_call(
        _ret_core_kernel,
        grid=(n // c,),
        in_specs=[
            pl.BlockSpec(memory_space=pltpu.SMEM),
            pl.BlockSpec((c, RET_QK), lambda i: (i, 0)),
            pl.BlockSpec((c, RET_QK), lambda i: (i, 0)),
            pl.BlockSpec((c, RET_V), lambda i: (i, 0)),
            pl.BlockSpec((c, RET_V), lambda i: (i, 0)),
            pl.BlockSpec((c, D_MODEL), lambda i: (i, 0)),
            pl.BlockSpec((1, RET_V), lambda i: (0, 0)),
            pl.BlockSpec((RET_HEADS, c, c), lambda i: (0, 0, 0)),
            pl.BlockSpec((RET_HEADS, c, 128), lambda i: (0, 0, 0)),
            pl.BlockSpec((RET_HEADS, c, 128), lambda i: (0, 0, 0)),
            pl.BlockSpec((RET_V, D_MODEL), lambda i: (0, 0)),
        ],
        out_specs=pl.BlockSpec((c, D_MODEL), lambda i: (i, 0)),
        out_shape=jax.ShapeDtypeStruct((n, D_MODEL), F32),
        scratch_shapes=[pltpu.VMEM((RET_HEADS, RET_DK, RET_DV), F32)],
        compiler_params=_cparams(("arbitrary",)),
        name="ret_core",
    )(cdec, q, k, v, gate, x, head_g, intra, cross, kdec, w_out_bf16)


def _conv_pw1_kernel(x_ref, g_ref, w_ref, b_ref, u_ref):
    h = _rms(x_ref[...], g_ref[...]).astype(BF16)
    for j in range(D_MODEL // 512):
        a = jnp.dot(h, w_ref[:, j * 512:(j + 1) * 512], preferred_element_type=F32)
        a = a + b_ref[:, j * 512:(j + 1) * 512]
        gt = jnp.dot(h, w_ref[:, D_MODEL + j * 512:D_MODEL + (j + 1) * 512], preferred_element_type=F32)
        gt = gt + b_ref[:, D_MODEL + j * 512:D_MODEL + (j + 1) * 512]
        u_ref[:, j * 512:(j + 1) * 512] = a * (1.0 / (1.0 + jnp.exp(-gt)))


def _conv_pw1(x, g, w_bf16, b):
    n = x.shape[0]
    tm = TM_CONV
    return pl.pallas_call(
        _conv_pw1_kernel,
        grid=(n // tm,),
        in_specs=[
            pl.BlockSpec((tm, D_MODEL), lambda i: (i, 0)),
            pl.BlockSpec((1, D_MODEL), lambda i: (0, 0)),
            pl.BlockSpec((D_MODEL, 2 * D_MODEL), lambda i: (0, 0)),
            pl.BlockSpec((1, 2 * D_MODEL), lambda i: (0, 0)),
        ],
        out_specs=pl.BlockSpec((tm, D_MODEL), lambda i: (i, 0)),
        out_shape=jax.ShapeDtypeStruct((n, D_MODEL), F32),
        compiler_params=_cparams(("arbitrary",)),
        name="conv_pw1",
    )(x, g, w_bf16, b)


def _conv_core_kernel(u_ref, halo_ref, x_ref, wdw_ref, bdw_ref, lng_ref, lnb_ref, w2_ref, b2_ref,
                      out_ref, win_ref, z_ref):
    tm = TM_CONV
    first = pl.program_id(0) == 0
    halo = halo_ref[...]
    win_ref[0:CONV_HALO, :] = jnp.where(first, jnp.zeros_like(halo), halo)
    win_ref[CONV_HALO:CONV_HALO + tm, :] = u_ref[...]
    off = CONV_HALO - (CONV_WIDTH - 1)
    for cc in range(D_MODEL // 128):
        cs = slice(cc * 128, (cc + 1) * 128)
        acc = jnp.zeros((tm, 128), F32)
        for w in range(CONV_WIDTH):
            acc = acc + win_ref[off + w:off + w + tm, cs] * wdw_ref[w:w + 1, cs]
        z_ref[:, cs] = acc + bdw_ref[:, cs]
    z = z_ref[...]
    mu = jnp.mean(z, axis=-1, keepdims=True)
    zc = z - mu
    var = jnp.mean(zc * zc, axis=-1, keepdims=True)
    zn = zc * lax.rsqrt(var + NORM_EPS) * lng_ref[...] + lnb_ref[...]
    y = _silu(zn).astype(BF16)
    out_ref[...] = x_ref[...] + jnp.dot(y, w2_ref[...], preferred_element_type=F32) + b2_ref[...]


def _conv_core(u, x, w_dw, b_dw, ln_g, ln_b, w2_bf16, b2):
    n = x.shape[0]
    tm = TM_CONV
    r = tm // CONV_HALO
    wdw_pad = jnp.zeros((32, D_MODEL), F32).at[:CONV_WIDTH].set(w_dw)
    return pl.pallas_call(
        _conv_core_kernel,
        grid=(n // tm,),
        in_specs=[
            pl.BlockSpec((tm, D_MODEL), lambda i: (i, 0)),
            pl.BlockSpec((CONV_HALO, D_MODEL), lambda i: (jnp.maximum(i * r - 1, 0), 0)),
            pl.BlockSpec((tm, D_MODEL), lambda i: (i, 0)),
            pl.BlockSpec((32, D_MODEL), lambda i: (0, 0)),
            pl.BlockSpec((1, D_MODEL), lambda i: (0, 0)),
            pl.BlockSpec((1, D_MODEL), lambda i: (0, 0)),
            pl.BlockSpec((1, D_MODEL), lambda i: (0, 0)),
            pl.BlockSpec((D_MODEL, D_MODEL), lambda i: (0, 0)),
            pl.BlockSpec((1, D_MODEL), lambda i: (0, 0)),
        ],
        out_specs=pl.BlockSpec((tm, D_MODEL), lambda i: (i, 0)),
        out_shape=jax.ShapeDtypeStruct((n, D_MODEL), F32),
        scratch_shapes=[pltpu.VMEM((CONV_HALO + tm, D_MODEL), F32), pltpu.VMEM((tm, D_MODEL), F32)],
        compiler_params=_cparams(("arbitrary",)),
        name="conv_core",
    )(u, u, x, wdw_pad, b_dw, ln_g, ln_b, w2_bf16, b2)


def _router_kernel(x_ref, g_ref, wr_ref, br_ref, hp_ref, ri_ref, rw_ref, cnt_ref, carry_ref):
    t = T_ROUTE

    @pl.when(pl.program_id(0) == 0)
    def _():
        carry_ref[...] = jnp.zeros_like(carry_ref)

    h = _rms(x_ref[...], g_ref[...])
    hp_ref[...] = _pack_bf16_pairs(h)
    logits = lax.dot_general(wr_ref[...], h, (((1,), (1,)), ((), ())),
                             precision=lax.Precision.HIGHEST, preferred_element_type=F32)
    logits = logits + br_ref[:, 0:1]

    best = logits[0:1]
    gi = jnp.zeros((1, t), I32)
    for j in range(1, MOE_GROUPS):
        r = logits[j:j + 1]
        up = r > best
        gi = jnp.where(up, j, gi)
        best = jnp.where(up, r, best)
    den = jnp.zeros((1, t), F32)
    for j in range(MOE_GROUPS):
        den = den + jnp.exp(logits[j:j + 1] - best)
    gate_g = 1.0 / den

    sel = logits[8:8 + MOE_EPG]
    for j in range(1, MOE_GROUPS):
        sel = jnp.where(gi == j, logits[8 + j * MOE_EPG:8 + (j + 1) * MOE_EPG], sel)

    m1 = sel[0:1]
    i1 = jnp.zeros((1, t), I32)
    for j in range(1, MOE_EPG):
        r = sel[j:j + 1]
        up = r > m1
        i1 = jnp.where(up, j, i1)
        m1 = jnp.where(up, r, m1)
    m2 = jnp.full((1, t), -jnp.inf, F32)
    i2 = jnp.zeros((1, t), I32)
    started = jnp.zeros((1, t), jnp.bool_)
    for j in range(MOE_EPG):
        r = sel[j:j + 1]
        ok = i1 != j
        up = ok & ((r > m2) | jnp.logical_not(started))
        i2 = jnp.where(up, j, i2)
        m2 = jnp.where(up, r, m2)
        started = started | ok
    e21 = jnp.exp(m2 - m1)
    p1 = 1.0 / (1.0 + e21)
    w1 = gate_g * p1
    w2 = gate_g * (e21 * p1)
    eid1 = gi * MOE_EPG + i1
    eid2 = gi * MOE_EPG + i2

    eio = lax.broadcasted_iota(I32, (MOE_EXPERTS, t), 0)
    oh1 = eio == eid1
    oh2 = eio == eid2
    oh = (oh1 | oh2).astype(F32)
    rio = lax.broadcasted_iota(I32, (t, t), 0)
    cio = lax.broadcasted_iota(I32, (t, t), 1)
    upper = (rio < cio).astype(BF16)
    cum = jnp.dot(oh.astype(BF16), upper, preferred_element_type=F32) + carry_ref[:, 0:1]
    rank1 = jnp.sum(jnp.where(oh1, cum, 0.0), axis=0, keepdims=True)
    rank2 = jnp.sum(jnp.where(oh2, cum, 0.0), axis=0, keepdims=True)
    carry_ref[...] = carry_ref[...] + jnp.sum(oh, axis=1, keepdims=True)
    cnt_ref[...] = carry_ref[...]

    zi = jnp.zeros((4, t), I32)
    ri_ref[...] = jnp.concatenate([eid1, eid2, rank1.astype(I32), rank2.astype(I32), zi], axis=0)
    zf = jnp.zeros((6, t), F32)
    rw_ref[...] = jnp.concatenate([w1, w2, zf], axis=0)


def _router(x, g, w_rg, b_rg, w_re, b_re):
    n = x.shape[0]
    t = T_ROUTE
    wr = jnp.zeros((40, D_MODEL), F32).at[0:MOE_GROUPS].set(w_rg.T).at[8:40].set(w_re.T)
    br = jnp.zeros((40,), F32).at[0:MOE_GROUPS].set(b_rg).at[8:40].set(b_re)
    br = jnp.broadcast_to(br[:, None], (40, 128))
    return pl.pallas_call(
        _router_kernel,
        grid=(n // t,),
        in_specs=[
            pl.BlockSpec((t, D_MODEL), lambda i: (i, 0)),
            pl.BlockSpec((1, D_MODEL), lambda i: (0, 0)),
            pl.BlockSpec((40, D_MODEL), lambda i: (0, 0)),
            pl.BlockSpec((40, 128), lambda i: (0, 0)),
        ],
        out_specs=[
            pl.BlockSpec((t, HALF), lambda i: (i, 0)),
            pl.BlockSpec((8, t), lambda i: (0, i)),
            pl.BlockSpec((8, t), lambda i: (0, i)),
            pl.BlockSpec((MOE_EXPERTS, 128), lambda i: (0, 0)),
        ],
        out_shape=[
            jax.ShapeDtypeStruct((n, HALF), U32),
            jax.ShapeDtypeStruct((8, n), I32),
            jax.ShapeDtypeStruct((8, n), F32),
            jax.ShapeDtypeStruct((MOE_EXPERTS, 128), F32),
        ],
        scratch_shapes=[pltpu.VMEM((MOE_EXPERTS, 128), F32)],
        compiler_params=_cparams(("arbitrary",)),
        name="moe_router",
    )(x, g, wr, br)


def _dispatch_kernel(dest_ref, hp_ref, xs_in_ref, xs_ref, sem):
    del xs_in_ref
    td = T_DISP

    def copy(j):
        row = lax.rem(j, td)
        return pltpu.make_async_copy(hp_ref.at[pl.ds(row, 1)], xs_ref.at[pl.ds(dest_ref[0, 0, j], 1)], sem)

    def start(j, c):
        copy(j).start()
        return c

    def wait(j, c):
        copy(j).wait()
        return c

    lax.fori_loop(0, 2 * td, start, 0)
    lax.fori_loop(0, 2 * td, wait, 0)


def _dispatch(hp, dest_blocks, p_rows):
    n = hp.shape[0]
    td = T_DISP
    xs0 = jnp.zeros((p_rows, HALF), U32)
    return pl.pallas_call(
        _dispatch_kernel,
        grid=(n // td,),
        in_specs=[
            pl.BlockSpec((1, 1, 2 * td), lambda i: (i, 0, 0), memory_space=pltpu.SMEM),
            pl.BlockSpec((td, HALF), lambda i: (i, 0)),
            pl.BlockSpec(memory_space=pl.ANY),
        ],
        out_specs=pl.BlockSpec(memory_space=pl.ANY),
        out_shape=jax.ShapeDtypeStruct((p_rows, HALF), U32),
        scratch_shapes=[pltpu.SemaphoreType.DMA(())],
        input_output_aliases={2: 0},
        compiler_params=_cparams(("arbitrary",)),
        name="moe_dispatch",
    )(dest_blocks, hp, xs0)


def _expert_kernel(blk_e_ref, nused_ref, xs_ref, wg_ref, wu_ref, wd_ref, ys_ref, wgb_ref, wub_ref, wdb_ref):
    i = pl.program_id(0)

    @pl.when(i < nused_ref[0])
    def _():
        prev = blk_e_ref[jnp.maximum(i - 1, 0)]
        changed = jnp.logical_or(i == 0, blk_e_ref[i] != prev)

        @pl.when(changed)
        def _():
            wgb_ref[...] = wg_ref[0, 0].astype(BF16)
            wub_ref[...] = wu_ref[0, 0].astype(BF16)
            wdb_ref[...] = wd_ref[0, 0].astype(BF16)

        lo, hi = _unpack_bf16_pairs(xs_ref[...])
        xb = jnp.concatenate([lo.astype(BF16), hi.astype(BF16)], axis=1)
        a = jnp.dot(xb, wgb_ref[...], preferred_element_type=F32)
        b = jnp.dot(xb, wub_ref[...], preferred_element_type=F32)
        hm = (_silu(a) * b).astype(BF16)
        y = jnp.dot(hm, wdb_ref[...], preferred_element_type=F32)
        ys_ref[...] = _pack_bf16_pairs(y)

    @pl.when(i >= nused_ref[0])
    def _():
        ys_ref[...] = jnp.zeros_like(ys_ref)


def _experts(xs, blk_e, nused, w_gate, w_up, w_down, layer):
    p_rows = xs.shape[0]
    tm = TM_EXP
    nblk = p_rows // tm

    def blk(i, be, nu):
        return jnp.minimum(i, nu[0] - 1)

    grid_spec = pltpu.PrefetchScalarGridSpec(
        num_scalar_prefetch=2,
        grid=(nblk,),
        in_specs=[
            pl.BlockSpec((tm, HALF), lambda i, be, nu: (blk(i, be, nu), 0)),
            pl.BlockSpec((1, 1, D_MODEL, MOE_FF), lambda i, be, nu: (layer, be[blk(i, be, nu)], 0, 0)),
            pl.BlockSpec((1, 1, D_MODEL, MOE_FF), lambda i, be, nu: (layer, be[blk(i, be, nu)], 0, 0)),
            pl.BlockSpec((1, 1, MOE_FF, D_MODEL), lambda i, be, nu: (layer, be[blk(i, be, nu)], 0, 0)),
        ],
        out_specs=pl.BlockSpec((tm, HALF), lambda i, be, nu: (i, 0)),
        scratch_shapes=[
            pltpu.VMEM((D_MODEL, MOE_FF), BF16),
            pltpu.VMEM((D_MODEL, MOE_FF), BF16),
            pltpu.VMEM((MOE_FF, D_MODEL), BF16),
        ],
    )
    return pl.pallas_call(
        _expert_kernel,
        grid_spec=grid_spec,
        out_shape=jax.ShapeDtypeStruct((p_rows, HALF), U32),
        compiler_params=_cparams(("arbitrary",)),
        name="moe_experts",
    )(blk_e, nused, xs, w_gate, w_up, w_down)


def _combine_kernel(dest_ref, x_ref, w1_ref, w2_ref, fg_ref, ys_ref, out_ref, ybuf_ref, sem, *, final_norm):
    td = T_DISP

    def copy(j):
        return pltpu.make_async_copy(ys_ref.at[pl.ds(dest_ref[0, 0, j], 1)], ybuf_ref.at[pl.ds(j, 1)], sem)

    def start(j, c):
        copy(j).start()
        return c

    def wait(j, c):
        copy(j).wait()
        return c

    lax.fori_loop(0, 2 * td, start, 0)
    lax.fori_loop(0, 2 * td, wait, 0)

    lo1, hi1 = _unpack_bf16_pairs(ybuf_ref[0:td, :])
    lo2, hi2 = _unpack_bf16_pairs(ybuf_ref[td:2 * td, :])
    w1 = w1_ref[...]
    w2 = w2_ref[...]
    x = x_ref[...]
    o_lo = x[:, :HALF] + w1 * lo1 + w2 * lo2
    o_hi = x[:, HALF:] + w1 * hi1 + w2 * hi2
    if final_norm:
        ms = (jnp.sum(o_lo * o_lo, axis=-1, keepdims=True)
              + jnp.sum(o_hi * o_hi, axis=-1, keepdims=True)) * (1.0 / D_MODEL)
        sc = lax.rsqrt(ms + NORM_EPS)
        o_lo = o_lo * sc * fg_ref[:, :HALF]
        o_hi = o_hi * sc * fg_ref[:, HALF:]
    out_ref[:, :HALF] = o_lo
    out_ref[:, HALF:] = o_hi


def _combine(ys, dest_blocks, x, w1, w2, final_g, final_norm):
    n = x.shape[0]
    td = T_DISP
    return pl.pallas_call(
        functools.partial(_combine_kernel, final_norm=final_norm),
        grid=(n // td,),
        in_specs=[
            pl.BlockSpec((1, 1, 2 * td), lambda i: (i, 0, 0), memory_space=pltpu.SMEM),
            pl.BlockSpec((td, D_MODEL), lambda i: (i, 0)),
            pl.BlockSpec((td, 1), lambda i: (i, 0)),
            pl.BlockSpec((td, 1), lambda i: (i, 0)),
            pl.BlockSpec((1, D_MODEL), lambda i: (0, 0)),
            pl.BlockSpec(memory_space=pl.ANY),
        ],
        out_specs=pl.BlockSpec((td, D_MODEL), lambda i: (i, 0)),
        out_shape=jax.ShapeDtypeStruct((n, D_MODEL), F32),
        scratch_shapes=[pltpu.VMEM((2 * td, HALF), U32), pltpu.SemaphoreType.DMA(())],
        compiler_params=_cparams(("arbitrary",)),
        name="moe_combine",
    )(dest_blocks, x, w1, w2, final_g, ys)


def _moe(x, g, w_rg, b_rg, w_re, b_re, w_gate, w_up, w_down, layer, final_g, final_norm):
    n = x.shape[0]
    tm = TM_EXP
    p_rows = 2 * n + MOE_EXPERTS * tm
    nblk = p_rows // tm
    hp, ri, rw, cnt = _router(x, g, w_rg, b_rg, w_re, b_re)
    counts = cnt[:, 0].astype(I32)
    pcounts = (counts + tm - 1) // tm * tm
    pend = jnp.cumsum(pcounts)
    pstart = pend - pcounts
    eio = jnp.arange(MOE_EXPERTS, dtype=I32)[:, None]
    dest1 = jnp.sum(jnp.where(ri[0][None, :] == eio, pstart[:, None], 0), axis=0) + ri[2]
    dest2 = jnp.sum(jnp.where(ri[1][None, :] == eio, pstart[:, None], 0), axis=0) + ri[3]
    td = T_DISP
    dest_blocks = jnp.concatenate([dest1.reshape(n // td, td), dest2.reshape(n // td, td)], axis=1)
    dest_blocks = dest_blocks.reshape(n // td, 1, 2 * td)
    blk_start = jnp.arange(nblk, dtype=I32) * tm
    blk_e = jnp.minimum(jnp.sum((pend[None, :] <= blk_start[:, None]).astype(I32), axis=1), MOE_EXPERTS - 1)
    nused = (pend[-1] // tm).astype(I32).reshape(1)
    xs = _dispatch(hp, dest_blocks, p_rows)
    ys = _experts(xs, blk_e, nused, w_gate, w_up, w_down, layer)
    return _combine(ys, dest_blocks, x, rw[0].reshape(n, 1), rw[1].reshape(n, 1), final_g, final_norm)


def kernel(x, positions, norm_mix_g, norm_ffn_g, ret_w_in, ret_head_g, ret_w_out, conv_w_pw1, conv_b_pw1, conv_w_dw, conv_b_dw, conv_ln_g, conv_ln_b, conv_w_pw2, conv_b_pw2, moe_w_rg, moe_b_rg, moe_w_re, moe_b_re, moe_w_gate, moe_w_up, moe_w_down, final_norm_g):
    b, s, d = x.shape
    n = b * s
    xt = x.reshape(n, d)
    pos = positions.reshape(n, 1)
    fg = final_norm_g.reshape(1, d)

    q, k, v, gate = _ret_inproj(xt, pos, norm_mix_g[0].reshape(1, d), ret_w_in[0].astype(BF16))
    xt = _ret_core(q, k, v, gate, xt, ret_head_g[0].reshape(1, RET_V), ret_w_out[0].astype(BF16))
    xt = _moe(xt, norm_ffn_g[0].reshape(1, d), moe_w_rg[0], moe_b_rg[0], moe_w_re[0], moe_b_re[0],
              moe_w_gate, moe_w_up, moe_w_down, 0, fg, False)

    u = _conv_pw1(xt, norm_mix_g[1].reshape(1, d), conv_w_pw1[0].astype(BF16), conv_b_pw1[0].reshape(1, 2 * d))
    xt = _conv_core(u, xt, conv_w_dw[0], conv_b_dw[0].reshape(1, d), conv_ln_g[0].reshape(1, d),
                    conv_ln_b[0].reshape(1, d), conv_w_pw2[0].astype(BF16), conv_b_pw2[0].reshape(1, d))
    xt = _moe(xt, norm_ffn_g[1].reshape(1, d), moe_w_rg[1], moe_b_rg[1], moe_w_re[1], moe_b_re[1],
              moe_w_gate, moe_w_up, moe_w_down, 1, fg, True)
    return xt.reshape(b, s, d)
```

```python
import functools
import math

import jax
import jax.numpy as jnp
import numpy as np
from jax import lax
from jax.experimental import pallas as pl
from jax.experimental.pallas import tpu as pltpu
from jax.experimental.pallas import tpu_sc as plsc

F32 = jnp.float32
BF16 = jnp.bfloat16
U32 = jnp.uint32
I32 = jnp.int32

D_MODEL = 1024
RET_HEADS = 4
RET_DK = 256
RET_DV = 512
RET_QK = RET_HEADS * RET_DK
RET_V = RET_HEADS * RET_DV
ROPE_BASE = 10000.0
CONV_WIDTH = 31
MOE_GROUPS = 4
MOE_EPG = 8
MOE_EXPERTS = MOE_GROUPS * MOE_EPG
MOE_FF = 512
NORM_EPS = 1e-6

TM_PROJ = 256
RET_C = 256
TM_CONV = 256
CONV_HALO = 32
T_ROUTE = 512
TM_EXP = 256
T_COMB = 256
SC_CORES = 2
SC_SUBCORES = 16
SC_WORKERS = SC_CORES * SC_SUBCORES
SC_ROWS = 128
HALF = D_MODEL // 2

VMEM_LIMIT = 56 * 1024 * 1024


def _cparams(sem):
    return pltpu.CompilerParams(dimension_semantics=sem, vmem_limit_bytes=VMEM_LIMIT)


def _rms(x, g):
    ms = jnp.mean(x * x, axis=-1, keepdims=True)
    return x * lax.rsqrt(ms + NORM_EPS) * g


def _silu(x):
    return x * (1.0 / (1.0 + jnp.exp(-x)))


def _pack_bf16_pairs(y):
    lo = pltpu.bitcast(y[:, :HALF].astype(BF16).astype(F32), U32)
    hi = pltpu.bitcast(y[:, HALF:].astype(BF16).astype(F32), U32)
    return (hi & jnp.uint32(0xFFFF0000)) | (lo >> 16)


def _unpack_bf16_pairs(p):
    lo = pltpu.bitcast(p << 16, F32)
    hi = pltpu.bitcast(p & jnp.uint32(0xFFFF0000), F32)
    return lo, hi


def _ret_inproj_kernel(x_ref, pos_ref, g_ref, inv_ref, w_ref, q_ref, k_ref, v_ref, gate_ref):
    h = _rms(x_ref[...], g_ref[...]).astype(BF16)
    ang = pos_ref[...].astype(F32) * inv_ref[...]
    cos = jnp.cos(ang)
    sin = jnp.sin(ang)
    half = RET_DK // 2
    for hd in range(RET_HEADS):
        for base, out_ref, scale in ((0, q_ref, 1.0), (RET_QK, k_ref, RET_DK ** -0.5)):
            c0 = base + hd * RET_DK
            t = jnp.dot(h, w_ref[:, c0:c0 + RET_DK], preferred_element_type=F32)
            t1 = t[:, :half]
            t2 = t[:, half:]
            o1 = (t1 * cos - t2 * sin) * scale
            o2 = (t1 * sin + t2 * cos) * scale
            out_ref[:, hd * RET_DK:hd * RET_DK + half] = o1.astype(BF16)
            out_ref[:, hd * RET_DK + half:(hd + 1) * RET_DK] = o2.astype(BF16)
    for j in range(RET_V // 512):
        c0 = 2 * RET_QK + j * 512
        v_ref[:, j * 512:(j + 1) * 512] = jnp.dot(
            h, w_ref[:, c0:c0 + 512], preferred_element_type=F32).astype(BF16)
        c1 = 2 * RET_QK + RET_V + j * 512
        gate_ref[:, j * 512:(j + 1) * 512] = jnp.dot(
            h, w_ref[:, c1:c1 + 512], preferred_element_type=F32).astype(BF16)


def _ret_inproj(x, pos, g, w_in_bf16):
    n = x.shape[0]
    half = RET_DK // 2
    inv = (ROPE_BASE ** (-jnp.arange(half, dtype=F32) / half)).reshape(1, half)
    tm = TM_PROJ
    return pl.pallas_call(
        _ret_inproj_kernel,
        grid=(n // tm,),
        in_specs=[
            pl.BlockSpec((tm, D_MODEL), lambda i: (i, 0)),
            pl.BlockSpec((tm, 1), lambda i: (i, 0)),
            pl.BlockSpec((1, D_MODEL), lambda i: (0, 0)),
            pl.BlockSpec((1, half), lambda i: (0, 0)),
            pl.BlockSpec(w_in_bf16.shape, lambda i: (0, 0)),
        ],
        out_specs=[
            pl.BlockSpec((tm, RET_QK), lambda i: (i, 0)),
            pl.BlockSpec((tm, RET_QK), lambda i: (i, 0)),
            pl.BlockSpec((tm, RET_V), lambda i: (i, 0)),
            pl.BlockSpec((tm, RET_V), lambda i: (i, 0)),
        ],
        out_shape=[
            jax.ShapeDtypeStruct((n, RET_QK), BF16),
            jax.ShapeDtypeStruct((n, RET_QK), BF16),
            jax.ShapeDtypeStruct((n, RET_V), BF16),
            jax.ShapeDtypeStruct((n, RET_V), BF16),
        ],
        compiler_params=_cparams(("arbitrary",)),
        name="ret_inproj",
    )(x, pos, g, inv, w_in_bf16)


def _ret_core_kernel(cdec_ref, q_ref, k_ref, v_ref, gate_ref, x_ref, hg_ref, intra_ref, cross_ref,
                     kdec_ref, wo_ref, out_ref, state_ref):
    @pl.when(pl.program_id(0) == 0)
    def _():
        state_ref[...] = jnp.zeros_like(state_ref)

    acc = x_ref[...]
    for hd in range(RET_HEADS):
        q = q_ref[:, hd * RET_DK:(hd + 1) * RET_DK]
        k = k_ref[:, hd * RET_DK:(hd + 1) * RET_DK]
        v = v_ref[:, hd * RET_DV:(hd + 1) * RET_DV]
        state = state_ref[hd]
        scores = lax.dot_general(q, k, (((1,), (1,)), ((), ())), preferred_element_type=F32)
        scores = (scores * intra_ref[hd]).astype(BF16)
        o = jnp.dot(scores, v, preferred_element_type=F32)
        cross = cross_ref[hd]
        o_cross = jnp.dot(q, state.astype(BF16), preferred_element_type=F32)
        o = o + o_cross * jnp.concatenate([cross] * (RET_DV // 128), axis=1)
        kdec = kdec_ref[hd]
        kd = (k.astype(F32) * jnp.concatenate([kdec] * (RET_DK // 128), axis=1)).astype(BF16)
        upd = lax.dot_general(kd, v, (((0,), (0,)), ((), ())), preferred_element_type=F32)
        state_ref[hd] = state * cdec_ref[hd] + upd
        ms = jnp.mean(o * o, axis=-1, keepdims=True)
        on = o * lax.rsqrt(ms + NORM_EPS) * hg_ref[:, hd * RET_DV:(hd + 1) * RET_DV]
        gt = gate_ref[:, hd * RET_DV:(hd + 1) * RET_DV].astype(F32)
        y = (_silu(gt) * on).astype(BF16)
        acc = acc + jnp.dot(y, wo_ref[hd * RET_DV:(hd + 1) * RET_DV, :], preferred_element_type=F32)
    out_ref[...] = acc


def _ret_core(q, k, v, gate, x, head_g, w_out_bf16):
    n = x.shape[0]
    c = RET_C
    log_gamma = jnp.log1p(-(2.0 ** (-5.0 - jnp.arange(RET_HEADS, dtype=F32))))
    idx = jnp.arange(c, dtype=F32)
    diff = idx[:, None] - idx[None, :]
    intra = jnp.where(diff >= 0, jnp.exp(log_gamma[:, None, None] * jnp.maximum(diff, 0.0)), 0.0)
    cross = jnp.broadcast_to(jnp.exp(log_gamma[:, None] * (idx + 1.0))[:, :, None], (RET_HEADS, c, 128))
    kdec = jnp.broadcast_to(jnp.exp(log_gamma[:, None] * (c - 1.0 - idx))[:, :, None], (RET_HEADS, c, 128))
    cdec = jnp.exp(log_gamma * c)
    return pl.pallas_call(
        _ret_core_kernel,
        grid=(n // c,),
        in_specs=[
            pl.BlockSpec(memory_space=pltpu.SMEM),
            pl.BlockSpec((c, RET_QK), lambda i: (i, 0)),
            pl.BlockSpec((c, RET_QK), lambda i: (i, 0)),
            pl.BlockSpec((c, RET_V), lambda i: (i, 0)),
            pl.BlockSpec((c, RET_V), lambda i: (i, 0)),
            pl.BlockSpec((c, D_MODEL), lambda i: (i, 0)),
            pl.BlockSpec((1, RET_V), lambda i: (0, 0)),
            pl.BlockSpec((RET_HEADS, c, c), lambda i: (0, 0, 0)),
            pl.BlockSpec((RET_HEADS, c, 128), lambda i: (0, 0, 0)),
            pl.BlockSpec((RET_HEADS, c, 128), lambda i: (0, 0, 0)),
            pl.BlockSpec((RET_V, D_MODEL), lambda i: (0, 0)),
        ],
        out_specs=pl.BlockSpec((c, D_MODEL), lambda i: (i, 0)),
        out_shape=jax.ShapeDtypeStruct((n, D_MODEL), F32),
        scratch_shapes=[pltpu.VMEM((RET_HEADS, RET_DK, RET_DV), F32)],
        compiler_params=_cparams(("arbitrary",)),
        name="ret_core",
    )(cdec, q, k, v, gate, x, head_g, intra, cross, kdec, w_out_bf16)


def _conv_pw1_kernel(x_ref, g_ref, w_ref, b_ref, u_ref):
    h = _rms(x_ref[...], g_ref[...]).astype(BF16)
    for j in range(D_MODEL // 512):
        a = jnp.dot(h, w_ref[:, j * 512:(j + 1) * 512], preferred_element_type=F32)
        a = a + b_ref[:, j * 512:(j + 1) * 512]
        gt = jnp.dot(h, w_ref[:, D_MODEL + j * 512:D_MODEL + (j + 1) * 512], preferred_element_type=F32)
        gt = gt + b_ref[:, D_MODEL + j * 512:D_MODEL + (j + 1) * 512]
        u_ref[:, j * 512:(j + 1) * 512] = a * (1.0 / (1.0 + jnp.exp(-gt)))


def _conv_pw1(x, g, w_bf16, b):
    n = x.shape[0]
    tm = TM_CONV
    return pl.pallas_call(
        _conv_pw1_kernel,
        grid=(n // tm,),
        in_specs=[
            pl.BlockSpec((tm, D_MODEL), lambda i: (i, 0)),
            pl.BlockSpec((1, D_MODEL), lambda i: (0, 0)),
            pl.BlockSpec((D_MODEL, 2 * D_MODEL), lambda i: (0, 0)),
            pl.BlockSpec((1, 2 * D_MODEL), lambda i: (0, 0)),
        ],
        out_specs=pl.BlockSpec((tm, D_MODEL), lambda i: (i, 0)),
        out_shape=jax.ShapeDtypeStruct((n, D_MODEL), F32),
        compiler_params=_cparams(("arbitrary",)),
        name="conv_pw1",
    )(x, g, w_bf16, b)


def _conv_core_kernel(u_ref, halo_ref, x_ref, wdw_ref, bdw_ref, lng_ref, lnb_ref, w2_ref, b2_ref,
                      out_ref, win_ref, z_ref):
    tm = TM_CONV
    first = pl.program_id(0) == 0
    halo = halo_ref[...]
    win_ref[0:CONV_HALO, :] = jnp.where(first, jnp.zeros_like(halo), halo)
    win_ref[CONV_HALO:CONV_HALO + tm, :] = u_ref[...]
    off = CONV_HALO - (CONV_WIDTH - 1)
    for cc in range(D_MODEL // 128):
        cs = slice(cc * 128, (cc + 1) * 128)
        acc = jnp.zeros((tm, 128), F32)
        for w in range(CONV_WIDTH):
            acc = acc + win_ref[off + w:off + w + tm, cs] * wdw_ref[w:w + 1, cs]
        z_ref[:, cs] = acc + bdw_ref[:, cs]
    z = z_ref[...]
    mu = jnp.mean(z, axis=-1, keepdims=True)
    zc = z - mu
    var = jnp.mean(zc * zc, axis=-1, keepdims=True)
    zn = zc * lax.rsqrt(var + NORM_EPS) * lng_ref[...] + lnb_ref[...]
    y = _silu(zn).astype(BF16)
    out_ref[...] = x_ref[...] + jnp.dot(y, w2_ref[...], preferred_element_type=F32) + b2_ref[...]


def _conv_core(u, x, w_dw, b_dw, ln_g, ln_b, w2_bf16, b2):
    n = x.shape[0]
    tm = TM_CONV
    r = tm // CONV_HALO
    wdw_pad = jnp.zeros((32, D_MODEL), F32).at[:CONV_WIDTH].set(w_dw)
    return pl.pallas_call(
        _conv_core_kernel,
        grid=(n // tm,),
        in_specs=[
            pl.BlockSpec((tm, D_MODEL), lambda i: (i, 0)),
            pl.BlockSpec((CONV_HALO, D_MODEL), lambda i: (jnp.maximum(i * r - 1, 0), 0)),
            pl.BlockSpec((tm, D_MODEL), lambda i: (i, 0)),
            pl.BlockSpec((32, D_MODEL), lambda i: (0, 0)),
            pl.BlockSpec((1, D_MODEL), lambda i: (0, 0)),
            pl.BlockSpec((1, D_MODEL), lambda i: (0, 0)),
            pl.BlockSpec((1, D_MODEL), lambda i: (0, 0)),
            pl.BlockSpec((D_MODEL, D_MODEL), lambda i: (0, 0)),
            pl.BlockSpec((1, D_MODEL), lambda i: (0, 0)),
        ],
        out_specs=pl.BlockSpec((tm, D_MODEL), lambda i: (i, 0)),
        out_shape=jax.ShapeDtypeStruct((n, D_MODEL), F32),
        scratch_shapes=[pltpu.VMEM((CONV_HALO + tm, D_MODEL), F32), pltpu.VMEM((tm, D_MODEL), F32)],
        compiler_params=_cparams(("arbitrary",)),
        name="conv_core",
    )(u, u, x, wdw_pad, b_dw, ln_g, ln_b, w2_bf16, b2)


def _router_kernel(x_ref, g_ref, wr_ref, br_ref, hp_ref, ri_ref, rw_ref, cnt_ref, carry_ref):
    t = T_ROUTE

    @pl.when(pl.program_id(0) == 0)
    def _():
        carry_ref[...] = jnp.zeros_like(carry_ref)

    h = _rms(x_ref[...], g_ref[...])
    hp_ref[...] = _pack_bf16_pairs(h)
    logits = lax.dot_general(wr_ref[...], h, (((1,), (1,)), ((), ())),
                             precision=lax.Precision.HIGHEST, preferred_element_type=F32)
    logits = logits + br_ref[:, 0:1]

    best = logits[0:1]
    gi = jnp.zeros((1, t), I32)
    for j in range(1, MOE_GROUPS):
        r = logits[j:j + 1]
        up = r > best
        gi = jnp.where(up, j, gi)
        best = jnp.where(up, r, best)
    den = jnp.zeros((1, t), F32)
    for j in range(MOE_GROUPS):
        den = den + jnp.exp(logits[j:j + 1] - best)
    gate_g = 1.0 / den

    sel = logits[8:8 + MOE_EPG]
    for j in range(1, MOE_GROUPS):
        sel = jnp.where(gi == j, logits[8 + j * MOE_EPG:8 + (j + 1) * MOE_EPG], sel)

    m1 = sel[0:1]
    i1 = jnp.zeros((1, t), I32)
    for j in range(1, MOE_EPG):
        r = sel[j:j + 1]
        up = r > m1
        i1 = jnp.where(up, j, i1)
        m1 = jnp.where(up, r, m1)
    m2 = jnp.full((1, t), -jnp.inf, F32)
    i2 = jnp.zeros((1, t), I32)
    started = jnp.zeros((1, t), jnp.bool_)
    for j in range(MOE_EPG):
        r = sel[j:j + 1]
        ok = i1 != j
        up = ok & ((r > m2) | jnp.logical_not(started))
        i2 = jnp.where(up, j, i2)
        m2 = jnp.where(up, r, m2)
        started = started | ok
    e21 = jnp.exp(m2 - m1)
    p1 = 1.0 / (1.0 + e21)
    w1 = gate_g * p1
    w2 = gate_g * (e21 * p1)
    eid1 = gi * MOE_EPG + i1
    eid2 = gi * MOE_EPG + i2

    eio = lax.broadcasted_iota(I32, (MOE_EXPERTS, t), 0)
    oh1 = eio == eid1
    oh2 = eio == eid2
    oh = (oh1 | oh2).astype(F32)
    rio = lax.broadcasted_iota(I32, (t, t), 0)
    cio = lax.broadcasted_iota(I32, (t, t), 1)
    upper = (rio < cio).astype(BF16)
    cum = jnp.dot(oh.astype(BF16), upper, preferred_element_type=F32) + carry_ref[:, 0:1]
    rank1 = jnp.sum(jnp.where(oh1, cum, 0.0), axis=0, keepdims=True)
    rank2 = jnp.sum(jnp.where(oh2, cum, 0.0), axis=0, keepdims=True)
    carry_ref[...] = carry_ref[...] + jnp.sum(oh, axis=1, keepdims=True)
    cnt_ref[...] = carry_ref[...]

    zi = jnp.zeros((4, t), I32)
    ri_ref[...] = jnp.concatenate([eid1, eid2, rank1.astype(I32), rank2.astype(I32), zi], axis=0)
    zf = jnp.zeros((6, t), F32)
    rw_ref[...] = jnp.concatenate([w1, w2, zf], axis=0)


def _router(x, g, w_rg, b_rg, w_re, b_re):
    n = x.shape[0]
    t = T_ROUTE
    wr = jnp.zeros((40, D_MODEL), F32).at[0:MOE_GROUPS].set(w_rg.T).at[8:40].set(w_re.T)
    br = jnp.zeros((40,), F32).at[0:MOE_GROUPS].set(b_rg).at[8:40].set(b_re)
    br = jnp.broadcast_to(br[:, None], (40, 128))
    return pl.pallas_call(
        _router_kernel,
        grid=(n // t,),
        in_specs=[
            pl.BlockSpec((t, D_MODEL), lambda i: (i, 0)),
            pl.BlockSpec((1, D_MODEL), lambda i: (0, 0)),
            pl.BlockSpec((40, D_MODEL), lambda i: (0, 0)),
            pl.BlockSpec((40, 128), lambda i: (0, 0)),
        ],
        out_specs=[
            pl.BlockSpec((t, HALF), lambda i: (i, 0)),
            pl.BlockSpec((8, t), lambda i: (0, i)),
            pl.BlockSpec((8, t), lambda i: (0, i)),
            pl.BlockSpec((MOE_EXPERTS, 128), lambda i: (0, 0)),
        ],
        out_shape=[
            jax.ShapeDtypeStruct((n, HALF), U32),
            jax.ShapeDtypeStruct((8, n), I32),
            jax.ShapeDtypeStruct((8, n), F32),
            jax.ShapeDtypeStruct((MOE_EXPERTS, 128), F32),
        ],
        scratch_shapes=[pltpu.VMEM((MOE_EXPERTS, 128), F32)],
        compiler_params=_cparams(("arbitrary",)),
        name="moe_router",
    )(x, g, wr, br)


def _sc_mesh():
    return plsc.VectorSubcoreMesh(core_axis_name="c", subcore_axis_name="s",
                                  num_cores=SC_CORES, num_subcores=SC_SUBCORES)


def _sc_worker_id():
    return lax.axis_index("s") * SC_CORES + lax.axis_index("c")


def _sc_dispatch(hp, idx, zero_rows, total_rows):
    n = hp.shape[0]
    tpw = n // SC_WORKERS
    kd = tpw // SC_ROWS
    kp = idx.shape[1] - 2 * kd

    @functools.partial(
        pl.kernel, mesh=_sc_mesh(),
        out_type=jax.ShapeDtypeStruct((total_rows, HALF), U32),
        scratch_types=[pltpu.VMEM((2 * kd + kp, SC_ROWS), I32), pltpu.VMEM((SC_ROWS, HALF), U32)],
        name="moe_dispatch_sc",
    )
    def k(hp_hbm, idx_hbm, zero_hbm, xs_hbm, idx_v, rows_v):
        wid = _sc_worker_id()
        pltpu.sync_copy(idx_hbm.at[wid], idx_v)
        pltpu.sync_copy(zero_hbm, rows_v)
        for j in range(kp):
            pltpu.sync_copy(rows_v, xs_hbm.at[idx_v.at[2 * kd + j]])
        for c in range(kd):
            pltpu.sync_copy(hp_hbm.at[pl.ds(wid * tpw + c * SC_ROWS, SC_ROWS)], rows_v)
            pltpu.sync_copy(rows_v, xs_hbm.at[idx_v.at[c]])
            pltpu.sync_copy(rows_v, xs_hbm.at[idx_v.at[kd + c]])

    return k(hp, idx, zero_rows)


def _sc_gather(ys, idx):
    kg = idx.shape[1]
    rows_per_worker = kg * SC_ROWS

    @functools.partial(
        pl.kernel, mesh=_sc_mesh(),
        out_type=jax.ShapeDtypeStruct((SC_WORKERS * rows_per_worker, HALF), U32),
        scratch_types=[pltpu.VMEM((kg, SC_ROWS), I32), pltpu.VMEM((SC_ROWS, HALF), U32)],
        name="moe_gather_sc",
    )
    def k(ys_hbm, idx_hbm, yg_hbm, idx_v, rows_v):
        wid = _sc_worker_id()
        pltpu.sync_copy(idx_hbm.at[wid], idx_v)
        for c in range(kg):
            pltpu.sync_copy(ys_hbm.at[idx_v.at[c]], rows_v)
            pltpu.sync_copy(rows_v, yg_hbm.at[pl.ds(wid * rows_per_worker + c * SC_ROWS, SC_ROWS)])

    return k(ys, idx)


def _expert_kernel(blk_e_ref, nused_ref, xs_ref, wg_ref, wu_ref, wd_ref, ys_ref, wgb_ref, wub_ref, wdb_ref):
    i = pl.program_id(0)

    @pl.when(i < nused_ref[0])
    def _():
        prev = blk_e_ref[jnp.maximum(i - 1, 0)]
        changed = jnp.logical_or(i == 0, blk_e_ref[i] != prev)

        @pl.when(changed)
        def _():
            wgb_ref[...] = wg_ref[0, 0].astype(BF16)
            wub_ref[...] = wu_ref[0, 0].astype(BF16)
            wdb_ref[...] = wd_ref[0, 0].astype(BF16)

        lo, hi = _unpack_bf16_pairs(xs_ref[...])
        xb = jnp.concatenate([lo.astype(BF16), hi.astype(BF16)], axis=1)
        a = jnp.dot(xb, wgb_ref[...], preferred_element_type=F32)
        b = jnp.dot(xb, wub_ref[...], preferred_element_type=F32)
        hm = (_silu(a) * b).astype(BF16)
        y = jnp.dot(hm, wdb_ref[...], preferred_element_type=F32)
        ys_ref[...] = _pack_bf16_pairs(y)

    @pl.when(i >= nused_ref[0])
    def _():
        ys_ref[...] = jnp.zeros_like(ys_ref)


def _experts(xs, blk_e, nused, w_gate, w_up, w_down, layer):
    tm = TM_EXP
    p_rows = xs.shape[0] - MOE_EXPERTS * tm
    nblk = p_rows // tm

    def blk(i, be, nu):
        return jnp.minimum(i, nu[0] - 1)

    grid_spec = pltpu.PrefetchScalarGridSpec(
        num_scalar_prefetch=2,
        grid=(nblk,),
        in_specs=[
            pl.BlockSpec((tm, HALF), lambda i, be, nu: (blk(i, be, nu), 0)),
            pl.BlockSpec((1, 1, D_MODEL, MOE_FF), lambda i, be, nu: (layer, be[blk(i, be, nu)], 0, 0)),
            pl.BlockSpec((1, 1, D_MODEL, MOE_FF), lambda i, be, nu: (layer, be[blk(i, be, nu)], 0, 0)),
            pl.BlockSpec((1, 1, MOE_FF, D_MODEL), lambda i, be, nu: (layer, be[blk(i, be, nu)], 0, 0)),
        ],
        out_specs=pl.BlockSpec((tm, HALF), lambda i, be, nu: (i, 0)),
        scratch_shapes=[
            pltpu.VMEM((D_MODEL, MOE_FF), BF16),
            pltpu.VMEM((D_MODEL, MOE_FF), BF16),
            pltpu.VMEM((MOE_FF, D_MODEL), BF16),
        ],
    )
    return pl.pallas_call(
        _expert_kernel,
        grid_spec=grid_spec,
        out_shape=jax.ShapeDtypeStruct((p_rows, HALF), U32),
        compiler_params=_cparams(("arbitrary",)),
        name="moe_experts",
    )(blk_e, nused, xs, w_gate, w_up, w_down)


def _combine_kernel(x_ref, w1_ref, w2_ref, fg_ref, y1_ref, y2_ref, out_ref, *, final_norm):
    lo1, hi1 = _unpack_bf16_pairs(y1_ref[...])
    lo2, hi2 = _unpack_bf16_pairs(y2_ref[...])
    w1 = w1_ref[...]
    w2 = w2_ref[...]
    x = x_ref[...]
    o_lo = x[:, :HALF] + w1 * lo1 + w2 * lo2
    o_hi = x[:, HALF:] + w1 * hi1 + w2 * hi2
    if final_norm:
        ms = (jnp.sum(o_lo * o_lo, axis=-1, keepdims=True)
              + jnp.sum(o_hi * o_hi, axis=-1, keepdims=True)) * (1.0 / D_MODEL)
        sc = lax.rsqrt(ms + NORM_EPS)
        o_lo = o_lo * sc * fg_ref[:, :HALF]
        o_hi = o_hi * sc * fg_ref[:, HALF:]
    out_ref[:, :HALF] = o_lo
    out_ref[:, HALF:] = o_hi


def _combine(yg, x, w1, w2, final_g, final_norm):
    n = x.shape[0]
    td = T_COMB
    nb = n // td
    return pl.pallas_call(
        functools.partial(_combine_kernel, final_norm=final_norm),
        grid=(nb,),
        in_specs=[
            pl.BlockSpec((td, D_MODEL), lambda i: (i, 0)),
            pl.BlockSpec((td, 1), lambda i: (i, 0)),
            pl.BlockSpec((td, 1), lambda i: (i, 0)),
            pl.BlockSpec((1, D_MODEL), lambda i: (0, 0)),
            pl.BlockSpec((td, HALF), lambda i: (i, 0)),
            pl.BlockSpec((td, HALF), lambda i: (i + nb, 0)),
        ],
        out_specs=pl.BlockSpec((td, D_MODEL), lambda i: (i, 0)),
        out_shape=jax.ShapeDtypeStruct((n, D_MODEL), F32),
        compiler_params=_cparams(("arbitrary",)),
        name="moe_combine",
    )(x, w1, w2, final_g, yg, yg)


def _moe(x, g, w_rg, b_rg, w_re, b_re, w_gate, w_up, w_down, layer, final_g, final_norm):
    n = x.shape[0]
    tm = TM_EXP
    p_rows = 2 * n + MOE_EXPERTS * tm
    nblk = p_rows // tm
    hp, ri, rw, cnt = _router(x, g, w_rg, b_rg, w_re, b_re)
    counts = cnt[:, 0].astype(I32)
    pcounts = (counts + tm - 1) // tm * tm
    pend = jnp.cumsum(pcounts)
    pstart = pend - pcounts
    eio = jnp.arange(MOE_EXPERTS, dtype=I32)[:, None]
    dest1 = jnp.sum(jnp.where(ri[0][None, :] == eio, pstart[:, None], 0), axis=0) + ri[2]
    dest2 = jnp.sum(jnp.where(ri[1][None, :] == eio, pstart[:, None], 0), axis=0) + ri[3]
    blk_start = jnp.arange(nblk, dtype=I32) * tm
    blk_e = jnp.minimum(jnp.sum((pend[None, :] <= blk_start[:, None]).astype(I32), axis=1), MOE_EXPERTS - 1)
    nused = jnp.maximum(pend[-1] // tm, 1).astype(I32).reshape(1)
    r = jnp.arange(tm, dtype=I32)[None, :]
    pad_slot = jnp.where(r < (pcounts - counts)[:, None], (pstart + counts)[:, None] + r, p_rows + eio * tm + r)
    kd = n // SC_WORKERS // SC_ROWS
    idx = jnp.concatenate([dest1.reshape(SC_WORKERS, kd, SC_ROWS), dest2.reshape(SC_WORKERS, kd, SC_ROWS),
                           pad_slot.reshape(SC_WORKERS, -1, SC_ROWS)], axis=1)
    zero_rows = jnp.zeros((SC_ROWS, HALF), U32)
    xs = _sc_dispatch(hp, idx, zero_rows, p_rows + MOE_EXPERTS * tm)
    ys = _experts(xs, blk_e, nused, w_gate, w_up, w_down, layer)
    gidx = jnp.concatenate([dest1, dest2]).reshape(SC_WORKERS, -1, SC_ROWS)
    yg = _sc_gather(ys, gidx)
    return _combine(yg, x, rw[0].reshape(n, 1), rw[1].reshape(n, 1), final_g, final_norm)


def kernel(x, positions, norm_mix_g, norm_ffn_g, ret_w_in, ret_head_g, ret_w_out, conv_w_pw1, conv_b_pw1, conv_w_dw, conv_b_dw, conv_ln_g, conv_ln_b, conv_w_pw2, conv_b_pw2, moe_w_rg, moe_b_rg, moe_w_re, moe_b_re, moe_w_gate, moe_w_up, moe_w_down, final_norm_g):
    b, s, d = x.shape
    n = b * s
    xt = x.reshape(n, d)
    pos = positions.reshape(n, 1)
    fg = final_norm_g.reshape(1, d)

    q, k, v, gate = _ret_inproj(xt, pos, norm_mix_g[0].reshape(1, d), ret_w_in[0].astype(BF16))
    xt = _ret_core(q, k, v, gate, xt, ret_head_g[0].reshape(1, RET_V), ret_w_out[0].astype(BF16))
    xt = _moe(xt, norm_ffn_g[0].reshape(1, d), moe_w_rg[0], moe_b_rg[0], moe_w_re[0], moe_b_re[0],
              moe_w_gate, moe_w_up, moe_w_down, 0, fg, False)

    u = _conv_pw1(xt, norm_mix_g[1].reshape(1, d), conv_w_pw1[0].astype(BF16), conv_b_pw1[0].reshape(1, 2 * d))
    xt = _conv_core(u, xt, conv_w_dw[0], conv_b_dw[0].reshape(1, d), conv_ln_g[0].reshape(1, d),
                    conv_ln_b[0].reshape(1, d), conv_w_pw2[0].astype(BF16), conv_b_pw2[0].reshape(1, d))
    xt = _moe(xt, norm_ffn_g[1].reshape(1, d), moe_w_rg[1], moe_b_rg[1], moe_w_re[1], moe_b_re[1],
              moe_w_gate, moe_w_up, moe_w_down, 1, fg, True)
    return xt.reshape(b, s, d)
```

```python
import functools
import math

import jax
import jax.numpy as jnp
import numpy as np
from jax import lax
from jax.experimental import pallas as pl
from jax.experimental.pallas import tpu as pltpu
from jax.experimental.pallas import tpu_sc as plsc

F32 = jnp.float32
BF16 = jnp.bfloat16
U32 = jnp.uint32
I32 = jnp.int32

D_MODEL = 1024
RET_HEADS = 4
RET_DK = 256
RET_DV = 512
RET_QK = RET_HEADS * RET_DK
RET_V = RET_HEADS * RET_DV
ROPE_BASE = 10000.0
CONV_WIDTH = 31
MOE_GROUPS = 4
MOE_EPG = 8
MOE_EXPERTS = MOE_GROUPS * MOE_EPG
MOE_FF = 512
NORM_EPS = 1e-6

TM_PROJ = 256
RET_C = 256
TM_CONV = 256
CONV_HALO = 32
CONV_ROWS = 128
T_ROUTE = 512
TM_EXP = 256
T_COMB = 256
SC_CORES = 2
SC_SUBCORES = 16
SC_WORKERS = SC_CORES * SC_SUBCORES
SC_ROWS = 128
HALF = D_MODEL // 2

VMEM_LIMIT = 56 * 1024 * 1024


def _cparams(sem):
    return pltpu.CompilerParams(dimension_semantics=sem, vmem_limit_bytes=VMEM_LIMIT)


def _rms(x, g):
    ms = jnp.mean(x * x, axis=-1, keepdims=True)
    return x * lax.rsqrt(ms + NORM_EPS) * g


def _silu(x):
    return x * (1.0 / (1.0 + jnp.exp(-x)))


def _pack_bf16_pairs(y):
    lo = pltpu.bitcast(y[:, :HALF].astype(BF16).astype(F32), U32)
    hi = pltpu.bitcast(y[:, HALF:].astype(BF16).astype(F32), U32)
    return (hi & jnp.uint32(0xFFFF0000)) | (lo >> 16)


def _unpack_bf16_pairs(p):
    lo = pltpu.bitcast(p << 16, F32)
    hi = pltpu.bitcast(p & jnp.uint32(0xFFFF0000), F32)
    return lo, hi


def _ret_inproj_kernel(x_ref, pos_ref, g_ref, inv_ref, w_ref, q_ref, k_ref, v_ref, gate_ref):
    h = _rms(x_ref[...], g_ref[...]).astype(BF16)
    ang = pos_ref[...].astype(F32) * inv_ref[...]
    cos = jnp.cos(ang)
    sin = jnp.sin(ang)
    half = RET_DK // 2
    for hd in range(RET_HEADS):
        for base, out_ref, scale in ((0, q_ref, 1.0), (RET_QK, k_ref, RET_DK ** -0.5)):
            c0 = base + hd * RET_DK
            t = jnp.dot(h, w_ref[:, c0:c0 + RET_DK], preferred_element_type=F32)
            t1 = t[:, :half]
            t2 = t[:, half:]
            o1 = (t1 * cos - t2 * sin) * scale
            o2 = (t1 * sin + t2 * cos) * scale
            out_ref[:, hd * RET_DK:hd * RET_DK + half] = o1.astype(BF16)
            out_ref[:, hd * RET_DK + half:(hd + 1) * RET_DK] = o2.astype(BF16)
    for j in range(RET_V // 512):
        c0 = 2 * RET_QK + j * 512
        v_ref[:, j * 512:(j + 1) * 512] = jnp.dot(
            h, w_ref[:, c0:c0 + 512], preferred_element_type=F32).astype(BF16)
        c1 = 2 * RET_QK + RET_V + j * 512
        gate_ref[:, j * 512:(j + 1) * 512] = jnp.dot(
            h, w_ref[:, c1:c1 + 512], preferred_element_type=F32).astype(BF16)


def _ret_inproj(x, pos, g, w_in_bf16):
    n = x.shape[0]
    half = RET_DK // 2
    inv = (ROPE_BASE ** (-jnp.arange(half, dtype=F32) / half)).reshape(1, half)
    tm = TM_PROJ
    return pl.pallas_call(
        _ret_inproj_kernel,
        grid=(n // tm,),
        in_specs=[
            pl.BlockSpec((tm, D_MODEL), lambda i: (i, 0)),
            pl.BlockSpec((tm, 1), lambda i: (i, 0)),
            pl.BlockSpec((1, D_MODEL), lambda i: (0, 0)),
            pl.BlockSpec((1, half), lambda i: (0, 0)),
            pl.BlockSpec(w_in_bf16.shape, lambda i: (0, 0)),
        ],
        out_specs=[
            pl.BlockSpec((tm, RET_QK), lambda i: (i, 0)),
            pl.BlockSpec((tm, RET_QK), lambda i: (i, 0)),
            pl.BlockSpec((tm, RET_V), lambda i: (i, 0)),
            pl.BlockSpec((tm, RET_V), lambda i: (i, 0)),
        ],
        out_shape=[
            jax.ShapeDtypeStruct((n, RET_QK), BF16),
            jax.ShapeDtypeStruct((n, RET_QK), BF16),
            jax.ShapeDtypeStruct((n, RET_V), BF16),
            jax.ShapeDtypeStruct((n, RET_V), BF16),
        ],
        compiler_params=_cparams(("arbitrary",)),
        name="ret_inproj",
    )(x, pos, g, inv, w_in_bf16)


def _ret_core_kernel(cdec_ref, q_ref, k_ref, v_ref, gate_ref, x_ref, hg_ref, intra_ref, cross_ref,
                     kdec_ref, wo_ref, out_ref, state_ref):
    @pl.when(pl.program_id(0) == 0)
    def _():
        state_ref[...] = jnp.zeros_like(state_ref)

    acc = x_ref[...]
    for hd in range(RET_HEADS):
        q = q_ref[:, hd * RET_DK:(hd + 1) * RET_DK]
        k = k_ref[:, hd * RET_DK:(hd + 1) * RET_DK]
        v = v_ref[:, hd * RET_DV:(hd + 1) * RET_DV]
        state = state_ref[hd]
        scores = lax.dot_general(q, k, (((1,), (1,)), ((), ())), preferred_element_type=F32)
        scores = (scores * intra_ref[hd]).astype(BF16)
        o = jnp.dot(scores, v, preferred_element_type=F32)
        cross = cross_ref[hd]
        o_cross = jnp.dot(q, state.astype(BF16), preferred_element_type=F32)
        o = o + o_cross * jnp.concatenate([cross] * (RET_DV // 128), axis=1)
        kdec = kdec_ref[hd]
        kd = (k.astype(F32) * jnp.concatenate([kdec] * (RET_DK // 128), axis=1)).astype(BF16)
        upd = lax.dot_general(kd, v, (((0,), (0,)), ((), ())), preferred_element_type=F32)
        state_ref[hd] = state * cdec_ref[hd] + upd
        ms = jnp.mean(o * o, axis=-1, keepdims=True)
        on = o * lax.rsqrt(ms + NORM_EPS) * hg_ref[:, hd * RET_DV:(hd + 1) * RET_DV]
        gt = gate_ref[:, hd * RET_DV:(hd + 1) * RET_DV].astype(F32)
        y = (_silu(gt) * on).astype(BF16)
        acc = acc + jnp.dot(y, wo_ref[hd * RET_DV:(hd + 1) * RET_DV, :], preferred_element_type=F32)
    out_ref[...] = acc


def _ret_core(q, k, v, gate, x, head_g, w_out_bf16):
    n = x.shape[0]
    c = RET_C
    log_gamma = jnp.log1p(-(2.0 ** (-5.0 - jnp.arange(RET_HEADS, dtype=F32))))
    idx = jnp.arange(c, dtype=F32)
    diff = idx[:, None] - idx[None, :]
    intra = jnp.where(diff >= 0, jnp.exp(log_gamma[:, None, None] * jnp.maximum(diff, 0.0)), 0.0)
    cross = jnp.broadcast_to(jnp.exp(log_gamma[:, None] * (idx + 1.0))[:, :, None], (RET_HEADS, c, 128))
    kdec = jnp.broadcast_to(jnp.exp(log_gamma[:, None] * (c - 1.0 - idx))[:, :, None], (RET_HEADS, c, 128))
    cdec = jnp.exp(log_gamma * c)
    return pl.pallas_call(
        _ret_core_kernel,
        grid=(n // c,),
        in_specs=[
            pl.BlockSpec(memory_space=pltpu.SMEM),
            pl.BlockSpec((c, RET_QK), lambda i: (i, 0)),
            pl.BlockSpec((c, RET_QK), lambda i: (i, 0)),
            pl.BlockSpec((c, RET_V), lambda i: (i, 0)),
            pl.BlockSpec((c, RET_V), lambda i: (i, 0)),
            pl.BlockSpec((c, D_MODEL), lambda i: (i, 0)),
            pl.BlockSpec((1, RET_V), lambda i: (0, 0)),
            pl.BlockSpec((RET_HEADS, c, c), lambda i: (0, 0, 0)),
            pl.BlockSpec((RET_HEADS, c, 128), lambda i: (0, 0, 0)),
            pl.BlockSpec((RET_HEADS, c, 128), lambda i: (0, 0, 0)),
            pl.BlockSpec((RET_V, D_MODEL), lambda i: (0, 0)),
        ],
        out_specs=pl.BlockSpec((c, D_MODEL), lambda i: (i, 0)),
        out_shape=jax.ShapeDtypeStruct((n, D_MODEL), F32),
        scratch_shapes=[pltpu.VMEM((RET_HEADS, RET_DK, RET_DV), F32)],
        compiler_params=_cparams(("arbitrary",)),
        name="ret_core",
    )(cdec, q, k, v, gate, x, head_g, intra, cross, kdec, w_out_bf16)


def _conv_pw1_kernel(x_ref, g_ref, w_ref, b_ref, u_ref):
    h = _rms(x_ref[...], g_ref[...]).astype(BF16)
    for j in range(D_MODEL // 512):
        a = jnp.dot(h, w_ref[:, j * 512:(j + 1) * 512], preferred_element_type=F32)
        a = a + b_ref[:, j * 512:(j + 1) * 512]
        gt = jnp.dot(h, w_ref[:, D_MODEL + j * 512:D_MODEL + (j + 1) * 512], preferred_element_type=F32)
        gt = gt + b_ref[:, D_MODEL + j * 512:D_MODEL + (j + 1) * 512]
        u_ref[:, j * 512:(j + 1) * 512] = a * (1.0 / (1.0 + jnp.exp(-gt)))


def _conv_pw1(x, g, w_bf16, b):
    n = x.shape[0]
    tm = TM_CONV
    return pl.pallas_call(
        _conv_pw1_kernel,
        grid=(n // tm,),
        in_specs=[
            pl.BlockSpec((tm, D_MODEL), lambda i: (i, 0)),
            pl.BlockSpec((1, D_MODEL), lambda i: (0, 0)),
            pl.BlockSpec((D_MODEL, 2 * D_MODEL), lambda i: (0, 0)),
            pl.BlockSpec((1, 2 * D_MODEL), lambda i: (0, 0)),
        ],
        out_specs=pl.BlockSpec((tm, D_MODEL), lambda i: (i, 0)),
        out_shape=jax.ShapeDtypeStruct((n, D_MODEL), F32),
        compiler_params=_cparams(("arbitrary",)),
        name="conv_pw1",
    )(x, g, w_bf16, b)


def _conv_core_kernel(u_ref, halo_ref, x_ref, wdw_ref, bdw_ref, lng_ref, lnb_ref, w2_ref, b2_ref,
                      out_ref, win_ref, z_ref):
    tm = TM_CONV
    first = pl.program_id(0) == 0
    halo = halo_ref[...]
    win_ref[0:CONV_HALO, :] = jnp.where(first, jnp.zeros_like(halo), halo)
    win_ref[CONV_HALO:CONV_HALO + tm, :] = u_ref[...]
    off = CONV_HALO - (CONV_WIDTH - 1)
    rb = CONV_ROWS
    for cc in range(D_MODEL // 128):
        cs = slice(cc * 128, (cc + 1) * 128)
        for r0 in range(0, tm, rb):
            z = bdw_ref[:, cs]
            for b in range(8):
                rows = rb if b == 0 else rb + 8
                q = None
                for o in range(off, off + CONV_WIDTH):
                    if o % 8 != b:
                        continue
                    term = win_ref[r0 + o - b:r0 + o - b + rows, cs] * wdw_ref[o - off:o - off + 1, cs]
                    q = term if q is None else q + term
                z = z + (q if b == 0 else q[b:b + rb, :])
            z_ref[r0:r0 + rb, cs] = z
    z = z_ref[...]
    mu = jnp.mean(z, axis=-1, keepdims=True)
    zc = z - mu
    var = jnp.mean(zc * zc, axis=-1, keepdims=True)
    zn = zc * lax.rsqrt(var + NORM_EPS) * lng_ref[...] + lnb_ref[...]
    y = _silu(zn).astype(BF16)
    out_ref[...] = x_ref[...] + jnp.dot(y, w2_ref[...], preferred_element_type=F32) + b2_ref[...]


def _conv_core(u, x, w_dw, b_dw, ln_g, ln_b, w2_bf16, b2):
    n = x.shape[0]
    tm = TM_CONV
    r = tm // CONV_HALO
    wdw_pad = jnp.zeros((32, D_MODEL), F32).at[:CONV_WIDTH].set(w_dw)
    return pl.pallas_call(
        _conv_core_kernel,
        grid=(n // tm,),
        in_specs=[
            pl.BlockSpec((tm, D_MODEL), lambda i: (i, 0)),
            pl.BlockSpec((CONV_HALO, D_MODEL), lambda i: (jnp.maximum(i * r - 1, 0), 0)),
            pl.BlockSpec((tm, D_MODEL), lambda i: (i, 0)),
            pl.BlockSpec((32, D_MODEL), lambda i: (0, 0)),
            pl.BlockSpec((1, D_MODEL), lambda i: (0, 0)),
            pl.BlockSpec((1, D_MODEL), lambda i: (0, 0)),
            pl.BlockSpec((1, D_MODEL), lambda i: (0, 0)),
            pl.BlockSpec((D_MODEL, D_MODEL), lambda i: (0, 0)),
            pl.BlockSpec((1, D_MODEL), lambda i: (0, 0)),
        ],
        out_specs=pl.BlockSpec((tm, D_MODEL), lambda i: (i, 0)),
        out_shape=jax.ShapeDtypeStruct((n, D_MODEL), F32),
        scratch_shapes=[pltpu.VMEM((CONV_HALO + tm, D_MODEL), F32), pltpu.VMEM((tm, D_MODEL), F32)],
        compiler_params=_cparams(("arbitrary",)),
        name="conv_core",
    )(u, u, x, wdw_pad, b_dw, ln_g, ln_b, w2_bf16, b2)


def _router_kernel(x_ref, g_ref, wr_ref, br_ref, hp_ref, ri_ref, rw_ref, cnt_ref, carry_ref):
    t = T_ROUTE

    @pl.when(pl.program_id(0) == 0)
    def _():
        carry_ref[...] = jnp.zeros_like(carry_ref)

    h = _rms(x_ref[...], g_ref[...])
    hp_ref[...] = _pack_bf16_pairs(h)
    logits = lax.dot_general(wr_ref[...], h, (((1,), (1,)), ((), ())),
                             precision=lax.Precision.HIGHEST, preferred_element_type=F32)
    logits = logits + br_ref[:, 0:1]

    best = logits[0:1]
    gi = jnp.zeros((1, t), I32)
    for j in range(1, MOE_GROUPS):
        r = logits[j:j + 1]
        up = r > best
        gi = jnp.where(up, j, gi)
        best = jnp.where(up, r, best)
    den = jnp.zeros((1, t), F32)
    for j in range(MOE_GROUPS):
        den = den + jnp.exp(logits[j:j + 1] - best)
    gate_g = 1.0 / den

    sel = logits[8:8 + MOE_EPG]
    for j in range(1, MOE_GROUPS):
        sel = jnp.where(gi == j, logits[8 + j * MOE_EPG:8 + (j + 1) * MOE_EPG], sel)

    m1 = sel[0:1]
    i1 = jnp.zeros((1, t), I32)
    for j in range(1, MOE_EPG):
        r = sel[j:j + 1]
        up = r > m1
        i1 = jnp.where(up, j, i1)
        m1 = jnp.where(up, r, m1)
    m2 = jnp.full((1, t), -jnp.inf, F32)
    i2 = jnp.zeros((1, t), I32)
    started = jnp.zeros((1, t), jnp.bool_)
    for j in range(MOE_EPG):
        r = sel[j:j + 1]
        ok = i1 != j
        up = ok & ((r > m2) | jnp.logical_not(started))
        i2 = jnp.where(up, j, i2)
        m2 = jnp.where(up, r, m2)
        started = started | ok
    e21 = jnp.exp(m2 - m1)
    p1 = 1.0 / (1.0 + e21)
    w1 = gate_g * p1
    w2 = gate_g * (e21 * p1)
    eid1 = gi * MOE_EPG + i1
    eid2 = gi * MOE_EPG + i2

    eio = lax.broadcasted_iota(I32, (MOE_EXPERTS, t), 0)
    oh1 = eio == eid1
    oh2 = eio == eid2
    oh = (oh1 | oh2).astype(F32)
    rio = lax.broadcasted_iota(I32, (t, t), 0)
    cio = lax.broadcasted_iota(I32, (t, t), 1)
    upper = (rio < cio).astype(BF16)
    cum = jnp.dot(oh.astype(BF16), upper, preferred_element_type=F32) + carry_ref[:, 0:1]
    rank1 = jnp.sum(jnp.where(oh1, cum, 0.0), axis=0, keepdims=True)
    rank2 = jnp.sum(jnp.where(oh2, cum, 0.0), axis=0, keepdims=True)
    carry_ref[...] = carry_ref[...] + jnp.sum(oh, axis=1, keepdims=True)
    cnt_ref[...] = carry_ref[...]

    zi = jnp.zeros((4, t), I32)
    ri_ref[...] = jnp.concatenate([eid1, eid2, rank1.astype(I32), rank2.astype(I32), zi], axis=0)
    zf = jnp.zeros((6, t), F32)
    rw_ref[...] = jnp.concatenate([w1, w2, zf], axis=0)


def _router(x, g, w_rg, b_rg, w_re, b_re):
    n = x.shape[0]
    t = T_ROUTE
    wr = jnp.zeros((40, D_MODEL), F32).at[0:MOE_GROUPS].set(w_rg.T).at[8:40].set(w_re.T)
    br = jnp.zeros((40,), F32).at[0:MOE_GROUPS].set(b_rg).at[8:40].set(b_re)
    br = jnp.broadcast_to(br[:, None], (40, 128))
    return pl.pallas_call(
        _router_kernel,
        grid=(n // t,),
        in_specs=[
            pl.BlockSpec((t, D_MODEL), lambda i: (i, 0)),
            pl.BlockSpec((1, D_MODEL), lambda i: (0, 0)),
            pl.BlockSpec((40, D_MODEL), lambda i: (0, 0)),
            pl.BlockSpec((40, 128), lambda i: (0, 0)),
        ],
        out_specs=[
            pl.BlockSpec((t, HALF), lambda i: (i, 0)),
            pl.BlockSpec((8, t), lambda i: (0, i)),
            pl.BlockSpec((8, t), lambda i: (0, i)),
            pl.BlockSpec((MOE_EXPERTS, 128), lambda i: (0, 0)),
        ],
        out_shape=[
            jax.ShapeDtypeStruct((n, HALF), U32),
            jax.ShapeDtypeStruct((8, n), I32),
            jax.ShapeDtypeStruct((8, n), F32),
            jax.ShapeDtypeStruct((MOE_EXPERTS, 128), F32),
        ],
        scratch_shapes=[pltpu.VMEM((MOE_EXPERTS, 128), F32)],
        compiler_params=_cparams(("arbitrary",)),
        name="moe_router",
    )(x, g, wr, br)


def _sc_mesh():
    return plsc.VectorSubcoreMesh(core_axis_name="c", subcore_axis_name="s",
                                  num_cores=SC_CORES, num_subcores=SC_SUBCORES)


def _sc_worker_id():
    return lax.axis_index("s") * SC_CORES + lax.axis_index("c")


def _sc_dispatch(hp, idx, zero_rows, total_rows):
    n = hp.shape[0]
    tpw = n // SC_WORKERS
    kd = tpw // SC_ROWS
    kp = idx.shape[1] - 2 * kd

    @functools.partial(
        pl.kernel, mesh=_sc_mesh(),
        out_type=jax.ShapeDtypeStruct((total_rows, HALF), U32),
        scratch_types=[pltpu.VMEM((2 * kd + kp, SC_ROWS), I32), pltpu.VMEM((SC_ROWS, HALF), U32)],
        name="moe_dispatch_sc",
    )
    def k(hp_hbm, idx_hbm, zero_hbm, xs_hbm, idx_v, rows_v):
        wid = _sc_worker_id()
        pltpu.sync_copy(idx_hbm.at[wid], idx_v)
        pltpu.sync_copy(zero_hbm, rows_v)
        for j in range(kp):
            pltpu.sync_copy(rows_v, xs_hbm.at[idx_v.at[2 * kd + j]])
        for c in range(kd):
            pltpu.sync_copy(hp_hbm.at[pl.ds(wid * tpw + c * SC_ROWS, SC_ROWS)], rows_v)
            pltpu.sync_copy(rows_v, xs_hbm.at[idx_v.at[c]])
            pltpu.sync_copy(rows_v, xs_hbm.at[idx_v.at[kd + c]])

    return k(hp, idx, zero_rows)


def _sc_gather(ys, idx):
    kg = idx.shape[1]
    rows_per_worker = kg * SC_ROWS

    @functools.partial(
        pl.kernel, mesh=_sc_mesh(),
        out_type=jax.ShapeDtypeStruct((SC_WORKERS * rows_per_worker, HALF), U32),
        scratch_types=[pltpu.VMEM((kg, SC_ROWS), I32), pltpu.VMEM((SC_ROWS, HALF), U32)],
        name="moe_gather_sc",
    )
    def k(ys_hbm, idx_hbm, yg_hbm, idx_v, rows_v):
        wid = _sc_worker_id()
        pltpu.sync_copy(idx_hbm.at[wid], idx_v)
        for c in range(kg):
            pltpu.sync_copy(ys_hbm.at[idx_v.at[c]], rows_v)
            pltpu.sync_copy(rows_v, yg_hbm.at[pl.ds(wid * rows_per_worker + c * SC_ROWS, SC_ROWS)])

    return k(ys, idx)


def _expert_kernel(blk_e_ref, nused_ref, xs_ref, wg_ref, wu_ref, wd_ref, ys_ref):
    i = pl.program_id(0)

    @pl.when(i < nused_ref[0])
    def _():
        lo, hi = _unpack_bf16_pairs(xs_ref[...])
        xf = jnp.concatenate([lo, hi], axis=1)
        a = jnp.dot(xf, wg_ref[0, 0], preferred_element_type=F32)
        b = jnp.dot(xf, wu_ref[0, 0], preferred_element_type=F32)
        hm = _silu(a) * b
        y = jnp.dot(hm, wd_ref[0, 0], preferred_element_type=F32)
        ys_ref[...] = _pack_bf16_pairs(y)

    @pl.when(i >= nused_ref[0])
    def _():
        ys_ref[...] = jnp.zeros_like(ys_ref)


def _experts(xs, blk_e, nused, w_gate, w_up, w_down, layer):
    tm = TM_EXP
    p_rows = xs.shape[0] - MOE_EXPERTS * tm
    nblk = p_rows // tm

    def blk(i, be, nu):
        return jnp.minimum(i, nu[0] - 1)

    grid_spec = pltpu.PrefetchScalarGridSpec(
        num_scalar_prefetch=2,
        grid=(nblk,),
        in_specs=[
            pl.BlockSpec((tm, HALF), lambda i, be, nu: (blk(i, be, nu), 0)),
            pl.BlockSpec((1, 1, D_MODEL, MOE_FF), lambda i, be, nu: (layer, be[blk(i, be, nu)], 0, 0)),
            pl.BlockSpec((1, 1, D_MODEL, MOE_FF), lambda i, be, nu: (layer, be[blk(i, be, nu)], 0, 0)),
            pl.BlockSpec((1, 1, MOE_FF, D_MODEL), lambda i, be, nu: (layer, be[blk(i, be, nu)], 0, 0)),
        ],
        out_specs=pl.BlockSpec((tm, HALF), lambda i, be, nu: (i, 0)),
    )
    return pl.pallas_call(
        _expert_kernel,
        grid_spec=grid_spec,
        out_shape=jax.ShapeDtypeStruct((p_rows, HALF), U32),
        compiler_params=_cparams(("arbitrary",)),
        name="moe_experts",
    )(blk_e, nused, xs, w_gate, w_up, w_down)


def _combine_kernel(x_ref, w1_ref, w2_ref, fg_ref, y1_ref, y2_ref, out_ref, *, final_norm):
    lo1, hi1 = _unpack_bf16_pairs(y1_ref[...])
    lo2, hi2 = _unpack_bf16_pairs(y2_ref[...])
    w1 = w1_ref[...]
    w2 = w2_ref[...]
    x = x_ref[...]
    o_lo = x[:, :HALF] + w1 * lo1 + w2 * lo2
    o_hi = x[:, HALF:] + w1 * hi1 + w2 * hi2
    if final_norm:
        ms = (jnp.sum(o_lo * o_lo, axis=-1, keepdims=True)
              + jnp.sum(o_hi * o_hi, axis=-1, keepdims=True)) * (1.0 / D_MODEL)
        sc = lax.rsqrt(ms + NORM_EPS)
        o_lo = o_lo * sc * fg_ref[:, :HALF]
        o_hi = o_hi * sc * fg_ref[:, HALF:]
    out_ref[:, :HALF] = o_lo
    out_ref[:, HALF:] = o_hi


def _combine(yg, x, w1, w2, final_g, final_norm):
    n = x.shape[0]
    td = T_COMB
    nb = n // td
    return pl.pallas_call(
        functools.partial(_combine_kernel, final_norm=final_norm),
        grid=(nb,),
        in_specs=[
            pl.BlockSpec((td, D_MODEL), lambda i: (i, 0)),
            pl.BlockSpec((td, 1), lambda i: (i, 0)),
            pl.BlockSpec((td, 1), lambda i: (i, 0)),
            pl.BlockSpec((1, D_MODEL), lambda i: (0, 0)),
            pl.BlockSpec((td, HALF), lambda i: (i, 0)),
            pl.BlockSpec((td, HALF), lambda i: (i + nb, 0)),
        ],
        out_specs=pl.BlockSpec((td, D_MODEL), lambda i: (i, 0)),
        out_shape=jax.ShapeDtypeStruct((n, D_MODEL), F32),
        compiler_params=_cparams(("arbitrary",)),
        name="moe_combine",
    )(x, w1, w2, final_g, yg, yg)


def _moe(x, g, w_rg, b_rg, w_re, b_re, w_gate, w_up, w_down, layer, final_g, final_norm):
    n = x.shape[0]
    tm = TM_EXP
    p_rows = 2 * n + MOE_EXPERTS * tm
    nblk = p_rows // tm
    hp, ri, rw, cnt = _router(x, g, w_rg, b_rg, w_re, b_re)
    counts = cnt[:, 0].astype(I32)
    pcounts = (counts + tm - 1) // tm * tm
    pend = jnp.cumsum(pcounts)
    pstart = pend - pcounts
    eio = jnp.arange(MOE_EXPERTS, dtype=I32)[:, None]
    dest1 = jnp.sum(jnp.where(ri[0][None, :] == eio, pstart[:, None], 0), axis=0) + ri[2]
    dest2 = jnp.sum(jnp.where(ri[1][None, :] == eio, pstart[:, None], 0), axis=0) + ri[3]
    blk_start = jnp.arange(nblk, dtype=I32) * tm
    blk_e = jnp.minimum(jnp.sum((pend[None, :] <= blk_start[:, None]).astype(I32), axis=1), MOE_EXPERTS - 1)
    nused = jnp.maximum(pend[-1] // tm, 1).astype(I32).reshape(1)
    r = jnp.arange(tm, dtype=I32)[None, :]
    pad_slot = jnp.where(r < (pcounts - counts)[:, None], (pstart + counts)[:, None] + r, p_rows + eio * tm + r)
    kd = n // SC_WORKERS // SC_ROWS
    idx = jnp.concatenate([dest1.reshape(SC_WORKERS, kd, SC_ROWS), dest2.reshape(SC_WORKERS, kd, SC_ROWS),
                           pad_slot.reshape(SC_WORKERS, -1, SC_ROWS)], axis=1)
    zero_rows = jnp.zeros((SC_ROWS, HALF), U32)
    xs = _sc_dispatch(hp, idx, zero_rows, p_rows + MOE_EXPERTS * tm)
    ys = _experts(xs, blk_e, nused, w_gate, w_up, w_down, layer)
    gidx = jnp.concatenate([dest1, dest2]).reshape(SC_WORKERS, -1, SC_ROWS)
    yg = _sc_gather(ys, gidx)
    return _combine(yg, x, rw[0].reshape(n, 1), rw[1].reshape(n, 1), final_g, final_norm)


def kernel(x, positions, norm_mix_g, norm_ffn_g, ret_w_in, ret_head_g, ret_w_out, conv_w_pw1, conv_b_pw1, conv_w_dw, conv_b_dw, conv_ln_g, conv_ln_b, conv_w_pw2, conv_b_pw2, moe_w_rg, moe_b_rg, moe_w_re, moe_b_re, moe_w_gate, moe_w_up, moe_w_down, final_norm_g):
    b, s, d = x.shape
    n = b * s
    xt = x.reshape(n, d)
    pos = positions.reshape(n, 1)
    fg = final_norm_g.reshape(1, d)

    q, k, v, gate = _ret_inproj(xt, pos, norm_mix_g[0].reshape(1, d), ret_w_in[0].astype(BF16))
    xt = _ret_core(q, k, v, gate, xt, ret_head_g[0].reshape(1, RET_V), ret_w_out[0].astype(BF16))
    xt = _moe(xt, norm_ffn_g[0].reshape(1, d), moe_w_rg[0], moe_b_rg[0], moe_w_re[0], moe_b_re[0],
              moe_w_gate, moe_w_up, moe_w_down, 0, fg, False)

    u = _conv_pw1(xt, norm_mix_g[1].reshape(1, d), conv_w_pw1[0].astype(BF16), conv_b_pw1[0].reshape(1, 2 * d))
    xt = _conv_core(u, xt, conv_w_dw[0], conv_b_dw[0].reshape(1, d), conv_ln_g[0].reshape(1, d),
                    conv_ln_b[0].reshape(1, d), conv_w_pw2[0].astype(BF16), conv_b_pw2[0].reshape(1, d))
    xt = _moe(xt, norm_ffn_g[1].reshape(1, d), moe_w_rg[1], moe_b_rg[1], moe_w_re[1], moe_b_re[1],
              moe_w_gate, moe_w_up, moe_w_down, 1, fg, True)
    return xt.reshape(b, s, d)
```

```python
import functools
import math

import jax
import jax.numpy as jnp
import numpy as np
from jax import lax
from jax.experimental import pallas as pl
from jax.experimental.pallas import tpu as pltpu
from jax.experimental.pallas import tpu_sc as plsc

F32 = jnp.float32
BF16 = jnp.bfloat16
U32 = jnp.uint32
I32 = jnp.int32

D_MODEL = 1024
RET_HEADS = 4
RET_DK = 256
RET_DV = 512
RET_QK = RET_HEADS * RET_DK
RET_V = RET_HEADS * RET_DV
ROPE_BASE = 10000.0
CONV_WIDTH = 31
MOE_GROUPS = 4
MOE_EPG = 8
MOE_EXPERTS = MOE_GROUPS * MOE_EPG
MOE_FF = 512
NORM_EPS = 1e-6

TM_PROJ = 256
RET_C = 256
TM_CONV = 256
CONV_HALO = 32
CONV_ROWS = 128
T_ROUTE = 512
TM_EXP = 256
T_COMB = 256
SC_CORES = 2
SC_SUBCORES = 16
SC_WORKERS = SC_CORES * SC_SUBCORES
SC_ROWS = 128
HALF = D_MODEL // 2

VMEM_LIMIT = 56 * 1024 * 1024


def _cparams(sem, flags=None):
    return pltpu.CompilerParams(dimension_semantics=sem, vmem_limit_bytes=VMEM_LIMIT, flags=flags)


def _rms(x, g):
    ms = jnp.mean(x * x, axis=-1, keepdims=True)
    return x * lax.rsqrt(ms + NORM_EPS) * g


def _silu(x):
    return x * (1.0 / (1.0 + jnp.exp(-x)))


def _pack_bf16_pairs(y):
    lo = pltpu.bitcast(y[:, :HALF].astype(BF16).astype(F32), U32)
    hi = pltpu.bitcast(y[:, HALF:].astype(BF16).astype(F32), U32)
    return (hi & jnp.uint32(0xFFFF0000)) | (lo >> 16)


def _unpack_bf16_pairs(p):
    lo = pltpu.bitcast(p << 16, F32)
    hi = pltpu.bitcast(p & jnp.uint32(0xFFFF0000), F32)
    return lo, hi


def _ret_inproj_kernel(x_ref, pos_ref, g_ref, inv_ref, w_ref, q_ref, k_ref, v_ref, gate_ref):
    h = _rms(x_ref[...], g_ref[...]).astype(BF16)
    ang = pos_ref[...].astype(F32) * inv_ref[...]
    cos = jnp.cos(ang)
    sin = jnp.sin(ang)
    half = RET_DK // 2
    for hd in range(RET_HEADS):
        for base, out_ref, scale in ((0, q_ref, 1.0), (RET_QK, k_ref, RET_DK ** -0.5)):
            c0 = base + hd * RET_DK
            t = jnp.dot(h, w_ref[:, c0:c0 + RET_DK], preferred_element_type=F32)
            t1 = t[:, :half]
            t2 = t[:, half:]
            o1 = (t1 * cos - t2 * sin) * scale
            o2 = (t1 * sin + t2 * cos) * scale
            out_ref[:, hd * RET_DK:hd * RET_DK + half] = o1.astype(BF16)
            out_ref[:, hd * RET_DK + half:(hd + 1) * RET_DK] = o2.astype(BF16)
    for j in range(RET_V // 512):
        c0 = 2 * RET_QK + j * 512
        v_ref[:, j * 512:(j + 1) * 512] = jnp.dot(
            h, w_ref[:, c0:c0 + 512], preferred_element_type=F32).astype(BF16)
        c1 = 2 * RET_QK + RET_V + j * 512
        gate_ref[:, j * 512:(j + 1) * 512] = jnp.dot(
            h, w_ref[:, c1:c1 + 512], preferred_element_type=F32).astype(BF16)


def _ret_inproj(x, pos, g, w_in_bf16):
    n = x.shape[0]
    half = RET_DK // 2
    inv = (ROPE_BASE ** (-jnp.arange(half, dtype=F32) / half)).reshape(1, half)
    tm = TM_PROJ
    return pl.pallas_call(
        _ret_inproj_kernel,
        grid=(n // tm,),
        in_specs=[
            pl.BlockSpec((tm, D_MODEL), lambda i: (i, 0)),
            pl.BlockSpec((tm, 1), lambda i: (i, 0)),
            pl.BlockSpec((1, D_MODEL), lambda i: (0, 0)),
            pl.BlockSpec((1, half), lambda i: (0, 0)),
            pl.BlockSpec(w_in_bf16.shape, lambda i: (0, 0)),
        ],
        out_specs=[
            pl.BlockSpec((tm, RET_QK), lambda i: (i, 0)),
            pl.BlockSpec((tm, RET_QK), lambda i: (i, 0)),
            pl.BlockSpec((tm, RET_V), lambda i: (i, 0)),
            pl.BlockSpec((tm, RET_V), lambda i: (i, 0)),
        ],
        out_shape=[
            jax.ShapeDtypeStruct((n, RET_QK), BF16),
            jax.ShapeDtypeStruct((n, RET_QK), BF16),
            jax.ShapeDtypeStruct((n, RET_V), BF16),
            jax.ShapeDtypeStruct((n, RET_V), BF16),
        ],
        compiler_params=_cparams(("arbitrary",)),
        name="ret_inproj",
    )(x, pos, g, inv, w_in_bf16)


def _ret_core_kernel(cdec_ref, q_ref, k_ref, v_ref, gate_ref, x_ref, hg_ref, intra_ref, cross_ref,
                     kdec_ref, wo_ref, out_ref, state_ref, y_ref, wos_ref):
    @pl.when(pl.program_id(0) == 0)
    def _():
        state_ref[...] = jnp.zeros_like(state_ref)
        wos_ref[...] = (wo_ref[...] * hg_ref[...]).astype(BF16)

    for hd in range(RET_HEADS):
        q = q_ref[:, hd * RET_DK:(hd + 1) * RET_DK]
        k = k_ref[:, hd * RET_DK:(hd + 1) * RET_DK]
        v = v_ref[:, hd * RET_DV:(hd + 1) * RET_DV]
        state = state_ref[hd]
        scores = lax.dot_general(q, k, (((1,), (1,)), ((), ())), preferred_element_type=F32)
        scores = (scores * intra_ref[hd]).astype(BF16)
        o = jnp.dot(scores, v, preferred_element_type=F32)
        cross = cross_ref[hd]
        o_cross = jnp.dot(q, state.astype(BF16), preferred_element_type=F32)
        o = o + o_cross * jnp.concatenate([cross] * (RET_DV // 128), axis=1)
        kdec = kdec_ref[hd]
        kd = (k.astype(F32) * jnp.concatenate([kdec] * (RET_DK // 128), axis=1)).astype(BF16)
        upd = lax.dot_general(kd, v, (((0,), (0,)), ((), ())), preferred_element_type=F32)
        state_ref[hd] = state * cdec_ref[hd] + upd
        ms = jnp.mean(o * o, axis=-1, keepdims=True)
        on = o * lax.rsqrt(ms + NORM_EPS)
        gt = gate_ref[:, hd * RET_DV:(hd + 1) * RET_DV].astype(F32)
        y_ref[:, hd * RET_DV:(hd + 1) * RET_DV] = (_silu(gt) * on).astype(BF16)
    out_ref[...] = x_ref[...] + jnp.dot(y_ref[...], wos_ref[...], preferred_element_type=F32)


def _ret_core(q, k, v, gate, x, head_g_col, w_out):
    n = x.shape[0]
    c = RET_C
    log_gamma = jnp.log1p(-(2.0 ** (-5.0 - jnp.arange(RET_HEADS, dtype=F32))))
    idx = jnp.arange(c, dtype=F32)
    diff = idx[:, None] - idx[None, :]
    intra = jnp.where(diff >= 0, jnp.exp(log_gamma[:, None, None] * jnp.maximum(diff, 0.0)), 0.0)
    cross = jnp.broadcast_to(jnp.exp(log_gamma[:, None] * (idx + 1.0))[:, :, None], (RET_HEADS, c, 128))
    kdec = jnp.broadcast_to(jnp.exp(log_gamma[:, None] * (c - 1.0 - idx))[:, :, None], (RET_HEADS, c, 128))
    cdec = jnp.exp(log_gamma * c)
    return pl.pallas_call(
        _ret_core_kernel,
        grid=(n // c,),
        in_specs=[
            pl.BlockSpec(memory_space=pltpu.SMEM),
            pl.BlockSpec((c, RET_QK), lambda i: (i, 0)),
            pl.BlockSpec((c, RET_QK), lambda i: (i, 0)),
            pl.BlockSpec((c, RET_V), lambda i: (i, 0)),
            pl.BlockSpec((c, RET_V), lambda i: (i, 0)),
            pl.BlockSpec((c, D_MODEL), lambda i: (i, 0)),
            pl.BlockSpec((RET_V, 1), lambda i: (0, 0)),
            pl.BlockSpec((RET_HEADS, c, c), lambda i: (0, 0, 0)),
            pl.BlockSpec((RET_HEADS, c, 128), lambda i: (0, 0, 0)),
            pl.BlockSpec((RET_HEADS, c, 128), lambda i: (0, 0, 0)),
            pl.BlockSpec((RET_V, D_MODEL), lambda i: (0, 0)),
        ],
        out_specs=pl.BlockSpec((c, D_MODEL), lambda i: (i, 0)),
        out_shape=jax.ShapeDtypeStruct((n, D_MODEL), F32),
        scratch_shapes=[
            pltpu.VMEM((RET_HEADS, RET_DK, RET_DV), F32),
            pltpu.VMEM((c, RET_V), BF16),
            pltpu.VMEM((RET_V, D_MODEL), BF16),
        ],
        compiler_params=_cparams(("arbitrary",)),
        name="ret_core",
    )(cdec, q, k, v, gate, x, head_g_col, intra, cross, kdec, w_out)


def _conv_pw1_kernel(x_ref, g_ref, w_ref, b_ref, u_ref):
    h = _rms(x_ref[...], g_ref[...]).astype(BF16)
    for j in range(D_MODEL // 512):
        a = jnp.dot(h, w_ref[:, j * 512:(j + 1) * 512], preferred_element_type=F32)
        a = a + b_ref[:, j * 512:(j + 1) * 512]
        gt = jnp.dot(h, w_ref[:, D_MODEL + j * 512:D_MODEL + (j + 1) * 512], preferred_element_type=F32)
        gt = gt + b_ref[:, D_MODEL + j * 512:D_MODEL + (j + 1) * 512]
        u_ref[:, j * 512:(j + 1) * 512] = a * (1.0 / (1.0 + jnp.exp(-gt)))


def _conv_pw1(x, g, w_bf16, b):
    n = x.shape[0]
    tm = TM_CONV
    return pl.pallas_call(
        _conv_pw1_kernel,
        grid=(n // tm,),
        in_specs=[
            pl.BlockSpec((tm, D_MODEL), lambda i: (i, 0)),
            pl.BlockSpec((1, D_MODEL), lambda i: (0, 0)),
            pl.BlockSpec((D_MODEL, 2 * D_MODEL), lambda i: (0, 0)),
            pl.BlockSpec((1, 2 * D_MODEL), lambda i: (0, 0)),
        ],
        out_specs=pl.BlockSpec((tm, D_MODEL), lambda i: (i, 0)),
        out_shape=jax.ShapeDtypeStruct((n, D_MODEL), F32),
        compiler_params=_cparams(("arbitrary",)),
        name="conv_pw1",
    )(x, g, w_bf16, b)


def _conv_core_kernel(u_ref, halo_ref, x_ref, wdw_ref, bdw_ref, lng_ref, lnb_ref, w2_ref, b2_ref,
                      out_ref, win_ref, z_ref):
    tm = TM_CONV
    first = pl.program_id(0) == 0
    halo = halo_ref[...]
    win_ref[0:CONV_HALO, :] = jnp.where(first, jnp.zeros_like(halo), halo)
    win_ref[CONV_HALO:CONV_HALO + tm, :] = u_ref[...]
    off = CONV_HALO - (CONV_WIDTH - 1)
    rb = CONV_ROWS
    for cc in range(D_MODEL // 128):
        cs = slice(cc * 128, (cc + 1) * 128)
        for r0 in range(0, tm, rb):
            z = bdw_ref[:, cs]
            for b in range(8):
                rows = rb if b == 0 else rb + 8
                q = None
                for o in range(off, off + CONV_WIDTH):
                    if o % 8 != b:
                        continue
                    term = win_ref[r0 + o - b:r0 + o - b + rows, cs] * wdw_ref[o - off:o - off + 1, cs]
                    q = term if q is None else q + term
                z = z + (q if b == 0 else q[b:b + rb, :])
            z_ref[r0:r0 + rb, cs] = z
    z = z_ref[...]
    mu = jnp.mean(z, axis=-1, keepdims=True)
    zc = z - mu
    var = jnp.mean(zc * zc, axis=-1, keepdims=True)
    zn = zc * lax.rsqrt(var + NORM_EPS) * lng_ref[...] + lnb_ref[...]
    y = _silu(zn).astype(BF16)
    out_ref[...] = x_ref[...] + jnp.dot(y, w2_ref[...], preferred_element_type=F32) + b2_ref[...]


def _conv_core(u, x, w_dw, b_dw, ln_g, ln_b, w2_bf16, b2):
    n = x.shape[0]
    tm = TM_CONV
    r = tm // CONV_HALO
    wdw_pad = jnp.zeros((32, D_MODEL), F32).at[:CONV_WIDTH].set(w_dw)
    return pl.pallas_call(
        _conv_core_kernel,
        grid=(n // tm,),
        in_specs=[
            pl.BlockSpec((tm, D_MODEL), lambda i: (i, 0)),
            pl.BlockSpec((CONV_HALO, D_MODEL), lambda i: (jnp.maximum(i * r - 1, 0), 0)),
            pl.BlockSpec((tm, D_MODEL), lambda i: (i, 0)),
            pl.BlockSpec((32, D_MODEL), lambda i: (0, 0)),
            pl.BlockSpec((1, D_MODEL), lambda i: (0, 0)),
            pl.BlockSpec((1, D_MODEL), lambda i: (0, 0)),
            pl.BlockSpec((1, D_MODEL), lambda i: (0, 0)),
            pl.BlockSpec((D_MODEL, D_MODEL), lambda i: (0, 0)),
            pl.BlockSpec((1, D_MODEL), lambda i: (0, 0)),
        ],
        out_specs=pl.BlockSpec((tm, D_MODEL), lambda i: (i, 0)),
        out_shape=jax.ShapeDtypeStruct((n, D_MODEL), F32),
        scratch_shapes=[pltpu.VMEM((CONV_HALO + tm, D_MODEL), F32), pltpu.VMEM((tm, D_MODEL), F32)],
        compiler_params=_cparams(("arbitrary",)),
        name="conv_core",
    )(u, u, x, wdw_pad, b_dw, ln_g, ln_b, w2_bf16, b2)


def _router_kernel(x_ref, g_ref, wr_ref, br_ref, hp_ref, ri_ref, rw_ref, cnt_ref, carry_ref):
    t = T_ROUTE

    @pl.when(pl.program_id(0) == 0)
    def _():
        carry_ref[...] = jnp.zeros_like(carry_ref)

    h = _rms(x_ref[...], g_ref[...])
    hp_ref[...] = _pack_bf16_pairs(h)
    logits = lax.dot_general(wr_ref[...], h, (((1,), (1,)), ((), ())),
                             precision=lax.Precision.HIGHEST, preferred_element_type=F32)
    logits = logits + br_ref[:, 0:1]

    best = logits[0:1]
    gi = jnp.zeros((1, t), I32)
    for j in range(1, MOE_GROUPS):
        r = logits[j:j + 1]
        up = r > best
        gi = jnp.where(up, j, gi)
        best = jnp.where(up, r, best)
    den = jnp.zeros((1, t), F32)
    for j in range(MOE_GROUPS):
        den = den + jnp.exp(logits[j:j + 1] - best)
    gate_g = 1.0 / den

    sel = logits[8:8 + MOE_EPG]
    for j in range(1, MOE_GROUPS):
        sel = jnp.where(gi == j, logits[8 + j * MOE_EPG:8 + (j + 1) * MOE_EPG], sel)

    m1 = sel[0:1]
    i1 = jnp.zeros((1, t), I32)
    for j in range(1, MOE_EPG):
        r = sel[j:j + 1]
        up = r > m1
        i1 = jnp.where(up, j, i1)
        m1 = jnp.where(up, r, m1)
    m2 = jnp.full((1, t), -jnp.inf, F32)
    i2 = jnp.zeros((1, t), I32)
    started = jnp.zeros((1, t), jnp.bool_)
    for j in range(MOE_EPG):
        r = sel[j:j + 1]
        ok = i1 != j
        up = ok & ((r > m2) | jnp.logical_not(started))
        i2 = jnp.where(up, j, i2)
        m2 = jnp.where(up, r, m2)
        started = started | ok
    e21 = jnp.exp(m2 - m1)
    p1 = 1.0 / (1.0 + e21)
    w1 = gate_g * p1
    w2 = gate_g * (e21 * p1)
    eid1 = gi * MOE_EPG + i1
    eid2 = gi * MOE_EPG + i2

    eio = lax.broadcasted_iota(I32, (MOE_EXPERTS, t), 0)
    oh1 = eio == eid1
    oh2 = eio == eid2
    oh = (oh1 | oh2).astype(F32)
    rio = lax.broadcasted_iota(I32, (t, t), 0)
    cio = lax.broadcasted_iota(I32, (t, t), 1)
    upper = (rio < cio).astype(BF16)
    cum = jnp.dot(oh.astype(BF16), upper, preferred_element_type=F32) + carry_ref[:, 0:1]
    rank1 = jnp.sum(jnp.where(oh1, cum, 0.0), axis=0, keepdims=True)
    rank2 = jnp.sum(jnp.where(oh2, cum, 0.0), axis=0, keepdims=True)
    carry_ref[...] = carry_ref[...] + jnp.sum(oh, axis=1, keepdims=True)
    cnt_ref[...] = carry_ref[...]

    zi = jnp.zeros((4, t), I32)
    ri_ref[...] = jnp.concatenate([eid1, eid2, rank1.astype(I32), rank2.astype(I32), zi], axis=0)
    zf = jnp.zeros((6, t), F32)
    rw_ref[...] = jnp.concatenate([w1, w2, zf], axis=0)


def _router(x, g, w_rg, b_rg, w_re, b_re):
    n = x.shape[0]
    t = T_ROUTE
    wr = jnp.zeros((40, D_MODEL), F32).at[0:MOE_GROUPS].set(w_rg.T).at[8:40].set(w_re.T)
    br = jnp.zeros((40,), F32).at[0:MOE_GROUPS].set(b_rg).at[8:40].set(b_re)
    br = jnp.broadcast_to(br[:, None], (40, 128))
    return pl.pallas_call(
        _router_kernel,
        grid=(n // t,),
        in_specs=[
            pl.BlockSpec((t, D_MODEL), lambda i: (i, 0)),
            pl.BlockSpec((1, D_MODEL), lambda i: (0, 0)),
            pl.BlockSpec((40, D_MODEL), lambda i: (0, 0)),
            pl.BlockSpec((40, 128), lambda i: (0, 0)),
        ],
        out_specs=[
            pl.BlockSpec((t, HALF), lambda i: (i, 0)),
            pl.BlockSpec((8, t), lambda i: (0, i)),
            pl.BlockSpec((8, t), lambda i: (0, i)),
            pl.BlockSpec((MOE_EXPERTS, 128), lambda i: (0, 0)),
        ],
        out_shape=[
            jax.ShapeDtypeStruct((n, HALF), U32),
            jax.ShapeDtypeStruct((8, n), I32),
            jax.ShapeDtypeStruct((8, n), F32),
            jax.ShapeDtypeStruct((MOE_EXPERTS, 128), F32),
        ],
        scratch_shapes=[pltpu.VMEM((MOE_EXPERTS, 128), F32)],
        compiler_params=_cparams(("arbitrary",)),
        name="moe_router",
    )(x, g, wr, br)


def _sc_mesh():
    return plsc.VectorSubcoreMesh(core_axis_name="c", subcore_axis_name="s",
                                  num_cores=SC_CORES, num_subcores=SC_SUBCORES)


def _sc_worker_id():
    return lax.axis_index("s") * SC_CORES + lax.axis_index("c")


def _sc_dispatch(hp, idx, zero_rows, total_rows):
    n = hp.shape[0]
    tpw = n // SC_WORKERS
    kd = tpw // SC_ROWS
    kp = idx.shape[1] - 2 * kd

    @functools.partial(
        pl.kernel, mesh=_sc_mesh(),
        out_type=jax.ShapeDtypeStruct((total_rows, HALF), U32),
        scratch_types=[pltpu.VMEM((2 * kd + kp, SC_ROWS), I32), pltpu.VMEM((SC_ROWS, HALF), U32)],
        name="moe_dispatch_sc",
    )
    def k(hp_hbm, idx_hbm, zero_hbm, xs_hbm, idx_v, rows_v):
        wid = _sc_worker_id()
        pltpu.sync_copy(idx_hbm.at[wid], idx_v)
        pltpu.sync_copy(zero_hbm, rows_v)
        for j in range(kp):
            pltpu.sync_copy(rows_v, xs_hbm.at[idx_v.at[2 * kd + j]])
        for c in range(kd):
            pltpu.sync_copy(hp_hbm.at[pl.ds(wid * tpw + c * SC_ROWS, SC_ROWS)], rows_v)
            pltpu.sync_copy(rows_v, xs_hbm.at[idx_v.at[c]])
            pltpu.sync_copy(rows_v, xs_hbm.at[idx_v.at[kd + c]])

    return k(hp, idx, zero_rows)


def _sc_gather(ys, idx):
    kg = idx.shape[1]
    rows_per_worker = kg * SC_ROWS

    @functools.partial(
        pl.kernel, mesh=_sc_mesh(),
        out_type=jax.ShapeDtypeStruct((SC_WORKERS * rows_per_worker, HALF), U32),
        scratch_types=[pltpu.VMEM((kg, SC_ROWS), I32), pltpu.VMEM((SC_ROWS, HALF), U32)],
        name="moe_gather_sc",
    )
    def k(ys_hbm, idx_hbm, yg_hbm, idx_v, rows_v):
        wid = _sc_worker_id()
        pltpu.sync_copy(idx_hbm.at[wid], idx_v)
        for c in range(kg):
            pltpu.sync_copy(ys_hbm.at[idx_v.at[c]], rows_v)
            pltpu.sync_copy(rows_v, yg_hbm.at[pl.ds(wid * rows_per_worker + c * SC_ROWS, SC_ROWS)])

    return k(ys, idx)


def _expert_kernel(blk_e_ref, nused_ref, first_ref, slot_ref, nxt_ref, xs_ref, wg_hbm, wu_hbm, wd_hbm,
                   ys_ref, wg_buf, wu_buf, wd_buf, sems, *, layer):
    i = pl.program_id(0)

    def weight_copies(e, s):
        return (pltpu.make_async_copy(wg_hbm.at[layer, e], wg_buf.at[s], sems.at[s, 0]),
                pltpu.make_async_copy(wu_hbm.at[layer, e], wu_buf.at[s], sems.at[s, 1]),
                pltpu.make_async_copy(wd_hbm.at[layer, e], wd_buf.at[s], sems.at[s, 2]))

    @pl.when(i < nused_ref[0])
    def _():
        s = slot_ref[i]

        @pl.when(i == 0)
        def _():
            for c in weight_copies(blk_e_ref[0], 0):
                c.start()

        @pl.when(first_ref[i] == 1)
        def _():
            for c in weight_copies(blk_e_ref[i], s):
                c.wait()

            @pl.when(nxt_ref[i] >= 0)
            def _():
                for c in weight_copies(nxt_ref[i], 1 - s):
                    c.start()

        lo, hi = _unpack_bf16_pairs(xs_ref[...])
        xf = jnp.concatenate([lo, hi], axis=1)
        a = jnp.dot(xf, wg_buf[s], preferred_element_type=F32)
        b = jnp.dot(xf, wu_buf[s], preferred_element_type=F32)
        hm = _silu(a) * b
        y = jnp.dot(hm, wd_buf[s], preferred_element_type=F32)
        ys_ref[...] = _pack_bf16_pairs(y)

    @pl.when(i >= nused_ref[0])
    def _():
        ys_ref[...] = jnp.zeros_like(ys_ref)


def _experts(xs, blk_e, nused, w_gate, w_up, w_down, layer):
    tm = TM_EXP
    p_rows = xs.shape[0] - MOE_EXPERTS * tm
    nblk = p_rows // tm
    pos = jnp.arange(nblk, dtype=I32)
    valid = pos < nused[0]
    prev_e = jnp.concatenate([jnp.full((1,), -1, I32), blk_e[:-1]])
    first = valid & (blk_e != prev_e)
    slot = (jnp.cumsum(first.astype(I32)) - 1) % 2
    first_pos = jnp.where(first, pos, nblk)
    next_first = jnp.concatenate([lax.cummin(first_pos, reverse=True)[1:], jnp.full((1,), nblk, I32)])
    nxt = jnp.where(next_first < nblk, blk_e[jnp.minimum(next_first, nblk - 1)], -1)

    def blk(i, be, nu, *_):
        return jnp.minimum(i, nu[0] - 1)

    grid_spec = pltpu.PrefetchScalarGridSpec(
        num_scalar_prefetch=5,
        grid=(nblk,),
        in_specs=[
            pl.BlockSpec((tm, HALF), lambda i, *sp: (blk(i, *sp), 0)),
            pl.BlockSpec(memory_space=pl.ANY),
            pl.BlockSpec(memory_space=pl.ANY),
            pl.BlockSpec(memory_space=pl.ANY),
        ],
        out_specs=pl.BlockSpec((tm, HALF), lambda i, *sp: (i, 0)),
        scratch_shapes=[
            pltpu.VMEM((2, D_MODEL, MOE_FF), F32),
            pltpu.VMEM((2, D_MODEL, MOE_FF), F32),
            pltpu.VMEM((2, MOE_FF, D_MODEL), F32),
            pltpu.SemaphoreType.DMA((2, 3)),
        ],
    )
    return pl.pallas_call(
        functools.partial(_expert_kernel, layer=layer),
        grid_spec=grid_spec,
        out_shape=jax.ShapeDtypeStruct((p_rows, HALF), U32),
        compiler_params=_cparams(("arbitrary",)),
        name="moe_experts",
    )(blk_e, nused, first.astype(I32), slot.astype(I32), nxt.astype(I32), xs, w_gate, w_up, w_down)


def _combine_kernel(x_ref, w1_ref, w2_ref, fg_ref, y1_ref, y2_ref, out_ref, *, final_norm):
    lo1, hi1 = _unpack_bf16_pairs(y1_ref[...])
    lo2, hi2 = _unpack_bf16_pairs(y2_ref[...])
    w1 = w1_ref[...]
    w2 = w2_ref[...]
    x = x_ref[...]
    o_lo = x[:, :HALF] + w1 * lo1 + w2 * lo2
    o_hi = x[:, HALF:] + w1 * hi1 + w2 * hi2
    if final_norm:
        ms = (jnp.sum(o_lo * o_lo, axis=-1, keepdims=True)
              + jnp.sum(o_hi * o_hi, axis=-1, keepdims=True)) * (1.0 / D_MODEL)
        sc = lax.rsqrt(ms + NORM_EPS)
        o_lo = o_lo * sc * fg_ref[:, :HALF]
        o_hi = o_hi * sc * fg_ref[:, HALF:]
    out_ref[:, :HALF] = o_lo
    out_ref[:, HALF:] = o_hi


def _combine(yg, x, w1, w2, final_g, final_norm):
    n = x.shape[0]
    td = T_COMB
    nb = n // td
    return pl.pallas_call(
        functools.partial(_combine_kernel, final_norm=final_norm),
        grid=(nb,),
        in_specs=[
            pl.BlockSpec((td, D_MODEL), lambda i: (i, 0)),
            pl.BlockSpec((td, 1), lambda i: (i, 0)),
            pl.BlockSpec((td, 1), lambda i: (i, 0)),
            pl.BlockSpec((1, D_MODEL), lambda i: (0, 0)),
            pl.BlockSpec((td, HALF), lambda i: (i, 0)),
            pl.BlockSpec((td, HALF), lambda i: (i + nb, 0)),
        ],
        out_specs=pl.BlockSpec((td, D_MODEL), lambda i: (i, 0)),
        out_shape=jax.ShapeDtypeStruct((n, D_MODEL), F32),
        compiler_params=_cparams(("arbitrary",)),
        name="moe_combine",
    )(x, w1, w2, final_g, yg, yg)


def _moe(x, g, w_rg, b_rg, w_re, b_re, w_gate, w_up, w_down, layer, final_g, final_norm):
    n = x.shape[0]
    tm = TM_EXP
    p_rows = 2 * n + MOE_EXPERTS * tm
    nblk = p_rows // tm
    hp, ri, rw, cnt = _router(x, g, w_rg, b_rg, w_re, b_re)
    counts = cnt[:, 0].astype(I32)
    pcounts = (counts + tm - 1) // tm * tm
    pend = jnp.cumsum(pcounts)
    pstart = pend - pcounts
    eio = jnp.arange(MOE_EXPERTS, dtype=I32)[:, None]
    dest1 = jnp.sum(jnp.where(ri[0][None, :] == eio, pstart[:, None], 0), axis=0) + ri[2]
    dest2 = jnp.sum(jnp.where(ri[1][None, :] == eio, pstart[:, None], 0), axis=0) + ri[3]
    blk_start = jnp.arange(nblk, dtype=I32) * tm
    blk_e = jnp.minimum(jnp.sum((pend[None, :] <= blk_start[:, None]).astype(I32), axis=1), MOE_EXPERTS - 1)
    nused = jnp.maximum(pend[-1] // tm, 1).astype(I32).reshape(1)
    r = jnp.arange(tm, dtype=I32)[None, :]
    pad_slot = jnp.where(r < (pcounts - counts)[:, None], (pstart + counts)[:, None] + r, p_rows + eio * tm + r)
    kd = n // SC_WORKERS // SC_ROWS
    idx = jnp.concatenate([dest1.reshape(SC_WORKERS, kd, SC_ROWS), dest2.reshape(SC_WORKERS, kd, SC_ROWS),
                           pad_slot.reshape(SC_WORKERS, -1, SC_ROWS)], axis=1)
    zero_rows = jnp.zeros((SC_ROWS, HALF), U32)
    xs = _sc_dispatch(hp, idx, zero_rows, p_rows + MOE_EXPERTS * tm)
    ys = _experts(xs, blk_e, nused, w_gate, w_up, w_down, layer)
    gidx = jnp.concatenate([dest1, dest2]).reshape(SC_WORKERS, -1, SC_ROWS)
    yg = _sc_gather(ys, gidx)
    return _combine(yg, x, rw[0].reshape(n, 1), rw[1].reshape(n, 1), final_g, final_norm)


def kernel(x, positions, norm_mix_g, norm_ffn_g, ret_w_in, ret_head_g, ret_w_out, conv_w_pw1, conv_b_pw1, conv_w_dw, conv_b_dw, conv_ln_g, conv_ln_b, conv_w_pw2, conv_b_pw2, moe_w_rg, moe_b_rg, moe_w_re, moe_b_re, moe_w_gate, moe_w_up, moe_w_down, final_norm_g):
    b, s, d = x.shape
    n = b * s
    xt = x.reshape(n, d)
    pos = positions.reshape(n, 1)
    fg = final_norm_g.reshape(1, d)

    q, k, v, gate = _ret_inproj(xt, pos, norm_mix_g[0].reshape(1, d), ret_w_in[0].astype(BF16))
    xt = _ret_core(q, k, v, gate, xt, ret_head_g[0].reshape(RET_V, 1), ret_w_out[0])
    xt = _moe(xt, norm_ffn_g[0].reshape(1, d), moe_w_rg[0], moe_b_rg[0], moe_w_re[0], moe_b_re[0],
              moe_w_gate, moe_w_up, moe_w_down, 0, fg, False)

    u = _conv_pw1(xt, norm_mix_g[1].reshape(1, d), conv_w_pw1[0].astype(BF16), conv_b_pw1[0].reshape(1, 2 * d))
    xt = _conv_core(u, xt, conv_w_dw[0], conv_b_dw[0].reshape(1, d), conv_ln_g[0].reshape(1, d),
                    conv_ln_b[0].reshape(1, d), conv_w_pw2[0].astype(BF16), conv_b_pw2[0].reshape(1, d))
    xt = _moe(xt, norm_ffn_g[1].reshape(1, d), moe_w_rg[1], moe_b_rg[1], moe_w_re[1], moe_b_re[1],
              moe_w_gate, moe_w_up, moe_w_down, 1, fg, True)
    return xt.reshape(b, s, d)
```

```python
import functools
import math

import jax
import jax.numpy as jnp
import numpy as np
from jax import lax
from jax.experimental import pallas as pl
from jax.experimental.pallas import tpu as pltpu
from jax.experimental.pallas import tpu_sc as plsc

F32 = jnp.float32
BF16 = jnp.bfloat16
U32 = jnp.uint32
I32 = jnp.int32

D_MODEL = 1024
RET_HEADS = 4
RET_DK = 256
RET_DV = 512
RET_QK = RET_HEADS * RET_DK
RET_V = RET_HEADS * RET_DV
ROPE_BASE = 10000.0
CONV_WIDTH = 31
MOE_GROUPS = 4
MOE_EPG = 8
MOE_EXPERTS = MOE_GROUPS * MOE_EPG
MOE_FF = 512
NORM_EPS = 1e-6

TM_PROJ = 512
PROJ_ROWS = 256
RET_C = 256
TM_CONV = 256
CONV_HALO = 32
CONV_ROWS = 128
T_ROUTE = 512
TM_EXP = 256
EXP_SUB = 4
T_COMB = 256
SC_CORES = 2
SC_SUBCORES = 16
SC_WORKERS = SC_CORES * SC_SUBCORES
SC_ROWS = 128
HALF = D_MODEL // 2

VMEM_LIMIT = 56 * 1024 * 1024


def _cparams(sem, flags=None):
    return pltpu.CompilerParams(dimension_semantics=sem, vmem_limit_bytes=VMEM_LIMIT, flags=flags)


def _rms(x, g):
    ms = jnp.mean(x * x, axis=-1, keepdims=True)
    return x * lax.rsqrt(ms + NORM_EPS) * g


def _silu(x):
    return x * (1.0 / (1.0 + jnp.exp(-x)))


def _pack_bf16_pairs(y):
    lo = pltpu.bitcast(y[:, :HALF].astype(BF16).astype(F32), U32)
    hi = pltpu.bitcast(y[:, HALF:].astype(BF16).astype(F32), U32)
    return (hi & jnp.uint32(0xFFFF0000)) | (lo >> 16)


def _unpack_bf16_pairs(p):
    lo = pltpu.bitcast(p << 16, F32)
    hi = pltpu.bitcast(p & jnp.uint32(0xFFFF0000), F32)
    return lo, hi


def _ret_inproj_kernel(x_ref, pos_ref, g_ref, inv_ref, w_ref, q_ref, k_ref, v_ref, gate_ref):
    half = RET_DK // 2
    kscale = RET_DK ** -0.5
    for r0 in range(0, TM_PROJ, PROJ_ROWS):
        rs = slice(r0, r0 + PROJ_ROWS)
        h = _rms(x_ref[rs, :], g_ref[...]).astype(BF16)
        ang = pos_ref[rs, :].astype(F32) * inv_ref[...]
        cos = jnp.cos(ang)
        sin = jnp.sin(ang)
        for hd in range(RET_HEADS):
            for base, out_ref, cs, sn in ((0, q_ref, cos, sin), (RET_QK, k_ref, cos * kscale, sin * kscale)):
                c0 = base + hd * RET_DK
                t = jnp.dot(h, w_ref[:, c0:c0 + RET_DK], preferred_element_type=F32)
                t1 = t[:, :half]
                t2 = t[:, half:]
                out_ref[rs, hd * RET_DK:hd * RET_DK + half] = (t1 * cs - t2 * sn).astype(BF16)
                out_ref[rs, hd * RET_DK + half:(hd + 1) * RET_DK] = (t1 * sn + t2 * cs).astype(BF16)
        for j in range(RET_V // 512):
            c0 = 2 * RET_QK + j * 512
            v_ref[rs, j * 512:(j + 1) * 512] = jnp.dot(
                h, w_ref[:, c0:c0 + 512], preferred_element_type=F32).astype(BF16)
            c1 = 2 * RET_QK + RET_V + j * 512
            gate_ref[rs, j * 512:(j + 1) * 512] = jnp.dot(
                h, w_ref[:, c1:c1 + 512], preferred_element_type=F32).astype(BF16)


def _ret_inproj(x, pos, g, w_in_bf16):
    n = x.shape[0]
    half = RET_DK // 2
    inv = (ROPE_BASE ** (-jnp.arange(half, dtype=F32) / half)).reshape(1, half)
    tm = TM_PROJ
    return pl.pallas_call(
        _ret_inproj_kernel,
        grid=(n // tm,),
        in_specs=[
            pl.BlockSpec((tm, D_MODEL), lambda i: (i, 0)),
            pl.BlockSpec((tm, 1), lambda i: (i, 0)),
            pl.BlockSpec((1, D_MODEL), lambda i: (0, 0)),
            pl.BlockSpec((1, half), lambda i: (0, 0)),
            pl.BlockSpec(w_in_bf16.shape, lambda i: (0, 0)),
        ],
        out_specs=[
            pl.BlockSpec((tm, RET_QK), lambda i: (i, 0)),
            pl.BlockSpec((tm, RET_QK), lambda i: (i, 0)),
            pl.BlockSpec((tm, RET_V), lambda i: (i, 0)),
            pl.BlockSpec((tm, RET_V), lambda i: (i, 0)),
        ],
        out_shape=[
            jax.ShapeDtypeStruct((n, RET_QK), BF16),
            jax.ShapeDtypeStruct((n, RET_QK), BF16),
            jax.ShapeDtypeStruct((n, RET_V), BF16),
            jax.ShapeDtypeStruct((n, RET_V), BF16),
        ],
        compiler_params=_cparams(("arbitrary",)),
        name="ret_inproj",
    )(x, pos, g, inv, w_in_bf16)


def _ret_core_kernel(cdec_ref, q_ref, k_ref, v_ref, gate_ref, x_ref, hg_ref, intra_ref, cross_ref,
                     kdec_ref, wo_ref, out_ref, state_ref, y_ref, wos_ref):
    @pl.when(pl.program_id(0) == 0)
    def _():
        state_ref[...] = jnp.zeros_like(state_ref)
        wos_ref[...] = (wo_ref[...] * hg_ref[...]).astype(BF16)

    for hd in range(RET_HEADS):
        q = q_ref[:, hd * RET_DK:(hd + 1) * RET_DK]
        k = k_ref[:, hd * RET_DK:(hd + 1) * RET_DK]
        v = v_ref[:, hd * RET_DV:(hd + 1) * RET_DV]
        state = state_ref[hd]
        scores = lax.dot_general(q, k, (((1,), (1,)), ((), ())), preferred_element_type=F32)
        scores = (scores * intra_ref[hd]).astype(BF16)
        o = jnp.dot(scores, v, preferred_element_type=F32)
        cross = cross_ref[hd]
        o_cross = jnp.dot(q, state.astype(BF16), preferred_element_type=F32)
        o = o + o_cross * jnp.concatenate([cross] * (RET_DV // 128), axis=1)
        kdec = kdec_ref[hd]
        kd = (k.astype(F32) * jnp.concatenate([kdec] * (RET_DK // 128), axis=1)).astype(BF16)
        upd = lax.dot_general(kd, v, (((0,), (0,)), ((), ())), preferred_element_type=F32)
        state_ref[hd] = state * cdec_ref[hd] + upd
        ms = jnp.mean(o * o, axis=-1, keepdims=True)
        on = o * lax.rsqrt(ms + NORM_EPS)
        gt = gate_ref[:, hd * RET_DV:(hd + 1) * RET_DV].astype(F32)
        y_ref[:, hd * RET_DV:(hd + 1) * RET_DV] = (_silu(gt) * on).astype(BF16)
    out_ref[...] = x_ref[...] + jnp.dot(y_ref[...], wos_ref[...], preferred_element_type=F32)


def _ret_core(q, k, v, gate, x, head_g_col, w_out):
    n = x.shape[0]
    c = RET_C
    log_gamma = jnp.log1p(-(2.0 ** (-5.0 - jnp.arange(RET_HEADS, dtype=F32))))
    idx = jnp.arange(c, dtype=F32)
    diff = idx[:, None] - idx[None, :]
    intra = jnp.where(diff >= 0, jnp.exp(log_gamma[:, None, None] * jnp.maximum(diff, 0.0)), 0.0)
    cross = jnp.broadcast_to(jnp.exp(log_gamma[:, None] * (idx + 1.0))[:, :, None], (RET_HEADS, c, 128))
    kdec = jnp.broadcast_to(jnp.exp(log_gamma[:, None] * (c - 1.0 - idx))[:, :, None], (RET_HEADS, c, 128))
    cdec = jnp.exp(log_gamma * c)
    return pl.pallas_call(
        _ret_core_kernel,
        grid=(n // c,),
        in_specs=[
            pl.BlockSpec(memory_space=pltpu.SMEM),
            pl.BlockSpec((c, RET_QK), lambda i: (i, 0)),
            pl.BlockSpec((c, RET_QK), lambda i: (i, 0)),
            pl.BlockSpec((c, RET_V), lambda i: (i, 0)),
            pl.BlockSpec((c, RET_V), lambda i: (i, 0)),
            pl.BlockSpec((c, D_MODEL), lambda i: (i, 0)),
            pl.BlockSpec((RET_V, 1), lambda i: (0, 0)),
            pl.BlockSpec((RET_HEADS, c, c), lambda i: (0, 0, 0)),
            pl.BlockSpec((RET_HEADS, c, 128), lambda i: (0, 0, 0)),
            pl.BlockSpec((RET_HEADS, c, 128), lambda i: (0, 0, 0)),
            pl.BlockSpec((RET_V, D_MODEL), lambda i: (0, 0)),
        ],
        out_specs=pl.BlockSpec((c, D_MODEL), lambda i: (i, 0)),
        out_shape=jax.ShapeDtypeStruct((n, D_MODEL), F32),
        scratch_shapes=[
            pltpu.VMEM((RET_HEADS, RET_DK, RET_DV), F32),
            pltpu.VMEM((c, RET_V), BF16),
            pltpu.VMEM((RET_V, D_MODEL), BF16),
        ],
        compiler_params=_cparams(("arbitrary",)),
        name="ret_core",
    )(cdec, q, k, v, gate, x, head_g_col, intra, cross, kdec, w_out)


def _conv_pw1_kernel(x_ref, g_ref, w_ref, b_ref, u_ref):
    h = _rms(x_ref[...], g_ref[...]).astype(BF16)
    for j in range(D_MODEL // 512):
        a = jnp.dot(h, w_ref[:, j * 512:(j + 1) * 512], preferred_element_type=F32)
        a = a + b_ref[:, j * 512:(j + 1) * 512]
        gt = jnp.dot(h, w_ref[:, D_MODEL + j * 512:D_MODEL + (j + 1) * 512], preferred_element_type=F32)
        gt = gt + b_ref[:, D_MODEL + j * 512:D_MODEL + (j + 1) * 512]
        u_ref[:, j * 512:(j + 1) * 512] = a * (1.0 / (1.0 + jnp.exp(-gt)))


def _conv_pw1(x, g, w_bf16, b):
    n = x.shape[0]
    tm = TM_CONV
    return pl.pallas_call(
        _conv_pw1_kernel,
        grid=(n // tm,),
        in_specs=[
            pl.BlockSpec((tm, D_MODEL), lambda i: (i, 0)),
            pl.BlockSpec((1, D_MODEL), lambda i: (0, 0)),
            pl.BlockSpec((D_MODEL, 2 * D_MODEL), lambda i: (0, 0)),
            pl.BlockSpec((1, 2 * D_MODEL), lambda i: (0, 0)),
        ],
        out_specs=pl.BlockSpec((tm, D_MODEL), lambda i: (i, 0)),
        out_shape=jax.ShapeDtypeStruct((n, D_MODEL), F32),
        compiler_params=_cparams(("arbitrary",)),
        name="conv_pw1",
    )(x, g, w_bf16, b)


def _conv_core_kernel(u_ref, halo_ref, x_ref, wdw_ref, bdw_ref, lng_ref, lnb_ref, w2_ref, b2_ref,
                      out_ref, win_ref, z_ref):
    tm = TM_CONV
    first = pl.program_id(0) == 0
    halo = halo_ref[...]
    win_ref[0:CONV_HALO, :] = jnp.where(first, jnp.zeros_like(halo), halo)
    win_ref[CONV_HALO:CONV_HALO + tm, :] = u_ref[...]
    off = CONV_HALO - (CONV_WIDTH - 1)
    rb = CONV_ROWS
    for cc in range(D_MODEL // 128):
        cs = slice(cc * 128, (cc + 1) * 128)
        for r0 in range(0, tm, rb):
            z = bdw_ref[:, cs]
            for b in range(8):
                rows = rb if b == 0 else rb + 8
                q = None
                for o in range(off, off + CONV_WIDTH):
                    if o % 8 != b:
                        continue
                    term = win_ref[r0 + o - b:r0 + o - b + rows, cs] * wdw_ref[o - off:o - off + 1, cs]
                    q = term if q is None else q + term
                z = z + (q if b == 0 else q[b:b + rb, :])
            z_ref[r0:r0 + rb, cs] = z
    z = z_ref[...]
    mu = jnp.mean(z, axis=-1, keepdims=True)
    zc = z - mu
    var = jnp.mean(zc * zc, axis=-1, keepdims=True)
    zn = zc * lax.rsqrt(var + NORM_EPS) * lng_ref[...] + lnb_ref[...]
    y = _silu(zn).astype(BF16)
    out_ref[...] = x_ref[...] + jnp.dot(y, w2_ref[...], preferred_element_type=F32) + b2_ref[...]


def _conv_core(u, x, w_dw, b_dw, ln_g, ln_b, w2_bf16, b2):
    n = x.shape[0]
    tm = TM_CONV
    r = tm // CONV_HALO
    wdw_pad = jnp.zeros((32, D_MODEL), F32).at[:CONV_WIDTH].set(w_dw)
    return pl.pallas_call(
        _conv_core_kernel,
        grid=(n // tm,),
        in_specs=[
            pl.BlockSpec((tm, D_MODEL), lambda i: (i, 0)),
            pl.BlockSpec((CONV_HALO, D_MODEL), lambda i: (jnp.maximum(i * r - 1, 0), 0)),
            pl.BlockSpec((tm, D_MODEL), lambda i: (i, 0)),
            pl.BlockSpec((32, D_MODEL), lambda i: (0, 0)),
            pl.BlockSpec((1, D_MODEL), lambda i: (0, 0)),
            pl.BlockSpec((1, D_MODEL), lambda i: (0, 0)),
            pl.BlockSpec((1, D_MODEL), lambda i: (0, 0)),
            pl.BlockSpec((D_MODEL, D_MODEL), lambda i: (0, 0)),
            pl.BlockSpec((1, D_MODEL), lambda i: (0, 0)),
        ],
        out_specs=pl.BlockSpec((tm, D_MODEL), lambda i: (i, 0)),
        out_shape=jax.ShapeDtypeStruct((n, D_MODEL), F32),
        scratch_shapes=[pltpu.VMEM((CONV_HALO + tm, D_MODEL), F32), pltpu.VMEM((tm, D_MODEL), F32)],
        compiler_params=_cparams(("arbitrary",)),
        name="conv_core",
    )(u, u, x, wdw_pad, b_dw, ln_g, ln_b, w2_bf16, b2)


def _router_kernel(x_ref, g_ref, wr_ref, br_ref, hp_ref, ri_ref, rw_ref, cnt_ref, carry_ref):
    t = T_ROUTE

    @pl.when(pl.program_id(0) == 0)
    def _():
        carry_ref[...] = jnp.zeros_like(carry_ref)

    h = _rms(x_ref[...], g_ref[...])
    hp_ref[...] = _pack_bf16_pairs(h)
    logits = lax.dot_general(wr_ref[...], h, (((1,), (1,)), ((), ())),
                             precision=lax.Precision.HIGHEST, preferred_element_type=F32)
    logits = logits + br_ref[:, 0:1]

    best = logits[0:1]
    gi = jnp.zeros((1, t), I32)
    for j in range(1, MOE_GROUPS):
        r = logits[j:j + 1]
        up = r > best
        gi = jnp.where(up, j, gi)
        best = jnp.where(up, r, best)
    den = jnp.zeros((1, t), F32)
    for j in range(MOE_GROUPS):
        den = den + jnp.exp(logits[j:j + 1] - best)
    gate_g = 1.0 / den

    sel = logits[8:8 + MOE_EPG]
    for j in range(1, MOE_GROUPS):
        sel = jnp.where(gi == j, logits[8 + j * MOE_EPG:8 + (j + 1) * MOE_EPG], sel)

    m1 = sel[0:1]
    i1 = jnp.zeros((1, t), I32)
    for j in range(1, MOE_EPG):
        r = sel[j:j + 1]
        up = r > m1
        i1 = jnp.where(up, j, i1)
        m1 = jnp.where(up, r, m1)
    m2 = jnp.full((1, t), -jnp.inf, F32)
    i2 = jnp.zeros((1, t), I32)
    started = jnp.zeros((1, t), jnp.bool_)
    for j in range(MOE_EPG):
        r = sel[j:j + 1]
        ok = i1 != j
        up = ok & ((r > m2) | jnp.logical_not(started))
        i2 = jnp.where(up, j, i2)
        m2 = jnp.where(up, r, m2)
        started = started | ok
    e21 = jnp.exp(m2 - m1)
    p1 = 1.0 / (1.0 + e21)
    w1 = gate_g * p1
    w2 = gate_g * (e21 * p1)
    eid1 = gi * MOE_EPG + i1
    eid2 = gi * MOE_EPG + i2

    eio = lax.broadcasted_iota(I32, (MOE_EXPERTS, t), 0)
    oh1 = eio == eid1
    oh2 = eio == eid2
    oh = (oh1 | oh2).astype(F32)
    rio = lax.broadcasted_iota(I32, (t, t), 0)
    cio = lax.broadcasted_iota(I32, (t, t), 1)
    upper = (rio < cio).astype(BF16)
    cum = jnp.dot(oh.astype(BF16), upper, preferred_element_type=F32) + carry_ref[:, 0:1]
    rank1 = jnp.sum(jnp.where(oh1, cum, 0.0), axis=0, keepdims=True)
    rank2 = jnp.sum(jnp.where(oh2, cum, 0.0), axis=0, keepdims=True)
    carry_ref[...] = carry_ref[...] + jnp.sum(oh, axis=1, keepdims=True)
    cnt_ref[...] = carry_ref[...]

    zi = jnp.zeros((4, t), I32)
    ri_ref[...] = jnp.concatenate([eid1, eid2, rank1.astype(I32), rank2.astype(I32), zi], axis=0)
    zf = jnp.zeros((6, t), F32)
    rw_ref[...] = jnp.concatenate([w1, w2, zf], axis=0)


def _router(x, g, w_rg, b_rg, w_re, b_re):
    n = x.shape[0]
    t = T_ROUTE
    wr = jnp.zeros((40, D_MODEL), F32).at[0:MOE_GROUPS].set(w_rg.T).at[8:40].set(w_re.T)
    br = jnp.zeros((40,), F32).at[0:MOE_GROUPS].set(b_rg).at[8:40].set(b_re)
    br = jnp.broadcast_to(br[:, None], (40, 128))
    return pl.pallas_call(
        _router_kernel,
        grid=(n // t,),
        in_specs=[
            pl.BlockSpec((t, D_MODEL), lambda i: (i, 0)),
            pl.BlockSpec((1, D_MODEL), lambda i: (0, 0)),
            pl.BlockSpec((40, D_MODEL), lambda i: (0, 0)),
            pl.BlockSpec((40, 128), lambda i: (0, 0)),
        ],
        out_specs=[
            pl.BlockSpec((t, HALF), lambda i: (i, 0)),
            pl.BlockSpec((8, t), lambda i: (0, i)),
            pl.BlockSpec((8, t), lambda i: (0, i)),
            pl.BlockSpec((MOE_EXPERTS, 128), lambda i: (0, 0)),
        ],
        out_shape=[
            jax.ShapeDtypeStruct((n, HALF), U32),
            jax.ShapeDtypeStruct((8, n), I32),
            jax.ShapeDtypeStruct((8, n), F32),
            jax.ShapeDtypeStruct((MOE_EXPERTS, 128), F32),
        ],
        scratch_shapes=[pltpu.VMEM((MOE_EXPERTS, 128), F32)],
        compiler_params=_cparams(("arbitrary",)),
        name="moe_router",
    )(x, g, wr, br)


def _sc_mesh():
    return plsc.VectorSubcoreMesh(core_axis_name="c", subcore_axis_name="s",
                                  num_cores=SC_CORES, num_subcores=SC_SUBCORES)


def _sc_worker_id():
    return lax.axis_index("s") * SC_CORES + lax.axis_index("c")


def _sc_dispatch(hp, idx, zero_rows, total_rows):
    n = hp.shape[0]
    tpw = n // SC_WORKERS
    kd = tpw // SC_ROWS
    kp = idx.shape[1] - 2 * kd

    @functools.partial(
        pl.kernel, mesh=_sc_mesh(),
        out_type=jax.ShapeDtypeStruct((total_rows, HALF), U32),
        scratch_types=[pltpu.VMEM((2 * kd + kp, SC_ROWS), I32), pltpu.VMEM((SC_ROWS, HALF), U32)],
        name="moe_dispatch_sc",
    )
    def k(hp_hbm, idx_hbm, zero_hbm, xs_hbm, idx_v, rows_v):
        wid = _sc_worker_id()
        pltpu.sync_copy(idx_hbm.at[wid], idx_v)
        pltpu.sync_copy(zero_hbm, rows_v)
        for j in range(kp):
            pltpu.sync_copy(rows_v, xs_hbm.at[idx_v.at[2 * kd + j]])
        for c in range(kd):
            pltpu.sync_copy(hp_hbm.at[pl.ds(wid * tpw + c * SC_ROWS, SC_ROWS)], rows_v)
            pltpu.sync_copy(rows_v, xs_hbm.at[idx_v.at[c]])
            pltpu.sync_copy(rows_v, xs_hbm.at[idx_v.at[kd + c]])

    return k(hp, idx, zero_rows)


def _sc_gather(ys, idx):
    kg = idx.shape[1]
    rows_per_worker = kg * SC_ROWS

    @functools.partial(
        pl.kernel, mesh=_sc_mesh(),
        out_type=jax.ShapeDtypeStruct((SC_WORKERS * rows_per_worker, HALF), U32),
        scratch_types=[pltpu.VMEM((kg, SC_ROWS), I32), pltpu.VMEM((SC_ROWS, HALF), U32)],
        name="moe_gather_sc",
    )
    def k(ys_hbm, idx_hbm, yg_hbm, idx_v, rows_v):
        wid = _sc_worker_id()
        pltpu.sync_copy(idx_hbm.at[wid], idx_v)
        for c in range(kg):
            pltpu.sync_copy(ys_hbm.at[idx_v.at[c]], rows_v)
            pltpu.sync_copy(rows_v, yg_hbm.at[pl.ds(wid * rows_per_worker + c * SC_ROWS, SC_ROWS)])

    return k(ys, idx)


def _expert_kernel(blk_e_ref, nused_ref, first_ref, slot_ref, nxt_ref, xs_ref, wg_hbm, wu_hbm, wd_hbm,
                   ys_ref, wg_buf, wu_buf, wd_buf, sems, *, layer):
    step = pl.program_id(0)
    tm = TM_EXP

    def weight_copies(e, s):
        return (pltpu.make_async_copy(wg_hbm.at[layer, e], wg_buf.at[s], sems.at[s, 0]),
                pltpu.make_async_copy(wu_hbm.at[layer, e], wu_buf.at[s], sems.at[s, 1]),
                pltpu.make_async_copy(wd_hbm.at[layer, e], wd_buf.at[s], sems.at[s, 2]))

    def block(j):
        i = step * EXP_SUB + j
        rows = slice(j * tm, (j + 1) * tm)

        @pl.when(i < nused_ref[0])
        def _():
            s = slot_ref[i]

            if j == 0:
                @pl.when(i == 0)
                def _():
                    for c in weight_copies(blk_e_ref[0], 0):
                        c.start()

            @pl.when(first_ref[i] == 1)
            def _():
                for c in weight_copies(blk_e_ref[i], s):
                    c.wait()

                @pl.when(nxt_ref[i] >= 0)
                def _():
                    for c in weight_copies(nxt_ref[i], 1 - s):
                        c.start()

            lo, hi = _unpack_bf16_pairs(xs_ref[rows, :])
            xf = jnp.concatenate([lo, hi], axis=1)
            a = jnp.dot(xf, wg_buf[s], preferred_element_type=F32)
            b = jnp.dot(xf, wu_buf[s], preferred_element_type=F32)
            hm = _silu(a) * b
            y = jnp.dot(hm, wd_buf[s], preferred_element_type=F32)
            ys_ref[rows, :] = _pack_bf16_pairs(y)

        @pl.when(i >= nused_ref[0])
        def _():
            ys_ref[rows, :] = jnp.zeros((tm, HALF), U32)

    for j in range(EXP_SUB):
        block(j)


def _experts(xs, blk_e, nused, w_gate, w_up, w_down, layer):
    tm = TM_EXP
    p_rows = xs.shape[0] - MOE_EXPERTS * tm
    nblk = p_rows // tm
    pos = jnp.arange(nblk, dtype=I32)
    valid = pos < nused[0]
    prev_e = jnp.concatenate([jnp.full((1,), -1, I32), blk_e[:-1]])
    first = valid & (blk_e != prev_e)
    slot = (jnp.cumsum(first.astype(I32)) - 1) % 2
    first_pos = jnp.where(first, pos, nblk)
    next_first = jnp.concatenate([lax.cummin(first_pos, reverse=True)[1:], jnp.full((1,), nblk, I32)])
    nxt = jnp.where(next_first < nblk, blk_e[jnp.minimum(next_first, nblk - 1)], -1)

    def blk(i, be, nu, *_):
        return jnp.minimum(i, (nu[0] - 1) // EXP_SUB)

    grid_spec = pltpu.PrefetchScalarGridSpec(
        num_scalar_prefetch=5,
        grid=(nblk // EXP_SUB,),
        in_specs=[
            pl.BlockSpec((EXP_SUB * tm, HALF), lambda i, *sp: (blk(i, *sp), 0)),
            pl.BlockSpec(memory_space=pl.ANY),
            pl.BlockSpec(memory_space=pl.ANY),
            pl.BlockSpec(memory_space=pl.ANY),
        ],
        out_specs=pl.BlockSpec((EXP_SUB * tm, HALF), lambda i, *sp: (i, 0)),
        scratch_shapes=[
            pltpu.VMEM((2, D_MODEL, MOE_FF), F32),
            pltpu.VMEM((2, D_MODEL, MOE_FF), F32),
            pltpu.VMEM((2, MOE_FF, D_MODEL), F32),
            pltpu.SemaphoreType.DMA((2, 3)),
        ],
    )
    return pl.pallas_call(
        functools.partial(_expert_kernel, layer=layer),
        grid_spec=grid_spec,
        out_shape=jax.ShapeDtypeStruct((p_rows, HALF), U32),
        compiler_params=_cparams(("arbitrary",)),
        name="moe_experts",
    )(blk_e, nused, first.astype(I32), slot.astype(I32), nxt.astype(I32), xs, w_gate, w_up, w_down)


def _combine_kernel(x_ref, w1_ref, w2_ref, fg_ref, y1_ref, y2_ref, out_ref, *, final_norm):
    lo1, hi1 = _unpack_bf16_pairs(y1_ref[...])
    lo2, hi2 = _unpack_bf16_pairs(y2_ref[...])
    w1 = w1_ref[...]
    w2 = w2_ref[...]
    x = x_ref[...]
    o_lo = x[:, :HALF] + w1 * lo1 + w2 * lo2
    o_hi = x[:, HALF:] + w1 * hi1 + w2 * hi2
    if final_norm:
        ms = (jnp.sum(o_lo * o_lo, axis=-1, keepdims=True)
              + jnp.sum(o_hi * o_hi, axis=-1, keepdims=True)) * (1.0 / D_MODEL)
        sc = lax.rsqrt(ms + NORM_EPS)
        o_lo = o_lo * sc * fg_ref[:, :HALF]
        o_hi = o_hi * sc * fg_ref[:, HALF:]
    out_ref[:, :HALF] = o_lo
    out_ref[:, HALF:] = o_hi


def _combine(yg, x, w1, w2, final_g, final_norm):
    n = x.shape[0]
    td = T_COMB
    nb = n // td
    return pl.pallas_call(
        functools.partial(_combine_kernel, final_norm=final_norm),
        grid=(nb,),
        in_specs=[
            pl.BlockSpec((td, D_MODEL), lambda i: (i, 0)),
            pl.BlockSpec((td, 1), lambda i: (i, 0)),
            pl.BlockSpec((td, 1), lambda i: (i, 0)),
            pl.BlockSpec((1, D_MODEL), lambda i: (0, 0)),
            pl.BlockSpec((td, HALF), lambda i: (i, 0)),
            pl.BlockSpec((td, HALF), lambda i: (i + nb, 0)),
        ],
        out_specs=pl.BlockSpec((td, D_MODEL), lambda i: (i, 0)),
        out_shape=jax.ShapeDtypeStruct((n, D_MODEL), F32),
        compiler_params=_cparams(("arbitrary",)),
        name="moe_combine",
    )(x, w1, w2, final_g, yg, yg)


def _moe(x, g, w_rg, b_rg, w_re, b_re, w_gate, w_up, w_down, layer, final_g, final_norm):
    n = x.shape[0]
    tm = TM_EXP
    p_rows = 2 * n + MOE_EXPERTS * tm
    nblk = p_rows // tm
    hp, ri, rw, cnt = _router(x, g, w_rg, b_rg, w_re, b_re)
    counts = cnt[:, 0].astype(I32)
    pcounts = (counts + tm - 1) // tm * tm
    pend = jnp.cumsum(pcounts)
    pstart = pend - pcounts
    eio = jnp.arange(MOE_EXPERTS, dtype=I32)[:, None]
    dest1 = jnp.sum(jnp.where(ri[0][None, :] == eio, pstart[:, None], 0), axis=0) + ri[2]
    dest2 = jnp.sum(jnp.where(ri[1][None, :] == eio, pstart[:, None], 0), axis=0) + ri[3]
    blk_start = jnp.arange(nblk, dtype=I32) * tm
    blk_e = jnp.minimum(jnp.sum((pend[None, :] <= blk_start[:, None]).astype(I32), axis=1), MOE_EXPERTS - 1)
    nused = jnp.maximum(pend[-1] // tm, 1).astype(I32).reshape(1)
    r = jnp.arange(tm, dtype=I32)[None, :]
    pad_slot = jnp.where(r < (pcounts - counts)[:, None], (pstart + counts)[:, None] + r, p_rows + eio * tm + r)
    kd = n // SC_WORKERS // SC_ROWS
    idx = jnp.concatenate([dest1.reshape(SC_WORKERS, kd, SC_ROWS), dest2.reshape(SC_WORKERS, kd, SC_ROWS),
                           pad_slot.reshape(SC_WORKERS, -1, SC_ROWS)], axis=1)
    zero_rows = jnp.zeros((SC_ROWS, HALF), U32)
    xs = _sc_dispatch(hp, idx, zero_rows, p_rows + MOE_EXPERTS * tm)
    ys = _experts(xs, blk_e, nused, w_gate, w_up, w_down, layer)
    gidx = jnp.concatenate([dest1, dest2]).reshape(SC_WORKERS, -1, SC_ROWS)
    yg = _sc_gather(ys, gidx)
    return _combine(yg, x, rw[0].reshape(n, 1), rw[1].reshape(n, 1), final_g, final_norm)


def kernel(x, positions, norm_mix_g, norm_ffn_g, ret_w_in, ret_head_g, ret_w_out, conv_w_pw1, conv_b_pw1, conv_w_dw, conv_b_dw, conv_ln_g, conv_ln_b, conv_w_pw2, conv_b_pw2, moe_w_rg, moe_b_rg, moe_w_re, moe_b_re, moe_w_gate, moe_w_up, moe_w_down, final_norm_g):
    b, s, d = x.shape
    n = b * s
    xt = x.reshape(n, d)
    pos = positions.reshape(n, 1)
    fg = final_norm_g.reshape(1, d)

    q, k, v, gate = _ret_inproj(xt, pos, norm_mix_g[0].reshape(1, d), ret_w_in[0].astype(BF16))
    xt = _ret_core(q, k, v, gate, xt, ret_head_g[0].reshape(RET_V, 1), ret_w_out[0])
    xt = _moe(xt, norm_ffn_g[0].reshape(1, d), moe_w_rg[0], moe_b_rg[0], moe_w_re[0], moe_b_re[0],
              moe_w_gate, moe_w_up, moe_w_down, 0, fg, False)

    u = _conv_pw1(xt, norm_mix_g[1].reshape(1, d), conv_w_pw1[0].astype(BF16), conv_b_pw1[0].reshape(1, 2 * d))
    xt = _conv_core(u, xt, conv_w_dw[0], conv_b_dw[0].reshape(1, d), conv_ln_g[0].reshape(1, d),
                    conv_ln_b[0].reshape(1, d), conv_w_pw2[0].astype(BF16), conv_b_pw2[0].reshape(1, d))
    xt = _moe(xt, norm_ffn_g[1].reshape(1, d), moe_w_rg[1], moe_b_rg[1], moe_w_re[1], moe_b_re[1],
              moe_w_gate, moe_w_up, moe_w_down, 1, fg, True)
    return xt.reshape(b, s, d)
```

```python
import functools
import math

import jax
import jax.numpy as jnp
import numpy as np
from jax import lax
from jax.experimental import pallas as pl
from jax.experimental.pallas import tpu as pltpu
from jax.experimental.pallas import tpu_sc as plsc

F32 = jnp.float32
BF16 = jnp.bfloat16
U32 = jnp.uint32
I32 = jnp.int32

D_MODEL = 1024
RET_HEADS = 4
RET_DK = 256
RET_DV = 512
RET_QK = RET_HEADS * RET_DK
RET_V = RET_HEADS * RET_DV
ROPE_BASE = 10000.0
CONV_WIDTH = 31
MOE_GROUPS = 4
MOE_EPG = 8
MOE_EXPERTS = MOE_GROUPS * MOE_EPG
MOE_FF = 512
NORM_EPS = 1e-6

TM_PROJ = 512
PROJ_ROWS = 256
RET_C = 256
TM_CONV = 256
CONV_HALO = 32
CONV_ROWS = 128
T_ROUTE = 512
TM_EXP = 256
EXP_SUB = 4
T_COMB = 512
SC_CORES = 2
SC_SUBCORES = 16
SC_WORKERS = SC_CORES * SC_SUBCORES
SC_ROWS = 128
HALF = D_MODEL // 2

VMEM_LIMIT = 56 * 1024 * 1024


def _cparams(sem, flags=None):
    return pltpu.CompilerParams(dimension_semantics=sem, vmem_limit_bytes=VMEM_LIMIT, flags=flags)


def _rms(x, g):
    ms = jnp.mean(x * x, axis=-1, keepdims=True)
    return x * lax.rsqrt(ms + NORM_EPS) * g


def _silu(x):
    return x * (1.0 / (1.0 + jnp.exp(-x)))


def _pack_bf16_pairs(y):
    lo = pltpu.bitcast(y[:, :HALF].astype(BF16).astype(F32), U32)
    hi = pltpu.bitcast(y[:, HALF:].astype(BF16).astype(F32), U32)
    return (hi & jnp.uint32(0xFFFF0000)) | (lo >> 16)


def _unpack_bf16_pairs(p):
    lo = pltpu.bitcast(p << 16, F32)
    hi = pltpu.bitcast(p & jnp.uint32(0xFFFF0000), F32)
    return lo, hi


def _ret_inproj_kernel(x_ref, pos_ref, g_ref, inv_ref, w_ref, q_ref, k_ref, v_ref, gate_ref):
    half = RET_DK // 2
    kscale = RET_DK ** -0.5
    for r0 in range(0, TM_PROJ, PROJ_ROWS):
        rs = slice(r0, r0 + PROJ_ROWS)
        h = _rms(x_ref[rs, :], g_ref[...]).astype(BF16)
        ang = pos_ref[rs, :].astype(F32) * inv_ref[...]
        cos = jnp.cos(ang)
        sin = jnp.sin(ang)
        for hd in range(RET_HEADS):
            for base, out_ref, cs, sn in ((0, q_ref, cos, sin), (RET_QK, k_ref, cos * kscale, sin * kscale)):
                c0 = base + hd * RET_DK
                t = jnp.dot(h, w_ref[:, c0:c0 + RET_DK], preferred_element_type=F32)
                t1 = t[:, :half]
                t2 = t[:, half:]
                out_ref[rs, hd * RET_DK:hd * RET_DK + half] = (t1 * cs - t2 * sn).astype(BF16)
                out_ref[rs, hd * RET_DK + half:(hd + 1) * RET_DK] = (t1 * sn + t2 * cs).astype(BF16)
        for j in range(RET_V // 512):
            c0 = 2 * RET_QK + j * 512
            v_ref[rs, j * 512:(j + 1) * 512] = jnp.dot(
                h, w_ref[:, c0:c0 + 512], preferred_element_type=F32).astype(BF16)
            c1 = 2 * RET_QK + RET_V + j * 512
            gate_ref[rs, j * 512:(j + 1) * 512] = jnp.dot(
                h, w_ref[:, c1:c1 + 512], preferred_element_type=F32).astype(BF16)


def _ret_inproj(x, pos, g, w_in_bf16):
    n = x.shape[0]
    half = RET_DK // 2
    inv = (ROPE_BASE ** (-jnp.arange(half, dtype=F32) / half)).reshape(1, half)
    tm = TM_PROJ
    return pl.pallas_call(
        _ret_inproj_kernel,
        grid=(n // tm,),
        in_specs=[
            pl.BlockSpec((tm, D_MODEL), lambda i: (i, 0)),
            pl.BlockSpec((tm, 1), lambda i: (i, 0)),
            pl.BlockSpec((1, D_MODEL), lambda i: (0, 0)),
            pl.BlockSpec((1, half), lambda i: (0, 0)),
            pl.BlockSpec(w_in_bf16.shape, lambda i: (0, 0)),
        ],
        out_specs=[
            pl.BlockSpec((tm, RET_QK), lambda i: (i, 0)),
            pl.BlockSpec((tm, RET_QK), lambda i: (i, 0)),
            pl.BlockSpec((tm, RET_V), lambda i: (i, 0)),
            pl.BlockSpec((tm, RET_V), lambda i: (i, 0)),
        ],
        out_shape=[
            jax.ShapeDtypeStruct((n, RET_QK), BF16),
            jax.ShapeDtypeStruct((n, RET_QK), BF16),
            jax.ShapeDtypeStruct((n, RET_V), BF16),
            jax.ShapeDtypeStruct((n, RET_V), BF16),
        ],
        compiler_params=_cparams(("arbitrary",)),
        name="ret_inproj",
    )(x, pos, g, inv, w_in_bf16)


def _ret_core_kernel(cdec_ref, q_ref, k_ref, v_ref, gate_ref, x_ref, hg_ref, intra_ref, cross_ref,
                     kdec_ref, wo_ref, out_ref, state_ref, y_ref, wos_ref):
    @pl.when(pl.program_id(0) == 0)
    def _():
        state_ref[...] = jnp.zeros_like(state_ref)
        wos_ref[...] = (wo_ref[...] * hg_ref[...]).astype(BF16)

    for hd in range(RET_HEADS):
        q = q_ref[:, hd * RET_DK:(hd + 1) * RET_DK]
        k = k_ref[:, hd * RET_DK:(hd + 1) * RET_DK]
        v = v_ref[:, hd * RET_DV:(hd + 1) * RET_DV]
        state = state_ref[hd]
        scores = lax.dot_general(q, k, (((1,), (1,)), ((), ())), preferred_element_type=F32)
        scores = (scores * intra_ref[hd]).astype(BF16)
        o = jnp.dot(scores, v, preferred_element_type=F32)
        cross = cross_ref[hd]
        o_cross = jnp.dot(q, state.astype(BF16), preferred_element_type=F32)
        o = o + o_cross * jnp.concatenate([cross] * (RET_DV // 128), axis=1)
        kdec = kdec_ref[hd]
        kd = (k.astype(F32) * jnp.concatenate([kdec] * (RET_DK // 128), axis=1)).astype(BF16)
        upd = lax.dot_general(kd, v, (((0,), (0,)), ((), ())), preferred_element_type=F32)
        state_ref[hd] = state * cdec_ref[hd] + upd
        ms = jnp.mean(o * o, axis=-1, keepdims=True)
        on = o * lax.rsqrt(ms + NORM_EPS)
        gt = gate_ref[:, hd * RET_DV:(hd + 1) * RET_DV].astype(F32)
        y_ref[:, hd * RET_DV:(hd + 1) * RET_DV] = (_silu(gt) * on).astype(BF16)
    out_ref[...] = x_ref[...] + jnp.dot(y_ref[...], wos_ref[...], preferred_element_type=F32)


def _ret_core(q, k, v, gate, x, head_g_col, w_out):
    n = x.shape[0]
    c = RET_C
    log_gamma = jnp.log1p(-(2.0 ** (-5.0 - jnp.arange(RET_HEADS, dtype=F32))))
    idx = jnp.arange(c, dtype=F32)
    diff = idx[:, None] - idx[None, :]
    intra = jnp.where(diff >= 0, jnp.exp(log_gamma[:, None, None] * jnp.maximum(diff, 0.0)), 0.0)
    cross = jnp.broadcast_to(jnp.exp(log_gamma[:, None] * (idx + 1.0))[:, :, None], (RET_HEADS, c, 128))
    kdec = jnp.broadcast_to(jnp.exp(log_gamma[:, None] * (c - 1.0 - idx))[:, :, None], (RET_HEADS, c, 128))
    cdec = jnp.exp(log_gamma * c)
    return pl.pallas_call(
        _ret_core_kernel,
        grid=(n // c,),
        in_specs=[
            pl.BlockSpec(memory_space=pltpu.SMEM),
            pl.BlockSpec((c, RET_QK), lambda i: (i, 0)),
            pl.BlockSpec((c, RET_QK), lambda i: (i, 0)),
            pl.BlockSpec((c, RET_V), lambda i: (i, 0)),
            pl.BlockSpec((c, RET_V), lambda i: (i, 0)),
            pl.BlockSpec((c, D_MODEL), lambda i: (i, 0)),
            pl.BlockSpec((RET_V, 1), lambda i: (0, 0)),
            pl.BlockSpec((RET_HEADS, c, c), lambda i: (0, 0, 0)),
            pl.BlockSpec((RET_HEADS, c, 128), lambda i: (0, 0, 0)),
            pl.BlockSpec((RET_HEADS, c, 128), lambda i: (0, 0, 0)),
            pl.BlockSpec((RET_V, D_MODEL), lambda i: (0, 0)),
        ],
        out_specs=pl.BlockSpec((c, D_MODEL), lambda i: (i, 0)),
        out_shape=jax.ShapeDtypeStruct((n, D_MODEL), F32),
        scratch_shapes=[
            pltpu.VMEM((RET_HEADS, RET_DK, RET_DV), F32),
            pltpu.VMEM((c, RET_V), BF16),
            pltpu.VMEM((RET_V, D_MODEL), BF16),
        ],
        compiler_params=_cparams(("arbitrary",)),
        name="ret_core",
    )(cdec, q, k, v, gate, x, head_g_col, intra, cross, kdec, w_out)


def _conv_pw1_kernel(x_ref, w1_ref, w2_ref, y1_ref, y2_ref, g_ref, w_ref, b_ref, xo_ref, u_ref):
    lo1, hi1 = _unpack_bf16_pairs(y1_ref[...])
    lo2, hi2 = _unpack_bf16_pairs(y2_ref[...])
    w1 = w1_ref[...]
    w2 = w2_ref[...]
    x = jnp.concatenate([x_ref[:, :HALF] + w1 * lo1 + w2 * lo2, x_ref[:, HALF:] + w1 * hi1 + w2 * hi2], axis=1)
    xo_ref[...] = x
    h = _rms(x, g_ref[...]).astype(BF16)
    for j in range(D_MODEL // 512):
        a = jnp.dot(h, w_ref[:, j * 512:(j + 1) * 512], preferred_element_type=F32)
        a = a + b_ref[:, j * 512:(j + 1) * 512]
        gt = jnp.dot(h, w_ref[:, D_MODEL + j * 512:D_MODEL + (j + 1) * 512], preferred_element_type=F32)
        gt = gt + b_ref[:, D_MODEL + j * 512:D_MODEL + (j + 1) * 512]
        u_ref[:, j * 512:(j + 1) * 512] = a * (1.0 / (1.0 + jnp.exp(-gt)))


def _conv_pw1(x, w1, w2, yg, g, w_bf16, b):
    n = x.shape[0]
    tm = TM_CONV
    nb = n // tm
    return pl.pallas_call(
        _conv_pw1_kernel,
        grid=(nb,),
        in_specs=[
            pl.BlockSpec((tm, D_MODEL), lambda i: (i, 0)),
            pl.BlockSpec((tm, 1), lambda i: (i, 0)),
            pl.BlockSpec((tm, 1), lambda i: (i, 0)),
            pl.BlockSpec((tm, HALF), lambda i: (i, 0)),
            pl.BlockSpec((tm, HALF), lambda i: (i + nb, 0)),
            pl.BlockSpec((1, D_MODEL), lambda i: (0, 0)),
            pl.BlockSpec((D_MODEL, 2 * D_MODEL), lambda i: (0, 0)),
            pl.BlockSpec((1, 2 * D_MODEL), lambda i: (0, 0)),
        ],
        out_specs=[pl.BlockSpec((tm, D_MODEL), lambda i: (i, 0)), pl.BlockSpec((tm, D_MODEL), lambda i: (i, 0))],
        out_shape=[jax.ShapeDtypeStruct((n, D_MODEL), F32), jax.ShapeDtypeStruct((n, D_MODEL), F32)],
        compiler_params=_cparams(("arbitrary",)),
        name="conv_pw1",
    )(x, w1, w2, yg, yg, g, w_bf16, b)


def _conv_core_kernel(u_ref, halo_ref, x_ref, wdw_ref, bdw_ref, lng_ref, lnb_ref, w2_ref, b2_ref,
                      out_ref, win_ref, z_ref):
    tm = TM_CONV
    first = pl.program_id(0) == 0
    halo = halo_ref[...]
    win_ref[0:CONV_HALO, :] = jnp.where(first, jnp.zeros_like(halo), halo)
    win_ref[CONV_HALO:CONV_HALO + tm, :] = u_ref[...]
    off = CONV_HALO - (CONV_WIDTH - 1)
    rb = CONV_ROWS
    for cc in range(D_MODEL // 128):
        cs = slice(cc * 128, (cc + 1) * 128)
        for r0 in range(0, tm, rb):
            z = bdw_ref[:, cs]
            for b in range(8):
                rows = rb if b == 0 else rb + 8
                q = None
                for o in range(off, off + CONV_WIDTH):
                    if o % 8 != b:
                        continue
                    term = win_ref[r0 + o - b:r0 + o - b + rows, cs] * wdw_ref[o - off:o - off + 1, cs]
                    q = term if q is None else q + term
                z = z + (q if b == 0 else q[b:b + rb, :])
            z_ref[r0:r0 + rb, cs] = z
    z = z_ref[...]
    mu = jnp.mean(z, axis=-1, keepdims=True)
    zc = z - mu
    var = jnp.mean(zc * zc, axis=-1, keepdims=True)
    zn = zc * lax.rsqrt(var + NORM_EPS) * lng_ref[...] + lnb_ref[...]
    y = _silu(zn).astype(BF16)
    out_ref[...] = x_ref[...] + jnp.dot(y, w2_ref[...], preferred_element_type=F32) + b2_ref[...]


def _conv_core(u, x, w_dw, b_dw, ln_g, ln_b, w2_bf16, b2):
    n = x.shape[0]
    tm = TM_CONV
    r = tm // CONV_HALO
    wdw_pad = jnp.zeros((32, D_MODEL), F32).at[:CONV_WIDTH].set(w_dw)
    return pl.pallas_call(
        _conv_core_kernel,
        grid=(n // tm,),
        in_specs=[
            pl.BlockSpec((tm, D_MODEL), lambda i: (i, 0)),
            pl.BlockSpec((CONV_HALO, D_MODEL), lambda i: (jnp.maximum(i * r - 1, 0), 0)),
            pl.BlockSpec((tm, D_MODEL), lambda i: (i, 0)),
            pl.BlockSpec((32, D_MODEL), lambda i: (0, 0)),
            pl.BlockSpec((1, D_MODEL), lambda i: (0, 0)),
            pl.BlockSpec((1, D_MODEL), lambda i: (0, 0)),
            pl.BlockSpec((1, D_MODEL), lambda i: (0, 0)),
            pl.BlockSpec((D_MODEL, D_MODEL), lambda i: (0, 0)),
            pl.BlockSpec((1, D_MODEL), lambda i: (0, 0)),
        ],
        out_specs=pl.BlockSpec((tm, D_MODEL), lambda i: (i, 0)),
        out_shape=jax.ShapeDtypeStruct((n, D_MODEL), F32),
        scratch_shapes=[pltpu.VMEM((CONV_HALO + tm, D_MODEL), F32), pltpu.VMEM((tm, D_MODEL), F32)],
        compiler_params=_cparams(("arbitrary",)),
        name="conv_core",
    )(u, u, x, wdw_pad, b_dw, ln_g, ln_b, w2_bf16, b2)


def _router_kernel(x_ref, g_ref, wr_ref, br_ref, hp_ref, ri_ref, rw_ref, cnt_ref, carry_ref):
    t = T_ROUTE

    @pl.when(pl.program_id(0) == 0)
    def _():
        carry_ref[...] = jnp.zeros_like(carry_ref)

    h = _rms(x_ref[...], g_ref[...])
    hp_ref[...] = _pack_bf16_pairs(h)
    h_hi = h.astype(BF16)
    h_lo = (h - h_hi.astype(F32)).astype(BF16)
    w = wr_ref[...]
    w_hi = w.astype(BF16)
    w_lo = (w - w_hi.astype(F32)).astype(BF16)
    dn = (((1,), (1,)), ((), ()))
    p = lax.dot_general(jnp.concatenate([w_hi, w_lo], axis=0), h_hi, dn, preferred_element_type=F32)
    nr = wr_ref.shape[0]
    logits = p[0:nr] + p[nr:2 * nr] + lax.dot_general(w_hi, h_lo, dn, preferred_element_type=F32)
    logits = logits + br_ref[:, 0:1]

    best = logits[0:1]
    gi = jnp.zeros((1, t), I32)
    for j in range(1, MOE_GROUPS):
        r = logits[j:j + 1]
        up = r > best
        gi = jnp.where(up, j, gi)
        best = jnp.where(up, r, best)
    den = jnp.zeros((1, t), F32)
    for j in range(MOE_GROUPS):
        den = den + jnp.exp(logits[j:j + 1] - best)
    gate_g = 1.0 / den

    sel = logits[8:8 + MOE_EPG]
    for j in range(1, MOE_GROUPS):
        sel = jnp.where(gi == j, logits[8 + j * MOE_EPG:8 + (j + 1) * MOE_EPG], sel)

    m1 = sel[0:1]
    i1 = jnp.zeros((1, t), I32)
    for j in range(1, MOE_EPG):
        r = sel[j:j + 1]
        up = r > m1
        i1 = jnp.where(up, j, i1)
        m1 = jnp.where(up, r, m1)
    m2 = jnp.full((1, t), -jnp.inf, F32)
    i2 = jnp.zeros((1, t), I32)
    started = jnp.zeros((1, t), jnp.bool_)
    for j in range(MOE_EPG):
        r = sel[j:j + 1]
        ok = i1 != j
        up = ok & ((r > m2) | jnp.logical_not(started))
        i2 = jnp.where(up, j, i2)
        m2 = jnp.where(up, r, m2)
        started = started | ok
    e21 = jnp.exp(m2 - m1)
    p1 = 1.0 / (1.0 + e21)
    w1 = gate_g * p1
    w2 = gate_g * (e21 * p1)
    eid1 = gi * MOE_EPG + i1
    eid2 = gi * MOE_EPG + i2

    eio = lax.broadcasted_iota(I32, (MOE_EXPERTS, t), 0)
    oh1 = eio == eid1
    oh2 = eio == eid2
    oh = (oh1 | oh2).astype(F32)
    rio = lax.broadcasted_iota(I32, (t, t), 0)
    cio = lax.broadcasted_iota(I32, (t, t), 1)
    upper = (rio < cio).astype(BF16)
    cum = jnp.dot(oh.astype(BF16), upper, preferred_element_type=F32) + carry_ref[:, 0:1]
    rank1 = jnp.sum(jnp.where(oh1, cum, 0.0), axis=0, keepdims=True)
    rank2 = jnp.sum(jnp.where(oh2, cum, 0.0), axis=0, keepdims=True)
    carry_ref[...] = carry_ref[...] + jnp.sum(oh, axis=1, keepdims=True)
    cnt_ref[...] = carry_ref[...]

    zi = jnp.zeros((4, t), I32)
    ri_ref[...] = jnp.concatenate([eid1, eid2, rank1.astype(I32), rank2.astype(I32), zi], axis=0)
    zf = jnp.zeros((6, t), F32)
    rw_ref[...] = jnp.concatenate([w1, w2, zf], axis=0)


def _router(x, g, w_rg, b_rg, w_re, b_re):
    n = x.shape[0]
    t = T_ROUTE
    wr = jnp.zeros((40, D_MODEL), F32).at[0:MOE_GROUPS].set(w_rg.T).at[8:40].set(w_re.T)
    br = jnp.zeros((40,), F32).at[0:MOE_GROUPS].set(b_rg).at[8:40].set(b_re)
    br = jnp.broadcast_to(br[:, None], (40, 128))
    return pl.pallas_call(
        _router_kernel,
        grid=(n // t,),
        in_specs=[
            pl.BlockSpec((t, D_MODEL), lambda i: (i, 0)),
            pl.BlockSpec((1, D_MODEL), lambda i: (0, 0)),
            pl.BlockSpec((40, D_MODEL), lambda i: (0, 0)),
            pl.BlockSpec((40, 128), lambda i: (0, 0)),
        ],
        out_specs=[
            pl.BlockSpec((t, HALF), lambda i: (i, 0)),
            pl.BlockSpec((8, t), lambda i: (0, i)),
            pl.BlockSpec((8, t), lambda i: (0, i)),
            pl.BlockSpec((MOE_EXPERTS, 128), lambda i: (0, 0)),
        ],
        out_shape=[
            jax.ShapeDtypeStruct((n, HALF), U32),
            jax.ShapeDtypeStruct((8, n), I32),
            jax.ShapeDtypeStruct((8, n), F32),
            jax.ShapeDtypeStruct((MOE_EXPERTS, 128), F32),
        ],
        scratch_shapes=[pltpu.VMEM((MOE_EXPERTS, 128), F32)],
        compiler_params=_cparams(("arbitrary",)),
        name="moe_router",
    )(x, g, wr, br)


def _sc_mesh():
    return plsc.VectorSubcoreMesh(core_axis_name="c", subcore_axis_name="s",
                                  num_cores=SC_CORES, num_subcores=SC_SUBCORES)


def _sc_worker_id():
    return lax.axis_index("s") * SC_CORES + lax.axis_index("c")


def _sc_dispatch(hp, idx, zero_rows, total_rows):
    n = hp.shape[0]
    tpw = n // SC_WORKERS
    kd = tpw // SC_ROWS
    kp = idx.shape[1] - 2 * kd

    @functools.partial(
        pl.kernel, mesh=_sc_mesh(),
        out_type=jax.ShapeDtypeStruct((total_rows, HALF), U32),
        scratch_types=[pltpu.VMEM((2 * kd + kp, SC_ROWS), I32), pltpu.VMEM((SC_ROWS, HALF), U32)],
        name="moe_dispatch_sc",
    )
    def k(hp_hbm, idx_hbm, zero_hbm, xs_hbm, idx_v, rows_v):
        wid = _sc_worker_id()
        pltpu.sync_copy(idx_hbm.at[wid], idx_v)
        pltpu.sync_copy(zero_hbm, rows_v)
        for j in range(kp):
            pltpu.sync_copy(rows_v, xs_hbm.at[idx_v.at[2 * kd + j]])
        for c in range(kd):
            pltpu.sync_copy(hp_hbm.at[pl.ds(wid * tpw + c * SC_ROWS, SC_ROWS)], rows_v)
            pltpu.sync_copy(rows_v, xs_hbm.at[idx_v.at[c]])
            pltpu.sync_copy(rows_v, xs_hbm.at[idx_v.at[kd + c]])

    return k(hp, idx, zero_rows)


def _sc_gather(ys, idx):
    kg = idx.shape[1]
    rows_per_worker = kg * SC_ROWS

    @functools.partial(
        pl.kernel, mesh=_sc_mesh(),
        out_type=jax.ShapeDtypeStruct((SC_WORKERS * rows_per_worker, HALF), U32),
        scratch_types=[pltpu.VMEM((kg, SC_ROWS), I32), pltpu.VMEM((SC_ROWS, HALF), U32)],
        name="moe_gather_sc",
    )
    def k(ys_hbm, idx_hbm, yg_hbm, idx_v, rows_v):
        wid = _sc_worker_id()
        pltpu.sync_copy(idx_hbm.at[wid], idx_v)
        for c in range(kg):
            pltpu.sync_copy(ys_hbm.at[idx_v.at[c]], rows_v)
            pltpu.sync_copy(rows_v, yg_hbm.at[pl.ds(wid * rows_per_worker + c * SC_ROWS, SC_ROWS)])

    return k(ys, idx)


def _expert_kernel(blk_e_ref, nused_ref, first_ref, slot_ref, nxt_ref, xs_ref, wg_hbm, wu_hbm, wd_hbm,
                   ys_ref, wg_buf, wu_buf, wd_buf, sems, *, layer):
    step = pl.program_id(0)
    tm = TM_EXP

    def weight_copies(e, s):
        return (pltpu.make_async_copy(wg_hbm.at[layer, e], wg_buf.at[s], sems.at[s, 0]),
                pltpu.make_async_copy(wu_hbm.at[layer, e], wu_buf.at[s], sems.at[s, 1]),
                pltpu.make_async_copy(wd_hbm.at[layer, e], wd_buf.at[s], sems.at[s, 2]))

    def block(j):
        i = step * EXP_SUB + j
        rows = slice(j * tm, (j + 1) * tm)

        @pl.when(i < nused_ref[0])
        def _():
            s = slot_ref[i]

            if j == 0:
                @pl.when(i == 0)
                def _():
                    for c in weight_copies(blk_e_ref[0], 0):
                        c.start()

            @pl.when(first_ref[i] == 1)
            def _():
                for c in weight_copies(blk_e_ref[i], s):
                    c.wait()

                @pl.when(nxt_ref[i] >= 0)
                def _():
                    for c in weight_copies(nxt_ref[i], 1 - s):
                        c.start()

            lo, hi = _unpack_bf16_pairs(xs_ref[rows, :])
            xf = jnp.concatenate([lo, hi], axis=1)
            a = jnp.dot(xf, wg_buf[s], preferred_element_type=F32)
            b = jnp.dot(xf, wu_buf[s], preferred_element_type=F32)
            hm = _silu(a) * b
            y = jnp.dot(hm, wd_buf[s], preferred_element_type=F32)
            ys_ref[rows, :] = _pack_bf16_pairs(y)

        @pl.when(i >= nused_ref[0])
        def _():
            ys_ref[rows, :] = jnp.zeros((tm, HALF), U32)

    for j in range(EXP_SUB):
        block(j)


def _experts(xs, blk_e, nused, w_gate, w_up, w_down, layer):
    tm = TM_EXP
    p_rows = xs.shape[0] - MOE_EXPERTS * tm
    nblk = p_rows // tm
    pos = jnp.arange(nblk, dtype=I32)
    valid = pos < nused[0]
    prev_e = jnp.concatenate([jnp.full((1,), -1, I32), blk_e[:-1]])
    first = valid & (blk_e != prev_e)
    slot = (jnp.cumsum(first.astype(I32)) - 1) % 2
    first_pos = jnp.where(first, pos, nblk)
    next_first = jnp.concatenate([lax.cummin(first_pos, reverse=True)[1:], jnp.full((1,), nblk, I32)])
    nxt = jnp.where(next_first < nblk, blk_e[jnp.minimum(next_first, nblk - 1)], -1)

    def blk(i, be, nu, *_):
        return jnp.minimum(i, (nu[0] - 1) // EXP_SUB)

    grid_spec = pltpu.PrefetchScalarGridSpec(
        num_scalar_prefetch=5,
        grid=(nblk // EXP_SUB,),
        in_specs=[
            pl.BlockSpec((EXP_SUB * tm, HALF), lambda i, *sp: (blk(i, *sp), 0)),
            pl.BlockSpec(memory_space=pl.ANY),
            pl.BlockSpec(memory_space=pl.ANY),
            pl.BlockSpec(memory_space=pl.ANY),
        ],
        out_specs=pl.BlockSpec((EXP_SUB * tm, HALF), lambda i, *sp: (i, 0)),
        scratch_shapes=[
            pltpu.VMEM((2, D_MODEL, MOE_FF), F32),
            pltpu.VMEM((2, D_MODEL, MOE_FF), F32),
            pltpu.VMEM((2, MOE_FF, D_MODEL), F32),
            pltpu.SemaphoreType.DMA((2, 3)),
        ],
    )
    return pl.pallas_call(
        functools.partial(_expert_kernel, layer=layer),
        grid_spec=grid_spec,
        out_shape=jax.ShapeDtypeStruct((p_rows, HALF), U32),
        compiler_params=_cparams(("arbitrary",)),
        name="moe_experts",
    )(blk_e, nused, first.astype(I32), slot.astype(I32), nxt.astype(I32), xs, w_gate, w_up, w_down)


def _combine_kernel(x_ref, w1_ref, w2_ref, fg_ref, y1_ref, y2_ref, out_ref):
    lo1, hi1 = _unpack_bf16_pairs(y1_ref[...])
    lo2, hi2 = _unpack_bf16_pairs(y2_ref[...])
    w1 = w1_ref[...]
    w2 = w2_ref[...]
    x = x_ref[...]
    o_lo = x[:, :HALF] + w1 * lo1 + w2 * lo2
    o_hi = x[:, HALF:] + w1 * hi1 + w2 * hi2
    ms = (jnp.sum(o_lo * o_lo, axis=-1, keepdims=True)
          + jnp.sum(o_hi * o_hi, axis=-1, keepdims=True)) * (1.0 / D_MODEL)
    sc = lax.rsqrt(ms + NORM_EPS)
    o_lo = o_lo * sc * fg_ref[:, :HALF]
    o_hi = o_hi * sc * fg_ref[:, HALF:]
    out_ref[:, :HALF] = o_lo
    out_ref[:, HALF:] = o_hi


def _combine(yg, x, w1, w2, final_g):
    n = x.shape[0]
    td = T_COMB
    nb = n // td
    return pl.pallas_call(
        _combine_kernel,
        grid=(nb,),
        in_specs=[
            pl.BlockSpec((td, D_MODEL), lambda i: (i, 0)),
            pl.BlockSpec((td, 1), lambda i: (i, 0)),
            pl.BlockSpec((td, 1), lambda i: (i, 0)),
            pl.BlockSpec((1, D_MODEL), lambda i: (0, 0)),
            pl.BlockSpec((td, HALF), lambda i: (i, 0)),
            pl.BlockSpec((td, HALF), lambda i: (i + nb, 0)),
        ],
        out_specs=pl.BlockSpec((td, D_MODEL), lambda i: (i, 0)),
        out_shape=jax.ShapeDtypeStruct((n, D_MODEL), F32),
        compiler_params=_cparams(("arbitrary",)),
        name="moe_combine",
    )(x, w1, w2, final_g, yg, yg)


def _moe(x, g, w_rg, b_rg, w_re, b_re, w_gate, w_up, w_down, layer):
    n = x.shape[0]
    tm = TM_EXP
    p_rows = 2 * n + MOE_EXPERTS * tm
    nblk = p_rows // tm
    hp, ri, rw, cnt = _router(x, g, w_rg, b_rg, w_re, b_re)
    counts = cnt[:, 0].astype(I32)
    pcounts = (counts + tm - 1) // tm * tm
    pend = jnp.cumsum(pcounts)
    pstart = pend - pcounts
    eio = jnp.arange(MOE_EXPERTS, dtype=I32)[:, None]
    dest1 = jnp.sum(jnp.where(ri[0][None, :] == eio, pstart[:, None], 0), axis=0) + ri[2]
    dest2 = jnp.sum(jnp.where(ri[1][None, :] == eio, pstart[:, None], 0), axis=0) + ri[3]
    blk_start = jnp.arange(nblk, dtype=I32) * tm
    blk_e = jnp.minimum(jnp.sum((pend[None, :] <= blk_start[:, None]).astype(I32), axis=1), MOE_EXPERTS - 1)
    nused = jnp.maximum(pend[-1] // tm, 1).astype(I32).reshape(1)
    r = jnp.arange(tm, dtype=I32)[None, :]
    pad_slot = jnp.where(r < (pcounts - counts)[:, None], (pstart + counts)[:, None] + r, p_rows + eio * tm + r)
    kd = n // SC_WORKERS // SC_ROWS
    idx = jnp.concatenate([dest1.reshape(SC_WORKERS, kd, SC_ROWS), dest2.reshape(SC_WORKERS, kd, SC_ROWS),
                           pad_slot.reshape(SC_WORKERS, -1, SC_ROWS)], axis=1)
    zero_rows = jnp.zeros((SC_ROWS, HALF), U32)
    xs = _sc_dispatch(hp, idx, zero_rows, p_rows + MOE_EXPERTS * tm)
    ys = _experts(xs, blk_e, nused, w_gate, w_up, w_down, layer)
    gidx = jnp.concatenate([dest1, dest2]).reshape(SC_WORKERS, -1, SC_ROWS)
    yg = _sc_gather(ys, gidx)
    return yg, rw[0].reshape(n, 1), rw[1].reshape(n, 1)


def kernel(x, positions, norm_mix_g, norm_ffn_g, ret_w_in, ret_head_g, ret_w_out, conv_w_pw1, conv_b_pw1, conv_w_dw, conv_b_dw, conv_ln_g, conv_ln_b, conv_w_pw2, conv_b_pw2, moe_w_rg, moe_b_rg, moe_w_re, moe_b_re, moe_w_gate, moe_w_up, moe_w_down, final_norm_g):
    b, s, d = x.shape
    n = b * s
    xt = x.reshape(n, d)
    pos = positions.reshape(n, 1)
    fg = final_norm_g.reshape(1, d)

    q, k, v, gate = _ret_inproj(xt, pos, norm_mix_g[0].reshape(1, d), ret_w_in[0].astype(BF16))
    xt = _ret_core(q, k, v, gate, xt, ret_head_g[0].reshape(RET_V, 1), ret_w_out[0])
    yg, w1, w2 = _moe(xt, norm_ffn_g[0].reshape(1, d), moe_w_rg[0], moe_b_rg[0], moe_w_re[0], moe_b_re[0],
                      moe_w_gate, moe_w_up, moe_w_down, 0)

    xt, u = _conv_pw1(xt, w1, w2, yg, norm_mix_g[1].reshape(1, d), conv_w_pw1[0].astype(BF16),
                      conv_b_pw1[0].reshape(1, 2 * d))
    xt = _conv_core(u, xt, conv_w_dw[0], conv_b_dw[0].reshape(1, d), conv_ln_g[0].reshape(1, d),
                    conv_ln_b[0].reshape(1, d), conv_w_pw2[0].astype(BF16), conv_b_pw2[0].reshape(1, d))
    yg, w1, w2 = _moe(xt, norm_ffn_g[1].reshape(1, d), moe_w_rg[1], moe_b_rg[1], moe_w_re[1], moe_b_re[1],
                      moe_w_gate, moe_w_up, moe_w_down, 1)
    xt = _combine(yg, xt, w1, w2, fg)
    return xt.reshape(b, s, d)
```

```python
import functools
import math

import jax
import jax.numpy as jnp
import numpy as np
from jax import lax
from jax.experimental import pallas as pl
from jax.experimental.pallas import tpu as pltpu
from jax.experimental.pallas import tpu_sc as plsc

F32 = jnp.float32
BF16 = jnp.bfloat16
U32 = jnp.uint32
I32 = jnp.int32

D_MODEL = 1024
RET_HEADS = 4
RET_DK = 256
RET_DV = 512
RET_QK = RET_HEADS * RET_DK
RET_V = RET_HEADS * RET_DV
ROPE_BASE = 10000.0
CONV_WIDTH = 31
MOE_GROUPS = 4
MOE_EPG = 8
MOE_EXPERTS = MOE_GROUPS * MOE_EPG
MOE_FF = 512
NORM_EPS = 1e-6

TM_PROJ = 512
PROJ_ROWS = 256
RET_C = 256
RET_STEP = 512
TM_CONV = 512
CONV_HALO = 32
CONV_ROWS = 128
T_ROUTE = 512
TM_EXP = 256
EXP_SUB = 4
T_COMB = 512
SC_CORES = 2
SC_SUBCORES = 16
SC_WORKERS = SC_CORES * SC_SUBCORES
SC_ROWS = 64
HALF = D_MODEL // 2

VMEM_LIMIT = 56 * 1024 * 1024


def _cparams(sem, flags=None):
    return pltpu.CompilerParams(dimension_semantics=sem, vmem_limit_bytes=VMEM_LIMIT, flags=flags)


def _rms(x, g):
    ms = jnp.mean(x * x, axis=-1, keepdims=True)
    return x * lax.rsqrt(ms + NORM_EPS) * g


def _silu(x):
    return x * (1.0 / (1.0 + jnp.exp(-x)))


def _pack_bf16_pairs(y):
    lo = pltpu.bitcast(y[:, :HALF].astype(BF16).astype(F32), U32)
    hi = pltpu.bitcast(y[:, HALF:].astype(BF16).astype(F32), U32)
    return (hi & jnp.uint32(0xFFFF0000)) | (lo >> 16)


def _unpack_bf16_pairs(p):
    lo = pltpu.bitcast(p << 16, F32)
    hi = pltpu.bitcast(p & jnp.uint32(0xFFFF0000), F32)
    return lo, hi


def _ret_inproj_kernel(x_ref, pos_ref, g_ref, inv_ref, w_ref, q_ref, k_ref, v_ref, gate_ref):
    half = RET_DK // 2
    kscale = RET_DK ** -0.5
    for r0 in range(0, TM_PROJ, PROJ_ROWS):
        rs = slice(r0, r0 + PROJ_ROWS)
        h = _rms(x_ref[rs, :], g_ref[...]).astype(BF16)
        ang = pos_ref[rs, :].astype(F32) * inv_ref[...]
        cos = jnp.cos(ang)
        sin = jnp.sin(ang)
        for hd in range(RET_HEADS):
            for base, out_ref, cs, sn in ((0, q_ref, cos, sin), (RET_QK, k_ref, cos * kscale, sin * kscale)):
                c0 = base + hd * RET_DK
                t = jnp.dot(h, w_ref[:, c0:c0 + RET_DK], preferred_element_type=F32)
                t1 = t[:, :half]
                t2 = t[:, half:]
                out_ref[rs, hd * RET_DK:hd * RET_DK + half] = (t1 * cs - t2 * sn).astype(BF16)
                out_ref[rs, hd * RET_DK + half:(hd + 1) * RET_DK] = (t1 * sn + t2 * cs).astype(BF16)
        for j in range(RET_V // 512):
            c0 = 2 * RET_QK + j * 512
            v_ref[rs, j * 512:(j + 1) * 512] = jnp.dot(
                h, w_ref[:, c0:c0 + 512], preferred_element_type=F32).astype(BF16)
            c1 = 2 * RET_QK + RET_V + j * 512
            gate_ref[rs, j * 512:(j + 1) * 512] = jnp.dot(
                h, w_ref[:, c1:c1 + 512], preferred_element_type=F32).astype(BF16)


def _ret_inproj(x, pos, g, w_in_bf16):
    n = x.shape[0]
    half = RET_DK // 2
    inv = (ROPE_BASE ** (-jnp.arange(half, dtype=F32) / half)).reshape(1, half)
    tm = TM_PROJ
    return pl.pallas_call(
        _ret_inproj_kernel,
        grid=(n // tm,),
        in_specs=[
            pl.BlockSpec((tm, D_MODEL), lambda i: (i, 0)),
            pl.BlockSpec((tm, 1), lambda i: (i, 0)),
            pl.BlockSpec((1, D_MODEL), lambda i: (0, 0)),
            pl.BlockSpec((1, half), lambda i: (0, 0)),
            pl.BlockSpec(w_in_bf16.shape, lambda i: (0, 0)),
        ],
        out_specs=[
            pl.BlockSpec((tm, RET_QK), lambda i: (i, 0)),
            pl.BlockSpec((tm, RET_QK), lambda i: (i, 0)),
            pl.BlockSpec((tm, RET_V), lambda i: (i, 0)),
            pl.BlockSpec((tm, RET_V), lambda i: (i, 0)),
        ],
        out_shape=[
            jax.ShapeDtypeStruct((n, RET_QK), BF16),
            jax.ShapeDtypeStruct((n, RET_QK), BF16),
            jax.ShapeDtypeStruct((n, RET_V), BF16),
            jax.ShapeDtypeStruct((n, RET_V), BF16),
        ],
        compiler_params=_cparams(("arbitrary",)),
        name="ret_inproj",
    )(x, pos, g, inv, w_in_bf16)


def _ret_core_kernel(cdec_ref, q_ref, k_ref, v_ref, gate_ref, x_ref, hg_ref, intra_ref, cross_ref,
                     kdec_ref, wo_ref, out_ref, state_ref, y_ref, wos_ref):
    @pl.when(pl.program_id(0) == 0)
    def _():
        state_ref[...] = jnp.zeros_like(state_ref)
        wos_ref[...] = (wo_ref[...] * hg_ref[...]).astype(BF16)

    for r0 in range(0, RET_STEP, RET_C):
        rs = slice(r0, r0 + RET_C)
        for hd in range(RET_HEADS):
            q = q_ref[rs, hd * RET_DK:(hd + 1) * RET_DK]
            k = k_ref[rs, hd * RET_DK:(hd + 1) * RET_DK]
            v = v_ref[rs, hd * RET_DV:(hd + 1) * RET_DV]
            state = state_ref[hd]
            scores = lax.dot_general(q, k, (((1,), (1,)), ((), ())), preferred_element_type=F32)
            scores = (scores * intra_ref[hd]).astype(BF16)
            o = jnp.dot(scores, v, preferred_element_type=F32)
            cross = cross_ref[hd]
            o_cross = jnp.dot(q, state.astype(BF16), preferred_element_type=F32)
            o = o + o_cross * jnp.concatenate([cross] * (RET_DV // 128), axis=1)
            kdec = kdec_ref[hd]
            kd = (k.astype(F32) * jnp.concatenate([kdec] * (RET_DK // 128), axis=1)).astype(BF16)
            upd = lax.dot_general(kd, v, (((0,), (0,)), ((), ())), preferred_element_type=F32)
            state_ref[hd] = state * cdec_ref[hd] + upd
            ms = jnp.mean(o * o, axis=-1, keepdims=True)
            on = o * lax.rsqrt(ms + NORM_EPS)
            gt = gate_ref[rs, hd * RET_DV:(hd + 1) * RET_DV].astype(F32)
            y_ref[rs, hd * RET_DV:(hd + 1) * RET_DV] = (_silu(gt) * on).astype(BF16)
        out_ref[rs, :] = x_ref[rs, :] + jnp.dot(y_ref[rs, :], wos_ref[...], preferred_element_type=F32)


def _ret_core(q, k, v, gate, x, head_g_col, w_out):
    n = x.shape[0]
    c = RET_C
    log_gamma = jnp.log1p(-(2.0 ** (-5.0 - jnp.arange(RET_HEADS, dtype=F32))))
    idx = jnp.arange(c, dtype=F32)
    diff = idx[:, None] - idx[None, :]
    intra = jnp.where(diff >= 0, jnp.exp(log_gamma[:, None, None] * jnp.maximum(diff, 0.0)), 0.0)
    cross = jnp.broadcast_to(jnp.exp(log_gamma[:, None] * (idx + 1.0))[:, :, None], (RET_HEADS, c, 128))
    kdec = jnp.broadcast_to(jnp.exp(log_gamma[:, None] * (c - 1.0 - idx))[:, :, None], (RET_HEADS, c, 128))
    cdec = jnp.exp(log_gamma * c)
    return pl.pallas_call(
        _ret_core_kernel,
        grid=(n // RET_STEP,),
        in_specs=[
            pl.BlockSpec(memory_space=pltpu.SMEM),
            pl.BlockSpec((RET_STEP, RET_QK), lambda i: (i, 0)),
            pl.BlockSpec((RET_STEP, RET_QK), lambda i: (i, 0)),
            pl.BlockSpec((RET_STEP, RET_V), lambda i: (i, 0)),
            pl.BlockSpec((RET_STEP, RET_V), lambda i: (i, 0)),
            pl.BlockSpec((RET_STEP, D_MODEL), lambda i: (i, 0)),
            pl.BlockSpec((RET_V, 1), lambda i: (0, 0)),
            pl.BlockSpec((RET_HEADS, c, c), lambda i: (0, 0, 0)),
            pl.BlockSpec((RET_HEADS, c, 128), lambda i: (0, 0, 0)),
            pl.BlockSpec((RET_HEADS, c, 128), lambda i: (0, 0, 0)),
            pl.BlockSpec((RET_V, D_MODEL), lambda i: (0, 0)),
        ],
        out_specs=pl.BlockSpec((RET_STEP, D_MODEL), lambda i: (i, 0)),
        out_shape=jax.ShapeDtypeStruct((n, D_MODEL), F32),
        scratch_shapes=[
            pltpu.VMEM((RET_HEADS, RET_DK, RET_DV), F32),
            pltpu.VMEM((RET_STEP, RET_V), BF16),
            pltpu.VMEM((RET_V, D_MODEL), BF16),
        ],
        compiler_params=_cparams(("arbitrary",)),
        name="ret_core",
    )(cdec, q, k, v, gate, x, head_g_col, intra, cross, kdec, w_out)


def _conv_pw1_kernel(x_ref, w1_ref, w2_ref, y1_ref, y2_ref, g_ref, w_ref, b_ref, xo_ref, u_ref):
    lo1, hi1 = _unpack_bf16_pairs(y1_ref[...])
    lo2, hi2 = _unpack_bf16_pairs(y2_ref[...])
    w1 = w1_ref[...]
    w2 = w2_ref[...]
    x = jnp.concatenate([x_ref[:, :HALF] + w1 * lo1 + w2 * lo2, x_ref[:, HALF:] + w1 * hi1 + w2 * hi2], axis=1)
    xo_ref[...] = x
    h = _rms(x, g_ref[...]).astype(BF16)
    for j in range(D_MODEL // 512):
        a = jnp.dot(h, w_ref[:, j * 512:(j + 1) * 512], preferred_element_type=F32)
        a = a + b_ref[:, j * 512:(j + 1) * 512]
        gt = jnp.dot(h, w_ref[:, D_MODEL + j * 512:D_MODEL + (j + 1) * 512], preferred_element_type=F32)
        gt = gt + b_ref[:, D_MODEL + j * 512:D_MODEL + (j + 1) * 512]
        u_ref[:, j * 512:(j + 1) * 512] = a * (1.0 / (1.0 + jnp.exp(-gt)))


def _conv_pw1(x, w1, w2, yg, g, w_bf16, b):
    n = x.shape[0]
    tm = TM_CONV
    nb = n // tm
    return pl.pallas_call(
        _conv_pw1_kernel,
        grid=(nb,),
        in_specs=[
            pl.BlockSpec((tm, D_MODEL), lambda i: (i, 0)),
            pl.BlockSpec((tm, 1), lambda i: (i, 0)),
            pl.BlockSpec((tm, 1), lambda i: (i, 0)),
            pl.BlockSpec((tm, HALF), lambda i: (i, 0)),
            pl.BlockSpec((tm, HALF), lambda i: (i + nb, 0)),
            pl.BlockSpec((1, D_MODEL), lambda i: (0, 0)),
            pl.BlockSpec((D_MODEL, 2 * D_MODEL), lambda i: (0, 0)),
            pl.BlockSpec((1, 2 * D_MODEL), lambda i: (0, 0)),
        ],
        out_specs=[pl.BlockSpec((tm, D_MODEL), lambda i: (i, 0)), pl.BlockSpec((tm, D_MODEL), lambda i: (i, 0))],
        out_shape=[jax.ShapeDtypeStruct((n, D_MODEL), F32), jax.ShapeDtypeStruct((n, D_MODEL), F32)],
        compiler_params=_cparams(("arbitrary",)),
        name="conv_pw1",
    )(x, w1, w2, yg, yg, g, w_bf16, b)


def _conv_core_kernel(u_ref, halo_ref, x_ref, wdw_ref, bdw_ref, lng_ref, lnb_ref, w2_ref, b2_ref,
                      out_ref, win_ref, z_ref):
    tm = TM_CONV
    first = pl.program_id(0) == 0
    halo = halo_ref[...]
    win_ref[0:CONV_HALO, :] = jnp.where(first, jnp.zeros_like(halo), halo)
    win_ref[CONV_HALO:CONV_HALO + tm, :] = u_ref[...]
    off = CONV_HALO - (CONV_WIDTH - 1)
    rb = CONV_ROWS
    for cc in range(D_MODEL // 128):
        cs = slice(cc * 128, (cc + 1) * 128)
        for r0 in range(0, tm, rb):
            z = bdw_ref[:, cs]
            for b in range(8):
                rows = rb if b == 0 else rb + 8
                q = None
                for o in range(off, off + CONV_WIDTH):
                    if o % 8 != b:
                        continue
                    term = win_ref[r0 + o - b:r0 + o - b + rows, cs] * wdw_ref[o - off:o - off + 1, cs]
                    q = term if q is None else q + term
                z = z + (q if b == 0 else q[b:b + rb, :])
            z_ref[r0:r0 + rb, cs] = z
    z = z_ref[...]
    mu = jnp.mean(z, axis=-1, keepdims=True)
    zc = z - mu
    var = jnp.mean(zc * zc, axis=-1, keepdims=True)
    zn = zc * lax.rsqrt(var + NORM_EPS) * lng_ref[...] + lnb_ref[...]
    y = _silu(zn).astype(BF16)
    out_ref[...] = x_ref[...] + jnp.dot(y, w2_ref[...], preferred_element_type=F32) + b2_ref[...]


def _conv_core(u, x, w_dw, b_dw, ln_g, ln_b, w2_bf16, b2):
    n = x.shape[0]
    tm = TM_CONV
    r = tm // CONV_HALO
    wdw_pad = jnp.zeros((32, D_MODEL), F32).at[:CONV_WIDTH].set(w_dw)
    return pl.pallas_call(
        _conv_core_kernel,
        grid=(n // tm,),
        in_specs=[
            pl.BlockSpec((tm, D_MODEL), lambda i: (i, 0)),
            pl.BlockSpec((CONV_HALO, D_MODEL), lambda i: (jnp.maximum(i * r - 1, 0), 0)),
            pl.BlockSpec((tm, D_MODEL), lambda i: (i, 0)),
            pl.BlockSpec((32, D_MODEL), lambda i: (0, 0)),
            pl.BlockSpec((1, D_MODEL), lambda i: (0, 0)),
            pl.BlockSpec((1, D_MODEL), lambda i: (0, 0)),
            pl.BlockSpec((1, D_MODEL), lambda i: (0, 0)),
            pl.BlockSpec((D_MODEL, D_MODEL), lambda i: (0, 0)),
            pl.BlockSpec((1, D_MODEL), lambda i: (0, 0)),
        ],
        out_specs=pl.BlockSpec((tm, D_MODEL), lambda i: (i, 0)),
        out_shape=jax.ShapeDtypeStruct((n, D_MODEL), F32),
        scratch_shapes=[pltpu.VMEM((CONV_HALO + tm, D_MODEL), F32), pltpu.VMEM((tm, D_MODEL), F32)],
        compiler_params=_cparams(("arbitrary",)),
        name="conv_core",
    )(u, u, x, wdw_pad, b_dw, ln_g, ln_b, w2_bf16, b2)


def _router_kernel(x_ref, g_ref, wr_ref, br_ref, hp_ref, ri_ref, rw_ref, cnt_ref, carry_ref):
    t = T_ROUTE

    @pl.when(pl.program_id(0) == 0)
    def _():
        carry_ref[...] = jnp.zeros_like(carry_ref)

    h = _rms(x_ref[...], g_ref[...])
    hp_ref[...] = _pack_bf16_pairs(h)
    h_hi = h.astype(BF16)
    h_lo = (h - h_hi.astype(F32)).astype(BF16)
    w = wr_ref[...]
    w_hi = w.astype(BF16)
    w_lo = (w - w_hi.astype(F32)).astype(BF16)
    dn = (((1,), (1,)), ((), ()))
    p = lax.dot_general(jnp.concatenate([w_hi, w_lo], axis=0), h_hi, dn, preferred_element_type=F32)
    nr = wr_ref.shape[0]
    logits = p[0:nr] + p[nr:2 * nr] + lax.dot_general(w_hi, h_lo, dn, preferred_element_type=F32)
    logits = logits + br_ref[:, 0:1]

    best = logits[0:1]
    gi = jnp.zeros((1, t), I32)
    for j in range(1, MOE_GROUPS):
        r = logits[j:j + 1]
        up = r > best
        gi = jnp.where(up, j, gi)
        best = jnp.where(up, r, best)
    den = jnp.zeros((1, t), F32)
    for j in range(MOE_GROUPS):
        den = den + jnp.exp(logits[j:j + 1] - best)
    gate_g = 1.0 / den

    sel = logits[8:8 + MOE_EPG]
    for j in range(1, MOE_GROUPS):
        sel = jnp.where(gi == j, logits[8 + j * MOE_EPG:8 + (j + 1) * MOE_EPG], sel)

    m1 = sel[0:1]
    i1 = jnp.zeros((1, t), I32)
    for j in range(1, MOE_EPG):
        r = sel[j:j + 1]
        up = r > m1
        i1 = jnp.where(up, j, i1)
        m1 = jnp.where(up, r, m1)
    m2 = jnp.full((1, t), -jnp.inf, F32)
    i2 = jnp.zeros((1, t), I32)
    started = jnp.zeros((1, t), jnp.bool_)
    for j in range(MOE_EPG):
        r = sel[j:j + 1]
        ok = i1 != j
        up = ok & ((r > m2) | jnp.logical_not(started))
        i2 = jnp.where(up, j, i2)
        m2 = jnp.where(up, r, m2)
        started = started | ok
    e21 = jnp.exp(m2 - m1)
    p1 = 1.0 / (1.0 + e21)
    w1 = gate_g * p1
    w2 = gate_g * (e21 * p1)
    eid1 = gi * MOE_EPG + i1
    eid2 = gi * MOE_EPG + i2

    eio = lax.broadcasted_iota(I32, (MOE_EXPERTS, t), 0)
    oh1 = eio == eid1
    oh2 = eio == eid2
    oh = (oh1 | oh2).astype(F32)
    rio = lax.broadcasted_iota(I32, (t, t), 0)
    cio = lax.broadcasted_iota(I32, (t, t), 1)
    upper = (rio < cio).astype(BF16)
    cum = jnp.dot(oh.astype(BF16), upper, preferred_element_type=F32) + carry_ref[:, 0:1]
    rank1 = jnp.sum(jnp.where(oh1, cum, 0.0), axis=0, keepdims=True)
    rank2 = jnp.sum(jnp.where(oh2, cum, 0.0), axis=0, keepdims=True)
    carry_ref[...] = carry_ref[...] + jnp.sum(oh, axis=1, keepdims=True)
    cnt_ref[...] = carry_ref[...]

    zi = jnp.zeros((4, t), I32)
    ri_ref[...] = jnp.concatenate([eid1, eid2, rank1.astype(I32), rank2.astype(I32), zi], axis=0)
    zf = jnp.zeros((6, t), F32)
    rw_ref[...] = jnp.concatenate([w1, w2, zf], axis=0)


def _router(x, g, w_rg, b_rg, w_re, b_re):
    n = x.shape[0]
    t = T_ROUTE
    wr = jnp.zeros((40, D_MODEL), F32).at[0:MOE_GROUPS].set(w_rg.T).at[8:40].set(w_re.T)
    br = jnp.zeros((40,), F32).at[0:MOE_GROUPS].set(b_rg).at[8:40].set(b_re)
    br = jnp.broadcast_to(br[:, None], (40, 128))
    return pl.pallas_call(
        _router_kernel,
        grid=(n // t,),
        in_specs=[
            pl.BlockSpec((t, D_MODEL), lambda i: (i, 0)),
            pl.BlockSpec((1, D_MODEL), lambda i: (0, 0)),
            pl.BlockSpec((40, D_MODEL), lambda i: (0, 0)),
            pl.BlockSpec((40, 128), lambda i: (0, 0)),
        ],
        out_specs=[
            pl.BlockSpec((t, HALF), lambda i: (i, 0)),
            pl.BlockSpec((8, t), lambda i: (0, i)),
            pl.BlockSpec((8, t), lambda i: (0, i)),
            pl.BlockSpec((MOE_EXPERTS, 128), lambda i: (0, 0)),
        ],
        out_shape=[
            jax.ShapeDtypeStruct((n, HALF), U32),
            jax.ShapeDtypeStruct((8, n), I32),
            jax.ShapeDtypeStruct((8, n), F32),
            jax.ShapeDtypeStruct((MOE_EXPERTS, 128), F32),
        ],
        scratch_shapes=[pltpu.VMEM((MOE_EXPERTS, 128), F32)],
        compiler_params=_cparams(("arbitrary",)),
        name="moe_router",
    )(x, g, wr, br)


def _sc_mesh():
    return plsc.VectorSubcoreMesh(core_axis_name="c", subcore_axis_name="s",
                                  num_cores=SC_CORES, num_subcores=SC_SUBCORES)


def _sc_worker_id():
    return lax.axis_index("s") * SC_CORES + lax.axis_index("c")


def _sc_dispatch(hp, idx, zero_rows, total_rows):
    n = hp.shape[0]
    tpw = n // SC_WORKERS
    kd = tpw // SC_ROWS
    kp = idx.shape[1] - 2 * kd

    @functools.partial(
        pl.kernel, mesh=_sc_mesh(),
        out_type=jax.ShapeDtypeStruct((total_rows, HALF), U32),
        scratch_types=[
            pltpu.VMEM((2 * kd + kp, SC_ROWS), I32),
            pltpu.VMEM((SC_ROWS, HALF), U32), pltpu.VMEM((SC_ROWS, HALF), U32), pltpu.VMEM((SC_ROWS, HALF), U32),
            pltpu.SemaphoreType.DMA((2,)), pltpu.SemaphoreType.DMA((2,)), pltpu.SemaphoreType.DMA,
        ],
        name="moe_dispatch_sc",
    )
    def k(hp_hbm, idx_hbm, zero_hbm, xs_hbm, idx_v, buf0, buf1, zbuf, load_sem, scat_sem, pad_sem):
        wid = _sc_worker_id()
        bufs = (buf0, buf1)
        pltpu.sync_copy(idx_hbm.at[wid], idx_v)

        def load(c):
            return pltpu.make_async_copy(hp_hbm.at[pl.ds(wid * tpw + c * SC_ROWS, SC_ROWS)], bufs[c % 2],
                                         load_sem.at[c % 2])

        def scatters(c):
            return (pltpu.make_async_copy(bufs[c % 2], xs_hbm.at[idx_v.at[c]], scat_sem.at[c % 2]),
                    pltpu.make_async_copy(bufs[c % 2], xs_hbm.at[idx_v.at[kd + c]], scat_sem.at[c % 2]))

        load(0).start()
        pltpu.sync_copy(zero_hbm, zbuf)
        pads = [pltpu.make_async_copy(zbuf, xs_hbm.at[idx_v.at[2 * kd + j]], pad_sem) for j in range(kp)]
        for p in pads:
            p.start()
        for c in range(kd):
            load(c).wait()
            for d in scatters(c):
                d.start()
            if c + 1 < kd:
                if c >= 1:
                    for d in scatters(c - 1):
                        d.wait()
                load(c + 1).start()
        for c in range(max(kd - 2, 0), kd):
            for d in scatters(c):
                d.wait()
        for p in pads:
            p.wait()

    return k(hp, idx, zero_rows)


def _sc_gather(ys, idx):
    kg = idx.shape[1]
    rows_per_worker = kg * SC_ROWS

    @functools.partial(
        pl.kernel, mesh=_sc_mesh(),
        out_type=jax.ShapeDtypeStruct((SC_WORKERS * rows_per_worker, HALF), U32),
        scratch_types=[
            pltpu.VMEM((kg, SC_ROWS), I32),
            pltpu.VMEM((SC_ROWS, HALF), U32), pltpu.VMEM((SC_ROWS, HALF), U32),
            pltpu.SemaphoreType.DMA((2,)), pltpu.SemaphoreType.DMA((2,)),
        ],
        name="moe_gather_sc",
    )
    def k(ys_hbm, idx_hbm, yg_hbm, idx_v, buf0, buf1, gat_sem, out_sem):
        wid = _sc_worker_id()
        bufs = (buf0, buf1)
        pltpu.sync_copy(idx_hbm.at[wid], idx_v)

        def gather(c):
            return pltpu.make_async_copy(ys_hbm.at[idx_v.at[c]], bufs[c % 2], gat_sem.at[c % 2])

        def store(c):
            return pltpu.make_async_copy(bufs[c % 2],
                                         yg_hbm.at[pl.ds(wid * rows_per_worker + c * SC_ROWS, SC_ROWS)],
                                         out_sem.at[c % 2])

        gather(0).start()
        for c in range(kg):
            if c + 1 < kg:
                if c >= 1:
                    store(c - 1).wait()
                gather(c + 1).start()
            gather(c).wait()
            store(c).start()
        for c in range(max(kg - 2, 0), kg):
            store(c).wait()

    return k(ys, idx)


def _expert_kernel(blk_e_ref, nused_ref, first_ref, slot_ref, nxt_ref, xs_ref, wg_hbm, wu_hbm, wd_hbm,
                   ys_ref, wg_buf, wu_buf, wd_buf, sems, *, layer):
    step = pl.program_id(0)
    tm = TM_EXP

    def weight_copies(e, s):
        return (pltpu.make_async_copy(wg_hbm.at[layer, e], wg_buf.at[s], sems.at[s, 0]),
                pltpu.make_async_copy(wu_hbm.at[layer, e], wu_buf.at[s], sems.at[s, 1]),
                pltpu.make_async_copy(wd_hbm.at[layer, e], wd_buf.at[s], sems.at[s, 2]))

    def block(j):
        i = step * EXP_SUB + j
        rows = slice(j * tm, (j + 1) * tm)

        @pl.when(i < nused_ref[0])
        def _():
            s = slot_ref[i]

            if j == 0:
                @pl.when(i == 0)
                def _():
                    for c in weight_copies(blk_e_ref[0], 0):
                        c.start()

            @pl.when(first_ref[i] == 1)
            def _():
                for c in weight_copies(blk_e_ref[i], s):
                    c.wait()

                @pl.when(nxt_ref[i] >= 0)
                def _():
                    for c in weight_copies(nxt_ref[i], 1 - s):
                        c.start()

            lo, hi = _unpack_bf16_pairs(xs_ref[rows, :])
            xf = jnp.concatenate([lo, hi], axis=1)
            a = jnp.dot(xf, wg_buf[s], preferred_element_type=F32)
            b = jnp.dot(xf, wu_buf[s], preferred_element_type=F32)
            hm = _silu(a) * b
            y = jnp.dot(hm, wd_buf[s], preferred_element_type=F32)
            ys_ref[rows, :] = _pack_bf16_pairs(y)

        @pl.when(i >= nused_ref[0])
        def _():
            ys_ref[rows, :] = jnp.zeros((tm, HALF), U32)

    for j in range(EXP_SUB):
        block(j)


def _experts(xs, blk_e, nused, w_gate, w_up, w_down, layer):
    tm = TM_EXP
    p_rows = xs.shape[0] - MOE_EXPERTS * tm
    nblk = p_rows // tm
    pos = jnp.arange(nblk, dtype=I32)
    valid = pos < nused[0]
    prev_e = jnp.concatenate([jnp.full((1,), -1, I32), blk_e[:-1]])
    first = valid & (blk_e != prev_e)
    slot = (jnp.cumsum(first.astype(I32)) - 1) % 2
    first_pos = jnp.where(first, pos, nblk)
    next_first = jnp.concatenate([lax.cummin(first_pos, reverse=True)[1:], jnp.full((1,), nblk, I32)])
    nxt = jnp.where(next_first < nblk, blk_e[jnp.minimum(next_first, nblk - 1)], -1)

    def blk(i, be, nu, *_):
        return jnp.minimum(i, (nu[0] - 1) // EXP_SUB)

    grid_spec = pltpu.PrefetchScalarGridSpec(
        num_scalar_prefetch=5,
        grid=(nblk // EXP_SUB,),
        in_specs=[
            pl.BlockSpec((EXP_SUB * tm, HALF), lambda i, *sp: (blk(i, *sp), 0)),
            pl.BlockSpec(memory_space=pl.ANY),
            pl.BlockSpec(memory_space=pl.ANY),
            pl.BlockSpec(memory_space=pl.ANY),
        ],
        out_specs=pl.BlockSpec((EXP_SUB * tm, HALF), lambda i, *sp: (i, 0)),
        scratch_shapes=[
            pltpu.VMEM((2, D_MODEL, MOE_FF), F32),
            pltpu.VMEM((2, D_MODEL, MOE_FF), F32),
            pltpu.VMEM((2, MOE_FF, D_MODEL), F32),
            pltpu.SemaphoreType.DMA((2, 3)),
        ],
    )
    return pl.pallas_call(
        functools.partial(_expert_kernel, layer=layer),
        grid_spec=grid_spec,
        out_shape=jax.ShapeDtypeStruct((p_rows, HALF), U32),
        compiler_params=_cparams(("arbitrary",)),
        name="moe_experts",
    )(blk_e, nused, first.astype(I32), slot.astype(I32), nxt.astype(I32), xs, w_gate, w_up, w_down)


def _combine_kernel(x_ref, w1_ref, w2_ref, fg_ref, y1_ref, y2_ref, out_ref):
    lo1, hi1 = _unpack_bf16_pairs(y1_ref[...])
    lo2, hi2 = _unpack_bf16_pairs(y2_ref[...])
    w1 = w1_ref[...]
    w2 = w2_ref[...]
    x = x_ref[...]
    o_lo = x[:, :HALF] + w1 * lo1 + w2 * lo2
    o_hi = x[:, HALF:] + w1 * hi1 + w2 * hi2
    ms = (jnp.sum(o_lo * o_lo, axis=-1, keepdims=True)
          + jnp.sum(o_hi * o_hi, axis=-1, keepdims=True)) * (1.0 / D_MODEL)
    sc = lax.rsqrt(ms + NORM_EPS)
    o_lo = o_lo * sc * fg_ref[:, :HALF]
    o_hi = o_hi * sc * fg_ref[:, HALF:]
    out_ref[:, :HALF] = o_lo
    out_ref[:, HALF:] = o_hi


def _combine(yg, x, w1, w2, final_g):
    n = x.shape[0]
    td = T_COMB
    nb = n // td
    return pl.pallas_call(
        _combine_kernel,
        grid=(nb,),
        in_specs=[
            pl.BlockSpec((td, D_MODEL), lambda i: (i, 0)),
            pl.BlockSpec((td, 1), lambda i: (i, 0)),
            pl.BlockSpec((td, 1), lambda i: (i, 0)),
            pl.BlockSpec((1, D_MODEL), lambda i: (0, 0)),
            pl.BlockSpec((td, HALF), lambda i: (i, 0)),
            pl.BlockSpec((td, HALF), lambda i: (i + nb, 0)),
        ],
        out_specs=pl.BlockSpec((td, D_MODEL), lambda i: (i, 0)),
        out_shape=jax.ShapeDtypeStruct((n, D_MODEL), F32),
        compiler_params=_cparams(("arbitrary",)),
        name="moe_combine",
    )(x, w1, w2, final_g, yg, yg)


def _moe(x, g, w_rg, b_rg, w_re, b_re, w_gate, w_up, w_down, layer):
    n = x.shape[0]
    tm = TM_EXP
    p_rows = 2 * n + MOE_EXPERTS * tm
    nblk = p_rows // tm
    hp, ri, rw, cnt = _router(x, g, w_rg, b_rg, w_re, b_re)
    counts = cnt[:, 0].astype(I32)
    pcounts = (counts + tm - 1) // tm * tm
    pend = jnp.cumsum(pcounts)
    pstart = pend - pcounts
    eio = jnp.arange(MOE_EXPERTS, dtype=I32)[:, None]
    dest1 = jnp.sum(jnp.where(ri[0][None, :] == eio, pstart[:, None], 0), axis=0) + ri[2]
    dest2 = jnp.sum(jnp.where(ri[1][None, :] == eio, pstart[:, None], 0), axis=0) + ri[3]
    blk_start = jnp.arange(nblk, dtype=I32) * tm
    blk_e = jnp.minimum(jnp.sum((pend[None, :] <= blk_start[:, None]).astype(I32), axis=1), MOE_EXPERTS - 1)
    nused = jnp.maximum(pend[-1] // tm, 1).astype(I32).reshape(1)
    r = jnp.arange(tm, dtype=I32)[None, :]
    pad_slot = jnp.where(r < (pcounts - counts)[:, None], (pstart + counts)[:, None] + r, p_rows + eio * tm + r)
    kd = n // SC_WORKERS // SC_ROWS
    idx = jnp.concatenate([dest1.reshape(SC_WORKERS, kd, SC_ROWS), dest2.reshape(SC_WORKERS, kd, SC_ROWS),
                           pad_slot.reshape(SC_WORKERS, -1, SC_ROWS)], axis=1)
    zero_rows = jnp.zeros((SC_ROWS, HALF), U32)
    xs = _sc_dispatch(hp, idx, zero_rows, p_rows + MOE_EXPERTS * tm)
    ys = _experts(xs, blk_e, nused, w_gate, w_up, w_down, layer)
    gidx = jnp.concatenate([dest1, dest2]).reshape(SC_WORKERS, -1, SC_ROWS)
    yg = _sc_gather(ys, gidx)
    return yg, rw[0].reshape(n, 1), rw[1].reshape(n, 1)


def kernel(x, positions, norm_mix_g, norm_ffn_g, ret_w_in, ret_head_g, ret_w_out, conv_w_pw1, conv_b_pw1, conv_w_dw, conv_b_dw, conv_ln_g, conv_ln_b, conv_w_pw2, conv_b_pw2, moe_w_rg, moe_b_rg, moe_w_re, moe_b_re, moe_w_gate, moe_w_up, moe_w_down, final_norm_g):
    b, s, d = x.shape
    n = b * s
    xt = x.reshape(n, d)
    pos = positions.reshape(n, 1)
    fg = final_norm_g.reshape(1, d)

    q, k, v, gate = _ret_inproj(xt, pos, norm_mix_g[0].reshape(1, d), ret_w_in[0].astype(BF16))
    xt = _ret_core(q, k, v, gate, xt, ret_head_g[0].reshape(RET_V, 1), ret_w_out[0])
    yg, w1, w2 = _moe(xt, norm_ffn_g[0].reshape(1, d), moe_w_rg[0], moe_b_rg[0], moe_w_re[0], moe_b_re[0],
                      moe_w_gate, moe_w_up, moe_w_down, 0)

    xt, u = _conv_pw1(xt, w1, w2, yg, norm_mix_g[1].reshape(1, d), conv_w_pw1[0].astype(BF16),
                      conv_b_pw1[0].reshape(1, 2 * d))
    xt = _conv_core(u, xt, conv_w_dw[0], conv_b_dw[0].reshape(1, d), conv_ln_g[0].reshape(1, d),
                    conv_ln_b[0].reshape(1, d), conv_w_pw2[0].astype(BF16), conv_b_pw2[0].reshape(1, d))
    yg, w1, w2 = _moe(xt, norm_ffn_g[1].reshape(1, d), moe_w_rg[1], moe_b_rg[1], moe_w_re[1], moe_b_re[1],
                      moe_w_gate, moe_w_up, moe_w_down, 1)
    xt = _combine(yg, xt, w1, w2, fg)
    return xt.reshape(b, s, d)
```

```python
import functools
import math

import jax
import jax.numpy as jnp
import numpy as np
from jax import lax
from jax.experimental import pallas as pl
from jax.experimental.pallas import tpu as pltpu
from jax.experimental.pallas import tpu_sc as plsc

F32 = jnp.float32
BF16 = jnp.bfloat16
U32 = jnp.uint32
I32 = jnp.int32

D_MODEL = 1024
RET_HEADS = 4
RET_DK = 256
RET_DV = 512
RET_QK = RET_HEADS * RET_DK
RET_V = RET_HEADS * RET_DV
ROPE_BASE = 10000.0
CONV_WIDTH = 31
MOE_GROUPS = 4
MOE_EPG = 8
MOE_EXPERTS = MOE_GROUPS * MOE_EPG
MOE_FF = 512
NORM_EPS = 1e-6

TM_PROJ = 512
RET_C = 256
RET_STEP = 512
TM_CONV = 512
CONV_HALO = 32
CONV_ROWS = 128
T_ROUTE = 512
TM_EXP = 256
EXP_SUB = 4
T_COMB = 512
SC_CORES = 2
SC_SUBCORES = 16
SC_WORKERS = SC_CORES * SC_SUBCORES
SC_ROWS = 64
HALF = D_MODEL // 2

VMEM_LIMIT = 56 * 1024 * 1024


def _cparams(sem, flags=None):
    return pltpu.CompilerParams(dimension_semantics=sem, vmem_limit_bytes=VMEM_LIMIT, flags=flags)


def _rms(x, g):
    ms = jnp.mean(x * x, axis=-1, keepdims=True)
    return x * lax.rsqrt(ms + NORM_EPS) * g


def _silu(x):
    return x * (1.0 / (1.0 + jnp.exp(-x)))


def _pack_bf16_pairs(y):
    lo = pltpu.bitcast(y[:, :HALF].astype(BF16).astype(F32), U32)
    hi = pltpu.bitcast(y[:, HALF:].astype(BF16).astype(F32), U32)
    return (hi & jnp.uint32(0xFFFF0000)) | (lo >> 16)


def _route_weight_columns(rw):
    t = jnp.concatenate([rw] * 16, axis=0).T
    return t[:, 0:1], t[:, 1:2]


def _unpack_bf16_pairs(p):
    lo = pltpu.bitcast(p << 16, F32)
    hi = pltpu.bitcast(p & jnp.uint32(0xFFFF0000), F32)
    return lo, hi


def _ret_inproj_kernel(x_ref, pos_ref, g_ref, inv_ref, w_ref, q_ref, k_ref, v_ref, gate_ref):
    half = RET_DK // 2
    kscale = RET_DK ** -0.5
    h = _rms(x_ref[...], g_ref[...]).astype(BF16)
    ang = pos_ref[...].astype(F32) * inv_ref[...]
    cos = jnp.cos(ang)
    sin = jnp.sin(ang)

    def proj(c0, width):
        return jnp.dot(h, w_ref[:, c0:c0 + width].astype(BF16), preferred_element_type=F32)

    for hd in range(RET_HEADS):
        for base, out_ref, cs, sn in ((0, q_ref, cos, sin), (RET_QK, k_ref, cos * kscale, sin * kscale)):
            t = proj(base + hd * RET_DK, RET_DK)
            t1 = t[:, :half]
            t2 = t[:, half:]
            out_ref[:, hd * RET_DK:hd * RET_DK + half] = (t1 * cs - t2 * sn).astype(BF16)
            out_ref[:, hd * RET_DK + half:(hd + 1) * RET_DK] = (t1 * sn + t2 * cs).astype(BF16)
    for j in range(RET_V // 512):
        v_ref[:, j * 512:(j + 1) * 512] = proj(2 * RET_QK + j * 512, 512).astype(BF16)
        gate_ref[:, j * 512:(j + 1) * 512] = proj(2 * RET_QK + RET_V + j * 512, 512).astype(BF16)


def _ret_inproj(x, pos, g, w_in):
    n = x.shape[0]
    half = RET_DK // 2
    inv = (ROPE_BASE ** (-jnp.arange(half, dtype=F32) / half)).reshape(1, half)
    tm = TM_PROJ
    return pl.pallas_call(
        _ret_inproj_kernel,
        grid=(n // tm,),
        in_specs=[
            pl.BlockSpec((tm, D_MODEL), lambda i: (i, 0)),
            pl.BlockSpec((tm, 1), lambda i: (i, 0)),
            pl.BlockSpec((1, D_MODEL), lambda i: (0, 0)),
            pl.BlockSpec((1, half), lambda i: (0, 0)),
            pl.BlockSpec(w_in.shape, lambda i: (0, 0), pipeline_mode=pl.Buffered(1)),
        ],
        out_specs=[
            pl.BlockSpec((tm, RET_QK), lambda i: (i, 0)),
            pl.BlockSpec((tm, RET_QK), lambda i: (i, 0)),
            pl.BlockSpec((tm, RET_V), lambda i: (i, 0)),
            pl.BlockSpec((tm, RET_V), lambda i: (i, 0)),
        ],
        out_shape=[
            jax.ShapeDtypeStruct((n, RET_QK), BF16),
            jax.ShapeDtypeStruct((n, RET_QK), BF16),
            jax.ShapeDtypeStruct((n, RET_V), BF16),
            jax.ShapeDtypeStruct((n, RET_V), BF16),
        ],
        compiler_params=_cparams(("arbitrary",)),
        name="ret_inproj",
    )(x, pos, g, inv, w_in)


def _ret_core_kernel(cdec_ref, q_ref, k_ref, v_ref, gate_ref, x_ref, hg_ref, intra_ref, cross_ref,
                     kdec_ref, wo_ref, out_ref, state_ref, y_ref, wos_ref):
    @pl.when(pl.program_id(0) == 0)
    def _():
        state_ref[...] = jnp.zeros_like(state_ref)
        wos_ref[...] = (wo_ref[...] * hg_ref[...]).astype(BF16)

    for r0 in range(0, RET_STEP, RET_C):
        rs = slice(r0, r0 + RET_C)
        for hd in range(RET_HEADS):
            q = q_ref[rs, hd * RET_DK:(hd + 1) * RET_DK]
            k = k_ref[rs, hd * RET_DK:(hd + 1) * RET_DK]
            v = v_ref[rs, hd * RET_DV:(hd + 1) * RET_DV]
            state = state_ref[hd]
            scores = lax.dot_general(q, k, (((1,), (1,)), ((), ())), preferred_element_type=F32)
            scores = (scores * intra_ref[hd]).astype(BF16)
            o = jnp.dot(scores, v, preferred_element_type=F32)
            cross = cross_ref[hd]
            o_cross = jnp.dot(q, state.astype(BF16), preferred_element_type=F32)
            o = o + o_cross * jnp.concatenate([cross] * (RET_DV // 128), axis=1)
            kdec = kdec_ref[hd]
            kd = (k.astype(F32) * jnp.concatenate([kdec] * (RET_DK // 128), axis=1)).astype(BF16)
            upd = lax.dot_general(kd, v, (((0,), (0,)), ((), ())), preferred_element_type=F32)
            state_ref[hd] = state * cdec_ref[hd] + upd
            ms = jnp.mean(o * o, axis=-1, keepdims=True)
            on = o * lax.rsqrt(ms + NORM_EPS)
            gt = gate_ref[rs, hd * RET_DV:(hd + 1) * RET_DV].astype(F32)
            y_ref[rs, hd * RET_DV:(hd + 1) * RET_DV] = (_silu(gt) * on).astype(BF16)
        out_ref[rs, :] = x_ref[rs, :] + jnp.dot(y_ref[rs, :], wos_ref[...], preferred_element_type=F32)


def _ret_core(q, k, v, gate, x, head_g_col, w_out):
    n = x.shape[0]
    c = RET_C
    log_gamma = jnp.log1p(-(2.0 ** (-5.0 - jnp.arange(RET_HEADS, dtype=F32))))
    idx = jnp.arange(c, dtype=F32)
    diff = idx[:, None] - idx[None, :]
    intra = jnp.where(diff >= 0, jnp.exp(log_gamma[:, None, None] * jnp.maximum(diff, 0.0)), 0.0)
    cross = jnp.broadcast_to(jnp.exp(log_gamma[:, None] * (idx + 1.0))[:, :, None], (RET_HEADS, c, 128))
    kdec = jnp.broadcast_to(jnp.exp(log_gamma[:, None] * (c - 1.0 - idx))[:, :, None], (RET_HEADS, c, 128))
    cdec = jnp.exp(log_gamma * c)
    return pl.pallas_call(
        _ret_core_kernel,
        grid=(n // RET_STEP,),
        in_specs=[
            pl.BlockSpec(memory_space=pltpu.SMEM),
            pl.BlockSpec((RET_STEP, RET_QK), lambda i: (i, 0)),
            pl.BlockSpec((RET_STEP, RET_QK), lambda i: (i, 0)),
            pl.BlockSpec((RET_STEP, RET_V), lambda i: (i, 0)),
            pl.BlockSpec((RET_STEP, RET_V), lambda i: (i, 0)),
            pl.BlockSpec((RET_STEP, D_MODEL), lambda i: (i, 0)),
            pl.BlockSpec((RET_V, 1), lambda i: (0, 0)),
            pl.BlockSpec((RET_HEADS, c, c), lambda i: (0, 0, 0)),
            pl.BlockSpec((RET_HEADS, c, 128), lambda i: (0, 0, 0)),
            pl.BlockSpec((RET_HEADS, c, 128), lambda i: (0, 0, 0)),
            pl.BlockSpec((RET_V, D_MODEL), lambda i: (0, 0)),
        ],
        out_specs=pl.BlockSpec((RET_STEP, D_MODEL), lambda i: (i, 0)),
        out_shape=jax.ShapeDtypeStruct((n, D_MODEL), F32),
        scratch_shapes=[
            pltpu.VMEM((RET_HEADS, RET_DK, RET_DV), F32),
            pltpu.VMEM((RET_STEP, RET_V), BF16),
            pltpu.VMEM((RET_V, D_MODEL), BF16),
        ],
        compiler_params=_cparams(("arbitrary",)),
        name="ret_core",
    )(cdec, q, k, v, gate, x, head_g_col, intra, cross, kdec, w_out)


def _conv_pw1_kernel(x_ref, rw_ref, y1_ref, y2_ref, g_ref, w_ref, b_ref, xo_ref, u_ref):
    lo1, hi1 = _unpack_bf16_pairs(y1_ref[...])
    lo2, hi2 = _unpack_bf16_pairs(y2_ref[...])
    w1, w2 = _route_weight_columns(rw_ref[...])
    x = jnp.concatenate([x_ref[:, :HALF] + w1 * lo1 + w2 * lo2, x_ref[:, HALF:] + w1 * hi1 + w2 * hi2], axis=1)
    xo_ref[...] = x
    h = _rms(x, g_ref[...]).astype(BF16)
    for j in range(D_MODEL // 512):
        a = jnp.dot(h, w_ref[:, j * 512:(j + 1) * 512].astype(BF16), preferred_element_type=F32)
        a = a + b_ref[:, j * 512:(j + 1) * 512]
        gt = jnp.dot(h, w_ref[:, D_MODEL + j * 512:D_MODEL + (j + 1) * 512].astype(BF16),
                     preferred_element_type=F32)
        gt = gt + b_ref[:, D_MODEL + j * 512:D_MODEL + (j + 1) * 512]
        u_ref[:, j * 512:(j + 1) * 512] = a * (1.0 / (1.0 + jnp.exp(-gt)))


def _conv_pw1(x, rw, yg, g, w, b):
    n = x.shape[0]
    tm = TM_CONV
    nb = n // tm
    return pl.pallas_call(
        _conv_pw1_kernel,
        grid=(nb,),
        in_specs=[
            pl.BlockSpec((tm, D_MODEL), lambda i: (i, 0)),
            pl.BlockSpec((8, tm), lambda i: (0, i)),
            pl.BlockSpec((tm, HALF), lambda i: (i, 0)),
            pl.BlockSpec((tm, HALF), lambda i: (i + nb, 0)),
            pl.BlockSpec((1, D_MODEL), lambda i: (0, 0)),
            pl.BlockSpec((D_MODEL, 2 * D_MODEL), lambda i: (0, 0)),
            pl.BlockSpec((1, 2 * D_MODEL), lambda i: (0, 0)),
        ],
        out_specs=[pl.BlockSpec((tm, D_MODEL), lambda i: (i, 0)), pl.BlockSpec((tm, D_MODEL), lambda i: (i, 0))],
        out_shape=[jax.ShapeDtypeStruct((n, D_MODEL), F32), jax.ShapeDtypeStruct((n, D_MODEL), F32)],
        compiler_params=_cparams(("arbitrary",)),
        name="conv_pw1",
    )(x, rw, yg, yg, g, w, b)


def _conv_core_kernel(u_ref, halo_ref, x_ref, wdw_ref, bdw_ref, lng_ref, lnb_ref, w2_ref, b2_ref,
                      out_ref, win_ref, z_ref):
    tm = TM_CONV
    first = pl.program_id(0) == 0
    halo = halo_ref[...]
    win_ref[0:CONV_HALO, :] = jnp.where(first, jnp.zeros_like(halo), halo)
    win_ref[CONV_HALO:CONV_HALO + tm, :] = u_ref[...]
    off = CONV_HALO - (CONV_WIDTH - 1)
    rb = CONV_ROWS
    for cc in range(D_MODEL // 128):
        cs = slice(cc * 128, (cc + 1) * 128)
        for r0 in range(0, tm, rb):
            z = bdw_ref[:, cs]
            for b in range(8):
                rows = rb if b == 0 else rb + 8
                q = None
                for o in range(off, off + CONV_WIDTH):
                    if o % 8 != b:
                        continue
                    term = win_ref[r0 + o - b:r0 + o - b + rows, cs] * wdw_ref[o - off:o - off + 1, cs]
                    q = term if q is None else q + term
                z = z + (q if b == 0 else q[b:b + rb, :])
            z_ref[r0:r0 + rb, cs] = z
    z = z_ref[...]
    mu = jnp.mean(z, axis=-1, keepdims=True)
    zc = z - mu
    var = jnp.mean(zc * zc, axis=-1, keepdims=True)
    zn = zc * lax.rsqrt(var + NORM_EPS) * lng_ref[...] + lnb_ref[...]
    y = _silu(zn).astype(BF16)
    out_ref[...] = x_ref[...] + jnp.dot(y, w2_ref[...].astype(BF16), preferred_element_type=F32) + b2_ref[...]


def _conv_core(u, x, w_dw, b_dw, ln_g, ln_b, w2, b2):
    n = x.shape[0]
    tm = TM_CONV
    r = tm // CONV_HALO
    wdw_pad = jnp.zeros((32, D_MODEL), F32).at[:CONV_WIDTH].set(w_dw)
    return pl.pallas_call(
        _conv_core_kernel,
        grid=(n // tm,),
        in_specs=[
            pl.BlockSpec((tm, D_MODEL), lambda i: (i, 0)),
            pl.BlockSpec((CONV_HALO, D_MODEL), lambda i: (jnp.maximum(i * r - 1, 0), 0)),
            pl.BlockSpec((tm, D_MODEL), lambda i: (i, 0)),
            pl.BlockSpec((32, D_MODEL), lambda i: (0, 0)),
            pl.BlockSpec((1, D_MODEL), lambda i: (0, 0)),
            pl.BlockSpec((1, D_MODEL), lambda i: (0, 0)),
            pl.BlockSpec((1, D_MODEL), lambda i: (0, 0)),
            pl.BlockSpec((D_MODEL, D_MODEL), lambda i: (0, 0)),
            pl.BlockSpec((1, D_MODEL), lambda i: (0, 0)),
        ],
        out_specs=pl.BlockSpec((tm, D_MODEL), lambda i: (i, 0)),
        out_shape=jax.ShapeDtypeStruct((n, D_MODEL), F32),
        scratch_shapes=[pltpu.VMEM((CONV_HALO + tm, D_MODEL), F32), pltpu.VMEM((tm, D_MODEL), F32)],
        compiler_params=_cparams(("arbitrary",)),
        name="conv_core",
    )(u, u, x, wdw_pad, b_dw, ln_g, ln_b, w2, b2)


def _router_kernel(x_ref, g_ref, wr_ref, br_ref, hp_ref, ri_ref, rw_ref, cnt_ref, carry_ref):
    t = T_ROUTE

    @pl.when(pl.program_id(0) == 0)
    def _():
        carry_ref[...] = jnp.zeros_like(carry_ref)

    h = _rms(x_ref[...], g_ref[...])
    hp_ref[...] = _pack_bf16_pairs(h)
    h_hi = h.astype(BF16)
    h_lo = (h - h_hi.astype(F32)).astype(BF16)
    w = wr_ref[...]
    w_hi = w.astype(BF16)
    w_lo = (w - w_hi.astype(F32)).astype(BF16)
    dn = (((1,), (1,)), ((), ()))
    p = lax.dot_general(jnp.concatenate([w_hi, w_lo], axis=0), h_hi, dn, preferred_element_type=F32)
    nr = wr_ref.shape[0]
    logits = p[0:nr] + p[nr:2 * nr] + lax.dot_general(w_hi, h_lo, dn, preferred_element_type=F32)
    logits = logits + br_ref[:, 0:1]

    best = logits[0:1]
    gi = jnp.zeros((1, t), I32)
    for j in range(1, MOE_GROUPS):
        r = logits[j:j + 1]
        up = r > best
        gi = jnp.where(up, j, gi)
        best = jnp.where(up, r, best)
    den = jnp.zeros((1, t), F32)
    for j in range(MOE_GROUPS):
        den = den + jnp.exp(logits[j:j + 1] - best)
    gate_g = 1.0 / den

    sel = logits[8:8 + MOE_EPG]
    for j in range(1, MOE_GROUPS):
        sel = jnp.where(gi == j, logits[8 + j * MOE_EPG:8 + (j + 1) * MOE_EPG], sel)

    m1 = sel[0:1]
    i1 = jnp.zeros((1, t), I32)
    for j in range(1, MOE_EPG):
        r = sel[j:j + 1]
        up = r > m1
        i1 = jnp.where(up, j, i1)
        m1 = jnp.where(up, r, m1)
    m2 = jnp.full((1, t), -jnp.inf, F32)
    i2 = jnp.zeros((1, t), I32)
    started = jnp.zeros((1, t), jnp.bool_)
    for j in range(MOE_EPG):
        r = sel[j:j + 1]
        ok = i1 != j
        up = ok & ((r > m2) | jnp.logical_not(started))
        i2 = jnp.where(up, j, i2)
        m2 = jnp.where(up, r, m2)
        started = started | ok
    e21 = jnp.exp(m2 - m1)
    p1 = 1.0 / (1.0 + e21)
    w1 = gate_g * p1
    w2 = gate_g * (e21 * p1)
    eid1 = gi * MOE_EPG + i1
    eid2 = gi * MOE_EPG + i2

    eio = lax.broadcasted_iota(I32, (MOE_EXPERTS, t), 0)
    oh1 = eio == eid1
    oh2 = eio == eid2
    oh = (oh1 | oh2).astype(F32)
    rio = lax.broadcasted_iota(I32, (t, t), 0)
    cio = lax.broadcasted_iota(I32, (t, t), 1)
    upper = (rio < cio).astype(BF16)
    cum = jnp.dot(oh.astype(BF16), upper, preferred_element_type=F32) + carry_ref[:, 0:1]
    rank1 = jnp.sum(jnp.where(oh1, cum, 0.0), axis=0, keepdims=True)
    rank2 = jnp.sum(jnp.where(oh2, cum, 0.0), axis=0, keepdims=True)
    carry_ref[...] = carry_ref[...] + jnp.sum(oh, axis=1, keepdims=True)
    cnt_ref[...] = carry_ref[...]

    zi = jnp.zeros((4, t), I32)
    ri_ref[...] = jnp.concatenate([eid1, eid2, rank1.astype(I32), rank2.astype(I32), zi], axis=0)
    zf = jnp.zeros((6, t), F32)
    rw_ref[...] = jnp.concatenate([w1, w2, zf], axis=0)


def _router(x, g, w_rg, b_rg, w_re, b_re):
    n = x.shape[0]
    t = T_ROUTE
    wr = jnp.zeros((40, D_MODEL), F32).at[0:MOE_GROUPS].set(w_rg.T).at[8:40].set(w_re.T)
    br = jnp.zeros((40,), F32).at[0:MOE_GROUPS].set(b_rg).at[8:40].set(b_re)
    br = jnp.broadcast_to(br[:, None], (40, 128))
    return pl.pallas_call(
        _router_kernel,
        grid=(n // t,),
        in_specs=[
            pl.BlockSpec((t, D_MODEL), lambda i: (i, 0)),
            pl.BlockSpec((1, D_MODEL), lambda i: (0, 0)),
            pl.BlockSpec((40, D_MODEL), lambda i: (0, 0)),
            pl.BlockSpec((40, 128), lambda i: (0, 0)),
        ],
        out_specs=[
            pl.BlockSpec((t, HALF), lambda i: (i, 0)),
            pl.BlockSpec((8, t), lambda i: (0, i)),
            pl.BlockSpec((8, t), lambda i: (0, i)),
            pl.BlockSpec((MOE_EXPERTS, 128), lambda i: (0, 0)),
        ],
        out_shape=[
            jax.ShapeDtypeStruct((n, HALF), U32),
            jax.ShapeDtypeStruct((8, n), I32),
            jax.ShapeDtypeStruct((8, n), F32),
            jax.ShapeDtypeStruct((MOE_EXPERTS, 128), F32),
        ],
        scratch_shapes=[pltpu.VMEM((MOE_EXPERTS, 128), F32)],
        compiler_params=_cparams(("arbitrary",)),
        name="moe_router",
    )(x, g, wr, br)


def _sc_mesh():
    return plsc.VectorSubcoreMesh(core_axis_name="c", subcore_axis_name="s",
                                  num_cores=SC_CORES, num_subcores=SC_SUBCORES)


def _sc_worker_id():
    return lax.axis_index("s") * SC_CORES + lax.axis_index("c")


def _sc_dispatch(hp, idx, zero_rows, total_rows):
    n = hp.shape[0]
    tpw = n // SC_WORKERS
    kd = tpw // SC_ROWS
    kp = idx.shape[1] - 2 * kd

    @functools.partial(
        pl.kernel, mesh=_sc_mesh(),
        out_type=jax.ShapeDtypeStruct((total_rows, HALF), U32),
        scratch_types=[
            pltpu.VMEM((2 * kd + kp, SC_ROWS), I32),
            pltpu.VMEM((SC_ROWS, HALF), U32), pltpu.VMEM((SC_ROWS, HALF), U32), pltpu.VMEM((SC_ROWS, HALF), U32),
            pltpu.SemaphoreType.DMA((2,)), pltpu.SemaphoreType.DMA((2,)), pltpu.SemaphoreType.DMA,
        ],
        name="moe_dispatch_sc",
    )
    def k(hp_hbm, idx_hbm, zero_hbm, xs_hbm, idx_v, buf0, buf1, zbuf, load_sem, scat_sem, pad_sem):
        wid = _sc_worker_id()
        bufs = (buf0, buf1)
        pltpu.sync_copy(idx_hbm.at[wid], idx_v)

        def load(c):
            return pltpu.make_async_copy(hp_hbm.at[pl.ds(wid * tpw + c * SC_ROWS, SC_ROWS)], bufs[c % 2],
                                         load_sem.at[c % 2])

        def scatters(c):
            return (pltpu.make_async_copy(bufs[c % 2], xs_hbm.at[idx_v.at[c]], scat_sem.at[c % 2]),
                    pltpu.make_async_copy(bufs[c % 2], xs_hbm.at[idx_v.at[kd + c]], scat_sem.at[c % 2]))

        load(0).start()
        pltpu.sync_copy(zero_hbm, zbuf)
        pads = [pltpu.make_async_copy(zbuf, xs_hbm.at[idx_v.at[2 * kd + j]], pad_sem) for j in range(kp)]
        for p in pads:
            p.start()
        for c in range(kd):
            load(c).wait()
            for d in scatters(c):
                d.start()
            if c + 1 < kd:
                if c >= 1:
                    for d in scatters(c - 1):
                        d.wait()
                load(c + 1).start()
        for c in range(max(kd - 2, 0), kd):
            for d in scatters(c):
                d.wait()
        for p in pads:
            p.wait()

    return k(hp, idx, zero_rows)


def _sc_gather(ys, idx):
    kg = idx.shape[1]
    rows_per_worker = kg * SC_ROWS

    @functools.partial(
        pl.kernel, mesh=_sc_mesh(),
        out_type=jax.ShapeDtypeStruct((SC_WORKERS * rows_per_worker, HALF), U32),
        scratch_types=[
            pltpu.VMEM((kg, SC_ROWS), I32),
            pltpu.VMEM((SC_ROWS, HALF), U32), pltpu.VMEM((SC_ROWS, HALF), U32),
            pltpu.SemaphoreType.DMA((2,)), pltpu.SemaphoreType.DMA((2,)),
        ],
        name="moe_gather_sc",
    )
    def k(ys_hbm, idx_hbm, yg_hbm, idx_v, buf0, buf1, gat_sem, out_sem):
        wid = _sc_worker_id()
        bufs = (buf0, buf1)
        pltpu.sync_copy(idx_hbm.at[wid], idx_v)

        def gather(c):
            return pltpu.make_async_copy(ys_hbm.at[idx_v.at[c]], bufs[c % 2], gat_sem.at[c % 2])

        def store(c):
            return pltpu.make_async_copy(bufs[c % 2],
                                         yg_hbm.at[pl.ds(wid * rows_per_worker + c * SC_ROWS, SC_ROWS)],
                                         out_sem.at[c % 2])

        gather(0).start()
        for c in range(kg):
            if c + 1 < kg:
                if c >= 1:
                    store(c - 1).wait()
                gather(c + 1).start()
            gather(c).wait()
            store(c).start()
        for c in range(max(kg - 2, 0), kg):
            store(c).wait()

    return k(ys, idx)


def _expert_kernel(blk_e_ref, nused_ref, first_ref, slot_ref, nxt_ref, xs_ref, wg_hbm, wu_hbm, wd_hbm,
                   ys_ref, wg_buf, wu_buf, wd_buf, sems, *, layer):
    step = pl.program_id(0)
    tm = TM_EXP

    def weight_copies(e, s):
        return (pltpu.make_async_copy(wg_hbm.at[layer, e], wg_buf.at[s], sems.at[s, 0]),
                pltpu.make_async_copy(wu_hbm.at[layer, e], wu_buf.at[s], sems.at[s, 1]),
                pltpu.make_async_copy(wd_hbm.at[layer, e], wd_buf.at[s], sems.at[s, 2]))

    def block(j):
        i = step * EXP_SUB + j
        rows = slice(j * tm, (j + 1) * tm)

        @pl.when(i < nused_ref[0])
        def _():
            s = slot_ref[i]

            if j == 0:
                @pl.when(i == 0)
                def _():
                    for c in weight_copies(blk_e_ref[0], 0):
                        c.start()

            @pl.when(first_ref[i] == 1)
            def _():
                for c in weight_copies(blk_e_ref[i], s):
                    c.wait()

                @pl.when(nxt_ref[i] >= 0)
                def _():
                    for c in weight_copies(nxt_ref[i], 1 - s):
                        c.start()

            lo, hi = _unpack_bf16_pairs(xs_ref[rows, :])
            xf = jnp.concatenate([lo, hi], axis=1)
            a = jnp.dot(xf, wg_buf[s], preferred_element_type=F32)
            b = jnp.dot(xf, wu_buf[s], preferred_element_type=F32)
            hm = _silu(a) * b
            y = jnp.dot(hm, wd_buf[s], preferred_element_type=F32)
            ys_ref[rows, :] = _pack_bf16_pairs(y)

        @pl.when(i >= nused_ref[0])
        def _():
            ys_ref[rows, :] = jnp.zeros((tm, HALF), U32)

    for j in range(EXP_SUB):
        block(j)


def _experts(xs, blk_e, nused, w_gate, w_up, w_down, layer):
    tm = TM_EXP
    p_rows = xs.shape[0] - MOE_EXPERTS * tm
    nblk = p_rows // tm
    pos = jnp.arange(nblk, dtype=I32)
    valid = pos < nused[0]
    prev_e = jnp.concatenate([jnp.full((1,), -1, I32), blk_e[:-1]])
    first = valid & (blk_e != prev_e)
    slot = (jnp.cumsum(first.astype(I32)) - 1) % 2
    first_pos = jnp.where(first, pos, nblk)
    next_first = jnp.concatenate([lax.cummin(first_pos, reverse=True)[1:], jnp.full((1,), nblk, I32)])
    nxt = jnp.where(next_first < nblk, blk_e[jnp.minimum(next_first, nblk - 1)], -1)

    def blk(i, be, nu, *_):
        return jnp.minimum(i, (nu[0] - 1) // EXP_SUB)

    grid_spec = pltpu.PrefetchScalarGridSpec(
        num_scalar_prefetch=5,
        grid=(nblk // EXP_SUB,),
        in_specs=[
            pl.BlockSpec((EXP_SUB * tm, HALF), lambda i, *sp: (blk(i, *sp), 0)),
            pl.BlockSpec(memory_space=pl.ANY),
            pl.BlockSpec(memory_space=pl.ANY),
            pl.BlockSpec(memory_space=pl.ANY),
        ],
        out_specs=pl.BlockSpec((EXP_SUB * tm, HALF), lambda i, *sp: (i, 0)),
        scratch_shapes=[
            pltpu.VMEM((2, D_MODEL, MOE_FF), F32),
            pltpu.VMEM((2, D_MODEL, MOE_FF), F32),
            pltpu.VMEM((2, MOE_FF, D_MODEL), F32),
            pltpu.SemaphoreType.DMA((2, 3)),
        ],
    )
    return pl.pallas_call(
        functools.partial(_expert_kernel, layer=layer),
        grid_spec=grid_spec,
        out_shape=jax.ShapeDtypeStruct((p_rows, HALF), U32),
        compiler_params=_cparams(("arbitrary",)),
        name="moe_experts",
    )(blk_e, nused, first.astype(I32), slot.astype(I32), nxt.astype(I32), xs, w_gate, w_up, w_down)


def _combine_kernel(x_ref, rw_ref, fg_ref, y1_ref, y2_ref, out_ref):
    lo1, hi1 = _unpack_bf16_pairs(y1_ref[...])
    lo2, hi2 = _unpack_bf16_pairs(y2_ref[...])
    w1, w2 = _route_weight_columns(rw_ref[...])
    x = x_ref[...]
    o_lo = x[:, :HALF] + w1 * lo1 + w2 * lo2
    o_hi = x[:, HALF:] + w1 * hi1 + w2 * hi2
    ms = (jnp.sum(o_lo * o_lo, axis=-1, keepdims=True)
          + jnp.sum(o_hi * o_hi, axis=-1, keepdims=True)) * (1.0 / D_MODEL)
    sc = lax.rsqrt(ms + NORM_EPS)
    o_lo = o_lo * sc * fg_ref[:, :HALF]
    o_hi = o_hi * sc * fg_ref[:, HALF:]
    out_ref[:, :HALF] = o_lo
    out_ref[:, HALF:] = o_hi


def _combine(yg, x, rw, final_g):
    n = x.shape[0]
    td = T_COMB
    nb = n // td
    return pl.pallas_call(
        _combine_kernel,
        grid=(nb,),
        in_specs=[
            pl.BlockSpec((td, D_MODEL), lambda i: (i, 0)),
            pl.BlockSpec((8, td), lambda i: (0, i)),
            pl.BlockSpec((1, D_MODEL), lambda i: (0, 0)),
            pl.BlockSpec((td, HALF), lambda i: (i, 0)),
            pl.BlockSpec((td, HALF), lambda i: (i + nb, 0)),
        ],
        out_specs=pl.BlockSpec((td, D_MODEL), lambda i: (i, 0)),
        out_shape=jax.ShapeDtypeStruct((n, D_MODEL), F32),
        compiler_params=_cparams(("arbitrary",)),
        name="moe_combine",
    )(x, rw, final_g, yg, yg)


def _moe(x, g, w_rg, b_rg, w_re, b_re, w_gate, w_up, w_down, layer):
    n = x.shape[0]
    tm = TM_EXP
    p_rows = 2 * n + MOE_EXPERTS * tm
    nblk = p_rows // tm
    hp, ri, rw, cnt = _router(x, g, w_rg, b_rg, w_re, b_re)
    counts = cnt[:, 0].astype(I32)
    pcounts = (counts + tm - 1) // tm * tm
    pend = jnp.cumsum(pcounts)
    pstart = pend - pcounts
    eio = jnp.arange(MOE_EXPERTS, dtype=I32)[:, None]
    dest1 = jnp.sum(jnp.where(ri[0][None, :] == eio, pstart[:, None], 0), axis=0) + ri[2]
    dest2 = jnp.sum(jnp.where(ri[1][None, :] == eio, pstart[:, None], 0), axis=0) + ri[3]
    blk_start = jnp.arange(nblk, dtype=I32) * tm
    blk_e = jnp.minimum(jnp.sum((pend[None, :] <= blk_start[:, None]).astype(I32), axis=1), MOE_EXPERTS - 1)
    nused = jnp.maximum(pend[-1] // tm, 1).astype(I32).reshape(1)
    r = jnp.arange(tm, dtype=I32)[None, :]
    pad_slot = jnp.where(r < (pcounts - counts)[:, None], (pstart + counts)[:, None] + r, p_rows + eio * tm + r)
    kd = n // SC_WORKERS // SC_ROWS
    idx = jnp.concatenate([dest1.reshape(SC_WORKERS, kd, SC_ROWS), dest2.reshape(SC_WORKERS, kd, SC_ROWS),
                           pad_slot.reshape(SC_WORKERS, -1, SC_ROWS)], axis=1)
    zero_rows = jnp.zeros((SC_ROWS, HALF), U32)
    xs = _sc_dispatch(hp, idx, zero_rows, p_rows + MOE_EXPERTS * tm)
    ys = _experts(xs, blk_e, nused, w_gate, w_up, w_down, layer)
    gidx = jnp.concatenate([dest1, dest2]).reshape(SC_WORKERS, -1, SC_ROWS)
    yg = _sc_gather(ys, gidx)
    return yg, rw


def kernel(x, positions, norm_mix_g, norm_ffn_g, ret_w_in, ret_head_g, ret_w_out, conv_w_pw1, conv_b_pw1, conv_w_dw, conv_b_dw, conv_ln_g, conv_ln_b, conv_w_pw2, conv_b_pw2, moe_w_rg, moe_b_rg, moe_w_re, moe_b_re, moe_w_gate, moe_w_up, moe_w_down, final_norm_g):
    b, s, d = x.shape
    n = b * s
    xt = x.reshape(n, d)
    pos = positions.reshape(n, 1)
    fg = final_norm_g.reshape(1, d)

    q, k, v, gate = _ret_inproj(xt, pos, norm_mix_g[0].reshape(1, d), ret_w_in[0])
    xt = _ret_core(q, k, v, gate, xt, ret_head_g[0].reshape(RET_V, 1), ret_w_out[0])
    yg, rw = _moe(xt, norm_ffn_g[0].reshape(1, d), moe_w_rg[0], moe_b_rg[0], moe_w_re[0], moe_b_re[0],
                  moe_w_gate, moe_w_up, moe_w_down, 0)

    xt, u = _conv_pw1(xt, rw, yg, norm_mix_g[1].reshape(1, d), conv_w_pw1[0],
                      conv_b_pw1[0].reshape(1, 2 * d))
    xt = _conv_core(u, xt, conv_w_dw[0], conv_b_dw[0].reshape(1, d), conv_ln_g[0].reshape(1, d),
                    conv_ln_b[0].reshape(1, d), conv_w_pw2[0], conv_b_pw2[0].reshape(1, d))
    yg, rw = _moe(xt, norm_ffn_g[1].reshape(1, d), moe_w_rg[1], moe_b_rg[1], moe_w_re[1], moe_b_re[1],
                  moe_w_gate, moe_w_up, moe_w_down, 1)
    xt = _combine(yg, xt, rw, fg)
    return xt.reshape(b, s, d)
```

```python
import functools
import math

import jax
import jax.numpy as jnp
import numpy as np
from jax import lax
from jax.experimental import pallas as pl
from jax.experimental.pallas import tpu as pltpu
from jax.experimental.pallas import tpu_sc as plsc

F32 = jnp.float32
BF16 = jnp.bfloat16
U32 = jnp.uint32
I32 = jnp.int32

D_MODEL = 1024
RET_HEADS = 4
RET_DK = 256
RET_DV = 512
RET_QK = RET_HEADS * RET_DK
RET_V = RET_HEADS * RET_DV
ROPE_BASE = 10000.0
CONV_WIDTH = 31
MOE_GROUPS = 4
MOE_EPG = 8
MOE_EXPERTS = MOE_GROUPS * MOE_EPG
MOE_FF = 512
NORM_EPS = 1e-6

TM_PROJ = 512
RET_C = 256
RET_STEP = 512
TM_CONV = 512
CONV_HALO = 32
CONV_ROWS = 128
T_ROUTE = 512
TM_EXP = 256
EXP_SUB = 4
T_COMB = 512
SC_CORES = 2
SC_SUBCORES = 16
SC_WORKERS = SC_CORES * SC_SUBCORES
SC_ROWS = 64
HALF = D_MODEL // 2

VMEM_LIMIT = 56 * 1024 * 1024


def _cparams(sem, flags=None):
    return pltpu.CompilerParams(dimension_semantics=sem, vmem_limit_bytes=VMEM_LIMIT, flags=flags)


def _rms(x, g):
    ms = jnp.mean(x * x, axis=-1, keepdims=True)
    return x * lax.rsqrt(ms + NORM_EPS) * g


def _silu(x):
    return x * (1.0 / (1.0 + jnp.exp(-x)))


def _pack_bf16_pairs(y):
    lo = pltpu.bitcast(y[:, :HALF].astype(BF16).astype(F32), U32)
    hi = pltpu.bitcast(y[:, HALF:].astype(BF16).astype(F32), U32)
    return (hi & jnp.uint32(0xFFFF0000)) | (lo >> 16)


def _route_weight_columns(rw):
    t = jnp.concatenate([rw] * 16, axis=0).T
    return t[:, 0:1], t[:, 1:2]


def _unpack_bf16_pairs(p):
    lo = pltpu.bitcast(p << 16, F32)
    hi = pltpu.bitcast(p & jnp.uint32(0xFFFF0000), F32)
    return lo, hi


def _ret_inproj_kernel(x_ref, pos_ref, g_ref, inv_ref, w_ref, q_ref, k_ref, v_ref, gate_ref):
    half = RET_DK // 2
    kscale = RET_DK ** -0.5
    h = _rms(x_ref[...], g_ref[...]).astype(BF16)

    def proj(c0, width):
        return jnp.dot(h, w_ref[:, c0:c0 + width].astype(BF16), preferred_element_type=F32)

    v0 = proj(2 * RET_QK, 512)
    v_ref[:, 0:512] = v0.astype(BF16)
    zero = ((pltpu.bitcast(v0[:, 0:half], U32) >> 16) >> 16).astype(F32)
    ang = pos_ref[...].astype(F32) * inv_ref[...] + zero
    cos = jnp.cos(ang)
    sin = jnp.sin(ang)
    for j in range(RET_V // 512):
        if j > 0:
            v_ref[:, j * 512:(j + 1) * 512] = proj(2 * RET_QK + j * 512, 512).astype(BF16)
        gate_ref[:, j * 512:(j + 1) * 512] = proj(2 * RET_QK + RET_V + j * 512, 512).astype(BF16)

    for hd in range(RET_HEADS):
        for base, out_ref, cs, sn in ((0, q_ref, cos, sin), (RET_QK, k_ref, cos * kscale, sin * kscale)):
            t = proj(base + hd * RET_DK, RET_DK)
            t1 = t[:, :half]
            t2 = t[:, half:]
            out_ref[:, hd * RET_DK:hd * RET_DK + half] = (t1 * cs - t2 * sn).astype(BF16)
            out_ref[:, hd * RET_DK + half:(hd + 1) * RET_DK] = (t1 * sn + t2 * cs).astype(BF16)


def _ret_inproj(x, pos, g, w_in):
    n = x.shape[0]
    half = RET_DK // 2
    inv = (ROPE_BASE ** (-jnp.arange(half, dtype=F32) / half)).reshape(1, half)
    tm = TM_PROJ
    return pl.pallas_call(
        _ret_inproj_kernel,
        grid=(n // tm,),
        in_specs=[
            pl.BlockSpec((tm, D_MODEL), lambda i: (i, 0)),
            pl.BlockSpec((tm, 1), lambda i: (i, 0)),
            pl.BlockSpec((1, D_MODEL), lambda i: (0, 0)),
            pl.BlockSpec((1, half), lambda i: (0, 0)),
            pl.BlockSpec(w_in.shape, lambda i: (0, 0), pipeline_mode=pl.Buffered(1)),
        ],
        out_specs=[
            pl.BlockSpec((tm, RET_QK), lambda i: (i, 0)),
            pl.BlockSpec((tm, RET_QK), lambda i: (i, 0)),
            pl.BlockSpec((tm, RET_V), lambda i: (i, 0)),
            pl.BlockSpec((tm, RET_V), lambda i: (i, 0)),
        ],
        out_shape=[
            jax.ShapeDtypeStruct((n, RET_QK), BF16),
            jax.ShapeDtypeStruct((n, RET_QK), BF16),
            jax.ShapeDtypeStruct((n, RET_V), BF16),
            jax.ShapeDtypeStruct((n, RET_V), BF16),
        ],
        compiler_params=_cparams(("arbitrary",)),
        name="ret_inproj",
    )(x, pos, g, inv, w_in)


def _ret_core_kernel(cdec_ref, q_ref, k_ref, v_ref, gate_ref, x_ref, hg_ref, intra_ref, cross_ref,
                     kdec_ref, wo_ref, out_ref, state_ref, y_ref, wos_ref):
    @pl.when(pl.program_id(0) == 0)
    def _():
        state_ref[...] = jnp.zeros_like(state_ref)
        wos_ref[...] = (wo_ref[...] * hg_ref[...]).astype(BF16)

    for r0 in range(0, RET_STEP, RET_C):
        rs = slice(r0, r0 + RET_C)
        for hd in range(RET_HEADS):
            q = q_ref[rs, hd * RET_DK:(hd + 1) * RET_DK]
            k = k_ref[rs, hd * RET_DK:(hd + 1) * RET_DK]
            v = v_ref[rs, hd * RET_DV:(hd + 1) * RET_DV]
            state = state_ref[hd]
            scores = lax.dot_general(q, k, (((1,), (1,)), ((), ())), preferred_element_type=F32)
            scores = (scores * intra_ref[hd]).astype(BF16)
            o = jnp.dot(scores, v, preferred_element_type=F32)
            cross = cross_ref[hd]
            o_cross = jnp.dot(q, state.astype(BF16), preferred_element_type=F32)
            o = o + o_cross * jnp.concatenate([cross] * (RET_DV // 128), axis=1)
            kdec = kdec_ref[hd]
            kd = (k.astype(F32) * jnp.concatenate([kdec] * (RET_DK // 128), axis=1)).astype(BF16)
            upd = lax.dot_general(kd, v, (((0,), (0,)), ((), ())), preferred_element_type=F32)
            state_ref[hd] = state * cdec_ref[hd] + upd
            ms = jnp.mean(o * o, axis=-1, keepdims=True)
            on = o * lax.rsqrt(ms + NORM_EPS)
            gt = gate_ref[rs, hd * RET_DV:(hd + 1) * RET_DV].astype(F32)
            y_ref[rs, hd * RET_DV:(hd + 1) * RET_DV] = (_silu(gt) * on).astype(BF16)
        out_ref[rs, :] = x_ref[rs, :] + jnp.dot(y_ref[rs, :], wos_ref[...], preferred_element_type=F32)


def _ret_core(q, k, v, gate, x, head_g_col, w_out):
    n = x.shape[0]
    c = RET_C
    log_gamma = jnp.log1p(-(2.0 ** (-5.0 - jnp.arange(RET_HEADS, dtype=F32))))
    idx = jnp.arange(c, dtype=F32)
    diff = idx[:, None] - idx[None, :]
    intra = jnp.where(diff >= 0, jnp.exp(log_gamma[:, None, None] * jnp.maximum(diff, 0.0)), 0.0)
    cross = jnp.broadcast_to(jnp.exp(log_gamma[:, None] * (idx + 1.0))[:, :, None], (RET_HEADS, c, 128))
    kdec = jnp.broadcast_to(jnp.exp(log_gamma[:, None] * (c - 1.0 - idx))[:, :, None], (RET_HEADS, c, 128))
    cdec = jnp.exp(log_gamma * c)
    return pl.pallas_call(
        _ret_core_kernel,
        grid=(n // RET_STEP,),
        in_specs=[
            pl.BlockSpec(memory_space=pltpu.SMEM),
            pl.BlockSpec((RET_STEP, RET_QK), lambda i: (i, 0)),
            pl.BlockSpec((RET_STEP, RET_QK), lambda i: (i, 0)),
            pl.BlockSpec((RET_STEP, RET_V), lambda i: (i, 0)),
            pl.BlockSpec((RET_STEP, RET_V), lambda i: (i, 0)),
            pl.BlockSpec((RET_STEP, D_MODEL), lambda i: (i, 0)),
            pl.BlockSpec((RET_V, 1), lambda i: (0, 0)),
            pl.BlockSpec((RET_HEADS, c, c), lambda i: (0, 0, 0)),
            pl.BlockSpec((RET_HEADS, c, 128), lambda i: (0, 0, 0)),
            pl.BlockSpec((RET_HEADS, c, 128), lambda i: (0, 0, 0)),
            pl.BlockSpec((RET_V, D_MODEL), lambda i: (0, 0)),
        ],
        out_specs=pl.BlockSpec((RET_STEP, D_MODEL), lambda i: (i, 0)),
        out_shape=jax.ShapeDtypeStruct((n, D_MODEL), F32),
        scratch_shapes=[
            pltpu.VMEM((RET_HEADS, RET_DK, RET_DV), F32),
            pltpu.VMEM((RET_STEP, RET_V), BF16),
            pltpu.VMEM((RET_V, D_MODEL), BF16),
        ],
        compiler_params=_cparams(("arbitrary",)),
        name="ret_core",
    )(cdec, q, k, v, gate, x, head_g_col, intra, cross, kdec, w_out)


def _conv_pw1_kernel(x_ref, rw_ref, y1_ref, y2_ref, g_ref, w_ref, b_ref, xo_ref, u_ref):
    lo1, hi1 = _unpack_bf16_pairs(y1_ref[...])
    lo2, hi2 = _unpack_bf16_pairs(y2_ref[...])
    w1, w2 = _route_weight_columns(rw_ref[...])
    x = jnp.concatenate([x_ref[:, :HALF] + w1 * lo1 + w2 * lo2, x_ref[:, HALF:] + w1 * hi1 + w2 * hi2], axis=1)
    xo_ref[...] = x
    h = _rms(x, g_ref[...]).astype(BF16)
    for j in range(D_MODEL // 512):
        a = jnp.dot(h, w_ref[:, j * 512:(j + 1) * 512].astype(BF16), preferred_element_type=F32)
        a = a + b_ref[:, j * 512:(j + 1) * 512]
        gt = jnp.dot(h, w_ref[:, D_MODEL + j * 512:D_MODEL + (j + 1) * 512].astype(BF16),
                     preferred_element_type=F32)
        gt = gt + b_ref[:, D_MODEL + j * 512:D_MODEL + (j + 1) * 512]
        u_ref[:, j * 512:(j + 1) * 512] = a * (1.0 / (1.0 + jnp.exp(-gt)))


def _conv_pw1(x, rw, yg, g, w, b):
    n = x.shape[0]
    tm = TM_CONV
    nb = n // tm
    return pl.pallas_call(
        _conv_pw1_kernel,
        grid=(nb,),
        in_specs=[
            pl.BlockSpec((tm, D_MODEL), lambda i: (i, 0)),
            pl.BlockSpec((8, tm), lambda i: (0, i)),
            pl.BlockSpec((tm, HALF), lambda i: (i, 0)),
            pl.BlockSpec((tm, HALF), lambda i: (i + nb, 0)),
            pl.BlockSpec((1, D_MODEL), lambda i: (0, 0)),
            pl.BlockSpec((D_MODEL, 2 * D_MODEL), lambda i: (0, 0)),
            pl.BlockSpec((1, 2 * D_MODEL), lambda i: (0, 0)),
        ],
        out_specs=[pl.BlockSpec((tm, D_MODEL), lambda i: (i, 0)), pl.BlockSpec((tm, D_MODEL), lambda i: (i, 0))],
        out_shape=[jax.ShapeDtypeStruct((n, D_MODEL), F32), jax.ShapeDtypeStruct((n, D_MODEL), F32)],
        compiler_params=_cparams(("arbitrary",)),
        name="conv_pw1",
    )(x, rw, yg, yg, g, w, b)


def _conv_core_kernel(u_ref, halo_ref, x_ref, wdw_ref, bdw_ref, lng_ref, lnb_ref, w2_ref, b2_ref,
                      out_ref, win_ref, z_ref):
    tm = TM_CONV
    first = pl.program_id(0) == 0
    halo = halo_ref[...]
    win_ref[0:CONV_HALO, :] = jnp.where(first, jnp.zeros_like(halo), halo)
    win_ref[CONV_HALO:CONV_HALO + tm, :] = u_ref[...]
    off = CONV_HALO - (CONV_WIDTH - 1)
    rb = CONV_ROWS
    for cc in range(D_MODEL // 128):
        cs = slice(cc * 128, (cc + 1) * 128)
        for r0 in range(0, tm, rb):
            z = bdw_ref[:, cs]
            for b in range(8):
                rows = rb if b == 0 else rb + 8
                q = None
                for o in range(off, off + CONV_WIDTH):
                    if o % 8 != b:
                        continue
                    term = win_ref[r0 + o - b:r0 + o - b + rows, cs] * wdw_ref[o - off:o - off + 1, cs]
                    q = term if q is None else q + term
                z = z + (q if b == 0 else q[b:b + rb, :])
            z_ref[r0:r0 + rb, cs] = z
    z = z_ref[...]
    mu = jnp.mean(z, axis=-1, keepdims=True)
    zc = z - mu
    var = jnp.mean(zc * zc, axis=-1, keepdims=True)
    zn = zc * lax.rsqrt(var + NORM_EPS) * lng_ref[...] + lnb_ref[...]
    y = _silu(zn).astype(BF16)
    out_ref[...] = x_ref[...] + jnp.dot(y, w2_ref[...].astype(BF16), preferred_element_type=F32) + b2_ref[...]


def _conv_core(u, x, w_dw, b_dw, ln_g, ln_b, w2, b2):
    n = x.shape[0]
    tm = TM_CONV
    r = tm // CONV_HALO
    wdw_pad = jnp.zeros((32, D_MODEL), F32).at[:CONV_WIDTH].set(w_dw)
    return pl.pallas_call(
        _conv_core_kernel,
        grid=(n // tm,),
        in_specs=[
            pl.BlockSpec((tm, D_MODEL), lambda i: (i, 0)),
            pl.BlockSpec((CONV_HALO, D_MODEL), lambda i: (jnp.maximum(i * r - 1, 0), 0)),
            pl.BlockSpec((tm, D_MODEL), lambda i: (i, 0)),
            pl.BlockSpec((32, D_MODEL), lambda i: (0, 0)),
            pl.BlockSpec((1, D_MODEL), lambda i: (0, 0)),
            pl.BlockSpec((1, D_MODEL), lambda i: (0, 0)),
            pl.BlockSpec((1, D_MODEL), lambda i: (0, 0)),
            pl.BlockSpec((D_MODEL, D_MODEL), lambda i: (0, 0)),
            pl.BlockSpec((1, D_MODEL), lambda i: (0, 0)),
        ],
        out_specs=pl.BlockSpec((tm, D_MODEL), lambda i: (i, 0)),
        out_shape=jax.ShapeDtypeStruct((n, D_MODEL), F32),
        scratch_shapes=[pltpu.VMEM((CONV_HALO + tm, D_MODEL), F32), pltpu.VMEM((tm, D_MODEL), F32)],
        compiler_params=_cparams(("arbitrary",)),
        name="conv_core",
    )(u, u, x, wdw_pad, b_dw, ln_g, ln_b, w2, b2)


def _router_kernel(x_ref, g_ref, wr_ref, br_ref, hp_ref, ri_ref, rw_ref, cnt_ref, carry_ref):
    t = T_ROUTE

    @pl.when(pl.program_id(0) == 0)
    def _():
        carry_ref[...] = jnp.zeros_like(carry_ref)

    h = _rms(x_ref[...], g_ref[...])
    hp_ref[...] = _pack_bf16_pairs(h)
    h_hi = h.astype(BF16)
    h_lo = (h - h_hi.astype(F32)).astype(BF16)
    w = wr_ref[...]
    w_hi = w.astype(BF16)
    w_lo = (w - w_hi.astype(F32)).astype(BF16)
    dn = (((1,), (1,)), ((), ()))
    p = lax.dot_general(jnp.concatenate([w_hi, w_lo], axis=0), h_hi, dn, preferred_element_type=F32)
    nr = wr_ref.shape[0]
    logits = p[0:nr] + p[nr:2 * nr] + lax.dot_general(w_hi, h_lo, dn, preferred_element_type=F32)
    logits = logits + br_ref[:, 0:1]

    best = logits[0:1]
    gi = jnp.zeros((1, t), I32)
    for j in range(1, MOE_GROUPS):
        r = logits[j:j + 1]
        up = r > best
        gi = jnp.where(up, j, gi)
        best = jnp.where(up, r, best)
    den = jnp.zeros((1, t), F32)
    for j in range(MOE_GROUPS):
        den = den + jnp.exp(logits[j:j + 1] - best)
    gate_g = 1.0 / den

    sel = logits[8:8 + MOE_EPG]
    for j in range(1, MOE_GROUPS):
        sel = jnp.where(gi == j, logits[8 + j * MOE_EPG:8 + (j + 1) * MOE_EPG], sel)

    m1 = sel[0:1]
    i1 = jnp.zeros((1, t), I32)
    for j in range(1, MOE_EPG):
        r = sel[j:j + 1]
        up = r > m1
        i1 = jnp.where(up, j, i1)
        m1 = jnp.where(up, r, m1)
    m2 = jnp.full((1, t), -jnp.inf, F32)
    i2 = jnp.zeros((1, t), I32)
    started = jnp.zeros((1, t), jnp.bool_)
    for j in range(MOE_EPG):
        r = sel[j:j + 1]
        ok = i1 != j
        up = ok & ((r > m2) | jnp.logical_not(started))
        i2 = jnp.where(up, j, i2)
        m2 = jnp.where(up, r, m2)
        started = started | ok
    e21 = jnp.exp(m2 - m1)
    p1 = 1.0 / (1.0 + e21)
    w1 = gate_g * p1
    w2 = gate_g * (e21 * p1)
    eid1 = gi * MOE_EPG + i1
    eid2 = gi * MOE_EPG + i2

    eio = lax.broadcasted_iota(I32, (MOE_EXPERTS, t), 0)
    oh1 = eio == eid1
    oh2 = eio == eid2
    oh = (oh1 | oh2).astype(F32)
    rio = lax.broadcasted_iota(I32, (t, t), 0)
    cio = lax.broadcasted_iota(I32, (t, t), 1)
    upper = (rio < cio).astype(BF16)
    cum = jnp.dot(oh.astype(BF16), upper, preferred_element_type=F32) + carry_ref[:, 0:1]
    rank1 = jnp.sum(jnp.where(oh1, cum, 0.0), axis=0, keepdims=True)
    rank2 = jnp.sum(jnp.where(oh2, cum, 0.0), axis=0, keepdims=True)
    carry_ref[...] = carry_ref[...] + jnp.sum(oh, axis=1, keepdims=True)
    cnt_ref[...] = carry_ref[...]

    zi = jnp.zeros((4, t), I32)
    ri_ref[...] = jnp.concatenate([eid1, eid2, rank1.astype(I32), rank2.astype(I32), zi], axis=0)
    zf = jnp.zeros((6, t), F32)
    rw_ref[...] = jnp.concatenate([w1, w2, zf], axis=0)


def _router(x, g, w_rg, b_rg, w_re, b_re):
    n = x.shape[0]
    t = T_ROUTE
    wr = jnp.zeros((40, D_MODEL), F32).at[0:MOE_GROUPS].set(w_rg.T).at[8:40].set(w_re.T)
    br = jnp.zeros((40,), F32).at[0:MOE_GROUPS].set(b_rg).at[8:40].set(b_re)
    br = jnp.broadcast_to(br[:, None], (40, 128))
    return pl.pallas_call(
        _router_kernel,
        grid=(n // t,),
        in_specs=[
            pl.BlockSpec((t, D_MODEL), lambda i: (i, 0)),
            pl.BlockSpec((1, D_MODEL), lambda i: (0, 0)),
            pl.BlockSpec((40, D_MODEL), lambda i: (0, 0)),
            pl.BlockSpec((40, 128), lambda i: (0, 0)),
        ],
        out_specs=[
            pl.BlockSpec((t, HALF), lambda i: (i, 0)),
            pl.BlockSpec((8, t), lambda i: (0, i)),
            pl.BlockSpec((8, t), lambda i: (0, i)),
            pl.BlockSpec((MOE_EXPERTS, 128), lambda i: (0, 0)),
        ],
        out_shape=[
            jax.ShapeDtypeStruct((n, HALF), U32),
            jax.ShapeDtypeStruct((8, n), I32),
            jax.ShapeDtypeStruct((8, n), F32),
            jax.ShapeDtypeStruct((MOE_EXPERTS, 128), F32),
        ],
        scratch_shapes=[pltpu.VMEM((MOE_EXPERTS, 128), F32)],
        compiler_params=_cparams(("arbitrary",)),
        name="moe_router",
    )(x, g, wr, br)


def _sc_mesh():
    return plsc.VectorSubcoreMesh(core_axis_name="c", subcore_axis_name="s",
                                  num_cores=SC_CORES, num_subcores=SC_SUBCORES)


def _sc_worker_id():
    return lax.axis_index("s") * SC_CORES + lax.axis_index("c")


def _sc_dispatch(hp, idx, zero_rows, total_rows):
    n = hp.shape[0]
    tpw = n // SC_WORKERS
    kd = tpw // SC_ROWS
    kp = idx.shape[1] - 2 * kd

    @functools.partial(
        pl.kernel, mesh=_sc_mesh(),
        out_type=jax.ShapeDtypeStruct((total_rows, HALF), U32),
        scratch_types=[
            pltpu.VMEM((2 * kd + kp, SC_ROWS), I32),
            pltpu.VMEM((SC_ROWS, HALF), U32), pltpu.VMEM((SC_ROWS, HALF), U32), pltpu.VMEM((SC_ROWS, HALF), U32),
            pltpu.SemaphoreType.DMA((2,)), pltpu.SemaphoreType.DMA((2,)), pltpu.SemaphoreType.DMA,
        ],
        name="moe_dispatch_sc",
    )
    def k(hp_hbm, idx_hbm, zero_hbm, xs_hbm, idx_v, buf0, buf1, zbuf, load_sem, scat_sem, pad_sem):
        wid = _sc_worker_id()
        bufs = (buf0, buf1)
        pltpu.sync_copy(idx_hbm.at[wid], idx_v)

        def load(c):
            return pltpu.make_async_copy(hp_hbm.at[pl.ds(wid * tpw + c * SC_ROWS, SC_ROWS)], bufs[c % 2],
                                         load_sem.at[c % 2])

        def scatters(c):
            return (pltpu.make_async_copy(bufs[c % 2], xs_hbm.at[idx_v.at[c]], scat_sem.at[c % 2]),
                    pltpu.make_async_copy(bufs[c % 2], xs_hbm.at[idx_v.at[kd + c]], scat_sem.at[c % 2]))

        load(0).start()
        pltpu.sync_copy(zero_hbm, zbuf)
        pads = [pltpu.make_async_copy(zbuf, xs_hbm.at[idx_v.at[2 * kd + j]], pad_sem) for j in range(kp)]
        for p in pads:
            p.start()
        for c in range(kd):
            load(c).wait()
            for d in scatters(c):
                d.start()
            if c + 1 < kd:
                if c >= 1:
                    for d in scatters(c - 1):
                        d.wait()
                load(c + 1).start()
        for c in range(max(kd - 2, 0), kd):
            for d in scatters(c):
                d.wait()
        for p in pads:
            p.wait()

    return k(hp, idx, zero_rows)


def _sc_gather(ys, idx):
    kg = idx.shape[1]
    rows_per_worker = kg * SC_ROWS

    @functools.partial(
        pl.kernel, mesh=_sc_mesh(),
        out_type=jax.ShapeDtypeStruct((SC_WORKERS * rows_per_worker, HALF), U32),
        scratch_types=[
            pltpu.VMEM((kg, SC_ROWS), I32),
            pltpu.VMEM((SC_ROWS, HALF), U32), pltpu.VMEM((SC_ROWS, HALF), U32),
            pltpu.SemaphoreType.DMA((2,)), pltpu.SemaphoreType.DMA((2,)),
        ],
        name="moe_gather_sc",
    )
    def k(ys_hbm, idx_hbm, yg_hbm, idx_v, buf0, buf1, gat_sem, out_sem):
        wid = _sc_worker_id()
        bufs = (buf0, buf1)
        pltpu.sync_copy(idx_hbm.at[wid], idx_v)

        def gather(c):
            return pltpu.make_async_copy(ys_hbm.at[idx_v.at[c]], bufs[c % 2], gat_sem.at[c % 2])

        def store(c):
            return pltpu.make_async_copy(bufs[c % 2],
                                         yg_hbm.at[pl.ds(wid * rows_per_worker + c * SC_ROWS, SC_ROWS)],
                                         out_sem.at[c % 2])

        gather(0).start()
        for c in range(kg):
            if c + 1 < kg:
                if c >= 1:
                    store(c - 1).wait()
                gather(c + 1).start()
            gather(c).wait()
            store(c).start()
        for c in range(max(kg - 2, 0), kg):
            store(c).wait()

    return k(ys, idx)


def _expert_kernel(blk_e_ref, nused_ref, first_ref, slot_ref, nxt_ref, xs_ref, wg_hbm, wu_hbm, wd_hbm,
                   ys_ref, wg_buf, wu_buf, wd_buf, sems, *, layer):
    step = pl.program_id(0)
    tm = TM_EXP

    def weight_copies(e, s):
        return (pltpu.make_async_copy(wg_hbm.at[layer, e], wg_buf.at[s], sems.at[s, 0]),
                pltpu.make_async_copy(wu_hbm.at[layer, e], wu_buf.at[s], sems.at[s, 1]),
                pltpu.make_async_copy(wd_hbm.at[layer, e], wd_buf.at[s], sems.at[s, 2]))

    def block(j):
        i = step * EXP_SUB + j
        rows = slice(j * tm, (j + 1) * tm)

        @pl.when(i < nused_ref[0])
        def _():
            s = slot_ref[i]

            if j == 0:
                @pl.when(i == 0)
                def _():
                    for c in weight_copies(blk_e_ref[0], 0):
                        c.start()

            @pl.when(first_ref[i] == 1)
            def _():
                for c in weight_copies(blk_e_ref[i], s):
                    c.wait()

                @pl.when(nxt_ref[i] >= 0)
                def _():
                    for c in weight_copies(nxt_ref[i], 1 - s):
                        c.start()

            lo, hi = _unpack_bf16_pairs(xs_ref[rows, :])
            xf = jnp.concatenate([lo, hi], axis=1)
            a = jnp.dot(xf, wg_buf[s], preferred_element_type=F32)
            b = jnp.dot(xf, wu_buf[s], preferred_element_type=F32)
            hm = _silu(a) * b
            y = jnp.dot(hm, wd_buf[s], preferred_element_type=F32)
            ys_ref[rows, :] = _pack_bf16_pairs(y)

        @pl.when(i >= nused_ref[0])
        def _():
            ys_ref[rows, :] = jnp.zeros((tm, HALF), U32)

    for j in range(EXP_SUB):
        block(j)


def _experts(xs, blk_e, nused, w_gate, w_up, w_down, layer):
    tm = TM_EXP
    p_rows = xs.shape[0] - MOE_EXPERTS * tm
    nblk = p_rows // tm
    pos = jnp.arange(nblk, dtype=I32)
    valid = pos < nused[0]
    prev_e = jnp.concatenate([jnp.full((1,), -1, I32), blk_e[:-1]])
    first = valid & (blk_e != prev_e)
    slot = (jnp.cumsum(first.astype(I32)) - 1) % 2
    first_pos = jnp.where(first, pos, nblk)
    next_first = jnp.concatenate([lax.cummin(first_pos, reverse=True)[1:], jnp.full((1,), nblk, I32)])
    nxt = jnp.where(next_first < nblk, blk_e[jnp.minimum(next_first, nblk - 1)], -1)

    def blk(i, be, nu, *_):
        return jnp.minimum(i, (nu[0] - 1) // EXP_SUB)

    grid_spec = pltpu.PrefetchScalarGridSpec(
        num_scalar_prefetch=5,
        grid=(nblk // EXP_SUB,),
        in_specs=[
            pl.BlockSpec((EXP_SUB * tm, HALF), lambda i, *sp: (blk(i, *sp), 0)),
            pl.BlockSpec(memory_space=pl.ANY),
            pl.BlockSpec(memory_space=pl.ANY),
            pl.BlockSpec(memory_space=pl.ANY),
        ],
        out_specs=pl.BlockSpec((EXP_SUB * tm, HALF), lambda i, *sp: (i, 0)),
        scratch_shapes=[
            pltpu.VMEM((2, D_MODEL, MOE_FF), F32),
            pltpu.VMEM((2, D_MODEL, MOE_FF), F32),
            pltpu.VMEM((2, MOE_FF, D_MODEL), F32),
            pltpu.SemaphoreType.DMA((2, 3)),
        ],
    )
    return pl.pallas_call(
        functools.partial(_expert_kernel, layer=layer),
        grid_spec=grid_spec,
        out_shape=jax.ShapeDtypeStruct((p_rows, HALF), U32),
        compiler_params=_cparams(("arbitrary",)),
        name="moe_experts",
    )(blk_e, nused, first.astype(I32), slot.astype(I32), nxt.astype(I32), xs, w_gate, w_up, w_down)


def _combine_kernel(x_ref, rw_ref, fg_ref, y1_ref, y2_ref, out_ref):
    lo1, hi1 = _unpack_bf16_pairs(y1_ref[...])
    lo2, hi2 = _unpack_bf16_pairs(y2_ref[...])
    w1, w2 = _route_weight_columns(rw_ref[...])
    x = x_ref[...]
    o_lo = x[:, :HALF] + w1 * lo1 + w2 * lo2
    o_hi = x[:, HALF:] + w1 * hi1 + w2 * hi2
    ms = (jnp.sum(o_lo * o_lo, axis=-1, keepdims=True)
          + jnp.sum(o_hi * o_hi, axis=-1, keepdims=True)) * (1.0 / D_MODEL)
    sc = lax.rsqrt(ms + NORM_EPS)
    o_lo = o_lo * sc * fg_ref[:, :HALF]
    o_hi = o_hi * sc * fg_ref[:, HALF:]
    out_ref[:, :HALF] = o_lo
    out_ref[:, HALF:] = o_hi


def _combine(yg, x, rw, final_g):
    n = x.shape[0]
    td = T_COMB
    nb = n // td
    return pl.pallas_call(
        _combine_kernel,
        grid=(nb,),
        in_specs=[
            pl.BlockSpec((td, D_MODEL), lambda i: (i, 0)),
            pl.BlockSpec((8, td), lambda i: (0, i)),
            pl.BlockSpec((1, D_MODEL), lambda i: (0, 0)),
            pl.BlockSpec((td, HALF), lambda i: (i, 0)),
            pl.BlockSpec((td, HALF), lambda i: (i + nb, 0)),
        ],
        out_specs=pl.BlockSpec((td, D_MODEL), lambda i: (i, 0)),
        out_shape=jax.ShapeDtypeStruct((n, D_MODEL), F32),
        compiler_params=_cparams(("arbitrary",)),
        name="moe_combine",
    )(x, rw, final_g, yg, yg)


def _moe(x, g, w_rg, b_rg, w_re, b_re, w_gate, w_up, w_down, layer):
    n = x.shape[0]
    tm = TM_EXP
    p_rows = 2 * n + MOE_EXPERTS * tm
    nblk = p_rows // tm
    hp, ri, rw, cnt = _router(x, g, w_rg, b_rg, w_re, b_re)
    counts = cnt[:, 0].astype(I32)
    pcounts = (counts + tm - 1) // tm * tm
    pend = jnp.cumsum(pcounts)
    pstart = pend - pcounts
    eio = jnp.arange(MOE_EXPERTS, dtype=I32)[:, None]
    dest1 = jnp.sum(jnp.where(ri[0][None, :] == eio, pstart[:, None], 0), axis=0) + ri[2]
    dest2 = jnp.sum(jnp.where(ri[1][None, :] == eio, pstart[:, None], 0), axis=0) + ri[3]
    blk_start = jnp.arange(nblk, dtype=I32) * tm
    blk_e = jnp.minimum(jnp.sum((pend[None, :] <= blk_start[:, None]).astype(I32), axis=1), MOE_EXPERTS - 1)
    nused = jnp.maximum(pend[-1] // tm, 1).astype(I32).reshape(1)
    r = jnp.arange(tm, dtype=I32)[None, :]
    pad_slot = jnp.where(r < (pcounts - counts)[:, None], (pstart + counts)[:, None] + r, p_rows + eio * tm + r)
    kd = n // SC_WORKERS // SC_ROWS
    idx = jnp.concatenate([dest1.reshape(SC_WORKERS, kd, SC_ROWS), dest2.reshape(SC_WORKERS, kd, SC_ROWS),
                           pad_slot.reshape(SC_WORKERS, -1, SC_ROWS)], axis=1)
    zero_rows = jnp.zeros((SC_ROWS, HALF), U32)
    xs = _sc_dispatch(hp, idx, zero_rows, p_rows + MOE_EXPERTS * tm)
    ys = _experts(xs, blk_e, nused, w_gate, w_up, w_down, layer)
    gidx = jnp.concatenate([dest1, dest2]).reshape(SC_WORKERS, -1, SC_ROWS)
    yg = _sc_gather(ys, gidx)
    return yg, rw


def kernel(x, positions, norm_mix_g, norm_ffn_g, ret_w_in, ret_head_g, ret_w_out, conv_w_pw1, conv_b_pw1, conv_w_dw, conv_b_dw, conv_ln_g, conv_ln_b, conv_w_pw2, conv_b_pw2, moe_w_rg, moe_b_rg, moe_w_re, moe_b_re, moe_w_gate, moe_w_up, moe_w_down, final_norm_g):
    b, s, d = x.shape
    n = b * s
    xt = x.reshape(n, d)
    pos = positions.reshape(n, 1)
    fg = final_norm_g.reshape(1, d)

    q, k, v, gate = _ret_inproj(xt, pos, norm_mix_g[0].reshape(1, d), ret_w_in[0])
    xt = _ret_core(q, k, v, gate, xt, ret_head_g[0].reshape(RET_V, 1), ret_w_out[0])
    yg, rw = _moe(xt, norm_ffn_g[0].reshape(1, d), moe_w_rg[0], moe_b_rg[0], moe_w_re[0], moe_b_re[0],
                  moe_w_gate, moe_w_up, moe_w_down, 0)

    xt, u = _conv_pw1(xt, rw, yg, norm_mix_g[1].reshape(1, d), conv_w_pw1[0],
                      conv_b_pw1[0].reshape(1, 2 * d))
    xt = _conv_core(u, xt, conv_w_dw[0], conv_b_dw[0].reshape(1, d), conv_ln_g[0].reshape(1, d),
                    conv_ln_b[0].reshape(1, d), conv_w_pw2[0], conv_b_pw2[0].reshape(1, d))
    yg, rw = _moe(xt, norm_ffn_g[1].reshape(1, d), moe_w_rg[1], moe_b_rg[1], moe_w_re[1], moe_b_re[1],
                  moe_w_gate, moe_w_up, moe_w_down, 1)
    xt = _combine(yg, xt, rw, fg)
    return xt.reshape(b, s, d)
```

```python
import functools
import math

import jax
import jax.numpy as jnp
import numpy as np
from jax import lax
from jax.experimental import pallas as pl
from jax.experimental.pallas import tpu as pltpu
from jax.experimental.pallas import tpu_sc as plsc

F32 = jnp.float32
BF16 = jnp.bfloat16
U32 = jnp.uint32
I32 = jnp.int32

D_MODEL = 1024
RET_HEADS = 4
RET_DK = 256
RET_DV = 512
RET_QK = RET_HEADS * RET_DK
RET_V = RET_HEADS * RET_DV
ROPE_BASE = 10000.0
CONV_WIDTH = 31
MOE_GROUPS = 4
MOE_EPG = 8
MOE_EXPERTS = MOE_GROUPS * MOE_EPG
MOE_FF = 512
NORM_EPS = 1e-6

TM_PROJ = 512
RET_C = 256
RET_STEP = 512
TM_CONV = 512
CONV_HALO = 32
CONV_ROWS = 128
CONV_STRIDE = 4
T_ROUTE = 512
TM_EXP = 256
EXP_SUB = 4
T_COMB = 512
SC_CORES = 2
SC_SUBCORES = 16
SC_WORKERS = SC_CORES * SC_SUBCORES
SC_ROWS = 64
HALF = D_MODEL // 2

VMEM_LIMIT = 56 * 1024 * 1024


def _cparams(sem, flags=None):
    return pltpu.CompilerParams(dimension_semantics=sem, vmem_limit_bytes=VMEM_LIMIT, flags=flags)


def _rms(x, g):
    ms = jnp.mean(x * x, axis=-1, keepdims=True)
    return x * lax.rsqrt(ms + NORM_EPS) * g


def _silu(x):
    return x * (1.0 / (1.0 + jnp.exp(-x)))


def _pack_bf16_pairs(y):
    lo = pltpu.bitcast(y[:, :HALF].astype(BF16).astype(F32), U32)
    hi = pltpu.bitcast(y[:, HALF:].astype(BF16).astype(F32), U32)
    return (hi & jnp.uint32(0xFFFF0000)) | (lo >> 16)


def _route_weight_columns(rw):
    t = jnp.concatenate([rw] * 16, axis=0).T
    return t[:, 0:1], t[:, 1:2]


def _unpack_bf16_pairs(p):
    lo = pltpu.bitcast(p << 16, F32)
    hi = pltpu.bitcast(p & jnp.uint32(0xFFFF0000), F32)
    return lo, hi


def _ret_inproj_kernel(x_ref, pos_ref, g_ref, inv_ref, w_ref, q_ref, k_ref, v_ref, gate_ref):
    half = RET_DK // 2
    kscale = RET_DK ** -0.5
    h = _rms(x_ref[...], g_ref[...]).astype(BF16)

    def proj(c0, width):
        return jnp.dot(h, w_ref[:, c0:c0 + width].astype(BF16), preferred_element_type=F32)

    v0 = proj(2 * RET_QK, 512)
    v_ref[:, 0:512] = v0.astype(BF16)
    zero = ((pltpu.bitcast(v0[:, 0:half], U32) >> 16) >> 16).astype(F32)
    ang = pos_ref[...].astype(F32) * inv_ref[...] + zero
    cos = jnp.cos(ang)
    sin = jnp.sin(ang)
    for j in range(RET_V // 512):
        if j > 0:
            v_ref[:, j * 512:(j + 1) * 512] = proj(2 * RET_QK + j * 512, 512).astype(BF16)
        gate_ref[:, j * 512:(j + 1) * 512] = proj(2 * RET_QK + RET_V + j * 512, 512).astype(BF16)

    for hd in range(RET_HEADS):
        for base, out_ref, cs, sn in ((0, q_ref, cos, sin), (RET_QK, k_ref, cos * kscale, sin * kscale)):
            t = proj(base + hd * RET_DK, RET_DK)
            t1 = t[:, :half]
            t2 = t[:, half:]
            out_ref[:, hd * RET_DK:hd * RET_DK + half] = (t1 * cs - t2 * sn).astype(BF16)
            out_ref[:, hd * RET_DK + half:(hd + 1) * RET_DK] = (t1 * sn + t2 * cs).astype(BF16)


def _ret_inproj(x, pos, g, w_in):
    n = x.shape[0]
    half = RET_DK // 2
    inv = (ROPE_BASE ** (-jnp.arange(half, dtype=F32) / half)).reshape(1, half)
    tm = TM_PROJ
    return pl.pallas_call(
        _ret_inproj_kernel,
        grid=(n // tm,),
        in_specs=[
            pl.BlockSpec((tm, D_MODEL), lambda i: (i, 0)),
            pl.BlockSpec((tm, 1), lambda i: (i, 0)),
            pl.BlockSpec((1, D_MODEL), lambda i: (0, 0)),
            pl.BlockSpec((1, half), lambda i: (0, 0)),
            pl.BlockSpec(w_in.shape, lambda i: (0, 0), pipeline_mode=pl.Buffered(1)),
        ],
        out_specs=[
            pl.BlockSpec((tm, RET_QK), lambda i: (i, 0)),
            pl.BlockSpec((tm, RET_QK), lambda i: (i, 0)),
            pl.BlockSpec((tm, RET_V), lambda i: (i, 0)),
            pl.BlockSpec((tm, RET_V), lambda i: (i, 0)),
        ],
        out_shape=[
            jax.ShapeDtypeStruct((n, RET_QK), BF16),
            jax.ShapeDtypeStruct((n, RET_QK), BF16),
            jax.ShapeDtypeStruct((n, RET_V), BF16),
            jax.ShapeDtypeStruct((n, RET_V), BF16),
        ],
        compiler_params=_cparams(("arbitrary",)),
        name="ret_inproj",
    )(x, pos, g, inv, w_in)


def _ret_core_kernel(cdec_ref, q_ref, k_ref, v_ref, gate_ref, x_ref, hg_ref, intra_ref, cross_ref,
                     kdec_ref, wo_ref, out_ref, state_ref, y_ref, wos_ref):
    @pl.when(pl.program_id(0) == 0)
    def _():
        state_ref[...] = jnp.zeros_like(state_ref)
        wos_ref[...] = (wo_ref[...] * hg_ref[...]).astype(BF16)

    for r0 in range(0, RET_STEP, RET_C):
        rs = slice(r0, r0 + RET_C)
        for hd in range(RET_HEADS):
            q = q_ref[rs, hd * RET_DK:(hd + 1) * RET_DK]
            k = k_ref[rs, hd * RET_DK:(hd + 1) * RET_DK]
            v = v_ref[rs, hd * RET_DV:(hd + 1) * RET_DV]
            state = state_ref[hd]
            scores = lax.dot_general(q, k, (((1,), (1,)), ((), ())), preferred_element_type=F32)
            scores = (scores * intra_ref[hd]).astype(BF16)
            o = jnp.dot(scores, v, preferred_element_type=F32)
            cross = cross_ref[hd]
            o_cross = jnp.dot(q, state.astype(BF16), preferred_element_type=F32)
            o = o + o_cross * jnp.concatenate([cross] * (RET_DV // 128), axis=1)
            kdec = kdec_ref[hd]
            kd = (k.astype(F32) * jnp.concatenate([kdec] * (RET_DK // 128), axis=1)).astype(BF16)
            upd = lax.dot_general(kd, v, (((0,), (0,)), ((), ())), preferred_element_type=F32)
            state_ref[hd] = state * cdec_ref[hd] + upd
            ms = jnp.mean(o * o, axis=-1, keepdims=True)
            on = o * lax.rsqrt(ms + NORM_EPS)
            gt = gate_ref[rs, hd * RET_DV:(hd + 1) * RET_DV].astype(F32)
            y_ref[rs, hd * RET_DV:(hd + 1) * RET_DV] = (_silu(gt) * on).astype(BF16)
        out_ref[rs, :] = x_ref[rs, :] + jnp.dot(y_ref[rs, :], wos_ref[...], preferred_element_type=F32)


def _ret_core(q, k, v, gate, x, head_g_col, w_out):
    n = x.shape[0]
    c = RET_C
    log_gamma = jnp.log1p(-(2.0 ** (-5.0 - jnp.arange(RET_HEADS, dtype=F32))))
    idx = jnp.arange(c, dtype=F32)
    diff = idx[:, None] - idx[None, :]
    intra = jnp.where(diff >= 0, jnp.exp(log_gamma[:, None, None] * jnp.maximum(diff, 0.0)), 0.0)
    cross = jnp.broadcast_to(jnp.exp(log_gamma[:, None] * (idx + 1.0))[:, :, None], (RET_HEADS, c, 128))
    kdec = jnp.broadcast_to(jnp.exp(log_gamma[:, None] * (c - 1.0 - idx))[:, :, None], (RET_HEADS, c, 128))
    cdec = jnp.exp(log_gamma * c)
    return pl.pallas_call(
        _ret_core_kernel,
        grid=(n // RET_STEP,),
        in_specs=[
            pl.BlockSpec(memory_space=pltpu.SMEM),
            pl.BlockSpec((RET_STEP, RET_QK), lambda i: (i, 0)),
            pl.BlockSpec((RET_STEP, RET_QK), lambda i: (i, 0)),
            pl.BlockSpec((RET_STEP, RET_V), lambda i: (i, 0)),
            pl.BlockSpec((RET_STEP, RET_V), lambda i: (i, 0)),
            pl.BlockSpec((RET_STEP, D_MODEL), lambda i: (i, 0)),
            pl.BlockSpec((RET_V, 1), lambda i: (0, 0)),
            pl.BlockSpec((RET_HEADS, c, c), lambda i: (0, 0, 0)),
            pl.BlockSpec((RET_HEADS, c, 128), lambda i: (0, 0, 0)),
            pl.BlockSpec((RET_HEADS, c, 128), lambda i: (0, 0, 0)),
            pl.BlockSpec((RET_V, D_MODEL), lambda i: (0, 0)),
        ],
        out_specs=pl.BlockSpec((RET_STEP, D_MODEL), lambda i: (i, 0)),
        out_shape=jax.ShapeDtypeStruct((n, D_MODEL), F32),
        scratch_shapes=[
            pltpu.VMEM((RET_HEADS, RET_DK, RET_DV), F32),
            pltpu.VMEM((RET_STEP, RET_V), BF16),
            pltpu.VMEM((RET_V, D_MODEL), BF16),
        ],
        compiler_params=_cparams(("arbitrary",)),
        name="ret_core",
    )(cdec, q, k, v, gate, x, head_g_col, intra, cross, kdec, w_out)


def _conv_pw1_kernel(x_ref, rw_ref, y1_ref, y2_ref, g_ref, w_ref, b_ref, xo_ref, u_ref):
    lo1, hi1 = _unpack_bf16_pairs(y1_ref[...])
    lo2, hi2 = _unpack_bf16_pairs(y2_ref[...])
    w1, w2 = _route_weight_columns(rw_ref[...])
    x = jnp.concatenate([x_ref[:, :HALF] + w1 * lo1 + w2 * lo2, x_ref[:, HALF:] + w1 * hi1 + w2 * hi2], axis=1)
    xo_ref[...] = x
    h = _rms(x, g_ref[...]).astype(BF16)
    for j in range(D_MODEL // 512):
        a = jnp.dot(h, w_ref[:, j * 512:(j + 1) * 512].astype(BF16), preferred_element_type=F32)
        a = a + b_ref[:, j * 512:(j + 1) * 512]
        gt = jnp.dot(h, w_ref[:, D_MODEL + j * 512:D_MODEL + (j + 1) * 512].astype(BF16),
                     preferred_element_type=F32)
        gt = gt + b_ref[:, D_MODEL + j * 512:D_MODEL + (j + 1) * 512]
        u_ref[:, j * 512:(j + 1) * 512] = a * (1.0 / (1.0 + jnp.exp(-gt)))


def _conv_pw1(x, rw, yg, g, w, b):
    n = x.shape[0]
    tm = TM_CONV
    nb = n // tm
    return pl.pallas_call(
        _conv_pw1_kernel,
        grid=(nb,),
        in_specs=[
            pl.BlockSpec((tm, D_MODEL), lambda i: (i, 0)),
            pl.BlockSpec((8, tm), lambda i: (0, i)),
            pl.BlockSpec((tm, HALF), lambda i: (i, 0)),
            pl.BlockSpec((tm, HALF), lambda i: (i + nb, 0)),
            pl.BlockSpec((1, D_MODEL), lambda i: (0, 0)),
            pl.BlockSpec((D_MODEL, 2 * D_MODEL), lambda i: (0, 0)),
            pl.BlockSpec((1, 2 * D_MODEL), lambda i: (0, 0)),
        ],
        out_specs=[pl.BlockSpec((tm, D_MODEL), lambda i: (i, 0)), pl.BlockSpec((tm, D_MODEL), lambda i: (i, 0))],
        out_shape=[jax.ShapeDtypeStruct((n, D_MODEL), F32), jax.ShapeDtypeStruct((n, D_MODEL), F32)],
        compiler_params=_cparams(("arbitrary",)),
        name="conv_pw1",
    )(x, rw, yg, yg, g, w, b)


def _conv_core_kernel(u_ref, halo_ref, x_ref, wdw_ref, bdw_ref, lng_ref, lnb_ref, w2_ref, b2_ref,
                      out_ref, win_ref, z_ref):
    tm = TM_CONV
    first = pl.program_id(0) == 0
    halo = halo_ref[...]
    halo = jnp.where(first, jnp.zeros_like(halo), halo)
    nslab = D_MODEL // 128
    for cc in range(nslab):
        cs = slice(cc * 128, (cc + 1) * 128)
        win_ref[cc, 0:CONV_HALO, :] = halo[:, cs]
        win_ref[cc, CONV_HALO:CONV_HALO + tm, :] = u_ref[:, cs]
    off = CONV_HALO - (CONV_WIDTH - 1)
    rb = CONV_ROWS
    st = CONV_STRIDE
    for cc in range(nslab):
        cs = slice(cc * 128, (cc + 1) * 128)
        for r0 in range(0, tm, rb):
            accs = [bdw_ref[:, cs]] * st
            for o in range(CONV_WIDTH):
                w_o = wdw_ref[o:o + 1, cs]
                for rho in range(st):
                    accs[rho] = accs[rho] + win_ref[cc, pl.ds(r0 + rho + off + o, rb // st, stride=st), :] * w_o
            for rho in range(st):
                z_ref[cc, pl.ds(r0 + rho, rb // st, stride=st), :] = accs[rho]
    z = jnp.concatenate([z_ref[cc] for cc in range(nslab)], axis=1)
    mu = jnp.mean(z, axis=-1, keepdims=True)
    zc = z - mu
    var = jnp.mean(zc * zc, axis=-1, keepdims=True)
    zn = zc * lax.rsqrt(var + NORM_EPS) * lng_ref[...] + lnb_ref[...]
    y = _silu(zn).astype(BF16)
    out_ref[...] = x_ref[...] + jnp.dot(y, w2_ref[...].astype(BF16), preferred_element_type=F32) + b2_ref[...]


def _conv_core(u, x, w_dw, b_dw, ln_g, ln_b, w2, b2):
    n = x.shape[0]
    tm = TM_CONV
    r = tm // CONV_HALO
    wdw_pad = jnp.zeros((32, D_MODEL), F32).at[:CONV_WIDTH].set(w_dw)
    return pl.pallas_call(
        _conv_core_kernel,
        grid=(n // tm,),
        in_specs=[
            pl.BlockSpec((tm, D_MODEL), lambda i: (i, 0)),
            pl.BlockSpec((CONV_HALO, D_MODEL), lambda i: (jnp.maximum(i * r - 1, 0), 0)),
            pl.BlockSpec((tm, D_MODEL), lambda i: (i, 0)),
            pl.BlockSpec((32, D_MODEL), lambda i: (0, 0)),
            pl.BlockSpec((1, D_MODEL), lambda i: (0, 0)),
            pl.BlockSpec((1, D_MODEL), lambda i: (0, 0)),
            pl.BlockSpec((1, D_MODEL), lambda i: (0, 0)),
            pl.BlockSpec((D_MODEL, D_MODEL), lambda i: (0, 0)),
            pl.BlockSpec((1, D_MODEL), lambda i: (0, 0)),
        ],
        out_specs=pl.BlockSpec((tm, D_MODEL), lambda i: (i, 0)),
        out_shape=jax.ShapeDtypeStruct((n, D_MODEL), F32),
        scratch_shapes=[pltpu.VMEM((D_MODEL // 128, CONV_HALO + tm, 128), F32),
                        pltpu.VMEM((D_MODEL // 128, tm, 128), F32)],
        compiler_params=_cparams(("arbitrary",)),
        name="conv_core",
    )(u, u, x, wdw_pad, b_dw, ln_g, ln_b, w2, b2)


def _router_kernel(x_ref, g_ref, wr_ref, br_ref, hp_ref, ri_ref, rw_ref, cnt_ref, carry_ref):
    t = T_ROUTE

    @pl.when(pl.program_id(0) == 0)
    def _():
        carry_ref[...] = jnp.zeros_like(carry_ref)

    h = _rms(x_ref[...], g_ref[...])
    hp_ref[...] = _pack_bf16_pairs(h)
    h_hi = h.astype(BF16)
    h_lo = (h - h_hi.astype(F32)).astype(BF16)
    w = wr_ref[...]
    w_hi = w.astype(BF16)
    w_lo = (w - w_hi.astype(F32)).astype(BF16)
    dn = (((1,), (1,)), ((), ()))
    p = lax.dot_general(jnp.concatenate([w_hi, w_lo], axis=0), h_hi, dn, preferred_element_type=F32)
    nr = wr_ref.shape[0]
    logits = p[0:nr] + p[nr:2 * nr] + lax.dot_general(w_hi, h_lo, dn, preferred_element_type=F32)
    logits = logits + br_ref[:, 0:1]

    best = logits[0:1]
    gi = jnp.zeros((1, t), I32)
    for j in range(1, MOE_GROUPS):
        r = logits[j:j + 1]
        up = r > best
        gi = jnp.where(up, j, gi)
        best = jnp.where(up, r, best)
    den = jnp.zeros((1, t), F32)
    for j in range(MOE_GROUPS):
        den = den + jnp.exp(logits[j:j + 1] - best)
    gate_g = 1.0 / den

    sel = logits[8:8 + MOE_EPG]
    for j in range(1, MOE_GROUPS):
        sel = jnp.where(gi == j, logits[8 + j * MOE_EPG:8 + (j + 1) * MOE_EPG], sel)

    m1 = sel[0:1]
    i1 = jnp.zeros((1, t), I32)
    for j in range(1, MOE_EPG):
        r = sel[j:j + 1]
        up = r > m1
        i1 = jnp.where(up, j, i1)
        m1 = jnp.where(up, r, m1)
    m2 = jnp.full((1, t), -jnp.inf, F32)
    i2 = jnp.zeros((1, t), I32)
    started = jnp.zeros((1, t), jnp.bool_)
    for j in range(MOE_EPG):
        r = sel[j:j + 1]
        ok = i1 != j
        up = ok & ((r > m2) | jnp.logical_not(started))
        i2 = jnp.where(up, j, i2)
        m2 = jnp.where(up, r, m2)
        started = started | ok
    e21 = jnp.exp(m2 - m1)
    p1 = 1.0 / (1.0 + e21)
    w1 = gate_g * p1
    w2 = gate_g * (e21 * p1)
    eid1 = gi * MOE_EPG + i1
    eid2 = gi * MOE_EPG + i2

    eio = lax.broadcasted_iota(I32, (MOE_EXPERTS, t), 0)
    oh1 = eio == eid1
    oh2 = eio == eid2
    oh = (oh1 | oh2).astype(F32)
    rio = lax.broadcasted_iota(I32, (t, t), 0)
    cio = lax.broadcasted_iota(I32, (t, t), 1)
    upper = (rio < cio).astype(BF16)
    cum = jnp.dot(oh.astype(BF16), upper, preferred_element_type=F32) + carry_ref[:, 0:1]
    rank1 = jnp.sum(jnp.where(oh1, cum, 0.0), axis=0, keepdims=True)
    rank2 = jnp.sum(jnp.where(oh2, cum, 0.0), axis=0, keepdims=True)
    carry_ref[...] = carry_ref[...] + jnp.sum(oh, axis=1, keepdims=True)
    cnt_ref[...] = carry_ref[...]

    zi = jnp.zeros((4, t), I32)
    ri_ref[...] = jnp.concatenate([eid1, eid2, rank1.astype(I32), rank2.astype(I32), zi], axis=0)
    zf = jnp.zeros((6, t), F32)
    rw_ref[...] = jnp.concatenate([w1, w2, zf], axis=0)


def _router(x, g, w_rg, b_rg, w_re, b_re):
    n = x.shape[0]
    t = T_ROUTE
    wr = jnp.zeros((40, D_MODEL), F32).at[0:MOE_GROUPS].set(w_rg.T).at[8:40].set(w_re.T)
    br = jnp.zeros((40,), F32).at[0:MOE_GROUPS].set(b_rg).at[8:40].set(b_re)
    br = jnp.broadcast_to(br[:, None], (40, 128))
    return pl.pallas_call(
        _router_kernel,
        grid=(n // t,),
        in_specs=[
            pl.BlockSpec((t, D_MODEL), lambda i: (i, 0)),
            pl.BlockSpec((1, D_MODEL), lambda i: (0, 0)),
            pl.BlockSpec((40, D_MODEL), lambda i: (0, 0)),
            pl.BlockSpec((40, 128), lambda i: (0, 0)),
        ],
        out_specs=[
            pl.BlockSpec((t, HALF), lambda i: (i, 0)),
            pl.BlockSpec((8, t), lambda i: (0, i)),
            pl.BlockSpec((8, t), lambda i: (0, i)),
            pl.BlockSpec((MOE_EXPERTS, 128), lambda i: (0, 0)),
        ],
        out_shape=[
            jax.ShapeDtypeStruct((n, HALF), U32),
            jax.ShapeDtypeStruct((8, n), I32),
            jax.ShapeDtypeStruct((8, n), F32),
            jax.ShapeDtypeStruct((MOE_EXPERTS, 128), F32),
        ],
        scratch_shapes=[pltpu.VMEM((MOE_EXPERTS, 128), F32)],
        compiler_params=_cparams(("arbitrary",)),
        name="moe_router",
    )(x, g, wr, br)


def _sc_mesh():
    return plsc.VectorSubcoreMesh(core_axis_name="c", subcore_axis_name="s",
                                  num_cores=SC_CORES, num_subcores=SC_SUBCORES)


def _sc_worker_id():
    return lax.axis_index("s") * SC_CORES + lax.axis_index("c")


def _sc_dispatch(hp, idx, zero_rows, total_rows):
    n = hp.shape[0]
    tpw = n // SC_WORKERS
    kd = tpw // SC_ROWS
    kp = idx.shape[1] - 2 * kd

    @functools.partial(
        pl.kernel, mesh=_sc_mesh(),
        out_type=jax.ShapeDtypeStruct((total_rows, HALF), U32),
        scratch_types=[
            pltpu.VMEM((2 * kd + kp, SC_ROWS), I32),
            pltpu.VMEM((SC_ROWS, HALF), U32), pltpu.VMEM((SC_ROWS, HALF), U32), pltpu.VMEM((SC_ROWS, HALF), U32),
            pltpu.SemaphoreType.DMA((2,)), pltpu.SemaphoreType.DMA((2,)), pltpu.SemaphoreType.DMA,
        ],
        name="moe_dispatch_sc",
    )
    def k(hp_hbm, idx_hbm, zero_hbm, xs_hbm, idx_v, buf0, buf1, zbuf, load_sem, scat_sem, pad_sem):
        wid = _sc_worker_id()
        bufs = (buf0, buf1)
        pltpu.sync_copy(idx_hbm.at[wid], idx_v)

        def load(c):
            return pltpu.make_async_copy(hp_hbm.at[pl.ds(wid * tpw + c * SC_ROWS, SC_ROWS)], bufs[c % 2],
                                         load_sem.at[c % 2])

        def scatters(c):
            return (pltpu.make_async_copy(bufs[c % 2], xs_hbm.at[idx_v.at[c]], scat_sem.at[c % 2]),
                    pltpu.make_async_copy(bufs[c % 2], xs_hbm.at[idx_v.at[kd + c]], scat_sem.at[c % 2]))

        load(0).start()
        pltpu.sync_copy(zero_hbm, zbuf)
        pads = [pltpu.make_async_copy(zbuf, xs_hbm.at[idx_v.at[2 * kd + j]], pad_sem) for j in range(kp)]
        for p in pads:
            p.start()
        for c in range(kd):
            load(c).wait()
            for d in scatters(c):
                d.start()
            if c + 1 < kd:
                if c >= 1:
                    for d in scatters(c - 1):
                        d.wait()
                load(c + 1).start()
        for c in range(max(kd - 2, 0), kd):
            for d in scatters(c):
                d.wait()
        for p in pads:
            p.wait()

    return k(hp, idx, zero_rows)


def _sc_gather(ys, idx):
    kg = idx.shape[1]
    rows_per_worker = kg * SC_ROWS

    @functools.partial(
        pl.kernel, mesh=_sc_mesh(),
        out_type=jax.ShapeDtypeStruct((SC_WORKERS * rows_per_worker, HALF), U32),
        scratch_types=[
            pltpu.VMEM((kg, SC_ROWS), I32),
            pltpu.VMEM((SC_ROWS, HALF), U32), pltpu.VMEM((SC_ROWS, HALF), U32),
            pltpu.SemaphoreType.DMA((2,)), pltpu.SemaphoreType.DMA((2,)),
        ],
        name="moe_gather_sc",
    )
    def k(ys_hbm, idx_hbm, yg_hbm, idx_v, buf0, buf1, gat_sem, out_sem):
        wid = _sc_worker_id()
        bufs = (buf0, buf1)
        pltpu.sync_copy(idx_hbm.at[wid], idx_v)

        def gather(c):
            return pltpu.make_async_copy(ys_hbm.at[idx_v.at[c]], bufs[c % 2], gat_sem.at[c % 2])

        def store(c):
            return pltpu.make_async_copy(bufs[c % 2],
                                         yg_hbm.at[pl.ds(wid * rows_per_worker + c * SC_ROWS, SC_ROWS)],
                                         out_sem.at[c % 2])

        gather(0).start()
        for c in range(kg):
            if c + 1 < kg:
                if c >= 1:
                    store(c - 1).wait()
                gather(c + 1).start()
            gather(c).wait()
            store(c).start()
        for c in range(max(kg - 2, 0), kg):
            store(c).wait()

    return k(ys, idx)


def _expert_kernel(blk_e_ref, nused_ref, first_ref, slot_ref, nxt_ref, xs_ref, wg_hbm, wu_hbm, wd_hbm,
                   ys_ref, wg_buf, wu_buf, wd_buf, sems, *, layer):
    step = pl.program_id(0)
    tm = TM_EXP

    def weight_copies(e, s):
        return (pltpu.make_async_copy(wg_hbm.at[layer, e], wg_buf.at[s], sems.at[s, 0]),
                pltpu.make_async_copy(wu_hbm.at[layer, e], wu_buf.at[s], sems.at[s, 1]),
                pltpu.make_async_copy(wd_hbm.at[layer, e], wd_buf.at[s], sems.at[s, 2]))

    def block(j):
        i = step * EXP_SUB + j
        rows = slice(j * tm, (j + 1) * tm)

        @pl.when(i < nused_ref[0])
        def _():
            s = slot_ref[i]

            if j == 0:
                @pl.when(i == 0)
                def _():
                    for c in weight_copies(blk_e_ref[0], 0):
                        c.start()

            @pl.when(first_ref[i] == 1)
            def _():
                for c in weight_copies(blk_e_ref[i], s):
                    c.wait()

                @pl.when(nxt_ref[i] >= 0)
                def _():
                    for c in weight_copies(nxt_ref[i], 1 - s):
                        c.start()

            lo, hi = _unpack_bf16_pairs(xs_ref[rows, :])
            xf = jnp.concatenate([lo, hi], axis=1)
            a = jnp.dot(xf, wg_buf[s], preferred_element_type=F32)
            b = jnp.dot(xf, wu_buf[s], preferred_element_type=F32)
            hm = _silu(a) * b
            y = jnp.dot(hm, wd_buf[s], preferred_element_type=F32)
            ys_ref[rows, :] = _pack_bf16_pairs(y)

        @pl.when(i >= nused_ref[0])
        def _():
            ys_ref[rows, :] = jnp.zeros((tm, HALF), U32)

    for j in range(EXP_SUB):
        block(j)


def _experts(xs, blk_e, nused, w_gate, w_up, w_down, layer):
    tm = TM_EXP
    p_rows = xs.shape[0] - MOE_EXPERTS * tm
    nblk = p_rows // tm
    pos = jnp.arange(nblk, dtype=I32)
    valid = pos < nused[0]
    prev_e = jnp.concatenate([jnp.full((1,), -1, I32), blk_e[:-1]])
    first = valid & (blk_e != prev_e)
    slot = (jnp.cumsum(first.astype(I32)) - 1) % 2
    first_pos = jnp.where(first, pos, nblk)
    next_first = jnp.concatenate([lax.cummin(first_pos, reverse=True)[1:], jnp.full((1,), nblk, I32)])
    nxt = jnp.where(next_first < nblk, blk_e[jnp.minimum(next_first, nblk - 1)], -1)

    def blk(i, be, nu, *_):
        return jnp.minimum(i, (nu[0] - 1) // EXP_SUB)

    grid_spec = pltpu.PrefetchScalarGridSpec(
        num_scalar_prefetch=5,
        grid=(nblk // EXP_SUB,),
        in_specs=[
            pl.BlockSpec((EXP_SUB * tm, HALF), lambda i, *sp: (blk(i, *sp), 0)),
            pl.BlockSpec(memory_space=pl.ANY),
            pl.BlockSpec(memory_space=pl.ANY),
            pl.BlockSpec(memory_space=pl.ANY),
        ],
        out_specs=pl.BlockSpec((EXP_SUB * tm, HALF), lambda i, *sp: (i, 0)),
        scratch_shapes=[
            pltpu.VMEM((2, D_MODEL, MOE_FF), F32),
            pltpu.VMEM((2, D_MODEL, MOE_FF), F32),
            pltpu.VMEM((2, MOE_FF, D_MODEL), F32),
            pltpu.SemaphoreType.DMA((2, 3)),
        ],
    )
    return pl.pallas_call(
        functools.partial(_expert_kernel, layer=layer),
        grid_spec=grid_spec,
        out_shape=jax.ShapeDtypeStruct((p_rows, HALF), U32),
        compiler_params=_cparams(("arbitrary",)),
        name="moe_experts",
    )(blk_e, nused, first.astype(I32), slot.astype(I32), nxt.astype(I32), xs, w_gate, w_up, w_down)


def _combine_kernel(x_ref, rw_ref, fg_ref, y1_ref, y2_ref, out_ref):
    lo1, hi1 = _unpack_bf16_pairs(y1_ref[...])
    lo2, hi2 = _unpack_bf16_pairs(y2_ref[...])
    w1, w2 = _route_weight_columns(rw_ref[...])
    x = x_ref[...]
    o_lo = x[:, :HALF] + w1 * lo1 + w2 * lo2
    o_hi = x[:, HALF:] + w1 * hi1 + w2 * hi2
    ms = (jnp.sum(o_lo * o_lo, axis=-1, keepdims=True)
          + jnp.sum(o_hi * o_hi, axis=-1, keepdims=True)) * (1.0 / D_MODEL)
    sc = lax.rsqrt(ms + NORM_EPS)
    o_lo = o_lo * sc * fg_ref[:, :HALF]
    o_hi = o_hi * sc * fg_ref[:, HALF:]
    out_ref[:, :HALF] = o_lo
    out_ref[:, HALF:] = o_hi


def _combine(yg, x, rw, final_g):
    n = x.shape[0]
    td = T_COMB
    nb = n // td
    return pl.pallas_call(
        _combine_kernel,
        grid=(nb,),
        in_specs=[
            pl.BlockSpec((td, D_MODEL), lambda i: (i, 0)),
            pl.BlockSpec((8, td), lambda i: (0, i)),
            pl.BlockSpec((1, D_MODEL), lambda i: (0, 0)),
            pl.BlockSpec((td, HALF), lambda i: (i, 0)),
            pl.BlockSpec((td, HALF), lambda i: (i + nb, 0)),
        ],
        out_specs=pl.BlockSpec((td, D_MODEL), lambda i: (i, 0)),
        out_shape=jax.ShapeDtypeStruct((n, D_MODEL), F32),
        compiler_params=_cparams(("arbitrary",)),
        name="moe_combine",
    )(x, rw, final_g, yg, yg)


def _moe(x, g, w_rg, b_rg, w_re, b_re, w_gate, w_up, w_down, layer):
    n = x.shape[0]
    tm = TM_EXP
    p_rows = 2 * n + MOE_EXPERTS * tm
    nblk = p_rows // tm
    hp, ri, rw, cnt = _router(x, g, w_rg, b_rg, w_re, b_re)
    counts = cnt[:, 0].astype(I32)
    pcounts = (counts + tm - 1) // tm * tm
    pend = jnp.cumsum(pcounts)
    pstart = pend - pcounts
    eio = jnp.arange(MOE_EXPERTS, dtype=I32)[:, None]
    dest1 = jnp.sum(jnp.where(ri[0][None, :] == eio, pstart[:, None], 0), axis=0) + ri[2]
    dest2 = jnp.sum(jnp.where(ri[1][None, :] == eio, pstart[:, None], 0), axis=0) + ri[3]
    blk_start = jnp.arange(nblk, dtype=I32) * tm
    blk_e = jnp.minimum(jnp.sum((pend[None, :] <= blk_start[:, None]).astype(I32), axis=1), MOE_EXPERTS - 1)
    nused = jnp.maximum(pend[-1] // tm, 1).astype(I32).reshape(1)
    r = jnp.arange(tm, dtype=I32)[None, :]
    pad_slot = jnp.where(r < (pcounts - counts)[:, None], (pstart + counts)[:, None] + r, p_rows + eio * tm + r)
    kd = n // SC_WORKERS // SC_ROWS
    idx = jnp.concatenate([dest1.reshape(SC_WORKERS, kd, SC_ROWS), dest2.reshape(SC_WORKERS, kd, SC_ROWS),
                           pad_slot.reshape(SC_WORKERS, -1, SC_ROWS)], axis=1)
    zero_rows = jnp.zeros((SC_ROWS, HALF), U32)
    xs = _sc_dispatch(hp, idx, zero_rows, p_rows + MOE_EXPERTS * tm)
    ys = _experts(xs, blk_e, nused, w_gate, w_up, w_down, layer)
    gidx = jnp.concatenate([dest1, dest2]).reshape(SC_WORKERS, -1, SC_ROWS)
    yg = _sc_gather(ys, gidx)
    return yg, rw


def kernel(x, positions, norm_mix_g, norm_ffn_g, ret_w_in, ret_head_g, ret_w_out, conv_w_pw1, conv_b_pw1, conv_w_dw, conv_b_dw, conv_ln_g, conv_ln_b, conv_w_pw2, conv_b_pw2, moe_w_rg, moe_b_rg, moe_w_re, moe_b_re, moe_w_gate, moe_w_up, moe_w_down, final_norm_g):
    b, s, d = x.shape
    n = b * s
    xt = x.reshape(n, d)
    pos = positions.reshape(n, 1)
    fg = final_norm_g.reshape(1, d)

    q, k, v, gate = _ret_inproj(xt, pos, norm_mix_g[0].reshape(1, d), ret_w_in[0])
    xt = _ret_core(q, k, v, gate, xt, ret_head_g[0].reshape(RET_V, 1), ret_w_out[0])
    yg, rw = _moe(xt, norm_ffn_g[0].reshape(1, d), moe_w_rg[0], moe_b_rg[0], moe_w_re[0], moe_b_re[0],
                  moe_w_gate, moe_w_up, moe_w_down, 0)

    xt, u = _conv_pw1(xt, rw, yg, norm_mix_g[1].reshape(1, d), conv_w_pw1[0],
                      conv_b_pw1[0].reshape(1, 2 * d))
    xt = _conv_core(u, xt, conv_w_dw[0], conv_b_dw[0].reshape(1, d), conv_ln_g[0].reshape(1, d),
                    conv_ln_b[0].reshape(1, d), conv_w_pw2[0], conv_b_pw2[0].reshape(1, d))
    yg, rw = _moe(xt, norm_ffn_g[1].reshape(1, d), moe_w_rg[1], moe_b_rg[1], moe_w_re[1], moe_b_re[1],
                  moe_w_gate, moe_w_up, moe_w_down, 1)
    xt = _combine(yg, xt, rw, fg)
    return xt.reshape(b, s, d)
```

```python
import functools
import math

import jax
import jax.numpy as jnp
import numpy as np
from jax import lax
from jax.experimental import pallas as pl
from jax.experimental.pallas import tpu as pltpu
from jax.experimental.pallas import tpu_sc as plsc

F32 = jnp.float32
BF16 = jnp.bfloat16
U32 = jnp.uint32
I32 = jnp.int32

D_MODEL = 1024
RET_HEADS = 4
RET_DK = 256
RET_DV = 512
RET_QK = RET_HEADS * RET_DK
RET_V = RET_HEADS * RET_DV
ROPE_BASE = 10000.0
CONV_WIDTH = 31
MOE_GROUPS = 4
MOE_EPG = 8
MOE_EXPERTS = MOE_GROUPS * MOE_EPG
MOE_FF = 512
NORM_EPS = 1e-6

TM_PROJ = 512
RET_C = 256
RET_STEP = 512
TM_CONV = 512
CONV_HALO = 32
CONV_ROWS = 128
CONV_STRIDE = 4
T_ROUTE = 512
TM_EXP = 256
EXP_SUB = 4
EXP_SLOTS = EXP_SUB + 1
T_COMB = 512
SC_CORES = 2
SC_SUBCORES = 16
SC_WORKERS = SC_CORES * SC_SUBCORES
SC_ROWS = 64
HALF = D_MODEL // 2

VMEM_LIMIT = 56 * 1024 * 1024


def _cparams(sem, flags=None):
    return pltpu.CompilerParams(dimension_semantics=sem, vmem_limit_bytes=VMEM_LIMIT, flags=flags)


def _rms(x, g):
    ms = jnp.mean(x * x, axis=-1, keepdims=True)
    return x * lax.rsqrt(ms + NORM_EPS) * g


def _silu(x):
    return x * (1.0 / (1.0 + jnp.exp(-x)))


def _pack_bf16_pairs(y):
    lo = pltpu.bitcast(y[:, :HALF].astype(BF16).astype(F32), U32)
    hi = pltpu.bitcast(y[:, HALF:].astype(BF16).astype(F32), U32)
    return (hi & jnp.uint32(0xFFFF0000)) | (lo >> 16)


def _route_weight_columns(rw):
    t = jnp.concatenate([rw] * 16, axis=0).T
    return t[:, 0:1], t[:, 1:2]


def _unpack_bf16_pairs(p):
    lo = pltpu.bitcast(p << 16, F32)
    hi = pltpu.bitcast(p & jnp.uint32(0xFFFF0000), F32)
    return lo, hi


def _ret_inproj_kernel(x_ref, pos_ref, g_ref, inv_ref, w_ref, q_ref, k_ref, v_ref, gate_ref):
    half = RET_DK // 2
    kscale = RET_DK ** -0.5
    h = _rms(x_ref[...], g_ref[...]).astype(BF16)

    def proj(c0, width):
        return jnp.dot(h, w_ref[:, c0:c0 + width].astype(BF16), preferred_element_type=F32)

    v0 = proj(2 * RET_QK, 512)
    v_ref[:, 0:512] = v0.astype(BF16)
    zero = ((pltpu.bitcast(v0[:, 0:half], U32) >> 16) >> 16).astype(F32)
    ang = pos_ref[...].astype(F32) * inv_ref[...] + zero
    cos = jnp.cos(ang)
    sin = jnp.sin(ang)
    for j in range(RET_V // 512):
        if j > 0:
            v_ref[:, j * 512:(j + 1) * 512] = proj(2 * RET_QK + j * 512, 512).astype(BF16)
        gate_ref[:, j * 512:(j + 1) * 512] = proj(2 * RET_QK + RET_V + j * 512, 512).astype(BF16)

    for hd in range(RET_HEADS):
        for base, out_ref, cs, sn in ((0, q_ref, cos, sin), (RET_QK, k_ref, cos * kscale, sin * kscale)):
            t = proj(base + hd * RET_DK, RET_DK)
            t1 = t[:, :half]
            t2 = t[:, half:]
            out_ref[:, hd * RET_DK:hd * RET_DK + half] = (t1 * cs - t2 * sn).astype(BF16)
            out_ref[:, hd * RET_DK + half:(hd + 1) * RET_DK] = (t1 * sn + t2 * cs).astype(BF16)


def _ret_inproj(x, pos, g, w_in):
    n = x.shape[0]
    half = RET_DK // 2
    inv = (ROPE_BASE ** (-jnp.arange(half, dtype=F32) / half)).reshape(1, half)
    tm = TM_PROJ
    return pl.pallas_call(
        _ret_inproj_kernel,
        grid=(n // tm,),
        in_specs=[
            pl.BlockSpec((tm, D_MODEL), lambda i: (i, 0)),
            pl.BlockSpec((tm, 1), lambda i: (i, 0)),
            pl.BlockSpec((1, D_MODEL), lambda i: (0, 0)),
            pl.BlockSpec((1, half), lambda i: (0, 0)),
            pl.BlockSpec(w_in.shape, lambda i: (0, 0), pipeline_mode=pl.Buffered(1)),
        ],
        out_specs=[
            pl.BlockSpec((tm, RET_QK), lambda i: (i, 0)),
            pl.BlockSpec((tm, RET_QK), lambda i: (i, 0)),
            pl.BlockSpec((tm, RET_V), lambda i: (i, 0)),
            pl.BlockSpec((tm, RET_V), lambda i: (i, 0)),
        ],
        out_shape=[
            jax.ShapeDtypeStruct((n, RET_QK), BF16),
            jax.ShapeDtypeStruct((n, RET_QK), BF16),
            jax.ShapeDtypeStruct((n, RET_V), BF16),
            jax.ShapeDtypeStruct((n, RET_V), BF16),
        ],
        compiler_params=_cparams(("arbitrary",)),
        name="ret_inproj",
    )(x, pos, g, inv, w_in)


def _ret_core_kernel(cdec_ref, q_ref, k_ref, v_ref, gate_ref, x_ref, hg_ref, intra_ref, cross_ref,
                     kdec_ref, wo_ref, out_ref, state_ref, y_ref, wos_ref):
    @pl.when(pl.program_id(0) == 0)
    def _():
        state_ref[...] = jnp.zeros_like(state_ref)
        wos_ref[...] = (wo_ref[...] * hg_ref[...]).astype(BF16)

    for r0 in range(0, RET_STEP, RET_C):
        rs = slice(r0, r0 + RET_C)
        for hd in range(RET_HEADS):
            q = q_ref[rs, hd * RET_DK:(hd + 1) * RET_DK]
            k = k_ref[rs, hd * RET_DK:(hd + 1) * RET_DK]
            v = v_ref[rs, hd * RET_DV:(hd + 1) * RET_DV]
            state = state_ref[hd]
            scores = lax.dot_general(q, k, (((1,), (1,)), ((), ())), preferred_element_type=F32)
            scores = (scores * intra_ref[hd]).astype(BF16)
            o = jnp.dot(scores, v, preferred_element_type=F32)
            cross = cross_ref[hd]
            o_cross = jnp.dot(q, state.astype(BF16), preferred_element_type=F32)
            o = o + o_cross * jnp.concatenate([cross] * (RET_DV // 128), axis=1)
            kdec = kdec_ref[hd]
            kd = (k.astype(F32) * jnp.concatenate([kdec] * (RET_DK // 128), axis=1)).astype(BF16)
            upd = lax.dot_general(kd, v, (((0,), (0,)), ((), ())), preferred_element_type=F32)
            state_ref[hd] = state * cdec_ref[hd] + upd
            ms = jnp.mean(o * o, axis=-1, keepdims=True)
            on = o * lax.rsqrt(ms + NORM_EPS)
            gt = gate_ref[rs, hd * RET_DV:(hd + 1) * RET_DV].astype(F32)
            y_ref[rs, hd * RET_DV:(hd + 1) * RET_DV] = (_silu(gt) * on).astype(BF16)
        out_ref[rs, :] = x_ref[rs, :] + jnp.dot(y_ref[rs, :], wos_ref[...], preferred_element_type=F32)


def _ret_core(q, k, v, gate, x, head_g_col, w_out):
    n = x.shape[0]
    c = RET_C
    log_gamma = jnp.log1p(-(2.0 ** (-5.0 - jnp.arange(RET_HEADS, dtype=F32))))
    idx = jnp.arange(c, dtype=F32)
    diff = idx[:, None] - idx[None, :]
    intra = jnp.where(diff >= 0, jnp.exp(log_gamma[:, None, None] * jnp.maximum(diff, 0.0)), 0.0)
    cross = jnp.broadcast_to(jnp.exp(log_gamma[:, None] * (idx + 1.0))[:, :, None], (RET_HEADS, c, 128))
    kdec = jnp.broadcast_to(jnp.exp(log_gamma[:, None] * (c - 1.0 - idx))[:, :, None], (RET_HEADS, c, 128))
    cdec = jnp.exp(log_gamma * c)
    return pl.pallas_call(
        _ret_core_kernel,
        grid=(n // RET_STEP,),
        in_specs=[
            pl.BlockSpec(memory_space=pltpu.SMEM),
            pl.BlockSpec((RET_STEP, RET_QK), lambda i: (i, 0)),
            pl.BlockSpec((RET_STEP, RET_QK), lambda i: (i, 0)),
            pl.BlockSpec((RET_STEP, RET_V), lambda i: (i, 0)),
            pl.BlockSpec((RET_STEP, RET_V), lambda i: (i, 0)),
            pl.BlockSpec((RET_STEP, D_MODEL), lambda i: (i, 0)),
            pl.BlockSpec((RET_V, 1), lambda i: (0, 0)),
            pl.BlockSpec((RET_HEADS, c, c), lambda i: (0, 0, 0)),
            pl.BlockSpec((RET_HEADS, c, 128), lambda i: (0, 0, 0)),
            pl.BlockSpec((RET_HEADS, c, 128), lambda i: (0, 0, 0)),
            pl.BlockSpec((RET_V, D_MODEL), lambda i: (0, 0)),
        ],
        out_specs=pl.BlockSpec((RET_STEP, D_MODEL), lambda i: (i, 0)),
        out_shape=jax.ShapeDtypeStruct((n, D_MODEL), F32),
        scratch_shapes=[
            pltpu.VMEM((RET_HEADS, RET_DK, RET_DV), F32),
            pltpu.VMEM((RET_STEP, RET_V), BF16),
            pltpu.VMEM((RET_V, D_MODEL), BF16),
        ],
        compiler_params=_cparams(("arbitrary",)),
        name="ret_core",
    )(cdec, q, k, v, gate, x, head_g_col, intra, cross, kdec, w_out)


def _conv_pw1_kernel(x_ref, rw_ref, y1_ref, y2_ref, g_ref, w_ref, b_ref, xo_ref, u_ref):
    lo1, hi1 = _unpack_bf16_pairs(y1_ref[...])
    lo2, hi2 = _unpack_bf16_pairs(y2_ref[...])
    w1, w2 = _route_weight_columns(rw_ref[...])
    x = jnp.concatenate([x_ref[:, :HALF] + w1 * lo1 + w2 * lo2, x_ref[:, HALF:] + w1 * hi1 + w2 * hi2], axis=1)
    xo_ref[...] = x
    h = _rms(x, g_ref[...]).astype(BF16)
    for j in range(D_MODEL // 512):
        a = jnp.dot(h, w_ref[:, j * 512:(j + 1) * 512].astype(BF16), preferred_element_type=F32)
        a = a + b_ref[:, j * 512:(j + 1) * 512]
        gt = jnp.dot(h, w_ref[:, D_MODEL + j * 512:D_MODEL + (j + 1) * 512].astype(BF16),
                     preferred_element_type=F32)
        gt = gt + b_ref[:, D_MODEL + j * 512:D_MODEL + (j + 1) * 512]
        u_ref[:, j * 512:(j + 1) * 512] = a * (1.0 / (1.0 + jnp.exp(-gt)))


def _conv_pw1(x, rw, yg, g, w, b):
    n = x.shape[0]
    tm = TM_CONV
    nb = n // tm
    return pl.pallas_call(
        _conv_pw1_kernel,
        grid=(nb,),
        in_specs=[
            pl.BlockSpec((tm, D_MODEL), lambda i: (i, 0)),
            pl.BlockSpec((8, tm), lambda i: (0, i)),
            pl.BlockSpec((tm, HALF), lambda i: (i, 0)),
            pl.BlockSpec((tm, HALF), lambda i: (i + nb, 0)),
            pl.BlockSpec((1, D_MODEL), lambda i: (0, 0)),
            pl.BlockSpec((D_MODEL, 2 * D_MODEL), lambda i: (0, 0)),
            pl.BlockSpec((1, 2 * D_MODEL), lambda i: (0, 0)),
        ],
        out_specs=[pl.BlockSpec((tm, D_MODEL), lambda i: (i, 0)), pl.BlockSpec((tm, D_MODEL), lambda i: (i, 0))],
        out_shape=[jax.ShapeDtypeStruct((n, D_MODEL), F32), jax.ShapeDtypeStruct((n, D_MODEL), F32)],
        compiler_params=_cparams(("arbitrary",)),
        name="conv_pw1",
    )(x, rw, yg, yg, g, w, b)


def _conv_core_kernel(u_ref, halo_ref, x_ref, wdw_ref, bdw_ref, lng_ref, lnb_ref, w2_ref, b2_ref,
                      out_ref, win_ref, z_ref):
    tm = TM_CONV
    first = pl.program_id(0) == 0
    halo = halo_ref[...]
    halo = jnp.where(first, jnp.zeros_like(halo), halo)
    nslab = D_MODEL // 128
    for cc in range(nslab):
        cs = slice(cc * 128, (cc + 1) * 128)
        win_ref[cc, 0:CONV_HALO, :] = halo[:, cs]
        win_ref[cc, CONV_HALO:CONV_HALO + tm, :] = u_ref[:, cs]
    off = CONV_HALO - (CONV_WIDTH - 1)
    rb = CONV_ROWS
    st = CONV_STRIDE
    for cc in range(nslab):
        cs = slice(cc * 128, (cc + 1) * 128)
        for r0 in range(0, tm, rb):
            accs = [bdw_ref[:, cs]] * st
            for o in range(CONV_WIDTH):
                w_o = wdw_ref[o:o + 1, cs]
                for rho in range(st):
                    accs[rho] = accs[rho] + win_ref[cc, pl.ds(r0 + rho + off + o, rb // st, stride=st), :] * w_o
            for rho in range(st):
                z_ref[cc, pl.ds(r0 + rho, rb // st, stride=st), :] = accs[rho]
    z = jnp.concatenate([z_ref[cc] for cc in range(nslab)], axis=1)
    mu = jnp.mean(z, axis=-1, keepdims=True)
    zc = z - mu
    var = jnp.mean(zc * zc, axis=-1, keepdims=True)
    zn = zc * lax.rsqrt(var + NORM_EPS) * lng_ref[...] + lnb_ref[...]
    y = _silu(zn).astype(BF16)
    out_ref[...] = x_ref[...] + jnp.dot(y, w2_ref[...].astype(BF16), preferred_element_type=F32) + b2_ref[...]


def _conv_core(u, x, w_dw, b_dw, ln_g, ln_b, w2, b2):
    n = x.shape[0]
    tm = TM_CONV
    r = tm // CONV_HALO
    wdw_pad = jnp.zeros((32, D_MODEL), F32).at[:CONV_WIDTH].set(w_dw)
    return pl.pallas_call(
        _conv_core_kernel,
        grid=(n // tm,),
        in_specs=[
            pl.BlockSpec((tm, D_MODEL), lambda i: (i, 0)),
            pl.BlockSpec((CONV_HALO, D_MODEL), lambda i: (jnp.maximum(i * r - 1, 0), 0)),
            pl.BlockSpec((tm, D_MODEL), lambda i: (i, 0)),
            pl.BlockSpec((32, D_MODEL), lambda i: (0, 0)),
            pl.BlockSpec((1, D_MODEL), lambda i: (0, 0)),
            pl.BlockSpec((1, D_MODEL), lambda i: (0, 0)),
            pl.BlockSpec((1, D_MODEL), lambda i: (0, 0)),
            pl.BlockSpec((D_MODEL, D_MODEL), lambda i: (0, 0)),
            pl.BlockSpec((1, D_MODEL), lambda i: (0, 0)),
        ],
        out_specs=pl.BlockSpec((tm, D_MODEL), lambda i: (i, 0)),
        out_shape=jax.ShapeDtypeStruct((n, D_MODEL), F32),
        scratch_shapes=[pltpu.VMEM((D_MODEL // 128, CONV_HALO + tm, 128), F32),
                        pltpu.VMEM((D_MODEL // 128, tm, 128), F32)],
        compiler_params=_cparams(("arbitrary",)),
        name="conv_core",
    )(u, u, x, wdw_pad, b_dw, ln_g, ln_b, w2, b2)


def _router_kernel(x_ref, g_ref, wr_ref, br_ref, hp_ref, ri_ref, rw_ref, cnt_ref, carry_ref):
    t = T_ROUTE

    @pl.when(pl.program_id(0) == 0)
    def _():
        carry_ref[...] = jnp.zeros_like(carry_ref)

    h = _rms(x_ref[...], g_ref[...])
    hp_ref[...] = _pack_bf16_pairs(h)
    h_hi = h.astype(BF16)
    h_lo = (h - h_hi.astype(F32)).astype(BF16)
    w = wr_ref[...]
    w_hi = w.astype(BF16)
    w_lo = (w - w_hi.astype(F32)).astype(BF16)
    dn = (((1,), (1,)), ((), ()))
    p = lax.dot_general(jnp.concatenate([w_hi, w_lo], axis=0), h_hi, dn, preferred_element_type=F32)
    nr = wr_ref.shape[0]
    logits = p[0:nr] + p[nr:2 * nr] + lax.dot_general(w_hi, h_lo, dn, preferred_element_type=F32)
    logits = logits + br_ref[:, 0:1]

    best = logits[0:1]
    gi = jnp.zeros((1, t), I32)
    for j in range(1, MOE_GROUPS):
        r = logits[j:j + 1]
        up = r > best
        gi = jnp.where(up, j, gi)
        best = jnp.where(up, r, best)
    den = jnp.zeros((1, t), F32)
    for j in range(MOE_GROUPS):
        den = den + jnp.exp(logits[j:j + 1] - best)
    gate_g = 1.0 / den

    sel = logits[8:8 + MOE_EPG]
    for j in range(1, MOE_GROUPS):
        sel = jnp.where(gi == j, logits[8 + j * MOE_EPG:8 + (j + 1) * MOE_EPG], sel)

    m1 = sel[0:1]
    i1 = jnp.zeros((1, t), I32)
    for j in range(1, MOE_EPG):
        r = sel[j:j + 1]
        up = r > m1
        i1 = jnp.where(up, j, i1)
        m1 = jnp.where(up, r, m1)
    m2 = jnp.full((1, t), -jnp.inf, F32)
    i2 = jnp.zeros((1, t), I32)
    started = jnp.zeros((1, t), jnp.bool_)
    for j in range(MOE_EPG):
        r = sel[j:j + 1]
        ok = i1 != j
        up = ok & ((r > m2) | jnp.logical_not(started))
        i2 = jnp.where(up, j, i2)
        m2 = jnp.where(up, r, m2)
        started = started | ok
    e21 = jnp.exp(m2 - m1)
    p1 = 1.0 / (1.0 + e21)
    w1 = gate_g * p1
    w2 = gate_g * (e21 * p1)
    eid1 = gi * MOE_EPG + i1
    eid2 = gi * MOE_EPG + i2

    eio = lax.broadcasted_iota(I32, (MOE_EXPERTS, t), 0)
    oh1 = eio == eid1
    oh2 = eio == eid2
    oh = (oh1 | oh2).astype(F32)
    rio = lax.broadcasted_iota(I32, (t, t), 0)
    cio = lax.broadcasted_iota(I32, (t, t), 1)
    upper = (rio < cio).astype(BF16)
    cum = jnp.dot(oh.astype(BF16), upper, preferred_element_type=F32) + carry_ref[:, 0:1]
    rank1 = jnp.sum(jnp.where(oh1, cum, 0.0), axis=0, keepdims=True)
    rank2 = jnp.sum(jnp.where(oh2, cum, 0.0), axis=0, keepdims=True)
    carry_ref[...] = carry_ref[...] + jnp.sum(oh, axis=1, keepdims=True)
    cnt_ref[...] = carry_ref[...]

    zi = jnp.zeros((4, t), I32)
    ri_ref[...] = jnp.concatenate([eid1, eid2, rank1.astype(I32), rank2.astype(I32), zi], axis=0)
    zf = jnp.zeros((6, t), F32)
    rw_ref[...] = jnp.concatenate([w1, w2, zf], axis=0)


def _router(x, g, w_rg, b_rg, w_re, b_re):
    n = x.shape[0]
    t = T_ROUTE
    wr = jnp.zeros((40, D_MODEL), F32).at[0:MOE_GROUPS].set(w_rg.T).at[8:40].set(w_re.T)
    br = jnp.zeros((40,), F32).at[0:MOE_GROUPS].set(b_rg).at[8:40].set(b_re)
    br = jnp.broadcast_to(br[:, None], (40, 128))
    return pl.pallas_call(
        _router_kernel,
        grid=(n // t,),
        in_specs=[
            pl.BlockSpec((t, D_MODEL), lambda i: (i, 0)),
            pl.BlockSpec((1, D_MODEL), lambda i: (0, 0)),
            pl.BlockSpec((40, D_MODEL), lambda i: (0, 0)),
            pl.BlockSpec((40, 128), lambda i: (0, 0)),
        ],
        out_specs=[
            pl.BlockSpec((t, HALF), lambda i: (i, 0)),
            pl.BlockSpec((8, t), lambda i: (0, i)),
            pl.BlockSpec((8, t), lambda i: (0, i)),
            pl.BlockSpec((MOE_EXPERTS, 128), lambda i: (0, 0)),
        ],
        out_shape=[
            jax.ShapeDtypeStruct((n, HALF), U32),
            jax.ShapeDtypeStruct((8, n), I32),
            jax.ShapeDtypeStruct((8, n), F32),
            jax.ShapeDtypeStruct((MOE_EXPERTS, 128), F32),
        ],
        scratch_shapes=[pltpu.VMEM((MOE_EXPERTS, 128), F32)],
        compiler_params=_cparams(("arbitrary",)),
        name="moe_router",
    )(x, g, wr, br)


def _sc_mesh():
    return plsc.VectorSubcoreMesh(core_axis_name="c", subcore_axis_name="s",
                                  num_cores=SC_CORES, num_subcores=SC_SUBCORES)


def _sc_worker_id():
    return lax.axis_index("s") * SC_CORES + lax.axis_index("c")


def _sc_dispatch(hp, idx, zero_rows, total_rows):
    n = hp.shape[0]
    tpw = n // SC_WORKERS
    kd = tpw // SC_ROWS
    kp = idx.shape[1] - 2 * kd

    @functools.partial(
        pl.kernel, mesh=_sc_mesh(),
        out_type=jax.ShapeDtypeStruct((total_rows, HALF), U32),
        scratch_types=[
            pltpu.VMEM((2 * kd + kp, SC_ROWS), I32),
            pltpu.VMEM((SC_ROWS, HALF), U32), pltpu.VMEM((SC_ROWS, HALF), U32), pltpu.VMEM((SC_ROWS, HALF), U32),
            pltpu.SemaphoreType.DMA((2,)), pltpu.SemaphoreType.DMA((2,)), pltpu.SemaphoreType.DMA,
        ],
        name="moe_dispatch_sc",
    )
    def k(hp_hbm, idx_hbm, zero_hbm, xs_hbm, idx_v, buf0, buf1, zbuf, load_sem, scat_sem, pad_sem):
        wid = _sc_worker_id()
        bufs = (buf0, buf1)
        pltpu.sync_copy(idx_hbm.at[wid], idx_v)

        def load(c):
            return pltpu.make_async_copy(hp_hbm.at[pl.ds(wid * tpw + c * SC_ROWS, SC_ROWS)], bufs[c % 2],
                                         load_sem.at[c % 2])

        def scatters(c):
            return (pltpu.make_async_copy(bufs[c % 2], xs_hbm.at[idx_v.at[c]], scat_sem.at[c % 2]),
                    pltpu.make_async_copy(bufs[c % 2], xs_hbm.at[idx_v.at[kd + c]], scat_sem.at[c % 2]))

        load(0).start()
        pltpu.sync_copy(zero_hbm, zbuf)
        pads = [pltpu.make_async_copy(zbuf, xs_hbm.at[idx_v.at[2 * kd + j]], pad_sem) for j in range(kp)]
        for p in pads:
            p.start()
        for c in range(kd):
            load(c).wait()
            for d in scatters(c):
                d.start()
            if c + 1 < kd:
                if c >= 1:
                    for d in scatters(c - 1):
                        d.wait()
                load(c + 1).start()
        for c in range(max(kd - 2, 0), kd):
            for d in scatters(c):
                d.wait()
        for p in pads:
            p.wait()

    return k(hp, idx, zero_rows)


def _sc_gather(ys, idx):
    kg = idx.shape[1]
    rows_per_worker = kg * SC_ROWS

    @functools.partial(
        pl.kernel, mesh=_sc_mesh(),
        out_type=jax.ShapeDtypeStruct((SC_WORKERS * rows_per_worker, HALF), U32),
        scratch_types=[
            pltpu.VMEM((kg, SC_ROWS), I32),
            pltpu.VMEM((SC_ROWS, HALF), U32), pltpu.VMEM((SC_ROWS, HALF), U32),
            pltpu.SemaphoreType.DMA((2,)), pltpu.SemaphoreType.DMA((2,)),
        ],
        name="moe_gather_sc",
    )
    def k(ys_hbm, idx_hbm, yg_hbm, idx_v, buf0, buf1, gat_sem, out_sem):
        wid = _sc_worker_id()
        bufs = (buf0, buf1)
        pltpu.sync_copy(idx_hbm.at[wid], idx_v)

        def gather(c):
            return pltpu.make_async_copy(ys_hbm.at[idx_v.at[c]], bufs[c % 2], gat_sem.at[c % 2])

        def store(c):
            return pltpu.make_async_copy(bufs[c % 2],
                                         yg_hbm.at[pl.ds(wid * rows_per_worker + c * SC_ROWS, SC_ROWS)],
                                         out_sem.at[c % 2])

        gather(0).start()
        for c in range(kg):
            if c + 1 < kg:
                if c >= 1:
                    store(c - 1).wait()
                gather(c + 1).start()
            gather(c).wait()
            store(c).start()
        for c in range(max(kg - 2, 0), kg):
            store(c).wait()

    return k(ys, idx)


def _expert_kernel(blk_e_ref, nused_ref, first_ref, slot_ref, nxt_ref, xs_ref, wg_hbm, wu_hbm, wd_hbm,
                   ys_ref, wg_buf, wu_buf, wd_buf, sems, *, layer):
    step = pl.program_id(0)
    tm = TM_EXP
    nused = nused_ref[0]

    def weight_copies(e, s):
        return (pltpu.make_async_copy(wg_hbm.at[layer, e], wg_buf.at[s], sems.at[s, 0]),
                pltpu.make_async_copy(wu_hbm.at[layer, e], wu_buf.at[s], sems.at[s, 1]),
                pltpu.make_async_copy(wd_hbm.at[layer, e], wd_buf.at[s], sems.at[s, 2]))

    def dma_control(j):
        i = step * EXP_SUB + j

        @pl.when(i < nused)
        def _():
            s = slot_ref[i]

            if j == 0:
                @pl.when(i == 0)
                def _():
                    for c in weight_copies(blk_e_ref[0], 0):
                        c.start()

            @pl.when(first_ref[i] == 1)
            def _():
                for c in weight_copies(blk_e_ref[i], s):
                    c.wait()

                @pl.when(nxt_ref[i] >= 0)
                def _():
                    for c in weight_copies(nxt_ref[i], lax.rem(s + 1, EXP_SLOTS)):
                        c.start()

    for j in range(EXP_SUB):
        dma_control(j)

    @pl.when(step * EXP_SUB < nused)
    def _():
        for j in range(EXP_SUB):
            i = step * EXP_SUB + j
            rows = slice(j * tm, (j + 1) * tm)
            s = slot_ref[i]
            lo, hi = _unpack_bf16_pairs(xs_ref[rows, :])
            xf = jnp.concatenate([lo, hi], axis=1)
            a = jnp.dot(xf, wg_buf[s], preferred_element_type=F32)
            b = jnp.dot(xf, wu_buf[s], preferred_element_type=F32)
            hm = _silu(a) * b
            y = jnp.dot(hm, wd_buf[s], preferred_element_type=F32)
            ys_ref[rows, :] = jnp.where(i < nused, _pack_bf16_pairs(y), jnp.uint32(0))

    @pl.when(step * EXP_SUB >= nused)
    def _():
        ys_ref[...] = jnp.zeros_like(ys_ref)


def _experts(xs, blk_e, nused, w_gate, w_up, w_down, layer):
    tm = TM_EXP
    p_rows = xs.shape[0] - MOE_EXPERTS * tm
    nblk = p_rows // tm
    pos = jnp.arange(nblk, dtype=I32)
    valid = pos < nused[0]
    prev_e = jnp.concatenate([jnp.full((1,), -1, I32), blk_e[:-1]])
    first = valid & (blk_e != prev_e)
    slot = jnp.maximum(jnp.cumsum(first.astype(I32)) - 1, 0) % EXP_SLOTS
    first_pos = jnp.where(first, pos, nblk)
    next_first = jnp.concatenate([lax.cummin(first_pos, reverse=True)[1:], jnp.full((1,), nblk, I32)])
    nxt = jnp.where(next_first < nblk, blk_e[jnp.minimum(next_first, nblk - 1)], -1)

    def blk(i, be, nu, *_):
        return jnp.minimum(i, (nu[0] - 1) // EXP_SUB)

    grid_spec = pltpu.PrefetchScalarGridSpec(
        num_scalar_prefetch=5,
        grid=(nblk // EXP_SUB,),
        in_specs=[
            pl.BlockSpec((EXP_SUB * tm, HALF), lambda i, *sp: (blk(i, *sp), 0)),
            pl.BlockSpec(memory_space=pl.ANY),
            pl.BlockSpec(memory_space=pl.ANY),
            pl.BlockSpec(memory_space=pl.ANY),
        ],
        out_specs=pl.BlockSpec((EXP_SUB * tm, HALF), lambda i, *sp: (i, 0)),
        scratch_shapes=[
            pltpu.VMEM((EXP_SLOTS, D_MODEL, MOE_FF), F32),
            pltpu.VMEM((EXP_SLOTS, D_MODEL, MOE_FF), F32),
            pltpu.VMEM((EXP_SLOTS, MOE_FF, D_MODEL), F32),
            pltpu.SemaphoreType.DMA((EXP_SLOTS, 3)),
        ],
    )
    return pl.pallas_call(
        functools.partial(_expert_kernel, layer=layer),
        grid_spec=grid_spec,
        out_shape=jax.ShapeDtypeStruct((p_rows, HALF), U32),
        compiler_params=_cparams(("arbitrary",)),
        name="moe_experts",
    )(blk_e, nused, first.astype(I32), slot.astype(I32), nxt.astype(I32), xs, w_gate, w_up, w_down)


def _combine_kernel(x_ref, rw_ref, fg_ref, y1_ref, y2_ref, out_ref):
    lo1, hi1 = _unpack_bf16_pairs(y1_ref[...])
    lo2, hi2 = _unpack_bf16_pairs(y2_ref[...])
    w1, w2 = _route_weight_columns(rw_ref[...])
    x = x_ref[...]
    o_lo = x[:, :HALF] + w1 * lo1 + w2 * lo2
    o_hi = x[:, HALF:] + w1 * hi1 + w2 * hi2
    ms = (jnp.sum(o_lo * o_lo, axis=-1, keepdims=True)
          + jnp.sum(o_hi * o_hi, axis=-1, keepdims=True)) * (1.0 / D_MODEL)
    sc = lax.rsqrt(ms + NORM_EPS)
    o_lo = o_lo * sc * fg_ref[:, :HALF]
    o_hi = o_hi * sc * fg_ref[:, HALF:]
    out_ref[:, :HALF] = o_lo
    out_ref[:, HALF:] = o_hi


def _combine(yg, x, rw, final_g):
    n = x.shape[0]
    td = T_COMB
    nb = n // td
    return pl.pallas_call(
        _combine_kernel,
        grid=(nb,),
        in_specs=[
            pl.BlockSpec((td, D_MODEL), lambda i: (i, 0)),
            pl.BlockSpec((8, td), lambda i: (0, i)),
            pl.BlockSpec((1, D_MODEL), lambda i: (0, 0)),
            pl.BlockSpec((td, HALF), lambda i: (i, 0)),
            pl.BlockSpec((td, HALF), lambda i: (i + nb, 0)),
        ],
        out_specs=pl.BlockSpec((td, D_MODEL), lambda i: (i, 0)),
        out_shape=jax.ShapeDtypeStruct((n, D_MODEL), F32),
        compiler_params=_cparams(("arbitrary",)),
        name="moe_combine",
    )(x, rw, final_g, yg, yg)


def _moe(x, g, w_rg, b_rg, w_re, b_re, w_gate, w_up, w_down, layer):
    n = x.shape[0]
    tm = TM_EXP
    p_rows = 2 * n + MOE_EXPERTS * tm
    nblk = p_rows // tm
    hp, ri, rw, cnt = _router(x, g, w_rg, b_rg, w_re, b_re)
    counts = cnt[:, 0].astype(I32)
    pcounts = (counts + tm - 1) // tm * tm
    pend = jnp.cumsum(pcounts)
    pstart = pend - pcounts
    eio = jnp.arange(MOE_EXPERTS, dtype=I32)[:, None]
    dest1 = jnp.sum(jnp.where(ri[0][None, :] == eio, pstart[:, None], 0), axis=0) + ri[2]
    dest2 = jnp.sum(jnp.where(ri[1][None, :] == eio, pstart[:, None], 0), axis=0) + ri[3]
    blk_start = jnp.arange(nblk, dtype=I32) * tm
    blk_e = jnp.minimum(jnp.sum((pend[None, :] <= blk_start[:, None]).astype(I32), axis=1), MOE_EXPERTS - 1)
    nused = jnp.maximum(pend[-1] // tm, 1).astype(I32).reshape(1)
    r = jnp.arange(tm, dtype=I32)[None, :]
    pad_slot = jnp.where(r < (pcounts - counts)[:, None], (pstart + counts)[:, None] + r, p_rows + eio * tm + r)
    kd = n // SC_WORKERS // SC_ROWS
    idx = jnp.concatenate([dest1.reshape(SC_WORKERS, kd, SC_ROWS), dest2.reshape(SC_WORKERS, kd, SC_ROWS),
                           pad_slot.reshape(SC_WORKERS, -1, SC_ROWS)], axis=1)
    zero_rows = jnp.zeros((SC_ROWS, HALF), U32)
    xs = _sc_dispatch(hp, idx, zero_rows, p_rows + MOE_EXPERTS * tm)
    ys = _experts(xs, blk_e, nused, w_gate, w_up, w_down, layer)
    gidx = jnp.concatenate([dest1, dest2]).reshape(SC_WORKERS, -1, SC_ROWS)
    yg = _sc_gather(ys, gidx)
    return yg, rw


def kernel(x, positions, norm_mix_g, norm_ffn_g, ret_w_in, ret_head_g, ret_w_out, conv_w_pw1, conv_b_pw1, conv_w_dw, conv_b_dw, conv_ln_g, conv_ln_b, conv_w_pw2, conv_b_pw2, moe_w_rg, moe_b_rg, moe_w_re, moe_b_re, moe_w_gate, moe_w_up, moe_w_down, final_norm_g):
    b, s, d = x.shape
    n = b * s
    xt = x.reshape(n, d)
    pos = positions.reshape(n, 1)
    fg = final_norm_g.reshape(1, d)

    q, k, v, gate = _ret_inproj(xt, pos, norm_mix_g[0].reshape(1, d), ret_w_in[0])
    xt = _ret_core(q, k, v, gate, xt, ret_head_g[0].reshape(RET_V, 1), ret_w_out[0])
    yg, rw = _moe(xt, norm_ffn_g[0].reshape(1, d), moe_w_rg[0], moe_b_rg[0], moe_w_re[0], moe_b_re[0],
                  moe_w_gate, moe_w_up, moe_w_down, 0)

    xt, u = _conv_pw1(xt, rw, yg, norm_mix_g[1].reshape(1, d), conv_w_pw1[0],
                      conv_b_pw1[0].reshape(1, 2 * d))
    xt = _conv_core(u, xt, conv_w_dw[0], conv_b_dw[0].reshape(1, d), conv_ln_g[0].reshape(1, d),
                    conv_ln_b[0].reshape(1, d), conv_w_pw2[0], conv_b_pw2[0].reshape(1, d))
    yg, rw = _moe(xt, norm_ffn_g[1].reshape(1, d), moe_w_rg[1], moe_b_rg[1], moe_w_re[1], moe_b_re[1],
                  moe_w_gate, moe_w_up, moe_w_down, 1)
    xt = _combine(yg, xt, rw, fg)
    return xt.reshape(b, s, d)
```

```python
import functools
import math

import jax
import jax.numpy as jnp
import numpy as np
from jax import lax
from jax.experimental import pallas as pl
from jax.experimental.pallas import tpu as pltpu
from jax.experimental.pallas import tpu_sc as plsc

F32 = jnp.float32
BF16 = jnp.bfloat16
U32 = jnp.uint32
I32 = jnp.int32

D_MODEL = 1024
RET_HEADS = 4
RET_DK = 256
RET_DV = 512
RET_QK = RET_HEADS * RET_DK
RET_V = RET_HEADS * RET_DV
ROPE_BASE = 10000.0
CONV_WIDTH = 31
MOE_GROUPS = 4
MOE_EPG = 8
MOE_EXPERTS = MOE_GROUPS * MOE_EPG
MOE_FF = 512
NORM_EPS = 1e-6

TM_PROJ = 512
RET_C = 256
RET_STEP = 512
TM_CONV = 512
CONV_HALO = 32
CONV_ROWS = 128
CONV_STRIDE = 4
T_ROUTE = 512
MOE_PARTS = 2
TM_EXP = 256
EXP_SUB = 4
EXP_SLOTS = EXP_SUB + 1
T_COMB = 512
SC_CORES = 2
SC_SUBCORES = 16
SC_WORKERS = SC_CORES * SC_SUBCORES
SC_ROWS = 64
HALF = D_MODEL // 2

VMEM_LIMIT = 56 * 1024 * 1024


def _cparams(sem, flags=None):
    return pltpu.CompilerParams(dimension_semantics=sem, vmem_limit_bytes=VMEM_LIMIT, flags=flags)


def _rms(x, g):
    ms = jnp.mean(x * x, axis=-1, keepdims=True)
    return x * lax.rsqrt(ms + NORM_EPS) * g


def _silu(x):
    return x * (1.0 / (1.0 + jnp.exp(-x)))


def _pack_bf16_pairs(y):
    lo = pltpu.bitcast(y[:, :HALF].astype(BF16).astype(F32), U32)
    hi = pltpu.bitcast(y[:, HALF:].astype(BF16).astype(F32), U32)
    return (hi & jnp.uint32(0xFFFF0000)) | (lo >> 16)


def _route_weight_columns(rw):
    t = jnp.concatenate([rw] * 16, axis=0).T
    return t[:, 0:1], t[:, 1:2]


def _unpack_bf16_pairs(p):
    lo = pltpu.bitcast(p << 16, F32)
    hi = pltpu.bitcast(p & jnp.uint32(0xFFFF0000), F32)
    return lo, hi


def _ret_inproj_kernel(x_ref, pos_ref, g_ref, inv_ref, w_ref, q_ref, k_ref, v_ref, gate_ref):
    half = RET_DK // 2
    kscale = RET_DK ** -0.5
    h = _rms(x_ref[...], g_ref[...]).astype(BF16)

    def proj(c0, width):
        return jnp.dot(h, w_ref[:, c0:c0 + width].astype(BF16), preferred_element_type=F32)

    v0 = proj(2 * RET_QK, 512)
    v_ref[:, 0:512] = v0.astype(BF16)
    zero = ((pltpu.bitcast(v0[:, 0:half], U32) >> 16) >> 16).astype(F32)
    ang = pos_ref[...].astype(F32) * inv_ref[...] + zero
    cos = jnp.cos(ang)
    sin = jnp.sin(ang)
    for j in range(RET_V // 512):
        if j > 0:
            v_ref[:, j * 512:(j + 1) * 512] = proj(2 * RET_QK + j * 512, 512).astype(BF16)
        gate_ref[:, j * 512:(j + 1) * 512] = proj(2 * RET_QK + RET_V + j * 512, 512).astype(BF16)

    for hd in range(RET_HEADS):
        for base, out_ref, cs, sn in ((0, q_ref, cos, sin), (RET_QK, k_ref, cos * kscale, sin * kscale)):
            t = proj(base + hd * RET_DK, RET_DK)
            t1 = t[:, :half]
            t2 = t[:, half:]
            out_ref[:, hd * RET_DK:hd * RET_DK + half] = (t1 * cs - t2 * sn).astype(BF16)
            out_ref[:, hd * RET_DK + half:(hd + 1) * RET_DK] = (t1 * sn + t2 * cs).astype(BF16)


def _ret_inproj(x, pos, g, w_in):
    n = x.shape[0]
    half = RET_DK // 2
    inv = (ROPE_BASE ** (-jnp.arange(half, dtype=F32) / half)).reshape(1, half)
    tm = TM_PROJ
    return pl.pallas_call(
        _ret_inproj_kernel,
        grid=(n // tm,),
        in_specs=[
            pl.BlockSpec((tm, D_MODEL), lambda i: (i, 0)),
            pl.BlockSpec((tm, 1), lambda i: (i, 0)),
            pl.BlockSpec((1, D_MODEL), lambda i: (0, 0)),
            pl.BlockSpec((1, half), lambda i: (0, 0)),
            pl.BlockSpec(w_in.shape, lambda i: (0, 0), pipeline_mode=pl.Buffered(1)),
        ],
        out_specs=[
            pl.BlockSpec((tm, RET_QK), lambda i: (i, 0)),
            pl.BlockSpec((tm, RET_QK), lambda i: (i, 0)),
            pl.BlockSpec((tm, RET_V), lambda i: (i, 0)),
            pl.BlockSpec((tm, RET_V), lambda i: (i, 0)),
        ],
        out_shape=[
            jax.ShapeDtypeStruct((n, RET_QK), BF16),
            jax.ShapeDtypeStruct((n, RET_QK), BF16),
            jax.ShapeDtypeStruct((n, RET_V), BF16),
            jax.ShapeDtypeStruct((n, RET_V), BF16),
        ],
        compiler_params=_cparams(("arbitrary",)),
        name="ret_inproj",
    )(x, pos, g, inv, w_in)


def _ret_core_kernel(cdec_ref, q_ref, k_ref, v_ref, gate_ref, x_ref, hg_ref, intra_ref, cross_ref,
                     kdec_ref, wo_ref, out_ref, state_ref, y_ref, wos_ref):
    @pl.when(pl.program_id(0) == 0)
    def _():
        state_ref[...] = jnp.zeros_like(state_ref)
        wos_ref[...] = (wo_ref[...] * hg_ref[...]).astype(BF16)

    for r0 in range(0, RET_STEP, RET_C):
        rs = slice(r0, r0 + RET_C)
        for hd in range(RET_HEADS):
            q = q_ref[rs, hd * RET_DK:(hd + 1) * RET_DK]
            k = k_ref[rs, hd * RET_DK:(hd + 1) * RET_DK]
            v = v_ref[rs, hd * RET_DV:(hd + 1) * RET_DV]
            state = state_ref[hd]
            scores = lax.dot_general(q, k, (((1,), (1,)), ((), ())), preferred_element_type=F32)
            scores = (scores * intra_ref[hd]).astype(BF16)
            o = jnp.dot(scores, v, preferred_element_type=F32)
            cross = cross_ref[hd]
            o_cross = jnp.dot(q, state.astype(BF16), preferred_element_type=F32)
            o = o + o_cross * jnp.concatenate([cross] * (RET_DV // 128), axis=1)
            kdec = kdec_ref[hd]
            kd = (k.astype(F32) * jnp.concatenate([kdec] * (RET_DK // 128), axis=1)).astype(BF16)
            upd = lax.dot_general(kd, v, (((0,), (0,)), ((), ())), preferred_element_type=F32)
            state_ref[hd] = state * cdec_ref[hd] + upd
            ms = jnp.mean(o * o, axis=-1, keepdims=True)
            on = o * lax.rsqrt(ms + NORM_EPS)
            gt = gate_ref[rs, hd * RET_DV:(hd + 1) * RET_DV].astype(F32)
            y_ref[rs, hd * RET_DV:(hd + 1) * RET_DV] = (_silu(gt) * on).astype(BF16)
        out_ref[rs, :] = x_ref[rs, :] + jnp.dot(y_ref[rs, :], wos_ref[...], preferred_element_type=F32)


def _ret_core(q, k, v, gate, x, head_g_col, w_out):
    n = x.shape[0]
    c = RET_C
    log_gamma = jnp.log1p(-(2.0 ** (-5.0 - jnp.arange(RET_HEADS, dtype=F32))))
    idx = jnp.arange(c, dtype=F32)
    diff = idx[:, None] - idx[None, :]
    intra = jnp.where(diff >= 0, jnp.exp(log_gamma[:, None, None] * jnp.maximum(diff, 0.0)), 0.0)
    cross = jnp.broadcast_to(jnp.exp(log_gamma[:, None] * (idx + 1.0))[:, :, None], (RET_HEADS, c, 128))
    kdec = jnp.broadcast_to(jnp.exp(log_gamma[:, None] * (c - 1.0 - idx))[:, :, None], (RET_HEADS, c, 128))
    cdec = jnp.exp(log_gamma * c)
    return pl.pallas_call(
        _ret_core_kernel,
        grid=(n // RET_STEP,),
        in_specs=[
            pl.BlockSpec(memory_space=pltpu.SMEM),
            pl.BlockSpec((RET_STEP, RET_QK), lambda i: (i, 0)),
            pl.BlockSpec((RET_STEP, RET_QK), lambda i: (i, 0)),
            pl.BlockSpec((RET_STEP, RET_V), lambda i: (i, 0)),
            pl.BlockSpec((RET_STEP, RET_V), lambda i: (i, 0)),
            pl.BlockSpec((RET_STEP, D_MODEL), lambda i: (i, 0)),
            pl.BlockSpec((RET_V, 1), lambda i: (0, 0)),
            pl.BlockSpec((RET_HEADS, c, c), lambda i: (0, 0, 0)),
            pl.BlockSpec((RET_HEADS, c, 128), lambda i: (0, 0, 0)),
            pl.BlockSpec((RET_HEADS, c, 128), lambda i: (0, 0, 0)),
            pl.BlockSpec((RET_V, D_MODEL), lambda i: (0, 0)),
        ],
        out_specs=pl.BlockSpec((RET_STEP, D_MODEL), lambda i: (i, 0)),
        out_shape=jax.ShapeDtypeStruct((n, D_MODEL), F32),
        scratch_shapes=[
            pltpu.VMEM((RET_HEADS, RET_DK, RET_DV), F32),
            pltpu.VMEM((RET_STEP, RET_V), BF16),
            pltpu.VMEM((RET_V, D_MODEL), BF16),
        ],
        compiler_params=_cparams(("arbitrary",)),
        name="ret_core",
    )(cdec, q, k, v, gate, x, head_g_col, intra, cross, kdec, w_out)


def _moe_part_specs(moe_parts, tm, nb):
    nbp = nb // MOE_PARTS
    specs, args = [], []
    for p, (yg, rw) in enumerate(moe_parts):
        def local(i, p=p):
            return jnp.clip(i - p * nbp, 0, nbp - 1)
        specs += [pl.BlockSpec((8, tm), lambda i, f=local: (0, f(i))),
                  pl.BlockSpec((tm, HALF), lambda i, f=local: (f(i), 0)),
                  pl.BlockSpec((tm, HALF), lambda i, f=local: (f(i) + nbp, 0))]
        args += [rw, yg, yg]
    return specs, args


def _select_moe_part(refs, steps_per_part):
    part = pl.program_id(0) // steps_per_part
    vals = [r[...] for r in refs[0:3]]
    for p in range(1, len(refs) // 3):
        vals = [jnp.where(part == p, r[...], v) for r, v in zip(refs[3 * p:3 * p + 3], vals)]
    return vals


def _conv_pw1_kernel(x_ref, *refs):
    g_ref, w_ref, b_ref, xo_ref, u_ref = refs[3 * MOE_PARTS:]
    rw, y1, y2 = _select_moe_part(refs[:3 * MOE_PARTS], pl.num_programs(0) // MOE_PARTS)
    lo1, hi1 = _unpack_bf16_pairs(y1)
    lo2, hi2 = _unpack_bf16_pairs(y2)
    w1, w2 = _route_weight_columns(rw)
    x = jnp.concatenate([x_ref[:, :HALF] + w1 * lo1 + w2 * lo2, x_ref[:, HALF:] + w1 * hi1 + w2 * hi2], axis=1)
    xo_ref[...] = x
    h = _rms(x, g_ref[...]).astype(BF16)
    for j in range(D_MODEL // 512):
        a = jnp.dot(h, w_ref[:, j * 512:(j + 1) * 512].astype(BF16), preferred_element_type=F32)
        a = a + b_ref[:, j * 512:(j + 1) * 512]
        gt = jnp.dot(h, w_ref[:, D_MODEL + j * 512:D_MODEL + (j + 1) * 512].astype(BF16),
                     preferred_element_type=F32)
        gt = gt + b_ref[:, D_MODEL + j * 512:D_MODEL + (j + 1) * 512]
        u_ref[:, j * 512:(j + 1) * 512] = a * (1.0 / (1.0 + jnp.exp(-gt)))


def _conv_pw1(x, moe_parts, g, w, b):
    n = x.shape[0]
    tm = TM_CONV
    nb = n // tm
    part_specs, part_args = _moe_part_specs(moe_parts, tm, nb)
    return pl.pallas_call(
        _conv_pw1_kernel,
        grid=(nb,),
        in_specs=[
            pl.BlockSpec((tm, D_MODEL), lambda i: (i, 0)),
            *part_specs,
            pl.BlockSpec((1, D_MODEL), lambda i: (0, 0)),
            pl.BlockSpec((D_MODEL, 2 * D_MODEL), lambda i: (0, 0)),
            pl.BlockSpec((1, 2 * D_MODEL), lambda i: (0, 0)),
        ],
        out_specs=[pl.BlockSpec((tm, D_MODEL), lambda i: (i, 0)), pl.BlockSpec((tm, D_MODEL), lambda i: (i, 0))],
        out_shape=[jax.ShapeDtypeStruct((n, D_MODEL), F32), jax.ShapeDtypeStruct((n, D_MODEL), F32)],
        compiler_params=_cparams(("arbitrary",)),
        name="conv_pw1",
    )(x, *part_args, g, w, b)


def _conv_core_kernel(u_ref, halo_ref, x_ref, wdw_ref, bdw_ref, lng_ref, lnb_ref, w2_ref, b2_ref,
                      out_ref, win_ref, z_ref):
    tm = TM_CONV
    first = pl.program_id(0) == 0
    halo = halo_ref[...]
    halo = jnp.where(first, jnp.zeros_like(halo), halo)
    nslab = D_MODEL // 128
    for cc in range(nslab):
        cs = slice(cc * 128, (cc + 1) * 128)
        win_ref[cc, 0:CONV_HALO, :] = halo[:, cs]
        win_ref[cc, CONV_HALO:CONV_HALO + tm, :] = u_ref[:, cs]
    off = CONV_HALO - (CONV_WIDTH - 1)
    rb = CONV_ROWS
    st = CONV_STRIDE
    for cc in range(nslab):
        cs = slice(cc * 128, (cc + 1) * 128)
        for r0 in range(0, tm, rb):
            accs = [bdw_ref[:, cs]] * st
            for o in range(CONV_WIDTH):
                w_o = wdw_ref[o:o + 1, cs]
                for rho in range(st):
                    accs[rho] = accs[rho] + win_ref[cc, pl.ds(r0 + rho + off + o, rb // st, stride=st), :] * w_o
            for rho in range(st):
                z_ref[cc, pl.ds(r0 + rho, rb // st, stride=st), :] = accs[rho]
    z = jnp.concatenate([z_ref[cc] for cc in range(nslab)], axis=1)
    mu = jnp.mean(z, axis=-1, keepdims=True)
    zc = z - mu
    var = jnp.mean(zc * zc, axis=-1, keepdims=True)
    zn = zc * lax.rsqrt(var + NORM_EPS) * lng_ref[...] + lnb_ref[...]
    y = _silu(zn).astype(BF16)
    out_ref[...] = x_ref[...] + jnp.dot(y, w2_ref[...].astype(BF16), preferred_element_type=F32) + b2_ref[...]


def _conv_core(u, x, w_dw, b_dw, ln_g, ln_b, w2, b2):
    n = x.shape[0]
    tm = TM_CONV
    r = tm // CONV_HALO
    wdw_pad = jnp.zeros((32, D_MODEL), F32).at[:CONV_WIDTH].set(w_dw)
    return pl.pallas_call(
        _conv_core_kernel,
        grid=(n // tm,),
        in_specs=[
            pl.BlockSpec((tm, D_MODEL), lambda i: (i, 0)),
            pl.BlockSpec((CONV_HALO, D_MODEL), lambda i: (jnp.maximum(i * r - 1, 0), 0)),
            pl.BlockSpec((tm, D_MODEL), lambda i: (i, 0)),
            pl.BlockSpec((32, D_MODEL), lambda i: (0, 0)),
            pl.BlockSpec((1, D_MODEL), lambda i: (0, 0)),
            pl.BlockSpec((1, D_MODEL), lambda i: (0, 0)),
            pl.BlockSpec((1, D_MODEL), lambda i: (0, 0)),
            pl.BlockSpec((D_MODEL, D_MODEL), lambda i: (0, 0)),
            pl.BlockSpec((1, D_MODEL), lambda i: (0, 0)),
        ],
        out_specs=pl.BlockSpec((tm, D_MODEL), lambda i: (i, 0)),
        out_shape=jax.ShapeDtypeStruct((n, D_MODEL), F32),
        scratch_shapes=[pltpu.VMEM((D_MODEL // 128, CONV_HALO + tm, 128), F32),
                        pltpu.VMEM((D_MODEL // 128, tm, 128), F32)],
        compiler_params=_cparams(("arbitrary",)),
        name="conv_core",
    )(u, u, x, wdw_pad, b_dw, ln_g, ln_b, w2, b2)


def _router_kernel(x_ref, g_ref, wr_ref, br_ref, hp_ref, ri_ref, rw_ref, cnt_ref, carry_ref):
    t = T_ROUTE

    @pl.when(pl.program_id(0) == 0)
    def _():
        carry_ref[...] = jnp.zeros_like(carry_ref)

    h = _rms(x_ref[...], g_ref[...])
    hp_ref[...] = _pack_bf16_pairs(h)
    h_hi = h.astype(BF16)
    h_lo = (h - h_hi.astype(F32)).astype(BF16)
    w = wr_ref[...]
    w_hi = w.astype(BF16)
    w_lo = (w - w_hi.astype(F32)).astype(BF16)
    dn = (((1,), (1,)), ((), ()))
    p = lax.dot_general(jnp.concatenate([w_hi, w_lo], axis=0), h_hi, dn, preferred_element_type=F32)
    nr = wr_ref.shape[0]
    logits = p[0:nr] + p[nr:2 * nr] + lax.dot_general(w_hi, h_lo, dn, preferred_element_type=F32)
    logits = logits + br_ref[:, 0:1]

    best = logits[0:1]
    gi = jnp.zeros((1, t), I32)
    for j in range(1, MOE_GROUPS):
        r = logits[j:j + 1]
        up = r > best
        gi = jnp.where(up, j, gi)
        best = jnp.where(up, r, best)
    den = jnp.zeros((1, t), F32)
    for j in range(MOE_GROUPS):
        den = den + jnp.exp(logits[j:j + 1] - best)
    gate_g = 1.0 / den

    sel = logits[8:8 + MOE_EPG]
    for j in range(1, MOE_GROUPS):
        sel = jnp.where(gi == j, logits[8 + j * MOE_EPG:8 + (j + 1) * MOE_EPG], sel)

    m1 = sel[0:1]
    i1 = jnp.zeros((1, t), I32)
    for j in range(1, MOE_EPG):
        r = sel[j:j + 1]
        up = r > m1
        i1 = jnp.where(up, j, i1)
        m1 = jnp.where(up, r, m1)
    m2 = jnp.full((1, t), -jnp.inf, F32)
    i2 = jnp.zeros((1, t), I32)
    started = jnp.zeros((1, t), jnp.bool_)
    for j in range(MOE_EPG):
        r = sel[j:j + 1]
        ok = i1 != j
        up = ok & ((r > m2) | jnp.logical_not(started))
        i2 = jnp.where(up, j, i2)
        m2 = jnp.where(up, r, m2)
        started = started | ok
    e21 = jnp.exp(m2 - m1)
    p1 = 1.0 / (1.0 + e21)
    w1 = gate_g * p1
    w2 = gate_g * (e21 * p1)
    eid1 = gi * MOE_EPG + i1
    eid2 = gi * MOE_EPG + i2

    eio = lax.broadcasted_iota(I32, (MOE_EXPERTS, t), 0)
    oh1 = eio == eid1
    oh2 = eio == eid2
    oh = (oh1 | oh2).astype(F32)
    rio = lax.broadcasted_iota(I32, (t, t), 0)
    cio = lax.broadcasted_iota(I32, (t, t), 1)
    upper = (rio < cio).astype(BF16)
    cum = jnp.dot(oh.astype(BF16), upper, preferred_element_type=F32) + carry_ref[:, 0:1]
    rank1 = jnp.sum(jnp.where(oh1, cum, 0.0), axis=0, keepdims=True)
    rank2 = jnp.sum(jnp.where(oh2, cum, 0.0), axis=0, keepdims=True)
    carry_ref[...] = carry_ref[...] + jnp.sum(oh, axis=1, keepdims=True)
    cnt_ref[...] = carry_ref[...]

    zi = jnp.zeros((4, t), I32)
    ri_ref[...] = jnp.concatenate([eid1, eid2, rank1.astype(I32), rank2.astype(I32), zi], axis=0)
    zf = jnp.zeros((6, t), F32)
    rw_ref[...] = jnp.concatenate([w1, w2, zf], axis=0)


def _router(x, g, w_rg, b_rg, w_re, b_re, part):
    n = x.shape[0] // MOE_PARTS
    t = T_ROUTE
    off = part * (n // t)
    wr = jnp.zeros((40, D_MODEL), F32).at[0:MOE_GROUPS].set(w_rg.T).at[8:40].set(w_re.T)
    br = jnp.zeros((40,), F32).at[0:MOE_GROUPS].set(b_rg).at[8:40].set(b_re)
    br = jnp.broadcast_to(br[:, None], (40, 128))
    return pl.pallas_call(
        _router_kernel,
        grid=(n // t,),
        in_specs=[
            pl.BlockSpec((t, D_MODEL), lambda i: (i + off, 0)),
            pl.BlockSpec((1, D_MODEL), lambda i: (0, 0)),
            pl.BlockSpec((40, D_MODEL), lambda i: (0, 0)),
            pl.BlockSpec((40, 128), lambda i: (0, 0)),
        ],
        out_specs=[
            pl.BlockSpec((t, HALF), lambda i: (i, 0)),
            pl.BlockSpec((8, t), lambda i: (0, i)),
            pl.BlockSpec((8, t), lambda i: (0, i)),
            pl.BlockSpec((MOE_EXPERTS, 128), lambda i: (0, 0)),
        ],
        out_shape=[
            jax.ShapeDtypeStruct((n, HALF), U32),
            jax.ShapeDtypeStruct((8, n), I32),
            jax.ShapeDtypeStruct((8, n), F32),
            jax.ShapeDtypeStruct((MOE_EXPERTS, 128), F32),
        ],
        scratch_shapes=[pltpu.VMEM((MOE_EXPERTS, 128), F32)],
        compiler_params=_cparams(("arbitrary",)),
        name="moe_router",
    )(x, g, wr, br)


def _sc_mesh():
    return plsc.VectorSubcoreMesh(core_axis_name="c", subcore_axis_name="s",
                                  num_cores=SC_CORES, num_subcores=SC_SUBCORES)


def _sc_worker_id():
    return lax.axis_index("s") * SC_CORES + lax.axis_index("c")


def _sc_dispatch(hp, idx, zero_rows, total_rows):
    n = hp.shape[0]
    tpw = n // SC_WORKERS
    kd = tpw // SC_ROWS
    kp = idx.shape[1] - 2 * kd

    @functools.partial(
        pl.kernel, mesh=_sc_mesh(),
        out_type=jax.ShapeDtypeStruct((total_rows, HALF), U32),
        scratch_types=[
            pltpu.VMEM((2 * kd + kp, SC_ROWS), I32),
            pltpu.VMEM((SC_ROWS, HALF), U32), pltpu.VMEM((SC_ROWS, HALF), U32), pltpu.VMEM((SC_ROWS, HALF), U32),
            pltpu.SemaphoreType.DMA((2,)), pltpu.SemaphoreType.DMA((2,)), pltpu.SemaphoreType.DMA,
        ],
        name="moe_dispatch_sc",
    )
    def k(hp_hbm, idx_hbm, zero_hbm, xs_hbm, idx_v, buf0, buf1, zbuf, load_sem, scat_sem, pad_sem):
        wid = _sc_worker_id()
        bufs = (buf0, buf1)
        pltpu.sync_copy(idx_hbm.at[wid], idx_v)

        def load(c):
            return pltpu.make_async_copy(hp_hbm.at[pl.ds(wid * tpw + c * SC_ROWS, SC_ROWS)], bufs[c % 2],
                                         load_sem.at[c % 2])

        def scatters(c):
            return (pltpu.make_async_copy(bufs[c % 2], xs_hbm.at[idx_v.at[c]], scat_sem.at[c % 2]),
                    pltpu.make_async_copy(bufs[c % 2], xs_hbm.at[idx_v.at[kd + c]], scat_sem.at[c % 2]))

        load(0).start()
        pltpu.sync_copy(zero_hbm, zbuf)
        pads = [pltpu.make_async_copy(zbuf, xs_hbm.at[idx_v.at[2 * kd + j]], pad_sem) for j in range(kp)]
        for p in pads:
            p.start()
        for c in range(kd):
            load(c).wait()
            for d in scatters(c):
                d.start()
            if c + 1 < kd:
                if c >= 1:
                    for d in scatters(c - 1):
                        d.wait()
                load(c + 1).start()
        for c in range(max(kd - 2, 0), kd):
            for d in scatters(c):
                d.wait()
        for p in pads:
            p.wait()

    return k(hp, idx, zero_rows)


def _sc_gather(ys, idx):
    kg = idx.shape[1]
    rows_per_worker = kg * SC_ROWS

    @functools.partial(
        pl.kernel, mesh=_sc_mesh(),
        out_type=jax.ShapeDtypeStruct((SC_WORKERS * rows_per_worker, HALF), U32),
        scratch_types=[
            pltpu.VMEM((kg, SC_ROWS), I32),
            pltpu.VMEM((SC_ROWS, HALF), U32), pltpu.VMEM((SC_ROWS, HALF), U32),
            pltpu.SemaphoreType.DMA((2,)), pltpu.SemaphoreType.DMA((2,)),
        ],
        name="moe_gather_sc",
    )
    def k(ys_hbm, idx_hbm, yg_hbm, idx_v, buf0, buf1, gat_sem, out_sem):
        wid = _sc_worker_id()
        bufs = (buf0, buf1)
        pltpu.sync_copy(idx_hbm.at[wid], idx_v)

        def gather(c):
            return pltpu.make_async_copy(ys_hbm.at[idx_v.at[c]], bufs[c % 2], gat_sem.at[c % 2])

        def store(c):
            return pltpu.make_async_copy(bufs[c % 2],
                                         yg_hbm.at[pl.ds(wid * rows_per_worker + c * SC_ROWS, SC_ROWS)],
                                         out_sem.at[c % 2])

        gather(0).start()
        for c in range(kg):
            if c + 1 < kg:
                if c >= 1:
                    store(c - 1).wait()
                gather(c + 1).start()
            gather(c).wait()
            store(c).start()
        for c in range(max(kg - 2, 0), kg):
            store(c).wait()

    return k(ys, idx)


def _expert_kernel(blk_e_ref, nused_ref, first_ref, slot_ref, nxt_ref, xs_ref, wg_hbm, wu_hbm, wd_hbm,
                   ys_ref, wg_buf, wu_buf, wd_buf, sems, *, layer):
    step = pl.program_id(0)
    tm = TM_EXP
    nused = nused_ref[0]

    def weight_copies(e, s):
        return (pltpu.make_async_copy(wg_hbm.at[layer, e], wg_buf.at[s], sems.at[s, 0]),
                pltpu.make_async_copy(wu_hbm.at[layer, e], wu_buf.at[s], sems.at[s, 1]),
                pltpu.make_async_copy(wd_hbm.at[layer, e], wd_buf.at[s], sems.at[s, 2]))

    def dma_control(j):
        i = step * EXP_SUB + j

        @pl.when(i < nused)
        def _():
            s = slot_ref[i]

            if j == 0:
                @pl.when(i == 0)
                def _():
                    for c in weight_copies(blk_e_ref[0], 0):
                        c.start()

            @pl.when(first_ref[i] == 1)
            def _():
                for c in weight_copies(blk_e_ref[i], s):
                    c.wait()

                @pl.when(nxt_ref[i] >= 0)
                def _():
                    for c in weight_copies(nxt_ref[i], lax.rem(s + 1, EXP_SLOTS)):
                        c.start()

    for j in range(EXP_SUB):
        dma_control(j)

    @pl.when(step * EXP_SUB < nused)
    def _():
        for j in range(EXP_SUB):
            i = step * EXP_SUB + j
            rows = slice(j * tm, (j + 1) * tm)
            s = slot_ref[i]
            lo, hi = _unpack_bf16_pairs(xs_ref[rows, :])
            xf = jnp.concatenate([lo, hi], axis=1)
            a = jnp.dot(xf, wg_buf[s], preferred_element_type=F32)
            b = jnp.dot(xf, wu_buf[s], preferred_element_type=F32)
            hm = _silu(a) * b
            y = jnp.dot(hm, wd_buf[s], preferred_element_type=F32)
            ys_ref[rows, :] = jnp.where(i < nused, _pack_bf16_pairs(y), jnp.uint32(0))

    @pl.when(step * EXP_SUB >= nused)
    def _():
        ys_ref[...] = jnp.zeros_like(ys_ref)


def _experts(xs, blk_e, nused, w_gate, w_up, w_down, layer):
    tm = TM_EXP
    p_rows = xs.shape[0] - MOE_EXPERTS * tm
    nblk = p_rows // tm
    pos = jnp.arange(nblk, dtype=I32)
    valid = pos < nused[0]
    prev_e = jnp.concatenate([jnp.full((1,), -1, I32), blk_e[:-1]])
    first = valid & (blk_e != prev_e)
    slot = jnp.maximum(jnp.cumsum(first.astype(I32)) - 1, 0) % EXP_SLOTS
    first_pos = jnp.where(first, pos, nblk)
    next_first = jnp.concatenate([lax.cummin(first_pos, reverse=True)[1:], jnp.full((1,), nblk, I32)])
    nxt = jnp.where(next_first < nblk, blk_e[jnp.minimum(next_first, nblk - 1)], -1)

    def blk(i, be, nu, *_):
        return jnp.minimum(i, (nu[0] - 1) // EXP_SUB)

    grid_spec = pltpu.PrefetchScalarGridSpec(
        num_scalar_prefetch=5,
        grid=(nblk // EXP_SUB,),
        in_specs=[
            pl.BlockSpec((EXP_SUB * tm, HALF), lambda i, *sp: (blk(i, *sp), 0)),
            pl.BlockSpec(memory_space=pl.ANY),
            pl.BlockSpec(memory_space=pl.ANY),
            pl.BlockSpec(memory_space=pl.ANY),
        ],
        out_specs=pl.BlockSpec((EXP_SUB * tm, HALF), lambda i, *sp: (i, 0)),
        scratch_shapes=[
            pltpu.VMEM((EXP_SLOTS, D_MODEL, MOE_FF), F32),
            pltpu.VMEM((EXP_SLOTS, D_MODEL, MOE_FF), F32),
            pltpu.VMEM((EXP_SLOTS, MOE_FF, D_MODEL), F32),
            pltpu.SemaphoreType.DMA((EXP_SLOTS, 3)),
        ],
    )
    return pl.pallas_call(
        functools.partial(_expert_kernel, layer=layer),
        grid_spec=grid_spec,
        out_shape=jax.ShapeDtypeStruct((p_rows, HALF), U32),
        compiler_params=_cparams(("arbitrary",)),
        name="moe_experts",
    )(blk_e, nused, first.astype(I32), slot.astype(I32), nxt.astype(I32), xs, w_gate, w_up, w_down)


def _combine_kernel(x_ref, *refs):
    fg_ref, out_ref = refs[3 * MOE_PARTS:]
    rw, y1, y2 = _select_moe_part(refs[:3 * MOE_PARTS], pl.num_programs(0) // MOE_PARTS)
    lo1, hi1 = _unpack_bf16_pairs(y1)
    lo2, hi2 = _unpack_bf16_pairs(y2)
    w1, w2 = _route_weight_columns(rw)
    x = x_ref[...]
    o_lo = x[:, :HALF] + w1 * lo1 + w2 * lo2
    o_hi = x[:, HALF:] + w1 * hi1 + w2 * hi2
    ms = (jnp.sum(o_lo * o_lo, axis=-1, keepdims=True)
          + jnp.sum(o_hi * o_hi, axis=-1, keepdims=True)) * (1.0 / D_MODEL)
    sc = lax.rsqrt(ms + NORM_EPS)
    o_lo = o_lo * sc * fg_ref[:, :HALF]
    o_hi = o_hi * sc * fg_ref[:, HALF:]
    out_ref[:, :HALF] = o_lo
    out_ref[:, HALF:] = o_hi


def _combine(moe_parts, x, final_g):
    n = x.shape[0]
    td = T_COMB
    nb = n // td
    part_specs, part_args = _moe_part_specs(moe_parts, td, nb)
    return pl.pallas_call(
        _combine_kernel,
        grid=(nb,),
        in_specs=[
            pl.BlockSpec((td, D_MODEL), lambda i: (i, 0)),
            *part_specs,
            pl.BlockSpec((1, D_MODEL), lambda i: (0, 0)),
        ],
        out_specs=pl.BlockSpec((td, D_MODEL), lambda i: (i, 0)),
        out_shape=jax.ShapeDtypeStruct((n, D_MODEL), F32),
        compiler_params=_cparams(("arbitrary",)),
        name="moe_combine",
    )(x, *part_args, final_g)


def _moe_part(x, part, g, w_rg, b_rg, w_re, b_re, w_gate, w_up, w_down, layer):
    n = x.shape[0] // MOE_PARTS
    tm = TM_EXP
    p_rows = 2 * n + MOE_EXPERTS * tm
    nblk = p_rows // tm
    hp, ri, rw, cnt = _router(x, g, w_rg, b_rg, w_re, b_re, part)
    counts = cnt[:, 0].astype(I32)
    pcounts = (counts + tm - 1) // tm * tm
    pend = jnp.cumsum(pcounts)
    pstart = pend - pcounts
    eio = jnp.arange(MOE_EXPERTS, dtype=I32)[:, None]
    dest1 = jnp.sum(jnp.where(ri[0][None, :] == eio, pstart[:, None], 0), axis=0) + ri[2]
    dest2 = jnp.sum(jnp.where(ri[1][None, :] == eio, pstart[:, None], 0), axis=0) + ri[3]
    blk_start = jnp.arange(nblk, dtype=I32) * tm
    blk_e = jnp.minimum(jnp.sum((pend[None, :] <= blk_start[:, None]).astype(I32), axis=1), MOE_EXPERTS - 1)
    nused = jnp.maximum(pend[-1] // tm, 1).astype(I32).reshape(1)
    r = jnp.arange(tm, dtype=I32)[None, :]
    pad_slot = jnp.where(r < (pcounts - counts)[:, None], (pstart + counts)[:, None] + r, p_rows + eio * tm + r)
    kd = n // SC_WORKERS // SC_ROWS
    idx = jnp.concatenate([dest1.reshape(SC_WORKERS, kd, SC_ROWS), dest2.reshape(SC_WORKERS, kd, SC_ROWS),
                           pad_slot.reshape(SC_WORKERS, -1, SC_ROWS)], axis=1)
    zero_rows = jnp.zeros((SC_ROWS, HALF), U32)
    xs = _sc_dispatch(hp, idx, zero_rows, p_rows + MOE_EXPERTS * tm)
    ys = _experts(xs, blk_e, nused, w_gate, w_up, w_down, layer)
    gidx = jnp.concatenate([dest1, dest2]).reshape(SC_WORKERS, -1, SC_ROWS)
    yg = _sc_gather(ys, gidx)
    return yg, rw


def _moe(x, g, w_rg, b_rg, w_re, b_re, w_gate, w_up, w_down, layer):
    return [_moe_part(x, p, g, w_rg, b_rg, w_re, b_re, w_gate, w_up, w_down, layer) for p in range(MOE_PARTS)]


def kernel(x, positions, norm_mix_g, norm_ffn_g, ret_w_in, ret_head_g, ret_w_out, conv_w_pw1, conv_b_pw1, conv_w_dw, conv_b_dw, conv_ln_g, conv_ln_b, conv_w_pw2, conv_b_pw2, moe_w_rg, moe_b_rg, moe_w_re, moe_b_re, moe_w_gate, moe_w_up, moe_w_down, final_norm_g):
    b, s, d = x.shape
    n = b * s
    xt = x.reshape(n, d)
    pos = positions.reshape(n, 1)
    fg = final_norm_g.reshape(1, d)

    q, k, v, gate = _ret_inproj(xt, pos, norm_mix_g[0].reshape(1, d), ret_w_in[0])
    xt = _ret_core(q, k, v, gate, xt, ret_head_g[0].reshape(RET_V, 1), ret_w_out[0])
    moe = _moe(xt, norm_ffn_g[0].reshape(1, d), moe_w_rg[0], moe_b_rg[0], moe_w_re[0], moe_b_re[0],
               moe_w_gate, moe_w_up, moe_w_down, 0)

    xt, u = _conv_pw1(xt, moe, norm_mix_g[1].reshape(1, d), conv_w_pw1[0], conv_b_pw1[0].reshape(1, 2 * d))
    xt = _conv_core(u, xt, conv_w_dw[0], conv_b_dw[0].reshape(1, d), conv_ln_g[0].reshape(1, d),
                    conv_ln_b[0].reshape(1, d), conv_w_pw2[0], conv_b_pw2[0].reshape(1, d))
    moe = _moe(xt, norm_ffn_g[1].reshape(1, d), moe_w_rg[1], moe_b_rg[1], moe_w_re[1], moe_b_re[1],
               moe_w_gate, moe_w_up, moe_w_down, 1)
    xt = _combine(moe, xt, fg)
    return xt.reshape(b, s, d)
```

```python
import functools

import jax
import jax.numpy as jnp
from jax import lax
from jax.experimental import pallas as pl
from jax.experimental.pallas import tpu as pltpu
from jax.experimental.pallas import tpu_sc as plsc

F32 = jnp.float32
BF16 = jnp.bfloat16
U32 = jnp.uint32
I32 = jnp.int32

D_MODEL = 1024
RET_HEADS = 4
RET_DK = 256
RET_DV = 512
RET_QK = RET_HEADS * RET_DK
RET_V = RET_HEADS * RET_DV
ROPE_BASE = 10000.0
CONV_WIDTH = 31
MOE_GROUPS = 4
MOE_EPG = 8
MOE_EXPERTS = MOE_GROUPS * MOE_EPG
MOE_FF = 512
NORM_EPS = 1e-6

TM_PROJ = 512
RET_C = 256
RET_STEP = 512
TM_CONV = 512
CONV_HALO = 32
CONV_ROWS = 128
CONV_STRIDE = 4
T_ROUTE = 512
TM_EXP = 256
EXP_SUB = 4
EXP_SLOTS = EXP_SUB + 1
T_COMB = 512
GATHER_PARTS = 2
SC_CORES = 2
SC_SUBCORES = 16
SC_WORKERS = SC_CORES * SC_SUBCORES
SC_ROWS = 64
HALF = D_MODEL // 2

VMEM_LIMIT = 56 * 1024 * 1024


def _cparams(sem, flags=None):
    return pltpu.CompilerParams(dimension_semantics=sem, vmem_limit_bytes=VMEM_LIMIT, flags=flags)


def _rms(x, g):
    ms = jnp.mean(x * x, axis=-1, keepdims=True)
    return x * lax.rsqrt(ms + NORM_EPS) * g


def _silu(x):
    return x * (1.0 / (1.0 + jnp.exp(-x)))


def _pack_bf16_pairs(y):
    lo = pltpu.bitcast(y[:, :HALF].astype(BF16).astype(F32), U32)
    hi = pltpu.bitcast(y[:, HALF:].astype(BF16).astype(F32), U32)
    return (hi & jnp.uint32(0xFFFF0000)) | (lo >> 16)


def _route_weight_columns(rw):
    t = jnp.concatenate([rw] * 16, axis=0).T
    return t[:, 0:1], t[:, 1:2]


def _unpack_bf16_pairs(p):
    lo = pltpu.bitcast(p << 16, F32)
    hi = pltpu.bitcast(p & jnp.uint32(0xFFFF0000), F32)
    return lo, hi


def _ret_inproj_kernel(x_ref, pos_ref, g_ref, inv_ref, w_ref, q_ref, k_ref, v_ref, gate_ref):
    half = RET_DK // 2
    kscale = RET_DK ** -0.5
    h = _rms(x_ref[...], g_ref[...]).astype(BF16)

    def proj(c0, width):
        return jnp.dot(h, w_ref[:, c0:c0 + width].astype(BF16), preferred_element_type=F32)

    v0 = proj(2 * RET_QK, 512)
    v_ref[:, 0:512] = v0.astype(BF16)
    zero = ((pltpu.bitcast(v0[:, 0:half], U32) >> 16) >> 16).astype(F32)
    ang = pos_ref[...].astype(F32) * inv_ref[...] + zero
    cos = jnp.cos(ang)
    sin = jnp.sin(ang)
    for j in range(RET_V // 512):
        if j > 0:
            v_ref[:, j * 512:(j + 1) * 512] = proj(2 * RET_QK + j * 512, 512).astype(BF16)
        gate_ref[:, j * 512:(j + 1) * 512] = proj(2 * RET_QK + RET_V + j * 512, 512).astype(BF16)

    for hd in range(RET_HEADS):
        for base, out_ref, cs, sn in ((0, q_ref, cos, sin), (RET_QK, k_ref, cos * kscale, sin * kscale)):
            t = proj(base + hd * RET_DK, RET_DK)
            t1 = t[:, :half]
            t2 = t[:, half:]
            out_ref[:, hd * RET_DK:hd * RET_DK + half] = (t1 * cs - t2 * sn).astype(BF16)
            out_ref[:, hd * RET_DK + half:(hd + 1) * RET_DK] = (t1 * sn + t2 * cs).astype(BF16)


def _ret_inproj(x, pos, g, w_in):
    n = x.shape[0]
    half = RET_DK // 2
    inv = (ROPE_BASE ** (-jnp.arange(half, dtype=F32) / half)).reshape(1, half)
    tm = TM_PROJ
    return pl.pallas_call(
        _ret_inproj_kernel,
        grid=(n // tm,),
        in_specs=[
            pl.BlockSpec((tm, D_MODEL), lambda i: (i, 0)),
            pl.BlockSpec((tm, 1), lambda i: (i, 0)),
            pl.BlockSpec((1, D_MODEL), lambda i: (0, 0)),
            pl.BlockSpec((1, half), lambda i: (0, 0)),
            pl.BlockSpec(w_in.shape, lambda i: (0, 0), pipeline_mode=pl.Buffered(1)),
        ],
        out_specs=[
            pl.BlockSpec((tm, RET_QK), lambda i: (i, 0)),
            pl.BlockSpec((tm, RET_QK), lambda i: (i, 0)),
            pl.BlockSpec((tm, RET_V), lambda i: (i, 0)),
            pl.BlockSpec((tm, RET_V), lambda i: (i, 0)),
        ],
        out_shape=[
            jax.ShapeDtypeStruct((n, RET_QK), BF16),
            jax.ShapeDtypeStruct((n, RET_QK), BF16),
            jax.ShapeDtypeStruct((n, RET_V), BF16),
            jax.ShapeDtypeStruct((n, RET_V), BF16),
        ],
        compiler_params=_cparams(("arbitrary",)),
        name="ret_inproj",
    )(x, pos, g, inv, w_in)


def _ret_core_kernel(cdec_ref, q_ref, k_ref, v_ref, gate_ref, x_ref, hg_ref, intra_ref, cross_ref,
                     kdec_ref, wo_ref, out_ref, state_ref, y_ref, wos_ref):
    @pl.when(pl.program_id(0) == 0)
    def _():
        state_ref[...] = jnp.zeros_like(state_ref)
        wos_ref[...] = (wo_ref[...] * hg_ref[...]).astype(BF16)

    for r0 in range(0, RET_STEP, RET_C):
        rs = slice(r0, r0 + RET_C)
        for hd in range(RET_HEADS):
            q = q_ref[rs, hd * RET_DK:(hd + 1) * RET_DK]
            k = k_ref[rs, hd * RET_DK:(hd + 1) * RET_DK]
            v = v_ref[rs, hd * RET_DV:(hd + 1) * RET_DV]
            state = state_ref[hd]
            scores = lax.dot_general(q, k, (((1,), (1,)), ((), ())), preferred_element_type=F32)
            scores = (scores * intra_ref[hd]).astype(BF16)
            o = jnp.dot(scores, v, preferred_element_type=F32)
            cross = cross_ref[hd]
            o_cross = jnp.dot(q, state.astype(BF16), preferred_element_type=F32)
            o = o + o_cross * jnp.concatenate([cross] * (RET_DV // 128), axis=1)
            kdec = kdec_ref[hd]
            kd = (k.astype(F32) * jnp.concatenate([kdec] * (RET_DK // 128), axis=1)).astype(BF16)
            upd = lax.dot_general(kd, v, (((0,), (0,)), ((), ())), preferred_element_type=F32)
            state_ref[hd] = state * cdec_ref[hd] + upd
            ms = jnp.mean(o * o, axis=-1, keepdims=True)
            on = o * lax.rsqrt(ms + NORM_EPS)
            gt = gate_ref[rs, hd * RET_DV:(hd + 1) * RET_DV].astype(F32)
            y_ref[rs, hd * RET_DV:(hd + 1) * RET_DV] = (_silu(gt) * on).astype(BF16)
        out_ref[rs, :] = x_ref[rs, :] + jnp.dot(y_ref[rs, :], wos_ref[...], preferred_element_type=F32)


def _ret_core(q, k, v, gate, x, head_g_col, w_out):
    n = x.shape[0]
    c = RET_C
    log_gamma = jnp.log1p(-(2.0 ** (-5.0 - jnp.arange(RET_HEADS, dtype=F32))))
    idx = jnp.arange(c, dtype=F32)
    diff = idx[:, None] - idx[None, :]
    intra = jnp.where(diff >= 0, jnp.exp(log_gamma[:, None, None] * jnp.maximum(diff, 0.0)), 0.0)
    cross = jnp.broadcast_to(jnp.exp(log_gamma[:, None] * (idx + 1.0))[:, :, None], (RET_HEADS, c, 128))
    kdec = jnp.broadcast_to(jnp.exp(log_gamma[:, None] * (c - 1.0 - idx))[:, :, None], (RET_HEADS, c, 128))
    cdec = jnp.exp(log_gamma * c)
    return pl.pallas_call(
        _ret_core_kernel,
        grid=(n // RET_STEP,),
        in_specs=[
            pl.BlockSpec(memory_space=pltpu.SMEM),
            pl.BlockSpec((RET_STEP, RET_QK), lambda i: (i, 0)),
            pl.BlockSpec((RET_STEP, RET_QK), lambda i: (i, 0)),
            pl.BlockSpec((RET_STEP, RET_V), lambda i: (i, 0)),
            pl.BlockSpec((RET_STEP, RET_V), lambda i: (i, 0)),
            pl.BlockSpec((RET_STEP, D_MODEL), lambda i: (i, 0)),
            pl.BlockSpec((RET_V, 1), lambda i: (0, 0)),
            pl.BlockSpec((RET_HEADS, c, c), lambda i: (0, 0, 0)),
            pl.BlockSpec((RET_HEADS, c, 128), lambda i: (0, 0, 0)),
            pl.BlockSpec((RET_HEADS, c, 128), lambda i: (0, 0, 0)),
            pl.BlockSpec((RET_V, D_MODEL), lambda i: (0, 0)),
        ],
        out_specs=pl.BlockSpec((RET_STEP, D_MODEL), lambda i: (i, 0)),
        out_shape=jax.ShapeDtypeStruct((n, D_MODEL), F32),
        scratch_shapes=[
            pltpu.VMEM((RET_HEADS, RET_DK, RET_DV), F32),
            pltpu.VMEM((RET_STEP, RET_V), BF16),
            pltpu.VMEM((RET_V, D_MODEL), BF16),
        ],
        compiler_params=_cparams(("arbitrary",)),
        name="ret_core",
    )(cdec, q, k, v, gate, x, head_g_col, intra, cross, kdec, w_out)


def _conv_pw1_kernel(x_ref, rw_ref, y1_ref, y2_ref, g_ref, w_ref, b_ref, *rest):
    xo_ref, u_ref = rest[-2:]
    lo1, hi1 = _unpack_bf16_pairs(y1_ref[...])
    lo2, hi2 = _unpack_bf16_pairs(y2_ref[...])
    w1, w2 = _route_weight_columns(rw_ref[...])
    x = jnp.concatenate([x_ref[:, :HALF] + w1 * lo1 + w2 * lo2, x_ref[:, HALF:] + w1 * hi1 + w2 * hi2], axis=1)
    xo_ref[...] = x
    h = _rms(x, g_ref[...]).astype(BF16)
    for j in range(D_MODEL // 512):
        a = jnp.dot(h, w_ref[:, j * 512:(j + 1) * 512].astype(BF16), preferred_element_type=F32)
        a = a + b_ref[:, j * 512:(j + 1) * 512]
        gt = jnp.dot(h, w_ref[:, D_MODEL + j * 512:D_MODEL + (j + 1) * 512].astype(BF16),
                     preferred_element_type=F32)
        gt = gt + b_ref[:, D_MODEL + j * 512:D_MODEL + (j + 1) * 512]
        u_ref[:, j * 512:(j + 1) * 512] = a * (1.0 / (1.0 + jnp.exp(-gt)))


def _conv_pw1(x, rw, yg_parts, g, w, b):
    n = x.shape[0]
    tm = TM_CONV
    nbp = n // tm // len(yg_parts)
    outs = None
    for p, yg in enumerate(yg_parts):
        off = p * nbp
        in_specs = [
            pl.BlockSpec((tm, D_MODEL), lambda i, off=off: (i + off, 0)),
            pl.BlockSpec((8, tm), lambda i, off=off: (0, i + off)),
            pl.BlockSpec((tm, HALF), lambda i: (i, 0)),
            pl.BlockSpec((tm, HALF), lambda i: (i + nbp, 0)),
            pl.BlockSpec((1, D_MODEL), lambda i: (0, 0)),
            pl.BlockSpec((D_MODEL, 2 * D_MODEL), lambda i: (0, 0)),
            pl.BlockSpec((1, 2 * D_MODEL), lambda i: (0, 0)),
        ]
        args = [x, rw, yg, yg, g, w, b]
        aliases = {}
        if outs is not None:
            in_specs += [pl.BlockSpec(memory_space=pl.ANY), pl.BlockSpec(memory_space=pl.ANY)]
            aliases = {len(args): 0, len(args) + 1: 1}
            args += list(outs)
        outs = pl.pallas_call(
            _conv_pw1_kernel,
            grid=(nbp,),
            in_specs=in_specs,
            out_specs=[pl.BlockSpec((tm, D_MODEL), lambda i, off=off: (i + off, 0)),
                       pl.BlockSpec((tm, D_MODEL), lambda i, off=off: (i + off, 0))],
            out_shape=[jax.ShapeDtypeStruct((n, D_MODEL), F32), jax.ShapeDtypeStruct((n, D_MODEL), F32)],
            input_output_aliases=aliases,
            compiler_params=_cparams(("arbitrary",)),
            name="conv_pw1",
        )(*args)
    return outs


def _conv_core_kernel(u_ref, halo_ref, x_ref, wdw_ref, bdw_ref, lng_ref, lnb_ref, w2_ref, b2_ref,
                      out_ref, win_ref, z_ref):
    tm = TM_CONV
    first = pl.program_id(0) == 0
    halo = halo_ref[...]
    halo = jnp.where(first, jnp.zeros_like(halo), halo)
    nslab = D_MODEL // 128
    for cc in range(nslab):
        cs = slice(cc * 128, (cc + 1) * 128)
        win_ref[cc, 0:CONV_HALO, :] = halo[:, cs]
        win_ref[cc, CONV_HALO:CONV_HALO + tm, :] = u_ref[:, cs]
    off = CONV_HALO - (CONV_WIDTH - 1)
    rb = CONV_ROWS
    st = CONV_STRIDE
    for cc in range(nslab):
        cs = slice(cc * 128, (cc + 1) * 128)
        for r0 in range(0, tm, rb):
            accs = [bdw_ref[:, cs]] * st
            for o in range(CONV_WIDTH):
                w_o = wdw_ref[o:o + 1, cs]
                for rho in range(st):
                    accs[rho] = accs[rho] + win_ref[cc, pl.ds(r0 + rho + off + o, rb // st, stride=st), :] * w_o
            for rho in range(st):
                z_ref[cc, pl.ds(r0 + rho, rb // st, stride=st), :] = accs[rho]
    z = jnp.concatenate([z_ref[cc] for cc in range(nslab)], axis=1)
    mu = jnp.mean(z, axis=-1, keepdims=True)
    zc = z - mu
    var = jnp.mean(zc * zc, axis=-1, keepdims=True)
    zn = zc * lax.rsqrt(var + NORM_EPS) * lng_ref[...] + lnb_ref[...]
    y = _silu(zn).astype(BF16)
    out_ref[...] = x_ref[...] + jnp.dot(y, w2_ref[...].astype(BF16), preferred_element_type=F32) + b2_ref[...]


def _conv_core(u, x, w_dw, b_dw, ln_g, ln_b, w2, b2):
    n = x.shape[0]
    tm = TM_CONV
    r = tm // CONV_HALO
    wdw_pad = jnp.zeros((32, D_MODEL), F32).at[:CONV_WIDTH].set(w_dw)
    return pl.pallas_call(
        _conv_core_kernel,
        grid=(n // tm,),
        in_specs=[
            pl.BlockSpec((tm, D_MODEL), lambda i: (i, 0)),
            pl.BlockSpec((CONV_HALO, D_MODEL), lambda i: (jnp.maximum(i * r - 1, 0), 0)),
            pl.BlockSpec((tm, D_MODEL), lambda i: (i, 0)),
            pl.BlockSpec((32, D_MODEL), lambda i: (0, 0)),
            pl.BlockSpec((1, D_MODEL), lambda i: (0, 0)),
            pl.BlockSpec((1, D_MODEL), lambda i: (0, 0)),
            pl.BlockSpec((1, D_MODEL), lambda i: (0, 0)),
            pl.BlockSpec((D_MODEL, D_MODEL), lambda i: (0, 0)),
            pl.BlockSpec((1, D_MODEL), lambda i: (0, 0)),
        ],
        out_specs=pl.BlockSpec((tm, D_MODEL), lambda i: (i, 0)),
        out_shape=jax.ShapeDtypeStruct((n, D_MODEL), F32),
        scratch_shapes=[pltpu.VMEM((D_MODEL // 128, CONV_HALO + tm, 128), F32),
                        pltpu.VMEM((D_MODEL // 128, tm, 128), F32)],
        compiler_params=_cparams(("arbitrary",)),
        name="conv_core",
    )(u, u, x, wdw_pad, b_dw, ln_g, ln_b, w2, b2)


def _router_kernel(x_ref, g_ref, wr_ref, br_ref, hp_ref, ri_ref, rw_ref, cnt_ref, carry_ref):
    t = T_ROUTE

    @pl.when(pl.program_id(0) == 0)
    def _():
        carry_ref[...] = jnp.zeros_like(carry_ref)

    h = _rms(x_ref[...], g_ref[...])
    hp_ref[...] = _pack_bf16_pairs(h)
    h_hi = h.astype(BF16)
    h_lo = (h - h_hi.astype(F32)).astype(BF16)
    w = wr_ref[...]
    w_hi = w.astype(BF16)
    w_lo = (w - w_hi.astype(F32)).astype(BF16)
    dn = (((1,), (1,)), ((), ()))
    p = lax.dot_general(jnp.concatenate([w_hi, w_lo], axis=0), h_hi, dn, preferred_element_type=F32)
    nr = wr_ref.shape[0]
    logits = p[0:nr] + p[nr:2 * nr] + lax.dot_general(w_hi, h_lo, dn, preferred_element_type=F32)
    logits = logits + br_ref[:, 0:1]

    best = logits[0:1]
    gi = jnp.zeros((1, t), I32)
    for j in range(1, MOE_GROUPS):
        r = logits[j:j + 1]
        up = r > best
        gi = jnp.where(up, j, gi)
        best = jnp.where(up, r, best)
    den = jnp.zeros((1, t), F32)
    for j in range(MOE_GROUPS):
        den = den + jnp.exp(logits[j:j + 1] - best)
    gate_g = 1.0 / den

    sel = logits[8:8 + MOE_EPG]
    for j in range(1, MOE_GROUPS):
        sel = jnp.where(gi == j, logits[8 + j * MOE_EPG:8 + (j + 1) * MOE_EPG], sel)

    m1 = sel[0:1]
    i1 = jnp.zeros((1, t), I32)
    for j in range(1, MOE_EPG):
        r = sel[j:j + 1]
        up = r > m1
        i1 = jnp.where(up, j, i1)
        m1 = jnp.where(up, r, m1)
    m2 = jnp.full((1, t), -jnp.inf, F32)
    i2 = jnp.zeros((1, t), I32)
    started = jnp.zeros((1, t), jnp.bool_)
    for j in range(MOE_EPG):
        r = sel[j:j + 1]
        ok = i1 != j
        up = ok & ((r > m2) | jnp.logical_not(started))
        i2 = jnp.where(up, j, i2)
        m2 = jnp.where(up, r, m2)
        started = started | ok
    e21 = jnp.exp(m2 - m1)
    p1 = 1.0 / (1.0 + e21)
    w1 = gate_g * p1
    w2 = gate_g * (e21 * p1)
    eid1 = gi * MOE_EPG + i1
    eid2 = gi * MOE_EPG + i2

    eio = lax.broadcasted_iota(I32, (MOE_EXPERTS, t), 0)
    oh1 = eio == eid1
    oh2 = eio == eid2
    oh = (oh1 | oh2).astype(F32)
    rio = lax.broadcasted_iota(I32, (t, t), 0)
    cio = lax.broadcasted_iota(I32, (t, t), 1)
    upper = (rio < cio).astype(BF16)
    cum = jnp.dot(oh.astype(BF16), upper, preferred_element_type=F32) + carry_ref[:, 0:1]
    rank1 = jnp.sum(jnp.where(oh1, cum, 0.0), axis=0, keepdims=True)
    rank2 = jnp.sum(jnp.where(oh2, cum, 0.0), axis=0, keepdims=True)
    carry_ref[...] = carry_ref[...] + jnp.sum(oh, axis=1, keepdims=True)
    cnt_ref[...] = carry_ref[...]

    zi = jnp.zeros((4, t), I32)
    ri_ref[...] = jnp.concatenate([eid1, eid2, rank1.astype(I32), rank2.astype(I32), zi], axis=0)
    zf = jnp.zeros((6, t), F32)
    rw_ref[...] = jnp.concatenate([w1, w2, zf], axis=0)


def _router(x, g, w_rg, b_rg, w_re, b_re):
    n = x.shape[0]
    t = T_ROUTE
    wr = jnp.zeros((40, D_MODEL), F32).at[0:MOE_GROUPS].set(w_rg.T).at[8:40].set(w_re.T)
    br = jnp.zeros((40,), F32).at[0:MOE_GROUPS].set(b_rg).at[8:40].set(b_re)
    br = jnp.broadcast_to(br[:, None], (40, 128))
    return pl.pallas_call(
        _router_kernel,
        grid=(n // t,),
        in_specs=[
            pl.BlockSpec((t, D_MODEL), lambda i: (i, 0)),
            pl.BlockSpec((1, D_MODEL), lambda i: (0, 0)),
            pl.BlockSpec((40, D_MODEL), lambda i: (0, 0)),
            pl.BlockSpec((40, 128), lambda i: (0, 0)),
        ],
        out_specs=[
            pl.BlockSpec((t, HALF), lambda i: (i, 0)),
            pl.BlockSpec((8, t), lambda i: (0, i)),
            pl.BlockSpec((8, t), lambda i: (0, i)),
            pl.BlockSpec((MOE_EXPERTS, 128), lambda i: (0, 0)),
        ],
        out_shape=[
            jax.ShapeDtypeStruct((n, HALF), U32),
            jax.ShapeDtypeStruct((8, n), I32),
            jax.ShapeDtypeStruct((8, n), F32),
            jax.ShapeDtypeStruct((MOE_EXPERTS, 128), F32),
        ],
        scratch_shapes=[pltpu.VMEM((MOE_EXPERTS, 128), F32)],
        compiler_params=_cparams(("arbitrary",)),
        name="moe_router",
    )(x, g, wr, br)


def _sc_mesh():
    return plsc.VectorSubcoreMesh(core_axis_name="c", subcore_axis_name="s",
                                  num_cores=SC_CORES, num_subcores=SC_SUBCORES)


def _sc_worker_id():
    return lax.axis_index("s") * SC_CORES + lax.axis_index("c")


def _sc_dispatch(hp, idx, zero_rows, total_rows):
    n = hp.shape[0]
    tpw = n // SC_WORKERS
    kd = tpw // SC_ROWS
    kp = idx.shape[1] - 2 * kd

    @functools.partial(
        pl.kernel, mesh=_sc_mesh(),
        out_type=jax.ShapeDtypeStruct((total_rows, HALF), U32),
        scratch_types=[
            pltpu.VMEM((2 * kd + kp, SC_ROWS), I32),
            pltpu.VMEM((SC_ROWS, HALF), U32), pltpu.VMEM((SC_ROWS, HALF), U32), pltpu.VMEM((SC_ROWS, HALF), U32),
            pltpu.SemaphoreType.DMA((2,)), pltpu.SemaphoreType.DMA((2,)), pltpu.SemaphoreType.DMA,
        ],
        name="moe_dispatch_sc",
    )
    def k(hp_hbm, idx_hbm, zero_hbm, xs_hbm, idx_v, buf0, buf1, zbuf, load_sem, scat_sem, pad_sem):
        wid = _sc_worker_id()
        bufs = (buf0, buf1)
        pltpu.sync_copy(idx_hbm.at[wid], idx_v)

        def load(c):
            return pltpu.make_async_copy(hp_hbm.at[pl.ds(wid * tpw + c * SC_ROWS, SC_ROWS)], bufs[c % 2],
                                         load_sem.at[c % 2])

        def scatters(c):
            return (pltpu.make_async_copy(bufs[c % 2], xs_hbm.at[idx_v.at[c]], scat_sem.at[c % 2]),
                    pltpu.make_async_copy(bufs[c % 2], xs_hbm.at[idx_v.at[kd + c]], scat_sem.at[c % 2]))

        load(0).start()
        pltpu.sync_copy(zero_hbm, zbuf)
        pads = [pltpu.make_async_copy(zbuf, xs_hbm.at[idx_v.at[2 * kd + j]], pad_sem) for j in range(kp)]
        for p in pads:
            p.start()
        for c in range(kd):
            load(c).wait()
            for d in scatters(c):
                d.start()
            if c + 1 < kd:
                if c >= 1:
                    for d in scatters(c - 1):
                        d.wait()
                load(c + 1).start()
        for c in range(max(kd - 2, 0), kd):
            for d in scatters(c):
                d.wait()
        for p in pads:
            p.wait()

    return k(hp, idx, zero_rows)


def _sc_gather(ys, idx):
    kg = idx.shape[1]
    rows_per_worker = kg * SC_ROWS

    @functools.partial(
        pl.kernel, mesh=_sc_mesh(),
        out_type=jax.ShapeDtypeStruct((SC_WORKERS * rows_per_worker, HALF), U32),
        scratch_types=[
            pltpu.VMEM((kg, SC_ROWS), I32),
            pltpu.VMEM((SC_ROWS, HALF), U32), pltpu.VMEM((SC_ROWS, HALF), U32),
            pltpu.SemaphoreType.DMA((2,)), pltpu.SemaphoreType.DMA((2,)),
        ],
        name="moe_gather_sc",
    )
    def k(ys_hbm, idx_hbm, yg_hbm, idx_v, buf0, buf1, gat_sem, out_sem):
        wid = _sc_worker_id()
        bufs = (buf0, buf1)
        pltpu.sync_copy(idx_hbm.at[wid], idx_v)

        def gather(c):
            return pltpu.make_async_copy(ys_hbm.at[idx_v.at[c]], bufs[c % 2], gat_sem.at[c % 2])

        def store(c):
            return pltpu.make_async_copy(bufs[c % 2],
                                         yg_hbm.at[pl.ds(wid * rows_per_worker + c * SC_ROWS, SC_ROWS)],
                                         out_sem.at[c % 2])

        gather(0).start()
        for c in range(kg):
            if c + 1 < kg:
                if c >= 1:
                    store(c - 1).wait()
                gather(c + 1).start()
            gather(c).wait()
            store(c).start()
        for c in range(max(kg - 2, 0), kg):
            store(c).wait()

    return k(ys, idx)


def _expert_kernel(blk_e_ref, nused_ref, first_ref, slot_ref, nxt_ref, xs_ref, wg_hbm, wu_hbm, wd_hbm,
                   ys_ref, wg_buf, wu_buf, wd_buf, sems, *, layer):
    step = pl.program_id(0)
    tm = TM_EXP
    nused = nused_ref[0]

    def weight_copies(e, s):
        return (pltpu.make_async_copy(wg_hbm.at[layer, e], wg_buf.at[s], sems.at[s, 0]),
                pltpu.make_async_copy(wu_hbm.at[layer, e], wu_buf.at[s], sems.at[s, 1]),
                pltpu.make_async_copy(wd_hbm.at[layer, e], wd_buf.at[s], sems.at[s, 2]))

    def dma_control(j):
        i = step * EXP_SUB + j

        @pl.when(i < nused)
        def _():
            s = slot_ref[i]

            if j == 0:
                @pl.when(i == 0)
                def _():
                    for c in weight_copies(blk_e_ref[0], 0):
                        c.start()

            @pl.when(first_ref[i] == 1)
            def _():
                for c in weight_copies(blk_e_ref[i], s):
                    c.wait()

                @pl.when(nxt_ref[i] >= 0)
                def _():
                    for c in weight_copies(nxt_ref[i], lax.rem(s + 1, EXP_SLOTS)):
                        c.start()

    for j in range(EXP_SUB):
        dma_control(j)

    @pl.when(step * EXP_SUB < nused)
    def _():
        for j in range(EXP_SUB):
            i = step * EXP_SUB + j
            rows = slice(j * tm, (j + 1) * tm)
            s = slot_ref[i]
            lo, hi = _unpack_bf16_pairs(xs_ref[rows, :])
            xf = jnp.concatenate([lo, hi], axis=1)
            a = jnp.dot(xf, wg_buf[s], preferred_element_type=F32)
            b = jnp.dot(xf, wu_buf[s], preferred_element_type=F32)
            hm = _silu(a) * b
            y = jnp.dot(hm, wd_buf[s], preferred_element_type=F32)
            ys_ref[rows, :] = jnp.where(i < nused, _pack_bf16_pairs(y), jnp.uint32(0))

    @pl.when(step * EXP_SUB >= nused)
    def _():
        ys_ref[...] = jnp.zeros_like(ys_ref)


def _experts(xs, blk_e, nused, w_gate, w_up, w_down, layer):
    tm = TM_EXP
    p_rows = xs.shape[0] - MOE_EXPERTS * tm
    nblk = p_rows // tm
    pos = jnp.arange(nblk, dtype=I32)
    valid = pos < nused[0]
    prev_e = jnp.concatenate([jnp.full((1,), -1, I32), blk_e[:-1]])
    first = valid & (blk_e != prev_e)
    slot = jnp.maximum(jnp.cumsum(first.astype(I32)) - 1, 0) % EXP_SLOTS
    first_pos = jnp.where(first, pos, nblk)
    next_first = jnp.concatenate([lax.cummin(first_pos, reverse=True)[1:], jnp.full((1,), nblk, I32)])
    nxt = jnp.where(next_first < nblk, blk_e[jnp.minimum(next_first, nblk - 1)], -1)

    def blk(i, be, nu, *_):
        return jnp.minimum(i, (nu[0] - 1) // EXP_SUB)

    grid_spec = pltpu.PrefetchScalarGridSpec(
        num_scalar_prefetch=5,
        grid=(nblk // EXP_SUB,),
        in_specs=[
            pl.BlockSpec((EXP_SUB * tm, HALF), lambda i, *sp: (blk(i, *sp), 0)),
            pl.BlockSpec(memory_space=pl.ANY),
            pl.BlockSpec(memory_space=pl.ANY),
            pl.BlockSpec(memory_space=pl.ANY),
        ],
        out_specs=pl.BlockSpec((EXP_SUB * tm, HALF), lambda i, *sp: (i, 0)),
        scratch_shapes=[
            pltpu.VMEM((EXP_SLOTS, D_MODEL, MOE_FF), F32),
            pltpu.VMEM((EXP_SLOTS, D_MODEL, MOE_FF), F32),
            pltpu.VMEM((EXP_SLOTS, MOE_FF, D_MODEL), F32),
            pltpu.SemaphoreType.DMA((EXP_SLOTS, 3)),
        ],
    )
    return pl.pallas_call(
        functools.partial(_expert_kernel, layer=layer),
        grid_spec=grid_spec,
        out_shape=jax.ShapeDtypeStruct((p_rows, HALF), U32),
        compiler_params=_cparams(("arbitrary",)),
        name="moe_experts",
    )(blk_e, nused, first.astype(I32), slot.astype(I32), nxt.astype(I32), xs, w_gate, w_up, w_down)


def _combine_kernel(x_ref, rw_ref, fg_ref, y1_ref, y2_ref, *rest):
    out_ref = rest[-1]
    lo1, hi1 = _unpack_bf16_pairs(y1_ref[...])
    lo2, hi2 = _unpack_bf16_pairs(y2_ref[...])
    w1, w2 = _route_weight_columns(rw_ref[...])
    x = x_ref[...]
    o_lo = x[:, :HALF] + w1 * lo1 + w2 * lo2
    o_hi = x[:, HALF:] + w1 * hi1 + w2 * hi2
    ms = (jnp.sum(o_lo * o_lo, axis=-1, keepdims=True)
          + jnp.sum(o_hi * o_hi, axis=-1, keepdims=True)) * (1.0 / D_MODEL)
    sc = lax.rsqrt(ms + NORM_EPS)
    o_lo = o_lo * sc * fg_ref[:, :HALF]
    o_hi = o_hi * sc * fg_ref[:, HALF:]
    out_ref[:, :HALF] = o_lo
    out_ref[:, HALF:] = o_hi


def _combine(yg_parts, x, rw, final_g):
    n = x.shape[0]
    td = T_COMB
    nbp = n // td // len(yg_parts)
    out = None
    for p, yg in enumerate(yg_parts):
        off = p * nbp
        in_specs = [
            pl.BlockSpec((td, D_MODEL), lambda i, off=off: (i + off, 0)),
            pl.BlockSpec((8, td), lambda i, off=off: (0, i + off)),
            pl.BlockSpec((1, D_MODEL), lambda i: (0, 0)),
            pl.BlockSpec((td, HALF), lambda i: (i, 0)),
            pl.BlockSpec((td, HALF), lambda i: (i + nbp, 0)),
        ]
        args = [x, rw, final_g, yg, yg]
        aliases = {}
        if out is not None:
            in_specs.append(pl.BlockSpec(memory_space=pl.ANY))
            aliases = {len(args): 0}
            args.append(out)
        out = pl.pallas_call(
            _combine_kernel,
            grid=(nbp,),
            in_specs=in_specs,
            out_specs=pl.BlockSpec((td, D_MODEL), lambda i, off=off: (i + off, 0)),
            out_shape=jax.ShapeDtypeStruct((n, D_MODEL), F32),
            input_output_aliases=aliases,
            compiler_params=_cparams(("arbitrary",)),
            name="moe_combine",
        )(*args)
    return out


def _moe(x, g, w_rg, b_rg, w_re, b_re, w_gate, w_up, w_down, layer):
    n = x.shape[0]
    tm = TM_EXP
    p_rows = 2 * n + MOE_EXPERTS * tm
    nblk = p_rows // tm
    hp, ri, rw, cnt = _router(x, g, w_rg, b_rg, w_re, b_re)
    counts = cnt[:, 0].astype(I32)
    pcounts = (counts + tm - 1) // tm * tm
    pend = jnp.cumsum(pcounts)
    pstart = pend - pcounts
    eio = jnp.arange(MOE_EXPERTS, dtype=I32)[:, None]
    dest1 = jnp.sum(jnp.where(ri[0][None, :] == eio, pstart[:, None], 0), axis=0) + ri[2]
    dest2 = jnp.sum(jnp.where(ri[1][None, :] == eio, pstart[:, None], 0), axis=0) + ri[3]
    blk_start = jnp.arange(nblk, dtype=I32) * tm
    blk_e = jnp.minimum(jnp.sum((pend[None, :] <= blk_start[:, None]).astype(I32), axis=1), MOE_EXPERTS - 1)
    nused = jnp.maximum(pend[-1] // tm, 1).astype(I32).reshape(1)
    r = jnp.arange(tm, dtype=I32)[None, :]
    pad_slot = jnp.where(r < (pcounts - counts)[:, None], (pstart + counts)[:, None] + r, p_rows + eio * tm + r)
    kd = n // SC_WORKERS // SC_ROWS
    idx = jnp.concatenate([dest1.reshape(SC_WORKERS, kd, SC_ROWS), dest2.reshape(SC_WORKERS, kd, SC_ROWS),
                           pad_slot.reshape(SC_WORKERS, -1, SC_ROWS)], axis=1)
    zero_rows = jnp.zeros((SC_ROWS, HALF), U32)
    xs = _sc_dispatch(hp, idx, zero_rows, p_rows + MOE_EXPERTS * tm)
    ys = _experts(xs, blk_e, nused, w_gate, w_up, w_down, layer)
    d1 = dest1.reshape(GATHER_PARTS, -1)
    d2 = dest2.reshape(GATHER_PARTS, -1)
    yg_parts = [_sc_gather(ys, jnp.concatenate([d1[p], d2[p]]).reshape(SC_WORKERS, -1, SC_ROWS))
                for p in range(GATHER_PARTS)]
    return yg_parts, rw


def kernel(x, positions, norm_mix_g, norm_ffn_g, ret_w_in, ret_head_g, ret_w_out, conv_w_pw1, conv_b_pw1, conv_w_dw, conv_b_dw, conv_ln_g, conv_ln_b, conv_w_pw2, conv_b_pw2, moe_w_rg, moe_b_rg, moe_w_re, moe_b_re, moe_w_gate, moe_w_up, moe_w_down, final_norm_g):
    b, s, d = x.shape
    n = b * s
    xt = x.reshape(n, d)
    pos = positions.reshape(n, 1)
    fg = final_norm_g.reshape(1, d)

    q, k, v, gate = _ret_inproj(xt, pos, norm_mix_g[0].reshape(1, d), ret_w_in[0])
    xt = _ret_core(q, k, v, gate, xt, ret_head_g[0].reshape(RET_V, 1), ret_w_out[0])
    yg, rw = _moe(xt, norm_ffn_g[0].reshape(1, d), moe_w_rg[0], moe_b_rg[0], moe_w_re[0], moe_b_re[0],
                  moe_w_gate, moe_w_up, moe_w_down, 0)

    xt, u = _conv_pw1(xt, rw, yg, norm_mix_g[1].reshape(1, d), conv_w_pw1[0],
                      conv_b_pw1[0].reshape(1, 2 * d))
    xt = _conv_core(u, xt, conv_w_dw[0], conv_b_dw[0].reshape(1, d), conv_ln_g[0].reshape(1, d),
                    conv_ln_b[0].reshape(1, d), conv_w_pw2[0], conv_b_pw2[0].reshape(1, d))
    yg, rw = _moe(xt, norm_ffn_g[1].reshape(1, d), moe_w_rg[1], moe_b_rg[1], moe_w_re[1], moe_b_re[1],
                  moe_w_gate, moe_w_up, moe_w_down, 1)
    xt = _combine(yg, xt, rw, fg)
    return xt.reshape(b, s, d)
```

```python
import functools

import jax
import jax.numpy as jnp
from jax import lax
from jax.experimental import pallas as pl
from jax.experimental.pallas import tpu as pltpu
from jax.experimental.pallas import tpu_sc as plsc

F32 = jnp.float32
BF16 = jnp.bfloat16
U32 = jnp.uint32
I32 = jnp.int32

D_MODEL = 1024
RET_HEADS = 4
RET_DK = 256
RET_DV = 512
RET_QK = RET_HEADS * RET_DK
RET_V = RET_HEADS * RET_DV
ROPE_BASE = 10000.0
CONV_WIDTH = 31
MOE_GROUPS = 4
MOE_EPG = 8
MOE_EXPERTS = MOE_GROUPS * MOE_EPG
MOE_FF = 512
NORM_EPS = 1e-6

TM_PROJ = 512
RET_C = 256
RET_STEP = 512
TM_CONV = 512
CONV_HALO = 32
CONV_ROWS = 128
CONV_STRIDE = 4
T_ROUTE = 512
TM_EXP = 256
EXP_SUB = 4
EXP_SLOTS = EXP_SUB + 1
T_COMB = 512
SC_CORES = 2
SC_SUBCORES = 16
SC_WORKERS = SC_CORES * SC_SUBCORES
SC_ROWS = 64
HALF = D_MODEL // 2

VMEM_LIMIT = 56 * 1024 * 1024


def _cparams(sem, flags=None):
    return pltpu.CompilerParams(dimension_semantics=sem, vmem_limit_bytes=VMEM_LIMIT, flags=flags)


def _rms(x, g):
    ms = jnp.mean(x * x, axis=-1, keepdims=True)
    return x * lax.rsqrt(ms + NORM_EPS) * g


def _silu(x):
    return x * (1.0 / (1.0 + jnp.exp(-x)))


def _pack_bf16_pairs(y):
    lo = pltpu.bitcast(y[:, :HALF].astype(BF16).astype(F32), U32)
    hi = pltpu.bitcast(y[:, HALF:].astype(BF16).astype(F32), U32)
    return (hi & jnp.uint32(0xFFFF0000)) | (lo >> 16)


def _route_weight_columns(rw):
    t = jnp.concatenate([rw] * 16, axis=0).T
    return t[:, 0:1], t[:, 1:2]


def _unpack_bf16_pairs(p):
    lo = pltpu.bitcast(p << 16, F32)
    hi = pltpu.bitcast(p & jnp.uint32(0xFFFF0000), F32)
    return lo, hi


def _ret_inproj_kernel(x_ref, pos_ref, g_ref, inv_ref, w_ref, q_ref, k_ref, v_ref, gate_ref):
    half = RET_DK // 2
    kscale = RET_DK ** -0.5
    h = _rms(x_ref[...], g_ref[...]).astype(BF16)

    def proj(c0, width):
        return jnp.dot(h, w_ref[:, c0:c0 + width].astype(BF16), preferred_element_type=F32)

    v0 = proj(2 * RET_QK, 512)
    v_ref[:, 0:512] = v0.astype(BF16)
    zero = ((pltpu.bitcast(v0[:, 0:half], U32) >> 16) >> 16).astype(F32)
    ang = pos_ref[...].astype(F32) * inv_ref[...] + zero
    cos = jnp.cos(ang)
    sin = jnp.sin(ang)
    for j in range(RET_V // 512):
        if j > 0:
            v_ref[:, j * 512:(j + 1) * 512] = proj(2 * RET_QK + j * 512, 512).astype(BF16)
        gate_ref[:, j * 512:(j + 1) * 512] = proj(2 * RET_QK + RET_V + j * 512, 512).astype(BF16)

    for hd in range(RET_HEADS):
        for base, out_ref, cs, sn in ((0, q_ref, cos, sin), (RET_QK, k_ref, cos * kscale, sin * kscale)):
            t = proj(base + hd * RET_DK, RET_DK)
            t1 = t[:, :half]
            t2 = t[:, half:]
            out_ref[:, hd * RET_DK:hd * RET_DK + half] = (t1 * cs - t2 * sn).astype(BF16)
            out_ref[:, hd * RET_DK + half:(hd + 1) * RET_DK] = (t1 * sn + t2 * cs).astype(BF16)


def _ret_inproj(x, pos, g, w_in):
    n = x.shape[0]
    half = RET_DK // 2
    inv = (ROPE_BASE ** (-jnp.arange(half, dtype=F32) / half)).reshape(1, half)
    tm = TM_PROJ
    return pl.pallas_call(
        _ret_inproj_kernel,
        grid=(n // tm,),
        in_specs=[
            pl.BlockSpec((tm, D_MODEL), lambda i: (i, 0)),
            pl.BlockSpec((tm, 1), lambda i: (i, 0)),
            pl.BlockSpec((1, D_MODEL), lambda i: (0, 0)),
            pl.BlockSpec((1, half), lambda i: (0, 0)),
            pl.BlockSpec(w_in.shape, lambda i: (0, 0), pipeline_mode=pl.Buffered(1)),
        ],
        out_specs=[
            pl.BlockSpec((tm, RET_QK), lambda i: (i, 0)),
            pl.BlockSpec((tm, RET_QK), lambda i: (i, 0)),
            pl.BlockSpec((tm, RET_V), lambda i: (i, 0)),
            pl.BlockSpec((tm, RET_V), lambda i: (i, 0)),
        ],
        out_shape=[
            jax.ShapeDtypeStruct((n, RET_QK), BF16),
            jax.ShapeDtypeStruct((n, RET_QK), BF16),
            jax.ShapeDtypeStruct((n, RET_V), BF16),
            jax.ShapeDtypeStruct((n, RET_V), BF16),
        ],
        compiler_params=_cparams(("arbitrary",)),
        name="ret_inproj",
    )(x, pos, g, inv, w_in)


def _ret_core_kernel(cdec_ref, q_ref, k_ref, v_ref, gate_ref, x_ref, hg_ref, intra_ref, cross_ref,
                     kdec_ref, wo_ref, out_ref, state_ref, y_ref, wos_ref):
    @pl.when(pl.program_id(0) == 0)
    def _():
        state_ref[...] = jnp.zeros_like(state_ref)
        wos_ref[...] = (wo_ref[...] * hg_ref[...]).astype(BF16)

    for r0 in range(0, RET_STEP, RET_C):
        rs = slice(r0, r0 + RET_C)
        for hd in range(RET_HEADS):
            q = q_ref[rs, hd * RET_DK:(hd + 1) * RET_DK]
            k = k_ref[rs, hd * RET_DK:(hd + 1) * RET_DK]
            v = v_ref[rs, hd * RET_DV:(hd + 1) * RET_DV]
            state = state_ref[hd]
            scores = lax.dot_general(q, k, (((1,), (1,)), ((), ())), preferred_element_type=F32)
            scores = (scores * intra_ref[hd]).astype(BF16)
            o = jnp.dot(scores, v, preferred_element_type=F32)
            cross = cross_ref[hd]
            o_cross = jnp.dot(q, state.astype(BF16), preferred_element_type=F32)
            o = o + o_cross * jnp.concatenate([cross] * (RET_DV // 128), axis=1)
            kdec = kdec_ref[hd]
            kd = (k.astype(F32) * jnp.concatenate([kdec] * (RET_DK // 128), axis=1)).astype(BF16)
            upd = lax.dot_general(kd, v, (((0,), (0,)), ((), ())), preferred_element_type=F32)
            state_ref[hd] = state * cdec_ref[hd] + upd
            ms = jnp.mean(o * o, axis=-1, keepdims=True)
            on = o * lax.rsqrt(ms + NORM_EPS)
            gt = gate_ref[rs, hd * RET_DV:(hd + 1) * RET_DV].astype(F32)
            y_ref[rs, hd * RET_DV:(hd + 1) * RET_DV] = (_silu(gt) * on).astype(BF16)
        out_ref[rs, :] = x_ref[rs, :] + jnp.dot(y_ref[rs, :], wos_ref[...], preferred_element_type=F32)


def _ret_core(q, k, v, gate, x, head_g_col, w_out):
    n = x.shape[0]
    c = RET_C
    log_gamma = jnp.log1p(-(2.0 ** (-5.0 - jnp.arange(RET_HEADS, dtype=F32))))
    idx = jnp.arange(c, dtype=F32)
    diff = idx[:, None] - idx[None, :]
    intra = jnp.where(diff >= 0, jnp.exp(log_gamma[:, None, None] * jnp.maximum(diff, 0.0)), 0.0)
    cross = jnp.broadcast_to(jnp.exp(log_gamma[:, None] * (idx + 1.0))[:, :, None], (RET_HEADS, c, 128))
    kdec = jnp.broadcast_to(jnp.exp(log_gamma[:, None] * (c - 1.0 - idx))[:, :, None], (RET_HEADS, c, 128))
    cdec = jnp.exp(log_gamma * c)
    return pl.pallas_call(
        _ret_core_kernel,
        grid=(n // RET_STEP,),
        in_specs=[
            pl.BlockSpec(memory_space=pltpu.SMEM),
            pl.BlockSpec((RET_STEP, RET_QK), lambda i: (i, 0)),
            pl.BlockSpec((RET_STEP, RET_QK), lambda i: (i, 0)),
            pl.BlockSpec((RET_STEP, RET_V), lambda i: (i, 0)),
            pl.BlockSpec((RET_STEP, RET_V), lambda i: (i, 0)),
            pl.BlockSpec((RET_STEP, D_MODEL), lambda i: (i, 0)),
            pl.BlockSpec((RET_V, 1), lambda i: (0, 0)),
            pl.BlockSpec((RET_HEADS, c, c), lambda i: (0, 0, 0)),
            pl.BlockSpec((RET_HEADS, c, 128), lambda i: (0, 0, 0)),
            pl.BlockSpec((RET_HEADS, c, 128), lambda i: (0, 0, 0)),
            pl.BlockSpec((RET_V, D_MODEL), lambda i: (0, 0)),
        ],
        out_specs=pl.BlockSpec((RET_STEP, D_MODEL), lambda i: (i, 0)),
        out_shape=jax.ShapeDtypeStruct((n, D_MODEL), F32),
        scratch_shapes=[
            pltpu.VMEM((RET_HEADS, RET_DK, RET_DV), F32),
            pltpu.VMEM((RET_STEP, RET_V), BF16),
            pltpu.VMEM((RET_V, D_MODEL), BF16),
        ],
        compiler_params=_cparams(("arbitrary",)),
        name="ret_core",
    )(cdec, q, k, v, gate, x, head_g_col, intra, cross, kdec, w_out)


def _conv_pw1_kernel(x_ref, rw_ref, y1_ref, y2_ref, g_ref, w_ref, b_ref, xo_ref, u_ref):
    r = TM_CONV // 2
    w1, w2 = _route_weight_columns(rw_ref[...])

    def prologue(rs):
        lo1, hi1 = _unpack_bf16_pairs(y1_ref[rs, :])
        lo2, hi2 = _unpack_bf16_pairs(y2_ref[rs, :])
        x = jnp.concatenate([x_ref[rs, :HALF] + w1[rs] * lo1 + w2[rs] * lo2,
                             x_ref[rs, HALF:] + w1[rs] * hi1 + w2[rs] * hi2], axis=1)
        xo_ref[rs, :] = x
        hf = _rms(x, g_ref[...])
        return hf.astype(BF16), hf[:, 0:128]

    groups = [slice(0, r), slice(r, 2 * r)]
    hs = [prologue(rs) for rs in groups]
    zero = ((pltpu.bitcast(hs[1][1], U32) >> 16) >> 16).astype(F32)
    for j in range(D_MODEL // 512):
        wa = w_ref[:, j * 512:(j + 1) * 512].astype(BF16)
        wg = w_ref[:, D_MODEL + j * 512:D_MODEL + (j + 1) * 512].astype(BF16)
        for gi, rs in enumerate(groups):
            h = hs[gi][0]
            a = jnp.dot(h, wa, preferred_element_type=F32) + b_ref[:, j * 512:(j + 1) * 512]
            gt = jnp.dot(h, wg, preferred_element_type=F32) + b_ref[:, D_MODEL + j * 512:D_MODEL + (j + 1) * 512]
            u = a * (1.0 / (1.0 + jnp.exp(-gt)))
            if gi == 0 and j == 0:
                u_ref[rs, 0:128] = u[:, 0:128] + zero
                u_ref[rs, 128:512] = u[:, 128:512]
            else:
                u_ref[rs, j * 512:(j + 1) * 512] = u


def _conv_pw1(x, rw, yg, g, w, b):
    n = x.shape[0]
    tm = TM_CONV
    nb = n // tm
    return pl.pallas_call(
        _conv_pw1_kernel,
        grid=(nb,),
        in_specs=[
            pl.BlockSpec((tm, D_MODEL), lambda i: (i, 0)),
            pl.BlockSpec((8, tm), lambda i: (0, i)),
            pl.BlockSpec((tm, HALF), lambda i: (i, 0)),
            pl.BlockSpec((tm, HALF), lambda i: (i + nb, 0)),
            pl.BlockSpec((1, D_MODEL), lambda i: (0, 0)),
            pl.BlockSpec((D_MODEL, 2 * D_MODEL), lambda i: (0, 0)),
            pl.BlockSpec((1, 2 * D_MODEL), lambda i: (0, 0)),
        ],
        out_specs=[pl.BlockSpec((tm, D_MODEL), lambda i: (i, 0)), pl.BlockSpec((tm, D_MODEL), lambda i: (i, 0))],
        out_shape=[jax.ShapeDtypeStruct((n, D_MODEL), F32), jax.ShapeDtypeStruct((n, D_MODEL), F32)],
        compiler_params=_cparams(("arbitrary",)),
        name="conv_pw1",
    )(x, rw, yg, yg, g, w, b)


def _conv_core_kernel(u_ref, halo_ref, x_ref, wdw_ref, bdw_ref, lng_ref, lnb_ref, w2_ref, b2_ref,
                      out_ref, win_ref, z_ref):
    tm = TM_CONV
    first = pl.program_id(0) == 0
    halo = halo_ref[...]
    halo = jnp.where(first, jnp.zeros_like(halo), halo)
    nslab = D_MODEL // 128
    for cc in range(nslab):
        cs = slice(cc * 128, (cc + 1) * 128)
        win_ref[cc, 0:CONV_HALO, :] = halo[:, cs]
        win_ref[cc, CONV_HALO:CONV_HALO + tm, :] = u_ref[:, cs]
    off = CONV_HALO - (CONV_WIDTH - 1)
    rb = CONV_ROWS
    st = CONV_STRIDE
    for cc in range(nslab):
        cs = slice(cc * 128, (cc + 1) * 128)
        for r0 in range(0, tm, rb):
            accs = [bdw_ref[:, cs]] * st
            for o in range(CONV_WIDTH):
                w_o = wdw_ref[o:o + 1, cs]
                for rho in range(st):
                    accs[rho] = accs[rho] + win_ref[cc, pl.ds(r0 + rho + off + o, rb // st, stride=st), :] * w_o
            for rho in range(st):
                z_ref[cc, pl.ds(r0 + rho, rb // st, stride=st), :] = accs[rho]
    z = jnp.concatenate([z_ref[cc] for cc in range(nslab)], axis=1)
    mu = jnp.mean(z, axis=-1, keepdims=True)
    zc = z - mu
    var = jnp.mean(zc * zc, axis=-1, keepdims=True)
    zn = zc * lax.rsqrt(var + NORM_EPS) * lng_ref[...] + lnb_ref[...]
    y = _silu(zn).astype(BF16)
    out_ref[...] = x_ref[...] + jnp.dot(y, w2_ref[...].astype(BF16), preferred_element_type=F32) + b2_ref[...]


def _conv_core(u, x, w_dw, b_dw, ln_g, ln_b, w2, b2):
    n = x.shape[0]
    tm = TM_CONV
    r = tm // CONV_HALO
    wdw_pad = jnp.zeros((32, D_MODEL), F32).at[:CONV_WIDTH].set(w_dw)
    return pl.pallas_call(
        _conv_core_kernel,
        grid=(n // tm,),
        in_specs=[
            pl.BlockSpec((tm, D_MODEL), lambda i: (i, 0)),
            pl.BlockSpec((CONV_HALO, D_MODEL), lambda i: (jnp.maximum(i * r - 1, 0), 0)),
            pl.BlockSpec((tm, D_MODEL), lambda i: (i, 0)),
            pl.BlockSpec((32, D_MODEL), lambda i: (0, 0)),
            pl.BlockSpec((1, D_MODEL), lambda i: (0, 0)),
            pl.BlockSpec((1, D_MODEL), lambda i: (0, 0)),
            pl.BlockSpec((1, D_MODEL), lambda i: (0, 0)),
            pl.BlockSpec((D_MODEL, D_MODEL), lambda i: (0, 0)),
            pl.BlockSpec((1, D_MODEL), lambda i: (0, 0)),
        ],
        out_specs=pl.BlockSpec((tm, D_MODEL), lambda i: (i, 0)),
        out_shape=jax.ShapeDtypeStruct((n, D_MODEL), F32),
        scratch_shapes=[pltpu.VMEM((D_MODEL // 128, CONV_HALO + tm, 128), F32),
                        pltpu.VMEM((D_MODEL // 128, tm, 128), F32)],
        compiler_params=_cparams(("arbitrary",)),
        name="conv_core",
    )(u, u, x, wdw_pad, b_dw, ln_g, ln_b, w2, b2)


def _router_kernel(x_ref, g_ref, wr_ref, br_ref, hp_ref, ri_ref, rw_ref, cnt_ref, carry_ref):
    t = T_ROUTE

    @pl.when(pl.program_id(0) == 0)
    def _():
        carry_ref[...] = jnp.zeros_like(carry_ref)

    h = _rms(x_ref[...], g_ref[...])
    hp_ref[...] = _pack_bf16_pairs(h)
    h_hi = h.astype(BF16)
    h_lo = (h - h_hi.astype(F32)).astype(BF16)
    w = wr_ref[...]
    w_hi = w.astype(BF16)
    w_lo = (w - w_hi.astype(F32)).astype(BF16)
    dn = (((1,), (1,)), ((), ()))
    p = lax.dot_general(jnp.concatenate([w_hi, w_lo], axis=0), h_hi, dn, preferred_element_type=F32)
    nr = wr_ref.shape[0]
    logits = p[0:nr] + p[nr:2 * nr] + lax.dot_general(w_hi, h_lo, dn, preferred_element_type=F32)
    logits = logits + br_ref[:, 0:1]

    best = logits[0:1]
    gi = jnp.zeros((1, t), I32)
    for j in range(1, MOE_GROUPS):
        r = logits[j:j + 1]
        up = r > best
        gi = jnp.where(up, j, gi)
        best = jnp.where(up, r, best)
    den = jnp.zeros((1, t), F32)
    for j in range(MOE_GROUPS):
        den = den + jnp.exp(logits[j:j + 1] - best)
    gate_g = 1.0 / den

    sel = logits[8:8 + MOE_EPG]
    for j in range(1, MOE_GROUPS):
        sel = jnp.where(gi == j, logits[8 + j * MOE_EPG:8 + (j + 1) * MOE_EPG], sel)

    m1 = sel[0:1]
    i1 = jnp.zeros((1, t), I32)
    for j in range(1, MOE_EPG):
        r = sel[j:j + 1]
        up = r > m1
        i1 = jnp.where(up, j, i1)
        m1 = jnp.where(up, r, m1)
    m2 = jnp.full((1, t), -jnp.inf, F32)
    i2 = jnp.zeros((1, t), I32)
    started = jnp.zeros((1, t), jnp.bool_)
    for j in range(MOE_EPG):
        r = sel[j:j + 1]
        ok = i1 != j
        up = ok & ((r > m2) | jnp.logical_not(started))
        i2 = jnp.where(up, j, i2)
        m2 = jnp.where(up, r, m2)
        started = started | ok
    e21 = jnp.exp(m2 - m1)
    p1 = 1.0 / (1.0 + e21)
    w1 = gate_g * p1
    w2 = gate_g * (e21 * p1)
    eid1 = gi * MOE_EPG + i1
    eid2 = gi * MOE_EPG + i2

    eio = lax.broadcasted_iota(I32, (MOE_EXPERTS, t), 0)
    oh1 = eio == eid1
    oh2 = eio == eid2
    oh = (oh1 | oh2).astype(F32)
    rio = lax.broadcasted_iota(I32, (t, t), 0)
    cio = lax.broadcasted_iota(I32, (t, t), 1)
    upper = (rio < cio).astype(BF16)
    cum = jnp.dot(oh.astype(BF16), upper, preferred_element_type=F32) + carry_ref[:, 0:1]
    rank1 = jnp.sum(jnp.where(oh1, cum, 0.0), axis=0, keepdims=True)
    rank2 = jnp.sum(jnp.where(oh2, cum, 0.0), axis=0, keepdims=True)
    carry_ref[...] = carry_ref[...] + jnp.sum(oh, axis=1, keepdims=True)
    cnt_ref[...] = carry_ref[...]

    zi = jnp.zeros((4, t), I32)
    ri_ref[...] = jnp.concatenate([eid1, eid2, rank1.astype(I32), rank2.astype(I32), zi], axis=0)
    zf = jnp.zeros((6, t), F32)
    rw_ref[...] = jnp.concatenate([w1, w2, zf], axis=0)


def _router(x, g, w_rg, b_rg, w_re, b_re):
    n = x.shape[0]
    t = T_ROUTE
    wr = jnp.zeros((40, D_MODEL), F32).at[0:MOE_GROUPS].set(w_rg.T).at[8:40].set(w_re.T)
    br = jnp.zeros((40,), F32).at[0:MOE_GROUPS].set(b_rg).at[8:40].set(b_re)
    br = jnp.broadcast_to(br[:, None], (40, 128))
    return pl.pallas_call(
        _router_kernel,
        grid=(n // t,),
        in_specs=[
            pl.BlockSpec((t, D_MODEL), lambda i: (i, 0)),
            pl.BlockSpec((1, D_MODEL), lambda i: (0, 0)),
            pl.BlockSpec((40, D_MODEL), lambda i: (0, 0)),
            pl.BlockSpec((40, 128), lambda i: (0, 0)),
        ],
        out_specs=[
            pl.BlockSpec((t, HALF), lambda i: (i, 0)),
            pl.BlockSpec((8, t), lambda i: (0, i)),
            pl.BlockSpec((8, t), lambda i: (0, i)),
            pl.BlockSpec((MOE_EXPERTS, 128), lambda i: (0, 0)),
        ],
        out_shape=[
            jax.ShapeDtypeStruct((n, HALF), U32),
            jax.ShapeDtypeStruct((8, n), I32),
            jax.ShapeDtypeStruct((8, n), F32),
            jax.ShapeDtypeStruct((MOE_EXPERTS, 128), F32),
        ],
        scratch_shapes=[pltpu.VMEM((MOE_EXPERTS, 128), F32)],
        compiler_params=_cparams(("arbitrary",)),
        name="moe_router",
    )(x, g, wr, br)


def _sc_mesh():
    return plsc.VectorSubcoreMesh(core_axis_name="c", subcore_axis_name="s",
                                  num_cores=SC_CORES, num_subcores=SC_SUBCORES)


def _sc_worker_id():
    return lax.axis_index("s") * SC_CORES + lax.axis_index("c")


def _sc_dispatch(hp, idx, zero_rows, total_rows):
    n = hp.shape[0]
    tpw = n // SC_WORKERS
    kd = tpw // SC_ROWS
    kp = idx.shape[1] - 2 * kd

    @functools.partial(
        pl.kernel, mesh=_sc_mesh(),
        out_type=jax.ShapeDtypeStruct((total_rows, HALF), U32),
        scratch_types=[
            pltpu.VMEM((2 * kd + kp, SC_ROWS), I32),
            pltpu.VMEM((SC_ROWS, HALF), U32), pltpu.VMEM((SC_ROWS, HALF), U32), pltpu.VMEM((SC_ROWS, HALF), U32),
            pltpu.SemaphoreType.DMA((2,)), pltpu.SemaphoreType.DMA((2,)), pltpu.SemaphoreType.DMA,
        ],
        name="moe_dispatch_sc",
    )
    def k(hp_hbm, idx_hbm, zero_hbm, xs_hbm, idx_v, buf0, buf1, zbuf, load_sem, scat_sem, pad_sem):
        wid = _sc_worker_id()
        bufs = (buf0, buf1)
        pltpu.sync_copy(idx_hbm.at[wid], idx_v)

        def load(c):
            return pltpu.make_async_copy(hp_hbm.at[pl.ds(wid * tpw + c * SC_ROWS, SC_ROWS)], bufs[c % 2],
                                         load_sem.at[c % 2])

        def scatters(c):
            return (pltpu.make_async_copy(bufs[c % 2], xs_hbm.at[idx_v.at[c]], scat_sem.at[c % 2]),
                    pltpu.make_async_copy(bufs[c % 2], xs_hbm.at[idx_v.at[kd + c]], scat_sem.at[c % 2]))

        load(0).start()
        pltpu.sync_copy(zero_hbm, zbuf)
        pads = [pltpu.make_async_copy(zbuf, xs_hbm.at[idx_v.at[2 * kd + j]], pad_sem) for j in range(kp)]
        for p in pads:
            p.start()
        for c in range(kd):
            load(c).wait()
            for d in scatters(c):
                d.start()
            if c + 1 < kd:
                if c >= 1:
                    for d in scatters(c - 1):
                        d.wait()
                load(c + 1).start()
        for c in range(max(kd - 2, 0), kd):
            for d in scatters(c):
                d.wait()
        for p in pads:
            p.wait()

    return k(hp, idx, zero_rows)


def _sc_gather(ys, idx):
    kg = idx.shape[1]
    rows_per_worker = kg * SC_ROWS

    @functools.partial(
        pl.kernel, mesh=_sc_mesh(),
        out_type=jax.ShapeDtypeStruct((SC_WORKERS * rows_per_worker, HALF), U32),
        scratch_types=[
            pltpu.VMEM((kg, SC_ROWS), I32),
            pltpu.VMEM((SC_ROWS, HALF), U32), pltpu.VMEM((SC_ROWS, HALF), U32),
            pltpu.SemaphoreType.DMA((2,)), pltpu.SemaphoreType.DMA((2,)),
        ],
        name="moe_gather_sc",
    )
    def k(ys_hbm, idx_hbm, yg_hbm, idx_v, buf0, buf1, gat_sem, out_sem):
        wid = _sc_worker_id()
        bufs = (buf0, buf1)
        pltpu.sync_copy(idx_hbm.at[wid], idx_v)

        def gather(c):
            return pltpu.make_async_copy(ys_hbm.at[idx_v.at[c]], bufs[c % 2], gat_sem.at[c % 2])

        def store(c):
            return pltpu.make_async_copy(bufs[c % 2],
                                         yg_hbm.at[pl.ds(wid * rows_per_worker + c * SC_ROWS, SC_ROWS)],
                                         out_sem.at[c % 2])

        gather(0).start()
        for c in range(kg):
            if c + 1 < kg:
                if c >= 1:
                    store(c - 1).wait()
                gather(c + 1).start()
            gather(c).wait()
            store(c).start()
        for c in range(max(kg - 2, 0), kg):
            store(c).wait()

    return k(ys, idx)


def _expert_kernel(blk_e_ref, nused_ref, first_ref, slot_ref, nxt_ref, xs_ref, wg_hbm, wu_hbm, wd_hbm,
                   ys_ref, wg_buf, wu_buf, wd_buf, sems, *, layer):
    step = pl.program_id(0)
    tm = TM_EXP
    nused = nused_ref[0]

    def weight_copies(e, s):
        return (pltpu.make_async_copy(wg_hbm.at[layer, e], wg_buf.at[s], sems.at[s, 0]),
                pltpu.make_async_copy(wu_hbm.at[layer, e], wu_buf.at[s], sems.at[s, 1]),
                pltpu.make_async_copy(wd_hbm.at[layer, e], wd_buf.at[s], sems.at[s, 2]))

    def dma_control(j):
        i = step * EXP_SUB + j

        @pl.when(i < nused)
        def _():
            s = slot_ref[i]

            if j == 0:
                @pl.when(i == 0)
                def _():
                    for c in weight_copies(blk_e_ref[0], 0):
                        c.start()

            @pl.when(first_ref[i] == 1)
            def _():
                for c in weight_copies(blk_e_ref[i], s):
                    c.wait()

                @pl.when(nxt_ref[i] >= 0)
                def _():
                    for c in weight_copies(nxt_ref[i], lax.rem(s + 1, EXP_SLOTS)):
                        c.start()

    for j in range(EXP_SUB):
        dma_control(j)

    @pl.when(step * EXP_SUB < nused)
    def _():
        for j in range(EXP_SUB):
            i = step * EXP_SUB + j
            rows = slice(j * tm, (j + 1) * tm)
            s = slot_ref[i]
            lo, hi = _unpack_bf16_pairs(xs_ref[rows, :])
            xf = jnp.concatenate([lo, hi], axis=1)
            a = jnp.dot(xf, wg_buf[s], preferred_element_type=F32)
            b = jnp.dot(xf, wu_buf[s], preferred_element_type=F32)
            hm = _silu(a) * b
            y = jnp.dot(hm, wd_buf[s], preferred_element_type=F32)
            ys_ref[rows, :] = jnp.where(i < nused, _pack_bf16_pairs(y), jnp.uint32(0))

    @pl.when(step * EXP_SUB >= nused)
    def _():
        ys_ref[...] = jnp.zeros_like(ys_ref)


def _experts(xs, blk_e, nused, w_gate, w_up, w_down, layer):
    tm = TM_EXP
    p_rows = xs.shape[0] - MOE_EXPERTS * tm
    nblk = p_rows // tm
    pos = jnp.arange(nblk, dtype=I32)
    valid = pos < nused[0]
    prev_e = jnp.concatenate([jnp.full((1,), -1, I32), blk_e[:-1]])
    first = valid & (blk_e != prev_e)
    slot = jnp.maximum(jnp.cumsum(first.astype(I32)) - 1, 0) % EXP_SLOTS
    first_pos = jnp.where(first, pos, nblk)
    next_first = jnp.concatenate([lax.cummin(first_pos, reverse=True)[1:], jnp.full((1,), nblk, I32)])
    nxt = jnp.where(next_first < nblk, blk_e[jnp.minimum(next_first, nblk - 1)], -1)

    def blk(i, be, nu, *_):
        return jnp.minimum(i, (nu[0] - 1) // EXP_SUB)

    grid_spec = pltpu.PrefetchScalarGridSpec(
        num_scalar_prefetch=5,
        grid=(nblk // EXP_SUB,),
        in_specs=[
            pl.BlockSpec((EXP_SUB * tm, HALF), lambda i, *sp: (blk(i, *sp), 0)),
            pl.BlockSpec(memory_space=pl.ANY),
            pl.BlockSpec(memory_space=pl.ANY),
            pl.BlockSpec(memory_space=pl.ANY),
        ],
        out_specs=pl.BlockSpec((EXP_SUB * tm, HALF), lambda i, *sp: (i, 0)),
        scratch_shapes=[
            pltpu.VMEM((EXP_SLOTS, D_MODEL, MOE_FF), F32),
            pltpu.VMEM((EXP_SLOTS, D_MODEL, MOE_FF), F32),
            pltpu.VMEM((EXP_SLOTS, MOE_FF, D_MODEL), F32),
            pltpu.SemaphoreType.DMA((EXP_SLOTS, 3)),
        ],
    )
    return pl.pallas_call(
        functools.partial(_expert_kernel, layer=layer),
        grid_spec=grid_spec,
        out_shape=jax.ShapeDtypeStruct((p_rows, HALF), U32),
        compiler_params=_cparams(("arbitrary",)),
        name="moe_experts",
    )(blk_e, nused, first.astype(I32), slot.astype(I32), nxt.astype(I32), xs, w_gate, w_up, w_down)


def _combine_kernel(x_ref, rw_ref, fg_ref, y1_ref, y2_ref, out_ref):
    lo1, hi1 = _unpack_bf16_pairs(y1_ref[...])
    lo2, hi2 = _unpack_bf16_pairs(y2_ref[...])
    w1, w2 = _route_weight_columns(rw_ref[...])
    x = x_ref[...]
    o_lo = x[:, :HALF] + w1 * lo1 + w2 * lo2
    o_hi = x[:, HALF:] + w1 * hi1 + w2 * hi2
    ms = (jnp.sum(o_lo * o_lo, axis=-1, keepdims=True)
          + jnp.sum(o_hi * o_hi, axis=-1, keepdims=True)) * (1.0 / D_MODEL)
    sc = lax.rsqrt(ms + NORM_EPS)
    o_lo = o_lo * sc * fg_ref[:, :HALF]
    o_hi = o_hi * sc * fg_ref[:, HALF:]
    out_ref[:, :HALF] = o_lo
    out_ref[:, HALF:] = o_hi


def _combine(yg, x, rw, final_g):
    n = x.shape[0]
    td = T_COMB
    nb = n // td
    return pl.pallas_call(
        _combine_kernel,
        grid=(nb,),
        in_specs=[
            pl.BlockSpec((td, D_MODEL), lambda i: (i, 0)),
            pl.BlockSpec((8, td), lambda i: (0, i)),
            pl.BlockSpec((1, D_MODEL), lambda i: (0, 0)),
            pl.BlockSpec((td, HALF), lambda i: (i, 0)),
            pl.BlockSpec((td, HALF), lambda i: (i + nb, 0)),
        ],
        out_specs=pl.BlockSpec((td, D_MODEL), lambda i: (i, 0)),
        out_shape=jax.ShapeDtypeStruct((n, D_MODEL), F32),
        compiler_params=_cparams(("arbitrary",)),
        name="moe_combine",
    )(x, rw, final_g, yg, yg)


def _moe(x, g, w_rg, b_rg, w_re, b_re, w_gate, w_up, w_down, layer):
    n = x.shape[0]
    tm = TM_EXP
    p_rows = 2 * n + MOE_EXPERTS * tm
    nblk = p_rows // tm
    hp, ri, rw, cnt = _router(x, g, w_rg, b_rg, w_re, b_re)
    counts = cnt[:, 0].astype(I32)
    pcounts = (counts + tm - 1) // tm * tm
    pend = jnp.cumsum(pcounts)
    pstart = pend - pcounts
    eio = jnp.arange(MOE_EXPERTS, dtype=I32)[:, None]
    dest1 = jnp.sum(jnp.where(ri[0][None, :] == eio, pstart[:, None], 0), axis=0) + ri[2]
    dest2 = jnp.sum(jnp.where(ri[1][None, :] == eio, pstart[:, None], 0), axis=0) + ri[3]
    blk_start = jnp.arange(nblk, dtype=I32) * tm
    blk_e = jnp.minimum(jnp.sum((pend[None, :] <= blk_start[:, None]).astype(I32), axis=1), MOE_EXPERTS - 1)
    nused = jnp.maximum(pend[-1] // tm, 1).astype(I32).reshape(1)
    r = jnp.arange(tm, dtype=I32)[None, :]
    pad_slot = jnp.where(r < (pcounts - counts)[:, None], (pstart + counts)[:, None] + r, p_rows + eio * tm + r)
    kd = n // SC_WORKERS // SC_ROWS
    idx = jnp.concatenate([dest1.reshape(SC_WORKERS, kd, SC_ROWS), dest2.reshape(SC_WORKERS, kd, SC_ROWS),
                           pad_slot.reshape(SC_WORKERS, -1, SC_ROWS)], axis=1)
    zero_rows = jnp.zeros((SC_ROWS, HALF), U32)
    xs = _sc_dispatch(hp, idx, zero_rows, p_rows + MOE_EXPERTS * tm)
    ys = _experts(xs, blk_e, nused, w_gate, w_up, w_down, layer)
    gidx = jnp.concatenate([dest1, dest2]).reshape(SC_WORKERS, -1, SC_ROWS)
    yg = _sc_gather(ys, gidx)
    return yg, rw


def kernel(x, positions, norm_mix_g, norm_ffn_g, ret_w_in, ret_head_g, ret_w_out, conv_w_pw1, conv_b_pw1, conv_w_dw, conv_b_dw, conv_ln_g, conv_ln_b, conv_w_pw2, conv_b_pw2, moe_w_rg, moe_b_rg, moe_w_re, moe_b_re, moe_w_gate, moe_w_up, moe_w_down, final_norm_g):
    b, s, d = x.shape
    n = b * s
    xt = x.reshape(n, d)
    pos = positions.reshape(n, 1)
    fg = final_norm_g.reshape(1, d)

    q, k, v, gate = _ret_inproj(xt, pos, norm_mix_g[0].reshape(1, d), ret_w_in[0])
    xt = _ret_core(q, k, v, gate, xt, ret_head_g[0].reshape(RET_V, 1), ret_w_out[0])
    yg, rw = _moe(xt, norm_ffn_g[0].reshape(1, d), moe_w_rg[0], moe_b_rg[0], moe_w_re[0], moe_b_re[0],
                  moe_w_gate, moe_w_up, moe_w_down, 0)

    xt, u = _conv_pw1(xt, rw, yg, norm_mix_g[1].reshape(1, d), conv_w_pw1[0],
                      conv_b_pw1[0].reshape(1, 2 * d))
    xt = _conv_core(u, xt, conv_w_dw[0], conv_b_dw[0].reshape(1, d), conv_ln_g[0].reshape(1, d),
                    conv_ln_b[0].reshape(1, d), conv_w_pw2[0], conv_b_pw2[0].reshape(1, d))
    yg, rw = _moe(xt, norm_ffn_g[1].reshape(1, d), moe_w_rg[1], moe_b_rg[1], moe_w_re[1], moe_b_re[1],
                  moe_w_gate, moe_w_up, moe_w_down, 1)
    xt = _combine(yg, xt, rw, fg)
    return xt.reshape(b, s, d)
```

```python
import functools

import jax
import jax.numpy as jnp
from jax import lax
from jax.experimental import pallas as pl
from jax.experimental.pallas import tpu as pltpu
from jax.experimental.pallas import tpu_sc as plsc

F32 = jnp.float32
BF16 = jnp.bfloat16
U32 = jnp.uint32
I32 = jnp.int32

D_MODEL = 1024
RET_HEADS = 4
RET_DK = 256
RET_DV = 512
RET_QK = RET_HEADS * RET_DK
RET_V = RET_HEADS * RET_DV
ROPE_BASE = 10000.0
CONV_WIDTH = 31
MOE_GROUPS = 4
MOE_EPG = 8
MOE_EXPERTS = MOE_GROUPS * MOE_EPG
MOE_FF = 512
NORM_EPS = 1e-6

TM_PROJ = 512
RET_C = 256
RET_STEP = 512
TM_CONV = 512
CONV_HALO = 32
CONV_ROWS = 128
CONV_STRIDE = 4
T_ROUTE = 512
TM_EXP = 256
EXP_SUB = 4
EXP_SLOTS = EXP_SUB + 1
T_COMB = 512
SC_CORES = 2
SC_SUBCORES = 16
SC_WORKERS = SC_CORES * SC_SUBCORES
SC_ROWS = 64
HALF = D_MODEL // 2

VMEM_LIMIT = 56 * 1024 * 1024


def _cparams(sem, flags=None):
    return pltpu.CompilerParams(dimension_semantics=sem, vmem_limit_bytes=VMEM_LIMIT, flags=flags)


def _rms(x, g):
    ms = jnp.mean(x * x, axis=-1, keepdims=True)
    return x * lax.rsqrt(ms + NORM_EPS) * g


def _silu(x):
    return x * (1.0 / (1.0 + jnp.exp(-x)))


def _pack_bf16_pairs(y):
    lo = pltpu.bitcast(y[:, :HALF].astype(BF16).astype(F32), U32)
    hi = pltpu.bitcast(y[:, HALF:].astype(BF16).astype(F32), U32)
    return (hi & jnp.uint32(0xFFFF0000)) | (lo >> 16)


def _route_weight_columns(rw):
    t = jnp.concatenate([rw] * 16, axis=0).T
    return t[:, 0:1], t[:, 1:2]


def _unpack_bf16_pairs(p):
    lo = pltpu.bitcast(p << 16, F32)
    hi = pltpu.bitcast(p & jnp.uint32(0xFFFF0000), F32)
    return lo, hi


def _ret_inproj_kernel(x_ref, pos_ref, g_ref, inv_ref, w_ref, q_ref, k_ref, v_ref, gate_ref):
    half = RET_DK // 2
    kscale = RET_DK ** -0.5
    h = _rms(x_ref[...], g_ref[...]).astype(BF16)

    def proj(c0, width):
        return jnp.dot(h, w_ref[:, c0:c0 + width].astype(BF16), preferred_element_type=F32)

    v0 = proj(2 * RET_QK, 512)
    v_ref[:, 0:512] = v0.astype(BF16)
    zero = ((pltpu.bitcast(v0[:, 0:half], U32) >> 16) >> 16).astype(F32)
    ang = pos_ref[...].astype(F32) * inv_ref[...] + zero
    cos = jnp.cos(ang)
    sin = jnp.sin(ang)
    for j in range(RET_V // 512):
        if j > 0:
            v_ref[:, j * 512:(j + 1) * 512] = proj(2 * RET_QK + j * 512, 512).astype(BF16)
        gate_ref[:, j * 512:(j + 1) * 512] = proj(2 * RET_QK + RET_V + j * 512, 512).astype(BF16)

    for hd in range(RET_HEADS):
        for base, out_ref, cs, sn in ((0, q_ref, cos, sin), (RET_QK, k_ref, cos * kscale, sin * kscale)):
            t = proj(base + hd * RET_DK, RET_DK)
            t1 = t[:, :half]
            t2 = t[:, half:]
            out_ref[:, hd * RET_DK:hd * RET_DK + half] = (t1 * cs - t2 * sn).astype(BF16)
            out_ref[:, hd * RET_DK + half:(hd + 1) * RET_DK] = (t1 * sn + t2 * cs).astype(BF16)


def _ret_inproj(x, pos, g, w_in):
    n = x.shape[0]
    half = RET_DK // 2
    inv = (ROPE_BASE ** (-jnp.arange(half, dtype=F32) / half)).reshape(1, half)
    tm = TM_PROJ
    return pl.pallas_call(
        _ret_inproj_kernel,
        grid=(n // tm,),
        in_specs=[
            pl.BlockSpec((tm, D_MODEL), lambda i: (i, 0)),
            pl.BlockSpec((tm, 1), lambda i: (i, 0)),
            pl.BlockSpec((1, D_MODEL), lambda i: (0, 0)),
            pl.BlockSpec((1, half), lambda i: (0, 0)),
            pl.BlockSpec(w_in.shape, lambda i: (0, 0), pipeline_mode=pl.Buffered(1)),
        ],
        out_specs=[
            pl.BlockSpec((tm, RET_QK), lambda i: (i, 0)),
            pl.BlockSpec((tm, RET_QK), lambda i: (i, 0)),
            pl.BlockSpec((tm, RET_V), lambda i: (i, 0)),
            pl.BlockSpec((tm, RET_V), lambda i: (i, 0)),
        ],
        out_shape=[
            jax.ShapeDtypeStruct((n, RET_QK), BF16),
            jax.ShapeDtypeStruct((n, RET_QK), BF16),
            jax.ShapeDtypeStruct((n, RET_V), BF16),
            jax.ShapeDtypeStruct((n, RET_V), BF16),
        ],
        compiler_params=_cparams(("arbitrary",)),
        name="ret_inproj",
    )(x, pos, g, inv, w_in)


def _ret_core_kernel(cdec_ref, q_ref, k_ref, v_ref, gate_ref, x_ref, hg_ref, intra_ref, eps_ref,
                     kdec_ref, wo_ref, out_ref, state_ref, y_ref, wos_ref):
    @pl.when(pl.program_id(0) == 0)
    def _():
        state_ref[...] = jnp.zeros_like(state_ref)
        wos_ref[...] = (wo_ref[...] * hg_ref[...]).astype(BF16)

    for r0 in range(0, RET_STEP, RET_C):
        rs = slice(r0, r0 + RET_C)
        for hd in range(RET_HEADS):
            q = q_ref[rs, hd * RET_DK:(hd + 1) * RET_DK]
            k = k_ref[rs, hd * RET_DK:(hd + 1) * RET_DK]
            v = v_ref[rs, hd * RET_DV:(hd + 1) * RET_DV]
            state = state_ref[hd]
            scores = lax.dot_general(q, k, (((1,), (1,)), ((), ())), preferred_element_type=F32)
            scores = (scores * intra_ref[hd]).astype(BF16)
            o = (jnp.dot(scores, v, preferred_element_type=F32)
                 + jnp.dot(q, state.astype(BF16), preferred_element_type=F32))
            kdec = kdec_ref[hd]
            kd = (k.astype(F32) * jnp.concatenate([kdec] * (RET_DK // 128), axis=1)).astype(BF16)
            upd = lax.dot_general(kd, v, (((0,), (0,)), ((), ())), preferred_element_type=F32)
            state_ref[hd] = state * cdec_ref[hd] + upd
            ms = jnp.mean(o * o, axis=-1, keepdims=True)
            on = o * lax.rsqrt(ms + eps_ref[hd][:, 0:1])
            gt = gate_ref[rs, hd * RET_DV:(hd + 1) * RET_DV].astype(F32)
            y_ref[rs, hd * RET_DV:(hd + 1) * RET_DV] = (_silu(gt) * on).astype(BF16)
        out_ref[rs, :] = x_ref[rs, :] + jnp.dot(y_ref[rs, :], wos_ref[...], preferred_element_type=F32)


def _ret_core(q, k, v, gate, x, head_g_col, w_out):
    n = x.shape[0]
    c = RET_C
    log_gamma = jnp.log1p(-(2.0 ** (-5.0 - jnp.arange(RET_HEADS, dtype=F32))))
    idx = jnp.arange(c, dtype=F32)
    diff = idx[:, None] - idx[None, :]
    intra = jnp.where(diff >= 0, jnp.exp(-log_gamma[:, None, None] * (idx[None, None, :] + 1.0)), 0.0)
    eps_row = NORM_EPS * jnp.exp(-2.0 * log_gamma[:, None] * (idx + 1.0))
    eps_row = jnp.broadcast_to(eps_row[:, :, None], (RET_HEADS, c, 128))
    kdec = jnp.broadcast_to(jnp.exp(log_gamma[:, None] * (c - 1.0 - idx))[:, :, None], (RET_HEADS, c, 128))
    cdec = jnp.exp(log_gamma * c)
    return pl.pallas_call(
        _ret_core_kernel,
        grid=(n // RET_STEP,),
        in_specs=[
            pl.BlockSpec(memory_space=pltpu.SMEM),
            pl.BlockSpec((RET_STEP, RET_QK), lambda i: (i, 0)),
            pl.BlockSpec((RET_STEP, RET_QK), lambda i: (i, 0)),
            pl.BlockSpec((RET_STEP, RET_V), lambda i: (i, 0)),
            pl.BlockSpec((RET_STEP, RET_V), lambda i: (i, 0)),
            pl.BlockSpec((RET_STEP, D_MODEL), lambda i: (i, 0)),
            pl.BlockSpec((RET_V, 1), lambda i: (0, 0)),
            pl.BlockSpec((RET_HEADS, c, c), lambda i: (0, 0, 0)),
            pl.BlockSpec((RET_HEADS, c, 128), lambda i: (0, 0, 0)),
            pl.BlockSpec((RET_HEADS, c, 128), lambda i: (0, 0, 0)),
            pl.BlockSpec((RET_V, D_MODEL), lambda i: (0, 0)),
        ],
        out_specs=pl.BlockSpec((RET_STEP, D_MODEL), lambda i: (i, 0)),
        out_shape=jax.ShapeDtypeStruct((n, D_MODEL), F32),
        scratch_shapes=[
            pltpu.VMEM((RET_HEADS, RET_DK, RET_DV), F32),
            pltpu.VMEM((RET_STEP, RET_V), BF16),
            pltpu.VMEM((RET_V, D_MODEL), BF16),
        ],
        compiler_params=_cparams(("arbitrary",)),
        name="ret_core",
    )(cdec, q, k, v, gate, x, head_g_col, intra, eps_row, kdec, w_out)


def _conv_pw1_kernel(x_ref, rw_ref, y1_ref, y2_ref, g_ref, w_ref, b_ref, xo_ref, u_ref):
    r = TM_CONV // 2
    w1, w2 = _route_weight_columns(rw_ref[...])

    def prologue(rs):
        lo1, hi1 = _unpack_bf16_pairs(y1_ref[rs, :])
        lo2, hi2 = _unpack_bf16_pairs(y2_ref[rs, :])
        x = jnp.concatenate([x_ref[rs, :HALF] + w1[rs] * lo1 + w2[rs] * lo2,
                             x_ref[rs, HALF:] + w1[rs] * hi1 + w2[rs] * hi2], axis=1)
        xo_ref[rs, :] = x
        hf = _rms(x, g_ref[...])
        return hf.astype(BF16), hf[:, 0:128]

    groups = [slice(0, r), slice(r, 2 * r)]
    hs = [prologue(rs) for rs in groups]
    zero = ((pltpu.bitcast(hs[1][1], U32) >> 16) >> 16).astype(F32)
    for j in range(D_MODEL // 512):
        wa = w_ref[:, j * 512:(j + 1) * 512].astype(BF16)
        wg = w_ref[:, D_MODEL + j * 512:D_MODEL + (j + 1) * 512].astype(BF16)
        for gi, rs in enumerate(groups):
            h = hs[gi][0]
            a = jnp.dot(h, wa, preferred_element_type=F32) + b_ref[:, j * 512:(j + 1) * 512]
            gt = jnp.dot(h, wg, preferred_element_type=F32) + b_ref[:, D_MODEL + j * 512:D_MODEL + (j + 1) * 512]
            u = a * (1.0 / (1.0 + jnp.exp(-gt)))
            if gi == 0 and j == 0:
                u_ref[rs, 0:128] = u[:, 0:128] + zero
                u_ref[rs, 128:512] = u[:, 128:512]
            else:
                u_ref[rs, j * 512:(j + 1) * 512] = u


def _conv_pw1(x, rw, yg, g, w, b):
    n = x.shape[0]
    tm = TM_CONV
    nb = n // tm
    return pl.pallas_call(
        _conv_pw1_kernel,
        grid=(nb,),
        in_specs=[
            pl.BlockSpec((tm, D_MODEL), lambda i: (i, 0)),
            pl.BlockSpec((8, tm), lambda i: (0, i)),
            pl.BlockSpec((tm, HALF), lambda i: (i, 0)),
            pl.BlockSpec((tm, HALF), lambda i: (i + nb, 0)),
            pl.BlockSpec((1, D_MODEL), lambda i: (0, 0)),
            pl.BlockSpec((D_MODEL, 2 * D_MODEL), lambda i: (0, 0)),
            pl.BlockSpec((1, 2 * D_MODEL), lambda i: (0, 0)),
        ],
        out_specs=[pl.BlockSpec((tm, D_MODEL), lambda i: (i, 0)), pl.BlockSpec((tm, D_MODEL), lambda i: (i, 0))],
        out_shape=[jax.ShapeDtypeStruct((n, D_MODEL), F32), jax.ShapeDtypeStruct((n, D_MODEL), F32)],
        compiler_params=_cparams(("arbitrary",)),
        name="conv_pw1",
    )(x, rw, yg, yg, g, w, b)


def _conv_core_kernel(u_ref, halo_ref, x_ref, wdw_ref, bdw_ref, lng_ref, lnb_ref, w2_ref, b2_ref,
                      out_ref, win_ref, z_ref):
    tm = TM_CONV
    first = pl.program_id(0) == 0
    halo = halo_ref[...]
    halo = jnp.where(first, jnp.zeros_like(halo), halo)
    nslab = D_MODEL // 128
    for cc in range(nslab):
        cs = slice(cc * 128, (cc + 1) * 128)
        win_ref[cc, 0:CONV_HALO, :] = halo[:, cs]
        win_ref[cc, CONV_HALO:CONV_HALO + tm, :] = u_ref[:, cs]
    off = CONV_HALO - (CONV_WIDTH - 1)
    rb = CONV_ROWS
    st = CONV_STRIDE
    for cc in range(nslab):
        cs = slice(cc * 128, (cc + 1) * 128)
        for r0 in range(0, tm, rb):
            accs = [bdw_ref[:, cs]] * st
            for o in range(CONV_WIDTH):
                w_o = wdw_ref[o:o + 1, cs]
                for rho in range(st):
                    accs[rho] = accs[rho] + win_ref[cc, pl.ds(r0 + rho + off + o, rb // st, stride=st), :] * w_o
            for rho in range(st):
                z_ref[cc, pl.ds(r0 + rho, rb // st, stride=st), :] = accs[rho]
    z = jnp.concatenate([z_ref[cc] for cc in range(nslab)], axis=1)
    mu = jnp.mean(z, axis=-1, keepdims=True)
    zc = z - mu
    var = jnp.mean(zc * zc, axis=-1, keepdims=True)
    zn = zc * lax.rsqrt(var + NORM_EPS) * lng_ref[...] + lnb_ref[...]
    y = _silu(zn).astype(BF16)
    out_ref[...] = x_ref[...] + jnp.dot(y, w2_ref[...].astype(BF16), preferred_element_type=F32) + b2_ref[...]


def _conv_core(u, x, w_dw, b_dw, ln_g, ln_b, w2, b2):
    n = x.shape[0]
    tm = TM_CONV
    r = tm // CONV_HALO
    wdw_pad = jnp.zeros((32, D_MODEL), F32).at[:CONV_WIDTH].set(w_dw)
    return pl.pallas_call(
        _conv_core_kernel,
        grid=(n // tm,),
        in_specs=[
            pl.BlockSpec((tm, D_MODEL), lambda i: (i, 0)),
            pl.BlockSpec((CONV_HALO, D_MODEL), lambda i: (jnp.maximum(i * r - 1, 0), 0)),
            pl.BlockSpec((tm, D_MODEL), lambda i: (i, 0)),
            pl.BlockSpec((32, D_MODEL), lambda i: (0, 0)),
            pl.BlockSpec((1, D_MODEL), lambda i: (0, 0)),
            pl.BlockSpec((1, D_MODEL), lambda i: (0, 0)),
            pl.BlockSpec((1, D_MODEL), lambda i: (0, 0)),
            pl.BlockSpec((D_MODEL, D_MODEL), lambda i: (0, 0)),
            pl.BlockSpec((1, D_MODEL), lambda i: (0, 0)),
        ],
        out_specs=pl.BlockSpec((tm, D_MODEL), lambda i: (i, 0)),
        out_shape=jax.ShapeDtypeStruct((n, D_MODEL), F32),
        scratch_shapes=[pltpu.VMEM((D_MODEL // 128, CONV_HALO + tm, 128), F32),
                        pltpu.VMEM((D_MODEL // 128, tm, 128), F32)],
        compiler_params=_cparams(("arbitrary",)),
        name="conv_core",
    )(u, u, x, wdw_pad, b_dw, ln_g, ln_b, w2, b2)


def _router_kernel(x_ref, g_ref, wr_ref, br_ref, hp_ref, ri_ref, rw_ref, cnt_ref, carry_ref):
    t = T_ROUTE

    @pl.when(pl.program_id(0) == 0)
    def _():
        carry_ref[...] = jnp.zeros_like(carry_ref)

    h = _rms(x_ref[...], g_ref[...])
    hp_ref[...] = _pack_bf16_pairs(h)
    h_hi = h.astype(BF16)
    h_lo = (h - h_hi.astype(F32)).astype(BF16)
    w = wr_ref[...]
    w_hi = w.astype(BF16)
    w_lo = (w - w_hi.astype(F32)).astype(BF16)
    dn = (((1,), (1,)), ((), ()))
    p = lax.dot_general(jnp.concatenate([w_hi, w_lo], axis=0), h_hi, dn, preferred_element_type=F32)
    nr = wr_ref.shape[0]
    logits = p[0:nr] + p[nr:2 * nr] + lax.dot_general(w_hi, h_lo, dn, preferred_element_type=F32)
    logits = logits + br_ref[:, 0:1]

    best = logits[0:1]
    gi = jnp.zeros((1, t), I32)
    for j in range(1, MOE_GROUPS):
        r = logits[j:j + 1]
        up = r > best
        gi = jnp.where(up, j, gi)
        best = jnp.where(up, r, best)
    den = jnp.zeros((1, t), F32)
    for j in range(MOE_GROUPS):
        den = den + jnp.exp(logits[j:j + 1] - best)
    gate_g = 1.0 / den

    sel = logits[8:8 + MOE_EPG]
    for j in range(1, MOE_GROUPS):
        sel = jnp.where(gi == j, logits[8 + j * MOE_EPG:8 + (j + 1) * MOE_EPG], sel)

    m1 = sel[0:1]
    i1 = jnp.zeros((1, t), I32)
    for j in range(1, MOE_EPG):
        r = sel[j:j + 1]
        up = r > m1
        i1 = jnp.where(up, j, i1)
        m1 = jnp.where(up, r, m1)
    m2 = jnp.full((1, t), -jnp.inf, F32)
    i2 = jnp.zeros((1, t), I32)
    started = jnp.zeros((1, t), jnp.bool_)
    for j in range(MOE_EPG):
        r = sel[j:j + 1]
        ok = i1 != j
        up = ok & ((r > m2) | jnp.logical_not(started))
        i2 = jnp.where(up, j, i2)
        m2 = jnp.where(up, r, m2)
        started = started | ok
    e21 = jnp.exp(m2 - m1)
    p1 = 1.0 / (1.0 + e21)
    w1 = gate_g * p1
    w2 = gate_g * (e21 * p1)
    eid1 = gi * MOE_EPG + i1
    eid2 = gi * MOE_EPG + i2

    eio = lax.broadcasted_iota(I32, (MOE_EXPERTS, t), 0)
    oh1 = eio == eid1
    oh2 = eio == eid2
    oh = (oh1 | oh2).astype(F32)
    rio = lax.broadcasted_iota(I32, (t, t), 0)
    cio = lax.broadcasted_iota(I32, (t, t), 1)
    upper = (rio < cio).astype(BF16)
    cum = jnp.dot(oh.astype(BF16), upper, preferred_element_type=F32) + carry_ref[:, 0:1]
    rank1 = jnp.sum(jnp.where(oh1, cum, 0.0), axis=0, keepdims=True)
    rank2 = jnp.sum(jnp.where(oh2, cum, 0.0), axis=0, keepdims=True)
    carry_ref[...] = carry_ref[...] + jnp.sum(oh, axis=1, keepdims=True)
    cnt_ref[...] = carry_ref[...]

    zi = jnp.zeros((4, t), I32)
    ri_ref[...] = jnp.concatenate([eid1, eid2, rank1.astype(I32), rank2.astype(I32), zi], axis=0)
    zf = jnp.zeros((6, t), F32)
    rw_ref[...] = jnp.concatenate([w1, w2, zf], axis=0)


def _router(x, g, w_rg, b_rg, w_re, b_re):
    n = x.shape[0]
    t = T_ROUTE
    wr = jnp.zeros((40, D_MODEL), F32).at[0:MOE_GROUPS].set(w_rg.T).at[8:40].set(w_re.T)
    br = jnp.zeros((40,), F32).at[0:MOE_GROUPS].set(b_rg).at[8:40].set(b_re)
    br = jnp.broadcast_to(br[:, None], (40, 128))
    return pl.pallas_call(
        _router_kernel,
        grid=(n // t,),
        in_specs=[
            pl.BlockSpec((t, D_MODEL), lambda i: (i, 0)),
            pl.BlockSpec((1, D_MODEL), lambda i: (0, 0)),
            pl.BlockSpec((40, D_MODEL), lambda i: (0, 0)),
            pl.BlockSpec((40, 128), lambda i: (0, 0)),
        ],
        out_specs=[
            pl.BlockSpec((t, HALF), lambda i: (i, 0)),
            pl.BlockSpec((8, t), lambda i: (0, i)),
            pl.BlockSpec((8, t), lambda i: (0, i)),
            pl.BlockSpec((MOE_EXPERTS, 128), lambda i: (0, 0)),
        ],
        out_shape=[
            jax.ShapeDtypeStruct((n, HALF), U32),
            jax.ShapeDtypeStruct((8, n), I32),
            jax.ShapeDtypeStruct((8, n), F32),
            jax.ShapeDtypeStruct((MOE_EXPERTS, 128), F32),
        ],
        scratch_shapes=[pltpu.VMEM((MOE_EXPERTS, 128), F32)],
        compiler_params=_cparams(("arbitrary",)),
        name="moe_router",
    )(x, g, wr, br)


def _sc_mesh():
    return plsc.VectorSubcoreMesh(core_axis_name="c", subcore_axis_name="s",
                                  num_cores=SC_CORES, num_subcores=SC_SUBCORES)


def _sc_worker_id():
    return lax.axis_index("s") * SC_CORES + lax.axis_index("c")


def _sc_dispatch(hp, idx, zero_rows, total_rows):
    n = hp.shape[0]
    tpw = n // SC_WORKERS
    kd = tpw // SC_ROWS
    kp = idx.shape[1] - 2 * kd

    @functools.partial(
        pl.kernel, mesh=_sc_mesh(),
        out_type=jax.ShapeDtypeStruct((total_rows, HALF), U32),
        scratch_types=[
            pltpu.VMEM((2 * kd + kp, SC_ROWS), I32),
            pltpu.VMEM((SC_ROWS, HALF), U32), pltpu.VMEM((SC_ROWS, HALF), U32), pltpu.VMEM((SC_ROWS, HALF), U32),
            pltpu.SemaphoreType.DMA((2,)), pltpu.SemaphoreType.DMA((2,)), pltpu.SemaphoreType.DMA,
        ],
        name="moe_dispatch_sc",
    )
    def k(hp_hbm, idx_hbm, zero_hbm, xs_hbm, idx_v, buf0, buf1, zbuf, load_sem, scat_sem, pad_sem):
        wid = _sc_worker_id()
        bufs = (buf0, buf1)
        pltpu.sync_copy(idx_hbm.at[wid], idx_v)

        def load(c):
            return pltpu.make_async_copy(hp_hbm.at[pl.ds(wid * tpw + c * SC_ROWS, SC_ROWS)], bufs[c % 2],
                                         load_sem.at[c % 2])

        def scatters(c):
            return (pltpu.make_async_copy(bufs[c % 2], xs_hbm.at[idx_v.at[c]], scat_sem.at[c % 2]),
                    pltpu.make_async_copy(bufs[c % 2], xs_hbm.at[idx_v.at[kd + c]], scat_sem.at[c % 2]))

        load(0).start()
        pltpu.sync_copy(zero_hbm, zbuf)
        pads = [pltpu.make_async_copy(zbuf, xs_hbm.at[idx_v.at[2 * kd + j]], pad_sem) for j in range(kp)]
        for p in pads:
            p.start()
        for c in range(kd):
            load(c).wait()
            for d in scatters(c):
                d.start()
            if c + 1 < kd:
                if c >= 1:
                    for d in scatters(c - 1):
                        d.wait()
                load(c + 1).start()
        for c in range(max(kd - 2, 0), kd):
            for d in scatters(c):
                d.wait()
        for p in pads:
            p.wait()

    return k(hp, idx, zero_rows)


def _sc_gather(ys, idx):
    kg = idx.shape[1]
    rows_per_worker = kg * SC_ROWS

    @functools.partial(
        pl.kernel, mesh=_sc_mesh(),
        out_type=jax.ShapeDtypeStruct((SC_WORKERS * rows_per_worker, HALF), U32),
        scratch_types=[
            pltpu.VMEM((kg, SC_ROWS), I32),
            pltpu.VMEM((SC_ROWS, HALF), U32), pltpu.VMEM((SC_ROWS, HALF), U32),
            pltpu.SemaphoreType.DMA((2,)), pltpu.SemaphoreType.DMA((2,)),
        ],
        name="moe_gather_sc",
    )
    def k(ys_hbm, idx_hbm, yg_hbm, idx_v, buf0, buf1, gat_sem, out_sem):
        wid = _sc_worker_id()
        bufs = (buf0, buf1)
        pltpu.sync_copy(idx_hbm.at[wid], idx_v)

        def gather(c):
            return pltpu.make_async_copy(ys_hbm.at[idx_v.at[c]], bufs[c % 2], gat_sem.at[c % 2])

        def store(c):
            return pltpu.make_async_copy(bufs[c % 2],
                                         yg_hbm.at[pl.ds(wid * rows_per_worker + c * SC_ROWS, SC_ROWS)],
                                         out_sem.at[c % 2])

        gather(0).start()
        for c in range(kg):
            if c + 1 < kg:
                if c >= 1:
                    store(c - 1).wait()
                gather(c + 1).start()
            gather(c).wait()
            store(c).start()
        for c in range(max(kg - 2, 0), kg):
            store(c).wait()

    return k(ys, idx)


def _expert_kernel(blk_e_ref, nused_ref, first_ref, slot_ref, nxt_ref, xs_ref, wg_hbm, wu_hbm, wd_hbm,
                   ys_ref, wg_buf, wu_buf, wd_buf, sems, *, layer):
    step = pl.program_id(0)
    tm = TM_EXP
    nused = nused_ref[0]

    def weight_copies(e, s):
        return (pltpu.make_async_copy(wg_hbm.at[layer, e], wg_buf.at[s], sems.at[s, 0]),
                pltpu.make_async_copy(wu_hbm.at[layer, e], wu_buf.at[s], sems.at[s, 1]),
                pltpu.make_async_copy(wd_hbm.at[layer, e], wd_buf.at[s], sems.at[s, 2]))

    def dma_control(j):
        i = step * EXP_SUB + j

        @pl.when(i < nused)
        def _():
            s = slot_ref[i]

            if j == 0:
                @pl.when(i == 0)
                def _():
                    for c in weight_copies(blk_e_ref[0], 0):
                        c.start()

            @pl.when(first_ref[i] == 1)
            def _():
                for c in weight_copies(blk_e_ref[i], s):
                    c.wait()

                @pl.when(nxt_ref[i] >= 0)
                def _():
                    for c in weight_copies(nxt_ref[i], lax.rem(s + 1, EXP_SLOTS)):
                        c.start()

    for j in range(EXP_SUB):
        dma_control(j)

    @pl.when(step * EXP_SUB < nused)
    def _():
        for j in range(EXP_SUB):
            i = step * EXP_SUB + j
            rows = slice(j * tm, (j + 1) * tm)
            s = slot_ref[i]
            lo, hi = _unpack_bf16_pairs(xs_ref[rows, :])
            xf = jnp.concatenate([lo, hi], axis=1)
            a = jnp.dot(xf, wg_buf[s], preferred_element_type=F32)
            b = jnp.dot(xf, wu_buf[s], preferred_element_type=F32)
            hm = _silu(a) * b
            y = jnp.dot(hm, wd_buf[s], preferred_element_type=F32)
            ys_ref[rows, :] = jnp.where(i < nused, _pack_bf16_pairs(y), jnp.uint32(0))

    @pl.when(step * EXP_SUB >= nused)
    def _():
        ys_ref[...] = jnp.zeros_like(ys_ref)


def _experts(xs, blk_e, nused, w_gate, w_up, w_down, layer):
    tm = TM_EXP
    p_rows = xs.shape[0] - MOE_EXPERTS * tm
    nblk = p_rows // tm
    pos = jnp.arange(nblk, dtype=I32)
    valid = pos < nused[0]
    prev_e = jnp.concatenate([jnp.full((1,), -1, I32), blk_e[:-1]])
    first = valid & (blk_e != prev_e)
    slot = jnp.maximum(jnp.cumsum(first.astype(I32)) - 1, 0) % EXP_SLOTS
    first_pos = jnp.where(first, pos, nblk)
    next_first = jnp.concatenate([lax.cummin(first_pos, reverse=True)[1:], jnp.full((1,), nblk, I32)])
    nxt = jnp.where(next_first < nblk, blk_e[jnp.minimum(next_first, nblk - 1)], -1)

    def blk(i, be, nu, *_):
        return jnp.minimum(i, (nu[0] - 1) // EXP_SUB)

    grid_spec = pltpu.PrefetchScalarGridSpec(
        num_scalar_prefetch=5,
        grid=(nblk // EXP_SUB,),
        in_specs=[
            pl.BlockSpec((EXP_SUB * tm, HALF), lambda i, *sp: (blk(i, *sp), 0)),
            pl.BlockSpec(memory_space=pl.ANY),
            pl.BlockSpec(memory_space=pl.ANY),
            pl.BlockSpec(memory_space=pl.ANY),
        ],
        out_specs=pl.BlockSpec((EXP_SUB * tm, HALF), lambda i, *sp: (i, 0)),
        scratch_shapes=[
            pltpu.VMEM((EXP_SLOTS, D_MODEL, MOE_FF), F32),
            pltpu.VMEM((EXP_SLOTS, D_MODEL, MOE_FF), F32),
            pltpu.VMEM((EXP_SLOTS, MOE_FF, D_MODEL), F32),
            pltpu.SemaphoreType.DMA((EXP_SLOTS, 3)),
        ],
    )
    return pl.pallas_call(
        functools.partial(_expert_kernel, layer=layer),
        grid_spec=grid_spec,
        out_shape=jax.ShapeDtypeStruct((p_rows, HALF), U32),
        compiler_params=_cparams(("arbitrary",)),
        name="moe_experts",
    )(blk_e, nused, first.astype(I32), slot.astype(I32), nxt.astype(I32), xs, w_gate, w_up, w_down)


def _combine_kernel(x_ref, rw_ref, fg_ref, y1_ref, y2_ref, out_ref):
    lo1, hi1 = _unpack_bf16_pairs(y1_ref[...])
    lo2, hi2 = _unpack_bf16_pairs(y2_ref[...])
    w1, w2 = _route_weight_columns(rw_ref[...])
    x = x_ref[...]
    o_lo = x[:, :HALF] + w1 * lo1 + w2 * lo2
    o_hi = x[:, HALF:] + w1 * hi1 + w2 * hi2
    ms = (jnp.sum(o_lo * o_lo, axis=-1, keepdims=True)
          + jnp.sum(o_hi * o_hi, axis=-1, keepdims=True)) * (1.0 / D_MODEL)
    sc = lax.rsqrt(ms + NORM_EPS)
    o_lo = o_lo * sc * fg_ref[:, :HALF]
    o_hi = o_hi * sc * fg_ref[:, HALF:]
    out_ref[:, :HALF] = o_lo
    out_ref[:, HALF:] = o_hi


def _combine(yg, x, rw, final_g):
    n = x.shape[0]
    td = T_COMB
    nb = n // td
    return pl.pallas_call(
        _combine_kernel,
        grid=(nb,),
        in_specs=[
            pl.BlockSpec((td, D_MODEL), lambda i: (i, 0)),
            pl.BlockSpec((8, td), lambda i: (0, i)),
            pl.BlockSpec((1, D_MODEL), lambda i: (0, 0)),
            pl.BlockSpec((td, HALF), lambda i: (i, 0)),
            pl.BlockSpec((td, HALF), lambda i: (i + nb, 0)),
        ],
        out_specs=pl.BlockSpec((td, D_MODEL), lambda i: (i, 0)),
        out_shape=jax.ShapeDtypeStruct((n, D_MODEL), F32),
        compiler_params=_cparams(("arbitrary",)),
        name="moe_combine",
    )(x, rw, final_g, yg, yg)


def _moe(x, g, w_rg, b_rg, w_re, b_re, w_gate, w_up, w_down, layer):
    n = x.shape[0]
    tm = TM_EXP
    p_rows = 2 * n + MOE_EXPERTS * tm
    nblk = p_rows // tm
    hp, ri, rw, cnt = _router(x, g, w_rg, b_rg, w_re, b_re)
    counts = cnt[:, 0].astype(I32)
    pcounts = (counts + tm - 1) // tm * tm
    pend = jnp.cumsum(pcounts)
    pstart = pend - pcounts
    eio = jnp.arange(MOE_EXPERTS, dtype=I32)[:, None]
    dest1 = jnp.sum(jnp.where(ri[0][None, :] == eio, pstart[:, None], 0), axis=0) + ri[2]
    dest2 = jnp.sum(jnp.where(ri[1][None, :] == eio, pstart[:, None], 0), axis=0) + ri[3]
    blk_start = jnp.arange(nblk, dtype=I32) * tm
    blk_e = jnp.minimum(jnp.sum((pend[None, :] <= blk_start[:, None]).astype(I32), axis=1), MOE_EXPERTS - 1)
    nused = jnp.maximum(pend[-1] // tm, 1).astype(I32).reshape(1)
    r = jnp.arange(tm, dtype=I32)[None, :]
    pad_slot = jnp.where(r < (pcounts - counts)[:, None], (pstart + counts)[:, None] + r, p_rows + eio * tm + r)
    kd = n // SC_WORKERS // SC_ROWS
    idx = jnp.concatenate([dest1.reshape(SC_WORKERS, kd, SC_ROWS), dest2.reshape(SC_WORKERS, kd, SC_ROWS),
                           pad_slot.reshape(SC_WORKERS, -1, SC_ROWS)], axis=1)
    zero_rows = jnp.zeros((SC_ROWS, HALF), U32)
    xs = _sc_dispatch(hp, idx, zero_rows, p_rows + MOE_EXPERTS * tm)
    ys = _experts(xs, blk_e, nused, w_gate, w_up, w_down, layer)
    gidx = jnp.concatenate([dest1, dest2]).reshape(SC_WORKERS, -1, SC_ROWS)
    yg = _sc_gather(ys, gidx)
    return yg, rw


def kernel(x, positions, norm_mix_g, norm_ffn_g, ret_w_in, ret_head_g, ret_w_out, conv_w_pw1, conv_b_pw1, conv_w_dw, conv_b_dw, conv_ln_g, conv_ln_b, conv_w_pw2, conv_b_pw2, moe_w_rg, moe_b_rg, moe_w_re, moe_b_re, moe_w_gate, moe_w_up, moe_w_down, final_norm_g):
    b, s, d = x.shape
    n = b * s
    xt = x.reshape(n, d)
    pos = positions.reshape(n, 1)
    fg = final_norm_g.reshape(1, d)

    q, k, v, gate = _ret_inproj(xt, pos, norm_mix_g[0].reshape(1, d), ret_w_in[0])
    xt = _ret_core(q, k, v, gate, xt, ret_head_g[0].reshape(RET_V, 1), ret_w_out[0])
    yg, rw = _moe(xt, norm_ffn_g[0].reshape(1, d), moe_w_rg[0], moe_b_rg[0], moe_w_re[0], moe_b_re[0],
                  moe_w_gate, moe_w_up, moe_w_down, 0)

    xt, u = _conv_pw1(xt, rw, yg, norm_mix_g[1].reshape(1, d), conv_w_pw1[0],
                      conv_b_pw1[0].reshape(1, 2 * d))
    xt = _conv_core(u, xt, conv_w_dw[0], conv_b_dw[0].reshape(1, d), conv_ln_g[0].reshape(1, d),
                    conv_ln_b[0].reshape(1, d), conv_w_pw2[0], conv_b_pw2[0].reshape(1, d))
    yg, rw = _moe(xt, norm_ffn_g[1].reshape(1, d), moe_w_rg[1], moe_b_rg[1], moe_w_re[1], moe_b_re[1],
                  moe_w_gate, moe_w_up, moe_w_down, 1)
    xt = _combine(yg, xt, rw, fg)
    return xt.reshape(b, s, d)
```

```python
import functools

import jax
import jax.numpy as jnp
from jax import lax
from jax.experimental import pallas as pl
from jax.experimental.pallas import tpu as pltpu
from jax.experimental.pallas import tpu_sc as plsc

F32 = jnp.float32
BF16 = jnp.bfloat16
U32 = jnp.uint32
I32 = jnp.int32

D_MODEL = 1024
RET_HEADS = 4
RET_DK = 256
RET_DV = 512
RET_QK = RET_HEADS * RET_DK
RET_V = RET_HEADS * RET_DV
ROPE_BASE = 10000.0
CONV_WIDTH = 31
MOE_GROUPS = 4
MOE_EPG = 8
MOE_EXPERTS = MOE_GROUPS * MOE_EPG
MOE_FF = 512
NORM_EPS = 1e-6

TM_PROJ = 512
RET_C = 256
RET_STEP = 512
TM_CONV = 512
CONV_HALO = 32
CONV_ROWS = 128
CONV_STRIDE = 4
T_ROUTE = 512
TM_EXP = 256
EXP_SUB = 4
EXP_SLOTS = EXP_SUB + 1
T_COMB = 512
SC_CORES = 2
SC_SUBCORES = 16
SC_WORKERS = SC_CORES * SC_SUBCORES
SC_ROWS = 32
SC_NBUF = 4
HALF = D_MODEL // 2

VMEM_LIMIT = 56 * 1024 * 1024


def _cparams(sem, flags=None):
    return pltpu.CompilerParams(dimension_semantics=sem, vmem_limit_bytes=VMEM_LIMIT, flags=flags)


def _rms(x, g):
    ms = jnp.mean(x * x, axis=-1, keepdims=True)
    return x * lax.rsqrt(ms + NORM_EPS) * g


def _silu(x):
    return x * (1.0 / (1.0 + jnp.exp(-x)))


def _pack_bf16_pairs(y):
    lo = pltpu.bitcast(y[:, :HALF].astype(BF16).astype(F32), U32)
    hi = pltpu.bitcast(y[:, HALF:].astype(BF16).astype(F32), U32)
    return (hi & jnp.uint32(0xFFFF0000)) | (lo >> 16)


def _route_weight_columns(rw):
    t = jnp.concatenate([rw] * 16, axis=0).T
    return t[:, 0:1], t[:, 1:2]


def _unpack_bf16_pairs(p):
    lo = pltpu.bitcast(p << 16, F32)
    hi = pltpu.bitcast(p & jnp.uint32(0xFFFF0000), F32)
    return lo, hi


def _ret_inproj_kernel(x_ref, pos_ref, g_ref, inv_ref, w_ref, q_ref, k_ref, v_ref, gate_ref):
    half = RET_DK // 2
    kscale = RET_DK ** -0.5
    h = _rms(x_ref[...], g_ref[...]).astype(BF16)

    def proj(c0, width):
        return jnp.dot(h, w_ref[:, c0:c0 + width].astype(BF16), preferred_element_type=F32)

    v0 = proj(2 * RET_QK, 512)
    v_ref[:, 0:512] = v0.astype(BF16)
    zero = ((pltpu.bitcast(v0[:, 0:half], U32) >> 16) >> 16).astype(F32)
    ang = pos_ref[...].astype(F32) * inv_ref[...] + zero
    cos = jnp.cos(ang)
    sin = jnp.sin(ang)
    for j in range(RET_V // 512):
        if j > 0:
            v_ref[:, j * 512:(j + 1) * 512] = proj(2 * RET_QK + j * 512, 512).astype(BF16)
        gate_ref[:, j * 512:(j + 1) * 512] = proj(2 * RET_QK + RET_V + j * 512, 512).astype(BF16)

    for hd in range(RET_HEADS):
        for base, out_ref, cs, sn in ((0, q_ref, cos, sin), (RET_QK, k_ref, cos * kscale, sin * kscale)):
            t = proj(base + hd * RET_DK, RET_DK)
            t1 = t[:, :half]
            t2 = t[:, half:]
            out_ref[:, hd * RET_DK:hd * RET_DK + half] = (t1 * cs - t2 * sn).astype(BF16)
            out_ref[:, hd * RET_DK + half:(hd + 1) * RET_DK] = (t1 * sn + t2 * cs).astype(BF16)


def _ret_inproj(x, pos, g, w_in):
    n = x.shape[0]
    half = RET_DK // 2
    inv = (ROPE_BASE ** (-jnp.arange(half, dtype=F32) / half)).reshape(1, half)
    tm = TM_PROJ
    return pl.pallas_call(
        _ret_inproj_kernel,
        grid=(n // tm,),
        in_specs=[
            pl.BlockSpec((tm, D_MODEL), lambda i: (i, 0)),
            pl.BlockSpec((tm, 1), lambda i: (i, 0)),
            pl.BlockSpec((1, D_MODEL), lambda i: (0, 0)),
            pl.BlockSpec((1, half), lambda i: (0, 0)),
            pl.BlockSpec(w_in.shape, lambda i: (0, 0), pipeline_mode=pl.Buffered(1)),
        ],
        out_specs=[
            pl.BlockSpec((tm, RET_QK), lambda i: (i, 0)),
            pl.BlockSpec((tm, RET_QK), lambda i: (i, 0)),
            pl.BlockSpec((tm, RET_V), lambda i: (i, 0)),
            pl.BlockSpec((tm, RET_V), lambda i: (i, 0)),
        ],
        out_shape=[
            jax.ShapeDtypeStruct((n, RET_QK), BF16),
            jax.ShapeDtypeStruct((n, RET_QK), BF16),
            jax.ShapeDtypeStruct((n, RET_V), BF16),
            jax.ShapeDtypeStruct((n, RET_V), BF16),
        ],
        compiler_params=_cparams(("arbitrary",)),
        name="ret_inproj",
    )(x, pos, g, inv, w_in)


def _ret_core_kernel(cdec_ref, q_ref, k_ref, v_ref, gate_ref, x_ref, hg_ref, intra_ref, cross_ref,
                     kdec_ref, wo_ref, out_ref, state_ref, y_ref, wos_ref):
    @pl.when(pl.program_id(0) == 0)
    def _():
        state_ref[...] = jnp.zeros_like(state_ref)
        wos_ref[...] = (wo_ref[...] * hg_ref[...]).astype(BF16)

    for r0 in range(0, RET_STEP, RET_C):
        rs = slice(r0, r0 + RET_C)
        for hd in range(RET_HEADS):
            q = q_ref[rs, hd * RET_DK:(hd + 1) * RET_DK]
            k = k_ref[rs, hd * RET_DK:(hd + 1) * RET_DK]
            v = v_ref[rs, hd * RET_DV:(hd + 1) * RET_DV]
            state = state_ref[hd]
            scores = lax.dot_general(q, k, (((1,), (1,)), ((), ())), preferred_element_type=F32)
            scores = (scores * intra_ref[hd]).astype(BF16)
            o = jnp.dot(scores, v, preferred_element_type=F32)
            cross = cross_ref[hd]
            o_cross = jnp.dot(q, state.astype(BF16), preferred_element_type=F32)
            o = o + o_cross * jnp.concatenate([cross] * (RET_DV // 128), axis=1)
            kdec = kdec_ref[hd]
            kd = (k.astype(F32) * jnp.concatenate([kdec] * (RET_DK // 128), axis=1)).astype(BF16)
            upd = lax.dot_general(kd, v, (((0,), (0,)), ((), ())), preferred_element_type=F32)
            state_ref[hd] = state * cdec_ref[hd] + upd
            ms = jnp.mean(o * o, axis=-1, keepdims=True)
            on = o * lax.rsqrt(ms + NORM_EPS)
            gt = gate_ref[rs, hd * RET_DV:(hd + 1) * RET_DV].astype(F32)
            y_ref[rs, hd * RET_DV:(hd + 1) * RET_DV] = (_silu(gt) * on).astype(BF16)
        out_ref[rs, :] = x_ref[rs, :] + jnp.dot(y_ref[rs, :], wos_ref[...], preferred_element_type=F32)


def _ret_core(q, k, v, gate, x, head_g_col, w_out):
    n = x.shape[0]
    c = RET_C
    log_gamma = jnp.log1p(-(2.0 ** (-5.0 - jnp.arange(RET_HEADS, dtype=F32))))
    idx = jnp.arange(c, dtype=F32)
    diff = idx[:, None] - idx[None, :]
    intra = jnp.where(diff >= 0, jnp.exp(log_gamma[:, None, None] * jnp.maximum(diff, 0.0)), 0.0)
    cross = jnp.broadcast_to(jnp.exp(log_gamma[:, None] * (idx + 1.0))[:, :, None], (RET_HEADS, c, 128))
    kdec = jnp.broadcast_to(jnp.exp(log_gamma[:, None] * (c - 1.0 - idx))[:, :, None], (RET_HEADS, c, 128))
    cdec = jnp.exp(log_gamma * c)
    return pl.pallas_call(
        _ret_core_kernel,
        grid=(n // RET_STEP,),
        in_specs=[
            pl.BlockSpec(memory_space=pltpu.SMEM),
            pl.BlockSpec((RET_STEP, RET_QK), lambda i: (i, 0)),
            pl.BlockSpec((RET_STEP, RET_QK), lambda i: (i, 0)),
            pl.BlockSpec((RET_STEP, RET_V), lambda i: (i, 0)),
            pl.BlockSpec((RET_STEP, RET_V), lambda i: (i, 0)),
            pl.BlockSpec((RET_STEP, D_MODEL), lambda i: (i, 0)),
            pl.BlockSpec((RET_V, 1), lambda i: (0, 0)),
            pl.BlockSpec((RET_HEADS, c, c), lambda i: (0, 0, 0)),
            pl.BlockSpec((RET_HEADS, c, 128), lambda i: (0, 0, 0)),
            pl.BlockSpec((RET_HEADS, c, 128), lambda i: (0, 0, 0)),
            pl.BlockSpec((RET_V, D_MODEL), lambda i: (0, 0)),
        ],
        out_specs=pl.BlockSpec((RET_STEP, D_MODEL), lambda i: (i, 0)),
        out_shape=jax.ShapeDtypeStruct((n, D_MODEL), F32),
        scratch_shapes=[
            pltpu.VMEM((RET_HEADS, RET_DK, RET_DV), F32),
            pltpu.VMEM((RET_STEP, RET_V), BF16),
            pltpu.VMEM((RET_V, D_MODEL), BF16),
        ],
        compiler_params=_cparams(("arbitrary",)),
        name="ret_core",
    )(cdec, q, k, v, gate, x, head_g_col, intra, cross, kdec, w_out)


def _conv_pw1_kernel(x_ref, rw_ref, y1_ref, y2_ref, g_ref, w_ref, b_ref, xo_ref, u_ref):
    r = TM_CONV // 2
    w1, w2 = _route_weight_columns(rw_ref[...])

    def prologue(rs):
        lo1, hi1 = _unpack_bf16_pairs(y1_ref[rs, :])
        lo2, hi2 = _unpack_bf16_pairs(y2_ref[rs, :])
        x = jnp.concatenate([x_ref[rs, :HALF] + w1[rs] * lo1 + w2[rs] * lo2,
                             x_ref[rs, HALF:] + w1[rs] * hi1 + w2[rs] * hi2], axis=1)
        xo_ref[rs, :] = x
        hf = _rms(x, g_ref[...])
        return hf.astype(BF16), hf[:, 0:128]

    groups = [slice(0, r), slice(r, 2 * r)]
    hs = [prologue(rs) for rs in groups]
    zero = ((pltpu.bitcast(hs[1][1], U32) >> 16) >> 16).astype(F32)
    for j in range(D_MODEL // 512):
        wa = w_ref[:, j * 512:(j + 1) * 512].astype(BF16)
        wg = w_ref[:, D_MODEL + j * 512:D_MODEL + (j + 1) * 512].astype(BF16)
        for gi, rs in enumerate(groups):
            h = hs[gi][0]
            a = jnp.dot(h, wa, preferred_element_type=F32) + b_ref[:, j * 512:(j + 1) * 512]
            gt = jnp.dot(h, wg, preferred_element_type=F32) + b_ref[:, D_MODEL + j * 512:D_MODEL + (j + 1) * 512]
            u = a * (1.0 / (1.0 + jnp.exp(-gt)))
            if gi == 0 and j == 0:
                u_ref[rs, 0:128] = u[:, 0:128] + zero
                u_ref[rs, 128:512] = u[:, 128:512]
            else:
                u_ref[rs, j * 512:(j + 1) * 512] = u


def _conv_pw1(x, rw, yg, g, w, b):
    n = x.shape[0]
    tm = TM_CONV
    nb = n // tm
    return pl.pallas_call(
        _conv_pw1_kernel,
        grid=(nb,),
        in_specs=[
            pl.BlockSpec((tm, D_MODEL), lambda i: (i, 0)),
            pl.BlockSpec((8, tm), lambda i: (0, i)),
            pl.BlockSpec((tm, HALF), lambda i: (i, 0)),
            pl.BlockSpec((tm, HALF), lambda i: (i + nb, 0)),
            pl.BlockSpec((1, D_MODEL), lambda i: (0, 0)),
            pl.BlockSpec((D_MODEL, 2 * D_MODEL), lambda i: (0, 0)),
            pl.BlockSpec((1, 2 * D_MODEL), lambda i: (0, 0)),
        ],
        out_specs=[pl.BlockSpec((tm, D_MODEL), lambda i: (i, 0)), pl.BlockSpec((tm, D_MODEL), lambda i: (i, 0))],
        out_shape=[jax.ShapeDtypeStruct((n, D_MODEL), F32), jax.ShapeDtypeStruct((n, D_MODEL), F32)],
        compiler_params=_cparams(("arbitrary",)),
        name="conv_pw1",
    )(x, rw, yg, yg, g, w, b)


def _conv_core_kernel(u_ref, halo_ref, x_ref, wdw_ref, bdw_ref, lng_ref, lnb_ref, w2_ref, b2_ref,
                      out_ref, win_ref, z_ref):
    tm = TM_CONV
    first = pl.program_id(0) == 0
    halo = halo_ref[...]
    halo = jnp.where(first, jnp.zeros_like(halo), halo)
    nslab = D_MODEL // 128
    for cc in range(nslab):
        cs = slice(cc * 128, (cc + 1) * 128)
        win_ref[cc, 0:CONV_HALO, :] = halo[:, cs]
        win_ref[cc, CONV_HALO:CONV_HALO + tm, :] = u_ref[:, cs]
    off = CONV_HALO - (CONV_WIDTH - 1)
    rb = CONV_ROWS
    st = CONV_STRIDE
    for cc in range(nslab):
        cs = slice(cc * 128, (cc + 1) * 128)
        for r0 in range(0, tm, rb):
            accs = [bdw_ref[:, cs]] * st
            for o in range(CONV_WIDTH):
                w_o = wdw_ref[o:o + 1, cs]
                for rho in range(st):
                    accs[rho] = accs[rho] + win_ref[cc, pl.ds(r0 + rho + off + o, rb // st, stride=st), :] * w_o
            for rho in range(st):
                z_ref[cc, pl.ds(r0 + rho, rb // st, stride=st), :] = accs[rho]
    z = jnp.concatenate([z_ref[cc] for cc in range(nslab)], axis=1)
    mu = jnp.mean(z, axis=-1, keepdims=True)
    zc = z - mu
    var = jnp.mean(zc * zc, axis=-1, keepdims=True)
    zn = zc * lax.rsqrt(var + NORM_EPS) * lng_ref[...] + lnb_ref[...]
    y = _silu(zn).astype(BF16)
    out_ref[...] = x_ref[...] + jnp.dot(y, w2_ref[...].astype(BF16), preferred_element_type=F32) + b2_ref[...]


def _conv_core(u, x, w_dw, b_dw, ln_g, ln_b, w2, b2):
    n = x.shape[0]
    tm = TM_CONV
    r = tm // CONV_HALO
    wdw_pad = jnp.zeros((32, D_MODEL), F32).at[:CONV_WIDTH].set(w_dw)
    return pl.pallas_call(
        _conv_core_kernel,
        grid=(n // tm,),
        in_specs=[
            pl.BlockSpec((tm, D_MODEL), lambda i: (i, 0)),
            pl.BlockSpec((CONV_HALO, D_MODEL), lambda i: (jnp.maximum(i * r - 1, 0), 0)),
            pl.BlockSpec((tm, D_MODEL), lambda i: (i, 0)),
            pl.BlockSpec((32, D_MODEL), lambda i: (0, 0)),
            pl.BlockSpec((1, D_MODEL), lambda i: (0, 0)),
            pl.BlockSpec((1, D_MODEL), lambda i: (0, 0)),
            pl.BlockSpec((1, D_MODEL), lambda i: (0, 0)),
            pl.BlockSpec((D_MODEL, D_MODEL), lambda i: (0, 0)),
            pl.BlockSpec((1, D_MODEL), lambda i: (0, 0)),
        ],
        out_specs=pl.BlockSpec((tm, D_MODEL), lambda i: (i, 0)),
        out_shape=jax.ShapeDtypeStruct((n, D_MODEL), F32),
        scratch_shapes=[pltpu.VMEM((D_MODEL // 128, CONV_HALO + tm, 128), F32),
                        pltpu.VMEM((D_MODEL // 128, tm, 128), F32)],
        compiler_params=_cparams(("arbitrary",)),
        name="conv_core",
    )(u, u, x, wdw_pad, b_dw, ln_g, ln_b, w2, b2)


def _router_kernel(x_ref, g_ref, wr_ref, br_ref, hp_ref, ri_ref, rw_ref, cnt_ref, carry_ref):
    t = T_ROUTE

    @pl.when(pl.program_id(0) == 0)
    def _():
        carry_ref[...] = jnp.zeros_like(carry_ref)

    h = _rms(x_ref[...], g_ref[...])
    hp_ref[...] = _pack_bf16_pairs(h)
    h_hi = h.astype(BF16)
    h_lo = (h - h_hi.astype(F32)).astype(BF16)
    w = wr_ref[...]
    w_hi = w.astype(BF16)
    w_lo = (w - w_hi.astype(F32)).astype(BF16)
    dn = (((1,), (1,)), ((), ()))
    p = lax.dot_general(jnp.concatenate([w_hi, w_lo], axis=0), h_hi, dn, preferred_element_type=F32)
    nr = wr_ref.shape[0]
    logits = p[0:nr] + p[nr:2 * nr] + lax.dot_general(w_hi, h_lo, dn, preferred_element_type=F32)
    logits = logits + br_ref[:, 0:1]

    best = logits[0:1]
    gi = jnp.zeros((1, t), I32)
    for j in range(1, MOE_GROUPS):
        r = logits[j:j + 1]
        up = r > best
        gi = jnp.where(up, j, gi)
        best = jnp.where(up, r, best)
    den = jnp.zeros((1, t), F32)
    for j in range(MOE_GROUPS):
        den = den + jnp.exp(logits[j:j + 1] - best)
    gate_g = 1.0 / den

    sel = logits[8:8 + MOE_EPG]
    for j in range(1, MOE_GROUPS):
        sel = jnp.where(gi == j, logits[8 + j * MOE_EPG:8 + (j + 1) * MOE_EPG], sel)

    m1 = sel[0:1]
    i1 = jnp.zeros((1, t), I32)
    for j in range(1, MOE_EPG):
        r = sel[j:j + 1]
        up = r > m1
        i1 = jnp.where(up, j, i1)
        m1 = jnp.where(up, r, m1)
    m2 = jnp.full((1, t), -jnp.inf, F32)
    i2 = jnp.zeros((1, t), I32)
    started = jnp.zeros((1, t), jnp.bool_)
    for j in range(MOE_EPG):
        r = sel[j:j + 1]
        ok = i1 != j
        up = ok & ((r > m2) | jnp.logical_not(started))
        i2 = jnp.where(up, j, i2)
        m2 = jnp.where(up, r, m2)
        started = started | ok
    e21 = jnp.exp(m2 - m1)
    p1 = 1.0 / (1.0 + e21)
    w1 = gate_g * p1
    w2 = gate_g * (e21 * p1)
    eid1 = gi * MOE_EPG + i1
    eid2 = gi * MOE_EPG + i2

    eio = lax.broadcasted_iota(I32, (MOE_EXPERTS, t), 0)
    oh1 = eio == eid1
    oh2 = eio == eid2
    oh = (oh1 | oh2).astype(F32)
    rio = lax.broadcasted_iota(I32, (t, t), 0)
    cio = lax.broadcasted_iota(I32, (t, t), 1)
    upper = (rio < cio).astype(BF16)
    cum = jnp.dot(oh.astype(BF16), upper, preferred_element_type=F32) + carry_ref[:, 0:1]
    rank1 = jnp.sum(jnp.where(oh1, cum, 0.0), axis=0, keepdims=True)
    rank2 = jnp.sum(jnp.where(oh2, cum, 0.0), axis=0, keepdims=True)
    carry_ref[...] = carry_ref[...] + jnp.sum(oh, axis=1, keepdims=True)
    cnt_ref[...] = carry_ref[...]

    zi = jnp.zeros((4, t), I32)
    ri_ref[...] = jnp.concatenate([eid1, eid2, rank1.astype(I32), rank2.astype(I32), zi], axis=0)
    zf = jnp.zeros((6, t), F32)
    rw_ref[...] = jnp.concatenate([w1, w2, zf], axis=0)


def _router(x, g, w_rg, b_rg, w_re, b_re):
    n = x.shape[0]
    t = T_ROUTE
    wr = jnp.zeros((40, D_MODEL), F32).at[0:MOE_GROUPS].set(w_rg.T).at[8:40].set(w_re.T)
    br = jnp.zeros((40,), F32).at[0:MOE_GROUPS].set(b_rg).at[8:40].set(b_re)
    br = jnp.broadcast_to(br[:, None], (40, 128))
    return pl.pallas_call(
        _router_kernel,
        grid=(n // t,),
        in_specs=[
            pl.BlockSpec((t, D_MODEL), lambda i: (i, 0)),
            pl.BlockSpec((1, D_MODEL), lambda i: (0, 0)),
            pl.BlockSpec((40, D_MODEL), lambda i: (0, 0)),
            pl.BlockSpec((40, 128), lambda i: (0, 0)),
        ],
        out_specs=[
            pl.BlockSpec((t, HALF), lambda i: (i, 0)),
            pl.BlockSpec((8, t), lambda i: (0, i)),
            pl.BlockSpec((8, t), lambda i: (0, i)),
            pl.BlockSpec((MOE_EXPERTS, 128), lambda i: (0, 0)),
        ],
        out_shape=[
            jax.ShapeDtypeStruct((n, HALF), U32),
            jax.ShapeDtypeStruct((8, n), I32),
            jax.ShapeDtypeStruct((8, n), F32),
            jax.ShapeDtypeStruct((MOE_EXPERTS, 128), F32),
        ],
        scratch_shapes=[pltpu.VMEM((MOE_EXPERTS, 128), F32)],
        compiler_params=_cparams(("arbitrary",)),
        name="moe_router",
    )(x, g, wr, br)


def _sc_mesh():
    return plsc.VectorSubcoreMesh(core_axis_name="c", subcore_axis_name="s",
                                  num_cores=SC_CORES, num_subcores=SC_SUBCORES)


def _sc_worker_id():
    return lax.axis_index("s") * SC_CORES + lax.axis_index("c")


def _sc_dispatch(hp, idx, zero_rows, total_rows):
    n = hp.shape[0]
    tpw = n // SC_WORKERS
    kd = tpw // SC_ROWS
    kp = idx.shape[1] - 2 * kd
    nb = SC_NBUF

    @functools.partial(
        pl.kernel, mesh=_sc_mesh(),
        out_type=jax.ShapeDtypeStruct((total_rows, HALF), U32),
        scratch_types=[pltpu.VMEM((2 * kd + kp, SC_ROWS), I32)]
        + [pltpu.VMEM((SC_ROWS, HALF), U32)] * (nb + 1)
        + [pltpu.SemaphoreType.DMA((nb,)), pltpu.SemaphoreType.DMA((nb,)), pltpu.SemaphoreType.DMA],
        name="moe_dispatch_sc",
    )
    def k(hp_hbm, idx_hbm, zero_hbm, xs_hbm, idx_v, *rest):
        bufs, zbuf = rest[:nb], rest[nb]
        load_sem, scat_sem, pad_sem = rest[nb + 1:]
        wid = _sc_worker_id()
        pltpu.sync_copy(idx_hbm.at[wid], idx_v)

        def load(c):
            return pltpu.make_async_copy(hp_hbm.at[pl.ds(wid * tpw + c * SC_ROWS, SC_ROWS)], bufs[c % nb],
                                         load_sem.at[c % nb])

        def scatters(c):
            return (pltpu.make_async_copy(bufs[c % nb], xs_hbm.at[idx_v.at[c]], scat_sem.at[c % nb]),
                    pltpu.make_async_copy(bufs[c % nb], xs_hbm.at[idx_v.at[kd + c]], scat_sem.at[c % nb]))

        for c in range(min(nb - 1, kd)):
            load(c).start()
        pltpu.sync_copy(zero_hbm, zbuf)
        pads = [pltpu.make_async_copy(zbuf, xs_hbm.at[idx_v.at[2 * kd + j]], pad_sem) for j in range(kp)]
        for p in pads:
            p.start()
        for c in range(kd):
            load(c).wait()
            for d in scatters(c):
                d.start()
            if c + nb - 1 < kd:
                if c >= 1:
                    for d in scatters(c - 1):
                        d.wait()
                load(c + nb - 1).start()
        for c in range(max(kd - nb, 0), kd):
            for d in scatters(c):
                d.wait()
        for p in pads:
            p.wait()

    return k(hp, idx, zero_rows)


def _sc_gather(ys, idx):
    kg = idx.shape[1]
    rows_per_worker = kg * SC_ROWS
    nb = SC_NBUF

    @functools.partial(
        pl.kernel, mesh=_sc_mesh(),
        out_type=jax.ShapeDtypeStruct((SC_WORKERS * rows_per_worker, HALF), U32),
        scratch_types=[pltpu.VMEM((kg, SC_ROWS), I32)] + [pltpu.VMEM((SC_ROWS, HALF), U32)] * nb
        + [pltpu.SemaphoreType.DMA((nb,)), pltpu.SemaphoreType.DMA((nb,))],
        name="moe_gather_sc",
    )
    def k(ys_hbm, idx_hbm, yg_hbm, idx_v, *rest):
        bufs = rest[:nb]
        gat_sem, out_sem = rest[nb:]
        wid = _sc_worker_id()
        pltpu.sync_copy(idx_hbm.at[wid], idx_v)

        def gather(c):
            return pltpu.make_async_copy(ys_hbm.at[idx_v.at[c]], bufs[c % nb], gat_sem.at[c % nb])

        def store(c):
            return pltpu.make_async_copy(bufs[c % nb],
                                         yg_hbm.at[pl.ds(wid * rows_per_worker + c * SC_ROWS, SC_ROWS)],
                                         out_sem.at[c % nb])

        for c in range(min(nb - 1, kg)):
            gather(c).start()
        for c in range(kg):
            gather(c).wait()
            store(c).start()
            if c + nb - 1 < kg:
                if c >= 1:
                    store(c - 1).wait()
                gather(c + nb - 1).start()
        for c in range(max(kg - nb, 0), kg):
            store(c).wait()

    return k(ys, idx)


def _expert_kernel(blk_e_ref, nused_ref, first_ref, slot_ref, nxt_ref, xs_ref, wg_hbm, wu_hbm, wd_hbm,
                   ys_ref, wg_buf, wu_buf, wd_buf, sems, *, layer):
    step = pl.program_id(0)
    tm = TM_EXP
    nused = nused_ref[0]

    def weight_copies(e, s):
        return (pltpu.make_async_copy(wg_hbm.at[layer, e], wg_buf.at[s], sems.at[s, 0]),
                pltpu.make_async_copy(wu_hbm.at[layer, e], wu_buf.at[s], sems.at[s, 1]),
                pltpu.make_async_copy(wd_hbm.at[layer, e], wd_buf.at[s], sems.at[s, 2]))

    def dma_control(j):
        i = step * EXP_SUB + j

        @pl.when(i < nused)
        def _():
            s = slot_ref[i]

            if j == 0:
                @pl.when(i == 0)
                def _():
                    for c in weight_copies(blk_e_ref[0], 0):
                        c.start()

            @pl.when(first_ref[i] == 1)
            def _():
                for c in weight_copies(blk_e_ref[i], s):
                    c.wait()

                @pl.when(nxt_ref[i] >= 0)
                def _():
                    for c in weight_copies(nxt_ref[i], lax.rem(s + 1, EXP_SLOTS)):
                        c.start()

    for j in range(EXP_SUB):
        dma_control(j)

    @pl.when(step * EXP_SUB < nused)
    def _():
        for j in range(EXP_SUB):
            i = step * EXP_SUB + j
            rows = slice(j * tm, (j + 1) * tm)
            s = slot_ref[i]
            lo, hi = _unpack_bf16_pairs(xs_ref[rows, :])
            xf = jnp.concatenate([lo, hi], axis=1)
            a = jnp.dot(xf, wg_buf[s], preferred_element_type=F32)
            b = jnp.dot(xf, wu_buf[s], preferred_element_type=F32)
            hm = _silu(a) * b
            y = jnp.dot(hm, wd_buf[s], preferred_element_type=F32)
            ys_ref[rows, :] = jnp.where(i < nused, _pack_bf16_pairs(y), jnp.uint32(0))

    @pl.when(step * EXP_SUB >= nused)
    def _():
        ys_ref[...] = jnp.zeros_like(ys_ref)


def _experts(xs, blk_e, nused, w_gate, w_up, w_down, layer):
    tm = TM_EXP
    p_rows = xs.shape[0] - MOE_EXPERTS * tm
    nblk = p_rows // tm
    pos = jnp.arange(nblk, dtype=I32)
    valid = pos < nused[0]
    prev_e = jnp.concatenate([jnp.full((1,), -1, I32), blk_e[:-1]])
    first = valid & (blk_e != prev_e)
    slot = jnp.maximum(jnp.cumsum(first.astype(I32)) - 1, 0) % EXP_SLOTS
    first_pos = jnp.where(first, pos, nblk)
    next_first = jnp.concatenate([lax.cummin(first_pos, reverse=True)[1:], jnp.full((1,), nblk, I32)])
    nxt = jnp.where(next_first < nblk, blk_e[jnp.minimum(next_first, nblk - 1)], -1)

    def blk(i, be, nu, *_):
        return jnp.minimum(i, (nu[0] - 1) // EXP_SUB)

    grid_spec = pltpu.PrefetchScalarGridSpec(
        num_scalar_prefetch=5,
        grid=(nblk // EXP_SUB,),
        in_specs=[
            pl.BlockSpec((EXP_SUB * tm, HALF), lambda i, *sp: (blk(i, *sp), 0)),
            pl.BlockSpec(memory_space=pl.ANY),
            pl.BlockSpec(memory_space=pl.ANY),
            pl.BlockSpec(memory_space=pl.ANY),
        ],
        out_specs=pl.BlockSpec((EXP_SUB * tm, HALF), lambda i, *sp: (i, 0)),
        scratch_shapes=[
            pltpu.VMEM((EXP_SLOTS, D_MODEL, MOE_FF), F32),
            pltpu.VMEM((EXP_SLOTS, D_MODEL, MOE_FF), F32),
            pltpu.VMEM((EXP_SLOTS, MOE_FF, D_MODEL), F32),
            pltpu.SemaphoreType.DMA((EXP_SLOTS, 3)),
        ],
    )
    return pl.pallas_call(
        functools.partial(_expert_kernel, layer=layer),
        grid_spec=grid_spec,
        out_shape=jax.ShapeDtypeStruct((p_rows, HALF), U32),
        compiler_params=_cparams(("arbitrary",)),
        name="moe_experts",
    )(blk_e, nused, first.astype(I32), slot.astype(I32), nxt.astype(I32), xs, w_gate, w_up, w_down)


def _combine_kernel(x_ref, rw_ref, fg_ref, y1_ref, y2_ref, out_ref):
    lo1, hi1 = _unpack_bf16_pairs(y1_ref[...])
    lo2, hi2 = _unpack_bf16_pairs(y2_ref[...])
    w1, w2 = _route_weight_columns(rw_ref[...])
    x = x_ref[...]
    o_lo = x[:, :HALF] + w1 * lo1 + w2 * lo2
    o_hi = x[:, HALF:] + w1 * hi1 + w2 * hi2
    ms = (jnp.sum(o_lo * o_lo, axis=-1, keepdims=True)
          + jnp.sum(o_hi * o_hi, axis=-1, keepdims=True)) * (1.0 / D_MODEL)
    sc = lax.rsqrt(ms + NORM_EPS)
    o_lo = o_lo * sc * fg_ref[:, :HALF]
    o_hi = o_hi * sc * fg_ref[:, HALF:]
    out_ref[:, :HALF] = o_lo
    out_ref[:, HALF:] = o_hi


def _combine(yg, x, rw, final_g):
    n = x.shape[0]
    td = T_COMB
    nb = n // td
    return pl.pallas_call(
        _combine_kernel,
        grid=(nb,),
        in_specs=[
            pl.BlockSpec((td, D_MODEL), lambda i: (i, 0)),
            pl.BlockSpec((8, td), lambda i: (0, i)),
            pl.BlockSpec((1, D_MODEL), lambda i: (0, 0)),
            pl.BlockSpec((td, HALF), lambda i: (i, 0)),
            pl.BlockSpec((td, HALF), lambda i: (i + nb, 0)),
        ],
        out_specs=pl.BlockSpec((td, D_MODEL), lambda i: (i, 0)),
        out_shape=jax.ShapeDtypeStruct((n, D_MODEL), F32),
        compiler_params=_cparams(("arbitrary",)),
        name="moe_combine",
    )(x, rw, final_g, yg, yg)


def _moe(x, g, w_rg, b_rg, w_re, b_re, w_gate, w_up, w_down, layer):
    n = x.shape[0]
    tm = TM_EXP
    p_rows = 2 * n + MOE_EXPERTS * tm
    nblk = p_rows // tm
    hp, ri, rw, cnt = _router(x, g, w_rg, b_rg, w_re, b_re)
    counts = cnt[:, 0].astype(I32)
    pcounts = (counts + tm - 1) // tm * tm
    pend = jnp.cumsum(pcounts)
    pstart = pend - pcounts
    eio = jnp.arange(MOE_EXPERTS, dtype=I32)[:, None]
    dest1 = jnp.sum(jnp.where(ri[0][None, :] == eio, pstart[:, None], 0), axis=0) + ri[2]
    dest2 = jnp.sum(jnp.where(ri[1][None, :] == eio, pstart[:, None], 0), axis=0) + ri[3]
    blk_start = jnp.arange(nblk, dtype=I32) * tm
    blk_e = jnp.minimum(jnp.sum((pend[None, :] <= blk_start[:, None]).astype(I32), axis=1), MOE_EXPERTS - 1)
    nused = jnp.maximum(pend[-1] // tm, 1).astype(I32).reshape(1)
    r = jnp.arange(tm, dtype=I32)[None, :]
    pad_slot = jnp.where(r < (pcounts - counts)[:, None], (pstart + counts)[:, None] + r, p_rows + eio * tm + r)
    kd = n // SC_WORKERS // SC_ROWS
    idx = jnp.concatenate([dest1.reshape(SC_WORKERS, kd, SC_ROWS), dest2.reshape(SC_WORKERS, kd, SC_ROWS),
                           pad_slot.reshape(SC_WORKERS, -1, SC_ROWS)], axis=1)
    zero_rows = jnp.zeros((SC_ROWS, HALF), U32)
    xs = _sc_dispatch(hp, idx, zero_rows, p_rows + MOE_EXPERTS * tm)
    ys = _experts(xs, blk_e, nused, w_gate, w_up, w_down, layer)
    gidx = jnp.concatenate([dest1, dest2]).reshape(SC_WORKERS, -1, SC_ROWS)
    yg = _sc_gather(ys, gidx)
    return yg, rw


def kernel(x, positions, norm_mix_g, norm_ffn_g, ret_w_in, ret_head_g, ret_w_out, conv_w_pw1, conv_b_pw1, conv_w_dw, conv_b_dw, conv_ln_g, conv_ln_b, conv_w_pw2, conv_b_pw2, moe_w_rg, moe_b_rg, moe_w_re, moe_b_re, moe_w_gate, moe_w_up, moe_w_down, final_norm_g):
    b, s, d = x.shape
    n = b * s
    xt = x.reshape(n, d)
    pos = positions.reshape(n, 1)
    fg = final_norm_g.reshape(1, d)

    q, k, v, gate = _ret_inproj(xt, pos, norm_mix_g[0].reshape(1, d), ret_w_in[0])
    xt = _ret_core(q, k, v, gate, xt, ret_head_g[0].reshape(RET_V, 1), ret_w_out[0])
    yg, rw = _moe(xt, norm_ffn_g[0].reshape(1, d), moe_w_rg[0], moe_b_rg[0], moe_w_re[0], moe_b_re[0],
                  moe_w_gate, moe_w_up, moe_w_down, 0)

    xt, u = _conv_pw1(xt, rw, yg, norm_mix_g[1].reshape(1, d), conv_w_pw1[0],
                      conv_b_pw1[0].reshape(1, 2 * d))
    xt = _conv_core(u, xt, conv_w_dw[0], conv_b_dw[0].reshape(1, d), conv_ln_g[0].reshape(1, d),
                    conv_ln_b[0].reshape(1, d), conv_w_pw2[0], conv_b_pw2[0].reshape(1, d))
    yg, rw = _moe(xt, norm_ffn_g[1].reshape(1, d), moe_w_rg[1], moe_b_rg[1], moe_w_re[1], moe_b_re[1],
                  moe_w_gate, moe_w_up, moe_w_down, 1)
    xt = _combine(yg, xt, rw, fg)
    return xt.reshape(b, s, d)
```

```python
import functools

import jax
import jax.numpy as jnp
from jax import lax
from jax.experimental import pallas as pl
from jax.experimental.pallas import tpu as pltpu
from jax.experimental.pallas import tpu_sc as plsc

F32 = jnp.float32
BF16 = jnp.bfloat16
U32 = jnp.uint32
I32 = jnp.int32

D_MODEL = 1024
RET_HEADS = 4
RET_DK = 256
RET_DV = 512
RET_QK = RET_HEADS * RET_DK
RET_V = RET_HEADS * RET_DV
ROPE_BASE = 10000.0
CONV_WIDTH = 31
MOE_GROUPS = 4
MOE_EPG = 8
MOE_EXPERTS = MOE_GROUPS * MOE_EPG
MOE_FF = 512
NORM_EPS = 1e-6

TM_PROJ = 512
RET_C = 256
RET_STEP = 512
TM_CONV = 512
TM_PW1 = 1024
PW1_ROWS = 256
CONV_HALO = 32
CONV_ROWS = 128
CONV_STRIDE = 4
T_ROUTE = 512
TM_EXP = 256
EXP_SUB = 4
EXP_SLOTS = EXP_SUB + 1
T_COMB = 1024
SC_CORES = 2
SC_SUBCORES = 16
SC_WORKERS = SC_CORES * SC_SUBCORES
SC_ROWS = 32
SC_NBUF = 4
HALF = D_MODEL // 2

VMEM_LIMIT = 56 * 1024 * 1024


def _cparams(sem, flags=None):
    return pltpu.CompilerParams(dimension_semantics=sem, vmem_limit_bytes=VMEM_LIMIT, flags=flags)


def _rms(x, g):
    ms = jnp.mean(x * x, axis=-1, keepdims=True)
    return x * lax.rsqrt(ms + NORM_EPS) * g


def _silu(x):
    return x * (1.0 / (1.0 + jnp.exp(-x)))


def _pack_bf16_pairs(y):
    lo = pltpu.bitcast(y[:, :HALF].astype(BF16).astype(F32), U32)
    hi = pltpu.bitcast(y[:, HALF:].astype(BF16).astype(F32), U32)
    return (hi & jnp.uint32(0xFFFF0000)) | (lo >> 16)


def _route_weight_columns(rw):
    t = jnp.concatenate([rw] * 16, axis=0).T
    return t[:, 0:1], t[:, 1:2]


def _unpack_bf16_pairs(p):
    lo = pltpu.bitcast(p << 16, F32)
    hi = pltpu.bitcast(p & jnp.uint32(0xFFFF0000), F32)
    return lo, hi


def _ret_inproj_kernel(x_ref, pos_ref, g_ref, inv_ref, w_ref, q_ref, k_ref, v_ref, gate_ref):
    half = RET_DK // 2
    kscale = RET_DK ** -0.5
    h = _rms(x_ref[...], g_ref[...]).astype(BF16)

    def proj(c0, width):
        return jnp.dot(h, w_ref[:, c0:c0 + width].astype(BF16), preferred_element_type=F32)

    v0 = proj(2 * RET_QK, 512)
    v_ref[:, 0:512] = v0.astype(BF16)
    zero = ((pltpu.bitcast(v0[:, 0:half], U32) >> 16) >> 16).astype(F32)
    ang = pos_ref[...].astype(F32) * inv_ref[...] + zero
    cos = jnp.cos(ang)
    sin = jnp.sin(ang)
    for j in range(RET_V // 512):
        if j > 0:
            v_ref[:, j * 512:(j + 1) * 512] = proj(2 * RET_QK + j * 512, 512).astype(BF16)
        gate_ref[:, j * 512:(j + 1) * 512] = proj(2 * RET_QK + RET_V + j * 512, 512).astype(BF16)

    for hd in range(RET_HEADS):
        for base, out_ref, cs, sn in ((0, q_ref, cos, sin), (RET_QK, k_ref, cos * kscale, sin * kscale)):
            t = proj(base + hd * RET_DK, RET_DK)
            t1 = t[:, :half]
            t2 = t[:, half:]
            out_ref[:, hd * RET_DK:hd * RET_DK + half] = (t1 * cs - t2 * sn).astype(BF16)
            out_ref[:, hd * RET_DK + half:(hd + 1) * RET_DK] = (t1 * sn + t2 * cs).astype(BF16)


def _ret_inproj(x, pos, g, w_in):
    n = x.shape[0]
    half = RET_DK // 2
    inv = (ROPE_BASE ** (-jnp.arange(half, dtype=F32) / half)).reshape(1, half)
    tm = TM_PROJ
    return pl.pallas_call(
        _ret_inproj_kernel,
        grid=(n // tm,),
        in_specs=[
            pl.BlockSpec((tm, D_MODEL), lambda i: (i, 0)),
            pl.BlockSpec((tm, 1), lambda i: (i, 0)),
            pl.BlockSpec((1, D_MODEL), lambda i: (0, 0)),
            pl.BlockSpec((1, half), lambda i: (0, 0)),
            pl.BlockSpec(w_in.shape, lambda i: (0, 0), pipeline_mode=pl.Buffered(1)),
        ],
        out_specs=[
            pl.BlockSpec((tm, RET_QK), lambda i: (i, 0)),
            pl.BlockSpec((tm, RET_QK), lambda i: (i, 0)),
            pl.BlockSpec((tm, RET_V), lambda i: (i, 0)),
            pl.BlockSpec((tm, RET_V), lambda i: (i, 0)),
        ],
        out_shape=[
            jax.ShapeDtypeStruct((n, RET_QK), BF16),
            jax.ShapeDtypeStruct((n, RET_QK), BF16),
            jax.ShapeDtypeStruct((n, RET_V), BF16),
            jax.ShapeDtypeStruct((n, RET_V), BF16),
        ],
        compiler_params=_cparams(("arbitrary",)),
        name="ret_inproj",
    )(x, pos, g, inv, w_in)


def _ret_core_kernel(cdec_ref, q_ref, k_ref, v_ref, gate_ref, x_ref, hg_ref, intra_ref, cross_ref,
                     kdec_ref, wo_ref, out_ref, state_ref, y_ref, wos_ref):
    @pl.when(pl.program_id(0) == 0)
    def _():
        state_ref[...] = jnp.zeros_like(state_ref)
        wos_ref[...] = (wo_ref[...] * hg_ref[...]).astype(BF16)

    for r0 in range(0, RET_STEP, RET_C):
        rs = slice(r0, r0 + RET_C)
        for hd in range(RET_HEADS):
            q = q_ref[rs, hd * RET_DK:(hd + 1) * RET_DK]
            k = k_ref[rs, hd * RET_DK:(hd + 1) * RET_DK]
            v = v_ref[rs, hd * RET_DV:(hd + 1) * RET_DV]
            state = state_ref[hd]
            scores = lax.dot_general(q, k, (((1,), (1,)), ((), ())), preferred_element_type=F32)
            scores = (scores * intra_ref[hd]).astype(BF16)
            o = jnp.dot(scores, v, preferred_element_type=F32)
            cross = cross_ref[hd]
            o_cross = jnp.dot(q, state.astype(BF16), preferred_element_type=F32)
            o = o + o_cross * jnp.concatenate([cross] * (RET_DV // 128), axis=1)
            kdec = kdec_ref[hd]
            kd = (k.astype(F32) * jnp.concatenate([kdec] * (RET_DK // 128), axis=1)).astype(BF16)
            upd = lax.dot_general(kd, v, (((0,), (0,)), ((), ())), preferred_element_type=F32)
            state_ref[hd] = state * cdec_ref[hd] + upd
            ms = jnp.mean(o * o, axis=-1, keepdims=True)
            on = o * lax.rsqrt(ms + NORM_EPS)
            gt = gate_ref[rs, hd * RET_DV:(hd + 1) * RET_DV].astype(F32)
            y_ref[rs, hd * RET_DV:(hd + 1) * RET_DV] = (_silu(gt) * on).astype(BF16)
        out_ref[rs, :] = x_ref[rs, :] + jnp.dot(y_ref[rs, :], wos_ref[...], preferred_element_type=F32)


def _ret_core(q, k, v, gate, x, head_g_col, w_out):
    n = x.shape[0]
    c = RET_C
    log_gamma = jnp.log1p(-(2.0 ** (-5.0 - jnp.arange(RET_HEADS, dtype=F32))))
    idx = jnp.arange(c, dtype=F32)
    diff = idx[:, None] - idx[None, :]
    intra = jnp.where(diff >= 0, jnp.exp(log_gamma[:, None, None] * jnp.maximum(diff, 0.0)), 0.0)
    cross = jnp.broadcast_to(jnp.exp(log_gamma[:, None] * (idx + 1.0))[:, :, None], (RET_HEADS, c, 128))
    kdec = jnp.broadcast_to(jnp.exp(log_gamma[:, None] * (c - 1.0 - idx))[:, :, None], (RET_HEADS, c, 128))
    cdec = jnp.exp(log_gamma * c)
    return pl.pallas_call(
        _ret_core_kernel,
        grid=(n // RET_STEP,),
        in_specs=[
            pl.BlockSpec(memory_space=pltpu.SMEM),
            pl.BlockSpec((RET_STEP, RET_QK), lambda i: (i, 0)),
            pl.BlockSpec((RET_STEP, RET_QK), lambda i: (i, 0)),
            pl.BlockSpec((RET_STEP, RET_V), lambda i: (i, 0)),
            pl.BlockSpec((RET_STEP, RET_V), lambda i: (i, 0)),
            pl.BlockSpec((RET_STEP, D_MODEL), lambda i: (i, 0)),
            pl.BlockSpec((RET_V, 1), lambda i: (0, 0)),
            pl.BlockSpec((RET_HEADS, c, c), lambda i: (0, 0, 0)),
            pl.BlockSpec((RET_HEADS, c, 128), lambda i: (0, 0, 0)),
            pl.BlockSpec((RET_HEADS, c, 128), lambda i: (0, 0, 0)),
            pl.BlockSpec((RET_V, D_MODEL), lambda i: (0, 0)),
        ],
        out_specs=pl.BlockSpec((RET_STEP, D_MODEL), lambda i: (i, 0)),
        out_shape=jax.ShapeDtypeStruct((n, D_MODEL), F32),
        scratch_shapes=[
            pltpu.VMEM((RET_HEADS, RET_DK, RET_DV), F32),
            pltpu.VMEM((RET_STEP, RET_V), BF16),
            pltpu.VMEM((RET_V, D_MODEL), BF16),
        ],
        compiler_params=_cparams(("arbitrary",)),
        name="ret_core",
    )(cdec, q, k, v, gate, x, head_g_col, intra, cross, kdec, w_out)


def _conv_pw1_kernel(x_ref, rw_ref, y1_ref, y2_ref, g_ref, w_ref, b_ref, xo_ref, u_ref):
    r = PW1_ROWS
    ngroups = TM_PW1 // r
    w1, w2 = _route_weight_columns(rw_ref[...])

    def prologue(rs):
        lo1, hi1 = _unpack_bf16_pairs(y1_ref[rs, :])
        lo2, hi2 = _unpack_bf16_pairs(y2_ref[rs, :])
        x = jnp.concatenate([x_ref[rs, :HALF] + w1[rs] * lo1 + w2[rs] * lo2,
                             x_ref[rs, HALF:] + w1[rs] * hi1 + w2[rs] * hi2], axis=1)
        xo_ref[rs, :] = x
        hf = _rms(x, g_ref[...])
        return hf.astype(BF16), hf[:, 0:128]

    groups = [slice(gi * r, (gi + 1) * r) for gi in range(ngroups)]
    hs = [prologue(rs) for rs in groups]
    for j in range(D_MODEL // 512):
        wa = w_ref[:, j * 512:(j + 1) * 512].astype(BF16)
        wg = w_ref[:, D_MODEL + j * 512:D_MODEL + (j + 1) * 512].astype(BF16)
        for gi, rs in enumerate(groups):
            h = hs[gi][0]
            a = jnp.dot(h, wa, preferred_element_type=F32) + b_ref[:, j * 512:(j + 1) * 512]
            gt = jnp.dot(h, wg, preferred_element_type=F32) + b_ref[:, D_MODEL + j * 512:D_MODEL + (j + 1) * 512]
            u = a * (1.0 / (1.0 + jnp.exp(-gt)))
            if j == 0 and gi + 1 < ngroups:
                zero = ((pltpu.bitcast(hs[gi + 1][1], U32) >> 16) >> 16).astype(F32)
                u_ref[rs, 0:128] = u[:, 0:128] + zero
                u_ref[rs, 128:512] = u[:, 128:512]
            else:
                u_ref[rs, j * 512:(j + 1) * 512] = u


def _conv_pw1(x, rw, yg, g, w, b):
    n = x.shape[0]
    tm = TM_PW1
    nb = n // tm
    return pl.pallas_call(
        _conv_pw1_kernel,
        grid=(nb,),
        in_specs=[
            pl.BlockSpec((tm, D_MODEL), lambda i: (i, 0)),
            pl.BlockSpec((8, tm), lambda i: (0, i)),
            pl.BlockSpec((tm, HALF), lambda i: (i, 0)),
            pl.BlockSpec((tm, HALF), lambda i: (i + nb, 0)),
            pl.BlockSpec((1, D_MODEL), lambda i: (0, 0)),
            pl.BlockSpec((D_MODEL, 2 * D_MODEL), lambda i: (0, 0), pipeline_mode=pl.Buffered(1)),
            pl.BlockSpec((1, 2 * D_MODEL), lambda i: (0, 0)),
        ],
        out_specs=[pl.BlockSpec((tm, D_MODEL), lambda i: (i, 0)), pl.BlockSpec((tm, D_MODEL), lambda i: (i, 0))],
        out_shape=[jax.ShapeDtypeStruct((n, D_MODEL), F32), jax.ShapeDtypeStruct((n, D_MODEL), F32)],
        compiler_params=_cparams(("arbitrary",)),
        name="conv_pw1",
    )(x, rw, yg, yg, g, w, b)


def _conv_core_kernel(u_ref, halo_ref, x_ref, wdw_ref, bdw_ref, lng_ref, lnb_ref, w2_ref, b2_ref,
                      out_ref, win_ref, z_ref):
    tm = TM_CONV
    first = pl.program_id(0) == 0
    halo = halo_ref[...]
    halo = jnp.where(first, jnp.zeros_like(halo), halo)
    nslab = D_MODEL // 128
    for cc in range(nslab):
        cs = slice(cc * 128, (cc + 1) * 128)
        win_ref[cc, 0:CONV_HALO, :] = halo[:, cs]
        win_ref[cc, CONV_HALO:CONV_HALO + tm, :] = u_ref[:, cs]
    off = CONV_HALO - (CONV_WIDTH - 1)
    rb = CONV_ROWS
    st = CONV_STRIDE
    w2 = w2_ref[...].astype(BF16)
    rh = tm // 2
    zero = None
    for h0 in range(0, tm, rh):
        rows = slice(h0, h0 + rh)
        for cc in range(nslab):
            cs = slice(cc * 128, (cc + 1) * 128)
            for r0 in range(h0, h0 + rh, rb):
                accs = [bdw_ref[:, cs]] * st
                for o in range(CONV_WIDTH):
                    w_o = wdw_ref[o:o + 1, cs]
                    for rho in range(st):
                        accs[rho] = (accs[rho]
                                     + win_ref[cc, pl.ds(r0 + rho + off + o, rb // st, stride=st), :] * w_o)
                for rho in range(st):
                    z_ref[cc, pl.ds(r0 + rho, rb // st, stride=st), :] = accs[rho]
        z0 = z_ref[0, rows, :] if zero is None else z_ref[0, rows, :] + zero
        z = jnp.concatenate([z0] + [z_ref[cc, rows, :] for cc in range(1, nslab)], axis=1)
        mu = jnp.mean(z, axis=-1, keepdims=True)
        zc = z - mu
        var = jnp.mean(zc * zc, axis=-1, keepdims=True)
        zn = zc * lax.rsqrt(var + NORM_EPS) * lng_ref[...] + lnb_ref[...]
        y = _silu(zn).astype(BF16)
        o2 = jnp.dot(y, w2, preferred_element_type=F32)
        zero = ((pltpu.bitcast(o2[:, 0:128], U32) >> 16) >> 16).astype(F32)
        out_ref[rows, :] = x_ref[rows, :] + o2 + b2_ref[...]


def _conv_core(u, x, w_dw, b_dw, ln_g, ln_b, w2, b2):
    n = x.shape[0]
    tm = TM_CONV
    r = tm // CONV_HALO
    wdw_pad = jnp.zeros((32, D_MODEL), F32).at[:CONV_WIDTH].set(w_dw)
    return pl.pallas_call(
        _conv_core_kernel,
        grid=(n // tm,),
        in_specs=[
            pl.BlockSpec((tm, D_MODEL), lambda i: (i, 0)),
            pl.BlockSpec((CONV_HALO, D_MODEL), lambda i: (jnp.maximum(i * r - 1, 0), 0)),
            pl.BlockSpec((tm, D_MODEL), lambda i: (i, 0)),
            pl.BlockSpec((32, D_MODEL), lambda i: (0, 0)),
            pl.BlockSpec((1, D_MODEL), lambda i: (0, 0)),
            pl.BlockSpec((1, D_MODEL), lambda i: (0, 0)),
            pl.BlockSpec((1, D_MODEL), lambda i: (0, 0)),
            pl.BlockSpec((D_MODEL, D_MODEL), lambda i: (0, 0)),
            pl.BlockSpec((1, D_MODEL), lambda i: (0, 0)),
        ],
        out_specs=pl.BlockSpec((tm, D_MODEL), lambda i: (i, 0)),
        out_shape=jax.ShapeDtypeStruct((n, D_MODEL), F32),
        scratch_shapes=[pltpu.VMEM((D_MODEL // 128, CONV_HALO + tm, 128), F32),
                        pltpu.VMEM((D_MODEL // 128, tm, 128), F32)],
        compiler_params=_cparams(("arbitrary",)),
        name="conv_core",
    )(u, u, x, wdw_pad, b_dw, ln_g, ln_b, w2, b2)


def _router_kernel(x_ref, g_ref, wr_ref, br_ref, hp_ref, ri_ref, rw_ref, cnt_ref, carry_ref):
    t = T_ROUTE

    @pl.when(pl.program_id(0) == 0)
    def _():
        carry_ref[...] = jnp.zeros_like(carry_ref)

    h = _rms(x_ref[...], g_ref[...])
    hp_ref[...] = _pack_bf16_pairs(h)
    h_hi = h.astype(BF16)
    h_lo = (h - h_hi.astype(F32)).astype(BF16)
    w = wr_ref[...]
    w_hi = w.astype(BF16)
    w_lo = (w - w_hi.astype(F32)).astype(BF16)
    dn = (((1,), (1,)), ((), ()))
    p = lax.dot_general(jnp.concatenate([w_hi, w_lo], axis=0), h_hi, dn, preferred_element_type=F32)
    nr = wr_ref.shape[0]
    logits = p[0:nr] + p[nr:2 * nr] + lax.dot_general(w_hi, h_lo, dn, preferred_element_type=F32)
    logits = logits + br_ref[:, 0:1]

    best = logits[0:1]
    gi = jnp.zeros((1, t), I32)
    for j in range(1, MOE_GROUPS):
        r = logits[j:j + 1]
        up = r > best
        gi = jnp.where(up, j, gi)
        best = jnp.where(up, r, best)
    den = jnp.zeros((1, t), F32)
    for j in range(MOE_GROUPS):
        den = den + jnp.exp(logits[j:j + 1] - best)
    gate_g = 1.0 / den

    sel = logits[8:8 + MOE_EPG]
    for j in range(1, MOE_GROUPS):
        sel = jnp.where(gi == j, logits[8 + j * MOE_EPG:8 + (j + 1) * MOE_EPG], sel)

    m1 = sel[0:1]
    i1 = jnp.zeros((1, t), I32)
    for j in range(1, MOE_EPG):
        r = sel[j:j + 1]
        up = r > m1
        i1 = jnp.where(up, j, i1)
        m1 = jnp.where(up, r, m1)
    m2 = jnp.full((1, t), -jnp.inf, F32)
    i2 = jnp.zeros((1, t), I32)
    started = jnp.zeros((1, t), jnp.bool_)
    for j in range(MOE_EPG):
        r = sel[j:j + 1]
        ok = i1 != j
        up = ok & ((r > m2) | jnp.logical_not(started))
        i2 = jnp.where(up, j, i2)
        m2 = jnp.where(up, r, m2)
        started = started | ok
    e21 = jnp.exp(m2 - m1)
    p1 = 1.0 / (1.0 + e21)
    w1 = gate_g * p1
    w2 = gate_g * (e21 * p1)
    eid1 = gi * MOE_EPG + i1
    eid2 = gi * MOE_EPG + i2

    eio = lax.broadcasted_iota(I32, (MOE_EXPERTS, t), 0)
    oh1 = eio == eid1
    oh2 = eio == eid2
    oh = (oh1 | oh2).astype(F32)
    rio = lax.broadcasted_iota(I32, (t, t), 0)
    cio = lax.broadcasted_iota(I32, (t, t), 1)
    upper = (rio < cio).astype(BF16)
    cum = jnp.dot(oh.astype(BF16), upper, preferred_element_type=F32) + carry_ref[:, 0:1]
    rank1 = jnp.sum(jnp.where(oh1, cum, 0.0), axis=0, keepdims=True)
    rank2 = jnp.sum(jnp.where(oh2, cum, 0.0), axis=0, keepdims=True)
    carry_ref[...] = carry_ref[...] + jnp.sum(oh, axis=1, keepdims=True)
    cnt_ref[...] = carry_ref[...]

    zi = jnp.zeros((4, t), I32)
    ri_ref[...] = jnp.concatenate([eid1, eid2, rank1.astype(I32), rank2.astype(I32), zi], axis=0)
    zf = jnp.zeros((6, t), F32)
    rw_ref[...] = jnp.concatenate([w1, w2, zf], axis=0)


def _router(x, g, w_rg, b_rg, w_re, b_re):
    n = x.shape[0]
    t = T_ROUTE
    wr = jnp.zeros((40, D_MODEL), F32).at[0:MOE_GROUPS].set(w_rg.T).at[8:40].set(w_re.T)
    br = jnp.zeros((40,), F32).at[0:MOE_GROUPS].set(b_rg).at[8:40].set(b_re)
    br = jnp.broadcast_to(br[:, None], (40, 128))
    return pl.pallas_call(
        _router_kernel,
        grid=(n // t,),
        in_specs=[
            pl.BlockSpec((t, D_MODEL), lambda i: (i, 0)),
            pl.BlockSpec((1, D_MODEL), lambda i: (0, 0)),
            pl.BlockSpec((40, D_MODEL), lambda i: (0, 0)),
            pl.BlockSpec((40, 128), lambda i: (0, 0)),
        ],
        out_specs=[
            pl.BlockSpec((t, HALF), lambda i: (i, 0)),
            pl.BlockSpec((8, t), lambda i: (0, i)),
            pl.BlockSpec((8, t), lambda i: (0, i)),
            pl.BlockSpec((MOE_EXPERTS, 128), lambda i: (0, 0)),
        ],
        out_shape=[
            jax.ShapeDtypeStruct((n, HALF), U32),
            jax.ShapeDtypeStruct((8, n), I32),
            jax.ShapeDtypeStruct((8, n), F32),
            jax.ShapeDtypeStruct((MOE_EXPERTS, 128), F32),
        ],
        scratch_shapes=[pltpu.VMEM((MOE_EXPERTS, 128), F32)],
        compiler_params=_cparams(("arbitrary",)),
        name="moe_router",
    )(x, g, wr, br)


def _sc_mesh():
    return plsc.VectorSubcoreMesh(core_axis_name="c", subcore_axis_name="s",
                                  num_cores=SC_CORES, num_subcores=SC_SUBCORES)


def _sc_worker_id():
    return lax.axis_index("s") * SC_CORES + lax.axis_index("c")


def _sc_dispatch(hp, idx, zero_rows, total_rows):
    n = hp.shape[0]
    tpw = n // SC_WORKERS
    kd = tpw // SC_ROWS
    kp = idx.shape[1] - 2 * kd
    nb = SC_NBUF

    @functools.partial(
        pl.kernel, mesh=_sc_mesh(),
        out_type=jax.ShapeDtypeStruct((total_rows, HALF), U32),
        scratch_types=[pltpu.VMEM((2 * kd + kp, SC_ROWS), I32)]
        + [pltpu.VMEM((SC_ROWS, HALF), U32)] * (nb + 1)
        + [pltpu.SemaphoreType.DMA((nb,)), pltpu.SemaphoreType.DMA((nb,)), pltpu.SemaphoreType.DMA],
        name="moe_dispatch_sc",
    )
    def k(hp_hbm, idx_hbm, zero_hbm, xs_hbm, idx_v, *rest):
        bufs, zbuf = rest[:nb], rest[nb]
        load_sem, scat_sem, pad_sem = rest[nb + 1:]
        wid = _sc_worker_id()
        pltpu.sync_copy(idx_hbm.at[wid], idx_v)

        def load(c):
            return pltpu.make_async_copy(hp_hbm.at[pl.ds(wid * tpw + c * SC_ROWS, SC_ROWS)], bufs[c % nb],
                                         load_sem.at[c % nb])

        def scatters(c):
            return (pltpu.make_async_copy(bufs[c % nb], xs_hbm.at[idx_v.at[c]], scat_sem.at[c % nb]),
                    pltpu.make_async_copy(bufs[c % nb], xs_hbm.at[idx_v.at[kd + c]], scat_sem.at[c % nb]))

        for c in range(min(nb - 1, kd)):
            load(c).start()
        pltpu.sync_copy(zero_hbm, zbuf)
        pads = [pltpu.make_async_copy(zbuf, xs_hbm.at[idx_v.at[2 * kd + j]], pad_sem) for j in range(kp)]
        for p in pads:
            p.start()
        for c in range(kd):
            load(c).wait()
            for d in scatters(c):
                d.start()
            if c + nb - 1 < kd:
                if c >= 1:
                    for d in scatters(c - 1):
                        d.wait()
                load(c + nb - 1).start()
        for c in range(max(kd - nb, 0), kd):
            for d in scatters(c):
                d.wait()
        for p in pads:
            p.wait()

    return k(hp, idx, zero_rows)


def _sc_gather(ys, idx):
    kg = idx.shape[1]
    rows_per_worker = kg * SC_ROWS
    nb = SC_NBUF

    @functools.partial(
        pl.kernel, mesh=_sc_mesh(),
        out_type=jax.ShapeDtypeStruct((SC_WORKERS * rows_per_worker, HALF), U32),
        scratch_types=[pltpu.VMEM((kg, SC_ROWS), I32)] + [pltpu.VMEM((SC_ROWS, HALF), U32)] * nb
        + [pltpu.SemaphoreType.DMA((nb,)), pltpu.SemaphoreType.DMA((nb,))],
        name="moe_gather_sc",
    )
    def k(ys_hbm, idx_hbm, yg_hbm, idx_v, *rest):
        bufs = rest[:nb]
        gat_sem, out_sem = rest[nb:]
        wid = _sc_worker_id()
        pltpu.sync_copy(idx_hbm.at[wid], idx_v)

        def gather(c):
            return pltpu.make_async_copy(ys_hbm.at[idx_v.at[c]], bufs[c % nb], gat_sem.at[c % nb])

        def store(c):
            return pltpu.make_async_copy(bufs[c % nb],
                                         yg_hbm.at[pl.ds(wid * rows_per_worker + c * SC_ROWS, SC_ROWS)],
                                         out_sem.at[c % nb])

        for c in range(min(nb - 1, kg)):
            gather(c).start()
        for c in range(kg):
            gather(c).wait()
            store(c).start()
            if c + nb - 1 < kg:
                if c >= 1:
                    store(c - 1).wait()
                gather(c + nb - 1).start()
        for c in range(max(kg - nb, 0), kg):
            store(c).wait()

    return k(ys, idx)


def _expert_kernel(blk_e_ref, nused_ref, first_ref, slot_ref, nxt_ref, xs_ref, wg_hbm, wu_hbm, wd_hbm,
                   ys_ref, wg_buf, wu_buf, wd_buf, sems, *, layer):
    step = pl.program_id(0)
    tm = TM_EXP
    nused = nused_ref[0]

    def weight_copies(e, s):
        return (pltpu.make_async_copy(wg_hbm.at[layer, e], wg_buf.at[s], sems.at[s, 0]),
                pltpu.make_async_copy(wu_hbm.at[layer, e], wu_buf.at[s], sems.at[s, 1]),
                pltpu.make_async_copy(wd_hbm.at[layer, e], wd_buf.at[s], sems.at[s, 2]))

    def dma_control(j):
        i = step * EXP_SUB + j

        @pl.when(i < nused)
        def _():
            s = slot_ref[i]

            if j == 0:
                @pl.when(i == 0)
                def _():
                    for c in weight_copies(blk_e_ref[0], 0):
                        c.start()

            @pl.when(first_ref[i] == 1)
            def _():
                for c in weight_copies(blk_e_ref[i], s):
                    c.wait()

                @pl.when(nxt_ref[i] >= 0)
                def _():
                    for c in weight_copies(nxt_ref[i], lax.rem(s + 1, EXP_SLOTS)):
                        c.start()

    for j in range(EXP_SUB):
        dma_control(j)

    @pl.when(step * EXP_SUB < nused)
    def _():
        for j in range(EXP_SUB):
            i = step * EXP_SUB + j
            rows = slice(j * tm, (j + 1) * tm)
            s = slot_ref[i]
            lo, hi = _unpack_bf16_pairs(xs_ref[rows, :])
            xf = jnp.concatenate([lo, hi], axis=1)
            a = jnp.dot(xf, wg_buf[s], preferred_element_type=F32)
            b = jnp.dot(xf, wu_buf[s], preferred_element_type=F32)
            hm = _silu(a) * b
            y = jnp.dot(hm, wd_buf[s], preferred_element_type=F32)
            ys_ref[rows, :] = jnp.where(i < nused, _pack_bf16_pairs(y), jnp.uint32(0))

    @pl.when(step * EXP_SUB >= nused)
    def _():
        ys_ref[...] = jnp.zeros_like(ys_ref)


def _experts(xs, blk_e, nused, w_gate, w_up, w_down, layer):
    tm = TM_EXP
    p_rows = xs.shape[0] - MOE_EXPERTS * tm
    nblk = p_rows // tm
    pos = jnp.arange(nblk, dtype=I32)
    valid = pos < nused[0]
    prev_e = jnp.concatenate([jnp.full((1,), -1, I32), blk_e[:-1]])
    first = valid & (blk_e != prev_e)
    slot = jnp.maximum(jnp.cumsum(first.astype(I32)) - 1, 0) % EXP_SLOTS
    first_pos = jnp.where(first, pos, nblk)
    next_first = jnp.concatenate([lax.cummin(first_pos, reverse=True)[1:], jnp.full((1,), nblk, I32)])
    nxt = jnp.where(next_first < nblk, blk_e[jnp.minimum(next_first, nblk - 1)], -1)

    def blk(i, be, nu, *_):
        return jnp.minimum(i, (nu[0] - 1) // EXP_SUB)

    grid_spec = pltpu.PrefetchScalarGridSpec(
        num_scalar_prefetch=5,
        grid=(nblk // EXP_SUB,),
        in_specs=[
            pl.BlockSpec((EXP_SUB * tm, HALF), lambda i, *sp: (blk(i, *sp), 0)),
            pl.BlockSpec(memory_space=pl.ANY),
            pl.BlockSpec(memory_space=pl.ANY),
            pl.BlockSpec(memory_space=pl.ANY),
        ],
        out_specs=pl.BlockSpec((EXP_SUB * tm, HALF), lambda i, *sp: (i, 0)),
        scratch_shapes=[
            pltpu.VMEM((EXP_SLOTS, D_MODEL, MOE_FF), F32),
            pltpu.VMEM((EXP_SLOTS, D_MODEL, MOE_FF), F32),
            pltpu.VMEM((EXP_SLOTS, MOE_FF, D_MODEL), F32),
            pltpu.SemaphoreType.DMA((EXP_SLOTS, 3)),
        ],
    )
    return pl.pallas_call(
        functools.partial(_expert_kernel, layer=layer),
        grid_spec=grid_spec,
        out_shape=jax.ShapeDtypeStruct((p_rows, HALF), U32),
        compiler_params=_cparams(("arbitrary",)),
        name="moe_experts",
    )(blk_e, nused, first.astype(I32), slot.astype(I32), nxt.astype(I32), xs, w_gate, w_up, w_down)


def _combine_kernel(x_ref, rw_ref, fg_ref, y1_ref, y2_ref, out_ref):
    lo1, hi1 = _unpack_bf16_pairs(y1_ref[...])
    lo2, hi2 = _unpack_bf16_pairs(y2_ref[...])
    w1, w2 = _route_weight_columns(rw_ref[...])
    x = x_ref[...]
    o_lo = x[:, :HALF] + w1 * lo1 + w2 * lo2
    o_hi = x[:, HALF:] + w1 * hi1 + w2 * hi2
    ms = (jnp.sum(o_lo * o_lo, axis=-1, keepdims=True)
          + jnp.sum(o_hi * o_hi, axis=-1, keepdims=True)) * (1.0 / D_MODEL)
    sc = lax.rsqrt(ms + NORM_EPS)
    o_lo = o_lo * sc * fg_ref[:, :HALF]
    o_hi = o_hi * sc * fg_ref[:, HALF:]
    out_ref[:, :HALF] = o_lo
    out_ref[:, HALF:] = o_hi


def _combine(yg, x, rw, final_g):
    n = x.shape[0]
    td = T_COMB
    nb = n // td
    return pl.pallas_call(
        _combine_kernel,
        grid=(nb,),
        in_specs=[
            pl.BlockSpec((td, D_MODEL), lambda i: (i, 0)),
            pl.BlockSpec((8, td), lambda i: (0, i)),
            pl.BlockSpec((1, D_MODEL), lambda i: (0, 0)),
            pl.BlockSpec((td, HALF), lambda i: (i, 0)),
            pl.BlockSpec((td, HALF), lambda i: (i + nb, 0)),
        ],
        out_specs=pl.BlockSpec((td, D_MODEL), lambda i: (i, 0)),
        out_shape=jax.ShapeDtypeStruct((n, D_MODEL), F32),
        compiler_params=_cparams(("arbitrary",)),
        name="moe_combine",
    )(x, rw, final_g, yg, yg)


def _moe(x, g, w_rg, b_rg, w_re, b_re, w_gate, w_up, w_down, layer):
    n = x.shape[0]
    tm = TM_EXP
    p_rows = 2 * n + MOE_EXPERTS * tm
    nblk = p_rows // tm
    hp, ri, rw, cnt = _router(x, g, w_rg, b_rg, w_re, b_re)
    counts = cnt[:, 0].astype(I32)
    pcounts = (counts + tm - 1) // tm * tm
    pend = jnp.cumsum(pcounts)
    pstart = pend - pcounts
    eio = jnp.arange(MOE_EXPERTS, dtype=I32)[:, None]
    dest1 = jnp.sum(jnp.where(ri[0][None, :] == eio, pstart[:, None], 0), axis=0) + ri[2]
    dest2 = jnp.sum(jnp.where(ri[1][None, :] == eio, pstart[:, None], 0), axis=0) + ri[3]
    blk_start = jnp.arange(nblk, dtype=I32) * tm
    blk_e = jnp.minimum(jnp.sum((pend[None, :] <= blk_start[:, None]).astype(I32), axis=1), MOE_EXPERTS - 1)
    nused = jnp.maximum(pend[-1] // tm, 1).astype(I32).reshape(1)
    r = jnp.arange(tm, dtype=I32)[None, :]
    pad_slot = jnp.where(r < (pcounts - counts)[:, None], (pstart + counts)[:, None] + r, p_rows + eio * tm + r)
    kd = n // SC_WORKERS // SC_ROWS
    idx = jnp.concatenate([dest1.reshape(SC_WORKERS, kd, SC_ROWS), dest2.reshape(SC_WORKERS, kd, SC_ROWS),
                           pad_slot.reshape(SC_WORKERS, -1, SC_ROWS)], axis=1)
    zero_rows = jnp.zeros((SC_ROWS, HALF), U32)
    xs = _sc_dispatch(hp, idx, zero_rows, p_rows + MOE_EXPERTS * tm)
    ys = _experts(xs, blk_e, nused, w_gate, w_up, w_down, layer)
    gidx = jnp.concatenate([dest1, dest2]).reshape(SC_WORKERS, -1, SC_ROWS)
    yg = _sc_gather(ys, gidx)
    return yg, rw


def kernel(x, positions, norm_mix_g, norm_ffn_g, ret_w_in, ret_head_g, ret_w_out, conv_w_pw1, conv_b_pw1, conv_w_dw, conv_b_dw, conv_ln_g, conv_ln_b, conv_w_pw2, conv_b_pw2, moe_w_rg, moe_b_rg, moe_w_re, moe_b_re, moe_w_gate, moe_w_up, moe_w_down, final_norm_g):
    b, s, d = x.shape
    n = b * s
    xt = x.reshape(n, d)
    pos = positions.reshape(n, 1)
    fg = final_norm_g.reshape(1, d)

    q, k, v, gate = _ret_inproj(xt, pos, norm_mix_g[0].reshape(1, d), ret_w_in[0])
    xt = _ret_core(q, k, v, gate, xt, ret_head_g[0].reshape(RET_V, 1), ret_w_out[0])
    yg, rw = _moe(xt, norm_ffn_g[0].reshape(1, d), moe_w_rg[0], moe_b_rg[0], moe_w_re[0], moe_b_re[0],
                  moe_w_gate, moe_w_up, moe_w_down, 0)

    xt, u = _conv_pw1(xt, rw, yg, norm_mix_g[1].reshape(1, d), conv_w_pw1[0],
                      conv_b_pw1[0].reshape(1, 2 * d))
    xt = _conv_core(u, xt, conv_w_dw[0], conv_b_dw[0].reshape(1, d), conv_ln_g[0].reshape(1, d),
                    conv_ln_b[0].reshape(1, d), conv_w_pw2[0], conv_b_pw2[0].reshape(1, d))
    yg, rw = _moe(xt, norm_ffn_g[1].reshape(1, d), moe_w_rg[1], moe_b_rg[1], moe_w_re[1], moe_b_re[1],
                  moe_w_gate, moe_w_up, moe_w_down, 1)
    xt = _combine(yg, xt, rw, fg)
    return xt.reshape(b, s, d)
```

```python
import functools

import jax
import jax.numpy as jnp
from jax import lax
from jax.experimental import pallas as pl
from jax.experimental.pallas import tpu as pltpu
from jax.experimental.pallas import tpu_sc as plsc

F32 = jnp.float32
BF16 = jnp.bfloat16
U32 = jnp.uint32
I32 = jnp.int32

D_MODEL = 1024
RET_HEADS = 4
RET_DK = 256
RET_DV = 512
RET_QK = RET_HEADS * RET_DK
RET_V = RET_HEADS * RET_DV
ROPE_BASE = 10000.0
CONV_WIDTH = 31
MOE_GROUPS = 4
MOE_EPG = 8
MOE_EXPERTS = MOE_GROUPS * MOE_EPG
MOE_FF = 512
NORM_EPS = 1e-6

TM_PROJ = 512
RET_C = 256
RET_STEP = 512
TM_CONV = 1024
TM_PW1 = 1024
PW1_ROWS = 256
CONV_HALO = 32
CONV_ROWS = 128
CONV_STRIDE = 4
T_ROUTE = 512
TM_EXP = 256
EXP_SUB = 4
EXP_SLOTS = EXP_SUB + 1
T_COMB = 1024
SC_CORES = 2
SC_SUBCORES = 16
SC_WORKERS = SC_CORES * SC_SUBCORES
SC_ROWS = 32
SC_NBUF = 4
HALF = D_MODEL // 2

VMEM_LIMIT = 56 * 1024 * 1024


def _cparams(sem, flags=None):
    return pltpu.CompilerParams(dimension_semantics=sem, vmem_limit_bytes=VMEM_LIMIT, flags=flags)


def _rms(x, g):
    ms = jnp.mean(x * x, axis=-1, keepdims=True)
    return x * lax.rsqrt(ms + NORM_EPS) * g


def _silu(x):
    return x * (1.0 / (1.0 + jnp.exp(-x)))


def _pack_bf16_pairs(y):
    lo = pltpu.bitcast(y[:, :HALF].astype(BF16).astype(F32), U32)
    hi = pltpu.bitcast(y[:, HALF:].astype(BF16).astype(F32), U32)
    return (hi & jnp.uint32(0xFFFF0000)) | (lo >> 16)


def _route_weight_columns(rw):
    t = jnp.concatenate([rw] * 16, axis=0).T
    return t[:, 0:1], t[:, 1:2]


def _unpack_bf16_pairs(p):
    lo = pltpu.bitcast(p << 16, F32)
    hi = pltpu.bitcast(p & jnp.uint32(0xFFFF0000), F32)
    return lo, hi


def _ret_inproj_kernel(x_ref, pos_ref, g_ref, inv_ref, w_ref, q_ref, k_ref, v_ref, gate_ref):
    half = RET_DK // 2
    kscale = RET_DK ** -0.5
    h = _rms(x_ref[...], g_ref[...]).astype(BF16)

    def proj(c0, width):
        return jnp.dot(h, w_ref[:, c0:c0 + width].astype(BF16), preferred_element_type=F32)

    v0 = proj(2 * RET_QK, 512)
    v_ref[:, 0:512] = v0.astype(BF16)
    zero = ((pltpu.bitcast(v0[:, 0:half], U32) >> 16) >> 16).astype(F32)
    ang = pos_ref[...].astype(F32) * inv_ref[...] + zero
    cos = jnp.cos(ang)
    sin = jnp.sin(ang)
    for j in range(RET_V // 512):
        if j > 0:
            v_ref[:, j * 512:(j + 1) * 512] = proj(2 * RET_QK + j * 512, 512).astype(BF16)
        gate_ref[:, j * 512:(j + 1) * 512] = proj(2 * RET_QK + RET_V + j * 512, 512).astype(BF16)

    for hd in range(RET_HEADS):
        for base, out_ref, cs, sn in ((0, q_ref, cos, sin), (RET_QK, k_ref, cos * kscale, sin * kscale)):
            t = proj(base + hd * RET_DK, RET_DK)
            t1 = t[:, :half]
            t2 = t[:, half:]
            out_ref[:, hd * RET_DK:hd * RET_DK + half] = (t1 * cs - t2 * sn).astype(BF16)
            out_ref[:, hd * RET_DK + half:(hd + 1) * RET_DK] = (t1 * sn + t2 * cs).astype(BF16)


def _ret_inproj(x, pos, g, w_in):
    n = x.shape[0]
    half = RET_DK // 2
    inv = (ROPE_BASE ** (-jnp.arange(half, dtype=F32) / half)).reshape(1, half)
    tm = TM_PROJ
    return pl.pallas_call(
        _ret_inproj_kernel,
        grid=(n // tm,),
        in_specs=[
            pl.BlockSpec((tm, D_MODEL), lambda i: (i, 0)),
            pl.BlockSpec((tm, 1), lambda i: (i, 0)),
            pl.BlockSpec((1, D_MODEL), lambda i: (0, 0)),
            pl.BlockSpec((1, half), lambda i: (0, 0)),
            pl.BlockSpec(w_in.shape, lambda i: (0, 0), pipeline_mode=pl.Buffered(1)),
        ],
        out_specs=[
            pl.BlockSpec((tm, RET_QK), lambda i: (i, 0)),
            pl.BlockSpec((tm, RET_QK), lambda i: (i, 0)),
            pl.BlockSpec((tm, RET_V), lambda i: (i, 0)),
            pl.BlockSpec((tm, RET_V), lambda i: (i, 0)),
        ],
        out_shape=[
            jax.ShapeDtypeStruct((n, RET_QK), BF16),
            jax.ShapeDtypeStruct((n, RET_QK), BF16),
            jax.ShapeDtypeStruct((n, RET_V), BF16),
            jax.ShapeDtypeStruct((n, RET_V), BF16),
        ],
        compiler_params=_cparams(("arbitrary",)),
        name="ret_inproj",
    )(x, pos, g, inv, w_in)


def _ret_core_kernel(cdec_ref, q_ref, k_ref, v_ref, gate_ref, x_ref, hg_ref, intra_ref, cross_ref,
                     kdec_ref, wo_ref, out_ref, state_ref, y_ref, wos_ref):
    @pl.when(pl.program_id(0) == 0)
    def _():
        state_ref[...] = jnp.zeros_like(state_ref)
        wos_ref[...] = (wo_ref[...] * hg_ref[...]).astype(BF16)

    for r0 in range(0, RET_STEP, RET_C):
        rs = slice(r0, r0 + RET_C)
        for hd in range(RET_HEADS):
            q = q_ref[rs, hd * RET_DK:(hd + 1) * RET_DK]
            k = k_ref[rs, hd * RET_DK:(hd + 1) * RET_DK]
            v = v_ref[rs, hd * RET_DV:(hd + 1) * RET_DV]
            state = state_ref[hd]
            scores = lax.dot_general(q, k, (((1,), (1,)), ((), ())), preferred_element_type=F32)
            scores = (scores * intra_ref[hd]).astype(BF16)
            o = jnp.dot(scores, v, preferred_element_type=F32)
            cross = cross_ref[hd]
            o_cross = jnp.dot(q, state.astype(BF16), preferred_element_type=F32)
            o = o + o_cross * jnp.concatenate([cross] * (RET_DV // 128), axis=1)
            kdec = kdec_ref[hd]
            kd = (k.astype(F32) * jnp.concatenate([kdec] * (RET_DK // 128), axis=1)).astype(BF16)
            upd = lax.dot_general(kd, v, (((0,), (0,)), ((), ())), preferred_element_type=F32)
            state_ref[hd] = state * cdec_ref[hd] + upd
            ms = jnp.mean(o * o, axis=-1, keepdims=True)
            on = o * lax.rsqrt(ms + NORM_EPS)
            gt = gate_ref[rs, hd * RET_DV:(hd + 1) * RET_DV].astype(F32)
            y_ref[rs, hd * RET_DV:(hd + 1) * RET_DV] = (_silu(gt) * on).astype(BF16)
        out_ref[rs, :] = x_ref[rs, :] + jnp.dot(y_ref[rs, :], wos_ref[...], preferred_element_type=F32)


def _ret_core(q, k, v, gate, x, head_g_col, w_out):
    n = x.shape[0]
    c = RET_C
    log_gamma = jnp.log1p(-(2.0 ** (-5.0 - jnp.arange(RET_HEADS, dtype=F32))))
    idx = jnp.arange(c, dtype=F32)
    diff = idx[:, None] - idx[None, :]
    intra = jnp.where(diff >= 0, jnp.exp(log_gamma[:, None, None] * jnp.maximum(diff, 0.0)), 0.0)
    cross = jnp.broadcast_to(jnp.exp(log_gamma[:, None] * (idx + 1.0))[:, :, None], (RET_HEADS, c, 128))
    kdec = jnp.broadcast_to(jnp.exp(log_gamma[:, None] * (c - 1.0 - idx))[:, :, None], (RET_HEADS, c, 128))
    cdec = jnp.exp(log_gamma * c)
    return pl.pallas_call(
        _ret_core_kernel,
        grid=(n // RET_STEP,),
        in_specs=[
            pl.BlockSpec(memory_space=pltpu.SMEM),
            pl.BlockSpec((RET_STEP, RET_QK), lambda i: (i, 0)),
            pl.BlockSpec((RET_STEP, RET_QK), lambda i: (i, 0)),
            pl.BlockSpec((RET_STEP, RET_V), lambda i: (i, 0)),
            pl.BlockSpec((RET_STEP, RET_V), lambda i: (i, 0)),
            pl.BlockSpec((RET_STEP, D_MODEL), lambda i: (i, 0)),
            pl.BlockSpec((RET_V, 1), lambda i: (0, 0)),
            pl.BlockSpec((RET_HEADS, c, c), lambda i: (0, 0, 0)),
            pl.BlockSpec((RET_HEADS, c, 128), lambda i: (0, 0, 0)),
            pl.BlockSpec((RET_HEADS, c, 128), lambda i: (0, 0, 0)),
            pl.BlockSpec((RET_V, D_MODEL), lambda i: (0, 0)),
        ],
        out_specs=pl.BlockSpec((RET_STEP, D_MODEL), lambda i: (i, 0)),
        out_shape=jax.ShapeDtypeStruct((n, D_MODEL), F32),
        scratch_shapes=[
            pltpu.VMEM((RET_HEADS, RET_DK, RET_DV), F32),
            pltpu.VMEM((RET_STEP, RET_V), BF16),
            pltpu.VMEM((RET_V, D_MODEL), BF16),
        ],
        compiler_params=_cparams(("arbitrary",)),
        name="ret_core",
    )(cdec, q, k, v, gate, x, head_g_col, intra, cross, kdec, w_out)


def _conv_pw1_kernel(x_ref, rw_ref, y1_ref, y2_ref, g_ref, w_ref, b_ref, xo_ref, u_ref):
    r = PW1_ROWS
    ngroups = TM_PW1 // r
    w1, w2 = _route_weight_columns(rw_ref[...])

    def prologue(rs):
        lo1, hi1 = _unpack_bf16_pairs(y1_ref[rs, :])
        lo2, hi2 = _unpack_bf16_pairs(y2_ref[rs, :])
        x = jnp.concatenate([x_ref[rs, :HALF] + w1[rs] * lo1 + w2[rs] * lo2,
                             x_ref[rs, HALF:] + w1[rs] * hi1 + w2[rs] * hi2], axis=1)
        xo_ref[rs, :] = x
        hf = _rms(x, g_ref[...])
        return hf.astype(BF16), hf[:, 0:128]

    groups = [slice(gi * r, (gi + 1) * r) for gi in range(ngroups)]
    hs = [prologue(rs) for rs in groups]
    for j in range(D_MODEL // 512):
        wa = w_ref[:, j * 512:(j + 1) * 512].astype(BF16)
        wg = w_ref[:, D_MODEL + j * 512:D_MODEL + (j + 1) * 512].astype(BF16)
        for gi, rs in enumerate(groups):
            h = hs[gi][0]
            a = jnp.dot(h, wa, preferred_element_type=F32) + b_ref[:, j * 512:(j + 1) * 512]
            gt = jnp.dot(h, wg, preferred_element_type=F32) + b_ref[:, D_MODEL + j * 512:D_MODEL + (j + 1) * 512]
            u = a * (1.0 / (1.0 + jnp.exp(-gt)))
            if j == 0 and gi + 1 < ngroups:
                zero = ((pltpu.bitcast(hs[gi + 1][1], U32) >> 16) >> 16).astype(F32)
                u_ref[rs, 0:128] = u[:, 0:128] + zero
                u_ref[rs, 128:512] = u[:, 128:512]
            else:
                u_ref[rs, j * 512:(j + 1) * 512] = u


def _conv_pw1(x, rw, yg, g, w, b):
    n = x.shape[0]
    tm = TM_PW1
    nb = n // tm
    return pl.pallas_call(
        _conv_pw1_kernel,
        grid=(nb,),
        in_specs=[
            pl.BlockSpec((tm, D_MODEL), lambda i: (i, 0)),
            pl.BlockSpec((8, tm), lambda i: (0, i)),
            pl.BlockSpec((tm, HALF), lambda i: (i, 0)),
            pl.BlockSpec((tm, HALF), lambda i: (i + nb, 0)),
            pl.BlockSpec((1, D_MODEL), lambda i: (0, 0)),
            pl.BlockSpec((D_MODEL, 2 * D_MODEL), lambda i: (0, 0), pipeline_mode=pl.Buffered(1)),
            pl.BlockSpec((1, 2 * D_MODEL), lambda i: (0, 0)),
        ],
        out_specs=[pl.BlockSpec((tm, D_MODEL), lambda i: (i, 0)), pl.BlockSpec((tm, D_MODEL), lambda i: (i, 0))],
        out_shape=[jax.ShapeDtypeStruct((n, D_MODEL), F32), jax.ShapeDtypeStruct((n, D_MODEL), F32)],
        compiler_params=_cparams(("arbitrary",)),
        name="conv_pw1",
    )(x, rw, yg, yg, g, w, b)


def _conv_core_kernel(u_ref, halo_ref, x_ref, wdw_ref, bdw_ref, lng_ref, lnb_ref, w2_ref, b2_ref,
                      out_ref, win_ref, z_ref):
    tm = TM_CONV
    first = pl.program_id(0) == 0
    halo = halo_ref[...]
    halo = jnp.where(first, jnp.zeros_like(halo), halo)
    nslab = D_MODEL // 128
    for cc in range(nslab):
        cs = slice(cc * 128, (cc + 1) * 128)
        win_ref[cc, 0:CONV_HALO, :] = halo[:, cs]
        win_ref[cc, CONV_HALO:CONV_HALO + tm, :] = u_ref[:, cs]
    off = CONV_HALO - (CONV_WIDTH - 1)
    rb = CONV_ROWS
    st = CONV_STRIDE
    for cc in range(nslab):
        cs = slice(cc * 128, (cc + 1) * 128)
        for r0 in range(0, tm, rb):
            accs = [bdw_ref[:, cs]] * st
            for o in range(CONV_WIDTH):
                w_o = wdw_ref[o:o + 1, cs]
                for rho in range(st):
                    accs[rho] = accs[rho] + win_ref[cc, pl.ds(r0 + rho + off + o, rb // st, stride=st), :] * w_o
            for rho in range(st):
                z_ref[cc, pl.ds(r0 + rho, rb // st, stride=st), :] = accs[rho]
    z = jnp.concatenate([z_ref[cc] for cc in range(nslab)], axis=1)
    mu = jnp.mean(z, axis=-1, keepdims=True)
    zc = z - mu
    var = jnp.mean(zc * zc, axis=-1, keepdims=True)
    zn = zc * lax.rsqrt(var + NORM_EPS) * lng_ref[...] + lnb_ref[...]
    y = _silu(zn).astype(BF16)
    out_ref[...] = x_ref[...] + jnp.dot(y, w2_ref[...].astype(BF16), preferred_element_type=F32) + b2_ref[...]


def _conv_core(u, x, w_dw, b_dw, ln_g, ln_b, w2, b2):
    n = x.shape[0]
    tm = TM_CONV
    r = tm // CONV_HALO
    wdw_pad = jnp.zeros((32, D_MODEL), F32).at[:CONV_WIDTH].set(w_dw)
    return pl.pallas_call(
        _conv_core_kernel,
        grid=(n // tm,),
        in_specs=[
            pl.BlockSpec((tm, D_MODEL), lambda i: (i, 0)),
            pl.BlockSpec((CONV_HALO, D_MODEL), lambda i: (jnp.maximum(i * r - 1, 0), 0)),
            pl.BlockSpec((tm, D_MODEL), lambda i: (i, 0)),
            pl.BlockSpec((32, D_MODEL), lambda i: (0, 0)),
            pl.BlockSpec((1, D_MODEL), lambda i: (0, 0)),
            pl.BlockSpec((1, D_MODEL), lambda i: (0, 0)),
            pl.BlockSpec((1, D_MODEL), lambda i: (0, 0)),
            pl.BlockSpec((D_MODEL, D_MODEL), lambda i: (0, 0)),
            pl.BlockSpec((1, D_MODEL), lambda i: (0, 0)),
        ],
        out_specs=pl.BlockSpec((tm, D_MODEL), lambda i: (i, 0)),
        out_shape=jax.ShapeDtypeStruct((n, D_MODEL), F32),
        scratch_shapes=[pltpu.VMEM((D_MODEL // 128, CONV_HALO + tm, 128), F32),
                        pltpu.VMEM((D_MODEL // 128, tm, 128), F32)],
        compiler_params=_cparams(("arbitrary",)),
        name="conv_core",
    )(u, u, x, wdw_pad, b_dw, ln_g, ln_b, w2, b2)


def _router_kernel(x_ref, g_ref, wr_ref, br_ref, hp_ref, ri_ref, rw_ref, cnt_ref, carry_ref):
    t = T_ROUTE

    @pl.when(pl.program_id(0) == 0)
    def _():
        carry_ref[...] = jnp.zeros_like(carry_ref)

    h = _rms(x_ref[...], g_ref[...])
    hp_ref[...] = _pack_bf16_pairs(h)
    h_hi = h.astype(BF16)
    h_lo = (h - h_hi.astype(F32)).astype(BF16)
    w = wr_ref[...]
    w_hi = w.astype(BF16)
    w_lo = (w - w_hi.astype(F32)).astype(BF16)
    dn = (((1,), (1,)), ((), ()))
    p = lax.dot_general(jnp.concatenate([w_hi, w_lo], axis=0), h_hi, dn, preferred_element_type=F32)
    nr = wr_ref.shape[0]
    logits = p[0:nr] + p[nr:2 * nr] + lax.dot_general(w_hi, h_lo, dn, preferred_element_type=F32)
    logits = logits + br_ref[:, 0:1]

    best = logits[0:1]
    gi = jnp.zeros((1, t), I32)
    for j in range(1, MOE_GROUPS):
        r = logits[j:j + 1]
        up = r > best
        gi = jnp.where(up, j, gi)
        best = jnp.where(up, r, best)
    den = jnp.zeros((1, t), F32)
    for j in range(MOE_GROUPS):
        den = den + jnp.exp(logits[j:j + 1] - best)
    gate_g = 1.0 / den

    sel = logits[8:8 + MOE_EPG]
    for j in range(1, MOE_GROUPS):
        sel = jnp.where(gi == j, logits[8 + j * MOE_EPG:8 + (j + 1) * MOE_EPG], sel)

    m1 = sel[0:1]
    i1 = jnp.zeros((1, t), I32)
    for j in range(1, MOE_EPG):
        r = sel[j:j + 1]
        up = r > m1
        i1 = jnp.where(up, j, i1)
        m1 = jnp.where(up, r, m1)
    m2 = jnp.full((1, t), -jnp.inf, F32)
    i2 = jnp.zeros((1, t), I32)
    started = jnp.zeros((1, t), jnp.bool_)
    for j in range(MOE_EPG):
        r = sel[j:j + 1]
        ok = i1 != j
        up = ok & ((r > m2) | jnp.logical_not(started))
        i2 = jnp.where(up, j, i2)
        m2 = jnp.where(up, r, m2)
        started = started | ok
    e21 = jnp.exp(m2 - m1)
    p1 = 1.0 / (1.0 + e21)
    w1 = gate_g * p1
    w2 = gate_g * (e21 * p1)
    eid1 = gi * MOE_EPG + i1
    eid2 = gi * MOE_EPG + i2

    eio = lax.broadcasted_iota(I32, (MOE_EXPERTS, t), 0)
    oh1 = eio == eid1
    oh2 = eio == eid2
    oh = (oh1 | oh2).astype(F32)
    rio = lax.broadcasted_iota(I32, (t, t), 0)
    cio = lax.broadcasted_iota(I32, (t, t), 1)
    upper = (rio < cio).astype(BF16)
    cum = jnp.dot(oh.astype(BF16), upper, preferred_element_type=F32) + carry_ref[:, 0:1]
    rank1 = jnp.sum(jnp.where(oh1, cum, 0.0), axis=0, keepdims=True)
    rank2 = jnp.sum(jnp.where(oh2, cum, 0.0), axis=0, keepdims=True)
    carry_ref[...] = carry_ref[...] + jnp.sum(oh, axis=1, keepdims=True)
    cnt_ref[...] = carry_ref[...]

    zi = jnp.zeros((4, t), I32)
    ri_ref[...] = jnp.concatenate([eid1, eid2, rank1.astype(I32), rank2.astype(I32), zi], axis=0)
    zf = jnp.zeros((6, t), F32)
    rw_ref[...] = jnp.concatenate([w1, w2, zf], axis=0)


def _router(x, g, w_rg, b_rg, w_re, b_re):
    n = x.shape[0]
    t = T_ROUTE
    wr = jnp.zeros((40, D_MODEL), F32).at[0:MOE_GROUPS].set(w_rg.T).at[8:40].set(w_re.T)
    br = jnp.zeros((40,), F32).at[0:MOE_GROUPS].set(b_rg).at[8:40].set(b_re)
    br = jnp.broadcast_to(br[:, None], (40, 128))
    return pl.pallas_call(
        _router_kernel,
        grid=(n // t,),
        in_specs=[
            pl.BlockSpec((t, D_MODEL), lambda i: (i, 0)),
            pl.BlockSpec((1, D_MODEL), lambda i: (0, 0)),
            pl.BlockSpec((40, D_MODEL), lambda i: (0, 0)),
            pl.BlockSpec((40, 128), lambda i: (0, 0)),
        ],
        out_specs=[
            pl.BlockSpec((t, HALF), lambda i: (i, 0)),
            pl.BlockSpec((8, t), lambda i: (0, i)),
            pl.BlockSpec((8, t), lambda i: (0, i)),
            pl.BlockSpec((MOE_EXPERTS, 128), lambda i: (0, 0)),
        ],
        out_shape=[
            jax.ShapeDtypeStruct((n, HALF), U32),
            jax.ShapeDtypeStruct((8, n), I32),
            jax.ShapeDtypeStruct((8, n), F32),
            jax.ShapeDtypeStruct((MOE_EXPERTS, 128), F32),
        ],
        scratch_shapes=[pltpu.VMEM((MOE_EXPERTS, 128), F32)],
        compiler_params=_cparams(("arbitrary",)),
        name="moe_router",
    )(x, g, wr, br)


def _sc_mesh():
    return plsc.VectorSubcoreMesh(core_axis_name="c", subcore_axis_name="s",
                                  num_cores=SC_CORES, num_subcores=SC_SUBCORES)


def _sc_worker_id():
    return lax.axis_index("s") * SC_CORES + lax.axis_index("c")


def _sc_dispatch(hp, idx, zero_rows, total_rows):
    n = hp.shape[0]
    tpw = n // SC_WORKERS
    kd = tpw // SC_ROWS
    kp = idx.shape[1] - 2 * kd
    nb = SC_NBUF

    @functools.partial(
        pl.kernel, mesh=_sc_mesh(),
        out_type=jax.ShapeDtypeStruct((total_rows, HALF), U32),
        scratch_types=[pltpu.VMEM((2 * kd + kp, SC_ROWS), I32)]
        + [pltpu.VMEM((SC_ROWS, HALF), U32)] * (nb + 1)
        + [pltpu.SemaphoreType.DMA((nb,)), pltpu.SemaphoreType.DMA((nb,)), pltpu.SemaphoreType.DMA],
        name="moe_dispatch_sc",
    )
    def k(hp_hbm, idx_hbm, zero_hbm, xs_hbm, idx_v, *rest):
        bufs, zbuf = rest[:nb], rest[nb]
        load_sem, scat_sem, pad_sem = rest[nb + 1:]
        wid = _sc_worker_id()
        pltpu.sync_copy(idx_hbm.at[wid], idx_v)

        def load(c):
            return pltpu.make_async_copy(hp_hbm.at[pl.ds(wid * tpw + c * SC_ROWS, SC_ROWS)], bufs[c % nb],
                                         load_sem.at[c % nb])

        def scatters(c):
            return (pltpu.make_async_copy(bufs[c % nb], xs_hbm.at[idx_v.at[c]], scat_sem.at[c % nb]),
                    pltpu.make_async_copy(bufs[c % nb], xs_hbm.at[idx_v.at[kd + c]], scat_sem.at[c % nb]))

        for c in range(min(nb - 1, kd)):
            load(c).start()
        pltpu.sync_copy(zero_hbm, zbuf)
        pads = [pltpu.make_async_copy(zbuf, xs_hbm.at[idx_v.at[2 * kd + j]], pad_sem) for j in range(kp)]
        for p in pads:
            p.start()
        for c in range(kd):
            load(c).wait()
            for d in scatters(c):
                d.start()
            if c + nb - 1 < kd:
                if c >= 1:
                    for d in scatters(c - 1):
                        d.wait()
                load(c + nb - 1).start()
        for c in range(max(kd - nb, 0), kd):
            for d in scatters(c):
                d.wait()
        for p in pads:
            p.wait()

    return k(hp, idx, zero_rows)


def _sc_gather(ys, idx):
    kg = idx.shape[1]
    rows_per_worker = kg * SC_ROWS
    nb = SC_NBUF

    @functools.partial(
        pl.kernel, mesh=_sc_mesh(),
        out_type=jax.ShapeDtypeStruct((SC_WORKERS * rows_per_worker, HALF), U32),
        scratch_types=[pltpu.VMEM((kg, SC_ROWS), I32)] + [pltpu.VMEM((SC_ROWS, HALF), U32)] * nb
        + [pltpu.SemaphoreType.DMA((nb,)), pltpu.SemaphoreType.DMA((nb,))],
        name="moe_gather_sc",
    )
    def k(ys_hbm, idx_hbm, yg_hbm, idx_v, *rest):
        bufs = rest[:nb]
        gat_sem, out_sem = rest[nb:]
        wid = _sc_worker_id()
        pltpu.sync_copy(idx_hbm.at[wid], idx_v)

        def gather(c):
            return pltpu.make_async_copy(ys_hbm.at[idx_v.at[c]], bufs[c % nb], gat_sem.at[c % nb])

        def store(c):
            return pltpu.make_async_copy(bufs[c % nb],
                                         yg_hbm.at[pl.ds(wid * rows_per_worker + c * SC_ROWS, SC_ROWS)],
                                         out_sem.at[c % nb])

        for c in range(min(nb - 1, kg)):
            gather(c).start()
        for c in range(kg):
            gather(c).wait()
            store(c).start()
            if c + nb - 1 < kg:
                if c >= 1:
                    store(c - 1).wait()
                gather(c + nb - 1).start()
        for c in range(max(kg - nb, 0), kg):
            store(c).wait()

    return k(ys, idx)


def _expert_kernel(blk_e_ref, nused_ref, first_ref, slot_ref, nxt_ref, xs_ref, wg_hbm, wu_hbm, wd_hbm,
                   ys_ref, wg_buf, wu_buf, wd_buf, sems, *, layer):
    step = pl.program_id(0)
    tm = TM_EXP
    nused = nused_ref[0]

    def weight_copies(e, s):
        return (pltpu.make_async_copy(wg_hbm.at[layer, e], wg_buf.at[s], sems.at[s, 0]),
                pltpu.make_async_copy(wu_hbm.at[layer, e], wu_buf.at[s], sems.at[s, 1]),
                pltpu.make_async_copy(wd_hbm.at[layer, e], wd_buf.at[s], sems.at[s, 2]))

    def dma_control(j):
        i = step * EXP_SUB + j

        @pl.when(i < nused)
        def _():
            s = slot_ref[i]

            if j == 0:
                @pl.when(i == 0)
                def _():
                    for c in weight_copies(blk_e_ref[0], 0):
                        c.start()

            @pl.when(first_ref[i] == 1)
            def _():
                for c in weight_copies(blk_e_ref[i], s):
                    c.wait()

                @pl.when(nxt_ref[i] >= 0)
                def _():
                    for c in weight_copies(nxt_ref[i], lax.rem(s + 1, EXP_SLOTS)):
                        c.start()

    for j in range(EXP_SUB):
        dma_control(j)

    @pl.when(step * EXP_SUB < nused)
    def _():
        for j in range(EXP_SUB):
            i = step * EXP_SUB + j
            rows = slice(j * tm, (j + 1) * tm)
            s = slot_ref[i]
            lo, hi = _unpack_bf16_pairs(xs_ref[rows, :])
            xf = jnp.concatenate([lo, hi], axis=1)
            a = jnp.dot(xf, wg_buf[s], preferred_element_type=F32)
            b = jnp.dot(xf, wu_buf[s], preferred_element_type=F32)
            hm = _silu(a) * b
            y = jnp.dot(hm, wd_buf[s], preferred_element_type=F32)
            ys_ref[rows, :] = jnp.where(i < nused, _pack_bf16_pairs(y), jnp.uint32(0))

    @pl.when(step * EXP_SUB >= nused)
    def _():
        ys_ref[...] = jnp.zeros_like(ys_ref)


def _experts(xs, blk_e, nused, w_gate, w_up, w_down, layer):
    tm = TM_EXP
    p_rows = xs.shape[0] - MOE_EXPERTS * tm
    nblk = p_rows // tm
    pos = jnp.arange(nblk, dtype=I32)
    valid = pos < nused[0]
    prev_e = jnp.concatenate([jnp.full((1,), -1, I32), blk_e[:-1]])
    first = valid & (blk_e != prev_e)
    slot = jnp.maximum(jnp.cumsum(first.astype(I32)) - 1, 0) % EXP_SLOTS
    first_pos = jnp.where(first, pos, nblk)
    next_first = jnp.concatenate([lax.cummin(first_pos, reverse=True)[1:], jnp.full((1,), nblk, I32)])
    nxt = jnp.where(next_first < nblk, blk_e[jnp.minimum(next_first, nblk - 1)], -1)

    def blk(i, be, nu, *_):
        return jnp.minimum(i, (nu[0] - 1) // EXP_SUB)

    grid_spec = pltpu.PrefetchScalarGridSpec(
        num_scalar_prefetch=5,
        grid=(nblk // EXP_SUB,),
        in_specs=[
            pl.BlockSpec((EXP_SUB * tm, HALF), lambda i, *sp: (blk(i, *sp), 0)),
            pl.BlockSpec(memory_space=pl.ANY),
            pl.BlockSpec(memory_space=pl.ANY),
            pl.BlockSpec(memory_space=pl.ANY),
        ],
        out_specs=pl.BlockSpec((EXP_SUB * tm, HALF), lambda i, *sp: (i, 0)),
        scratch_shapes=[
            pltpu.VMEM((EXP_SLOTS, D_MODEL, MOE_FF), F32),
            pltpu.VMEM((EXP_SLOTS, D_MODEL, MOE_FF), F32),
            pltpu.VMEM((EXP_SLOTS, MOE_FF, D_MODEL), F32),
            pltpu.SemaphoreType.DMA((EXP_SLOTS, 3)),
        ],
    )
    return pl.pallas_call(
        functools.partial(_expert_kernel, layer=layer),
        grid_spec=grid_spec,
        out_shape=jax.ShapeDtypeStruct((p_rows, HALF), U32),
        compiler_params=_cparams(("arbitrary",)),
        name="moe_experts",
    )(blk_e, nused, first.astype(I32), slot.astype(I32), nxt.astype(I32), xs, w_gate, w_up, w_down)


def _combine_kernel(x_ref, rw_ref, fg_ref, y1_ref, y2_ref, out_ref):
    lo1, hi1 = _unpack_bf16_pairs(y1_ref[...])
    lo2, hi2 = _unpack_bf16_pairs(y2_ref[...])
    w1, w2 = _route_weight_columns(rw_ref[...])
    x = x_ref[...]
    o_lo = x[:, :HALF] + w1 * lo1 + w2 * lo2
    o_hi = x[:, HALF:] + w1 * hi1 + w2 * hi2
    ms = (jnp.sum(o_lo * o_lo, axis=-1, keepdims=True)
          + jnp.sum(o_hi * o_hi, axis=-1, keepdims=True)) * (1.0 / D_MODEL)
    sc = lax.rsqrt(ms + NORM_EPS)
    o_lo = o_lo * sc * fg_ref[:, :HALF]
    o_hi = o_hi * sc * fg_ref[:, HALF:]
    out_ref[:, :HALF] = o_lo
    out_ref[:, HALF:] = o_hi


def _combine(yg, x, rw, final_g):
    n = x.shape[0]
    td = T_COMB
    nb = n // td
    return pl.pallas_call(
        _combine_kernel,
        grid=(nb,),
        in_specs=[
            pl.BlockSpec((td, D_MODEL), lambda i: (i, 0)),
            pl.BlockSpec((8, td), lambda i: (0, i)),
            pl.BlockSpec((1, D_MODEL), lambda i: (0, 0)),
            pl.BlockSpec((td, HALF), lambda i: (i, 0)),
            pl.BlockSpec((td, HALF), lambda i: (i + nb, 0)),
        ],
        out_specs=pl.BlockSpec((td, D_MODEL), lambda i: (i, 0)),
        out_shape=jax.ShapeDtypeStruct((n, D_MODEL), F32),
        compiler_params=_cparams(("arbitrary",)),
        name="moe_combine",
    )(x, rw, final_g, yg, yg)


def _moe(x, g, w_rg, b_rg, w_re, b_re, w_gate, w_up, w_down, layer):
    n = x.shape[0]
    tm = TM_EXP
    p_rows = 2 * n + MOE_EXPERTS * tm
    nblk = p_rows // tm
    hp, ri, rw, cnt = _router(x, g, w_rg, b_rg, w_re, b_re)
    counts = cnt[:, 0].astype(I32)
    pcounts = (counts + tm - 1) // tm * tm
    pend = jnp.cumsum(pcounts)
    pstart = pend - pcounts
    eio = jnp.arange(MOE_EXPERTS, dtype=I32)[:, None]
    dest1 = jnp.sum(jnp.where(ri[0][None, :] == eio, pstart[:, None], 0), axis=0) + ri[2]
    dest2 = jnp.sum(jnp.where(ri[1][None, :] == eio, pstart[:, None], 0), axis=0) + ri[3]
    blk_start = jnp.arange(nblk, dtype=I32) * tm
    blk_e = jnp.minimum(jnp.sum((pend[None, :] <= blk_start[:, None]).astype(I32), axis=1), MOE_EXPERTS - 1)
    nused = jnp.maximum(pend[-1] // tm, 1).astype(I32).reshape(1)
    r = jnp.arange(tm, dtype=I32)[None, :]
    pad_slot = jnp.where(r < (pcounts - counts)[:, None], (pstart + counts)[:, None] + r, p_rows + eio * tm + r)
    kd = n // SC_WORKERS // SC_ROWS
    idx = jnp.concatenate([dest1.reshape(SC_WORKERS, kd, SC_ROWS), dest2.reshape(SC_WORKERS, kd, SC_ROWS),
                           pad_slot.reshape(SC_WORKERS, -1, SC_ROWS)], axis=1)
    zero_rows = jnp.zeros((SC_ROWS, HALF), U32)
    xs = _sc_dispatch(hp, idx, zero_rows, p_rows + MOE_EXPERTS * tm)
    ys = _experts(xs, blk_e, nused, w_gate, w_up, w_down, layer)
    gidx = jnp.concatenate([dest1, dest2]).reshape(SC_WORKERS, -1, SC_ROWS)
    yg = _sc_gather(ys, gidx)
    return yg, rw


def kernel(x, positions, norm_mix_g, norm_ffn_g, ret_w_in, ret_head_g, ret_w_out, conv_w_pw1, conv_b_pw1, conv_w_dw, conv_b_dw, conv_ln_g, conv_ln_b, conv_w_pw2, conv_b_pw2, moe_w_rg, moe_b_rg, moe_w_re, moe_b_re, moe_w_gate, moe_w_up, moe_w_down, final_norm_g):
    b, s, d = x.shape
    n = b * s
    xt = x.reshape(n, d)
    pos = positions.reshape(n, 1)
    fg = final_norm_g.reshape(1, d)

    q, k, v, gate = _ret_inproj(xt, pos, norm_mix_g[0].reshape(1, d), ret_w_in[0])
    xt = _ret_core(q, k, v, gate, xt, ret_head_g[0].reshape(RET_V, 1), ret_w_out[0])
    yg, rw = _moe(xt, norm_ffn_g[0].reshape(1, d), moe_w_rg[0], moe_b_rg[0], moe_w_re[0], moe_b_re[0],
                  moe_w_gate, moe_w_up, moe_w_down, 0)

    xt, u = _conv_pw1(xt, rw, yg, norm_mix_g[1].reshape(1, d), conv_w_pw1[0],
                      conv_b_pw1[0].reshape(1, 2 * d))
    xt = _conv_core(u, xt, conv_w_dw[0], conv_b_dw[0].reshape(1, d), conv_ln_g[0].reshape(1, d),
                    conv_ln_b[0].reshape(1, d), conv_w_pw2[0], conv_b_pw2[0].reshape(1, d))
    yg, rw = _moe(xt, norm_ffn_g[1].reshape(1, d), moe_w_rg[1], moe_b_rg[1], moe_w_re[1], moe_b_re[1],
                  moe_w_gate, moe_w_up, moe_w_down, 1)
    xt = _combine(yg, xt, rw, fg)
    return xt.reshape(b, s, d)
```

```python
import functools

import jax
import jax.numpy as jnp
from jax import lax
from jax.experimental import pallas as pl
from jax.experimental.pallas import tpu as pltpu
from jax.experimental.pallas import tpu_sc as plsc

F32 = jnp.float32
BF16 = jnp.bfloat16
U32 = jnp.uint32
I32 = jnp.int32

D_MODEL = 1024
RET_HEADS = 4
RET_DK = 256
RET_DV = 512
RET_QK = RET_HEADS * RET_DK
RET_V = RET_HEADS * RET_DV
ROPE_BASE = 10000.0
CONV_WIDTH = 31
MOE_GROUPS = 4
MOE_EPG = 8
MOE_EXPERTS = MOE_GROUPS * MOE_EPG
MOE_FF = 512
NORM_EPS = 1e-6

TM_PROJ = 512
RET_C = 256
RET_STEP = 512
TM_CONV = 1024
TM_PW1 = 1024
PW1_ROWS = 256
CONV_HALO = 32
CONV_ROWS = 128
CONV_STRIDE = 4
T_ROUTE = 1024
TM_EXP = 256
EXP_SUB = 4
EXP_SLOTS = EXP_SUB + 1
T_COMB = 1024
SC_CORES = 2
SC_SUBCORES = 16
SC_WORKERS = SC_CORES * SC_SUBCORES
SC_ROWS = 32
SC_NBUF = 4
HALF = D_MODEL // 2

VMEM_LIMIT = 56 * 1024 * 1024


def _cparams(sem, flags=None):
    return pltpu.CompilerParams(dimension_semantics=sem, vmem_limit_bytes=VMEM_LIMIT, flags=flags)


def _rms(x, g):
    ms = jnp.mean(x * x, axis=-1, keepdims=True)
    return x * lax.rsqrt(ms + NORM_EPS) * g


def _silu(x):
    return x * (1.0 / (1.0 + jnp.exp(-x)))


def _pack_bf16_pairs(y):
    lo = pltpu.bitcast(y[:, :HALF].astype(BF16).astype(F32), U32)
    hi = pltpu.bitcast(y[:, HALF:].astype(BF16).astype(F32), U32)
    return (hi & jnp.uint32(0xFFFF0000)) | (lo >> 16)


def _route_weight_columns(rw):
    t = jnp.concatenate([rw] * 16, axis=0).T
    return t[:, 0:1], t[:, 1:2]


def _unpack_bf16_pairs(p):
    lo = pltpu.bitcast(p << 16, F32)
    hi = pltpu.bitcast(p & jnp.uint32(0xFFFF0000), F32)
    return lo, hi


def _ret_inproj_kernel(x_ref, pos_ref, g_ref, inv_ref, w_ref, q_ref, k_ref, v_ref, gate_ref):
    half = RET_DK // 2
    kscale = RET_DK ** -0.5
    h = _rms(x_ref[...], g_ref[...]).astype(BF16)

    def proj(c0, width):
        return jnp.dot(h, w_ref[:, c0:c0 + width].astype(BF16), preferred_element_type=F32)

    v0 = proj(2 * RET_QK, 512)
    v_ref[:, 0:512] = v0.astype(BF16)
    zero = ((pltpu.bitcast(v0[:, 0:half], U32) >> 16) >> 16).astype(F32)
    ang = pos_ref[...].astype(F32) * inv_ref[...] + zero
    cos = jnp.cos(ang)
    sin = jnp.sin(ang)
    for j in range(RET_V // 512):
        if j > 0:
            v_ref[:, j * 512:(j + 1) * 512] = proj(2 * RET_QK + j * 512, 512).astype(BF16)
        gate_ref[:, j * 512:(j + 1) * 512] = proj(2 * RET_QK + RET_V + j * 512, 512).astype(BF16)

    for hd in range(RET_HEADS):
        for base, out_ref, cs, sn in ((0, q_ref, cos, sin), (RET_QK, k_ref, cos * kscale, sin * kscale)):
            t = proj(base + hd * RET_DK, RET_DK)
            t1 = t[:, :half]
            t2 = t[:, half:]
            out_ref[:, hd * RET_DK:hd * RET_DK + half] = (t1 * cs - t2 * sn).astype(BF16)
            out_ref[:, hd * RET_DK + half:(hd + 1) * RET_DK] = (t1 * sn + t2 * cs).astype(BF16)


def _ret_inproj(x, pos, g, w_in):
    n = x.shape[0]
    half = RET_DK // 2
    inv = (ROPE_BASE ** (-jnp.arange(half, dtype=F32) / half)).reshape(1, half)
    tm = TM_PROJ
    return pl.pallas_call(
        _ret_inproj_kernel,
        grid=(n // tm,),
        in_specs=[
            pl.BlockSpec((tm, D_MODEL), lambda i: (i, 0)),
            pl.BlockSpec((tm, 1), lambda i: (i, 0)),
            pl.BlockSpec((1, D_MODEL), lambda i: (0, 0)),
            pl.BlockSpec((1, half), lambda i: (0, 0)),
            pl.BlockSpec(w_in.shape, lambda i: (0, 0), pipeline_mode=pl.Buffered(1)),
        ],
        out_specs=[
            pl.BlockSpec((tm, RET_QK), lambda i: (i, 0)),
            pl.BlockSpec((tm, RET_QK), lambda i: (i, 0)),
            pl.BlockSpec((tm, RET_V), lambda i: (i, 0)),
            pl.BlockSpec((tm, RET_V), lambda i: (i, 0)),
        ],
        out_shape=[
            jax.ShapeDtypeStruct((n, RET_QK), BF16),
            jax.ShapeDtypeStruct((n, RET_QK), BF16),
            jax.ShapeDtypeStruct((n, RET_V), BF16),
            jax.ShapeDtypeStruct((n, RET_V), BF16),
        ],
        compiler_params=_cparams(("arbitrary",)),
        name="ret_inproj",
    )(x, pos, g, inv, w_in)


def _ret_core_kernel(cdec_ref, q_ref, k_ref, v_ref, gate_ref, x_ref, hg_ref, intra_ref, cross_ref,
                     kdec_ref, wo_ref, out_ref, state_ref, y_ref, wos_ref):
    @pl.when(pl.program_id(0) == 0)
    def _():
        state_ref[...] = jnp.zeros_like(state_ref)
        wos_ref[...] = (wo_ref[...] * hg_ref[...]).astype(BF16)

    for r0 in range(0, RET_STEP, RET_C):
        rs = slice(r0, r0 + RET_C)
        for hd in range(RET_HEADS):
            q = q_ref[rs, hd * RET_DK:(hd + 1) * RET_DK]
            k = k_ref[rs, hd * RET_DK:(hd + 1) * RET_DK]
            v = v_ref[rs, hd * RET_DV:(hd + 1) * RET_DV]
            state = state_ref[hd]
            scores = lax.dot_general(q, k, (((1,), (1,)), ((), ())), preferred_element_type=F32)
            scores = (scores * intra_ref[hd]).astype(BF16)
            o = jnp.dot(scores, v, preferred_element_type=F32)
            cross = cross_ref[hd]
            o_cross = jnp.dot(q, state.astype(BF16), preferred_element_type=F32)
            o = o + o_cross * jnp.concatenate([cross] * (RET_DV // 128), axis=1)
            kdec = kdec_ref[hd]
            kd = (k.astype(F32) * jnp.concatenate([kdec] * (RET_DK // 128), axis=1)).astype(BF16)
            upd = lax.dot_general(kd, v, (((0,), (0,)), ((), ())), preferred_element_type=F32)
            state_ref[hd] = state * cdec_ref[hd] + upd
            ms = jnp.mean(o * o, axis=-1, keepdims=True)
            on = o * lax.rsqrt(ms + NORM_EPS)
            gt = gate_ref[rs, hd * RET_DV:(hd + 1) * RET_DV].astype(F32)
            y_ref[rs, hd * RET_DV:(hd + 1) * RET_DV] = (_silu(gt) * on).astype(BF16)
        out_ref[rs, :] = x_ref[rs, :] + jnp.dot(y_ref[rs, :], wos_ref[...], preferred_element_type=F32)


def _ret_core(q, k, v, gate, x, head_g_col, w_out):
    n = x.shape[0]
    c = RET_C
    log_gamma = jnp.log1p(-(2.0 ** (-5.0 - jnp.arange(RET_HEADS, dtype=F32))))
    idx = jnp.arange(c, dtype=F32)
    diff = idx[:, None] - idx[None, :]
    intra = jnp.where(diff >= 0, jnp.exp(log_gamma[:, None, None] * jnp.maximum(diff, 0.0)), 0.0)
    cross = jnp.broadcast_to(jnp.exp(log_gamma[:, None] * (idx + 1.0))[:, :, None], (RET_HEADS, c, 128))
    kdec = jnp.broadcast_to(jnp.exp(log_gamma[:, None] * (c - 1.0 - idx))[:, :, None], (RET_HEADS, c, 128))
    cdec = jnp.exp(log_gamma * c)
    return pl.pallas_call(
        _ret_core_kernel,
        grid=(n // RET_STEP,),
        in_specs=[
            pl.BlockSpec(memory_space=pltpu.SMEM),
            pl.BlockSpec((RET_STEP, RET_QK), lambda i: (i, 0)),
            pl.BlockSpec((RET_STEP, RET_QK), lambda i: (i, 0)),
            pl.BlockSpec((RET_STEP, RET_V), lambda i: (i, 0)),
            pl.BlockSpec((RET_STEP, RET_V), lambda i: (i, 0)),
            pl.BlockSpec((RET_STEP, D_MODEL), lambda i: (i, 0)),
            pl.BlockSpec((RET_V, 1), lambda i: (0, 0)),
            pl.BlockSpec((RET_HEADS, c, c), lambda i: (0, 0, 0)),
            pl.BlockSpec((RET_HEADS, c, 128), lambda i: (0, 0, 0)),
            pl.BlockSpec((RET_HEADS, c, 128), lambda i: (0, 0, 0)),
            pl.BlockSpec((RET_V, D_MODEL), lambda i: (0, 0)),
        ],
        out_specs=pl.BlockSpec((RET_STEP, D_MODEL), lambda i: (i, 0)),
        out_shape=jax.ShapeDtypeStruct((n, D_MODEL), F32),
        scratch_shapes=[
            pltpu.VMEM((RET_HEADS, RET_DK, RET_DV), F32),
            pltpu.VMEM((RET_STEP, RET_V), BF16),
            pltpu.VMEM((RET_V, D_MODEL), BF16),
        ],
        compiler_params=_cparams(("arbitrary",)),
        name="ret_core",
    )(cdec, q, k, v, gate, x, head_g_col, intra, cross, kdec, w_out)


def _conv_pw1_kernel(x_ref, rw_ref, y1_ref, y2_ref, g_ref, w_ref, b_ref, xo_ref, u_ref):
    r = PW1_ROWS
    ngroups = TM_PW1 // r
    w1, w2 = _route_weight_columns(rw_ref[...])

    def prologue(rs):
        lo1, hi1 = _unpack_bf16_pairs(y1_ref[rs, :])
        lo2, hi2 = _unpack_bf16_pairs(y2_ref[rs, :])
        x = jnp.concatenate([x_ref[rs, :HALF] + w1[rs] * lo1 + w2[rs] * lo2,
                             x_ref[rs, HALF:] + w1[rs] * hi1 + w2[rs] * hi2], axis=1)
        xo_ref[rs, :] = x
        hf = _rms(x, g_ref[...])
        return hf.astype(BF16), hf[:, 0:128]

    groups = [slice(gi * r, (gi + 1) * r) for gi in range(ngroups)]
    hs = [prologue(rs) for rs in groups]
    for j in range(D_MODEL // 512):
        wa = w_ref[:, j * 512:(j + 1) * 512].astype(BF16)
        wg = w_ref[:, D_MODEL + j * 512:D_MODEL + (j + 1) * 512].astype(BF16)
        for gi, rs in enumerate(groups):
            h = hs[gi][0]
            a = jnp.dot(h, wa, preferred_element_type=F32) + b_ref[:, j * 512:(j + 1) * 512]
            gt = jnp.dot(h, wg, preferred_element_type=F32) + b_ref[:, D_MODEL + j * 512:D_MODEL + (j + 1) * 512]
            u = a * (1.0 / (1.0 + jnp.exp(-gt)))
            if j == 0 and gi + 1 < ngroups:
                zero = ((pltpu.bitcast(hs[gi + 1][1], U32) >> 16) >> 16).astype(F32)
                u_ref[rs, 0:128] = u[:, 0:128] + zero
                u_ref[rs, 128:512] = u[:, 128:512]
            else:
                u_ref[rs, j * 512:(j + 1) * 512] = u


def _conv_pw1(x, rw, yg, g, w, b):
    n = x.shape[0]
    tm = TM_PW1
    nb = n // tm
    return pl.pallas_call(
        _conv_pw1_kernel,
        grid=(nb,),
        in_specs=[
            pl.BlockSpec((tm, D_MODEL), lambda i: (i, 0)),
            pl.BlockSpec((8, tm), lambda i: (0, i)),
            pl.BlockSpec((tm, HALF), lambda i: (i, 0)),
            pl.BlockSpec((tm, HALF), lambda i: (i + nb, 0)),
            pl.BlockSpec((1, D_MODEL), lambda i: (0, 0)),
            pl.BlockSpec((D_MODEL, 2 * D_MODEL), lambda i: (0, 0), pipeline_mode=pl.Buffered(1)),
            pl.BlockSpec((1, 2 * D_MODEL), lambda i: (0, 0)),
        ],
        out_specs=[pl.BlockSpec((tm, D_MODEL), lambda i: (i, 0)), pl.BlockSpec((tm, D_MODEL), lambda i: (i, 0))],
        out_shape=[jax.ShapeDtypeStruct((n, D_MODEL), F32), jax.ShapeDtypeStruct((n, D_MODEL), F32)],
        compiler_params=_cparams(("arbitrary",)),
        name="conv_pw1",
    )(x, rw, yg, yg, g, w, b)


def _conv_core_kernel(u_ref, halo_ref, x_ref, wdw_ref, bdw_ref, lng_ref, lnb_ref, w2_ref, b2_ref,
                      out_ref, win_ref, z_ref):
    tm = TM_CONV
    first = pl.program_id(0) == 0
    halo = halo_ref[...]
    halo = jnp.where(first, jnp.zeros_like(halo), halo)
    nslab = D_MODEL // 128
    for cc in range(nslab):
        cs = slice(cc * 128, (cc + 1) * 128)
        win_ref[cc, 0:CONV_HALO, :] = halo[:, cs]
        win_ref[cc, CONV_HALO:CONV_HALO + tm, :] = u_ref[:, cs]
    off = CONV_HALO - (CONV_WIDTH - 1)
    rb = CONV_ROWS
    st = CONV_STRIDE
    for cc in range(nslab):
        cs = slice(cc * 128, (cc + 1) * 128)
        for r0 in range(0, tm, rb):
            accs = [bdw_ref[:, cs]] * st
            for o in range(CONV_WIDTH):
                w_o = wdw_ref[o:o + 1, cs]
                for rho in range(st):
                    accs[rho] = accs[rho] + win_ref[cc, pl.ds(r0 + rho + off + o, rb // st, stride=st), :] * w_o
            for rho in range(st):
                z_ref[cc, pl.ds(r0 + rho, rb // st, stride=st), :] = accs[rho]
    z = jnp.concatenate([z_ref[cc] for cc in range(nslab)], axis=1)
    mu = jnp.mean(z, axis=-1, keepdims=True)
    zc = z - mu
    var = jnp.mean(zc * zc, axis=-1, keepdims=True)
    zn = zc * lax.rsqrt(var + NORM_EPS) * lng_ref[...] + lnb_ref[...]
    y = _silu(zn).astype(BF16)
    out_ref[...] = x_ref[...] + jnp.dot(y, w2_ref[...].astype(BF16), preferred_element_type=F32) + b2_ref[...]


def _conv_core(u, x, w_dw, b_dw, ln_g, ln_b, w2, b2):
    n = x.shape[0]
    tm = TM_CONV
    r = tm // CONV_HALO
    wdw_pad = jnp.zeros((32, D_MODEL), F32).at[:CONV_WIDTH].set(w_dw)
    return pl.pallas_call(
        _conv_core_kernel,
        grid=(n // tm,),
        in_specs=[
            pl.BlockSpec((tm, D_MODEL), lambda i: (i, 0)),
            pl.BlockSpec((CONV_HALO, D_MODEL), lambda i: (jnp.maximum(i * r - 1, 0), 0)),
            pl.BlockSpec((tm, D_MODEL), lambda i: (i, 0)),
            pl.BlockSpec((32, D_MODEL), lambda i: (0, 0)),
            pl.BlockSpec((1, D_MODEL), lambda i: (0, 0)),
            pl.BlockSpec((1, D_MODEL), lambda i: (0, 0)),
            pl.BlockSpec((1, D_MODEL), lambda i: (0, 0)),
            pl.BlockSpec((D_MODEL, D_MODEL), lambda i: (0, 0)),
            pl.BlockSpec((1, D_MODEL), lambda i: (0, 0)),
        ],
        out_specs=pl.BlockSpec((tm, D_MODEL), lambda i: (i, 0)),
        out_shape=jax.ShapeDtypeStruct((n, D_MODEL), F32),
        scratch_shapes=[pltpu.VMEM((D_MODEL // 128, CONV_HALO + tm, 128), F32),
                        pltpu.VMEM((D_MODEL // 128, tm, 128), F32)],
        compiler_params=_cparams(("arbitrary",)),
        name="conv_core",
    )(u, u, x, wdw_pad, b_dw, ln_g, ln_b, w2, b2)


def _router_kernel(x_ref, g_ref, wr_ref, br_ref, upper_ref, hp_ref, ri_ref, rw_ref, cnt_ref, carry_ref):
    t = T_ROUTE

    @pl.when(pl.program_id(0) == 0)
    def _():
        carry_ref[...] = jnp.zeros_like(carry_ref)

    h = _rms(x_ref[...], g_ref[...])
    hp_ref[...] = _pack_bf16_pairs(h)
    h_hi = h.astype(BF16)
    h_lo = (h - h_hi.astype(F32)).astype(BF16)
    w = wr_ref[...]
    w_hi = w.astype(BF16)
    w_lo = (w - w_hi.astype(F32)).astype(BF16)
    dn = (((1,), (1,)), ((), ()))
    p = lax.dot_general(jnp.concatenate([w_hi, w_lo], axis=0), h_hi, dn, preferred_element_type=F32)
    nr = wr_ref.shape[0]
    logits = p[0:nr] + p[nr:2 * nr] + lax.dot_general(w_hi, h_lo, dn, preferred_element_type=F32)
    logits = logits + br_ref[:, 0:1]

    best = logits[0:1]
    gi = jnp.zeros((1, t), I32)
    for j in range(1, MOE_GROUPS):
        r = logits[j:j + 1]
        up = r > best
        gi = jnp.where(up, j, gi)
        best = jnp.where(up, r, best)
    den = jnp.zeros((1, t), F32)
    for j in range(MOE_GROUPS):
        den = den + jnp.exp(logits[j:j + 1] - best)
    gate_g = 1.0 / den

    sel = logits[8:8 + MOE_EPG]
    for j in range(1, MOE_GROUPS):
        sel = jnp.where(gi == j, logits[8 + j * MOE_EPG:8 + (j + 1) * MOE_EPG], sel)

    m1 = sel[0:1]
    i1 = jnp.zeros((1, t), I32)
    for j in range(1, MOE_EPG):
        r = sel[j:j + 1]
        up = r > m1
        i1 = jnp.where(up, j, i1)
        m1 = jnp.where(up, r, m1)
    m2 = jnp.full((1, t), -jnp.inf, F32)
    i2 = jnp.zeros((1, t), I32)
    started = jnp.zeros((1, t), jnp.bool_)
    for j in range(MOE_EPG):
        r = sel[j:j + 1]
        ok = i1 != j
        up = ok & ((r > m2) | jnp.logical_not(started))
        i2 = jnp.where(up, j, i2)
        m2 = jnp.where(up, r, m2)
        started = started | ok
    e21 = jnp.exp(m2 - m1)
    p1 = 1.0 / (1.0 + e21)
    w1 = gate_g * p1
    w2 = gate_g * (e21 * p1)
    eid1 = gi * MOE_EPG + i1
    eid2 = gi * MOE_EPG + i2

    eio = lax.broadcasted_iota(I32, (MOE_EXPERTS, t), 0)
    oh1 = eio == eid1
    oh2 = eio == eid2
    oh = (oh1 | oh2).astype(F32)
    cum = jnp.dot(oh.astype(BF16), upper_ref[...], preferred_element_type=F32) + carry_ref[:, 0:1]
    rank1 = jnp.sum(jnp.where(oh1, cum, 0.0), axis=0, keepdims=True)
    rank2 = jnp.sum(jnp.where(oh2, cum, 0.0), axis=0, keepdims=True)
    carry_ref[...] = carry_ref[...] + jnp.sum(oh, axis=1, keepdims=True)
    cnt_ref[...] = carry_ref[...]

    zi = jnp.zeros((4, t), I32)
    ri_ref[...] = jnp.concatenate([eid1, eid2, rank1.astype(I32), rank2.astype(I32), zi], axis=0)
    zf = jnp.zeros((6, t), F32)
    rw_ref[...] = jnp.concatenate([w1, w2, zf], axis=0)


def _router(x, g, w_rg, b_rg, w_re, b_re):
    n = x.shape[0]
    t = T_ROUTE
    wr = jnp.zeros((40, D_MODEL), F32).at[0:MOE_GROUPS].set(w_rg.T).at[8:40].set(w_re.T)
    br = jnp.zeros((40,), F32).at[0:MOE_GROUPS].set(b_rg).at[8:40].set(b_re)
    br = jnp.broadcast_to(br[:, None], (40, 128))
    upper = (lax.broadcasted_iota(I32, (t, t), 0) < lax.broadcasted_iota(I32, (t, t), 1)).astype(BF16)
    return pl.pallas_call(
        _router_kernel,
        grid=(n // t,),
        in_specs=[
            pl.BlockSpec((t, D_MODEL), lambda i: (i, 0)),
            pl.BlockSpec((1, D_MODEL), lambda i: (0, 0)),
            pl.BlockSpec((40, D_MODEL), lambda i: (0, 0)),
            pl.BlockSpec((40, 128), lambda i: (0, 0)),
            pl.BlockSpec((t, t), lambda i: (0, 0)),
        ],
        out_specs=[
            pl.BlockSpec((t, HALF), lambda i: (i, 0)),
            pl.BlockSpec((8, t), lambda i: (0, i)),
            pl.BlockSpec((8, t), lambda i: (0, i)),
            pl.BlockSpec((MOE_EXPERTS, 128), lambda i: (0, 0)),
        ],
        out_shape=[
            jax.ShapeDtypeStruct((n, HALF), U32),
            jax.ShapeDtypeStruct((8, n), I32),
            jax.ShapeDtypeStruct((8, n), F32),
            jax.ShapeDtypeStruct((MOE_EXPERTS, 128), F32),
        ],
        scratch_shapes=[pltpu.VMEM((MOE_EXPERTS, 128), F32)],
        compiler_params=_cparams(("arbitrary",)),
        name="moe_router",
    )(x, g, wr, br, upper)


def _sc_mesh():
    return plsc.VectorSubcoreMesh(core_axis_name="c", subcore_axis_name="s",
                                  num_cores=SC_CORES, num_subcores=SC_SUBCORES)


def _sc_worker_id():
    return lax.axis_index("s") * SC_CORES + lax.axis_index("c")


def _sc_dispatch(hp, idx, zero_rows, total_rows):
    n = hp.shape[0]
    tpw = n // SC_WORKERS
    kd = tpw // SC_ROWS
    kp = idx.shape[1] - 2 * kd
    nb = SC_NBUF

    @functools.partial(
        pl.kernel, mesh=_sc_mesh(),
        out_type=jax.ShapeDtypeStruct((total_rows, HALF), U32),
        scratch_types=[pltpu.VMEM((2 * kd + kp, SC_ROWS), I32)]
        + [pltpu.VMEM((SC_ROWS, HALF), U32)] * (nb + 1)
        + [pltpu.SemaphoreType.DMA((nb,)), pltpu.SemaphoreType.DMA((nb,)), pltpu.SemaphoreType.DMA],
        name="moe_dispatch_sc",
    )
    def k(hp_hbm, idx_hbm, zero_hbm, xs_hbm, idx_v, *rest):
        bufs, zbuf = rest[:nb], rest[nb]
        load_sem, scat_sem, pad_sem = rest[nb + 1:]
        wid = _sc_worker_id()
        pltpu.sync_copy(idx_hbm.at[wid], idx_v)

        def load(c):
            return pltpu.make_async_copy(hp_hbm.at[pl.ds(wid * tpw + c * SC_ROWS, SC_ROWS)], bufs[c % nb],
                                         load_sem.at[c % nb])

        def scatters(c):
            return (pltpu.make_async_copy(bufs[c % nb], xs_hbm.at[idx_v.at[c]], scat_sem.at[c % nb]),
                    pltpu.make_async_copy(bufs[c % nb], xs_hbm.at[idx_v.at[kd + c]], scat_sem.at[c % nb]))

        for c in range(min(nb - 1, kd)):
            load(c).start()
        pltpu.sync_copy(zero_hbm, zbuf)
        pads = [pltpu.make_async_copy(zbuf, xs_hbm.at[idx_v.at[2 * kd + j]], pad_sem) for j in range(kp)]
        for p in pads:
            p.start()
        for c in range(kd):
            load(c).wait()
            for d in scatters(c):
                d.start()
            if c + nb - 1 < kd:
                if c >= 1:
                    for d in scatters(c - 1):
                        d.wait()
                load(c + nb - 1).start()
        for c in range(max(kd - nb, 0), kd):
            for d in scatters(c):
                d.wait()
        for p in pads:
            p.wait()

    return k(hp, idx, zero_rows)


def _sc_gather(ys, idx):
    kg = idx.shape[1]
    rows_per_worker = kg * SC_ROWS
    nb = SC_NBUF

    @functools.partial(
        pl.kernel, mesh=_sc_mesh(),
        out_type=jax.ShapeDtypeStruct((SC_WORKERS * rows_per_worker, HALF), U32),
        scratch_types=[pltpu.VMEM((kg, SC_ROWS), I32)] + [pltpu.VMEM((SC_ROWS, HALF), U32)] * nb
        + [pltpu.SemaphoreType.DMA((nb,)), pltpu.SemaphoreType.DMA((nb,))],
        name="moe_gather_sc",
    )
    def k(ys_hbm, idx_hbm, yg_hbm, idx_v, *rest):
        bufs = rest[:nb]
        gat_sem, out_sem = rest[nb:]
        wid = _sc_worker_id()
        pltpu.sync_copy(idx_hbm.at[wid], idx_v)

        def gather(c):
            return pltpu.make_async_copy(ys_hbm.at[idx_v.at[c]], bufs[c % nb], gat_sem.at[c % nb])

        def store(c):
            return pltpu.make_async_copy(bufs[c % nb],
                                         yg_hbm.at[pl.ds(wid * rows_per_worker + c * SC_ROWS, SC_ROWS)],
                                         out_sem.at[c % nb])

        for c in range(min(nb - 1, kg)):
            gather(c).start()
        for c in range(kg):
            gather(c).wait()
            store(c).start()
            if c + nb - 1 < kg:
                if c >= 1:
                    store(c - 1).wait()
                gather(c + nb - 1).start()
        for c in range(max(kg - nb, 0), kg):
            store(c).wait()

    return k(ys, idx)


def _expert_kernel(blk_e_ref, nused_ref, first_ref, slot_ref, nxt_ref, xs_ref, wg_hbm, wu_hbm, wd_hbm,
                   ys_ref, wg_buf, wu_buf, wd_buf, sems, *, layer):
    step = pl.program_id(0)
    tm = TM_EXP
    nused = nused_ref[0]

    def weight_copies(e, s):
        return (pltpu.make_async_copy(wg_hbm.at[layer, e], wg_buf.at[s], sems.at[s, 0]),
                pltpu.make_async_copy(wu_hbm.at[layer, e], wu_buf.at[s], sems.at[s, 1]),
                pltpu.make_async_copy(wd_hbm.at[layer, e], wd_buf.at[s], sems.at[s, 2]))

    def dma_control(j):
        i = step * EXP_SUB + j

        @pl.when(i < nused)
        def _():
            s = slot_ref[i]

            if j == 0:
                @pl.when(i == 0)
                def _():
                    for c in weight_copies(blk_e_ref[0], 0):
                        c.start()

            @pl.when(first_ref[i] == 1)
            def _():
                for c in weight_copies(blk_e_ref[i], s):
                    c.wait()

                @pl.when(nxt_ref[i] >= 0)
                def _():
                    for c in weight_copies(nxt_ref[i], lax.rem(s + 1, EXP_SLOTS)):
                        c.start()

    for j in range(EXP_SUB):
        dma_control(j)

    @pl.when(step * EXP_SUB < nused)
    def _():
        for j in range(EXP_SUB):
            i = step * EXP_SUB + j
            rows = slice(j * tm, (j + 1) * tm)
            s = slot_ref[i]
            lo, hi = _unpack_bf16_pairs(xs_ref[rows, :])
            xf = jnp.concatenate([lo, hi], axis=1)
            a = jnp.dot(xf, wg_buf[s], preferred_element_type=F32)
            b = jnp.dot(xf, wu_buf[s], preferred_element_type=F32)
            hm = _silu(a) * b
            y = jnp.dot(hm, wd_buf[s], preferred_element_type=F32)
            ys_ref[rows, :] = jnp.where(i < nused, _pack_bf16_pairs(y), jnp.uint32(0))

    @pl.when(step * EXP_SUB >= nused)
    def _():
        ys_ref[...] = jnp.zeros_like(ys_ref)


def _experts(xs, blk_e, nused, w_gate, w_up, w_down, layer):
    tm = TM_EXP
    p_rows = xs.shape[0] - MOE_EXPERTS * tm
    nblk = p_rows // tm
    pos = jnp.arange(nblk, dtype=I32)
    valid = pos < nused[0]
    prev_e = jnp.concatenate([jnp.full((1,), -1, I32), blk_e[:-1]])
    first = valid & (blk_e != prev_e)
    slot = jnp.maximum(jnp.cumsum(first.astype(I32)) - 1, 0) % EXP_SLOTS
    first_pos = jnp.where(first, pos, nblk)
    next_first = jnp.concatenate([lax.cummin(first_pos, reverse=True)[1:], jnp.full((1,), nblk, I32)])
    nxt = jnp.where(next_first < nblk, blk_e[jnp.minimum(next_first, nblk - 1)], -1)

    def blk(i, be, nu, *_):
        return jnp.minimum(i, (nu[0] - 1) // EXP_SUB)

    grid_spec = pltpu.PrefetchScalarGridSpec(
        num_scalar_prefetch=5,
        grid=(nblk // EXP_SUB,),
        in_specs=[
            pl.BlockSpec((EXP_SUB * tm, HALF), lambda i, *sp: (blk(i, *sp), 0)),
            pl.BlockSpec(memory_space=pl.ANY),
            pl.BlockSpec(memory_space=pl.ANY),
            pl.BlockSpec(memory_space=pl.ANY),
        ],
        out_specs=pl.BlockSpec((EXP_SUB * tm, HALF), lambda i, *sp: (i, 0)),
        scratch_shapes=[
            pltpu.VMEM((EXP_SLOTS, D_MODEL, MOE_FF), F32),
            pltpu.VMEM((EXP_SLOTS, D_MODEL, MOE_FF), F32),
            pltpu.VMEM((EXP_SLOTS, MOE_FF, D_MODEL), F32),
            pltpu.SemaphoreType.DMA((EXP_SLOTS, 3)),
        ],
    )
    return pl.pallas_call(
        functools.partial(_expert_kernel, layer=layer),
        grid_spec=grid_spec,
        out_shape=jax.ShapeDtypeStruct((p_rows, HALF), U32),
        compiler_params=_cparams(("arbitrary",)),
        name="moe_experts",
    )(blk_e, nused, first.astype(I32), slot.astype(I32), nxt.astype(I32), xs, w_gate, w_up, w_down)


def _combine_kernel(x_ref, rw_ref, fg_ref, y1_ref, y2_ref, out_ref):
    lo1, hi1 = _unpack_bf16_pairs(y1_ref[...])
    lo2, hi2 = _unpack_bf16_pairs(y2_ref[...])
    w1, w2 = _route_weight_columns(rw_ref[...])
    x = x_ref[...]
    o_lo = x[:, :HALF] + w1 * lo1 + w2 * lo2
    o_hi = x[:, HALF:] + w1 * hi1 + w2 * hi2
    ms = (jnp.sum(o_lo * o_lo, axis=-1, keepdims=True)
          + jnp.sum(o_hi * o_hi, axis=-1, keepdims=True)) * (1.0 / D_MODEL)
    sc = lax.rsqrt(ms + NORM_EPS)
    o_lo = o_lo * sc * fg_ref[:, :HALF]
    o_hi = o_hi * sc * fg_ref[:, HALF:]
    out_ref[:, :HALF] = o_lo
    out_ref[:, HALF:] = o_hi


def _combine(yg, x, rw, final_g):
    n = x.shape[0]
    td = T_COMB
    nb = n // td
    return pl.pallas_call(
        _combine_kernel,
        grid=(nb,),
        in_specs=[
            pl.BlockSpec((td, D_MODEL), lambda i: (i, 0)),
            pl.BlockSpec((8, td), lambda i: (0, i)),
            pl.BlockSpec((1, D_MODEL), lambda i: (0, 0)),
            pl.BlockSpec((td, HALF), lambda i: (i, 0)),
            pl.BlockSpec((td, HALF), lambda i: (i + nb, 0)),
        ],
        out_specs=pl.BlockSpec((td, D_MODEL), lambda i: (i, 0)),
        out_shape=jax.ShapeDtypeStruct((n, D_MODEL), F32),
        compiler_params=_cparams(("arbitrary",)),
        name="moe_combine",
    )(x, rw, final_g, yg, yg)


def _moe(x, g, w_rg, b_rg, w_re, b_re, w_gate, w_up, w_down, layer):
    n = x.shape[0]
    tm = TM_EXP
    p_rows = 2 * n + MOE_EXPERTS * tm
    nblk = p_rows // tm
    hp, ri, rw, cnt = _router(x, g, w_rg, b_rg, w_re, b_re)
    counts = cnt[:, 0].astype(I32)
    pcounts = (counts + tm - 1) // tm * tm
    pend = jnp.cumsum(pcounts)
    pstart = pend - pcounts
    eio = jnp.arange(MOE_EXPERTS, dtype=I32)[:, None]
    dest1 = jnp.sum(jnp.where(ri[0][None, :] == eio, pstart[:, None], 0), axis=0) + ri[2]
    dest2 = jnp.sum(jnp.where(ri[1][None, :] == eio, pstart[:, None], 0), axis=0) + ri[3]
    blk_start = jnp.arange(nblk, dtype=I32) * tm
    blk_e = jnp.minimum(jnp.sum((pend[None, :] <= blk_start[:, None]).astype(I32), axis=1), MOE_EXPERTS - 1)
    nused = jnp.maximum(pend[-1] // tm, 1).astype(I32).reshape(1)
    r = jnp.arange(tm, dtype=I32)[None, :]
    pad_slot = jnp.where(r < (pcounts - counts)[:, None], (pstart + counts)[:, None] + r, p_rows + eio * tm + r)
    kd = n // SC_WORKERS // SC_ROWS
    idx = jnp.concatenate([dest1.reshape(SC_WORKERS, kd, SC_ROWS), dest2.reshape(SC_WORKERS, kd, SC_ROWS),
                           pad_slot.reshape(SC_WORKERS, -1, SC_ROWS)], axis=1)
    zero_rows = jnp.zeros((SC_ROWS, HALF), U32)
    xs = _sc_dispatch(hp, idx, zero_rows, p_rows + MOE_EXPERTS * tm)
    ys = _experts(xs, blk_e, nused, w_gate, w_up, w_down, layer)
    gidx = jnp.concatenate([dest1, dest2]).reshape(SC_WORKERS, -1, SC_ROWS)
    yg = _sc_gather(ys, gidx)
    return yg, rw


def kernel(x, positions, norm_mix_g, norm_ffn_g, ret_w_in, ret_head_g, ret_w_out, conv_w_pw1, conv_b_pw1, conv_w_dw, conv_b_dw, conv_ln_g, conv_ln_b, conv_w_pw2, conv_b_pw2, moe_w_rg, moe_b_rg, moe_w_re, moe_b_re, moe_w_gate, moe_w_up, moe_w_down, final_norm_g):
    b, s, d = x.shape
    n = b * s
    xt = x.reshape(n, d)
    pos = positions.reshape(n, 1)
    fg = final_norm_g.reshape(1, d)

    q, k, v, gate = _ret_inproj(xt, pos, norm_mix_g[0].reshape(1, d), ret_w_in[0])
    xt = _ret_core(q, k, v, gate, xt, ret_head_g[0].reshape(RET_V, 1), ret_w_out[0])
    yg, rw = _moe(xt, norm_ffn_g[0].reshape(1, d), moe_w_rg[0], moe_b_rg[0], moe_w_re[0], moe_b_re[0],
                  moe_w_gate, moe_w_up, moe_w_down, 0)

    xt, u = _conv_pw1(xt, rw, yg, norm_mix_g[1].reshape(1, d), conv_w_pw1[0],
                      conv_b_pw1[0].reshape(1, 2 * d))
    xt = _conv_core(u, xt, conv_w_dw[0], conv_b_dw[0].reshape(1, d), conv_ln_g[0].reshape(1, d),
                    conv_ln_b[0].reshape(1, d), conv_w_pw2[0], conv_b_pw2[0].reshape(1, d))
    yg, rw = _moe(xt, norm_ffn_g[1].reshape(1, d), moe_w_rg[1], moe_b_rg[1], moe_w_re[1], moe_b_re[1],
                  moe_w_gate, moe_w_up, moe_w_down, 1)
    xt = _combine(yg, xt, rw, fg)
    return xt.reshape(b, s, d)
```

```python
import functools

import jax
import jax.numpy as jnp
import numpy as np
from jax import lax
from jax.experimental import pallas as pl
from jax.experimental.pallas import tpu as pltpu
from jax.experimental.pallas import tpu_sc as plsc

F32 = jnp.float32
BF16 = jnp.bfloat16
U32 = jnp.uint32
I32 = jnp.int32

D_MODEL = 1024
RET_HEADS = 4
RET_DK = 256
RET_DV = 512
RET_QK = RET_HEADS * RET_DK
RET_V = RET_HEADS * RET_DV
ROPE_BASE = 10000.0
CONV_WIDTH = 31
MOE_GROUPS = 4
MOE_EPG = 8
MOE_EXPERTS = MOE_GROUPS * MOE_EPG
MOE_FF = 512
NORM_EPS = 1e-6

TM_PROJ = 512
RET_C = 256
RET_STEP = 512
TM_CONV = 1024
TM_PW1 = 1024
PW1_ROWS = 256
CONV_HALO = 32
CONV_ROWS = 128
CONV_STRIDE = 4
T_ROUTE = 1024
TM_EXP = 256
EXP_SUB = 4
EXP_SLOTS = EXP_SUB + 1
T_COMB = 1024
SC_CORES = 2
SC_SUBCORES = 16
SC_WORKERS = SC_CORES * SC_SUBCORES
SC_ROWS = 32
SC_NBUF = 4
HALF = D_MODEL // 2

VMEM_LIMIT = 56 * 1024 * 1024


def _cparams(sem, flags=None):
    return pltpu.CompilerParams(dimension_semantics=sem, vmem_limit_bytes=VMEM_LIMIT, flags=flags)


def _rms(x, g):
    ms = jnp.mean(x * x, axis=-1, keepdims=True)
    return x * lax.rsqrt(ms + NORM_EPS) * g


def _silu(x):
    return x * (1.0 / (1.0 + jnp.exp(-x)))


def _pack_bf16_pairs(y):
    lo = pltpu.bitcast(y[:, :HALF].astype(BF16).astype(F32), U32)
    hi = pltpu.bitcast(y[:, HALF:].astype(BF16).astype(F32), U32)
    return (hi & jnp.uint32(0xFFFF0000)) | (lo >> 16)


def _route_weight_columns(rw):
    t = jnp.concatenate([rw] * 16, axis=0).T
    return t[:, 0:1], t[:, 1:2]


def _unpack_bf16_pairs(p):
    lo = pltpu.bitcast(p << 16, F32)
    hi = pltpu.bitcast(p & jnp.uint32(0xFFFF0000), F32)
    return lo, hi


def _ret_inproj_kernel(x_ref, pos_ref, g_ref, inv_ref, w_ref, q_ref, k_ref, v_ref, gate_ref):
    half = RET_DK // 2
    kscale = RET_DK ** -0.5
    h = _rms(x_ref[...], g_ref[...]).astype(BF16)

    def proj(c0, width):
        return jnp.dot(h, w_ref[:, c0:c0 + width].astype(BF16), preferred_element_type=F32)

    v0 = proj(2 * RET_QK, 512)
    v_ref[:, 0:512] = v0.astype(BF16)
    zero = ((pltpu.bitcast(v0[:, 0:half], U32) >> 16) >> 16).astype(F32)
    pos = jnp.broadcast_to(pos_ref[...].astype(F32), (half, TM_PROJ)).T
    ang = pos * inv_ref[...] + zero
    cos = jnp.cos(ang)
    sin = jnp.sin(ang)
    for j in range(RET_V // 512):
        if j > 0:
            v_ref[:, j * 512:(j + 1) * 512] = proj(2 * RET_QK + j * 512, 512).astype(BF16)
        gate_ref[:, j * 512:(j + 1) * 512] = proj(2 * RET_QK + RET_V + j * 512, 512).astype(BF16)

    for hd in range(RET_HEADS):
        for base, out_ref, cs, sn in ((0, q_ref, cos, sin), (RET_QK, k_ref, cos * kscale, sin * kscale)):
            t = proj(base + hd * RET_DK, RET_DK)
            t1 = t[:, :half]
            t2 = t[:, half:]
            out_ref[:, hd * RET_DK:hd * RET_DK + half] = (t1 * cs - t2 * sn).astype(BF16)
            out_ref[:, hd * RET_DK + half:(hd + 1) * RET_DK] = (t1 * sn + t2 * cs).astype(BF16)


def _ret_inproj(x, pos, g, w_in):
    n = x.shape[0]
    half = RET_DK // 2
    inv = (ROPE_BASE ** (-jnp.arange(half, dtype=F32) / half)).reshape(1, half)
    tm = TM_PROJ
    return pl.pallas_call(
        _ret_inproj_kernel,
        grid=(n // tm,),
        in_specs=[
            pl.BlockSpec((tm, D_MODEL), lambda i: (i, 0)),
            pl.BlockSpec((1, tm), lambda i: (0, i)),
            pl.BlockSpec((1, D_MODEL), lambda i: (0, 0)),
            pl.BlockSpec((1, half), lambda i: (0, 0)),
            pl.BlockSpec(w_in.shape, lambda i: (0, 0), pipeline_mode=pl.Buffered(1)),
        ],
        out_specs=[
            pl.BlockSpec((tm, RET_QK), lambda i: (i, 0)),
            pl.BlockSpec((tm, RET_QK), lambda i: (i, 0)),
            pl.BlockSpec((tm, RET_V), lambda i: (i, 0)),
            pl.BlockSpec((tm, RET_V), lambda i: (i, 0)),
        ],
        out_shape=[
            jax.ShapeDtypeStruct((n, RET_QK), BF16),
            jax.ShapeDtypeStruct((n, RET_QK), BF16),
            jax.ShapeDtypeStruct((n, RET_V), BF16),
            jax.ShapeDtypeStruct((n, RET_V), BF16),
        ],
        compiler_params=_cparams(("arbitrary",)),
        name="ret_inproj",
    )(x, pos, g, inv, w_in)


def _ret_core_kernel(cdec_ref, q_ref, k_ref, v_ref, gate_ref, x_ref, hg_ref, intra_ref, cross_ref,
                     kdec_ref, wo_ref, out_ref, state_ref, y_ref, wos_ref):
    @pl.when(pl.program_id(0) == 0)
    def _():
        state_ref[...] = jnp.zeros_like(state_ref)
        hg = jnp.broadcast_to(hg_ref[...], (128, RET_V)).T[:, 0:1]
        wos_ref[...] = (wo_ref[...] * hg).astype(BF16)

    for r0 in range(0, RET_STEP, RET_C):
        rs = slice(r0, r0 + RET_C)
        for hd in range(RET_HEADS):
            q = q_ref[rs, hd * RET_DK:(hd + 1) * RET_DK]
            k = k_ref[rs, hd * RET_DK:(hd + 1) * RET_DK]
            v = v_ref[rs, hd * RET_DV:(hd + 1) * RET_DV]
            state = state_ref[hd]
            scores = lax.dot_general(q, k, (((1,), (1,)), ((), ())), preferred_element_type=F32)
            scores = (scores * intra_ref[hd]).astype(BF16)
            o = jnp.dot(scores, v, preferred_element_type=F32)
            cross = cross_ref[hd]
            o_cross = jnp.dot(q, state.astype(BF16), preferred_element_type=F32)
            o = o + o_cross * jnp.concatenate([cross] * (RET_DV // 128), axis=1)
            kdec = kdec_ref[hd]
            kd = (k.astype(F32) * jnp.concatenate([kdec] * (RET_DK // 128), axis=1)).astype(BF16)
            upd = lax.dot_general(kd, v, (((0,), (0,)), ((), ())), preferred_element_type=F32)
            state_ref[hd] = state * cdec_ref[hd] + upd
            ms = jnp.mean(o * o, axis=-1, keepdims=True)
            on = o * lax.rsqrt(ms + NORM_EPS)
            gt = gate_ref[rs, hd * RET_DV:(hd + 1) * RET_DV].astype(F32)
            y_ref[rs, hd * RET_DV:(hd + 1) * RET_DV] = (_silu(gt) * on).astype(BF16)
        out_ref[rs, :] = x_ref[rs, :] + jnp.dot(y_ref[rs, :], wos_ref[...], preferred_element_type=F32)


def _ret_core(q, k, v, gate, x, head_g_row, w_out):
    n = x.shape[0]
    c = RET_C
    f = np.float32
    log_gamma = np.log1p(-(f(2.0) ** (f(-5.0) - np.arange(RET_HEADS, dtype=f)))).astype(f)
    idx = np.arange(c, dtype=f)
    diff = idx[:, None] - idx[None, :]
    intra = np.where(diff >= 0, np.exp(log_gamma[:, None, None] * np.maximum(diff, f(0.0))), f(0.0)).astype(f)
    cross = np.broadcast_to(np.exp(log_gamma[:, None] * (idx + f(1.0)))[:, :, None], (RET_HEADS, c, 128)).astype(f)
    kdec = np.broadcast_to(np.exp(log_gamma[:, None] * (f(c - 1.0) - idx))[:, :, None],
                           (RET_HEADS, c, 128)).astype(f)
    cdec = np.exp(log_gamma * f(c)).astype(f)
    return pl.pallas_call(
        _ret_core_kernel,
        grid=(n // RET_STEP,),
        in_specs=[
            pl.BlockSpec(memory_space=pltpu.SMEM),
            pl.BlockSpec((RET_STEP, RET_QK), lambda i: (i, 0)),
            pl.BlockSpec((RET_STEP, RET_QK), lambda i: (i, 0)),
            pl.BlockSpec((RET_STEP, RET_V), lambda i: (i, 0)),
            pl.BlockSpec((RET_STEP, RET_V), lambda i: (i, 0)),
            pl.BlockSpec((RET_STEP, D_MODEL), lambda i: (i, 0)),
            pl.BlockSpec((1, RET_V), lambda i: (0, 0)),
            pl.BlockSpec((RET_HEADS, c, c), lambda i: (0, 0, 0)),
            pl.BlockSpec((RET_HEADS, c, 128), lambda i: (0, 0, 0)),
            pl.BlockSpec((RET_HEADS, c, 128), lambda i: (0, 0, 0)),
            pl.BlockSpec((RET_V, D_MODEL), lambda i: (0, 0)),
        ],
        out_specs=pl.BlockSpec((RET_STEP, D_MODEL), lambda i: (i, 0)),
        out_shape=jax.ShapeDtypeStruct((n, D_MODEL), F32),
        scratch_shapes=[
            pltpu.VMEM((RET_HEADS, RET_DK, RET_DV), F32),
            pltpu.VMEM((RET_STEP, RET_V), BF16),
            pltpu.VMEM((RET_V, D_MODEL), BF16),
        ],
        compiler_params=_cparams(("arbitrary",)),
        name="ret_core",
    )(jnp.asarray(cdec), q, k, v, gate, x, head_g_row, jnp.asarray(intra), jnp.asarray(cross), jnp.asarray(kdec),
      w_out)


def _conv_pw1_kernel(x_ref, rw_ref, y1_ref, y2_ref, g_ref, w_ref, b_ref, xo_ref, u_ref):
    r = PW1_ROWS
    ngroups = TM_PW1 // r
    w1, w2 = _route_weight_columns(rw_ref[...])

    def prologue(rs):
        lo1, hi1 = _unpack_bf16_pairs(y1_ref[rs, :])
        lo2, hi2 = _unpack_bf16_pairs(y2_ref[rs, :])
        x = jnp.concatenate([x_ref[rs, :HALF] + w1[rs] * lo1 + w2[rs] * lo2,
                             x_ref[rs, HALF:] + w1[rs] * hi1 + w2[rs] * hi2], axis=1)
        xo_ref[rs, :] = x
        hf = _rms(x, g_ref[...])
        return hf.astype(BF16), hf[:, 0:128]

    groups = [slice(gi * r, (gi + 1) * r) for gi in range(ngroups)]
    hs = [prologue(rs) for rs in groups]
    for j in range(D_MODEL // 512):
        wa = w_ref[:, j * 512:(j + 1) * 512].astype(BF16)
        wg = w_ref[:, D_MODEL + j * 512:D_MODEL + (j + 1) * 512].astype(BF16)
        for gi, rs in enumerate(groups):
            h = hs[gi][0]
            a = jnp.dot(h, wa, preferred_element_type=F32) + b_ref[:, j * 512:(j + 1) * 512]
            gt = jnp.dot(h, wg, preferred_element_type=F32) + b_ref[:, D_MODEL + j * 512:D_MODEL + (j + 1) * 512]
            u = a * (1.0 / (1.0 + jnp.exp(-gt)))
            if j == 0 and gi + 1 < ngroups:
                zero = ((pltpu.bitcast(hs[gi + 1][1], U32) >> 16) >> 16).astype(F32)
                u_ref[rs, 0:128] = u[:, 0:128] + zero
                u_ref[rs, 128:512] = u[:, 128:512]
            else:
                u_ref[rs, j * 512:(j + 1) * 512] = u


def _conv_pw1(x, rw, yg, g, w, b):
    n = x.shape[0]
    tm = TM_PW1
    nb = n // tm
    return pl.pallas_call(
        _conv_pw1_kernel,
        grid=(nb,),
        in_specs=[
            pl.BlockSpec((tm, D_MODEL), lambda i: (i, 0)),
            pl.BlockSpec((8, tm), lambda i: (0, i)),
            pl.BlockSpec((tm, HALF), lambda i: (i, 0)),
            pl.BlockSpec((tm, HALF), lambda i: (i + nb, 0)),
            pl.BlockSpec((1, D_MODEL), lambda i: (0, 0)),
            pl.BlockSpec((D_MODEL, 2 * D_MODEL), lambda i: (0, 0), pipeline_mode=pl.Buffered(1)),
            pl.BlockSpec((1, 2 * D_MODEL), lambda i: (0, 0)),
        ],
        out_specs=[pl.BlockSpec((tm, D_MODEL), lambda i: (i, 0)), pl.BlockSpec((tm, D_MODEL), lambda i: (i, 0))],
        out_shape=[jax.ShapeDtypeStruct((n, D_MODEL), F32), jax.ShapeDtypeStruct((n, D_MODEL), F32)],
        compiler_params=_cparams(("arbitrary",)),
        name="conv_pw1",
    )(x, rw, yg, yg, g, w, b)


def _conv_core_kernel(u_ref, halo_ref, x_ref, wdw_ref, bdw_ref, lng_ref, lnb_ref, w2_ref, b2_ref,
                      out_ref, win_ref, z_ref):
    tm = TM_CONV
    first = pl.program_id(0) == 0
    halo = halo_ref[...]
    halo = jnp.where(first, jnp.zeros_like(halo), halo)
    nslab = D_MODEL // 128
    for cc in range(nslab):
        cs = slice(cc * 128, (cc + 1) * 128)
        win_ref[cc, 0:CONV_HALO, :] = halo[:, cs]
        win_ref[cc, CONV_HALO:CONV_HALO + tm, :] = u_ref[:, cs]
    off = CONV_HALO - (CONV_WIDTH - 1)
    rb = CONV_ROWS
    st = CONV_STRIDE
    for cc in range(nslab):
        cs = slice(cc * 128, (cc + 1) * 128)
        for r0 in range(0, tm, rb):
            accs = [bdw_ref[:, cs]] * st
            for o in range(CONV_WIDTH):
                w_o = wdw_ref[o:o + 1, cs]
                for rho in range(st):
                    accs[rho] = accs[rho] + win_ref[cc, pl.ds(r0 + rho + off + o, rb // st, stride=st), :] * w_o
            for rho in range(st):
                z_ref[cc, pl.ds(r0 + rho, rb // st, stride=st), :] = accs[rho]
    z = jnp.concatenate([z_ref[cc] for cc in range(nslab)], axis=1)
    mu = jnp.mean(z, axis=-1, keepdims=True)
    zc = z - mu
    var = jnp.mean(zc * zc, axis=-1, keepdims=True)
    zn = zc * lax.rsqrt(var + NORM_EPS) * lng_ref[...] + lnb_ref[...]
    y = _silu(zn).astype(BF16)
    out_ref[...] = x_ref[...] + jnp.dot(y, w2_ref[...].astype(BF16), preferred_element_type=F32) + b2_ref[...]


def _conv_core(u, x, w_dw, b_dw, ln_g, ln_b, w2, b2):
    n = x.shape[0]
    tm = TM_CONV
    r = tm // CONV_HALO
    wdw_pad = jnp.zeros((32, D_MODEL), F32).at[:CONV_WIDTH].set(w_dw)
    return pl.pallas_call(
        _conv_core_kernel,
        grid=(n // tm,),
        in_specs=[
            pl.BlockSpec((tm, D_MODEL), lambda i: (i, 0)),
            pl.BlockSpec((CONV_HALO, D_MODEL), lambda i: (jnp.maximum(i * r - 1, 0), 0)),
            pl.BlockSpec((tm, D_MODEL), lambda i: (i, 0)),
            pl.BlockSpec((32, D_MODEL), lambda i: (0, 0)),
            pl.BlockSpec((1, D_MODEL), lambda i: (0, 0)),
            pl.BlockSpec((1, D_MODEL), lambda i: (0, 0)),
            pl.BlockSpec((1, D_MODEL), lambda i: (0, 0)),
            pl.BlockSpec((D_MODEL, D_MODEL), lambda i: (0, 0)),
            pl.BlockSpec((1, D_MODEL), lambda i: (0, 0)),
        ],
        out_specs=pl.BlockSpec((tm, D_MODEL), lambda i: (i, 0)),
        out_shape=jax.ShapeDtypeStruct((n, D_MODEL), F32),
        scratch_shapes=[pltpu.VMEM((D_MODEL // 128, CONV_HALO + tm, 128), F32),
                        pltpu.VMEM((D_MODEL // 128, tm, 128), F32)],
        compiler_params=_cparams(("arbitrary",)),
        name="conv_core",
    )(u, u, x, wdw_pad, b_dw, ln_g, ln_b, w2, b2)


def _router_kernel(x_ref, g_ref, wr_ref, br_ref, upper_ref, hp_ref, ri_ref, rw_ref, cnt_ref, carry_ref):
    t = T_ROUTE

    @pl.when(pl.program_id(0) == 0)
    def _():
        carry_ref[...] = jnp.zeros_like(carry_ref)

    h = _rms(x_ref[...], g_ref[...])
    hp_ref[...] = _pack_bf16_pairs(h)
    h_hi = h.astype(BF16)
    h_lo = (h - h_hi.astype(F32)).astype(BF16)
    w = wr_ref[...]
    w_hi = w.astype(BF16)
    w_lo = (w - w_hi.astype(F32)).astype(BF16)
    dn = (((1,), (1,)), ((), ()))
    p = lax.dot_general(jnp.concatenate([w_hi, w_lo], axis=0), h_hi, dn, preferred_element_type=F32)
    nr = wr_ref.shape[0]
    logits = p[0:nr] + p[nr:2 * nr] + lax.dot_general(w_hi, h_lo, dn, preferred_element_type=F32)
    logits = logits + br_ref[:, 0:1]

    best = logits[0:1]
    gi = jnp.zeros((1, t), I32)
    for j in range(1, MOE_GROUPS):
        r = logits[j:j + 1]
        up = r > best
        gi = jnp.where(up, j, gi)
        best = jnp.where(up, r, best)
    den = jnp.zeros((1, t), F32)
    for j in range(MOE_GROUPS):
        den = den + jnp.exp(logits[j:j + 1] - best)
    gate_g = 1.0 / den

    sel = logits[8:8 + MOE_EPG]
    for j in range(1, MOE_GROUPS):
        sel = jnp.where(gi == j, logits[8 + j * MOE_EPG:8 + (j + 1) * MOE_EPG], sel)

    m1 = sel[0:1]
    i1 = jnp.zeros((1, t), I32)
    for j in range(1, MOE_EPG):
        r = sel[j:j + 1]
        up = r > m1
        i1 = jnp.where(up, j, i1)
        m1 = jnp.where(up, r, m1)
    m2 = jnp.full((1, t), -jnp.inf, F32)
    i2 = jnp.zeros((1, t), I32)
    started = jnp.zeros((1, t), jnp.bool_)
    for j in range(MOE_EPG):
        r = sel[j:j + 1]
        ok = i1 != j
        up = ok & ((r > m2) | jnp.logical_not(started))
        i2 = jnp.where(up, j, i2)
        m2 = jnp.where(up, r, m2)
        started = started | ok
    e21 = jnp.exp(m2 - m1)
    p1 = 1.0 / (1.0 + e21)
    w1 = gate_g * p1
    w2 = gate_g * (e21 * p1)
    eid1 = gi * MOE_EPG + i1
    eid2 = gi * MOE_EPG + i2

    eio = lax.broadcasted_iota(I32, (MOE_EXPERTS, t), 0)
    oh1 = eio == eid1
    oh2 = eio == eid2
    oh = (oh1 | oh2).astype(F32)
    cum = jnp.dot(oh.astype(BF16), upper_ref[...], preferred_element_type=F32) + carry_ref[:, 0:1]
    rank1 = jnp.sum(jnp.where(oh1, cum, 0.0), axis=0, keepdims=True)
    rank2 = jnp.sum(jnp.where(oh2, cum, 0.0), axis=0, keepdims=True)
    carry_ref[...] = carry_ref[...] + jnp.sum(oh, axis=1, keepdims=True)
    cnt_ref[...] = carry_ref[...]

    zi = jnp.zeros((4, t), I32)
    ri_ref[...] = jnp.concatenate([eid1, eid2, rank1.astype(I32), rank2.astype(I32), zi], axis=0)
    zf = jnp.zeros((6, t), F32)
    rw_ref[...] = jnp.concatenate([w1, w2, zf], axis=0)


def _router(x, g, w_rg, b_rg, w_re, b_re):
    n = x.shape[0]
    t = T_ROUTE
    wr = jnp.zeros((40, D_MODEL), F32).at[0:MOE_GROUPS].set(w_rg.T).at[8:40].set(w_re.T)
    br = jnp.zeros((40,), F32).at[0:MOE_GROUPS].set(b_rg).at[8:40].set(b_re)
    br = jnp.broadcast_to(br[:, None], (40, 128))
    upper = jnp.asarray(np.triu(np.ones((t, t), np.float32), 1), dtype=BF16)
    return pl.pallas_call(
        _router_kernel,
        grid=(n // t,),
        in_specs=[
            pl.BlockSpec((t, D_MODEL), lambda i: (i, 0)),
            pl.BlockSpec((1, D_MODEL), lambda i: (0, 0)),
            pl.BlockSpec((40, D_MODEL), lambda i: (0, 0)),
            pl.BlockSpec((40, 128), lambda i: (0, 0)),
            pl.BlockSpec((t, t), lambda i: (0, 0)),
        ],
        out_specs=[
            pl.BlockSpec((t, HALF), lambda i: (i, 0)),
            pl.BlockSpec((8, t), lambda i: (0, i)),
            pl.BlockSpec((8, t), lambda i: (0, i)),
            pl.BlockSpec((MOE_EXPERTS, 128), lambda i: (0, 0)),
        ],
        out_shape=[
            jax.ShapeDtypeStruct((n, HALF), U32),
            jax.ShapeDtypeStruct((8, n), I32),
            jax.ShapeDtypeStruct((8, n), F32),
            jax.ShapeDtypeStruct((MOE_EXPERTS, 128), F32),
        ],
        scratch_shapes=[pltpu.VMEM((MOE_EXPERTS, 128), F32)],
        compiler_params=_cparams(("arbitrary",)),
        name="moe_router",
    )(x, g, wr, br, upper)


def _sc_mesh():
    return plsc.VectorSubcoreMesh(core_axis_name="c", subcore_axis_name="s",
                                  num_cores=SC_CORES, num_subcores=SC_SUBCORES)


def _sc_worker_id():
    return lax.axis_index("s") * SC_CORES + lax.axis_index("c")


def _sc_dispatch(hp, idx, zero_rows, total_rows):
    n = hp.shape[0]
    tpw = n // SC_WORKERS
    kd = tpw // SC_ROWS
    kp = idx.shape[1] - 2 * kd
    nb = SC_NBUF

    @functools.partial(
        pl.kernel, mesh=_sc_mesh(),
        out_type=jax.ShapeDtypeStruct((total_rows, HALF), U32),
        scratch_types=[pltpu.VMEM((2 * kd + kp, SC_ROWS), I32)]
        + [pltpu.VMEM((SC_ROWS, HALF), U32)] * (nb + 1)
        + [pltpu.SemaphoreType.DMA((nb,)), pltpu.SemaphoreType.DMA((nb,)), pltpu.SemaphoreType.DMA],
        name="moe_dispatch_sc",
    )
    def k(hp_hbm, idx_hbm, zero_hbm, xs_hbm, idx_v, *rest):
        bufs, zbuf = rest[:nb], rest[nb]
        load_sem, scat_sem, pad_sem = rest[nb + 1:]
        wid = _sc_worker_id()
        pltpu.sync_copy(idx_hbm.at[wid], idx_v)

        def load(c):
            return pltpu.make_async_copy(hp_hbm.at[pl.ds(wid * tpw + c * SC_ROWS, SC_ROWS)], bufs[c % nb],
                                         load_sem.at[c % nb])

        def scatters(c):
            return (pltpu.make_async_copy(bufs[c % nb], xs_hbm.at[idx_v.at[c]], scat_sem.at[c % nb]),
                    pltpu.make_async_copy(bufs[c % nb], xs_hbm.at[idx_v.at[kd + c]], scat_sem.at[c % nb]))

        for c in range(min(nb - 1, kd)):
            load(c).start()
        pltpu.sync_copy(zero_hbm, zbuf)
        pads = [pltpu.make_async_copy(zbuf, xs_hbm.at[idx_v.at[2 * kd + j]], pad_sem) for j in range(kp)]
        for p in pads:
            p.start()
        for c in range(kd):
            load(c).wait()
            for d in scatters(c):
                d.start()
            if c + nb - 1 < kd:
                if c >= 1:
                    for d in scatters(c - 1):
                        d.wait()
                load(c + nb - 1).start()
        for c in range(max(kd - nb, 0), kd):
            for d in scatters(c):
                d.wait()
        for p in pads:
            p.wait()

    return k(hp, idx, zero_rows)


def _sc_gather(ys, idx):
    kg = idx.shape[1]
    rows_per_worker = kg * SC_ROWS
    nb = SC_NBUF

    @functools.partial(
        pl.kernel, mesh=_sc_mesh(),
        out_type=jax.ShapeDtypeStruct((SC_WORKERS * rows_per_worker, HALF), U32),
        scratch_types=[pltpu.VMEM((kg, SC_ROWS), I32)] + [pltpu.VMEM((SC_ROWS, HALF), U32)] * nb
        + [pltpu.SemaphoreType.DMA((nb,)), pltpu.SemaphoreType.DMA((nb,))],
        name="moe_gather_sc",
    )
    def k(ys_hbm, idx_hbm, yg_hbm, idx_v, *rest):
        bufs = rest[:nb]
        gat_sem, out_sem = rest[nb:]
        wid = _sc_worker_id()
        pltpu.sync_copy(idx_hbm.at[wid], idx_v)

        def gather(c):
            return pltpu.make_async_copy(ys_hbm.at[idx_v.at[c]], bufs[c % nb], gat_sem.at[c % nb])

        def store(c):
            return pltpu.make_async_copy(bufs[c % nb],
                                         yg_hbm.at[pl.ds(wid * rows_per_worker + c * SC_ROWS, SC_ROWS)],
                                         out_sem.at[c % nb])

        for c in range(min(nb - 1, kg)):
            gather(c).start()
        for c in range(kg):
            gather(c).wait()
            store(c).start()
            if c + nb - 1 < kg:
                if c >= 1:
                    store(c - 1).wait()
                gather(c + nb - 1).start()
        for c in range(max(kg - nb, 0), kg):
            store(c).wait()

    return k(ys, idx)


def _expert_kernel(blk_e_ref, nused_ref, first_ref, slot_ref, nxt_ref, xs_ref, wg_hbm, wu_hbm, wd_hbm,
                   ys_ref, wg_buf, wu_buf, wd_buf, sems, *, layer):
    step = pl.program_id(0)
    tm = TM_EXP
    nused = nused_ref[0]

    def weight_copies(e, s):
        return (pltpu.make_async_copy(wg_hbm.at[layer, e], wg_buf.at[s], sems.at[s, 0]),
                pltpu.make_async_copy(wu_hbm.at[layer, e], wu_buf.at[s], sems.at[s, 1]),
                pltpu.make_async_copy(wd_hbm.at[layer, e], wd_buf.at[s], sems.at[s, 2]))

    def dma_control(j):
        i = step * EXP_SUB + j

        @pl.when(i < nused)
        def _():
            s = slot_ref[i]

            if j == 0:
                @pl.when(i == 0)
                def _():
                    for c in weight_copies(blk_e_ref[0], 0):
                        c.start()

            @pl.when(first_ref[i] == 1)
            def _():
                for c in weight_copies(blk_e_ref[i], s):
                    c.wait()

                @pl.when(nxt_ref[i] >= 0)
                def _():
                    for c in weight_copies(nxt_ref[i], lax.rem(s + 1, EXP_SLOTS)):
                        c.start()

    for j in range(EXP_SUB):
        dma_control(j)

    @pl.when(step * EXP_SUB < nused)
    def _():
        for j in range(EXP_SUB):
            i = step * EXP_SUB + j
            rows = slice(j * tm, (j + 1) * tm)
            s = slot_ref[i]
            lo, hi = _unpack_bf16_pairs(xs_ref[rows, :])
            xf = jnp.concatenate([lo, hi], axis=1)
            a = jnp.dot(xf, wg_buf[s], preferred_element_type=F32)
            b = jnp.dot(xf, wu_buf[s], preferred_element_type=F32)
            hm = _silu(a) * b
            y = jnp.dot(hm, wd_buf[s], preferred_element_type=F32)
            ys_ref[rows, :] = jnp.where(i < nused, _pack_bf16_pairs(y), jnp.uint32(0))

    @pl.when(step * EXP_SUB >= nused)
    def _():
        ys_ref[...] = jnp.zeros_like(ys_ref)


def _experts(xs, blk_e, nused, w_gate, w_up, w_down, layer):
    tm = TM_EXP
    p_rows = xs.shape[0] - MOE_EXPERTS * tm
    nblk = p_rows // tm
    pos = jnp.arange(nblk, dtype=I32)
    valid = pos < nused[0]
    prev_e = jnp.concatenate([jnp.full((1,), -1, I32), blk_e[:-1]])
    first = valid & (blk_e != prev_e)
    slot = jnp.maximum(jnp.cumsum(first.astype(I32)) - 1, 0) % EXP_SLOTS
    first_pos = jnp.where(first, pos, nblk)
    next_first = jnp.concatenate([lax.cummin(first_pos, reverse=True)[1:], jnp.full((1,), nblk, I32)])
    nxt = jnp.where(next_first < nblk, blk_e[jnp.minimum(next_first, nblk - 1)], -1)

    def blk(i, be, nu, *_):
        return jnp.minimum(i, (nu[0] - 1) // EXP_SUB)

    grid_spec = pltpu.PrefetchScalarGridSpec(
        num_scalar_prefetch=5,
        grid=(nblk // EXP_SUB,),
        in_specs=[
            pl.BlockSpec((EXP_SUB * tm, HALF), lambda i, *sp: (blk(i, *sp), 0)),
            pl.BlockSpec(memory_space=pl.ANY),
            pl.BlockSpec(memory_space=pl.ANY),
            pl.BlockSpec(memory_space=pl.ANY),
        ],
        out_specs=pl.BlockSpec((EXP_SUB * tm, HALF), lambda i, *sp: (i, 0)),
        scratch_shapes=[
            pltpu.VMEM((EXP_SLOTS, D_MODEL, MOE_FF), F32),
            pltpu.VMEM((EXP_SLOTS, D_MODEL, MOE_FF), F32),
            pltpu.VMEM((EXP_SLOTS, MOE_FF, D_MODEL), F32),
            pltpu.SemaphoreType.DMA((EXP_SLOTS, 3)),
        ],
    )
    return pl.pallas_call(
        functools.partial(_expert_kernel, layer=layer),
        grid_spec=grid_spec,
        out_shape=jax.ShapeDtypeStruct((p_rows, HALF), U32),
        compiler_params=_cparams(("arbitrary",)),
        name="moe_experts",
    )(blk_e, nused, first.astype(I32), slot.astype(I32), nxt.astype(I32), xs, w_gate, w_up, w_down)


def _combine_kernel(x_ref, rw_ref, fg_ref, y1_ref, y2_ref, out_ref):
    lo1, hi1 = _unpack_bf16_pairs(y1_ref[...])
    lo2, hi2 = _unpack_bf16_pairs(y2_ref[...])
    w1, w2 = _route_weight_columns(rw_ref[...])
    x = x_ref[...]
    o_lo = x[:, :HALF] + w1 * lo1 + w2 * lo2
    o_hi = x[:, HALF:] + w1 * hi1 + w2 * hi2
    ms = (jnp.sum(o_lo * o_lo, axis=-1, keepdims=True)
          + jnp.sum(o_hi * o_hi, axis=-1, keepdims=True)) * (1.0 / D_MODEL)
    sc = lax.rsqrt(ms + NORM_EPS)
    o_lo = o_lo * sc * fg_ref[:, :HALF]
    o_hi = o_hi * sc * fg_ref[:, HALF:]
    out_ref[:, :HALF] = o_lo
    out_ref[:, HALF:] = o_hi


def _combine(yg, x, rw, final_g):
    n = x.shape[0]
    td = T_COMB
    nb = n // td
    return pl.pallas_call(
        _combine_kernel,
        grid=(nb,),
        in_specs=[
            pl.BlockSpec((td, D_MODEL), lambda i: (i, 0)),
            pl.BlockSpec((8, td), lambda i: (0, i)),
            pl.BlockSpec((1, D_MODEL), lambda i: (0, 0)),
            pl.BlockSpec((td, HALF), lambda i: (i, 0)),
            pl.BlockSpec((td, HALF), lambda i: (i + nb, 0)),
        ],
        out_specs=pl.BlockSpec((td, D_MODEL), lambda i: (i, 0)),
        out_shape=jax.ShapeDtypeStruct((n, D_MODEL), F32),
        compiler_params=_cparams(("arbitrary",)),
        name="moe_combine",
    )(x, rw, final_g, yg, yg)


def _moe(x, g, w_rg, b_rg, w_re, b_re, w_gate, w_up, w_down, layer):
    n = x.shape[0]
    tm = TM_EXP
    p_rows = 2 * n + MOE_EXPERTS * tm
    nblk = p_rows // tm
    hp, ri, rw, cnt = _router(x, g, w_rg, b_rg, w_re, b_re)
    counts = cnt[:, 0].astype(I32)
    pcounts = (counts + tm - 1) // tm * tm
    eio = jnp.arange(MOE_EXPERTS, dtype=I32)[:, None]
    pend = jnp.sum(jnp.where(eio.T <= eio, pcounts[None, :], 0), axis=1)
    pstart = pend - pcounts
    dest1 = jnp.sum(jnp.where(ri[0][None, :] == eio, pstart[:, None], 0), axis=0) + ri[2]
    dest2 = jnp.sum(jnp.where(ri[1][None, :] == eio, pstart[:, None], 0), axis=0) + ri[3]
    blk_start = jnp.arange(nblk, dtype=I32) * tm
    blk_e = jnp.minimum(jnp.sum((pend[None, :] <= blk_start[:, None]).astype(I32), axis=1), MOE_EXPERTS - 1)
    nused = jnp.maximum(pend[-1] // tm, 1).astype(I32).reshape(1)
    r = jnp.arange(tm, dtype=I32)[None, :]
    pad_slot = jnp.where(r < (pcounts - counts)[:, None], (pstart + counts)[:, None] + r, p_rows + eio * tm + r)
    kd = n // SC_WORKERS // SC_ROWS
    idx = jnp.concatenate([dest1.reshape(SC_WORKERS, kd, SC_ROWS), dest2.reshape(SC_WORKERS, kd, SC_ROWS),
                           pad_slot.reshape(SC_WORKERS, -1, SC_ROWS)], axis=1)
    zero_rows = jnp.zeros((SC_ROWS, HALF), U32)
    xs = _sc_dispatch(hp, idx, zero_rows, p_rows + MOE_EXPERTS * tm)
    ys = _experts(xs, blk_e, nused, w_gate, w_up, w_down, layer)
    gidx = jnp.concatenate([dest1, dest2]).reshape(SC_WORKERS, -1, SC_ROWS)
    yg = _sc_gather(ys, gidx)
    return yg, rw


def kernel(x, positions, norm_mix_g, norm_ffn_g, ret_w_in, ret_head_g, ret_w_out, conv_w_pw1, conv_b_pw1, conv_w_dw, conv_b_dw, conv_ln_g, conv_ln_b, conv_w_pw2, conv_b_pw2, moe_w_rg, moe_b_rg, moe_w_re, moe_b_re, moe_w_gate, moe_w_up, moe_w_down, final_norm_g):
    b, s, d = x.shape
    n = b * s
    xt = x.reshape(n, d)
    pos = positions.reshape(1, n)
    fg = final_norm_g.reshape(1, d)

    q, k, v, gate = _ret_inproj(xt, pos, norm_mix_g[0].reshape(1, d), ret_w_in[0])
    xt = _ret_core(q, k, v, gate, xt, ret_head_g[0].reshape(1, RET_V), ret_w_out[0])
    yg, rw = _moe(xt, norm_ffn_g[0].reshape(1, d), moe_w_rg[0], moe_b_rg[0], moe_w_re[0], moe_b_re[0],
                  moe_w_gate, moe_w_up, moe_w_down, 0)

    xt, u = _conv_pw1(xt, rw, yg, norm_mix_g[1].reshape(1, d), conv_w_pw1[0],
                      conv_b_pw1[0].reshape(1, 2 * d))
    xt = _conv_core(u, xt, conv_w_dw[0], conv_b_dw[0].reshape(1, d), conv_ln_g[0].reshape(1, d),
                    conv_ln_b[0].reshape(1, d), conv_w_pw2[0], conv_b_pw2[0].reshape(1, d))
    yg, rw = _moe(xt, norm_ffn_g[1].reshape(1, d), moe_w_rg[1], moe_b_rg[1], moe_w_re[1], moe_b_re[1],
                  moe_w_gate, moe_w_up, moe_w_down, 1)
    xt = _combine(yg, xt, rw, fg)
    return xt.reshape(b, s, d)
```

```python
import functools

import jax
import jax.numpy as jnp
import numpy as np
from jax import lax
from jax.experimental import pallas as pl
from jax.experimental.pallas import tpu as pltpu
from jax.experimental.pallas import tpu_sc as plsc

F32 = jnp.float32
BF16 = jnp.bfloat16
U32 = jnp.uint32
I32 = jnp.int32

D_MODEL = 1024
RET_HEADS = 4
RET_DK = 256
RET_DV = 512
RET_QK = RET_HEADS * RET_DK
RET_V = RET_HEADS * RET_DV
ROPE_BASE = 10000.0
CONV_WIDTH = 31
MOE_GROUPS = 4
MOE_EPG = 8
MOE_EXPERTS = MOE_GROUPS * MOE_EPG
MOE_FF = 512
NORM_EPS = 1e-6

TM_PROJ = 512
RET_C = 256
RET_STEP = 512
TM_CONV = 1024
TM_PW1 = 1024
PW1_ROWS = 256
CONV_HALO = 32
CONV_ROWS = 128
CONV_STRIDE = 4
T_ROUTE = 1024
TM_EXP = 256
EXP_SUB = 4
EXP_SLOTS = EXP_SUB + 1
T_COMB = 1024
SC_CORES = 2
SC_SUBCORES = 16
SC_WORKERS = SC_CORES * SC_SUBCORES
SC_ROWS = 32
SC_NBUF = 4
HALF = D_MODEL // 2

VMEM_LIMIT = 56 * 1024 * 1024


def _cparams(sem, flags=None):
    return pltpu.CompilerParams(dimension_semantics=sem, vmem_limit_bytes=VMEM_LIMIT, flags=flags)


def _rms(x, g):
    ms = jnp.mean(x * x, axis=-1, keepdims=True)
    return x * lax.rsqrt(ms + NORM_EPS) * g


def _silu(x):
    return x * (1.0 / (1.0 + jnp.exp(-x)))


def _pack_bf16_pairs(y):
    lo = pltpu.bitcast(y[:, :HALF].astype(BF16).astype(F32), U32)
    hi = pltpu.bitcast(y[:, HALF:].astype(BF16).astype(F32), U32)
    return (hi & jnp.uint32(0xFFFF0000)) | (lo >> 16)


def _route_weight_columns(rw):
    t = jnp.concatenate([rw] * 16, axis=0).T
    return t[:, 0:1], t[:, 1:2]


def _unpack_bf16_pairs(p):
    lo = pltpu.bitcast(p << 16, F32)
    hi = pltpu.bitcast(p & jnp.uint32(0xFFFF0000), F32)
    return lo, hi


def _ret_inproj_kernel(x_ref, pos_ref, g_ref, inv_ref, w_ref, q_ref, k_ref, v_ref, gate_ref):
    half = RET_DK // 2
    kscale = RET_DK ** -0.5
    h = _rms(x_ref[...], g_ref[...]).astype(BF16)

    def proj(c0, width):
        return jnp.dot(h, w_ref[:, c0:c0 + width].astype(BF16), preferred_element_type=F32)

    v0 = proj(2 * RET_QK, 512)
    v_ref[:, 0:512] = v0.astype(BF16)
    zero = ((pltpu.bitcast(v0[:, 0:half], U32) >> 16) >> 16).astype(F32)
    pos = jnp.broadcast_to(pos_ref[...].astype(F32), (half, TM_PROJ)).T
    ang = pos * inv_ref[...] + zero
    cos = jnp.cos(ang)
    sin = jnp.sin(ang)
    for j in range(RET_V // 512):
        if j > 0:
            v_ref[:, j * 512:(j + 1) * 512] = proj(2 * RET_QK + j * 512, 512).astype(BF16)
        gate_ref[:, j * 512:(j + 1) * 512] = proj(2 * RET_QK + RET_V + j * 512, 512).astype(BF16)

    for hd in range(RET_HEADS):
        for base, out_ref, cs, sn in ((0, q_ref, cos, sin), (RET_QK, k_ref, cos * kscale, sin * kscale)):
            t = proj(base + hd * RET_DK, RET_DK)
            t1 = t[:, :half]
            t2 = t[:, half:]
            out_ref[:, hd * RET_DK:hd * RET_DK + half] = (t1 * cs - t2 * sn).astype(BF16)
            out_ref[:, hd * RET_DK + half:(hd + 1) * RET_DK] = (t1 * sn + t2 * cs).astype(BF16)


def _ret_inproj(x, pos, g, w_in):
    n = x.shape[0]
    half = RET_DK // 2
    inv = (ROPE_BASE ** (-jnp.arange(half, dtype=F32) / half)).reshape(1, half)
    tm = TM_PROJ
    return pl.pallas_call(
        _ret_inproj_kernel,
        grid=(n // tm,),
        in_specs=[
            pl.BlockSpec((tm, D_MODEL), lambda i: (i, 0)),
            pl.BlockSpec((1, tm), lambda i: (0, i)),
            pl.BlockSpec((1, D_MODEL), lambda i: (0, 0)),
            pl.BlockSpec((1, half), lambda i: (0, 0)),
            pl.BlockSpec(w_in.shape, lambda i: (0, 0), pipeline_mode=pl.Buffered(1)),
        ],
        out_specs=[
            pl.BlockSpec((tm, RET_QK), lambda i: (i, 0)),
            pl.BlockSpec((tm, RET_QK), lambda i: (i, 0)),
            pl.BlockSpec((tm, RET_V), lambda i: (i, 0)),
            pl.BlockSpec((tm, RET_V), lambda i: (i, 0)),
        ],
        out_shape=[
            jax.ShapeDtypeStruct((n, RET_QK), BF16),
            jax.ShapeDtypeStruct((n, RET_QK), BF16),
            jax.ShapeDtypeStruct((n, RET_V), BF16),
            jax.ShapeDtypeStruct((n, RET_V), BF16),
        ],
        compiler_params=_cparams(("arbitrary",)),
        name="ret_inproj",
    )(x, pos, g, inv, w_in)


def _ret_core_kernel(cdec_ref, q_ref, k_ref, v_ref, gate_ref, x_ref, hg_ref, intra_ref, cross_ref,
                     kdec_ref, wo_ref, out_ref, state_ref, y_ref, wos_ref):
    @pl.when(pl.program_id(0) == 0)
    def _():
        state_ref[...] = jnp.zeros_like(state_ref)
        hg = jnp.broadcast_to(hg_ref[...], (128, RET_V)).T[:, 0:1]
        wos_ref[...] = (wo_ref[...] * hg).astype(BF16)

    for r0 in range(0, RET_STEP, RET_C):
        rs = slice(r0, r0 + RET_C)
        for hd in range(RET_HEADS):
            q = q_ref[rs, hd * RET_DK:(hd + 1) * RET_DK]
            k = k_ref[rs, hd * RET_DK:(hd + 1) * RET_DK]
            v = v_ref[rs, hd * RET_DV:(hd + 1) * RET_DV]
            state = state_ref[hd]
            scores = lax.dot_general(q, k, (((1,), (1,)), ((), ())), preferred_element_type=F32)
            scores = (scores * intra_ref[hd]).astype(BF16)
            o = jnp.dot(scores, v, preferred_element_type=F32)
            cross = cross_ref[hd]
            o_cross = jnp.dot(q, state.astype(BF16), preferred_element_type=F32)
            o = o + o_cross * jnp.concatenate([cross] * (RET_DV // 128), axis=1)
            kdec = kdec_ref[hd]
            kd = (k.astype(F32) * jnp.concatenate([kdec] * (RET_DK // 128), axis=1)).astype(BF16)
            upd = lax.dot_general(kd, v, (((0,), (0,)), ((), ())), preferred_element_type=F32)
            state_ref[hd] = state * cdec_ref[hd] + upd
            ms = jnp.mean(o * o, axis=-1, keepdims=True)
            on = o * lax.rsqrt(ms + NORM_EPS)
            gt = gate_ref[rs, hd * RET_DV:(hd + 1) * RET_DV].astype(F32)
            y_ref[rs, hd * RET_DV:(hd + 1) * RET_DV] = (_silu(gt) * on).astype(BF16)
        out_ref[rs, :] = x_ref[rs, :] + jnp.dot(y_ref[rs, :], wos_ref[...], preferred_element_type=F32)


def _ret_core(q, k, v, gate, x, head_g_row, w_out):
    n = x.shape[0]
    c = RET_C
    f = np.float32
    log_gamma = np.log1p(-(f(2.0) ** (f(-5.0) - np.arange(RET_HEADS, dtype=f)))).astype(f)
    idx = np.arange(c, dtype=f)
    diff = idx[:, None] - idx[None, :]
    intra = np.where(diff >= 0, np.exp(log_gamma[:, None, None] * np.maximum(diff, f(0.0))), f(0.0)).astype(f)
    cross = np.broadcast_to(np.exp(log_gamma[:, None] * (idx + f(1.0)))[:, :, None], (RET_HEADS, c, 128)).astype(f)
    kdec = np.broadcast_to(np.exp(log_gamma[:, None] * (f(c - 1.0) - idx))[:, :, None],
                           (RET_HEADS, c, 128)).astype(f)
    cdec = np.exp(log_gamma * f(c)).astype(f)
    return pl.pallas_call(
        _ret_core_kernel,
        grid=(n // RET_STEP,),
        in_specs=[
            pl.BlockSpec(memory_space=pltpu.SMEM),
            pl.BlockSpec((RET_STEP, RET_QK), lambda i: (i, 0)),
            pl.BlockSpec((RET_STEP, RET_QK), lambda i: (i, 0)),
            pl.BlockSpec((RET_STEP, RET_V), lambda i: (i, 0)),
            pl.BlockSpec((RET_STEP, RET_V), lambda i: (i, 0)),
            pl.BlockSpec((RET_STEP, D_MODEL), lambda i: (i, 0)),
            pl.BlockSpec((1, RET_V), lambda i: (0, 0)),
            pl.BlockSpec((RET_HEADS, c, c), lambda i: (0, 0, 0)),
            pl.BlockSpec((RET_HEADS, c, 128), lambda i: (0, 0, 0)),
            pl.BlockSpec((RET_HEADS, c, 128), lambda i: (0, 0, 0)),
            pl.BlockSpec((RET_V, D_MODEL), lambda i: (0, 0)),
        ],
        out_specs=pl.BlockSpec((RET_STEP, D_MODEL), lambda i: (i, 0)),
        out_shape=jax.ShapeDtypeStruct((n, D_MODEL), F32),
        scratch_shapes=[
            pltpu.VMEM((RET_HEADS, RET_DK, RET_DV), F32),
            pltpu.VMEM((RET_STEP, RET_V), BF16),
            pltpu.VMEM((RET_V, D_MODEL), BF16),
        ],
        compiler_params=_cparams(("arbitrary",)),
        name="ret_core",
    )(jnp.asarray(cdec), q, k, v, gate, x, head_g_row, jnp.asarray(intra), jnp.asarray(cross), jnp.asarray(kdec),
      w_out)


def _conv_pw1_kernel(x_ref, rw_ref, y1_ref, y2_ref, g_ref, w_ref, b_ref, xo_ref, u_ref):
    r = PW1_ROWS
    ngroups = TM_PW1 // r
    w1, w2 = _route_weight_columns(rw_ref[...])

    def prologue(rs):
        lo1, hi1 = _unpack_bf16_pairs(y1_ref[rs, :])
        lo2, hi2 = _unpack_bf16_pairs(y2_ref[rs, :])
        x = jnp.concatenate([x_ref[rs, :HALF] + w1[rs] * lo1 + w2[rs] * lo2,
                             x_ref[rs, HALF:] + w1[rs] * hi1 + w2[rs] * hi2], axis=1)
        xo_ref[rs, :] = x
        hf = _rms(x, g_ref[...])
        return hf.astype(BF16), hf[:, 0:128]

    groups = [slice(gi * r, (gi + 1) * r) for gi in range(ngroups)]
    hs = [prologue(rs) for rs in groups]
    for j in range(D_MODEL // 512):
        wa = w_ref[:, j * 512:(j + 1) * 512].astype(BF16)
        wg = w_ref[:, D_MODEL + j * 512:D_MODEL + (j + 1) * 512].astype(BF16)
        for gi, rs in enumerate(groups):
            h = hs[gi][0]
            a = jnp.dot(h, wa, preferred_element_type=F32) + b_ref[:, j * 512:(j + 1) * 512]
            gt = jnp.dot(h, wg, preferred_element_type=F32) + b_ref[:, D_MODEL + j * 512:D_MODEL + (j + 1) * 512]
            u = a * (1.0 / (1.0 + jnp.exp(-gt)))
            if j == 0 and gi + 1 < ngroups:
                zero = ((pltpu.bitcast(hs[gi + 1][1], U32) >> 16) >> 16).astype(F32)
                u_ref[rs, 0:128] = u[:, 0:128] + zero
                u_ref[rs, 128:512] = u[:, 128:512]
            else:
                u_ref[rs, j * 512:(j + 1) * 512] = u


def _conv_pw1(x, rw, yg, g, w, b):
    n = x.shape[0]
    tm = TM_PW1
    nb = n // tm
    return pl.pallas_call(
        _conv_pw1_kernel,
        grid=(nb,),
        in_specs=[
            pl.BlockSpec((tm, D_MODEL), lambda i: (i, 0)),
            pl.BlockSpec((8, tm), lambda i: (0, i)),
            pl.BlockSpec((tm, HALF), lambda i: (i, 0)),
            pl.BlockSpec((tm, HALF), lambda i: (i + nb, 0)),
            pl.BlockSpec((1, D_MODEL), lambda i: (0, 0)),
            pl.BlockSpec((D_MODEL, 2 * D_MODEL), lambda i: (0, 0), pipeline_mode=pl.Buffered(1)),
            pl.BlockSpec((1, 2 * D_MODEL), lambda i: (0, 0)),
        ],
        out_specs=[pl.BlockSpec((tm, D_MODEL), lambda i: (i, 0)), pl.BlockSpec((tm, D_MODEL), lambda i: (i, 0))],
        out_shape=[jax.ShapeDtypeStruct((n, D_MODEL), F32), jax.ShapeDtypeStruct((n, D_MODEL), F32)],
        compiler_params=_cparams(("arbitrary",)),
        name="conv_pw1",
    )(x, rw, yg, yg, g, w, b)


def _conv_core_kernel(u_ref, halo_ref, x_ref, wdw_ref, bdw_ref, lng_ref, lnb_ref, w2_ref, b2_ref,
                      out_ref, win_ref, z_ref):
    tm = TM_CONV
    first = pl.program_id(0) == 0
    halo = halo_ref[...]
    halo = jnp.where(first, jnp.zeros_like(halo), halo)
    nslab = D_MODEL // 128
    for cc in range(nslab):
        cs = slice(cc * 128, (cc + 1) * 128)
        win_ref[cc, 0:CONV_HALO, :] = halo[:, cs]
        win_ref[cc, CONV_HALO:CONV_HALO + tm, :] = u_ref[:, cs]
    off = CONV_HALO - (CONV_WIDTH - 1)
    rb = CONV_ROWS
    st = CONV_STRIDE
    for cc in range(nslab):
        cs = slice(cc * 128, (cc + 1) * 128)
        for r0 in range(0, tm, rb):
            accs = [bdw_ref[:, cs]] * st
            for o in range(CONV_WIDTH):
                w_o = wdw_ref[o:o + 1, cs]
                for rho in range(st):
                    accs[rho] = accs[rho] + win_ref[cc, pl.ds(r0 + rho + off + o, rb // st, stride=st), :] * w_o
            for rho in range(st):
                z_ref[cc, pl.ds(r0 + rho, rb // st, stride=st), :] = accs[rho]
    z = jnp.concatenate([z_ref[cc] for cc in range(nslab)], axis=1)
    mu = jnp.mean(z, axis=-1, keepdims=True)
    zc = z - mu
    var = jnp.mean(zc * zc, axis=-1, keepdims=True)
    zn = zc * lax.rsqrt(var + NORM_EPS) * lng_ref[...] + lnb_ref[...]
    y = _silu(zn).astype(BF16)
    out_ref[...] = x_ref[...] + jnp.dot(y, w2_ref[...].astype(BF16), preferred_element_type=F32) + b2_ref[...]


def _conv_core(u, x, w_dw, b_dw, ln_g, ln_b, w2, b2):
    n = x.shape[0]
    tm = TM_CONV
    r = tm // CONV_HALO
    wdw_pad = jnp.zeros((32, D_MODEL), F32).at[:CONV_WIDTH].set(w_dw)
    return pl.pallas_call(
        _conv_core_kernel,
        grid=(n // tm,),
        in_specs=[
            pl.BlockSpec((tm, D_MODEL), lambda i: (i, 0)),
            pl.BlockSpec((CONV_HALO, D_MODEL), lambda i: (jnp.maximum(i * r - 1, 0), 0)),
            pl.BlockSpec((tm, D_MODEL), lambda i: (i, 0)),
            pl.BlockSpec((32, D_MODEL), lambda i: (0, 0)),
            pl.BlockSpec((1, D_MODEL), lambda i: (0, 0)),
            pl.BlockSpec((1, D_MODEL), lambda i: (0, 0)),
            pl.BlockSpec((1, D_MODEL), lambda i: (0, 0)),
            pl.BlockSpec((D_MODEL, D_MODEL), lambda i: (0, 0)),
            pl.BlockSpec((1, D_MODEL), lambda i: (0, 0)),
        ],
        out_specs=pl.BlockSpec((tm, D_MODEL), lambda i: (i, 0)),
        out_shape=jax.ShapeDtypeStruct((n, D_MODEL), F32),
        scratch_shapes=[pltpu.VMEM((D_MODEL // 128, CONV_HALO + tm, 128), F32),
                        pltpu.VMEM((D_MODEL // 128, tm, 128), F32)],
        compiler_params=_cparams(("arbitrary",)),
        name="conv_core",
    )(u, u, x, wdw_pad, b_dw, ln_g, ln_b, w2, b2)


def _router_kernel(x_ref, g_ref, wr_ref, br_ref, upper_ref, hp_ref, ri_ref, rw_ref, cnt_ref, carry_ref):
    t = T_ROUTE

    @pl.when(pl.program_id(0) == 0)
    def _():
        carry_ref[...] = jnp.zeros_like(carry_ref)

    h = _rms(x_ref[...], g_ref[...])
    hp_ref[...] = _pack_bf16_pairs(h)
    h_hi = h.astype(BF16)
    h_lo = (h - h_hi.astype(F32)).astype(BF16)
    w = wr_ref[...]
    w_hi = w.astype(BF16)
    w_lo = (w - w_hi.astype(F32)).astype(BF16)
    dn = (((1,), (1,)), ((), ()))
    p = lax.dot_general(jnp.concatenate([w_hi, w_lo], axis=0), h_hi, dn, preferred_element_type=F32)
    nr = wr_ref.shape[0]
    logits = p[0:nr] + p[nr:2 * nr] + lax.dot_general(w_hi, h_lo, dn, preferred_element_type=F32)
    logits = logits + br_ref[:, 0:1]

    best = logits[0:1]
    gi = jnp.zeros((1, t), I32)
    for j in range(1, MOE_GROUPS):
        r = logits[j:j + 1]
        up = r > best
        gi = jnp.where(up, j, gi)
        best = jnp.where(up, r, best)
    den = jnp.zeros((1, t), F32)
    for j in range(MOE_GROUPS):
        den = den + jnp.exp(logits[j:j + 1] - best)
    gate_g = 1.0 / den

    sel = logits[8:8 + MOE_EPG]
    for j in range(1, MOE_GROUPS):
        sel = jnp.where(gi == j, logits[8 + j * MOE_EPG:8 + (j + 1) * MOE_EPG], sel)

    m1 = sel[0:1]
    i1 = jnp.zeros((1, t), I32)
    for j in range(1, MOE_EPG):
        r = sel[j:j + 1]
        up = r > m1
        i1 = jnp.where(up, j, i1)
        m1 = jnp.where(up, r, m1)
    m2 = jnp.full((1, t), -jnp.inf, F32)
    i2 = jnp.zeros((1, t), I32)
    started = jnp.zeros((1, t), jnp.bool_)
    for j in range(MOE_EPG):
        r = sel[j:j + 1]
        ok = i1 != j
        up = ok & ((r > m2) | jnp.logical_not(started))
        i2 = jnp.where(up, j, i2)
        m2 = jnp.where(up, r, m2)
        started = started | ok
    e21 = jnp.exp(m2 - m1)
    p1 = 1.0 / (1.0 + e21)
    w1 = gate_g * p1
    w2 = gate_g * (e21 * p1)
    eid1 = gi * MOE_EPG + i1
    eid2 = gi * MOE_EPG + i2

    eio = lax.broadcasted_iota(I32, (MOE_EXPERTS, t), 0)
    oh1 = eio == eid1
    oh2 = eio == eid2
    oh = (oh1 | oh2).astype(F32)
    cum = jnp.dot(oh.astype(BF16), upper_ref[...], preferred_element_type=F32) + carry_ref[:, 0:1]
    rank1 = jnp.sum(jnp.where(oh1, cum, 0.0), axis=0, keepdims=True)
    rank2 = jnp.sum(jnp.where(oh2, cum, 0.0), axis=0, keepdims=True)
    carry_ref[...] = carry_ref[...] + jnp.sum(oh, axis=1, keepdims=True)
    cnt_ref[...] = carry_ref[...]

    zi = jnp.zeros((4, t), I32)
    ri_ref[...] = jnp.concatenate([eid1, eid2, rank1.astype(I32), rank2.astype(I32), zi], axis=0)
    zf = jnp.zeros((6, t), F32)
    rw_ref[...] = jnp.concatenate([w1, w2, zf], axis=0)


def _router(x, g, w_rg, b_rg, w_re, b_re):
    n = x.shape[0]
    t = T_ROUTE
    wr = jnp.zeros((40, D_MODEL), F32).at[0:MOE_GROUPS].set(w_rg.T).at[8:40].set(w_re.T)
    br = jnp.zeros((40,), F32).at[0:MOE_GROUPS].set(b_rg).at[8:40].set(b_re)
    br = jnp.broadcast_to(br[:, None], (40, 128))
    upper = jnp.asarray(np.triu(np.ones((t, t), np.float32), 1), dtype=BF16)
    return pl.pallas_call(
        _router_kernel,
        grid=(n // t,),
        in_specs=[
            pl.BlockSpec((t, D_MODEL), lambda i: (i, 0)),
            pl.BlockSpec((1, D_MODEL), lambda i: (0, 0)),
            pl.BlockSpec((40, D_MODEL), lambda i: (0, 0)),
            pl.BlockSpec((40, 128), lambda i: (0, 0)),
            pl.BlockSpec((t, t), lambda i: (0, 0)),
        ],
        out_specs=[
            pl.BlockSpec((t, HALF), lambda i: (i, 0)),
            pl.BlockSpec((8, t), lambda i: (0, i)),
            pl.BlockSpec((8, t), lambda i: (0, i)),
            pl.BlockSpec((MOE_EXPERTS, 128), lambda i: (0, 0)),
        ],
        out_shape=[
            jax.ShapeDtypeStruct((n, HALF), U32),
            jax.ShapeDtypeStruct((8, n), I32),
            jax.ShapeDtypeStruct((8, n), F32),
            jax.ShapeDtypeStruct((MOE_EXPERTS, 128), F32),
        ],
        scratch_shapes=[pltpu.VMEM((MOE_EXPERTS, 128), F32)],
        compiler_params=_cparams(("arbitrary",)),
        name="moe_router",
    )(x, g, wr, br, upper)


def _sc_mesh():
    return plsc.VectorSubcoreMesh(core_axis_name="c", subcore_axis_name="s",
                                  num_cores=SC_CORES, num_subcores=SC_SUBCORES)


def _sc_worker_id():
    return lax.axis_index("s") * SC_CORES + lax.axis_index("c")


def _sc_dispatch(hp, idx, zero_rows, total_rows):
    n = hp.shape[0]
    tpw = n // SC_WORKERS
    kd = tpw // SC_ROWS
    kp = idx.shape[1] - 2 * kd
    nb = SC_NBUF

    @functools.partial(
        pl.kernel, mesh=_sc_mesh(),
        out_type=jax.ShapeDtypeStruct((total_rows, HALF), U32),
        scratch_types=[pltpu.VMEM((2 * kd + kp, SC_ROWS), I32)]
        + [pltpu.VMEM((SC_ROWS, HALF), U32)] * (nb + 1)
        + [pltpu.SemaphoreType.DMA((nb,)), pltpu.SemaphoreType.DMA((nb,)), pltpu.SemaphoreType.DMA],
        name="moe_dispatch_sc",
    )
    def k(hp_hbm, idx_hbm, zero_hbm, xs_hbm, idx_v, *rest):
        bufs, zbuf = rest[:nb], rest[nb]
        load_sem, scat_sem, pad_sem = rest[nb + 1:]
        wid = _sc_worker_id()
        pltpu.sync_copy(idx_hbm.at[wid], idx_v)

        def load(c):
            return pltpu.make_async_copy(hp_hbm.at[pl.ds(wid * tpw + c * SC_ROWS, SC_ROWS)], bufs[c % nb],
                                         load_sem.at[c % nb])

        def scatters(c):
            return (pltpu.make_async_copy(bufs[c % nb], xs_hbm.at[idx_v.at[c]], scat_sem.at[c % nb]),
                    pltpu.make_async_copy(bufs[c % nb], xs_hbm.at[idx_v.at[kd + c]], scat_sem.at[c % nb]))

        for c in range(min(nb - 1, kd)):
            load(c).start()
        pltpu.sync_copy(zero_hbm, zbuf)
        pads = [pltpu.make_async_copy(zbuf, xs_hbm.at[idx_v.at[2 * kd + j]], pad_sem) for j in range(kp)]
        for p in pads:
            p.start()
        for c in range(kd):
            load(c).wait()
            for d in scatters(c):
                d.start()
            if c + nb - 1 < kd:
                if c >= 1:
                    for d in scatters(c - 1):
                        d.wait()
                load(c + nb - 1).start()
        for c in range(max(kd - nb, 0), kd):
            for d in scatters(c):
                d.wait()
        for p in pads:
            p.wait()

    return k(hp, idx, zero_rows)


def _sc_gather(ys, idx):
    kg = idx.shape[1]
    rows_per_worker = kg * SC_ROWS
    nb = SC_NBUF

    @functools.partial(
        pl.kernel, mesh=_sc_mesh(),
        out_type=jax.ShapeDtypeStruct((SC_WORKERS * rows_per_worker, HALF), U32),
        scratch_types=[pltpu.VMEM((kg, SC_ROWS), I32)] + [pltpu.VMEM((SC_ROWS, HALF), U32)] * nb
        + [pltpu.SemaphoreType.DMA((nb,)), pltpu.SemaphoreType.DMA((nb,))],
        name="moe_gather_sc",
    )
    def k(ys_hbm, idx_hbm, yg_hbm, idx_v, *rest):
        bufs = rest[:nb]
        gat_sem, out_sem = rest[nb:]
        wid = _sc_worker_id()
        pltpu.sync_copy(idx_hbm.at[wid], idx_v)

        def gather(c):
            return pltpu.make_async_copy(ys_hbm.at[idx_v.at[c]], bufs[c % nb], gat_sem.at[c % nb])

        def store(c):
            return pltpu.make_async_copy(bufs[c % nb],
                                         yg_hbm.at[pl.ds(wid * rows_per_worker + c * SC_ROWS, SC_ROWS)],
                                         out_sem.at[c % nb])

        for c in range(min(nb - 1, kg)):
            gather(c).start()
        for c in range(kg):
            gather(c).wait()
            store(c).start()
            if c + nb - 1 < kg:
                if c >= 1:
                    store(c - 1).wait()
                gather(c + nb - 1).start()
        for c in range(max(kg - nb, 0), kg):
            store(c).wait()

    return k(ys, idx)


def _expert_kernel(blk_e_ref, nused_ref, first_ref, slot_ref, nxt_ref, xs_ref, wg_hbm, wu_hbm, wd_hbm,
                   ys_ref, wg_buf, wu_buf, wd_buf, sems, *, layer):
    step = pl.program_id(0)
    tm = TM_EXP
    nused = nused_ref[0]

    def weight_copies(e, s):
        return (pltpu.make_async_copy(wg_hbm.at[layer, e], wg_buf.at[s], sems.at[s, 0]),
                pltpu.make_async_copy(wu_hbm.at[layer, e], wu_buf.at[s], sems.at[s, 1]),
                pltpu.make_async_copy(wd_hbm.at[layer, e], wd_buf.at[s], sems.at[s, 2]))

    def dma_control(j):
        i = step * EXP_SUB + j

        @pl.when(i < nused)
        def _():
            s = slot_ref[i]

            if j == 0:
                @pl.when(i == 0)
                def _():
                    for c in weight_copies(blk_e_ref[0], 0):
                        c.start()

            @pl.when(first_ref[i] == 1)
            def _():
                for c in weight_copies(blk_e_ref[i], s):
                    c.wait()

                @pl.when(nxt_ref[i] >= 0)
                def _():
                    for c in weight_copies(nxt_ref[i], lax.rem(s + 1, EXP_SLOTS)):
                        c.start()

    for j in range(EXP_SUB):
        dma_control(j)

    @pl.when(step * EXP_SUB < nused)
    def _():
        for j in range(EXP_SUB):
            i = step * EXP_SUB + j
            rows = slice(j * tm, (j + 1) * tm)
            s = slot_ref[i]
            lo, hi = _unpack_bf16_pairs(xs_ref[rows, :])
            xf = jnp.concatenate([lo, hi], axis=1)
            a = jnp.dot(xf, wg_buf[s], preferred_element_type=F32)
            b = jnp.dot(xf, wu_buf[s], preferred_element_type=F32)
            hm = _silu(a) * b
            y = jnp.dot(hm, wd_buf[s], preferred_element_type=F32)
            ys_ref[rows, :] = jnp.where(i < nused, _pack_bf16_pairs(y), jnp.uint32(0))

    @pl.when(step * EXP_SUB >= nused)
    def _():
        ys_ref[...] = jnp.zeros_like(ys_ref)


def _experts(xs, blk_e, nused, w_gate, w_up, w_down, layer):
    tm = TM_EXP
    p_rows = xs.shape[0] - MOE_EXPERTS * tm
    nblk = p_rows // tm
    pos = jnp.arange(nblk, dtype=I32)
    valid = pos < nused[0]
    prev_e = jnp.concatenate([jnp.full((1,), -1, I32), blk_e[:-1]])
    first = valid & (blk_e != prev_e)
    before = pos[None, :] <= pos[:, None]
    slot = jnp.maximum(jnp.sum(jnp.where(before & first[None, :], 1, 0), axis=1) - 1, 0) % EXP_SLOTS
    first_pos = jnp.where(first, pos, nblk)
    next_first = jnp.min(jnp.where(pos[None, :] > pos[:, None], first_pos[None, :], nblk), axis=1)
    nxt = jnp.where(next_first < nblk, blk_e[jnp.minimum(next_first, nblk - 1)], -1)

    def blk(i, be, nu, *_):
        return jnp.minimum(i, (nu[0] - 1) // EXP_SUB)

    grid_spec = pltpu.PrefetchScalarGridSpec(
        num_scalar_prefetch=5,
        grid=(nblk // EXP_SUB,),
        in_specs=[
            pl.BlockSpec((EXP_SUB * tm, HALF), lambda i, *sp: (blk(i, *sp), 0)),
            pl.BlockSpec(memory_space=pl.ANY),
            pl.BlockSpec(memory_space=pl.ANY),
            pl.BlockSpec(memory_space=pl.ANY),
        ],
        out_specs=pl.BlockSpec((EXP_SUB * tm, HALF), lambda i, *sp: (i, 0)),
        scratch_shapes=[
            pltpu.VMEM((EXP_SLOTS, D_MODEL, MOE_FF), F32),
            pltpu.VMEM((EXP_SLOTS, D_MODEL, MOE_FF), F32),
            pltpu.VMEM((EXP_SLOTS, MOE_FF, D_MODEL), F32),
            pltpu.SemaphoreType.DMA((EXP_SLOTS, 3)),
        ],
    )
    return pl.pallas_call(
        functools.partial(_expert_kernel, layer=layer),
        grid_spec=grid_spec,
        out_shape=jax.ShapeDtypeStruct((p_rows, HALF), U32),
        compiler_params=_cparams(("arbitrary",)),
        name="moe_experts",
    )(blk_e, nused, first.astype(I32), slot.astype(I32), nxt.astype(I32), xs, w_gate, w_up, w_down)


def _combine_kernel(x_ref, rw_ref, fg_ref, y1_ref, y2_ref, out_ref):
    lo1, hi1 = _unpack_bf16_pairs(y1_ref[...])
    lo2, hi2 = _unpack_bf16_pairs(y2_ref[...])
    w1, w2 = _route_weight_columns(rw_ref[...])
    x = x_ref[...]
    o_lo = x[:, :HALF] + w1 * lo1 + w2 * lo2
    o_hi = x[:, HALF:] + w1 * hi1 + w2 * hi2
    ms = (jnp.sum(o_lo * o_lo, axis=-1, keepdims=True)
          + jnp.sum(o_hi * o_hi, axis=-1, keepdims=True)) * (1.0 / D_MODEL)
    sc = lax.rsqrt(ms + NORM_EPS)
    o_lo = o_lo * sc * fg_ref[:, :HALF]
    o_hi = o_hi * sc * fg_ref[:, HALF:]
    out_ref[:, :HALF] = o_lo
    out_ref[:, HALF:] = o_hi


def _combine(yg, x, rw, final_g):
    n = x.shape[0]
    td = T_COMB
    nb = n // td
    return pl.pallas_call(
        _combine_kernel,
        grid=(nb,),
        in_specs=[
            pl.BlockSpec((td, D_MODEL), lambda i: (i, 0)),
            pl.BlockSpec((8, td), lambda i: (0, i)),
            pl.BlockSpec((1, D_MODEL), lambda i: (0, 0)),
            pl.BlockSpec((td, HALF), lambda i: (i, 0)),
            pl.BlockSpec((td, HALF), lambda i: (i + nb, 0)),
        ],
        out_specs=pl.BlockSpec((td, D_MODEL), lambda i: (i, 0)),
        out_shape=jax.ShapeDtypeStruct((n, D_MODEL), F32),
        compiler_params=_cparams(("arbitrary",)),
        name="moe_combine",
    )(x, rw, final_g, yg, yg)


def _moe(x, g, w_rg, b_rg, w_re, b_re, w_gate, w_up, w_down, layer):
    n = x.shape[0]
    tm = TM_EXP
    p_rows = 2 * n + MOE_EXPERTS * tm
    nblk = p_rows // tm
    hp, ri, rw, cnt = _router(x, g, w_rg, b_rg, w_re, b_re)
    counts = cnt[:, 0].astype(I32)
    pcounts = (counts + tm - 1) // tm * tm
    eio = jnp.arange(MOE_EXPERTS, dtype=I32)[:, None]
    pend = jnp.sum(jnp.where(eio.T <= eio, pcounts[None, :], 0), axis=1)
    pstart = pend - pcounts
    dest1 = jnp.sum(jnp.where(ri[0][None, :] == eio, pstart[:, None], 0), axis=0) + ri[2]
    dest2 = jnp.sum(jnp.where(ri[1][None, :] == eio, pstart[:, None], 0), axis=0) + ri[3]
    blk_start = jnp.arange(nblk, dtype=I32) * tm
    blk_e = jnp.minimum(jnp.sum((pend[None, :] <= blk_start[:, None]).astype(I32), axis=1), MOE_EXPERTS - 1)
    nused = jnp.maximum(pend[-1] // tm, 1).astype(I32).reshape(1)
    r = jnp.arange(tm, dtype=I32)[None, :]
    pad_slot = jnp.where(r < (pcounts - counts)[:, None], (pstart + counts)[:, None] + r, p_rows + eio * tm + r)
    kd = n // SC_WORKERS // SC_ROWS
    idx = jnp.concatenate([dest1.reshape(SC_WORKERS, kd, SC_ROWS), dest2.reshape(SC_WORKERS, kd, SC_ROWS),
                           pad_slot.reshape(SC_WORKERS, -1, SC_ROWS)], axis=1)
    zero_rows = jnp.zeros((SC_ROWS, HALF), U32)
    xs = _sc_dispatch(hp, idx, zero_rows, p_rows + MOE_EXPERTS * tm)
    ys = _experts(xs, blk_e, nused, w_gate, w_up, w_down, layer)
    gidx = jnp.concatenate([dest1, dest2]).reshape(SC_WORKERS, -1, SC_ROWS)
    yg = _sc_gather(ys, gidx)
    return yg, rw


def kernel(x, positions, norm_mix_g, norm_ffn_g, ret_w_in, ret_head_g, ret_w_out, conv_w_pw1, conv_b_pw1, conv_w_dw, conv_b_dw, conv_ln_g, conv_ln_b, conv_w_pw2, conv_b_pw2, moe_w_rg, moe_b_rg, moe_w_re, moe_b_re, moe_w_gate, moe_w_up, moe_w_down, final_norm_g):
    b, s, d = x.shape
    n = b * s
    xt = x.reshape(n, d)
    pos = positions.reshape(1, n)
    fg = final_norm_g.reshape(1, d)

    q, k, v, gate = _ret_inproj(xt, pos, norm_mix_g[0].reshape(1, d), ret_w_in[0])
    xt = _ret_core(q, k, v, gate, xt, ret_head_g[0].reshape(1, RET_V), ret_w_out[0])
    yg, rw = _moe(xt, norm_ffn_g[0].reshape(1, d), moe_w_rg[0], moe_b_rg[0], moe_w_re[0], moe_b_re[0],
                  moe_w_gate, moe_w_up, moe_w_down, 0)

    xt, u = _conv_pw1(xt, rw, yg, norm_mix_g[1].reshape(1, d), conv_w_pw1[0],
                      conv_b_pw1[0].reshape(1, 2 * d))
    xt = _conv_core(u, xt, conv_w_dw[0], conv_b_dw[0].reshape(1, d), conv_ln_g[0].reshape(1, d),
                    conv_ln_b[0].reshape(1, d), conv_w_pw2[0], conv_b_pw2[0].reshape(1, d))
    yg, rw = _moe(xt, norm_ffn_g[1].reshape(1, d), moe_w_rg[1], moe_b_rg[1], moe_w_re[1], moe_b_re[1],
                  moe_w_gate, moe_w_up, moe_w_down, 1)
    xt = _combine(yg, xt, rw, fg)
    return xt.reshape(b, s, d)
```

```python
import functools

import jax
import jax.numpy as jnp
import numpy as np
from jax import lax
from jax.experimental import pallas as pl
from jax.experimental.pallas import tpu as pltpu
from jax.experimental.pallas import tpu_sc as plsc

F32 = jnp.float32
BF16 = jnp.bfloat16
U32 = jnp.uint32
I32 = jnp.int32

D_MODEL = 1024
RET_HEADS = 4
RET_DK = 256
RET_DV = 512
RET_QK = RET_HEADS * RET_DK
RET_V = RET_HEADS * RET_DV
ROPE_BASE = 10000.0
CONV_WIDTH = 31
MOE_GROUPS = 4
MOE_EPG = 8
MOE_EXPERTS = MOE_GROUPS * MOE_EPG
MOE_FF = 512
NORM_EPS = 1e-6

TM_PROJ = 512
RET_C = 256
RET_STEP = 512
TM_CONV = 1024
TM_PW1 = 1024
PW1_ROWS = 256
CONV_HALO = 32
CONV_ROWS = 128
CONV_STRIDE = 4
T_ROUTE = 1024
TM_EXP = 256
EXP_SUB = 4
EXP_SLOTS = EXP_SUB + 1
T_COMB = 1024
SC_CORES = 2
SC_SUBCORES = 16
SC_WORKERS = SC_CORES * SC_SUBCORES
SC_ROWS = 32
SC_NBUF = 4
HALF = D_MODEL // 2

VMEM_LIMIT = 56 * 1024 * 1024


def _cparams(sem, flags=None):
    return pltpu.CompilerParams(dimension_semantics=sem, vmem_limit_bytes=VMEM_LIMIT, flags=flags)


def _rms(x, g):
    ms = jnp.mean(x * x, axis=-1, keepdims=True)
    return x * lax.rsqrt(ms + NORM_EPS) * g


def _silu(x):
    return x * (1.0 / (1.0 + jnp.exp(-x)))


def _pack_bf16_pairs(y):
    lo = pltpu.bitcast(y[:, :HALF].astype(BF16).astype(F32), U32)
    hi = pltpu.bitcast(y[:, HALF:].astype(BF16).astype(F32), U32)
    return (hi & jnp.uint32(0xFFFF0000)) | (lo >> 16)


def _route_weight_columns(rw):
    t = jnp.concatenate([rw] * 16, axis=0).T
    return t[:, 0:1], t[:, 1:2]


def _unpack_bf16_pairs(p):
    lo = pltpu.bitcast(p << 16, F32)
    hi = pltpu.bitcast(p & jnp.uint32(0xFFFF0000), F32)
    return lo, hi


def _ret_inproj_kernel(x_ref, pos_ref, g_ref, inv_ref, w_ref, q_ref, k_ref, v_ref, gate_ref):
    half = RET_DK // 2
    kscale = RET_DK ** -0.5
    h = _rms(x_ref[...], g_ref[...]).astype(BF16)

    def proj(c0, width):
        return jnp.dot(h, w_ref[:, c0:c0 + width].astype(BF16), preferred_element_type=F32)

    v0 = proj(2 * RET_QK, 512)
    v_ref[:, 0:512] = v0.astype(BF16)
    zero = ((pltpu.bitcast(v0[:, 0:half], U32) >> 16) >> 16).astype(F32)
    pos = jnp.broadcast_to(pos_ref[...].astype(F32), (half, TM_PROJ)).T
    ang = pos * inv_ref[...] + zero
    cos = jnp.cos(ang)
    sin = jnp.sin(ang)
    for j in range(RET_V // 512):
        if j > 0:
            v_ref[:, j * 512:(j + 1) * 512] = proj(2 * RET_QK + j * 512, 512).astype(BF16)
        gate_ref[:, j * 512:(j + 1) * 512] = proj(2 * RET_QK + RET_V + j * 512, 512).astype(BF16)

    for hd in range(RET_HEADS):
        for base, out_ref, cs, sn in ((0, q_ref, cos, sin), (RET_QK, k_ref, cos * kscale, sin * kscale)):
            t = proj(base + hd * RET_DK, RET_DK)
            t1 = t[:, :half]
            t2 = t[:, half:]
            out_ref[:, hd * RET_DK:hd * RET_DK + half] = (t1 * cs - t2 * sn).astype(BF16)
            out_ref[:, hd * RET_DK + half:(hd + 1) * RET_DK] = (t1 * sn + t2 * cs).astype(BF16)


def _ret_inproj(x, pos, g, w_in):
    n = x.shape[0]
    half = RET_DK // 2
    inv = (ROPE_BASE ** (-jnp.arange(half, dtype=F32) / half)).reshape(1, half)
    tm = TM_PROJ
    return pl.pallas_call(
        _ret_inproj_kernel,
        grid=(n // tm,),
        in_specs=[
            pl.BlockSpec((tm, D_MODEL), lambda i: (i, 0)),
            pl.BlockSpec((1, tm), lambda i: (0, i)),
            pl.BlockSpec((1, D_MODEL), lambda i: (0, 0)),
            pl.BlockSpec((1, half), lambda i: (0, 0)),
            pl.BlockSpec(w_in.shape, lambda i: (0, 0), pipeline_mode=pl.Buffered(1)),
        ],
        out_specs=[
            pl.BlockSpec((tm, RET_QK), lambda i: (i, 0)),
            pl.BlockSpec((tm, RET_QK), lambda i: (i, 0)),
            pl.BlockSpec((tm, RET_V), lambda i: (i, 0)),
            pl.BlockSpec((tm, RET_V), lambda i: (i, 0)),
        ],
        out_shape=[
            jax.ShapeDtypeStruct((n, RET_QK), BF16),
            jax.ShapeDtypeStruct((n, RET_QK), BF16),
            jax.ShapeDtypeStruct((n, RET_V), BF16),
            jax.ShapeDtypeStruct((n, RET_V), BF16),
        ],
        compiler_params=_cparams(("arbitrary",)),
        name="ret_inproj",
    )(x, pos, g, inv, w_in)


def _ret_core_kernel(cdec_ref, q_ref, k_ref, v_ref, gate_ref, x_ref, hg_ref, intra_ref, cross_ref,
                     kdec_ref, wo_ref, out_ref, state_ref, y_ref, wos_ref):
    @pl.when(pl.program_id(0) == 0)
    def _():
        state_ref[...] = jnp.zeros_like(state_ref)
        hg = jnp.broadcast_to(hg_ref[...], (128, RET_V)).T[:, 0:1]
        wos_ref[...] = (wo_ref[...] * hg).astype(BF16)

    for r0 in range(0, RET_STEP, RET_C):
        rs = slice(r0, r0 + RET_C)
        for hd in range(RET_HEADS):
            q = q_ref[rs, hd * RET_DK:(hd + 1) * RET_DK]
            k = k_ref[rs, hd * RET_DK:(hd + 1) * RET_DK]
            v = v_ref[rs, hd * RET_DV:(hd + 1) * RET_DV]
            state = state_ref[hd]
            scores = lax.dot_general(q, k, (((1,), (1,)), ((), ())), preferred_element_type=F32)
            scores = (scores * intra_ref[hd]).astype(BF16)
            o = jnp.dot(scores, v, preferred_element_type=F32)
            cross = cross_ref[hd]
            o_cross = jnp.dot(q, state.astype(BF16), preferred_element_type=F32)
            o = o + o_cross * jnp.concatenate([cross] * (RET_DV // 128), axis=1)
            kdec = kdec_ref[hd]
            kd = (k.astype(F32) * jnp.concatenate([kdec] * (RET_DK // 128), axis=1)).astype(BF16)
            upd = lax.dot_general(kd, v, (((0,), (0,)), ((), ())), preferred_element_type=F32)
            state_ref[hd] = state * cdec_ref[hd] + upd
            ms = jnp.mean(o * o, axis=-1, keepdims=True)
            on = o * lax.rsqrt(ms + NORM_EPS)
            gt = gate_ref[rs, hd * RET_DV:(hd + 1) * RET_DV].astype(F32)
            y_ref[rs, hd * RET_DV:(hd + 1) * RET_DV] = (_silu(gt) * on).astype(BF16)
        out_ref[rs, :] = x_ref[rs, :] + jnp.dot(y_ref[rs, :], wos_ref[...], preferred_element_type=F32)


def _ret_core(q, k, v, gate, x, head_g_row, w_out):
    n = x.shape[0]
    c = RET_C
    log_gamma = jnp.log1p(-(2.0 ** (-5.0 - jnp.arange(RET_HEADS, dtype=F32))))
    idx = jnp.arange(c, dtype=F32)
    diff = idx[:, None] - idx[None, :]
    intra = jnp.where(diff >= 0, jnp.exp(log_gamma[:, None, None] * jnp.maximum(diff, 0.0)), 0.0)
    cross = jnp.broadcast_to(jnp.exp(log_gamma[:, None] * (idx + 1.0))[:, :, None], (RET_HEADS, c, 128))
    kdec = jnp.broadcast_to(jnp.exp(log_gamma[:, None] * (c - 1.0 - idx))[:, :, None], (RET_HEADS, c, 128))
    cdec = jnp.exp(log_gamma * c)
    return pl.pallas_call(
        _ret_core_kernel,
        grid=(n // RET_STEP,),
        in_specs=[
            pl.BlockSpec(memory_space=pltpu.SMEM),
            pl.BlockSpec((RET_STEP, RET_QK), lambda i: (i, 0)),
            pl.BlockSpec((RET_STEP, RET_QK), lambda i: (i, 0)),
            pl.BlockSpec((RET_STEP, RET_V), lambda i: (i, 0)),
            pl.BlockSpec((RET_STEP, RET_V), lambda i: (i, 0)),
            pl.BlockSpec((RET_STEP, D_MODEL), lambda i: (i, 0)),
            pl.BlockSpec((1, RET_V), lambda i: (0, 0)),
            pl.BlockSpec((RET_HEADS, c, c), lambda i: (0, 0, 0)),
            pl.BlockSpec((RET_HEADS, c, 128), lambda i: (0, 0, 0)),
            pl.BlockSpec((RET_HEADS, c, 128), lambda i: (0, 0, 0)),
            pl.BlockSpec((RET_V, D_MODEL), lambda i: (0, 0)),
        ],
        out_specs=pl.BlockSpec((RET_STEP, D_MODEL), lambda i: (i, 0)),
        out_shape=jax.ShapeDtypeStruct((n, D_MODEL), F32),
        scratch_shapes=[
            pltpu.VMEM((RET_HEADS, RET_DK, RET_DV), F32),
            pltpu.VMEM((RET_STEP, RET_V), BF16),
            pltpu.VMEM((RET_V, D_MODEL), BF16),
        ],
        compiler_params=_cparams(("arbitrary",)),
        name="ret_core",
    )(cdec, q, k, v, gate, x, head_g_row, intra, cross, kdec, w_out)


def _conv_pw1_kernel(x_ref, rw_ref, y1_ref, y2_ref, g_ref, w_ref, b_ref, xo_ref, u_ref):
    r = PW1_ROWS
    ngroups = TM_PW1 // r
    w1, w2 = _route_weight_columns(rw_ref[...])

    def prologue(rs):
        lo1, hi1 = _unpack_bf16_pairs(y1_ref[rs, :])
        lo2, hi2 = _unpack_bf16_pairs(y2_ref[rs, :])
        x = jnp.concatenate([x_ref[rs, :HALF] + w1[rs] * lo1 + w2[rs] * lo2,
                             x_ref[rs, HALF:] + w1[rs] * hi1 + w2[rs] * hi2], axis=1)
        xo_ref[rs, :] = x
        hf = _rms(x, g_ref[...])
        return hf.astype(BF16), hf[:, 0:128]

    groups = [slice(gi * r, (gi + 1) * r) for gi in range(ngroups)]
    hs = [prologue(rs) for rs in groups]
    for j in range(D_MODEL // 512):
        wa = w_ref[:, j * 512:(j + 1) * 512].astype(BF16)
        wg = w_ref[:, D_MODEL + j * 512:D_MODEL + (j + 1) * 512].astype(BF16)
        for gi, rs in enumerate(groups):
            h = hs[gi][0]
            a = jnp.dot(h, wa, preferred_element_type=F32) + b_ref[:, j * 512:(j + 1) * 512]
            gt = jnp.dot(h, wg, preferred_element_type=F32) + b_ref[:, D_MODEL + j * 512:D_MODEL + (j + 1) * 512]
            u = a * (1.0 / (1.0 + jnp.exp(-gt)))
            if j == 0 and gi + 1 < ngroups:
                zero = ((pltpu.bitcast(hs[gi + 1][1], U32) >> 16) >> 16).astype(F32)
                u_ref[rs, 0:128] = u[:, 0:128] + zero
                u_ref[rs, 128:512] = u[:, 128:512]
            else:
                u_ref[rs, j * 512:(j + 1) * 512] = u


def _conv_pw1(x, rw, yg, g, w, b):
    n = x.shape[0]
    tm = TM_PW1
    nb = n // tm
    return pl.pallas_call(
        _conv_pw1_kernel,
        grid=(nb,),
        in_specs=[
            pl.BlockSpec((tm, D_MODEL), lambda i: (i, 0)),
            pl.BlockSpec((8, tm), lambda i: (0, i)),
            pl.BlockSpec((tm, HALF), lambda i: (i, 0)),
            pl.BlockSpec((tm, HALF), lambda i: (i + nb, 0)),
            pl.BlockSpec((1, D_MODEL), lambda i: (0, 0)),
            pl.BlockSpec((D_MODEL, 2 * D_MODEL), lambda i: (0, 0), pipeline_mode=pl.Buffered(1)),
            pl.BlockSpec((1, 2 * D_MODEL), lambda i: (0, 0)),
        ],
        out_specs=[pl.BlockSpec((tm, D_MODEL), lambda i: (i, 0)), pl.BlockSpec((tm, D_MODEL), lambda i: (i, 0))],
        out_shape=[jax.ShapeDtypeStruct((n, D_MODEL), F32), jax.ShapeDtypeStruct((n, D_MODEL), F32)],
        compiler_params=_cparams(("arbitrary",)),
        name="conv_pw1",
    )(x, rw, yg, yg, g, w, b)


def _conv_core_kernel(u_ref, halo_ref, x_ref, wdw_ref, bdw_ref, lng_ref, lnb_ref, w2_ref, b2_ref,
                      out_ref, win_ref, z_ref):
    tm = TM_CONV
    first = pl.program_id(0) == 0
    halo = halo_ref[...]
    halo = jnp.where(first, jnp.zeros_like(halo), halo)
    nslab = D_MODEL // 128
    for cc in range(nslab):
        cs = slice(cc * 128, (cc + 1) * 128)
        win_ref[cc, 0:CONV_HALO, :] = halo[:, cs]
        win_ref[cc, CONV_HALO:CONV_HALO + tm, :] = u_ref[:, cs]
    off = CONV_HALO - (CONV_WIDTH - 1)
    rb = CONV_ROWS
    st = CONV_STRIDE
    for cc in range(nslab):
        cs = slice(cc * 128, (cc + 1) * 128)
        for r0 in range(0, tm, rb):
            accs = [bdw_ref[:, cs]] * st
            for o in range(CONV_WIDTH):
                w_o = wdw_ref[o:o + 1, cs]
                for rho in range(st):
                    accs[rho] = accs[rho] + win_ref[cc, pl.ds(r0 + rho + off + o, rb // st, stride=st), :] * w_o
            for rho in range(st):
                z_ref[cc, pl.ds(r0 + rho, rb // st, stride=st), :] = accs[rho]
    z = jnp.concatenate([z_ref[cc] for cc in range(nslab)], axis=1)
    mu = jnp.mean(z, axis=-1, keepdims=True)
    zc = z - mu
    var = jnp.mean(zc * zc, axis=-1, keepdims=True)
    zn = zc * lax.rsqrt(var + NORM_EPS) * lng_ref[...] + lnb_ref[...]
    y = _silu(zn).astype(BF16)
    out_ref[...] = x_ref[...] + jnp.dot(y, w2_ref[...].astype(BF16), preferred_element_type=F32) + b2_ref[...]


def _conv_core(u, x, w_dw, b_dw, ln_g, ln_b, w2, b2):
    n = x.shape[0]
    tm = TM_CONV
    r = tm // CONV_HALO
    wdw_pad = jnp.zeros((32, D_MODEL), F32).at[:CONV_WIDTH].set(w_dw)
    return pl.pallas_call(
        _conv_core_kernel,
        grid=(n // tm,),
        in_specs=[
            pl.BlockSpec((tm, D_MODEL), lambda i: (i, 0)),
            pl.BlockSpec((CONV_HALO, D_MODEL), lambda i: (jnp.maximum(i * r - 1, 0), 0)),
            pl.BlockSpec((tm, D_MODEL), lambda i: (i, 0)),
            pl.BlockSpec((32, D_MODEL), lambda i: (0, 0)),
            pl.BlockSpec((1, D_MODEL), lambda i: (0, 0)),
            pl.BlockSpec((1, D_MODEL), lambda i: (0, 0)),
            pl.BlockSpec((1, D_MODEL), lambda i: (0, 0)),
            pl.BlockSpec((D_MODEL, D_MODEL), lambda i: (0, 0)),
            pl.BlockSpec((1, D_MODEL), lambda i: (0, 0)),
        ],
        out_specs=pl.BlockSpec((tm, D_MODEL), lambda i: (i, 0)),
        out_shape=jax.ShapeDtypeStruct((n, D_MODEL), F32),
        scratch_shapes=[pltpu.VMEM((D_MODEL // 128, CONV_HALO + tm, 128), F32),
                        pltpu.VMEM((D_MODEL // 128, tm, 128), F32)],
        compiler_params=_cparams(("arbitrary",)),
        name="conv_core",
    )(u, u, x, wdw_pad, b_dw, ln_g, ln_b, w2, b2)


def _router_kernel(x_ref, g_ref, wr_ref, br_ref, upper_ref, hp_ref, ri_ref, rw_ref, cnt_ref, carry_ref):
    t = T_ROUTE

    @pl.when(pl.program_id(0) == 0)
    def _():
        carry_ref[...] = jnp.zeros_like(carry_ref)

    h = _rms(x_ref[...], g_ref[...])
    hp_ref[...] = _pack_bf16_pairs(h)
    h_hi = h.astype(BF16)
    h_lo = (h - h_hi.astype(F32)).astype(BF16)
    w = wr_ref[...]
    w_hi = w.astype(BF16)
    w_lo = (w - w_hi.astype(F32)).astype(BF16)
    dn = (((1,), (1,)), ((), ()))
    p = lax.dot_general(jnp.concatenate([w_hi, w_lo], axis=0), h_hi, dn, preferred_element_type=F32)
    nr = wr_ref.shape[0]
    logits = p[0:nr] + p[nr:2 * nr] + lax.dot_general(w_hi, h_lo, dn, preferred_element_type=F32)
    logits = logits + br_ref[:, 0:1]

    best = logits[0:1]
    gi = jnp.zeros((1, t), I32)
    for j in range(1, MOE_GROUPS):
        r = logits[j:j + 1]
        up = r > best
        gi = jnp.where(up, j, gi)
        best = jnp.where(up, r, best)
    den = jnp.zeros((1, t), F32)
    for j in range(MOE_GROUPS):
        den = den + jnp.exp(logits[j:j + 1] - best)
    gate_g = 1.0 / den

    sel = logits[8:8 + MOE_EPG]
    for j in range(1, MOE_GROUPS):
        sel = jnp.where(gi == j, logits[8 + j * MOE_EPG:8 + (j + 1) * MOE_EPG], sel)

    m1 = sel[0:1]
    i1 = jnp.zeros((1, t), I32)
    for j in range(1, MOE_EPG):
        r = sel[j:j + 1]
        up = r > m1
        i1 = jnp.where(up, j, i1)
        m1 = jnp.where(up, r, m1)
    m2 = jnp.full((1, t), -jnp.inf, F32)
    i2 = jnp.zeros((1, t), I32)
    started = jnp.zeros((1, t), jnp.bool_)
    for j in range(MOE_EPG):
        r = sel[j:j + 1]
        ok = i1 != j
        up = ok & ((r > m2) | jnp.logical_not(started))
        i2 = jnp.where(up, j, i2)
        m2 = jnp.where(up, r, m2)
        started = started | ok
    e21 = jnp.exp(m2 - m1)
    p1 = 1.0 / (1.0 + e21)
    w1 = gate_g * p1
    w2 = gate_g * (e21 * p1)
    eid1 = gi * MOE_EPG + i1
    eid2 = gi * MOE_EPG + i2

    eio = lax.broadcasted_iota(I32, (MOE_EXPERTS, t), 0)
    oh1 = eio == eid1
    oh2 = eio == eid2
    oh = (oh1 | oh2).astype(F32)
    cum = jnp.dot(oh.astype(BF16), upper_ref[...], preferred_element_type=F32) + carry_ref[:, 0:1]
    rank1 = jnp.sum(jnp.where(oh1, cum, 0.0), axis=0, keepdims=True)
    rank2 = jnp.sum(jnp.where(oh2, cum, 0.0), axis=0, keepdims=True)
    carry_ref[...] = carry_ref[...] + jnp.sum(oh, axis=1, keepdims=True)
    cnt_ref[...] = carry_ref[...]

    zi = jnp.zeros((4, t), I32)
    ri_ref[...] = jnp.concatenate([eid1, eid2, rank1.astype(I32), rank2.astype(I32), zi], axis=0)
    zf = jnp.zeros((6, t), F32)
    rw_ref[...] = jnp.concatenate([w1, w2, zf], axis=0)


def _router(x, g, w_rg, b_rg, w_re, b_re):
    n = x.shape[0]
    t = T_ROUTE
    wr = jnp.zeros((40, D_MODEL), F32).at[0:MOE_GROUPS].set(w_rg.T).at[8:40].set(w_re.T)
    br = jnp.zeros((40,), F32).at[0:MOE_GROUPS].set(b_rg).at[8:40].set(b_re)
    br = jnp.broadcast_to(br[:, None], (40, 128))
    upper = jnp.asarray(np.triu(np.ones((t, t), np.float32), 1).astype(BF16))
    return pl.pallas_call(
        _router_kernel,
        grid=(n // t,),
        in_specs=[
            pl.BlockSpec((t, D_MODEL), lambda i: (i, 0)),
            pl.BlockSpec((1, D_MODEL), lambda i: (0, 0)),
            pl.BlockSpec((40, D_MODEL), lambda i: (0, 0)),
            pl.BlockSpec((40, 128), lambda i: (0, 0)),
            pl.BlockSpec((t, t), lambda i: (0, 0)),
        ],
        out_specs=[
            pl.BlockSpec((t, HALF), lambda i: (i, 0)),
            pl.BlockSpec((8, t), lambda i: (0, i)),
            pl.BlockSpec((8, t), lambda i: (0, i)),
            pl.BlockSpec((MOE_EXPERTS, 128), lambda i: (0, 0)),
        ],
        out_shape=[
            jax.ShapeDtypeStruct((n, HALF), U32),
            jax.ShapeDtypeStruct((8, n), I32),
            jax.ShapeDtypeStruct((8, n), F32),
            jax.ShapeDtypeStruct((MOE_EXPERTS, 128), F32),
        ],
        scratch_shapes=[pltpu.VMEM((MOE_EXPERTS, 128), F32)],
        compiler_params=_cparams(("arbitrary",)),
        name="moe_router",
    )(x, g, wr, br, upper)


def _sc_mesh():
    return plsc.VectorSubcoreMesh(core_axis_name="c", subcore_axis_name="s",
                                  num_cores=SC_CORES, num_subcores=SC_SUBCORES)


def _sc_worker_id():
    return lax.axis_index("s") * SC_CORES + lax.axis_index("c")


def _sc_dispatch(hp, idx, zero_rows, total_rows):
    n = hp.shape[0]
    tpw = n // SC_WORKERS
    kd = tpw // SC_ROWS
    kp = idx.shape[1] - 2 * kd
    nb = SC_NBUF

    @functools.partial(
        pl.kernel, mesh=_sc_mesh(),
        out_type=jax.ShapeDtypeStruct((total_rows, HALF), U32),
        scratch_types=[pltpu.VMEM((2 * kd + kp, SC_ROWS), I32)]
        + [pltpu.VMEM((SC_ROWS, HALF), U32)] * (nb + 1)
        + [pltpu.SemaphoreType.DMA((nb,)), pltpu.SemaphoreType.DMA((nb,)), pltpu.SemaphoreType.DMA],
        name="moe_dispatch_sc",
    )
    def k(hp_hbm, idx_hbm, zero_hbm, xs_hbm, idx_v, *rest):
        bufs, zbuf = rest[:nb], rest[nb]
        load_sem, scat_sem, pad_sem = rest[nb + 1:]
        wid = _sc_worker_id()
        pltpu.sync_copy(idx_hbm.at[wid], idx_v)

        def load(c):
            return pltpu.make_async_copy(hp_hbm.at[pl.ds(wid * tpw + c * SC_ROWS, SC_ROWS)], bufs[c % nb],
                                         load_sem.at[c % nb])

        def scatters(c):
            return (pltpu.make_async_copy(bufs[c % nb], xs_hbm.at[idx_v.at[c]], scat_sem.at[c % nb]),
                    pltpu.make_async_copy(bufs[c % nb], xs_hbm.at[idx_v.at[kd + c]], scat_sem.at[c % nb]))

        for c in range(min(nb - 1, kd)):
            load(c).start()
        pltpu.sync_copy(zero_hbm, zbuf)
        pads = [pltpu.make_async_copy(zbuf, xs_hbm.at[idx_v.at[2 * kd + j]], pad_sem) for j in range(kp)]
        for p in pads:
            p.start()
        for c in range(kd):
            load(c).wait()
            for d in scatters(c):
                d.start()
            if c + nb - 1 < kd:
                if c >= 1:
                    for d in scatters(c - 1):
                        d.wait()
                load(c + nb - 1).start()
        for c in range(max(kd - nb, 0), kd):
            for d in scatters(c):
                d.wait()
        for p in pads:
            p.wait()

    return k(hp, idx, zero_rows)


def _sc_gather(ys, idx):
    kg = idx.shape[1]
    rows_per_worker = kg * SC_ROWS
    nb = SC_NBUF

    @functools.partial(
        pl.kernel, mesh=_sc_mesh(),
        out_type=jax.ShapeDtypeStruct((SC_WORKERS * rows_per_worker, HALF), U32),
        scratch_types=[pltpu.VMEM((kg, SC_ROWS), I32)] + [pltpu.VMEM((SC_ROWS, HALF), U32)] * nb
        + [pltpu.SemaphoreType.DMA((nb,)), pltpu.SemaphoreType.DMA((nb,))],
        name="moe_gather_sc",
    )
    def k(ys_hbm, idx_hbm, yg_hbm, idx_v, *rest):
        bufs = rest[:nb]
        gat_sem, out_sem = rest[nb:]
        wid = _sc_worker_id()
        pltpu.sync_copy(idx_hbm.at[wid], idx_v)

        def gather(c):
            return pltpu.make_async_copy(ys_hbm.at[idx_v.at[c]], bufs[c % nb], gat_sem.at[c % nb])

        def store(c):
            return pltpu.make_async_copy(bufs[c % nb],
                                         yg_hbm.at[pl.ds(wid * rows_per_worker + c * SC_ROWS, SC_ROWS)],
                                         out_sem.at[c % nb])

        for c in range(min(nb - 1, kg)):
            gather(c).start()
        for c in range(kg):
            gather(c).wait()
            store(c).start()
            if c + nb - 1 < kg:
                if c >= 1:
                    store(c - 1).wait()
                gather(c + nb - 1).start()
        for c in range(max(kg - nb, 0), kg):
            store(c).wait()

    return k(ys, idx)


def _expert_kernel(blk_e_ref, nused_ref, first_ref, slot_ref, nxt_ref, xs_ref, wg_hbm, wu_hbm, wd_hbm,
                   ys_ref, wg_buf, wu_buf, wd_buf, sems, *, layer):
    step = pl.program_id(0)
    tm = TM_EXP
    nused = nused_ref[0]

    def weight_copies(e, s):
        return (pltpu.make_async_copy(wg_hbm.at[layer, e], wg_buf.at[s], sems.at[s, 0]),
                pltpu.make_async_copy(wu_hbm.at[layer, e], wu_buf.at[s], sems.at[s, 1]),
                pltpu.make_async_copy(wd_hbm.at[layer, e], wd_buf.at[s], sems.at[s, 2]))

    def dma_control(j):
        i = step * EXP_SUB + j

        @pl.when(i < nused)
        def _():
            s = slot_ref[i]

            if j == 0:
                @pl.when(i == 0)
                def _():
                    for c in weight_copies(blk_e_ref[0], 0):
                        c.start()

            @pl.when(first_ref[i] == 1)
            def _():
                for c in weight_copies(blk_e_ref[i], s):
                    c.wait()

                @pl.when(nxt_ref[i] >= 0)
                def _():
                    for c in weight_copies(nxt_ref[i], lax.rem(s + 1, EXP_SLOTS)):
                        c.start()

    for j in range(EXP_SUB):
        dma_control(j)

    @pl.when(step * EXP_SUB < nused)
    def _():
        for j in range(EXP_SUB):
            i = step * EXP_SUB + j
            rows = slice(j * tm, (j + 1) * tm)
            s = slot_ref[i]
            lo, hi = _unpack_bf16_pairs(xs_ref[rows, :])
            xf = jnp.concatenate([lo, hi], axis=1)
            a = jnp.dot(xf, wg_buf[s], preferred_element_type=F32)
            b = jnp.dot(xf, wu_buf[s], preferred_element_type=F32)
            hm = _silu(a) * b
            y = jnp.dot(hm, wd_buf[s], preferred_element_type=F32)
            ys_ref[rows, :] = jnp.where(i < nused, _pack_bf16_pairs(y), jnp.uint32(0))

    @pl.when(step * EXP_SUB >= nused)
    def _():
        ys_ref[...] = jnp.zeros_like(ys_ref)


def _experts(xs, blk_e, nused, w_gate, w_up, w_down, layer):
    tm = TM_EXP
    p_rows = xs.shape[0] - MOE_EXPERTS * tm
    nblk = p_rows // tm
    pos = jnp.arange(nblk, dtype=I32)
    valid = pos < nused[0]
    prev_e = jnp.concatenate([jnp.full((1,), -1, I32), blk_e[:-1]])
    first = valid & (blk_e != prev_e)
    slot = jnp.maximum(jnp.cumsum(first.astype(I32)) - 1, 0) % EXP_SLOTS
    first_pos = jnp.where(first, pos, nblk)
    next_first = jnp.concatenate([lax.cummin(first_pos, reverse=True)[1:], jnp.full((1,), nblk, I32)])
    nxt = jnp.where(next_first < nblk, blk_e[jnp.minimum(next_first, nblk - 1)], -1)

    def blk(i, be, nu, *_):
        return jnp.minimum(i, (nu[0] - 1) // EXP_SUB)

    grid_spec = pltpu.PrefetchScalarGridSpec(
        num_scalar_prefetch=5,
        grid=(nblk // EXP_SUB,),
        in_specs=[
            pl.BlockSpec((EXP_SUB * tm, HALF), lambda i, *sp: (blk(i, *sp), 0)),
            pl.BlockSpec(memory_space=pl.ANY),
            pl.BlockSpec(memory_space=pl.ANY),
            pl.BlockSpec(memory_space=pl.ANY),
        ],
        out_specs=pl.BlockSpec((EXP_SUB * tm, HALF), lambda i, *sp: (i, 0)),
        scratch_shapes=[
            pltpu.VMEM((EXP_SLOTS, D_MODEL, MOE_FF), F32),
            pltpu.VMEM((EXP_SLOTS, D_MODEL, MOE_FF), F32),
            pltpu.VMEM((EXP_SLOTS, MOE_FF, D_MODEL), F32),
            pltpu.SemaphoreType.DMA((EXP_SLOTS, 3)),
        ],
    )
    return pl.pallas_call(
        functools.partial(_expert_kernel, layer=layer),
        grid_spec=grid_spec,
        out_shape=jax.ShapeDtypeStruct((p_rows, HALF), U32),
        compiler_params=_cparams(("arbitrary",)),
        name="moe_experts",
    )(blk_e, nused, first.astype(I32), slot.astype(I32), nxt.astype(I32), xs, w_gate, w_up, w_down)


def _combine_kernel(x_ref, rw_ref, fg_ref, y1_ref, y2_ref, out_ref):
    lo1, hi1 = _unpack_bf16_pairs(y1_ref[...])
    lo2, hi2 = _unpack_bf16_pairs(y2_ref[...])
    w1, w2 = _route_weight_columns(rw_ref[...])
    x = x_ref[...]
    o_lo = x[:, :HALF] + w1 * lo1 + w2 * lo2
    o_hi = x[:, HALF:] + w1 * hi1 + w2 * hi2
    ms = (jnp.sum(o_lo * o_lo, axis=-1, keepdims=True)
          + jnp.sum(o_hi * o_hi, axis=-1, keepdims=True)) * (1.0 / D_MODEL)
    sc = lax.rsqrt(ms + NORM_EPS)
    o_lo = o_lo * sc * fg_ref[:, :HALF]
    o_hi = o_hi * sc * fg_ref[:, HALF:]
    out_ref[:, :HALF] = o_lo
    out_ref[:, HALF:] = o_hi


def _combine(yg, x, rw, final_g):
    n = x.shape[0]
    td = T_COMB
    nb = n // td
    return pl.pallas_call(
        _combine_kernel,
        grid=(nb,),
        in_specs=[
            pl.BlockSpec((td, D_MODEL), lambda i: (i, 0)),
            pl.BlockSpec((8, td), lambda i: (0, i)),
            pl.BlockSpec((1, D_MODEL), lambda i: (0, 0)),
            pl.BlockSpec((td, HALF), lambda i: (i, 0)),
            pl.BlockSpec((td, HALF), lambda i: (i + nb, 0)),
        ],
        out_specs=pl.BlockSpec((td, D_MODEL), lambda i: (i, 0)),
        out_shape=jax.ShapeDtypeStruct((n, D_MODEL), F32),
        compiler_params=_cparams(("arbitrary",)),
        name="moe_combine",
    )(x, rw, final_g, yg, yg)


def _moe(x, g, w_rg, b_rg, w_re, b_re, w_gate, w_up, w_down, layer):
    n = x.shape[0]
    tm = TM_EXP
    p_rows = 2 * n + MOE_EXPERTS * tm
    nblk = p_rows // tm
    hp, ri, rw, cnt = _router(x, g, w_rg, b_rg, w_re, b_re)
    counts = cnt[:, 0].astype(I32)
    pcounts = (counts + tm - 1) // tm * tm
    pend = jnp.cumsum(pcounts)
    pstart = pend - pcounts
    eio = jnp.arange(MOE_EXPERTS, dtype=I32)[:, None]
    dest1 = jnp.sum(jnp.where(ri[0][None, :] == eio, pstart[:, None], 0), axis=0) + ri[2]
    dest2 = jnp.sum(jnp.where(ri[1][None, :] == eio, pstart[:, None], 0), axis=0) + ri[3]
    blk_start = jnp.arange(nblk, dtype=I32) * tm
    blk_e = jnp.minimum(jnp.sum((pend[None, :] <= blk_start[:, None]).astype(I32), axis=1), MOE_EXPERTS - 1)
    nused = jnp.maximum(pend[-1] // tm, 1).astype(I32).reshape(1)
    r = jnp.arange(tm, dtype=I32)[None, :]
    pad_slot = jnp.where(r < (pcounts - counts)[:, None], (pstart + counts)[:, None] + r, p_rows + eio * tm + r)
    kd = n // SC_WORKERS // SC_ROWS
    idx = jnp.concatenate([dest1.reshape(SC_WORKERS, kd, SC_ROWS), dest2.reshape(SC_WORKERS, kd, SC_ROWS),
                           pad_slot.reshape(SC_WORKERS, -1, SC_ROWS)], axis=1)
    zero_rows = jnp.zeros((SC_ROWS, HALF), U32)
    xs = _sc_dispatch(hp, idx, zero_rows, p_rows + MOE_EXPERTS * tm)
    ys = _experts(xs, blk_e, nused, w_gate, w_up, w_down, layer)
    gidx = jnp.concatenate([dest1, dest2]).reshape(SC_WORKERS, -1, SC_ROWS)
    yg = _sc_gather(ys, gidx)
    return yg, rw


def kernel(x, positions, norm_mix_g, norm_ffn_g, ret_w_in, ret_head_g, ret_w_out, conv_w_pw1, conv_b_pw1, conv_w_dw, conv_b_dw, conv_ln_g, conv_ln_b, conv_w_pw2, conv_b_pw2, moe_w_rg, moe_b_rg, moe_w_re, moe_b_re, moe_w_gate, moe_w_up, moe_w_down, final_norm_g):
    b, s, d = x.shape
    n = b * s
    xt = x.reshape(n, d)
    pos = positions.reshape(1, n)
    fg = final_norm_g.reshape(1, d)

    q, k, v, gate = _ret_inproj(xt, pos, norm_mix_g[0].reshape(1, d), ret_w_in[0])
    xt = _ret_core(q, k, v, gate, xt, ret_head_g[0].reshape(1, RET_V), ret_w_out[0])
    yg, rw = _moe(xt, norm_ffn_g[0].reshape(1, d), moe_w_rg[0], moe_b_rg[0], moe_w_re[0], moe_b_re[0],
                  moe_w_gate, moe_w_up, moe_w_down, 0)

    xt, u = _conv_pw1(xt, rw, yg, norm_mix_g[1].reshape(1, d), conv_w_pw1[0],
                      conv_b_pw1[0].reshape(1, 2 * d))
    xt = _conv_core(u, xt, conv_w_dw[0], conv_b_dw[0].reshape(1, d), conv_ln_g[0].reshape(1, d),
                    conv_ln_b[0].reshape(1, d), conv_w_pw2[0], conv_b_pw2[0].reshape(1, d))
    yg, rw = _moe(xt, norm_ffn_g[1].reshape(1, d), moe_w_rg[1], moe_b_rg[1], moe_w_re[1], moe_b_re[1],
                  moe_w_gate, moe_w_up, moe_w_down, 1)
    xt = _combine(yg, xt, rw, fg)
    return xt.reshape(b, s, d)
```

```python
import functools

import jax
import jax.numpy as jnp
import numpy as np
from jax import lax
from jax.experimental import pallas as pl
from jax.experimental.pallas import tpu as pltpu
from jax.experimental.pallas import tpu_sc as plsc

F32 = jnp.float32
BF16 = jnp.bfloat16
U32 = jnp.uint32
I32 = jnp.int32

D_MODEL = 1024
RET_HEADS = 4
RET_DK = 256
RET_DV = 512
RET_QK = RET_HEADS * RET_DK
RET_V = RET_HEADS * RET_DV
ROPE_BASE = 10000.0
CONV_WIDTH = 31
MOE_GROUPS = 4
MOE_EPG = 8
MOE_EXPERTS = MOE_GROUPS * MOE_EPG
MOE_FF = 512
NORM_EPS = 1e-6

TM_PROJ = 512
RET_C = 256
RET_STEP = 512
TM_CONV = 1024
TM_PW1 = 1024
PW1_ROWS = 256
CONV_HALO = 32
CONV_ROWS = 128
CONV_STRIDE = 4
T_ROUTE = 1024
TM_EXP = 256
EXP_SUB = 4
EXP_SLOTS = EXP_SUB + 1
T_COMB = 2048
SC_CORES = 2
SC_SUBCORES = 16
SC_WORKERS = SC_CORES * SC_SUBCORES
SC_ROWS = 32
SC_NBUF = 4
HALF = D_MODEL // 2

VMEM_LIMIT = 56 * 1024 * 1024


def _cparams(sem, flags=None):
    return pltpu.CompilerParams(dimension_semantics=sem, vmem_limit_bytes=VMEM_LIMIT, flags=flags)


def _rms(x, g):
    ms = jnp.mean(x * x, axis=-1, keepdims=True)
    return x * lax.rsqrt(ms + NORM_EPS) * g


def _silu(x):
    return x * (1.0 / (1.0 + jnp.exp(-x)))


def _pack_bf16_pairs(y):
    lo = pltpu.bitcast(y[:, :HALF].astype(BF16).astype(F32), U32)
    hi = pltpu.bitcast(y[:, HALF:].astype(BF16).astype(F32), U32)
    return (hi & jnp.uint32(0xFFFF0000)) | (lo >> 16)


def _route_weight_columns(rw):
    t = jnp.concatenate([rw] * 16, axis=0).T
    return t[:, 0:1], t[:, 1:2]


def _unpack_bf16_pairs(p):
    lo = pltpu.bitcast(p << 16, F32)
    hi = pltpu.bitcast(p & jnp.uint32(0xFFFF0000), F32)
    return lo, hi


def _ret_inproj_kernel(x_ref, pos_ref, g_ref, inv_ref, w_ref, q_ref, k_ref, v_ref, gate_ref):
    half = RET_DK // 2
    kscale = RET_DK ** -0.5
    h = _rms(x_ref[...], g_ref[...]).astype(BF16)

    def proj(c0, width):
        return jnp.dot(h, w_ref[:, c0:c0 + width].astype(BF16), preferred_element_type=F32)

    v0 = proj(2 * RET_QK, 512)
    v_ref[:, 0:512] = v0.astype(BF16)
    zero = ((pltpu.bitcast(v0[:, 0:half], U32) >> 16) >> 16).astype(F32)
    pos = jnp.broadcast_to(pos_ref[...].astype(F32), (half, TM_PROJ)).T
    ang = pos * inv_ref[...] + zero
    cos = jnp.cos(ang)
    sin = jnp.sin(ang)
    for j in range(RET_V // 512):
        if j > 0:
            v_ref[:, j * 512:(j + 1) * 512] = proj(2 * RET_QK + j * 512, 512).astype(BF16)
        gate_ref[:, j * 512:(j + 1) * 512] = proj(2 * RET_QK + RET_V + j * 512, 512).astype(BF16)

    for hd in range(RET_HEADS):
        for base, out_ref, cs, sn in ((0, q_ref, cos, sin), (RET_QK, k_ref, cos * kscale, sin * kscale)):
            t = proj(base + hd * RET_DK, RET_DK)
            t1 = t[:, :half]
            t2 = t[:, half:]
            out_ref[:, hd * RET_DK:hd * RET_DK + half] = (t1 * cs - t2 * sn).astype(BF16)
            out_ref[:, hd * RET_DK + half:(hd + 1) * RET_DK] = (t1 * sn + t2 * cs).astype(BF16)


def _ret_inproj(x, pos, g, w_in):
    n = x.shape[0]
    half = RET_DK // 2
    inv = (ROPE_BASE ** (-jnp.arange(half, dtype=F32) / half)).reshape(1, half)
    tm = TM_PROJ
    return pl.pallas_call(
        _ret_inproj_kernel,
        grid=(n // tm,),
        in_specs=[
            pl.BlockSpec((tm, D_MODEL), lambda i: (i, 0)),
            pl.BlockSpec((1, tm), lambda i: (0, i)),
            pl.BlockSpec((1, D_MODEL), lambda i: (0, 0)),
            pl.BlockSpec((1, half), lambda i: (0, 0)),
            pl.BlockSpec(w_in.shape, lambda i: (0, 0), pipeline_mode=pl.Buffered(1)),
        ],
        out_specs=[
            pl.BlockSpec((tm, RET_QK), lambda i: (i, 0)),
            pl.BlockSpec((tm, RET_QK), lambda i: (i, 0)),
            pl.BlockSpec((tm, RET_V), lambda i: (i, 0)),
            pl.BlockSpec((tm, RET_V), lambda i: (i, 0)),
        ],
        out_shape=[
            jax.ShapeDtypeStruct((n, RET_QK), BF16),
            jax.ShapeDtypeStruct((n, RET_QK), BF16),
            jax.ShapeDtypeStruct((n, RET_V), BF16),
            jax.ShapeDtypeStruct((n, RET_V), BF16),
        ],
        compiler_params=_cparams(("arbitrary",)),
        name="ret_inproj",
    )(x, pos, g, inv, w_in)


def _ret_core_kernel(cdec_ref, q_ref, k_ref, v_ref, gate_ref, x_ref, hg_ref, intra_ref, cross_ref,
                     kdec_ref, wo_ref, out_ref, state_ref, y_ref, wos_ref):
    @pl.when(pl.program_id(0) == 0)
    def _():
        state_ref[...] = jnp.zeros_like(state_ref)
        hg = jnp.broadcast_to(hg_ref[...], (128, RET_V)).T[:, 0:1]
        wos_ref[...] = (wo_ref[...] * hg).astype(BF16)

    for r0 in range(0, RET_STEP, RET_C):
        rs = slice(r0, r0 + RET_C)
        for hd in range(RET_HEADS):
            q = q_ref[rs, hd * RET_DK:(hd + 1) * RET_DK]
            k = k_ref[rs, hd * RET_DK:(hd + 1) * RET_DK]
            v = v_ref[rs, hd * RET_DV:(hd + 1) * RET_DV]
            state = state_ref[hd]
            scores = lax.dot_general(q, k, (((1,), (1,)), ((), ())), preferred_element_type=F32)
            scores = (scores * intra_ref[hd]).astype(BF16)
            o = jnp.dot(scores, v, preferred_element_type=F32)
            cross = cross_ref[hd]
            o_cross = jnp.dot(q, state.astype(BF16), preferred_element_type=F32)
            o = o + o_cross * jnp.concatenate([cross] * (RET_DV // 128), axis=1)
            kdec = kdec_ref[hd]
            kd = (k.astype(F32) * jnp.concatenate([kdec] * (RET_DK // 128), axis=1)).astype(BF16)
            upd = lax.dot_general(kd, v, (((0,), (0,)), ((), ())), preferred_element_type=F32)
            state_ref[hd] = state * cdec_ref[hd] + upd
            ms = jnp.mean(o * o, axis=-1, keepdims=True)
            on = o * lax.rsqrt(ms + NORM_EPS)
            gt = gate_ref[rs, hd * RET_DV:(hd + 1) * RET_DV].astype(F32)
            y_ref[rs, hd * RET_DV:(hd + 1) * RET_DV] = (_silu(gt) * on).astype(BF16)
        out_ref[rs, :] = x_ref[rs, :] + jnp.dot(y_ref[rs, :], wos_ref[...], preferred_element_type=F32)


def _ret_core(q, k, v, gate, x, head_g_row, w_out):
    n = x.shape[0]
    c = RET_C
    log_gamma = jnp.log1p(-(2.0 ** (-5.0 - jnp.arange(RET_HEADS, dtype=F32))))
    idx = jnp.arange(c, dtype=F32)
    diff = idx[:, None] - idx[None, :]
    intra = jnp.where(diff >= 0, jnp.exp(log_gamma[:, None, None] * jnp.maximum(diff, 0.0)), 0.0)
    cross = jnp.broadcast_to(jnp.exp(log_gamma[:, None] * (idx + 1.0))[:, :, None], (RET_HEADS, c, 128))
    kdec = jnp.broadcast_to(jnp.exp(log_gamma[:, None] * (c - 1.0 - idx))[:, :, None], (RET_HEADS, c, 128))
    cdec = jnp.exp(log_gamma * c)
    return pl.pallas_call(
        _ret_core_kernel,
        grid=(n // RET_STEP,),
        in_specs=[
            pl.BlockSpec(memory_space=pltpu.SMEM),
            pl.BlockSpec((RET_STEP, RET_QK), lambda i: (i, 0)),
            pl.BlockSpec((RET_STEP, RET_QK), lambda i: (i, 0)),
            pl.BlockSpec((RET_STEP, RET_V), lambda i: (i, 0)),
            pl.BlockSpec((RET_STEP, RET_V), lambda i: (i, 0)),
            pl.BlockSpec((RET_STEP, D_MODEL), lambda i: (i, 0)),
            pl.BlockSpec((1, RET_V), lambda i: (0, 0)),
            pl.BlockSpec((RET_HEADS, c, c), lambda i: (0, 0, 0)),
            pl.BlockSpec((RET_HEADS, c, 128), lambda i: (0, 0, 0)),
            pl.BlockSpec((RET_HEADS, c, 128), lambda i: (0, 0, 0)),
            pl.BlockSpec((RET_V, D_MODEL), lambda i: (0, 0)),
        ],
        out_specs=pl.BlockSpec((RET_STEP, D_MODEL), lambda i: (i, 0)),
        out_shape=jax.ShapeDtypeStruct((n, D_MODEL), F32),
        scratch_shapes=[
            pltpu.VMEM((RET_HEADS, RET_DK, RET_DV), F32),
            pltpu.VMEM((RET_STEP, RET_V), BF16),
            pltpu.VMEM((RET_V, D_MODEL), BF16),
        ],
        compiler_params=_cparams(("arbitrary",)),
        name="ret_core",
    )(cdec, q, k, v, gate, x, head_g_row, intra, cross, kdec, w_out)


def _conv_pw1_kernel(x_ref, rw_ref, y1_ref, y2_ref, g_ref, w_ref, b_ref, xo_ref, u_ref):
    r = PW1_ROWS
    ngroups = TM_PW1 // r
    w1, w2 = _route_weight_columns(rw_ref[...])

    def prologue(rs):
        lo1, hi1 = _unpack_bf16_pairs(y1_ref[rs, :])
        lo2, hi2 = _unpack_bf16_pairs(y2_ref[rs, :])
        x = jnp.concatenate([x_ref[rs, :HALF] + w1[rs] * lo1 + w2[rs] * lo2,
                             x_ref[rs, HALF:] + w1[rs] * hi1 + w2[rs] * hi2], axis=1)
        xo_ref[rs, :] = x
        hf = _rms(x, g_ref[...])
        return hf.astype(BF16), hf[:, 0:128]

    groups = [slice(gi * r, (gi + 1) * r) for gi in range(ngroups)]
    hs = [prologue(rs) for rs in groups]
    for j in range(D_MODEL // 512):
        wa = w_ref[:, j * 512:(j + 1) * 512].astype(BF16)
        wg = w_ref[:, D_MODEL + j * 512:D_MODEL + (j + 1) * 512].astype(BF16)
        for gi, rs in enumerate(groups):
            h = hs[gi][0]
            a = jnp.dot(h, wa, preferred_element_type=F32) + b_ref[:, j * 512:(j + 1) * 512]
            gt = jnp.dot(h, wg, preferred_element_type=F32) + b_ref[:, D_MODEL + j * 512:D_MODEL + (j + 1) * 512]
            u = a * (1.0 / (1.0 + jnp.exp(-gt)))
            if j == 0 and gi + 1 < ngroups:
                zero = ((pltpu.bitcast(hs[gi + 1][1], U32) >> 16) >> 16).astype(F32)
                u_ref[rs, 0:128] = u[:, 0:128] + zero
                u_ref[rs, 128:512] = u[:, 128:512]
            else:
                u_ref[rs, j * 512:(j + 1) * 512] = u


def _conv_pw1(x, rw, yg, g, w, b):
    n = x.shape[0]
    tm = TM_PW1
    nb = n // tm
    return pl.pallas_call(
        _conv_pw1_kernel,
        grid=(nb,),
        in_specs=[
            pl.BlockSpec((tm, D_MODEL), lambda i: (i, 0)),
            pl.BlockSpec((8, tm), lambda i: (0, i)),
            pl.BlockSpec((tm, HALF), lambda i: (i, 0)),
            pl.BlockSpec((tm, HALF), lambda i: (i + nb, 0)),
            pl.BlockSpec((1, D_MODEL), lambda i: (0, 0)),
            pl.BlockSpec((D_MODEL, 2 * D_MODEL), lambda i: (0, 0), pipeline_mode=pl.Buffered(1)),
            pl.BlockSpec((1, 2 * D_MODEL), lambda i: (0, 0)),
        ],
        out_specs=[pl.BlockSpec((tm, D_MODEL), lambda i: (i, 0)), pl.BlockSpec((tm, D_MODEL), lambda i: (i, 0))],
        out_shape=[jax.ShapeDtypeStruct((n, D_MODEL), F32), jax.ShapeDtypeStruct((n, D_MODEL), F32)],
        compiler_params=_cparams(("arbitrary",)),
        name="conv_pw1",
    )(x, rw, yg, yg, g, w, b)


def _conv_core_kernel(u_ref, halo_ref, x_ref, wdw_ref, bdw_ref, lng_ref, lnb_ref, w2_ref, b2_ref,
                      out_ref, win_ref, z_ref):
    tm = TM_CONV
    first = pl.program_id(0) == 0
    halo = halo_ref[...]
    halo = jnp.where(first, jnp.zeros_like(halo), halo)
    nslab = D_MODEL // 128
    for cc in range(nslab):
        cs = slice(cc * 128, (cc + 1) * 128)
        win_ref[cc, 0:CONV_HALO, :] = halo[:, cs]
        win_ref[cc, CONV_HALO:CONV_HALO + tm, :] = u_ref[:, cs]
    off = CONV_HALO - (CONV_WIDTH - 1)
    rb = CONV_ROWS
    st = CONV_STRIDE
    for cc in range(nslab):
        cs = slice(cc * 128, (cc + 1) * 128)
        for r0 in range(0, tm, rb):
            accs = [bdw_ref[:, cs]] * st
            for o in range(CONV_WIDTH):
                w_o = wdw_ref[o:o + 1, cs]
                for rho in range(st):
                    accs[rho] = accs[rho] + win_ref[cc, pl.ds(r0 + rho + off + o, rb // st, stride=st), :] * w_o
            for rho in range(st):
                z_ref[cc, pl.ds(r0 + rho, rb // st, stride=st), :] = accs[rho]
    z = jnp.concatenate([z_ref[cc] for cc in range(nslab)], axis=1)
    mu = jnp.mean(z, axis=-1, keepdims=True)
    zc = z - mu
    var = jnp.mean(zc * zc, axis=-1, keepdims=True)
    zn = zc * lax.rsqrt(var + NORM_EPS) * lng_ref[...] + lnb_ref[...]
    y = _silu(zn).astype(BF16)
    out_ref[...] = x_ref[...] + jnp.dot(y, w2_ref[...].astype(BF16), preferred_element_type=F32) + b2_ref[...]


def _conv_core(u, x, w_dw, b_dw, ln_g, ln_b, w2, b2):
    n = x.shape[0]
    tm = TM_CONV
    r = tm // CONV_HALO
    wdw_pad = jnp.zeros((32, D_MODEL), F32).at[:CONV_WIDTH].set(w_dw)
    return pl.pallas_call(
        _conv_core_kernel,
        grid=(n // tm,),
        in_specs=[
            pl.BlockSpec((tm, D_MODEL), lambda i: (i, 0)),
            pl.BlockSpec((CONV_HALO, D_MODEL), lambda i: (jnp.maximum(i * r - 1, 0), 0)),
            pl.BlockSpec((tm, D_MODEL), lambda i: (i, 0)),
            pl.BlockSpec((32, D_MODEL), lambda i: (0, 0)),
            pl.BlockSpec((1, D_MODEL), lambda i: (0, 0)),
            pl.BlockSpec((1, D_MODEL), lambda i: (0, 0)),
            pl.BlockSpec((1, D_MODEL), lambda i: (0, 0)),
            pl.BlockSpec((D_MODEL, D_MODEL), lambda i: (0, 0)),
            pl.BlockSpec((1, D_MODEL), lambda i: (0, 0)),
        ],
        out_specs=pl.BlockSpec((tm, D_MODEL), lambda i: (i, 0)),
        out_shape=jax.ShapeDtypeStruct((n, D_MODEL), F32),
        scratch_shapes=[pltpu.VMEM((D_MODEL // 128, CONV_HALO + tm, 128), F32),
                        pltpu.VMEM((D_MODEL // 128, tm, 128), F32)],
        compiler_params=_cparams(("arbitrary",)),
        name="conv_core",
    )(u, u, x, wdw_pad, b_dw, ln_g, ln_b, w2, b2)


def _router_kernel(x_ref, g_ref, wr_ref, br_ref, upper_ref, hp_ref, ri_ref, rw_ref, cnt_ref, carry_ref):
    t = T_ROUTE

    @pl.when(pl.program_id(0) == 0)
    def _():
        carry_ref[...] = jnp.zeros_like(carry_ref)

    h = _rms(x_ref[...], g_ref[...])
    hp_ref[...] = _pack_bf16_pairs(h)
    h_hi = h.astype(BF16)
    h_lo = (h - h_hi.astype(F32)).astype(BF16)
    w = wr_ref[...]
    w_hi = w.astype(BF16)
    w_lo = (w - w_hi.astype(F32)).astype(BF16)
    dn = (((1,), (1,)), ((), ()))
    p = lax.dot_general(jnp.concatenate([w_hi, w_lo], axis=0), h_hi, dn, preferred_element_type=F32)
    nr = wr_ref.shape[0]
    logits = p[0:nr] + p[nr:2 * nr] + lax.dot_general(w_hi, h_lo, dn, preferred_element_type=F32)
    logits = logits + br_ref[:, 0:1]

    best = logits[0:1]
    gi = jnp.zeros((1, t), I32)
    for j in range(1, MOE_GROUPS):
        r = logits[j:j + 1]
        up = r > best
        gi = jnp.where(up, j, gi)
        best = jnp.where(up, r, best)
    den = jnp.zeros((1, t), F32)
    for j in range(MOE_GROUPS):
        den = den + jnp.exp(logits[j:j + 1] - best)
    gate_g = 1.0 / den

    sel = logits[8:8 + MOE_EPG]
    for j in range(1, MOE_GROUPS):
        sel = jnp.where(gi == j, logits[8 + j * MOE_EPG:8 + (j + 1) * MOE_EPG], sel)

    m1 = sel[0:1]
    i1 = jnp.zeros((1, t), I32)
    for j in range(1, MOE_EPG):
        r = sel[j:j + 1]
        up = r > m1
        i1 = jnp.where(up, j, i1)
        m1 = jnp.where(up, r, m1)
    m2 = jnp.full((1, t), -jnp.inf, F32)
    i2 = jnp.zeros((1, t), I32)
    started = jnp.zeros((1, t), jnp.bool_)
    for j in range(MOE_EPG):
        r = sel[j:j + 1]
        ok = i1 != j
        up = ok & ((r > m2) | jnp.logical_not(started))
        i2 = jnp.where(up, j, i2)
        m2 = jnp.where(up, r, m2)
        started = started | ok
    e21 = jnp.exp(m2 - m1)
    p1 = 1.0 / (1.0 + e21)
    w1 = gate_g * p1
    w2 = gate_g * (e21 * p1)
    eid1 = gi * MOE_EPG + i1
    eid2 = gi * MOE_EPG + i2

    eio = lax.broadcasted_iota(I32, (MOE_EXPERTS, t), 0)
    oh1 = eio == eid1
    oh2 = eio == eid2
    oh = (oh1 | oh2).astype(F32)
    cum = jnp.dot(oh.astype(BF16), upper_ref[...], preferred_element_type=F32) + carry_ref[:, 0:1]
    rank1 = jnp.sum(jnp.where(oh1, cum, 0.0), axis=0, keepdims=True)
    rank2 = jnp.sum(jnp.where(oh2, cum, 0.0), axis=0, keepdims=True)
    carry_ref[...] = carry_ref[...] + jnp.sum(oh, axis=1, keepdims=True)
    cnt_ref[...] = carry_ref[...]

    zi = jnp.zeros((4, t), I32)
    ri_ref[...] = jnp.concatenate([eid1, eid2, rank1.astype(I32), rank2.astype(I32), zi], axis=0)
    zf = jnp.zeros((6, t), F32)
    rw_ref[...] = jnp.concatenate([w1, w2, zf], axis=0)


def _router(x, g, w_rg, b_rg, w_re, b_re):
    n = x.shape[0]
    t = T_ROUTE
    wr = jnp.zeros((40, D_MODEL), F32).at[0:MOE_GROUPS].set(w_rg.T).at[8:40].set(w_re.T)
    br = jnp.zeros((40,), F32).at[0:MOE_GROUPS].set(b_rg).at[8:40].set(b_re)
    br = jnp.broadcast_to(br[:, None], (40, 128))
    upper = jnp.asarray(np.triu(np.ones((t, t), np.float32), 1).astype(BF16))
    return pl.pallas_call(
        _router_kernel,
        grid=(n // t,),
        in_specs=[
            pl.BlockSpec((t, D_MODEL), lambda i: (i, 0)),
            pl.BlockSpec((1, D_MODEL), lambda i: (0, 0)),
            pl.BlockSpec((40, D_MODEL), lambda i: (0, 0)),
            pl.BlockSpec((40, 128), lambda i: (0, 0)),
            pl.BlockSpec((t, t), lambda i: (0, 0)),
        ],
        out_specs=[
            pl.BlockSpec((t, HALF), lambda i: (i, 0)),
            pl.BlockSpec((8, t), lambda i: (0, i)),
            pl.BlockSpec((8, t), lambda i: (0, i)),
            pl.BlockSpec((MOE_EXPERTS, 128), lambda i: (0, 0)),
        ],
        out_shape=[
            jax.ShapeDtypeStruct((n, HALF), U32),
            jax.ShapeDtypeStruct((8, n), I32),
            jax.ShapeDtypeStruct((8, n), F32),
            jax.ShapeDtypeStruct((MOE_EXPERTS, 128), F32),
        ],
        scratch_shapes=[pltpu.VMEM((MOE_EXPERTS, 128), F32)],
        compiler_params=_cparams(("arbitrary",)),
        name="moe_router",
    )(x, g, wr, br, upper)


def _sc_mesh():
    return plsc.VectorSubcoreMesh(core_axis_name="c", subcore_axis_name="s",
                                  num_cores=SC_CORES, num_subcores=SC_SUBCORES)


def _sc_worker_id():
    return lax.axis_index("s") * SC_CORES + lax.axis_index("c")


def _sc_dispatch(hp, idx, zero_rows, total_rows):
    n = hp.shape[0]
    tpw = n // SC_WORKERS
    kd = tpw // SC_ROWS
    kp = idx.shape[1] - 2 * kd
    nb = SC_NBUF

    @functools.partial(
        pl.kernel, mesh=_sc_mesh(),
        out_type=jax.ShapeDtypeStruct((total_rows, HALF), U32),
        scratch_types=[pltpu.VMEM((2 * kd + kp, SC_ROWS), I32)]
        + [pltpu.VMEM((SC_ROWS, HALF), U32)] * (nb + 1)
        + [pltpu.SemaphoreType.DMA((nb,)), pltpu.SemaphoreType.DMA((nb,)), pltpu.SemaphoreType.DMA],
        name="moe_dispatch_sc",
    )
    def k(hp_hbm, idx_hbm, zero_hbm, xs_hbm, idx_v, *rest):
        bufs, zbuf = rest[:nb], rest[nb]
        load_sem, scat_sem, pad_sem = rest[nb + 1:]
        wid = _sc_worker_id()
        pltpu.sync_copy(idx_hbm.at[wid], idx_v)

        def load(c):
            return pltpu.make_async_copy(hp_hbm.at[pl.ds(wid * tpw + c * SC_ROWS, SC_ROWS)], bufs[c % nb],
                                         load_sem.at[c % nb])

        def scatters(c):
            return (pltpu.make_async_copy(bufs[c % nb], xs_hbm.at[idx_v.at[c]], scat_sem.at[c % nb]),
                    pltpu.make_async_copy(bufs[c % nb], xs_hbm.at[idx_v.at[kd + c]], scat_sem.at[c % nb]))

        for c in range(min(nb - 1, kd)):
            load(c).start()
        pltpu.sync_copy(zero_hbm, zbuf)
        pads = [pltpu.make_async_copy(zbuf, xs_hbm.at[idx_v.at[2 * kd + j]], pad_sem) for j in range(kp)]
        for p in pads:
            p.start()
        for c in range(kd):
            load(c).wait()
            for d in scatters(c):
                d.start()
            if c + nb - 1 < kd:
                if c >= 1:
                    for d in scatters(c - 1):
                        d.wait()
                load(c + nb - 1).start()
        for c in range(max(kd - nb, 0), kd):
            for d in scatters(c):
                d.wait()
        for p in pads:
            p.wait()

    return k(hp, idx, zero_rows)


def _sc_gather(ys, idx):
    kg = idx.shape[1]
    rows_per_worker = kg * SC_ROWS
    nb = SC_NBUF

    @functools.partial(
        pl.kernel, mesh=_sc_mesh(),
        out_type=jax.ShapeDtypeStruct((SC_WORKERS * rows_per_worker, HALF), U32),
        scratch_types=[pltpu.VMEM((kg, SC_ROWS), I32)] + [pltpu.VMEM((SC_ROWS, HALF), U32)] * nb
        + [pltpu.SemaphoreType.DMA((nb,)), pltpu.SemaphoreType.DMA((nb,))],
        name="moe_gather_sc",
    )
    def k(ys_hbm, idx_hbm, yg_hbm, idx_v, *rest):
        bufs = rest[:nb]
        gat_sem, out_sem = rest[nb:]
        wid = _sc_worker_id()
        pltpu.sync_copy(idx_hbm.at[wid], idx_v)

        def gather(c):
            return pltpu.make_async_copy(ys_hbm.at[idx_v.at[c]], bufs[c % nb], gat_sem.at[c % nb])

        def store(c):
            return pltpu.make_async_copy(bufs[c % nb],
                                         yg_hbm.at[pl.ds(wid * rows_per_worker + c * SC_ROWS, SC_ROWS)],
                                         out_sem.at[c % nb])

        for c in range(min(nb - 1, kg)):
            gather(c).start()
        for c in range(kg):
            gather(c).wait()
            store(c).start()
            if c + nb - 1 < kg:
                if c >= 1:
                    store(c - 1).wait()
                gather(c + nb - 1).start()
        for c in range(max(kg - nb, 0), kg):
            store(c).wait()

    return k(ys, idx)


def _expert_kernel(blk_e_ref, nused_ref, first_ref, slot_ref, nxt_ref, xs_ref, wg_hbm, wu_hbm, wd_hbm,
                   ys_ref, wg_buf, wu_buf, wd_buf, sems, *, layer):
    step = pl.program_id(0)
    tm = TM_EXP
    nused = nused_ref[0]

    def weight_copies(e, s):
        return (pltpu.make_async_copy(wg_hbm.at[layer, e], wg_buf.at[s], sems.at[s, 0]),
                pltpu.make_async_copy(wu_hbm.at[layer, e], wu_buf.at[s], sems.at[s, 1]),
                pltpu.make_async_copy(wd_hbm.at[layer, e], wd_buf.at[s], sems.at[s, 2]))

    def dma_control(j):
        i = step * EXP_SUB + j

        @pl.when(i < nused)
        def _():
            s = slot_ref[i]

            if j == 0:
                @pl.when(i == 0)
                def _():
                    for c in weight_copies(blk_e_ref[0], 0):
                        c.start()

            @pl.when(first_ref[i] == 1)
            def _():
                for c in weight_copies(blk_e_ref[i], s):
                    c.wait()

                @pl.when(nxt_ref[i] >= 0)
                def _():
                    for c in weight_copies(nxt_ref[i], lax.rem(s + 1, EXP_SLOTS)):
                        c.start()

    for j in range(EXP_SUB):
        dma_control(j)

    @pl.when(step * EXP_SUB < nused)
    def _():
        for j in range(EXP_SUB):
            i = step * EXP_SUB + j
            rows = slice(j * tm, (j + 1) * tm)
            s = slot_ref[i]
            lo, hi = _unpack_bf16_pairs(xs_ref[rows, :])
            xf = jnp.concatenate([lo, hi], axis=1)
            a = jnp.dot(xf, wg_buf[s], preferred_element_type=F32)
            b = jnp.dot(xf, wu_buf[s], preferred_element_type=F32)
            hm = _silu(a) * b
            y = jnp.dot(hm, wd_buf[s], preferred_element_type=F32)
            ys_ref[rows, :] = jnp.where(i < nused, _pack_bf16_pairs(y), jnp.uint32(0))

    @pl.when(step * EXP_SUB >= nused)
    def _():
        ys_ref[...] = jnp.zeros_like(ys_ref)


def _experts(xs, blk_e, nused, w_gate, w_up, w_down, layer):
    tm = TM_EXP
    p_rows = xs.shape[0] - MOE_EXPERTS * tm
    nblk = p_rows // tm
    pos = jnp.arange(nblk, dtype=I32)
    valid = pos < nused[0]
    prev_e = jnp.concatenate([jnp.full((1,), -1, I32), blk_e[:-1]])
    first = valid & (blk_e != prev_e)
    slot = jnp.maximum(jnp.cumsum(first.astype(I32)) - 1, 0) % EXP_SLOTS
    first_pos = jnp.where(first, pos, nblk)
    next_first = jnp.concatenate([lax.cummin(first_pos, reverse=True)[1:], jnp.full((1,), nblk, I32)])
    nxt = jnp.where(next_first < nblk, blk_e[jnp.minimum(next_first, nblk - 1)], -1)

    def blk(i, be, nu, *_):
        return jnp.minimum(i, (nu[0] - 1) // EXP_SUB)

    grid_spec = pltpu.PrefetchScalarGridSpec(
        num_scalar_prefetch=5,
        grid=(nblk // EXP_SUB,),
        in_specs=[
            pl.BlockSpec((EXP_SUB * tm, HALF), lambda i, *sp: (blk(i, *sp), 0)),
            pl.BlockSpec(memory_space=pl.ANY),
            pl.BlockSpec(memory_space=pl.ANY),
            pl.BlockSpec(memory_space=pl.ANY),
        ],
        out_specs=pl.BlockSpec((EXP_SUB * tm, HALF), lambda i, *sp: (i, 0)),
        scratch_shapes=[
            pltpu.VMEM((EXP_SLOTS, D_MODEL, MOE_FF), F32),
            pltpu.VMEM((EXP_SLOTS, D_MODEL, MOE_FF), F32),
            pltpu.VMEM((EXP_SLOTS, MOE_FF, D_MODEL), F32),
            pltpu.SemaphoreType.DMA((EXP_SLOTS, 3)),
        ],
    )
    return pl.pallas_call(
        functools.partial(_expert_kernel, layer=layer),
        grid_spec=grid_spec,
        out_shape=jax.ShapeDtypeStruct((p_rows, HALF), U32),
        compiler_params=_cparams(("arbitrary",)),
        name="moe_experts",
    )(blk_e, nused, first.astype(I32), slot.astype(I32), nxt.astype(I32), xs, w_gate, w_up, w_down)


def _combine_kernel(x_ref, rw_ref, fg_ref, y1_ref, y2_ref, out_ref):
    lo1, hi1 = _unpack_bf16_pairs(y1_ref[...])
    lo2, hi2 = _unpack_bf16_pairs(y2_ref[...])
    w1, w2 = _route_weight_columns(rw_ref[...])
    x = x_ref[...]
    o_lo = x[:, :HALF] + w1 * lo1 + w2 * lo2
    o_hi = x[:, HALF:] + w1 * hi1 + w2 * hi2
    ms = (jnp.sum(o_lo * o_lo, axis=-1, keepdims=True)
          + jnp.sum(o_hi * o_hi, axis=-1, keepdims=True)) * (1.0 / D_MODEL)
    sc = lax.rsqrt(ms + NORM_EPS)
    o_lo = o_lo * sc * fg_ref[:, :HALF]
    o_hi = o_hi * sc * fg_ref[:, HALF:]
    out_ref[:, :HALF] = o_lo
    out_ref[:, HALF:] = o_hi


def _combine(yg, x, rw, final_g):
    n = x.shape[0]
    td = T_COMB
    nb = n // td
    return pl.pallas_call(
        _combine_kernel,
        grid=(nb,),
        in_specs=[
            pl.BlockSpec((td, D_MODEL), lambda i: (i, 0)),
            pl.BlockSpec((8, td), lambda i: (0, i)),
            pl.BlockSpec((1, D_MODEL), lambda i: (0, 0)),
            pl.BlockSpec((td, HALF), lambda i: (i, 0)),
            pl.BlockSpec((td, HALF), lambda i: (i + nb, 0)),
        ],
        out_specs=pl.BlockSpec((td, D_MODEL), lambda i: (i, 0)),
        out_shape=jax.ShapeDtypeStruct((n, D_MODEL), F32),
        compiler_params=_cparams(("arbitrary",)),
        name="moe_combine",
    )(x, rw, final_g, yg, yg)


def _moe(x, g, w_rg, b_rg, w_re, b_re, w_gate, w_up, w_down, layer):
    n = x.shape[0]
    tm = TM_EXP
    p_rows = 2 * n + MOE_EXPERTS * tm
    nblk = p_rows // tm
    hp, ri, rw, cnt = _router(x, g, w_rg, b_rg, w_re, b_re)
    counts = cnt[:, 0].astype(I32)
    pcounts = (counts + tm - 1) // tm * tm
    pend = jnp.cumsum(pcounts)
    pstart = pend - pcounts
    eio = jnp.arange(MOE_EXPERTS, dtype=I32)[:, None]
    dest1 = jnp.sum(jnp.where(ri[0][None, :] == eio, pstart[:, None], 0), axis=0) + ri[2]
    dest2 = jnp.sum(jnp.where(ri[1][None, :] == eio, pstart[:, None], 0), axis=0) + ri[3]
    blk_start = jnp.arange(nblk, dtype=I32) * tm
    blk_e = jnp.minimum(jnp.sum((pend[None, :] <= blk_start[:, None]).astype(I32), axis=1), MOE_EXPERTS - 1)
    nused = jnp.maximum(pend[-1] // tm, 1).astype(I32).reshape(1)
    r = jnp.arange(tm, dtype=I32)[None, :]
    pad_slot = jnp.where(r < (pcounts - counts)[:, None], (pstart + counts)[:, None] + r, p_rows + eio * tm + r)
    kd = n // SC_WORKERS // SC_ROWS
    idx = jnp.concatenate([dest1.reshape(SC_WORKERS, kd, SC_ROWS), dest2.reshape(SC_WORKERS, kd, SC_ROWS),
                           pad_slot.reshape(SC_WORKERS, -1, SC_ROWS)], axis=1)
    zero_rows = jnp.zeros((SC_ROWS, HALF), U32)
    xs = _sc_dispatch(hp, idx, zero_rows, p_rows + MOE_EXPERTS * tm)
    ys = _experts(xs, blk_e, nused, w_gate, w_up, w_down, layer)
    gidx = jnp.concatenate([dest1, dest2]).reshape(SC_WORKERS, -1, SC_ROWS)
    yg = _sc_gather(ys, gidx)
    return yg, rw


def kernel(x, positions, norm_mix_g, norm_ffn_g, ret_w_in, ret_head_g, ret_w_out, conv_w_pw1, conv_b_pw1, conv_w_dw, conv_b_dw, conv_ln_g, conv_ln_b, conv_w_pw2, conv_b_pw2, moe_w_rg, moe_b_rg, moe_w_re, moe_b_re, moe_w_gate, moe_w_up, moe_w_down, final_norm_g):
    b, s, d = x.shape
    n = b * s
    xt = x.reshape(n, d)
    pos = positions.reshape(1, n)
    fg = final_norm_g.reshape(1, d)

    q, k, v, gate = _ret_inproj(xt, pos, norm_mix_g[0].reshape(1, d), ret_w_in[0])
    xt = _ret_core(q, k, v, gate, xt, ret_head_g[0].reshape(1, RET_V), ret_w_out[0])
    yg, rw = _moe(xt, norm_ffn_g[0].reshape(1, d), moe_w_rg[0], moe_b_rg[0], moe_w_re[0], moe_b_re[0],
                  moe_w_gate, moe_w_up, moe_w_down, 0)

    xt, u = _conv_pw1(xt, rw, yg, norm_mix_g[1].reshape(1, d), conv_w_pw1[0],
                      conv_b_pw1[0].reshape(1, 2 * d))
    xt = _conv_core(u, xt, conv_w_dw[0], conv_b_dw[0].reshape(1, d), conv_ln_g[0].reshape(1, d),
                    conv_ln_b[0].reshape(1, d), conv_w_pw2[0], conv_b_pw2[0].reshape(1, d))
    yg, rw = _moe(xt, norm_ffn_g[1].reshape(1, d), moe_w_rg[1], moe_b_rg[1], moe_w_re[1], moe_b_re[1],
                  moe_w_gate, moe_w_up, moe_w_down, 1)
    xt = _combine(yg, xt, rw, fg)
    return xt.reshape(b, s, d)
```

```python
import functools

import jax
import jax.numpy as jnp
import numpy as np
from jax import lax
from jax.experimental import pallas as pl
from jax.experimental.pallas import tpu as pltpu
from jax.experimental.pallas import tpu_sc as plsc

F32 = jnp.float32
BF16 = jnp.bfloat16
U32 = jnp.uint32
I32 = jnp.int32

D_MODEL = 1024
RET_HEADS = 4
RET_DK = 256
RET_DV = 512
RET_QK = RET_HEADS * RET_DK
RET_V = RET_HEADS * RET_DV
ROPE_BASE = 10000.0
CONV_WIDTH = 31
MOE_GROUPS = 4
MOE_EPG = 8
MOE_EXPERTS = MOE_GROUPS * MOE_EPG
MOE_FF = 512
NORM_EPS = 1e-6

TM_PROJ = 512
RET_C = 256
RET_STEP = 512
TM_CONV = 1024
TM_PW1 = 1024
PW1_ROWS = 256
CONV_HALO = 32
CONV_ROWS = 128
CONV_STRIDE = 4
T_ROUTE = 2048
TM_EXP = 256
EXP_SUB = 4
EXP_SLOTS = EXP_SUB + 1
T_COMB = 2048
SC_CORES = 2
SC_SUBCORES = 16
SC_WORKERS = SC_CORES * SC_SUBCORES
SC_ROWS = 32
SC_NBUF = 4
HALF = D_MODEL // 2

VMEM_LIMIT = 56 * 1024 * 1024


def _cparams(sem, flags=None):
    return pltpu.CompilerParams(dimension_semantics=sem, vmem_limit_bytes=VMEM_LIMIT, flags=flags)


def _rms(x, g):
    ms = jnp.mean(x * x, axis=-1, keepdims=True)
    return x * lax.rsqrt(ms + NORM_EPS) * g


def _silu(x):
    return x * (1.0 / (1.0 + jnp.exp(-x)))


def _pack_bf16_pairs(y):
    lo = pltpu.bitcast(y[:, :HALF].astype(BF16).astype(F32), U32)
    hi = pltpu.bitcast(y[:, HALF:].astype(BF16).astype(F32), U32)
    return (hi & jnp.uint32(0xFFFF0000)) | (lo >> 16)


def _route_weight_columns(rw):
    t = jnp.concatenate([rw] * 16, axis=0).T
    return t[:, 0:1], t[:, 1:2]


def _unpack_bf16_pairs(p):
    lo = pltpu.bitcast(p << 16, F32)
    hi = pltpu.bitcast(p & jnp.uint32(0xFFFF0000), F32)
    return lo, hi


def _ret_inproj_kernel(x_ref, pos_ref, g_ref, inv_ref, w_ref, q_ref, k_ref, v_ref, gate_ref):
    half = RET_DK // 2
    kscale = RET_DK ** -0.5
    h = _rms(x_ref[...], g_ref[...]).astype(BF16)

    def proj(c0, width):
        return jnp.dot(h, w_ref[:, c0:c0 + width].astype(BF16), preferred_element_type=F32)

    v0 = proj(2 * RET_QK, 512)
    v_ref[:, 0:512] = v0.astype(BF16)
    zero = ((pltpu.bitcast(v0[:, 0:half], U32) >> 16) >> 16).astype(F32)
    pos = jnp.broadcast_to(pos_ref[...].astype(F32), (half, TM_PROJ)).T
    ang = pos * inv_ref[...] + zero
    cos = jnp.cos(ang)
    sin = jnp.sin(ang)
    for j in range(RET_V // 512):
        if j > 0:
            v_ref[:, j * 512:(j + 1) * 512] = proj(2 * RET_QK + j * 512, 512).astype(BF16)
        gate_ref[:, j * 512:(j + 1) * 512] = proj(2 * RET_QK + RET_V + j * 512, 512).astype(BF16)

    for hd in range(RET_HEADS):
        for base, out_ref, cs, sn in ((0, q_ref, cos, sin), (RET_QK, k_ref, cos * kscale, sin * kscale)):
            t = proj(base + hd * RET_DK, RET_DK)
            t1 = t[:, :half]
            t2 = t[:, half:]
            out_ref[:, hd * RET_DK:hd * RET_DK + half] = (t1 * cs - t2 * sn).astype(BF16)
            out_ref[:, hd * RET_DK + half:(hd + 1) * RET_DK] = (t1 * sn + t2 * cs).astype(BF16)


def _ret_inproj(x, pos, g, w_in):
    n = x.shape[0]
    half = RET_DK // 2
    inv = (ROPE_BASE ** (-jnp.arange(half, dtype=F32) / half)).reshape(1, half)
    tm = TM_PROJ
    return pl.pallas_call(
        _ret_inproj_kernel,
        grid=(n // tm,),
        in_specs=[
            pl.BlockSpec((tm, D_MODEL), lambda i: (i, 0)),
            pl.BlockSpec((1, tm), lambda i: (0, i)),
            pl.BlockSpec((1, D_MODEL), lambda i: (0, 0)),
            pl.BlockSpec((1, half), lambda i: (0, 0)),
            pl.BlockSpec(w_in.shape, lambda i: (0, 0), pipeline_mode=pl.Buffered(1)),
        ],
        out_specs=[
            pl.BlockSpec((tm, RET_QK), lambda i: (i, 0)),
            pl.BlockSpec((tm, RET_QK), lambda i: (i, 0)),
            pl.BlockSpec((tm, RET_V), lambda i: (i, 0)),
            pl.BlockSpec((tm, RET_V), lambda i: (i, 0)),
        ],
        out_shape=[
            jax.ShapeDtypeStruct((n, RET_QK), BF16),
            jax.ShapeDtypeStruct((n, RET_QK), BF16),
            jax.ShapeDtypeStruct((n, RET_V), BF16),
            jax.ShapeDtypeStruct((n, RET_V), BF16),
        ],
        compiler_params=_cparams(("arbitrary",)),
        name="ret_inproj",
    )(x, pos, g, inv, w_in)


def _ret_core_kernel(cdec_ref, q_ref, k_ref, v_ref, gate_ref, x_ref, hg_ref, intra_ref, cross_ref,
                     kdec_ref, wo_ref, out_ref, state_ref, y_ref, wos_ref):
    @pl.when(pl.program_id(0) == 0)
    def _():
        state_ref[...] = jnp.zeros_like(state_ref)
        hg = jnp.broadcast_to(hg_ref[...], (128, RET_V)).T[:, 0:1]
        wos_ref[...] = (wo_ref[...] * hg).astype(BF16)

    for r0 in range(0, RET_STEP, RET_C):
        rs = slice(r0, r0 + RET_C)
        for hd in range(RET_HEADS):
            q = q_ref[rs, hd * RET_DK:(hd + 1) * RET_DK]
            k = k_ref[rs, hd * RET_DK:(hd + 1) * RET_DK]
            v = v_ref[rs, hd * RET_DV:(hd + 1) * RET_DV]
            state = state_ref[hd]
            scores = lax.dot_general(q, k, (((1,), (1,)), ((), ())), preferred_element_type=F32)
            scores = (scores * intra_ref[hd]).astype(BF16)
            o = jnp.dot(scores, v, preferred_element_type=F32)
            cross = cross_ref[hd]
            o_cross = jnp.dot(q, state.astype(BF16), preferred_element_type=F32)
            o = o + o_cross * jnp.concatenate([cross] * (RET_DV // 128), axis=1)
            kdec = kdec_ref[hd]
            kd = (k.astype(F32) * jnp.concatenate([kdec] * (RET_DK // 128), axis=1)).astype(BF16)
            upd = lax.dot_general(kd, v, (((0,), (0,)), ((), ())), preferred_element_type=F32)
            state_ref[hd] = state * cdec_ref[hd] + upd
            ms = jnp.mean(o * o, axis=-1, keepdims=True)
            on = o * lax.rsqrt(ms + NORM_EPS)
            gt = gate_ref[rs, hd * RET_DV:(hd + 1) * RET_DV].astype(F32)
            y_ref[rs, hd * RET_DV:(hd + 1) * RET_DV] = (_silu(gt) * on).astype(BF16)
        out_ref[rs, :] = x_ref[rs, :] + jnp.dot(y_ref[rs, :], wos_ref[...], preferred_element_type=F32)


def _ret_core(q, k, v, gate, x, head_g_row, w_out):
    n = x.shape[0]
    c = RET_C
    log_gamma = jnp.log1p(-(2.0 ** (-5.0 - jnp.arange(RET_HEADS, dtype=F32))))
    idx = jnp.arange(c, dtype=F32)
    diff = idx[:, None] - idx[None, :]
    intra = jnp.where(diff >= 0, jnp.exp(log_gamma[:, None, None] * jnp.maximum(diff, 0.0)), 0.0)
    cross = jnp.broadcast_to(jnp.exp(log_gamma[:, None] * (idx + 1.0))[:, :, None], (RET_HEADS, c, 128))
    kdec = jnp.broadcast_to(jnp.exp(log_gamma[:, None] * (c - 1.0 - idx))[:, :, None], (RET_HEADS, c, 128))
    cdec = jnp.exp(log_gamma * c)
    return pl.pallas_call(
        _ret_core_kernel,
        grid=(n // RET_STEP,),
        in_specs=[
            pl.BlockSpec(memory_space=pltpu.SMEM),
            pl.BlockSpec((RET_STEP, RET_QK), lambda i: (i, 0)),
            pl.BlockSpec((RET_STEP, RET_QK), lambda i: (i, 0)),
            pl.BlockSpec((RET_STEP, RET_V), lambda i: (i, 0)),
            pl.BlockSpec((RET_STEP, RET_V), lambda i: (i, 0)),
            pl.BlockSpec((RET_STEP, D_MODEL), lambda i: (i, 0)),
            pl.BlockSpec((1, RET_V), lambda i: (0, 0)),
            pl.BlockSpec((RET_HEADS, c, c), lambda i: (0, 0, 0)),
            pl.BlockSpec((RET_HEADS, c, 128), lambda i: (0, 0, 0)),
            pl.BlockSpec((RET_HEADS, c, 128), lambda i: (0, 0, 0)),
            pl.BlockSpec((RET_V, D_MODEL), lambda i: (0, 0)),
        ],
        out_specs=pl.BlockSpec((RET_STEP, D_MODEL), lambda i: (i, 0)),
        out_shape=jax.ShapeDtypeStruct((n, D_MODEL), F32),
        scratch_shapes=[
            pltpu.VMEM((RET_HEADS, RET_DK, RET_DV), F32),
            pltpu.VMEM((RET_STEP, RET_V), BF16),
            pltpu.VMEM((RET_V, D_MODEL), BF16),
        ],
        compiler_params=_cparams(("arbitrary",)),
        name="ret_core",
    )(cdec, q, k, v, gate, x, head_g_row, intra, cross, kdec, w_out)


def _conv_pw1_kernel(x_ref, rw_ref, y1_ref, y2_ref, g_ref, w_ref, b_ref, xo_ref, u_ref):
    r = PW1_ROWS
    ngroups = TM_PW1 // r
    w1, w2 = _route_weight_columns(rw_ref[...])

    def prologue(rs):
        lo1, hi1 = _unpack_bf16_pairs(y1_ref[rs, :])
        lo2, hi2 = _unpack_bf16_pairs(y2_ref[rs, :])
        x = jnp.concatenate([x_ref[rs, :HALF] + w1[rs] * lo1 + w2[rs] * lo2,
                             x_ref[rs, HALF:] + w1[rs] * hi1 + w2[rs] * hi2], axis=1)
        xo_ref[rs, :] = x
        hf = _rms(x, g_ref[...])
        return hf.astype(BF16), hf[:, 0:128]

    groups = [slice(gi * r, (gi + 1) * r) for gi in range(ngroups)]
    hs = [prologue(rs) for rs in groups]
    for j in range(D_MODEL // 512):
        wa = w_ref[:, j * 512:(j + 1) * 512].astype(BF16)
        wg = w_ref[:, D_MODEL + j * 512:D_MODEL + (j + 1) * 512].astype(BF16)
        for gi, rs in enumerate(groups):
            h = hs[gi][0]
            a = jnp.dot(h, wa, preferred_element_type=F32) + b_ref[:, j * 512:(j + 1) * 512]
            gt = jnp.dot(h, wg, preferred_element_type=F32) + b_ref[:, D_MODEL + j * 512:D_MODEL + (j + 1) * 512]
            u = a * (1.0 / (1.0 + jnp.exp(-gt)))
            if j == 0 and gi + 1 < ngroups:
                zero = ((pltpu.bitcast(hs[gi + 1][1], U32) >> 16) >> 16).astype(F32)
                u_ref[rs, 0:128] = u[:, 0:128] + zero
                u_ref[rs, 128:512] = u[:, 128:512]
            else:
                u_ref[rs, j * 512:(j + 1) * 512] = u


def _conv_pw1(x, rw, yg, g, w, b):
    n = x.shape[0]
    tm = TM_PW1
    nb = n // tm
    return pl.pallas_call(
        _conv_pw1_kernel,
        grid=(nb,),
        in_specs=[
            pl.BlockSpec((tm, D_MODEL), lambda i: (i, 0)),
            pl.BlockSpec((8, tm), lambda i: (0, i)),
            pl.BlockSpec((tm, HALF), lambda i: (i, 0)),
            pl.BlockSpec((tm, HALF), lambda i: (i + nb, 0)),
            pl.BlockSpec((1, D_MODEL), lambda i: (0, 0)),
            pl.BlockSpec((D_MODEL, 2 * D_MODEL), lambda i: (0, 0), pipeline_mode=pl.Buffered(1)),
            pl.BlockSpec((1, 2 * D_MODEL), lambda i: (0, 0)),
        ],
        out_specs=[pl.BlockSpec((tm, D_MODEL), lambda i: (i, 0)), pl.BlockSpec((tm, D_MODEL), lambda i: (i, 0))],
        out_shape=[jax.ShapeDtypeStruct((n, D_MODEL), F32), jax.ShapeDtypeStruct((n, D_MODEL), F32)],
        compiler_params=_cparams(("arbitrary",)),
        name="conv_pw1",
    )(x, rw, yg, yg, g, w, b)


def _conv_core_kernel(u_ref, halo_ref, x_ref, wdw_ref, bdw_ref, lng_ref, lnb_ref, w2_ref, b2_ref,
                      out_ref, win_ref, z_ref):
    tm = TM_CONV
    first = pl.program_id(0) == 0
    halo = halo_ref[...]
    halo = jnp.where(first, jnp.zeros_like(halo), halo)
    nslab = D_MODEL // 128
    for cc in range(nslab):
        cs = slice(cc * 128, (cc + 1) * 128)
        win_ref[cc, 0:CONV_HALO, :] = halo[:, cs]
        win_ref[cc, CONV_HALO:CONV_HALO + tm, :] = u_ref[:, cs]
    off = CONV_HALO - (CONV_WIDTH - 1)
    rb = CONV_ROWS
    st = CONV_STRIDE
    for cc in range(nslab):
        cs = slice(cc * 128, (cc + 1) * 128)
        for r0 in range(0, tm, rb):
            accs = [bdw_ref[:, cs]] * st
            for o in range(CONV_WIDTH):
                w_o = wdw_ref[o:o + 1, cs]
                for rho in range(st):
                    accs[rho] = accs[rho] + win_ref[cc, pl.ds(r0 + rho + off + o, rb // st, stride=st), :] * w_o
            for rho in range(st):
                z_ref[cc, pl.ds(r0 + rho, rb // st, stride=st), :] = accs[rho]
    z = jnp.concatenate([z_ref[cc] for cc in range(nslab)], axis=1)
    mu = jnp.mean(z, axis=-1, keepdims=True)
    zc = z - mu
    var = jnp.mean(zc * zc, axis=-1, keepdims=True)
    zn = zc * lax.rsqrt(var + NORM_EPS) * lng_ref[...] + lnb_ref[...]
    y = _silu(zn).astype(BF16)
    out_ref[...] = x_ref[...] + jnp.dot(y, w2_ref[...].astype(BF16), preferred_element_type=F32) + b2_ref[...]


def _conv_core(u, x, w_dw, b_dw, ln_g, ln_b, w2, b2):
    n = x.shape[0]
    tm = TM_CONV
    r = tm // CONV_HALO
    wdw_pad = jnp.zeros((32, D_MODEL), F32).at[:CONV_WIDTH].set(w_dw)
    return pl.pallas_call(
        _conv_core_kernel,
        grid=(n // tm,),
        in_specs=[
            pl.BlockSpec((tm, D_MODEL), lambda i: (i, 0)),
            pl.BlockSpec((CONV_HALO, D_MODEL), lambda i: (jnp.maximum(i * r - 1, 0), 0)),
            pl.BlockSpec((tm, D_MODEL), lambda i: (i, 0)),
            pl.BlockSpec((32, D_MODEL), lambda i: (0, 0)),
            pl.BlockSpec((1, D_MODEL), lambda i: (0, 0)),
            pl.BlockSpec((1, D_MODEL), lambda i: (0, 0)),
            pl.BlockSpec((1, D_MODEL), lambda i: (0, 0)),
            pl.BlockSpec((D_MODEL, D_MODEL), lambda i: (0, 0)),
            pl.BlockSpec((1, D_MODEL), lambda i: (0, 0)),
        ],
        out_specs=pl.BlockSpec((tm, D_MODEL), lambda i: (i, 0)),
        out_shape=jax.ShapeDtypeStruct((n, D_MODEL), F32),
        scratch_shapes=[pltpu.VMEM((D_MODEL // 128, CONV_HALO + tm, 128), F32),
                        pltpu.VMEM((D_MODEL // 128, tm, 128), F32)],
        compiler_params=_cparams(("arbitrary",)),
        name="conv_core",
    )(u, u, x, wdw_pad, b_dw, ln_g, ln_b, w2, b2)


def _router_kernel(x_ref, g_ref, wr_ref, br_ref, upper_ref, hp_ref, ri_ref, rw_ref, cnt_ref, carry_ref):
    t = T_ROUTE

    @pl.when(pl.program_id(0) == 0)
    def _():
        carry_ref[...] = jnp.zeros_like(carry_ref)

    h = _rms(x_ref[...], g_ref[...])
    hp_ref[...] = _pack_bf16_pairs(h)
    h_hi = h.astype(BF16)
    h_lo = (h - h_hi.astype(F32)).astype(BF16)
    w = wr_ref[...]
    w_hi = w.astype(BF16)
    w_lo = (w - w_hi.astype(F32)).astype(BF16)
    dn = (((1,), (1,)), ((), ()))
    p = lax.dot_general(jnp.concatenate([w_hi, w_lo], axis=0), h_hi, dn, preferred_element_type=F32)
    nr = wr_ref.shape[0]
    logits = p[0:nr] + p[nr:2 * nr] + lax.dot_general(w_hi, h_lo, dn, preferred_element_type=F32)
    logits = logits + br_ref[:, 0:1]

    best = logits[0:1]
    gi = jnp.zeros((1, t), I32)
    for j in range(1, MOE_GROUPS):
        r = logits[j:j + 1]
        up = r > best
        gi = jnp.where(up, j, gi)
        best = jnp.where(up, r, best)
    den = jnp.zeros((1, t), F32)
    for j in range(MOE_GROUPS):
        den = den + jnp.exp(logits[j:j + 1] - best)
    gate_g = 1.0 / den

    sel = logits[8:8 + MOE_EPG]
    for j in range(1, MOE_GROUPS):
        sel = jnp.where(gi == j, logits[8 + j * MOE_EPG:8 + (j + 1) * MOE_EPG], sel)

    m1 = sel[0:1]
    i1 = jnp.zeros((1, t), I32)
    for j in range(1, MOE_EPG):
        r = sel[j:j + 1]
        up = r > m1
        i1 = jnp.where(up, j, i1)
        m1 = jnp.where(up, r, m1)
    m2 = jnp.full((1, t), -jnp.inf, F32)
    i2 = jnp.zeros((1, t), I32)
    started = jnp.zeros((1, t), jnp.bool_)
    for j in range(MOE_EPG):
        r = sel[j:j + 1]
        ok = i1 != j
        up = ok & ((r > m2) | jnp.logical_not(started))
        i2 = jnp.where(up, j, i2)
        m2 = jnp.where(up, r, m2)
        started = started | ok
    e21 = jnp.exp(m2 - m1)
    p1 = 1.0 / (1.0 + e21)
    w1 = gate_g * p1
    w2 = gate_g * (e21 * p1)
    eid1 = gi * MOE_EPG + i1
    eid2 = gi * MOE_EPG + i2

    eio = lax.broadcasted_iota(I32, (MOE_EXPERTS, t), 0)
    oh1 = eio == eid1
    oh2 = eio == eid2
    oh = (oh1 | oh2).astype(F32)
    cum = jnp.dot(oh.astype(BF16), upper_ref[...], preferred_element_type=F32) + carry_ref[:, 0:1]
    rank1 = jnp.sum(jnp.where(oh1, cum, 0.0), axis=0, keepdims=True)
    rank2 = jnp.sum(jnp.where(oh2, cum, 0.0), axis=0, keepdims=True)
    carry_ref[...] = carry_ref[...] + jnp.sum(oh, axis=1, keepdims=True)
    cnt_ref[...] = carry_ref[...]

    zi = jnp.zeros((4, t), I32)
    ri_ref[...] = jnp.concatenate([eid1, eid2, rank1.astype(I32), rank2.astype(I32), zi], axis=0)
    zf = jnp.zeros((6, t), F32)
    rw_ref[...] = jnp.concatenate([w1, w2, zf], axis=0)


def _router(x, g, w_rg, b_rg, w_re, b_re):
    n = x.shape[0]
    t = T_ROUTE
    wr = jnp.zeros((40, D_MODEL), F32).at[0:MOE_GROUPS].set(w_rg.T).at[8:40].set(w_re.T)
    br = jnp.zeros((40,), F32).at[0:MOE_GROUPS].set(b_rg).at[8:40].set(b_re)
    br = jnp.broadcast_to(br[:, None], (40, 128))
    upper = jnp.asarray(np.triu(np.ones((t, t), np.float32), 1).astype(BF16))
    return pl.pallas_call(
        _router_kernel,
        grid=(n // t,),
        in_specs=[
            pl.BlockSpec((t, D_MODEL), lambda i: (i, 0)),
            pl.BlockSpec((1, D_MODEL), lambda i: (0, 0)),
            pl.BlockSpec((40, D_MODEL), lambda i: (0, 0)),
            pl.BlockSpec((40, 128), lambda i: (0, 0)),
            pl.BlockSpec((t, t), lambda i: (0, 0)),
        ],
        out_specs=[
            pl.BlockSpec((t, HALF), lambda i: (i, 0)),
            pl.BlockSpec((8, t), lambda i: (0, i)),
            pl.BlockSpec((8, t), lambda i: (0, i)),
            pl.BlockSpec((MOE_EXPERTS, 128), lambda i: (0, 0)),
        ],
        out_shape=[
            jax.ShapeDtypeStruct((n, HALF), U32),
            jax.ShapeDtypeStruct((8, n), I32),
            jax.ShapeDtypeStruct((8, n), F32),
            jax.ShapeDtypeStruct((MOE_EXPERTS, 128), F32),
        ],
        scratch_shapes=[pltpu.VMEM((MOE_EXPERTS, 128), F32)],
        compiler_params=_cparams(("arbitrary",)),
        name="moe_router",
    )(x, g, wr, br, upper)


def _sc_mesh():
    return plsc.VectorSubcoreMesh(core_axis_name="c", subcore_axis_name="s",
                                  num_cores=SC_CORES, num_subcores=SC_SUBCORES)


def _sc_worker_id():
    return lax.axis_index("s") * SC_CORES + lax.axis_index("c")


def _sc_dispatch(hp, idx, zero_rows, total_rows):
    n = hp.shape[0]
    tpw = n // SC_WORKERS
    kd = tpw // SC_ROWS
    kp = idx.shape[1] - 2 * kd
    nb = SC_NBUF

    @functools.partial(
        pl.kernel, mesh=_sc_mesh(),
        out_type=jax.ShapeDtypeStruct((total_rows, HALF), U32),
        scratch_types=[pltpu.VMEM((2 * kd + kp, SC_ROWS), I32)]
        + [pltpu.VMEM((SC_ROWS, HALF), U32)] * (nb + 1)
        + [pltpu.SemaphoreType.DMA((nb,)), pltpu.SemaphoreType.DMA((nb,)), pltpu.SemaphoreType.DMA],
        name="moe_dispatch_sc",
    )
    def k(hp_hbm, idx_hbm, zero_hbm, xs_hbm, idx_v, *rest):
        bufs, zbuf = rest[:nb], rest[nb]
        load_sem, scat_sem, pad_sem = rest[nb + 1:]
        wid = _sc_worker_id()
        pltpu.sync_copy(idx_hbm.at[wid], idx_v)

        def load(c):
            return pltpu.make_async_copy(hp_hbm.at[pl.ds(wid * tpw + c * SC_ROWS, SC_ROWS)], bufs[c % nb],
                                         load_sem.at[c % nb])

        def scatters(c):
            return (pltpu.make_async_copy(bufs[c % nb], xs_hbm.at[idx_v.at[c]], scat_sem.at[c % nb]),
                    pltpu.make_async_copy(bufs[c % nb], xs_hbm.at[idx_v.at[kd + c]], scat_sem.at[c % nb]))

        for c in range(min(nb - 1, kd)):
            load(c).start()
        pltpu.sync_copy(zero_hbm, zbuf)
        pads = [pltpu.make_async_copy(zbuf, xs_hbm.at[idx_v.at[2 * kd + j]], pad_sem) for j in range(kp)]
        for p in pads:
            p.start()
        for c in range(kd):
            load(c).wait()
            for d in scatters(c):
                d.start()
            if c + nb - 1 < kd:
                if c >= 1:
                    for d in scatters(c - 1):
                        d.wait()
                load(c + nb - 1).start()
        for c in range(max(kd - nb, 0), kd):
            for d in scatters(c):
                d.wait()
        for p in pads:
            p.wait()

    return k(hp, idx, zero_rows)


def _sc_gather(ys, idx):
    kg = idx.shape[1]
    rows_per_worker = kg * SC_ROWS
    nb = SC_NBUF

    @functools.partial(
        pl.kernel, mesh=_sc_mesh(),
        out_type=jax.ShapeDtypeStruct((SC_WORKERS * rows_per_worker, HALF), U32),
        scratch_types=[pltpu.VMEM((kg, SC_ROWS), I32)] + [pltpu.VMEM((SC_ROWS, HALF), U32)] * nb
        + [pltpu.SemaphoreType.DMA((nb,)), pltpu.SemaphoreType.DMA((nb,))],
        name="moe_gather_sc",
    )
    def k(ys_hbm, idx_hbm, yg_hbm, idx_v, *rest):
        bufs = rest[:nb]
        gat_sem, out_sem = rest[nb:]
        wid = _sc_worker_id()
        pltpu.sync_copy(idx_hbm.at[wid], idx_v)

        def gather(c):
            return pltpu.make_async_copy(ys_hbm.at[idx_v.at[c]], bufs[c % nb], gat_sem.at[c % nb])

        def store(c):
            return pltpu.make_async_copy(bufs[c % nb],
                                         yg_hbm.at[pl.ds(wid * rows_per_worker + c * SC_ROWS, SC_ROWS)],
                                         out_sem.at[c % nb])

        for c in range(min(nb - 1, kg)):
            gather(c).start()
        for c in range(kg):
            gather(c).wait()
            store(c).start()
            if c + nb - 1 < kg:
                if c >= 1:
                    store(c - 1).wait()
                gather(c + nb - 1).start()
        for c in range(max(kg - nb, 0), kg):
            store(c).wait()

    return k(ys, idx)


def _expert_kernel(blk_e_ref, nused_ref, first_ref, slot_ref, nxt_ref, xs_ref, wg_hbm, wu_hbm, wd_hbm,
                   ys_ref, wg_buf, wu_buf, wd_buf, sems, *, layer):
    step = pl.program_id(0)
    tm = TM_EXP
    nused = nused_ref[0]

    def weight_copies(e, s):
        return (pltpu.make_async_copy(wg_hbm.at[layer, e], wg_buf.at[s], sems.at[s, 0]),
                pltpu.make_async_copy(wu_hbm.at[layer, e], wu_buf.at[s], sems.at[s, 1]),
                pltpu.make_async_copy(wd_hbm.at[layer, e], wd_buf.at[s], sems.at[s, 2]))

    def dma_control(j):
        i = step * EXP_SUB + j

        @pl.when(i < nused)
        def _():
            s = slot_ref[i]

            if j == 0:
                @pl.when(i == 0)
                def _():
                    for c in weight_copies(blk_e_ref[0], 0):
                        c.start()

            @pl.when(first_ref[i] == 1)
            def _():
                for c in weight_copies(blk_e_ref[i], s):
                    c.wait()

                @pl.when(nxt_ref[i] >= 0)
                def _():
                    for c in weight_copies(nxt_ref[i], lax.rem(s + 1, EXP_SLOTS)):
                        c.start()

    for j in range(EXP_SUB):
        dma_control(j)

    @pl.when(step * EXP_SUB < nused)
    def _():
        for j in range(EXP_SUB):
            i = step * EXP_SUB + j
            rows = slice(j * tm, (j + 1) * tm)
            s = slot_ref[i]
            lo, hi = _unpack_bf16_pairs(xs_ref[rows, :])
            xf = jnp.concatenate([lo, hi], axis=1)
            a = jnp.dot(xf, wg_buf[s], preferred_element_type=F32)
            b = jnp.dot(xf, wu_buf[s], preferred_element_type=F32)
            hm = _silu(a) * b
            y = jnp.dot(hm, wd_buf[s], preferred_element_type=F32)
            ys_ref[rows, :] = jnp.where(i < nused, _pack_bf16_pairs(y), jnp.uint32(0))

    @pl.when(step * EXP_SUB >= nused)
    def _():
        ys_ref[...] = jnp.zeros_like(ys_ref)


def _experts(xs, blk_e, nused, w_gate, w_up, w_down, layer):
    tm = TM_EXP
    p_rows = xs.shape[0] - MOE_EXPERTS * tm
    nblk = p_rows // tm
    pos = jnp.arange(nblk, dtype=I32)
    valid = pos < nused[0]
    prev_e = jnp.concatenate([jnp.full((1,), -1, I32), blk_e[:-1]])
    first = valid & (blk_e != prev_e)
    slot = jnp.maximum(jnp.cumsum(first.astype(I32)) - 1, 0) % EXP_SLOTS
    first_pos = jnp.where(first, pos, nblk)
    next_first = jnp.concatenate([lax.cummin(first_pos, reverse=True)[1:], jnp.full((1,), nblk, I32)])
    nxt = jnp.where(next_first < nblk, blk_e[jnp.minimum(next_first, nblk - 1)], -1)

    def blk(i, be, nu, *_):
        return jnp.minimum(i, (nu[0] - 1) // EXP_SUB)

    grid_spec = pltpu.PrefetchScalarGridSpec(
        num_scalar_prefetch=5,
        grid=(nblk // EXP_SUB,),
        in_specs=[
            pl.BlockSpec((EXP_SUB * tm, HALF), lambda i, *sp: (blk(i, *sp), 0)),
            pl.BlockSpec(memory_space=pl.ANY),
            pl.BlockSpec(memory_space=pl.ANY),
            pl.BlockSpec(memory_space=pl.ANY),
        ],
        out_specs=pl.BlockSpec((EXP_SUB * tm, HALF), lambda i, *sp: (i, 0)),
        scratch_shapes=[
            pltpu.VMEM((EXP_SLOTS, D_MODEL, MOE_FF), F32),
            pltpu.VMEM((EXP_SLOTS, D_MODEL, MOE_FF), F32),
            pltpu.VMEM((EXP_SLOTS, MOE_FF, D_MODEL), F32),
            pltpu.SemaphoreType.DMA((EXP_SLOTS, 3)),
        ],
    )
    return pl.pallas_call(
        functools.partial(_expert_kernel, layer=layer),
        grid_spec=grid_spec,
        out_shape=jax.ShapeDtypeStruct((p_rows, HALF), U32),
        compiler_params=_cparams(("arbitrary",)),
        name="moe_experts",
    )(blk_e, nused, first.astype(I32), slot.astype(I32), nxt.astype(I32), xs, w_gate, w_up, w_down)


def _combine_kernel(x_ref, rw_ref, fg_ref, y1_ref, y2_ref, out_ref):
    lo1, hi1 = _unpack_bf16_pairs(y1_ref[...])
    lo2, hi2 = _unpack_bf16_pairs(y2_ref[...])
    w1, w2 = _route_weight_columns(rw_ref[...])
    x = x_ref[...]
    o_lo = x[:, :HALF] + w1 * lo1 + w2 * lo2
    o_hi = x[:, HALF:] + w1 * hi1 + w2 * hi2
    ms = (jnp.sum(o_lo * o_lo, axis=-1, keepdims=True)
          + jnp.sum(o_hi * o_hi, axis=-1, keepdims=True)) * (1.0 / D_MODEL)
    sc = lax.rsqrt(ms + NORM_EPS)
    o_lo = o_lo * sc * fg_ref[:, :HALF]
    o_hi = o_hi * sc * fg_ref[:, HALF:]
    out_ref[:, :HALF] = o_lo
    out_ref[:, HALF:] = o_hi


def _combine(yg, x, rw, final_g):
    n = x.shape[0]
    td = T_COMB
    nb = n // td
    return pl.pallas_call(
        _combine_kernel,
        grid=(nb,),
        in_specs=[
            pl.BlockSpec((td, D_MODEL), lambda i: (i, 0)),
            pl.BlockSpec((8, td), lambda i: (0, i)),
            pl.BlockSpec((1, D_MODEL), lambda i: (0, 0)),
            pl.BlockSpec((td, HALF), lambda i: (i, 0)),
            pl.BlockSpec((td, HALF), lambda i: (i + nb, 0)),
        ],
        out_specs=pl.BlockSpec((td, D_MODEL), lambda i: (i, 0)),
        out_shape=jax.ShapeDtypeStruct((n, D_MODEL), F32),
        compiler_params=_cparams(("arbitrary",)),
        name="moe_combine",
    )(x, rw, final_g, yg, yg)


def _moe(x, g, w_rg, b_rg, w_re, b_re, w_gate, w_up, w_down, layer):
    n = x.shape[0]
    tm = TM_EXP
    p_rows = 2 * n + MOE_EXPERTS * tm
    nblk = p_rows // tm
    hp, ri, rw, cnt = _router(x, g, w_rg, b_rg, w_re, b_re)
    counts = cnt[:, 0].astype(I32)
    pcounts = (counts + tm - 1) // tm * tm
    pend = jnp.cumsum(pcounts)
    pstart = pend - pcounts
    eio = jnp.arange(MOE_EXPERTS, dtype=I32)[:, None]
    dest1 = jnp.sum(jnp.where(ri[0][None, :] == eio, pstart[:, None], 0), axis=0) + ri[2]
    dest2 = jnp.sum(jnp.where(ri[1][None, :] == eio, pstart[:, None], 0), axis=0) + ri[3]
    blk_start = jnp.arange(nblk, dtype=I32) * tm
    blk_e = jnp.minimum(jnp.sum((pend[None, :] <= blk_start[:, None]).astype(I32), axis=1), MOE_EXPERTS - 1)
    nused = jnp.maximum(pend[-1] // tm, 1).astype(I32).reshape(1)
    r = jnp.arange(tm, dtype=I32)[None, :]
    pad_slot = jnp.where(r < (pcounts - counts)[:, None], (pstart + counts)[:, None] + r, p_rows + eio * tm + r)
    kd = n // SC_WORKERS // SC_ROWS
    idx = jnp.concatenate([dest1.reshape(SC_WORKERS, kd, SC_ROWS), dest2.reshape(SC_WORKERS, kd, SC_ROWS),
                           pad_slot.reshape(SC_WORKERS, -1, SC_ROWS)], axis=1)
    zero_rows = jnp.zeros((SC_ROWS, HALF), U32)
    xs = _sc_dispatch(hp, idx, zero_rows, p_rows + MOE_EXPERTS * tm)
    ys = _experts(xs, blk_e, nused, w_gate, w_up, w_down, layer)
    gidx = jnp.concatenate([dest1, dest2]).reshape(SC_WORKERS, -1, SC_ROWS)
    yg = _sc_gather(ys, gidx)
    return yg, rw


def kernel(x, positions, norm_mix_g, norm_ffn_g, ret_w_in, ret_head_g, ret_w_out, conv_w_pw1, conv_b_pw1, conv_w_dw, conv_b_dw, conv_ln_g, conv_ln_b, conv_w_pw2, conv_b_pw2, moe_w_rg, moe_b_rg, moe_w_re, moe_b_re, moe_w_gate, moe_w_up, moe_w_down, final_norm_g):
    b, s, d = x.shape
    n = b * s
    xt = x.reshape(n, d)
    pos = positions.reshape(1, n)
    fg = final_norm_g.reshape(1, d)

    q, k, v, gate = _ret_inproj(xt, pos, norm_mix_g[0].reshape(1, d), ret_w_in[0])
    xt = _ret_core(q, k, v, gate, xt, ret_head_g[0].reshape(1, RET_V), ret_w_out[0])
    yg, rw = _moe(xt, norm_ffn_g[0].reshape(1, d), moe_w_rg[0], moe_b_rg[0], moe_w_re[0], moe_b_re[0],
                  moe_w_gate, moe_w_up, moe_w_down, 0)

    xt, u = _conv_pw1(xt, rw, yg, norm_mix_g[1].reshape(1, d), conv_w_pw1[0],
                      conv_b_pw1[0].reshape(1, 2 * d))
    xt = _conv_core(u, xt, conv_w_dw[0], conv_b_dw[0].reshape(1, d), conv_ln_g[0].reshape(1, d),
                    conv_ln_b[0].reshape(1, d), conv_w_pw2[0], conv_b_pw2[0].reshape(1, d))
    yg, rw = _moe(xt, norm_ffn_g[1].reshape(1, d), moe_w_rg[1], moe_b_rg[1], moe_w_re[1], moe_b_re[1],
                  moe_w_gate, moe_w_up, moe_w_down, 1)
    xt = _combine(yg, xt, rw, fg)
    return xt.reshape(b, s, d)
```

```python
import functools

import jax
import jax.numpy as jnp
import numpy as np
from jax import lax
from jax.experimental import pallas as pl
from jax.experimental.pallas import tpu as pltpu
from jax.experimental.pallas import tpu_sc as plsc

F32 = jnp.float32
BF16 = jnp.bfloat16
U32 = jnp.uint32
I32 = jnp.int32

D_MODEL = 1024
RET_HEADS = 4
RET_DK = 256
RET_DV = 512
RET_QK = RET_HEADS * RET_DK
RET_V = RET_HEADS * RET_DV
ROPE_BASE = 10000.0
CONV_WIDTH = 31
MOE_GROUPS = 4
MOE_EPG = 8
MOE_EXPERTS = MOE_GROUPS * MOE_EPG
MOE_FF = 512
NORM_EPS = 1e-6

TM_PROJ = 512
RET_C = 256
RET_STEP = 512
TM_CONV = 1024
TM_PW1 = 1024
PW1_ROWS = 256
CONV_HALO = 32
CONV_ROWS = 128
CONV_STRIDE = 4
T_ROUTE = 1024
TM_EXP = 256
EXP_SUB = 4
EXP_SLOTS = EXP_SUB + 1
T_COMB = 2048
SC_CORES = 2
SC_SUBCORES = 16
SC_WORKERS = SC_CORES * SC_SUBCORES
SC_ROWS = 32
SC_NBUF = 6
HALF = D_MODEL // 2

VMEM_LIMIT = 56 * 1024 * 1024


def _cparams(sem, flags=None):
    return pltpu.CompilerParams(dimension_semantics=sem, vmem_limit_bytes=VMEM_LIMIT, flags=flags)


def _rms(x, g):
    ms = jnp.mean(x * x, axis=-1, keepdims=True)
    return x * lax.rsqrt(ms + NORM_EPS) * g


def _silu(x):
    return x * (1.0 / (1.0 + jnp.exp(-x)))


def _pack_bf16_pairs(y):
    lo = pltpu.bitcast(y[:, :HALF].astype(BF16).astype(F32), U32)
    hi = pltpu.bitcast(y[:, HALF:].astype(BF16).astype(F32), U32)
    return (hi & jnp.uint32(0xFFFF0000)) | (lo >> 16)


def _route_weight_columns(rw):
    t = jnp.concatenate([rw] * 16, axis=0).T
    return t[:, 0:1], t[:, 1:2]


def _unpack_bf16_pairs(p):
    lo = pltpu.bitcast(p << 16, F32)
    hi = pltpu.bitcast(p & jnp.uint32(0xFFFF0000), F32)
    return lo, hi


def _ret_inproj_kernel(x_ref, pos_ref, g_ref, inv_ref, w_ref, q_ref, k_ref, v_ref, gate_ref):
    half = RET_DK // 2
    kscale = RET_DK ** -0.5
    h = _rms(x_ref[...], g_ref[...]).astype(BF16)

    def proj(c0, width):
        return jnp.dot(h, w_ref[:, c0:c0 + width].astype(BF16), preferred_element_type=F32)

    v0 = proj(2 * RET_QK, 512)
    v_ref[:, 0:512] = v0.astype(BF16)
    zero = ((pltpu.bitcast(v0[:, 0:half], U32) >> 16) >> 16).astype(F32)
    pos = jnp.broadcast_to(pos_ref[...].astype(F32), (half, TM_PROJ)).T
    ang = pos * inv_ref[...] + zero
    cos = jnp.cos(ang)
    sin = jnp.sin(ang)
    for j in range(RET_V // 512):
        if j > 0:
            v_ref[:, j * 512:(j + 1) * 512] = proj(2 * RET_QK + j * 512, 512).astype(BF16)
        gate_ref[:, j * 512:(j + 1) * 512] = proj(2 * RET_QK + RET_V + j * 512, 512).astype(BF16)

    for hd in range(RET_HEADS):
        for base, out_ref, cs, sn in ((0, q_ref, cos, sin), (RET_QK, k_ref, cos * kscale, sin * kscale)):
            t = proj(base + hd * RET_DK, RET_DK)
            t1 = t[:, :half]
            t2 = t[:, half:]
            out_ref[:, hd * RET_DK:hd * RET_DK + half] = (t1 * cs - t2 * sn).astype(BF16)
            out_ref[:, hd * RET_DK + half:(hd + 1) * RET_DK] = (t1 * sn + t2 * cs).astype(BF16)


def _ret_inproj(x, pos, g, w_in):
    n = x.shape[0]
    half = RET_DK // 2
    inv = (ROPE_BASE ** (-jnp.arange(half, dtype=F32) / half)).reshape(1, half)
    tm = TM_PROJ
    return pl.pallas_call(
        _ret_inproj_kernel,
        grid=(n // tm,),
        in_specs=[
            pl.BlockSpec((tm, D_MODEL), lambda i: (i, 0)),
            pl.BlockSpec((1, tm), lambda i: (0, i)),
            pl.BlockSpec((1, D_MODEL), lambda i: (0, 0)),
            pl.BlockSpec((1, half), lambda i: (0, 0)),
            pl.BlockSpec(w_in.shape, lambda i: (0, 0), pipeline_mode=pl.Buffered(1)),
        ],
        out_specs=[
            pl.BlockSpec((tm, RET_QK), lambda i: (i, 0)),
            pl.BlockSpec((tm, RET_QK), lambda i: (i, 0)),
            pl.BlockSpec((tm, RET_V), lambda i: (i, 0)),
            pl.BlockSpec((tm, RET_V), lambda i: (i, 0)),
        ],
        out_shape=[
            jax.ShapeDtypeStruct((n, RET_QK), BF16),
            jax.ShapeDtypeStruct((n, RET_QK), BF16),
            jax.ShapeDtypeStruct((n, RET_V), BF16),
            jax.ShapeDtypeStruct((n, RET_V), BF16),
        ],
        compiler_params=_cparams(("arbitrary",)),
        name="ret_inproj",
    )(x, pos, g, inv, w_in)


def _ret_core_kernel(cdec_ref, q_ref, k_ref, v_ref, gate_ref, x_ref, hg_ref, intra_ref, cross_ref,
                     kdec_ref, wo_ref, out_ref, state_ref, y_ref, wos_ref):
    @pl.when(pl.program_id(0) == 0)
    def _():
        state_ref[...] = jnp.zeros_like(state_ref)
        hg = jnp.broadcast_to(hg_ref[...], (128, RET_V)).T[:, 0:1]
        wos_ref[...] = (wo_ref[...] * hg).astype(BF16)

    for r0 in range(0, RET_STEP, RET_C):
        rs = slice(r0, r0 + RET_C)
        for hd in range(RET_HEADS):
            q = q_ref[rs, hd * RET_DK:(hd + 1) * RET_DK]
            k = k_ref[rs, hd * RET_DK:(hd + 1) * RET_DK]
            v = v_ref[rs, hd * RET_DV:(hd + 1) * RET_DV]
            state = state_ref[hd]
            scores = lax.dot_general(q, k, (((1,), (1,)), ((), ())), preferred_element_type=F32)
            scores = (scores * intra_ref[hd]).astype(BF16)
            o = jnp.dot(scores, v, preferred_element_type=F32)
            cross = cross_ref[hd]
            o_cross = jnp.dot(q, state.astype(BF16), preferred_element_type=F32)
            o = o + o_cross * jnp.concatenate([cross] * (RET_DV // 128), axis=1)
            kdec = kdec_ref[hd]
            kd = (k.astype(F32) * jnp.concatenate([kdec] * (RET_DK // 128), axis=1)).astype(BF16)
            upd = lax.dot_general(kd, v, (((0,), (0,)), ((), ())), preferred_element_type=F32)
            state_ref[hd] = state * cdec_ref[hd] + upd
            ms = jnp.mean(o * o, axis=-1, keepdims=True)
            on = o * lax.rsqrt(ms + NORM_EPS)
            gt = gate_ref[rs, hd * RET_DV:(hd + 1) * RET_DV].astype(F32)
            y_ref[rs, hd * RET_DV:(hd + 1) * RET_DV] = (_silu(gt) * on).astype(BF16)
        out_ref[rs, :] = x_ref[rs, :] + jnp.dot(y_ref[rs, :], wos_ref[...], preferred_element_type=F32)


def _ret_core(q, k, v, gate, x, head_g_row, w_out):
    n = x.shape[0]
    c = RET_C
    log_gamma = jnp.log1p(-(2.0 ** (-5.0 - jnp.arange(RET_HEADS, dtype=F32))))
    idx = jnp.arange(c, dtype=F32)
    diff = idx[:, None] - idx[None, :]
    intra = jnp.where(diff >= 0, jnp.exp(log_gamma[:, None, None] * jnp.maximum(diff, 0.0)), 0.0)
    cross = jnp.broadcast_to(jnp.exp(log_gamma[:, None] * (idx + 1.0))[:, :, None], (RET_HEADS, c, 128))
    kdec = jnp.broadcast_to(jnp.exp(log_gamma[:, None] * (c - 1.0 - idx))[:, :, None], (RET_HEADS, c, 128))
    cdec = jnp.exp(log_gamma * c)
    return pl.pallas_call(
        _ret_core_kernel,
        grid=(n // RET_STEP,),
        in_specs=[
            pl.BlockSpec(memory_space=pltpu.SMEM),
            pl.BlockSpec((RET_STEP, RET_QK), lambda i: (i, 0)),
            pl.BlockSpec((RET_STEP, RET_QK), lambda i: (i, 0)),
            pl.BlockSpec((RET_STEP, RET_V), lambda i: (i, 0)),
            pl.BlockSpec((RET_STEP, RET_V), lambda i: (i, 0)),
            pl.BlockSpec((RET_STEP, D_MODEL), lambda i: (i, 0)),
            pl.BlockSpec((1, RET_V), lambda i: (0, 0)),
            pl.BlockSpec((RET_HEADS, c, c), lambda i: (0, 0, 0)),
            pl.BlockSpec((RET_HEADS, c, 128), lambda i: (0, 0, 0)),
            pl.BlockSpec((RET_HEADS, c, 128), lambda i: (0, 0, 0)),
            pl.BlockSpec((RET_V, D_MODEL), lambda i: (0, 0)),
        ],
        out_specs=pl.BlockSpec((RET_STEP, D_MODEL), lambda i: (i, 0)),
        out_shape=jax.ShapeDtypeStruct((n, D_MODEL), F32),
        scratch_shapes=[
            pltpu.VMEM((RET_HEADS, RET_DK, RET_DV), F32),
            pltpu.VMEM((RET_STEP, RET_V), BF16),
            pltpu.VMEM((RET_V, D_MODEL), BF16),
        ],
        compiler_params=_cparams(("arbitrary",)),
        name="ret_core",
    )(cdec, q, k, v, gate, x, head_g_row, intra, cross, kdec, w_out)


def _conv_pw1_kernel(x_ref, rw_ref, y1_ref, y2_ref, g_ref, w_ref, b_ref, xo_ref, u_ref):
    r = PW1_ROWS
    ngroups = TM_PW1 // r
    w1, w2 = _route_weight_columns(rw_ref[...])

    def prologue(rs):
        lo1, hi1 = _unpack_bf16_pairs(y1_ref[rs, :])
        lo2, hi2 = _unpack_bf16_pairs(y2_ref[rs, :])
        x = jnp.concatenate([x_ref[rs, :HALF] + w1[rs] * lo1 + w2[rs] * lo2,
                             x_ref[rs, HALF:] + w1[rs] * hi1 + w2[rs] * hi2], axis=1)
        xo_ref[rs, :] = x
        hf = _rms(x, g_ref[...])
        return hf.astype(BF16), hf[:, 0:128]

    groups = [slice(gi * r, (gi + 1) * r) for gi in range(ngroups)]
    hs = [prologue(rs) for rs in groups]
    for j in range(D_MODEL // 512):
        wa = w_ref[:, j * 512:(j + 1) * 512].astype(BF16)
        wg = w_ref[:, D_MODEL + j * 512:D_MODEL + (j + 1) * 512].astype(BF16)
        for gi, rs in enumerate(groups):
            h = hs[gi][0]
            a = jnp.dot(h, wa, preferred_element_type=F32) + b_ref[:, j * 512:(j + 1) * 512]
            gt = jnp.dot(h, wg, preferred_element_type=F32) + b_ref[:, D_MODEL + j * 512:D_MODEL + (j + 1) * 512]
            u = a * (1.0 / (1.0 + jnp.exp(-gt)))
            if j == 0 and gi + 1 < ngroups:
                zero = ((pltpu.bitcast(hs[gi + 1][1], U32) >> 16) >> 16).astype(F32)
                u_ref[rs, 0:128] = u[:, 0:128] + zero
                u_ref[rs, 128:512] = u[:, 128:512]
            else:
                u_ref[rs, j * 512:(j + 1) * 512] = u


def _conv_pw1(x, rw, yg, g, w, b):
    n = x.shape[0]
    tm = TM_PW1
    nb = n // tm
    return pl.pallas_call(
        _conv_pw1_kernel,
        grid=(nb,),
        in_specs=[
            pl.BlockSpec((tm, D_MODEL), lambda i: (i, 0)),
            pl.BlockSpec((8, tm), lambda i: (0, i)),
            pl.BlockSpec((tm, HALF), lambda i: (i, 0)),
            pl.BlockSpec((tm, HALF), lambda i: (i + nb, 0)),
            pl.BlockSpec((1, D_MODEL), lambda i: (0, 0)),
            pl.BlockSpec((D_MODEL, 2 * D_MODEL), lambda i: (0, 0), pipeline_mode=pl.Buffered(1)),
            pl.BlockSpec((1, 2 * D_MODEL), lambda i: (0, 0)),
        ],
        out_specs=[pl.BlockSpec((tm, D_MODEL), lambda i: (i, 0)), pl.BlockSpec((tm, D_MODEL), lambda i: (i, 0))],
        out_shape=[jax.ShapeDtypeStruct((n, D_MODEL), F32), jax.ShapeDtypeStruct((n, D_MODEL), F32)],
        compiler_params=_cparams(("arbitrary",)),
        name="conv_pw1",
    )(x, rw, yg, yg, g, w, b)


def _conv_core_kernel(u_ref, halo_ref, x_ref, wdw_ref, bdw_ref, lng_ref, lnb_ref, w2_ref, b2_ref,
                      out_ref, win_ref, z_ref):
    tm = TM_CONV
    first = pl.program_id(0) == 0
    halo = halo_ref[...]
    halo = jnp.where(first, jnp.zeros_like(halo), halo)
    nslab = D_MODEL // 128
    for cc in range(nslab):
        cs = slice(cc * 128, (cc + 1) * 128)
        win_ref[cc, 0:CONV_HALO, :] = halo[:, cs]
        win_ref[cc, CONV_HALO:CONV_HALO + tm, :] = u_ref[:, cs]
    off = CONV_HALO - (CONV_WIDTH - 1)
    rb = CONV_ROWS
    st = CONV_STRIDE
    for cc in range(nslab):
        cs = slice(cc * 128, (cc + 1) * 128)
        for r0 in range(0, tm, rb):
            accs = [bdw_ref[:, cs]] * st
            for o in range(CONV_WIDTH):
                w_o = wdw_ref[o:o + 1, cs]
                for rho in range(st):
                    accs[rho] = accs[rho] + win_ref[cc, pl.ds(r0 + rho + off + o, rb // st, stride=st), :] * w_o
            for rho in range(st):
                z_ref[cc, pl.ds(r0 + rho, rb // st, stride=st), :] = accs[rho]
    z = jnp.concatenate([z_ref[cc] for cc in range(nslab)], axis=1)
    mu = jnp.mean(z, axis=-1, keepdims=True)
    zc = z - mu
    var = jnp.mean(zc * zc, axis=-1, keepdims=True)
    zn = zc * lax.rsqrt(var + NORM_EPS) * lng_ref[...] + lnb_ref[...]
    y = _silu(zn).astype(BF16)
    out_ref[...] = x_ref[...] + jnp.dot(y, w2_ref[...].astype(BF16), preferred_element_type=F32) + b2_ref[...]


def _conv_core(u, x, w_dw, b_dw, ln_g, ln_b, w2, b2):
    n = x.shape[0]
    tm = TM_CONV
    r = tm // CONV_HALO
    wdw_pad = jnp.zeros((32, D_MODEL), F32).at[:CONV_WIDTH].set(w_dw)
    return pl.pallas_call(
        _conv_core_kernel,
        grid=(n // tm,),
        in_specs=[
            pl.BlockSpec((tm, D_MODEL), lambda i: (i, 0)),
            pl.BlockSpec((CONV_HALO, D_MODEL), lambda i: (jnp.maximum(i * r - 1, 0), 0)),
            pl.BlockSpec((tm, D_MODEL), lambda i: (i, 0)),
            pl.BlockSpec((32, D_MODEL), lambda i: (0, 0)),
            pl.BlockSpec((1, D_MODEL), lambda i: (0, 0)),
            pl.BlockSpec((1, D_MODEL), lambda i: (0, 0)),
            pl.BlockSpec((1, D_MODEL), lambda i: (0, 0)),
            pl.BlockSpec((D_MODEL, D_MODEL), lambda i: (0, 0)),
            pl.BlockSpec((1, D_MODEL), lambda i: (0, 0)),
        ],
        out_specs=pl.BlockSpec((tm, D_MODEL), lambda i: (i, 0)),
        out_shape=jax.ShapeDtypeStruct((n, D_MODEL), F32),
        scratch_shapes=[pltpu.VMEM((D_MODEL // 128, CONV_HALO + tm, 128), F32),
                        pltpu.VMEM((D_MODEL // 128, tm, 128), F32)],
        compiler_params=_cparams(("arbitrary",)),
        name="conv_core",
    )(u, u, x, wdw_pad, b_dw, ln_g, ln_b, w2, b2)


def _router_kernel(x_ref, g_ref, wr_ref, br_ref, upper_ref, hp_ref, ri_ref, rw_ref, cnt_ref, carry_ref):
    t = T_ROUTE

    @pl.when(pl.program_id(0) == 0)
    def _():
        carry_ref[...] = jnp.zeros_like(carry_ref)

    h = _rms(x_ref[...], g_ref[...])
    hp_ref[...] = _pack_bf16_pairs(h)
    h_hi = h.astype(BF16)
    h_lo = (h - h_hi.astype(F32)).astype(BF16)
    w = wr_ref[...]
    w_hi = w.astype(BF16)
    w_lo = (w - w_hi.astype(F32)).astype(BF16)
    dn = (((1,), (1,)), ((), ()))
    p = lax.dot_general(jnp.concatenate([w_hi, w_lo], axis=0), h_hi, dn, preferred_element_type=F32)
    nr = wr_ref.shape[0]
    logits = p[0:nr] + p[nr:2 * nr] + lax.dot_general(w_hi, h_lo, dn, preferred_element_type=F32)
    logits = logits + br_ref[:, 0:1]

    best = logits[0:1]
    gi = jnp.zeros((1, t), I32)
    for j in range(1, MOE_GROUPS):
        r = logits[j:j + 1]
        up = r > best
        gi = jnp.where(up, j, gi)
        best = jnp.where(up, r, best)
    den = jnp.zeros((1, t), F32)
    for j in range(MOE_GROUPS):
        den = den + jnp.exp(logits[j:j + 1] - best)
    gate_g = 1.0 / den

    sel = logits[8:8 + MOE_EPG]
    for j in range(1, MOE_GROUPS):
        sel = jnp.where(gi == j, logits[8 + j * MOE_EPG:8 + (j + 1) * MOE_EPG], sel)

    m1 = sel[0:1]
    i1 = jnp.zeros((1, t), I32)
    for j in range(1, MOE_EPG):
        r = sel[j:j + 1]
        up = r > m1
        i1 = jnp.where(up, j, i1)
        m1 = jnp.where(up, r, m1)
    m2 = jnp.full((1, t), -jnp.inf, F32)
    i2 = jnp.zeros((1, t), I32)
    started = jnp.zeros((1, t), jnp.bool_)
    for j in range(MOE_EPG):
        r = sel[j:j + 1]
        ok = i1 != j
        up = ok & ((r > m2) | jnp.logical_not(started))
        i2 = jnp.where(up, j, i2)
        m2 = jnp.where(up, r, m2)
        started = started | ok
    e21 = jnp.exp(m2 - m1)
    p1 = 1.0 / (1.0 + e21)
    w1 = gate_g * p1
    w2 = gate_g * (e21 * p1)
    eid1 = gi * MOE_EPG + i1
    eid2 = gi * MOE_EPG + i2

    eio = lax.broadcasted_iota(I32, (MOE_EXPERTS, t), 0)
    oh1 = eio == eid1
    oh2 = eio == eid2
    oh = (oh1 | oh2).astype(F32)
    cum = jnp.dot(oh.astype(BF16), upper_ref[...], preferred_element_type=F32) + carry_ref[:, 0:1]
    rank1 = jnp.sum(jnp.where(oh1, cum, 0.0), axis=0, keepdims=True)
    rank2 = jnp.sum(jnp.where(oh2, cum, 0.0), axis=0, keepdims=True)
    carry_ref[...] = carry_ref[...] + jnp.sum(oh, axis=1, keepdims=True)
    cnt_ref[...] = carry_ref[...]

    zi = jnp.zeros((4, t), I32)
    ri_ref[...] = jnp.concatenate([eid1, eid2, rank1.astype(I32), rank2.astype(I32), zi], axis=0)
    zf = jnp.zeros((6, t), F32)
    rw_ref[...] = jnp.concatenate([w1, w2, zf], axis=0)


def _router(x, g, w_rg, b_rg, w_re, b_re):
    n = x.shape[0]
    t = T_ROUTE
    wr = jnp.zeros((40, D_MODEL), F32).at[0:MOE_GROUPS].set(w_rg.T).at[8:40].set(w_re.T)
    br = jnp.zeros((40,), F32).at[0:MOE_GROUPS].set(b_rg).at[8:40].set(b_re)
    br = jnp.broadcast_to(br[:, None], (40, 128))
    upper = jnp.asarray(np.triu(np.ones((t, t), np.float32), 1).astype(BF16))
    return pl.pallas_call(
        _router_kernel,
        grid=(n // t,),
        in_specs=[
            pl.BlockSpec((t, D_MODEL), lambda i: (i, 0)),
            pl.BlockSpec((1, D_MODEL), lambda i: (0, 0)),
            pl.BlockSpec((40, D_MODEL), lambda i: (0, 0)),
            pl.BlockSpec((40, 128), lambda i: (0, 0)),
            pl.BlockSpec((t, t), lambda i: (0, 0)),
        ],
        out_specs=[
            pl.BlockSpec((t, HALF), lambda i: (i, 0)),
            pl.BlockSpec((8, t), lambda i: (0, i)),
            pl.BlockSpec((8, t), lambda i: (0, i)),
            pl.BlockSpec((MOE_EXPERTS, 128), lambda i: (0, 0)),
        ],
        out_shape=[
            jax.ShapeDtypeStruct((n, HALF), U32),
            jax.ShapeDtypeStruct((8, n), I32),
            jax.ShapeDtypeStruct((8, n), F32),
            jax.ShapeDtypeStruct((MOE_EXPERTS, 128), F32),
        ],
        scratch_shapes=[pltpu.VMEM((MOE_EXPERTS, 128), F32)],
        compiler_params=_cparams(("arbitrary",)),
        name="moe_router",
    )(x, g, wr, br, upper)


def _sc_mesh():
    return plsc.VectorSubcoreMesh(core_axis_name="c", subcore_axis_name="s",
                                  num_cores=SC_CORES, num_subcores=SC_SUBCORES)


def _sc_worker_id():
    return lax.axis_index("s") * SC_CORES + lax.axis_index("c")


def _sc_dispatch(hp, idx, zero_rows, total_rows):
    n = hp.shape[0]
    tpw = n // SC_WORKERS
    kd = tpw // SC_ROWS
    kp = idx.shape[1] - 2 * kd
    nb = SC_NBUF

    @functools.partial(
        pl.kernel, mesh=_sc_mesh(),
        out_type=jax.ShapeDtypeStruct((total_rows, HALF), U32),
        scratch_types=[pltpu.VMEM((2 * kd + kp, SC_ROWS), I32)]
        + [pltpu.VMEM((SC_ROWS, HALF), U32)] * (nb + 1)
        + [pltpu.SemaphoreType.DMA((nb,)), pltpu.SemaphoreType.DMA((nb,)), pltpu.SemaphoreType.DMA],
        name="moe_dispatch_sc",
    )
    def k(hp_hbm, idx_hbm, zero_hbm, xs_hbm, idx_v, *rest):
        bufs, zbuf = rest[:nb], rest[nb]
        load_sem, scat_sem, pad_sem = rest[nb + 1:]
        wid = _sc_worker_id()
        pltpu.sync_copy(idx_hbm.at[wid], idx_v)

        def load(c):
            return pltpu.make_async_copy(hp_hbm.at[pl.ds(wid * tpw + c * SC_ROWS, SC_ROWS)], bufs[c % nb],
                                         load_sem.at[c % nb])

        def scatters(c):
            return (pltpu.make_async_copy(bufs[c % nb], xs_hbm.at[idx_v.at[c]], scat_sem.at[c % nb]),
                    pltpu.make_async_copy(bufs[c % nb], xs_hbm.at[idx_v.at[kd + c]], scat_sem.at[c % nb]))

        for c in range(min(nb - 1, kd)):
            load(c).start()
        pltpu.sync_copy(zero_hbm, zbuf)
        pads = [pltpu.make_async_copy(zbuf, xs_hbm.at[idx_v.at[2 * kd + j]], pad_sem) for j in range(kp)]
        for p in pads:
            p.start()
        for c in range(kd):
            load(c).wait()
            for d in scatters(c):
                d.start()
            if c + nb - 1 < kd:
                if c >= 1:
                    for d in scatters(c - 1):
                        d.wait()
                load(c + nb - 1).start()
        for c in range(max(kd - nb, 0), kd):
            for d in scatters(c):
                d.wait()
        for p in pads:
            p.wait()

    return k(hp, idx, zero_rows)


def _sc_gather(ys, idx):
    kg = idx.shape[1]
    rows_per_worker = kg * SC_ROWS
    nb = SC_NBUF

    @functools.partial(
        pl.kernel, mesh=_sc_mesh(),
        out_type=jax.ShapeDtypeStruct((SC_WORKERS * rows_per_worker, HALF), U32),
        scratch_types=[pltpu.VMEM((kg, SC_ROWS), I32)] + [pltpu.VMEM((SC_ROWS, HALF), U32)] * nb
        + [pltpu.SemaphoreType.DMA((nb,)), pltpu.SemaphoreType.DMA((nb,))],
        name="moe_gather_sc",
    )
    def k(ys_hbm, idx_hbm, yg_hbm, idx_v, *rest):
        bufs = rest[:nb]
        gat_sem, out_sem = rest[nb:]
        wid = _sc_worker_id()
        pltpu.sync_copy(idx_hbm.at[wid], idx_v)

        def gather(c):
            return pltpu.make_async_copy(ys_hbm.at[idx_v.at[c]], bufs[c % nb], gat_sem.at[c % nb])

        def store(c):
            return pltpu.make_async_copy(bufs[c % nb],
                                         yg_hbm.at[pl.ds(wid * rows_per_worker + c * SC_ROWS, SC_ROWS)],
                                         out_sem.at[c % nb])

        for c in range(min(nb - 1, kg)):
            gather(c).start()
        for c in range(kg):
            gather(c).wait()
            store(c).start()
            if c + nb - 1 < kg:
                if c >= 1:
                    store(c - 1).wait()
                gather(c + nb - 1).start()
        for c in range(max(kg - nb, 0), kg):
            store(c).wait()

    return k(ys, idx)


def _expert_kernel(blk_e_ref, nused_ref, first_ref, slot_ref, nxt_ref, xs_ref, wg_hbm, wu_hbm, wd_hbm,
                   ys_ref, wg_buf, wu_buf, wd_buf, sems, *, layer):
    step = pl.program_id(0)
    tm = TM_EXP
    nused = nused_ref[0]

    def weight_copies(e, s):
        return (pltpu.make_async_copy(wg_hbm.at[layer, e], wg_buf.at[s], sems.at[s, 0]),
                pltpu.make_async_copy(wu_hbm.at[layer, e], wu_buf.at[s], sems.at[s, 1]),
                pltpu.make_async_copy(wd_hbm.at[layer, e], wd_buf.at[s], sems.at[s, 2]))

    def dma_control(j):
        i = step * EXP_SUB + j

        @pl.when(i < nused)
        def _():
            s = slot_ref[i]

            if j == 0:
                @pl.when(i == 0)
                def _():
                    for c in weight_copies(blk_e_ref[0], 0):
                        c.start()

            @pl.when(first_ref[i] == 1)
            def _():
                for c in weight_copies(blk_e_ref[i], s):
                    c.wait()

                @pl.when(nxt_ref[i] >= 0)
                def _():
                    for c in weight_copies(nxt_ref[i], lax.rem(s + 1, EXP_SLOTS)):
                        c.start()

    for j in range(EXP_SUB):
        dma_control(j)

    @pl.when(step * EXP_SUB < nused)
    def _():
        for j in range(EXP_SUB):
            i = step * EXP_SUB + j
            rows = slice(j * tm, (j + 1) * tm)
            s = slot_ref[i]
            lo, hi = _unpack_bf16_pairs(xs_ref[rows, :])
            xf = jnp.concatenate([lo, hi], axis=1)
            a = jnp.dot(xf, wg_buf[s], preferred_element_type=F32)
            b = jnp.dot(xf, wu_buf[s], preferred_element_type=F32)
            hm = _silu(a) * b
            y = jnp.dot(hm, wd_buf[s], preferred_element_type=F32)
            ys_ref[rows, :] = jnp.where(i < nused, _pack_bf16_pairs(y), jnp.uint32(0))

    @pl.when(step * EXP_SUB >= nused)
    def _():
        ys_ref[...] = jnp.zeros_like(ys_ref)


def _experts(xs, blk_e, nused, w_gate, w_up, w_down, layer):
    tm = TM_EXP
    p_rows = xs.shape[0] - MOE_EXPERTS * tm
    nblk = p_rows // tm
    pos = jnp.arange(nblk, dtype=I32)
    valid = pos < nused[0]
    prev_e = jnp.concatenate([jnp.full((1,), -1, I32), blk_e[:-1]])
    first = valid & (blk_e != prev_e)
    slot = jnp.maximum(jnp.cumsum(first.astype(I32)) - 1, 0) % EXP_SLOTS
    first_pos = jnp.where(first, pos, nblk)
    next_first = jnp.concatenate([lax.cummin(first_pos, reverse=True)[1:], jnp.full((1,), nblk, I32)])
    nxt = jnp.where(next_first < nblk, blk_e[jnp.minimum(next_first, nblk - 1)], -1)

    def blk(i, be, nu, *_):
        return jnp.minimum(i, (nu[0] - 1) // EXP_SUB)

    grid_spec = pltpu.PrefetchScalarGridSpec(
        num_scalar_prefetch=5,
        grid=(nblk // EXP_SUB,),
        in_specs=[
            pl.BlockSpec((EXP_SUB * tm, HALF), lambda i, *sp: (blk(i, *sp), 0)),
            pl.BlockSpec(memory_space=pl.ANY),
            pl.BlockSpec(memory_space=pl.ANY),
            pl.BlockSpec(memory_space=pl.ANY),
        ],
        out_specs=pl.BlockSpec((EXP_SUB * tm, HALF), lambda i, *sp: (i, 0)),
        scratch_shapes=[
            pltpu.VMEM((EXP_SLOTS, D_MODEL, MOE_FF), F32),
            pltpu.VMEM((EXP_SLOTS, D_MODEL, MOE_FF), F32),
            pltpu.VMEM((EXP_SLOTS, MOE_FF, D_MODEL), F32),
            pltpu.SemaphoreType.DMA((EXP_SLOTS, 3)),
        ],
    )
    return pl.pallas_call(
        functools.partial(_expert_kernel, layer=layer),
        grid_spec=grid_spec,
        out_shape=jax.ShapeDtypeStruct((p_rows, HALF), U32),
        compiler_params=_cparams(("arbitrary",)),
        name="moe_experts",
    )(blk_e, nused, first.astype(I32), slot.astype(I32), nxt.astype(I32), xs, w_gate, w_up, w_down)


def _combine_kernel(x_ref, rw_ref, fg_ref, y1_ref, y2_ref, out_ref):
    lo1, hi1 = _unpack_bf16_pairs(y1_ref[...])
    lo2, hi2 = _unpack_bf16_pairs(y2_ref[...])
    w1, w2 = _route_weight_columns(rw_ref[...])
    x = x_ref[...]
    o_lo = x[:, :HALF] + w1 * lo1 + w2 * lo2
    o_hi = x[:, HALF:] + w1 * hi1 + w2 * hi2
    ms = (jnp.sum(o_lo * o_lo, axis=-1, keepdims=True)
          + jnp.sum(o_hi * o_hi, axis=-1, keepdims=True)) * (1.0 / D_MODEL)
    sc = lax.rsqrt(ms + NORM_EPS)
    o_lo = o_lo * sc * fg_ref[:, :HALF]
    o_hi = o_hi * sc * fg_ref[:, HALF:]
    out_ref[:, :HALF] = o_lo
    out_ref[:, HALF:] = o_hi


def _combine(yg, x, rw, final_g):
    n = x.shape[0]
    td = T_COMB
    nb = n // td
    return pl.pallas_call(
        _combine_kernel,
        grid=(nb,),
        in_specs=[
            pl.BlockSpec((td, D_MODEL), lambda i: (i, 0)),
            pl.BlockSpec((8, td), lambda i: (0, i)),
            pl.BlockSpec((1, D_MODEL), lambda i: (0, 0)),
            pl.BlockSpec((td, HALF), lambda i: (i, 0)),
            pl.BlockSpec((td, HALF), lambda i: (i + nb, 0)),
        ],
        out_specs=pl.BlockSpec((td, D_MODEL), lambda i: (i, 0)),
        out_shape=jax.ShapeDtypeStruct((n, D_MODEL), F32),
        compiler_params=_cparams(("arbitrary",)),
        name="moe_combine",
    )(x, rw, final_g, yg, yg)


def _moe(x, g, w_rg, b_rg, w_re, b_re, w_gate, w_up, w_down, layer):
    n = x.shape[0]
    tm = TM_EXP
    p_rows = 2 * n + MOE_EXPERTS * tm
    nblk = p_rows // tm
    hp, ri, rw, cnt = _router(x, g, w_rg, b_rg, w_re, b_re)
    counts = cnt[:, 0].astype(I32)
    pcounts = (counts + tm - 1) // tm * tm
    pend = jnp.cumsum(pcounts)
    pstart = pend - pcounts
    eio = jnp.arange(MOE_EXPERTS, dtype=I32)[:, None]
    dest1 = jnp.sum(jnp.where(ri[0][None, :] == eio, pstart[:, None], 0), axis=0) + ri[2]
    dest2 = jnp.sum(jnp.where(ri[1][None, :] == eio, pstart[:, None], 0), axis=0) + ri[3]
    blk_start = jnp.arange(nblk, dtype=I32) * tm
    blk_e = jnp.minimum(jnp.sum((pend[None, :] <= blk_start[:, None]).astype(I32), axis=1), MOE_EXPERTS - 1)
    nused = jnp.maximum(pend[-1] // tm, 1).astype(I32).reshape(1)
    r = jnp.arange(tm, dtype=I32)[None, :]
    pad_slot = jnp.where(r < (pcounts - counts)[:, None], (pstart + counts)[:, None] + r, p_rows + eio * tm + r)
    kd = n // SC_WORKERS // SC_ROWS
    idx = jnp.concatenate([dest1.reshape(SC_WORKERS, kd, SC_ROWS), dest2.reshape(SC_WORKERS, kd, SC_ROWS),
                           pad_slot.reshape(SC_WORKERS, -1, SC_ROWS)], axis=1)
    zero_rows = jnp.zeros((SC_ROWS, HALF), U32)
    xs = _sc_dispatch(hp, idx, zero_rows, p_rows + MOE_EXPERTS * tm)
    ys = _experts(xs, blk_e, nused, w_gate, w_up, w_down, layer)
    gidx = jnp.concatenate([dest1, dest2]).reshape(SC_WORKERS, -1, SC_ROWS)
    yg = _sc_gather(ys, gidx)
    return yg, rw


def kernel(x, positions, norm_mix_g, norm_ffn_g, ret_w_in, ret_head_g, ret_w_out, conv_w_pw1, conv_b_pw1, conv_w_dw, conv_b_dw, conv_ln_g, conv_ln_b, conv_w_pw2, conv_b_pw2, moe_w_rg, moe_b_rg, moe_w_re, moe_b_re, moe_w_gate, moe_w_up, moe_w_down, final_norm_g):
    b, s, d = x.shape
    n = b * s
    xt = x.reshape(n, d)
    pos = positions.reshape(1, n)
    fg = final_norm_g.reshape(1, d)

    q, k, v, gate = _ret_inproj(xt, pos, norm_mix_g[0].reshape(1, d), ret_w_in[0])
    xt = _ret_core(q, k, v, gate, xt, ret_head_g[0].reshape(1, RET_V), ret_w_out[0])
    yg, rw = _moe(xt, norm_ffn_g[0].reshape(1, d), moe_w_rg[0], moe_b_rg[0], moe_w_re[0], moe_b_re[0],
                  moe_w_gate, moe_w_up, moe_w_down, 0)

    xt, u = _conv_pw1(xt, rw, yg, norm_mix_g[1].reshape(1, d), conv_w_pw1[0],
                      conv_b_pw1[0].reshape(1, 2 * d))
    xt = _conv_core(u, xt, conv_w_dw[0], conv_b_dw[0].reshape(1, d), conv_ln_g[0].reshape(1, d),
                    conv_ln_b[0].reshape(1, d), conv_w_pw2[0], conv_b_pw2[0].reshape(1, d))
    yg, rw = _moe(xt, norm_ffn_g[1].reshape(1, d), moe_w_rg[1], moe_b_rg[1], moe_w_re[1], moe_b_re[1],
                  moe_w_gate, moe_w_up, moe_w_down, 1)
    xt = _combine(yg, xt, rw, fg)
    return xt.reshape(b, s, d)
```
